```python
import jax, jax.numpy as jnp
from jax import lax
import numpy as np

D_MODEL = 4096
BATCH = 16
SEQ = 2048
DEPTH = 1

D_MIX = D_MODEL
HG_WIDTH = D_MIX // 2
HG_HEAD_DIM = 128
HG_HEADS = HG_WIDTH // HG_HEAD_DIM
HG_CHUNK = 64
LRU_WIDTH = D_MIX - HG_WIDTH
LRU_BLOCKS = 16
LRU_BLOCK_DIM = LRU_WIDTH // LRU_BLOCKS
LRU_CONV = 4
LRU_C = 8.0
D_FF = 256 * ((8 * D_MODEL // 3 + 255) // 256)
FFN_CONV = 3
EPS = 1e-6
IN_WIDTHS = (HG_WIDTH, HG_WIDTH, HG_WIDTH, HG_WIDTH, LRU_WIDTH, LRU_WIDTH)
IN_TOTAL = sum(IN_WIDTHS)
IN_SPLIT = tuple(int(v) for v in np.cumsum(IN_WIDTHS)[:-1])

kernel_name = 'hymba_hgrn2_rglru_convffn_block'


def rmsnorm(x, w):
    x32 = x.astype(jnp.float32)
    y = x32 * lax.rsqrt(jnp.mean(x32 * x32, axis=-1, keepdims=True) + EPS)
    return (y * w.astype(jnp.float32)).astype(x.dtype)


def causal_dwconv(x, w, b):
    width = w.shape[0]
    seq = x.shape[1]
    xp = jnp.pad(x, ((0, 0), (width - 1, 0), (0, 0)))
    y = b + xp[:, 0:seq, :] * w[0]
    for j in range(1, width):
        y = y + xp[:, j:j + seq, :] * w[j]
    return y


def hgrn2_chunk_scan(q, k, v, g):
    bsz, seq, nh, dk = q.shape
    dv = v.shape[-1]
    n = seq // HG_CHUNK

    def to_chunks(t):
        return t.reshape(bsz, n, HG_CHUNK, nh, t.shape[-1]).transpose(1, 0, 3, 2, 4)

    causal = jnp.tril(jnp.ones((HG_CHUNK, HG_CHUNK), dtype=bool))[:, :, None]

    def step(state, inp):
        qc, kc, vc, gc = inp
        b = jnp.cumsum(gc, axis=2)
        o_inter = jnp.einsum('bhtk,bhkv->bhtv', qc * jnp.exp(b), state)
        diff = b[:, :, :, None, :] - b[:, :, None, :, :]
        decay = jnp.exp(jnp.where(causal, diff, -jnp.inf))
        scores = jnp.einsum('bhtk,bhtsk,bhsk->bhts', qc, decay, kc)
        o = o_inter + jnp.einsum('bhts,bhsv->bhtv', scores, vc)
        b_last = b[:, :, -1:, :]
        state = (jnp.exp(b_last[:, :, 0, :])[..., None] * state
                 + jnp.einsum('bhsk,bhsv->bhkv', kc * jnp.exp(b_last - b), vc))
        return state, o

    init = jnp.zeros((bsz, nh, dk, dv), jnp.float32)
    _, o = lax.scan(step, init, (to_chunks(q), to_chunks(k), to_chunks(v), to_chunks(g)))
    return o.transpose(1, 0, 3, 2, 4).reshape(bsz, seq, nh, dv)


def hgrn2_group(q_raw, f_raw, i_raw, g_raw, lb, norm_w):
    bsz, seq, _ = q_raw.shape
    q = jax.nn.silu(q_raw.astype(jnp.float32))
    f = lb + (1.0 - lb) * jax.nn.sigmoid(f_raw.astype(jnp.float32))
    k = 1.0 - f
    logf = jnp.log(f)
    v = i_raw.astype(jnp.float32)
    shp = (bsz, seq, HG_HEADS, HG_HEAD_DIM)
    o = hgrn2_chunk_scan(q.reshape(shp), k.reshape(shp), v.reshape(shp), logf.reshape(shp))
    o = rmsnorm(o, norm_w.reshape(HG_HEADS, HG_HEAD_DIM))
    return o.reshape(bsz, seq, HG_WIDTH) * jax.nn.silu(g_raw.astype(jnp.float32))


def _lin_combine(c1, c2):
    a1, b1 = c1
    a2, b2 = c2
    return a1 * a2, a2 * b1 + b2


def rglru_group(x_raw, y_raw, conv_w, conv_b, wa, ba, wx, bx, lam):
    bsz, seq, _ = x_raw.shape
    xb = causal_dwconv(x_raw, conv_w, conv_b).astype(jnp.float32)
    xblk = xb.reshape(bsz, seq, LRU_BLOCKS, LRU_BLOCK_DIM)
    r = jax.nn.sigmoid(jnp.einsum('bsnd,nde->bsne', xblk, wa).reshape(bsz, seq, LRU_WIDTH) + ba)
    i = jax.nn.sigmoid(jnp.einsum('bsnd,nde->bsne', xblk, wx).reshape(bsz, seq, LRU_WIDTH) + bx)
    log_a = -LRU_C * r * jax.nn.softplus(-lam.astype(jnp.float32))
    a = jnp.exp(log_a)
    mult = jnp.sqrt(-jnp.expm1(2.0 * log_a))
    mult = mult.at[:, 0].set(1.0)
    u = xb * i * mult
    _, h = lax.associative_scan(_lin_combine, (a, u), axis=1)
    return h * jax.nn.gelu(y_raw.astype(jnp.float32))


def _fwd_setup_inputs(seed: int = 0) -> dict:
    key = jax.random.key(seed)
    ks = jax.random.split(key, 20)
    f32 = jnp.float32
    nrm = lambda k, shp, s: (jax.random.normal(k, shp, f32) * s)
    x = jax.random.normal(ks[0], (BATCH, SEQ, D_MODEL), f32)
    ln1_w = 1.0 + nrm(ks[1], (DEPTH, D_MODEL), 0.02)
    w_in = nrm(ks[2], (DEPTH, D_MODEL, IN_TOTAL), D_MODEL ** -0.5)
    lb_gamma = nrm(ks[3], (DEPTH + 1, HG_WIDTH), 0.5)
    hg_norm_w = 1.0 + nrm(ks[4], (DEPTH, HG_WIDTH), 0.02)
    lru_conv_w = nrm(ks[5], (DEPTH, LRU_CONV, LRU_WIDTH), LRU_CONV ** -0.5)
    lru_conv_b = nrm(ks[6], (DEPTH, LRU_WIDTH), 0.02)
    lru_wa = nrm(ks[7], (DEPTH, LRU_BLOCKS, LRU_BLOCK_DIM, LRU_BLOCK_DIM), LRU_BLOCK_DIM ** -0.5)
    lru_ba = nrm(ks[8], (DEPTH, LRU_WIDTH), 0.1)
    lru_wx = nrm(ks[9], (DEPTH, LRU_BLOCKS, LRU_BLOCK_DIM, LRU_BLOCK_DIM), LRU_BLOCK_DIM ** -0.5)
    lru_bx = nrm(ks[10], (DEPTH, LRU_WIDTH), 0.1)
    a_c = jax.random.uniform(ks[11], (DEPTH, LRU_WIDTH), f32, 0.9, 0.999)
    a0 = a_c ** (1.0 / LRU_C)
    lru_lambda = jnp.log(a0) - jnp.log1p(-a0)
    lru_norm_w = 1.0 + nrm(ks[12], (DEPTH, LRU_WIDTH), 0.02)
    w_out = nrm(ks[13], (DEPTH, D_MIX, D_MODEL), D_MIX ** -0.5)
    ln2_w = 1.0 + nrm(ks[14], (DEPTH, D_MODEL), 0.02)
    ffn_w_up = nrm(ks[15], (DEPTH, D_MODEL, 2 * D_FF), D_MODEL ** -0.5)
    ffn_conv_w = nrm(ks[16], (DEPTH, FFN_CONV, 2 * D_FF), FFN_CONV ** -0.5)
    ffn_conv_b = nrm(ks[17], (DEPTH, 2 * D_FF), 0.02)
    ffn_w_down = nrm(ks[18], (DEPTH, D_FF, D_MODEL), D_FF ** -0.5)
    final_norm_w = 1.0 + nrm(ks[19], (D_MODEL,), 0.02)
    return {'x': x, 'ln1_w': ln1_w, 'w_in': w_in, 'lb_gamma': lb_gamma,
            'hg_norm_w': hg_norm_w, 'lru_conv_w': lru_conv_w, 'lru_conv_b': lru_conv_b,
            'lru_wa': lru_wa, 'lru_ba': lru_ba, 'lru_wx': lru_wx, 'lru_bx': lru_bx,
            'lru_lambda': lru_lambda, 'lru_norm_w': lru_norm_w, 'w_out': w_out,
            'ln2_w': ln2_w, 'ffn_w_up': ffn_w_up, 'ffn_conv_w': ffn_conv_w,
            'ffn_conv_b': ffn_conv_b, 'ffn_w_down': ffn_w_down, 'final_norm_w': final_norm_w}


def _fwd_reference(x, ln1_w, w_in, lb_gamma, hg_norm_w, lru_conv_w, lru_conv_b, lru_wa, lru_ba,
              lru_wx, lru_bx, lru_lambda, lru_norm_w, w_out, ln2_w, ffn_w_up, ffn_conv_w,
              ffn_conv_b, ffn_w_down, final_norm_w):
    lb_all = jnp.cumsum(jax.nn.softmax(lb_gamma.astype(jnp.float32), axis=0), axis=0)
    h = x
    for l in range(DEPTH):
        hn = rmsnorm(h, ln1_w[l])
        proj = jnp.einsum('bsd,de->bse', hn, w_in[l])
        q_r, f_r, i_r, g_r, x_r, y_r = jnp.split(proj, IN_SPLIT, axis=-1)
        o_hg = hgrn2_group(q_r, f_r, i_r, g_r, lb_all[l], hg_norm_w[l])
        o_lru = rglru_group(x_r, y_r, lru_conv_w[l], lru_conv_b[l], lru_wa[l], lru_ba[l],
                            lru_wx[l], lru_bx[l], lru_lambda[l])
        o_lru = rmsnorm(o_lru, lru_norm_w[l])
        mix = jnp.concatenate([o_hg, o_lru], axis=-1).astype(h.dtype)
        h = h + jnp.einsum('bse,ed->bsd', mix, w_out[l])
        hn = rmsnorm(h, ln2_w[l])
        up = jnp.einsum('bsd,df->bsf', hn, ffn_w_up[l])
        up = causal_dwconv(up, ffn_conv_w[l], ffn_conv_b[l])
        gate, val = jnp.split(up, [D_FF], axis=-1)
        h = h + jnp.einsum('bsf,fd->bsd', jax.nn.silu(gate) * val, ffn_w_down[l])
    return rmsnorm(h, final_norm_w)


import jax as _jax
import jax.numpy as _jnp

TWIN_FORMAT = 'train_step'
FWD_PARAMS = ['x', 'ln1_w', 'w_in', 'lb_gamma', 'hg_norm_w', 'lru_conv_w', 'lru_conv_b', 'lru_wa', 'lru_ba', 'lru_wx', 'lru_bx', 'lru_lambda', 'lru_norm_w', 'w_out', 'ln2_w', 'ffn_w_up', 'ffn_conv_w', 'ffn_conv_b', 'ffn_w_down', 'final_norm_w']
TWIN_WEIGHTS = ['ln1_w', 'w_in', 'lb_gamma', 'hg_norm_w', 'lru_conv_w', 'lru_conv_b', 'lru_wa', 'lru_ba', 'lru_wx', 'lru_bx', 'lru_lambda', 'lru_norm_w', 'w_out', 'ln2_w', 'ffn_w_up', 'ffn_conv_w', 'ffn_conv_b', 'ffn_w_down', 'final_norm_w']
TWIN_DIFF_INPUT = 'x'
TWIN_INPUTS = ['x', 'ln1_w', 'w_in', 'lb_gamma', 'hg_norm_w', 'lru_conv_w', 'lru_conv_b', 'lru_wa', 'lru_ba', 'lru_wx', 'lru_bx', 'lru_lambda', 'lru_norm_w', 'w_out', 'ln2_w', 'ffn_w_up', 'ffn_conv_w', 'ffn_conv_b', 'ffn_w_down', 'final_norm_w', 'loss_target', 'm_ln1_w', 'm_w_in', 'm_lb_gamma', 'm_hg_norm_w', 'm_lru_conv_w', 'm_lru_conv_b', 'm_lru_wa', 'm_lru_ba', 'm_lru_wx', 'm_lru_bx', 'm_lru_lambda', 'm_lru_norm_w', 'm_w_out', 'm_ln2_w', 'm_ffn_w_up', 'm_ffn_conv_w', 'm_ffn_conv_b', 'm_ffn_w_down', 'm_final_norm_w', 'v_ln1_w', 'v_w_in', 'v_lb_gamma', 'v_hg_norm_w', 'v_lru_conv_w', 'v_lru_conv_b', 'v_lru_wa', 'v_lru_ba', 'v_lru_wx', 'v_lru_bx', 'v_lru_lambda', 'v_lru_norm_w', 'v_w_out', 'v_ln2_w', 'v_ffn_w_up', 'v_ffn_conv_w', 'v_ffn_conv_b', 'v_ffn_w_down', 'v_final_norm_w']
TWIN_OUTPUTS = ['loss', 'grad_x', 'grad_ln1_w', 'grad_w_in', 'grad_lb_gamma', 'grad_hg_norm_w', 'grad_lru_conv_w', 'grad_lru_conv_b', 'grad_lru_wa', 'grad_lru_ba', 'grad_lru_wx', 'grad_lru_bx', 'grad_lru_lambda', 'grad_lru_norm_w', 'grad_w_out', 'grad_ln2_w', 'grad_ffn_w_up', 'grad_ffn_conv_w', 'grad_ffn_conv_b', 'grad_ffn_w_down', 'grad_final_norm_w', 'delta_ln1_w', 'delta_w_in', 'delta_lb_gamma', 'delta_hg_norm_w', 'delta_lru_conv_w', 'delta_lru_conv_b', 'delta_lru_wa', 'delta_lru_ba', 'delta_lru_wx', 'delta_lru_bx', 'delta_lru_lambda', 'delta_lru_norm_w', 'delta_w_out', 'delta_ln2_w', 'delta_ffn_w_up', 'delta_ffn_conv_w', 'delta_ffn_conv_b', 'delta_ffn_w_down', 'delta_final_norm_w', 'new_m_ln1_w', 'new_m_w_in', 'new_m_lb_gamma', 'new_m_hg_norm_w', 'new_m_lru_conv_w', 'new_m_lru_conv_b', 'new_m_lru_wa', 'new_m_lru_ba', 'new_m_lru_wx', 'new_m_lru_bx', 'new_m_lru_lambda', 'new_m_lru_norm_w', 'new_m_w_out', 'new_m_ln2_w', 'new_m_ffn_w_up', 'new_m_ffn_conv_w', 'new_m_ffn_conv_b', 'new_m_ffn_w_down', 'new_m_final_norm_w', 'new_v_ln1_w', 'new_v_w_in', 'new_v_lb_gamma', 'new_v_hg_norm_w', 'new_v_lru_conv_w', 'new_v_lru_conv_b', 'new_v_lru_wa', 'new_v_lru_ba', 'new_v_lru_wx', 'new_v_lru_bx', 'new_v_lru_lambda', 'new_v_lru_norm_w', 'new_v_w_out', 'new_v_ln2_w', 'new_v_ffn_w_up', 'new_v_ffn_conv_w', 'new_v_ffn_conv_b', 'new_v_ffn_w_down', 'new_v_final_norm_w']
TWIN_LEAF_KINDS = {'loss': 'loss', 'grad_x': 'grad_x', 'grad_ln1_w': 'grad_w', 'grad_w_in': 'grad_w', 'grad_lb_gamma': 'grad_w', 'grad_hg_norm_w': 'grad_w', 'grad_lru_conv_w': 'grad_w', 'grad_lru_conv_b': 'grad_w', 'grad_lru_wa': 'grad_w', 'grad_lru_ba': 'grad_w', 'grad_lru_wx': 'grad_w', 'grad_lru_bx': 'grad_w', 'grad_lru_lambda': 'grad_w', 'grad_lru_norm_w': 'grad_w', 'grad_w_out': 'grad_w', 'grad_ln2_w': 'grad_w', 'grad_ffn_w_up': 'grad_w', 'grad_ffn_conv_w': 'grad_w', 'grad_ffn_conv_b': 'grad_w', 'grad_ffn_w_down': 'grad_w', 'grad_final_norm_w': 'grad_w', 'delta_ln1_w': 'delta_w', 'delta_w_in': 'delta_w', 'delta_lb_gamma': 'delta_w', 'delta_hg_norm_w': 'delta_w', 'delta_lru_conv_w': 'delta_w', 'delta_lru_conv_b': 'delta_w', 'delta_lru_wa': 'delta_w', 'delta_lru_ba': 'delta_w', 'delta_lru_wx': 'delta_w', 'delta_lru_bx': 'delta_w', 'delta_lru_lambda': 'delta_w', 'delta_lru_norm_w': 'delta_w', 'delta_w_out': 'delta_w', 'delta_ln2_w': 'delta_w', 'delta_ffn_w_up': 'delta_w', 'delta_ffn_conv_w': 'delta_w', 'delta_ffn_conv_b': 'delta_w', 'delta_ffn_w_down': 'delta_w', 'delta_final_norm_w': 'delta_w', 'new_m_ln1_w': 'new_m', 'new_m_w_in': 'new_m', 'new_m_lb_gamma': 'new_m', 'new_m_hg_norm_w': 'new_m', 'new_m_lru_conv_w': 'new_m', 'new_m_lru_conv_b': 'new_m', 'new_m_lru_wa': 'new_m', 'new_m_lru_ba': 'new_m', 'new_m_lru_wx': 'new_m', 'new_m_lru_bx': 'new_m', 'new_m_lru_lambda': 'new_m', 'new_m_lru_norm_w': 'new_m', 'new_m_w_out': 'new_m', 'new_m_ln2_w': 'new_m', 'new_m_ffn_w_up': 'new_m', 'new_m_ffn_conv_w': 'new_m', 'new_m_ffn_conv_b': 'new_m', 'new_m_ffn_w_down': 'new_m', 'new_m_final_norm_w': 'new_m', 'new_v_ln1_w': 'new_v', 'new_v_w_in': 'new_v', 'new_v_lb_gamma': 'new_v', 'new_v_hg_norm_w': 'new_v', 'new_v_lru_conv_w': 'new_v', 'new_v_lru_conv_b': 'new_v', 'new_v_lru_wa': 'new_v', 'new_v_lru_ba': 'new_v', 'new_v_lru_wx': 'new_v', 'new_v_lru_bx': 'new_v', 'new_v_lru_lambda': 'new_v', 'new_v_lru_norm_w': 'new_v', 'new_v_w_out': 'new_v', 'new_v_ln2_w': 'new_v', 'new_v_ffn_w_up': 'new_v', 'new_v_ffn_conv_w': 'new_v', 'new_v_ffn_conv_b': 'new_v', 'new_v_ffn_w_down': 'new_v', 'new_v_final_norm_w': 'new_v'}


def _forward(args):
    return _fwd_reference(*[args[k] for k in FWD_PARAMS])


def _output_shape():
    def fwd():
        inp = _fwd_setup_inputs(0)
        return _fwd_reference(*[inp[k] for k in FWD_PARAMS])
    out = _jax.eval_shape(fwd)
    return out.shape, out.dtype

N_MICROBATCH = 1
ADAM_LR = 0.001
ADAM_B1 = 0.9
ADAM_B2 = 0.999
ADAM_EPS = 1e-08
ADAM_WD = 0.01
ADAM_STEP = 10
PER_EXAMPLE_BATCH_AXIS = {'x': 0, 'loss_target': 0}
SHARED_INPUTS = []
_WEIGHT_DTYPES = {'ln1_w': _jnp.float32, 'w_in': _jnp.float32, 'lb_gamma': _jnp.float32, 'hg_norm_w': _jnp.float32, 'lru_conv_w': _jnp.float32, 'lru_conv_b': _jnp.float32, 'lru_wa': _jnp.float32, 'lru_ba': _jnp.float32, 'lru_wx': _jnp.float32, 'lru_bx': _jnp.float32, 'lru_lambda': _jnp.float32, 'lru_norm_w': _jnp.float32, 'w_out': _jnp.float32, 'ln2_w': _jnp.float32, 'ffn_w_up': _jnp.float32, 'ffn_conv_w': _jnp.float32, 'ffn_conv_b': _jnp.float32, 'ffn_w_down': _jnp.float32, 'final_norm_w': _jnp.float32}
MOMENT_SCALE = {'ln1_w': 4.458112e-02, 'w_in': 2.584523e-02, 'lb_gamma': 1.967140e-03, 'hg_norm_w': 2.302406e-02, 'lru_conv_w': 4.001864e-02, 'lru_conv_b': 3.988689e-01, 'lru_wa': 1.188167e-02, 'lru_ba': 9.988185e-03, 'lru_wx': 2.096120e-02, 'lru_bx': 1.394757e-02, 'lru_lambda': 1.952022e-02, 'lru_norm_w': 3.736002e-02, 'w_out': 3.094901e-02, 'ln2_w': 2.699441e-02, 'ffn_w_up': 1.170491e-02, 'ffn_conv_w': 1.178139e-02, 'ffn_conv_b': 1.182533e-02, 'ffn_w_down': 1.890859e-02, 'final_norm_w': 8.002689e+00}


def _to_microbatches(a, axis):
    t = _jnp.moveaxis(a, axis, 0)
    t = t.reshape((N_MICROBATCH, t.shape[0] // N_MICROBATCH) + t.shape[1:])
    return _jnp.moveaxis(t, 1, axis + 1)


def setup_inputs(seed: int = 0) -> dict:
    inp = _fwd_setup_inputs(seed)
    key = _jax.random.fold_in(_jax.random.key(seed), 7919)
    shape, _ = _output_shape()
    out = dict(inp)
    out["loss_target"] = _jax.random.normal(_jax.random.fold_in(key, 0), shape, _jnp.float32)
    for i, name in enumerate(TWIN_WEIGHTS):
        w = inp[name].astype(_jnp.float32)
        if MOMENT_SCALE is None:
            s = _jnp.sqrt(_jnp.mean(_jnp.square(w)) + 1e-30)
        else:
            s = MOMENT_SCALE[name]
        km, kv = _jax.random.split(_jax.random.fold_in(key, i + 1))
        out[name] = w
        out["m_" + name] = s * _jax.random.normal(km, w.shape, _jnp.float32)
        out["v_" + name] = (s * s) * _jax.random.uniform(kv, w.shape, _jnp.float32, 0.5, 1.5)
    if N_MICROBATCH > 1:
        for name, axis in PER_EXAMPLE_BATCH_AXIS.items():
            out[name] = _to_microbatches(out[name], axis)
    return {'x': out['x'], 'ln1_w': out['ln1_w'], 'w_in': out['w_in'], 'lb_gamma': out['lb_gamma'], 'hg_norm_w': out['hg_norm_w'], 'lru_conv_w': out['lru_conv_w'], 'lru_conv_b': out['lru_conv_b'], 'lru_wa': out['lru_wa'], 'lru_ba': out['lru_ba'], 'lru_wx': out['lru_wx'], 'lru_bx': out['lru_bx'], 'lru_lambda': out['lru_lambda'], 'lru_norm_w': out['lru_norm_w'], 'w_out': out['w_out'], 'ln2_w': out['ln2_w'], 'ffn_w_up': out['ffn_w_up'], 'ffn_conv_w': out['ffn_conv_w'], 'ffn_conv_b': out['ffn_conv_b'], 'ffn_w_down': out['ffn_w_down'], 'final_norm_w': out['final_norm_w'], 'loss_target': out['loss_target'], 'm_ln1_w': out['m_ln1_w'], 'm_w_in': out['m_w_in'], 'm_lb_gamma': out['m_lb_gamma'], 'm_hg_norm_w': out['m_hg_norm_w'], 'm_lru_conv_w': out['m_lru_conv_w'], 'm_lru_conv_b': out['m_lru_conv_b'], 'm_lru_wa': out['m_lru_wa'], 'm_lru_ba': out['m_lru_ba'], 'm_lru_wx': out['m_lru_wx'], 'm_lru_bx': out['m_lru_bx'], 'm_lru_lambda': out['m_lru_lambda'], 'm_lru_norm_w': out['m_lru_norm_w'], 'm_w_out': out['m_w_out'], 'm_ln2_w': out['m_ln2_w'], 'm_ffn_w_up': out['m_ffn_w_up'], 'm_ffn_conv_w': out['m_ffn_conv_w'], 'm_ffn_conv_b': out['m_ffn_conv_b'], 'm_ffn_w_down': out['m_ffn_w_down'], 'm_final_norm_w': out['m_final_norm_w'], 'v_ln1_w': out['v_ln1_w'], 'v_w_in': out['v_w_in'], 'v_lb_gamma': out['v_lb_gamma'], 'v_hg_norm_w': out['v_hg_norm_w'], 'v_lru_conv_w': out['v_lru_conv_w'], 'v_lru_conv_b': out['v_lru_conv_b'], 'v_lru_wa': out['v_lru_wa'], 'v_lru_ba': out['v_lru_ba'], 'v_lru_wx': out['v_lru_wx'], 'v_lru_bx': out['v_lru_bx'], 'v_lru_lambda': out['v_lru_lambda'], 'v_lru_norm_w': out['v_lru_norm_w'], 'v_w_out': out['v_w_out'], 'v_ln2_w': out['v_ln2_w'], 'v_ffn_w_up': out['v_ffn_w_up'], 'v_ffn_conv_w': out['v_ffn_conv_w'], 'v_ffn_conv_b': out['v_ffn_conv_b'], 'v_ffn_w_down': out['v_ffn_w_down'], 'v_final_norm_w': out['v_final_norm_w']}


def _loss(weights, diff, rest, loss_target):
    with _jax.named_scope("forward"):
        args = {**rest, TWIN_DIFF_INPUT: diff, **{k: w.astype(_WEIGHT_DTYPES[k]) for k, w in weights.items()}}
        y = _forward(args)
    with _jax.named_scope("loss_head"):
        err = _jnp.square(y.astype(_jnp.float32) - loss_target)
        return 0.5 * _jnp.sum(_jnp.mean(err, axis=-1)) if err.ndim else 0.5 * err


def _adamw(w, g, m, v):
    m = ADAM_B1 * m + (1.0 - ADAM_B1) * g
    v = ADAM_B2 * v + (1.0 - ADAM_B2) * _jnp.square(g)
    m_hat = m / (1.0 - ADAM_B1 ** ADAM_STEP)
    v_hat = v / (1.0 - ADAM_B2 ** ADAM_STEP)
    delta = -ADAM_LR * (m_hat / (_jnp.sqrt(v_hat) + ADAM_EPS) + ADAM_WD * w)
    return delta, m, v


def reference(x, ln1_w, w_in, lb_gamma, hg_norm_w, lru_conv_w, lru_conv_b, lru_wa, lru_ba, lru_wx, lru_bx, lru_lambda, lru_norm_w, w_out, ln2_w, ffn_w_up, ffn_conv_w, ffn_conv_b, ffn_w_down, final_norm_w, loss_target, m_ln1_w, m_w_in, m_lb_gamma, m_hg_norm_w, m_lru_conv_w, m_lru_conv_b, m_lru_wa, m_lru_ba, m_lru_wx, m_lru_bx, m_lru_lambda, m_lru_norm_w, m_w_out, m_ln2_w, m_ffn_w_up, m_ffn_conv_w, m_ffn_conv_b, m_ffn_w_down, m_final_norm_w, v_ln1_w, v_w_in, v_lb_gamma, v_hg_norm_w, v_lru_conv_w, v_lru_conv_b, v_lru_wa, v_lru_ba, v_lru_wx, v_lru_bx, v_lru_lambda, v_lru_norm_w, v_w_out, v_ln2_w, v_ffn_w_up, v_ffn_conv_w, v_ffn_conv_b, v_ffn_w_down, v_final_norm_w):
    given = dict(x=x, ln1_w=ln1_w, w_in=w_in, lb_gamma=lb_gamma, hg_norm_w=hg_norm_w, lru_conv_w=lru_conv_w, lru_conv_b=lru_conv_b, lru_wa=lru_wa, lru_ba=lru_ba, lru_wx=lru_wx, lru_bx=lru_bx, lru_lambda=lru_lambda, lru_norm_w=lru_norm_w, w_out=w_out, ln2_w=ln2_w, ffn_w_up=ffn_w_up, ffn_conv_w=ffn_conv_w, ffn_conv_b=ffn_conv_b, ffn_w_down=ffn_w_down, final_norm_w=final_norm_w, loss_target=loss_target, m_ln1_w=m_ln1_w, m_w_in=m_w_in, m_lb_gamma=m_lb_gamma, m_hg_norm_w=m_hg_norm_w, m_lru_conv_w=m_lru_conv_w, m_lru_conv_b=m_lru_conv_b, m_lru_wa=m_lru_wa, m_lru_ba=m_lru_ba, m_lru_wx=m_lru_wx, m_lru_bx=m_lru_bx, m_lru_lambda=m_lru_lambda, m_lru_norm_w=m_lru_norm_w, m_w_out=m_w_out, m_ln2_w=m_ln2_w, m_ffn_w_up=m_ffn_w_up, m_ffn_conv_w=m_ffn_conv_w, m_ffn_conv_b=m_ffn_conv_b, m_ffn_w_down=m_ffn_w_down, m_final_norm_w=m_final_norm_w, v_ln1_w=v_ln1_w, v_w_in=v_w_in, v_lb_gamma=v_lb_gamma, v_hg_norm_w=v_hg_norm_w, v_lru_conv_w=v_lru_conv_w, v_lru_conv_b=v_lru_conv_b, v_lru_wa=v_lru_wa, v_lru_ba=v_lru_ba, v_lru_wx=v_lru_wx, v_lru_bx=v_lru_bx, v_lru_lambda=v_lru_lambda, v_lru_norm_w=v_lru_norm_w, v_w_out=v_w_out, v_ln2_w=v_ln2_w, v_ffn_w_up=v_ffn_w_up, v_ffn_conv_w=v_ffn_conv_w, v_ffn_conv_b=v_ffn_conv_b, v_ffn_w_down=v_ffn_w_down, v_final_norm_w=v_final_norm_w)
    weights = {n: given[n] for n in TWIN_WEIGHTS}
    shared = {n: given[n] for n in SHARED_INPUTS}
    per_example = {n: given[n] for n in ['x']}
    grad_fn = _jax.value_and_grad(_loss, argnums=(0, 1))

    def one_microbatch(ex, loss_target):
        ex = dict(ex)
        diff = ex.pop(TWIN_DIFF_INPUT)
        return grad_fn(weights, diff, {**shared, **ex}, loss_target)

    if N_MICROBATCH == 1:
        loss, (grad_w, grad_x) = one_microbatch(per_example, given["loss_target"])
    else:
        def body(carry, xs):
            loss_sum, grad_sum = carry
            l_k, (gw_k, gx_k) = one_microbatch(xs[0], xs[1])
            with _jax.named_scope("update"):
                return (loss_sum + l_k, _jax.tree.map(_jnp.add, grad_sum, gw_k)), gx_k

        init = (_jnp.zeros((), _jnp.float32), _jax.tree.map(_jnp.zeros_like, weights))
        (loss, grad_w), grad_x = _jax.lax.scan(body, init, (per_example, given["loss_target"]))
    with _jax.named_scope("update"):
        delta_w, new_m, new_v = {}, {}, {}
        for n in TWIN_WEIGHTS:
            delta_w[n], new_m[n], new_v[n] = _adamw(weights[n], grad_w[n], given["m_" + n], given["v_" + n])
    return (loss, grad_x, *[grad_w[n] for n in TWIN_WEIGHTS], *[delta_w[n] for n in TWIN_WEIGHTS],
            *[new_m[n] for n in TWIN_WEIGHTS], *[new_v[n] for n in TWIN_WEIGHTS])
```

```python
import math

import jax
import jax.numpy as jnp
from jax import lax
from jax.experimental import pallas as pl
from jax.experimental.pallas import tpu as pltpu

F32, BF16 = jnp.float32, jnp.bfloat16
EPS = 1e-6
HEAD = 128
CHUNK = 64
SUB = 16
NSUB = CHUNK // SUB
LRU_C = 8.0
LANE = 128
NDEV = 8
ADAM_LR, ADAM_B1, ADAM_B2, ADAM_EPS, ADAM_WD, ADAM_STEP = 0.001, 0.9, 0.999, 1e-08, 0.01, 10
MESH = pl.DeviceIdType.MESH
ANY = pl.BlockSpec(memory_space=pl.ANY)
VMEM_LIMIT = 56 * 1024 * 1024


def _cp(**kw):
    return pltpu.CompilerParams(vmem_limit_bytes=VMEM_LIMIT, **kw)


def _tile(n, cap, mult=LANE):
    best = None
    for t in range(mult, min(n, cap) + 1, mult):
        if n % t == 0:
            best = t
    return best if best is not None else n


def _row_tile(r, cdim, budget=262144):
    return _tile(r, max(16, budget // cdim), 16)


def _sigmoid(x):
    return jax.nn.sigmoid(x)


def _dsilu(x, s):
    return s * (1.0 + x * (1.0 - s))


def _iota_rows(n, w=LANE):
    return lax.broadcasted_iota(jnp.int32, (n, w), 0)


def _shift_down(prev8, xt, k):
    cat = jnp.concatenate([prev8, xt], axis=0)
    return pltpu.roll(cat, k, 0)[8:]


def _shift_up(xt, next8, k):
    cat = jnp.concatenate([xt, next8], axis=0)
    n = cat.shape[0]
    return pltpu.roll(cat, n - k, 0)[: xt.shape[0]]


def _scan_fwd(a, u):
    n = a.shape[0]
    row = _iota_rows(n, a.shape[1])
    k = 1
    while k < n:
        keep = row >= k
        a_s = jnp.where(keep, pltpu.roll(a, k, 0), 1.0)
        u_s = jnp.where(keep, pltpu.roll(u, k, 0), 0.0)
        u = a * u_s + u
        a = a * a_s
        k *= 2
    return a, u


def _scan_bwd(a, u):
    n = a.shape[0]
    row = _iota_rows(n, a.shape[1])
    k = 1
    while k < n:
        keep = row < n - k
        a_s = jnp.where(keep, pltpu.roll(a, n - k, 0), 1.0)
        u_s = jnp.where(keep, pltpu.roll(u, n - k, 0), 0.0)
        u = a * u_s + u
        a = a * a_s
        k *= 2
    return a, u


def _cumsum_fwd(u):
    n = u.shape[0]
    row = _iota_rows(n, u.shape[1])
    k = 1
    while k < n:
        u = u + jnp.where(row >= k, pltpu.roll(u, k, 0), 0.0)
        k *= 2
    return u


def _cumsum_bwd(u):
    n = u.shape[0]
    row = _iota_rows(n, u.shape[1])
    k = 1
    while k < n:
        u = u + jnp.where(row < n - k, pltpu.roll(u, n - k, 0), 0.0)
        k *= 2
    return u


def _dot(a, b, dims):
    return lax.dot_general(a.astype(BF16), b.astype(BF16), (dims, ((), ())), preferred_element_type=F32)


NN = ((1,), (0,))
NT = ((1,), (1,))
TN = ((0,), (0,))


def _mm(a, b3, *, kind, out_dtype, name, res=None, tm_cap=1024, tn_cap=1536, tk_cap=1024, out_blocks=1):
    if kind == "nn":
        m, kdim = a.shape
        nb, _, nsh = b3.shape
        n = nb * nsh
        tm, tn, tk = _tile(m, tm_cap), _tile(nsh, tn_cap), _tile(kdim, tk_cap)
        per = nsh // tn
        a_spec = pl.BlockSpec((tm, tk), lambda i, j, k: (i, k))
        b_spec = pl.BlockSpec((None, tk, tn), lambda i, j, k: (j // per, k, j % per))
        dims = NN
    elif kind == "nt":
        m, kdim = a.shape
        nb, n, ksh = b3.shape
        tm, tn, tk = _tile(m, tm_cap), _tile(n, tn_cap), _tile(ksh, tk_cap)
        per = ksh // tk
        a_spec = pl.BlockSpec((tm, tk), lambda i, j, k: (i, k))
        b_spec = pl.BlockSpec((None, tn, tk), lambda i, j, k: (k // per, j, k % per))
        dims = NT
    else:
        kdim, m = a.shape
        n = b3.shape[1]
        nsh = n // out_blocks
        tm, tn, tk = _tile(m, tm_cap), _tile(nsh, tn_cap), _tile(kdim, tk_cap)
        per = nsh // tn
        a_spec = pl.BlockSpec((tk, tm), lambda i, j, k: (k, i))
        b_spec = pl.BlockSpec((tk, tn), lambda i, j, k: (k, j))
        dims = TN
    nk = kdim // tk
    grid = (m // tm, n // tn, nk)

    def body(*refs):
        if res is None:
            a_ref, b_ref, o_ref, acc = refs
        else:
            a_ref, b_ref, r_ref, o_ref, acc = refs
        k = pl.program_id(2)

        @pl.when(k == 0)
        def _():
            acc[...] = jnp.zeros_like(acc)

        acc[...] += lax.dot_general(a_ref[...], b_ref[...], (dims, ((), ())), preferred_element_type=F32)

        @pl.when(k == nk - 1)
        def _():
            r = acc[...]
            if res is not None:
                r = r + r_ref[...]
            o_ref[...] = r.astype(o_ref.dtype)

    in_specs = [a_spec, b_spec]
    args = [a, b3]
    if res is not None:
        in_specs.append(pl.BlockSpec((tm, tn), lambda i, j, k: (i, j)))
        args.append(res)
    if kind == "tn":
        out_shape = jax.ShapeDtypeStruct((out_blocks, m, nsh), out_dtype)
        out_spec = pl.BlockSpec((None, tm, tn), lambda i, j, k: (j // per, i, j % per))
    else:
        out_shape = jax.ShapeDtypeStruct((m, n), out_dtype)
        out_spec = pl.BlockSpec((tm, tn), lambda i, j, k: (i, j))
    return pl.pallas_call(
        body, name=name, grid=grid, in_specs=in_specs, out_specs=out_spec, out_shape=out_shape,
        scratch_shapes=[pltpu.VMEM((tm, tn), F32)],
        compiler_params=_cp(dimension_semantics=("parallel", "parallel", "arbitrary")),
    )(*args)


def _rms_fwd(x, w, *, name, tm=256):
    t, d = x.shape
    tm = _tile(t, tm, 16)

    def body(x_ref, w_ref, o_ref):
        xv = x_ref[...]
        r = lax.rsqrt(jnp.mean(xv * xv, axis=-1, keepdims=True) + EPS)
        o_ref[...] = ((xv * r) * w_ref[...]).astype(o_ref.dtype)

    return pl.pallas_call(
        body, name=name, grid=(t // tm,),
        in_specs=[pl.BlockSpec((tm, d), lambda i: (i, 0)), pl.BlockSpec((1, d), lambda i: (0, 0))],
        out_specs=pl.BlockSpec((tm, d), lambda i: (i, 0)),
        out_shape=jax.ShapeDtypeStruct((t, d), BF16), compiler_params=_cp(),
    )(x, w)


def _rms_bwd(dy, x, w, *, name, extra=None, dy_cb=0, want_bf16=False, tm=256):
    t, d = x.shape
    tm = _tile(t, tm, 16)

    def body(*refs):
        refs = list(refs)
        dy_ref, x_ref, w_ref = refs[:3]
        e_ref = refs[3] if extra is not None else None
        outs = refs[4:] if extra is not None else refs[3:]
        dx_ref = outs[0]
        dxb_ref = outs[1] if want_bf16 else None
        dw_ref = outs[-1]
        i = pl.program_id(0)
        xv = x_ref[...]
        r = lax.rsqrt(jnp.mean(xv * xv, axis=-1, keepdims=True) + EPS)
        nh = xv * r
        dyv = dy_ref[...]
        dn = dyv * w_ref[...]
        dx = r * (dn - nh * jnp.mean(dn * nh, axis=-1, keepdims=True))
        if extra is not None:
            dx = dx + e_ref[...]
        dx_ref[...] = dx
        if want_bf16:
            dxb_ref[...] = dx.astype(BF16)
        part = jnp.sum(dyv * nh, axis=0, keepdims=True)

        @pl.when(i == 0)
        def _():
            dw_ref[...] = part

        @pl.when(i > 0)
        def _():
            dw_ref[...] += part

    row = pl.BlockSpec((tm, d), lambda i: (i, 0))
    in_specs = [pl.BlockSpec((tm, d), lambda i: (i, dy_cb)), row, pl.BlockSpec((1, d), lambda i: (0, 0))]
    args = [dy, x, w]
    if extra is not None:
        in_specs.append(row)
        args.append(extra)
    out_shape = [jax.ShapeDtypeStruct((t, d), F32)]
    out_specs = [row]
    if want_bf16:
        out_shape.append(jax.ShapeDtypeStruct((t, d), BF16))
        out_specs.append(row)
    out_shape.append(jax.ShapeDtypeStruct((1, d), F32))
    out_specs.append(pl.BlockSpec((1, d), lambda i: (0, 0)))
    return pl.pallas_call(
        body, name=name, grid=(t // tm,), in_specs=in_specs, out_specs=out_specs, out_shape=out_shape,
        compiler_params=_cp(dimension_semantics=("arbitrary",)),
    )(*args)


def _loss_head(h, w, tgt, *, name, tm=256):
    t, d = h.shape
    tm = _tile(t, tm, 16)

    def body(h_ref, w_ref, t_ref, dh_ref, dhb_ref, dw_ref, loss_ref):
        i = pl.program_id(0)
        xv = h_ref[...]
        wv = w_ref[...]
        r = lax.rsqrt(jnp.mean(xv * xv, axis=-1, keepdims=True) + EPS)
        nh = xv * r
        e = nh * wv - t_ref[...]
        part_loss = jnp.full((1, LANE), 0.5 * jnp.sum(jnp.mean(e * e, axis=-1, keepdims=True)), F32)
        dyv = e * (1.0 / d)
        dn = dyv * wv
        dx = r * (dn - nh * jnp.mean(dn * nh, axis=-1, keepdims=True))
        dh_ref[...] = dx
        dhb_ref[...] = dx.astype(BF16)
        part = jnp.sum(dyv * nh, axis=0, keepdims=True)

        @pl.when(i == 0)
        def _():
            dw_ref[...] = part
            loss_ref[...] = part_loss

        @pl.when(i > 0)
        def _():
            dw_ref[...] += part
            loss_ref[...] += part_loss

    row = pl.BlockSpec((tm, d), lambda i: (i, 0))
    vec = pl.BlockSpec((1, d), lambda i: (0, 0))
    return pl.pallas_call(
        body, name=name, grid=(t // tm,), in_specs=[row, vec, row],
        out_specs=[row, row, vec, pl.BlockSpec((1, LANE), lambda i: (0, 0))],
        out_shape=[jax.ShapeDtypeStruct((t, d), F32), jax.ShapeDtypeStruct((t, d), BF16),
                   jax.ShapeDtypeStruct((1, d), F32), jax.ShapeDtypeStruct((1, LANE), F32)],
        compiler_params=_cp(dimension_semantics=("arbitrary",)),
    )(h, w, tgt)


def _lower_bound(lbg_ref):
    g0, g1 = lbg_ref[0:1, :], lbg_ref[1:2, :]
    m = jnp.maximum(g0, g1)
    e0, e1 = jnp.exp(g0 - m), jnp.exp(g1 - m)
    return e0 / (e0 + e1)


def _seg_bounds():
    offs, o = {}, 0
    for i in range(1, NSUB):
        offs[i] = (o, o + SUB * i)
        o += SUB * i
    return offs, o


def _pad_rows(x, n):
    if x.shape[0] == n:
        return x
    return jnp.concatenate([x, jnp.zeros((n - x.shape[0], x.shape[1]), x.dtype)], axis=0)


def _offdiag_setup(q, k, b, v, b_c):
    offs, total = _seg_bounds()
    padded = -(-total // LANE) * LANE
    eq_parts = [jnp.zeros((SUB, HEAD), F32)]
    ek_parts, k_parts, v_parts = [], [], []
    for i in range(1, NSUB):
        r_i = b_c[SUB * i - 1:SUB * i, :]
        eq_parts.append(jnp.exp(b[SUB * i:SUB * (i + 1)] - r_i))
        ek_parts.append(jnp.exp(r_i - b[0:SUB * i]))
        k_parts.append(k[0:SUB * i])
        v_parts.append(v[0:SUB * i])
    eq = jnp.concatenate(eq_parts, axis=0)
    ek = _pad_rows(jnp.concatenate(ek_parts, axis=0), padded)
    kt = _pad_rows(jnp.concatenate(k_parts, axis=0), padded) * ek
    vs = _pad_rows(jnp.concatenate(v_parts, axis=0), padded)
    qt = q * eq
    rsub = lax.broadcasted_iota(jnp.int32, (CHUNK, padded), 0) // SUB
    col = lax.broadcasted_iota(jnp.int32, (CHUNK, padded), 1)
    cseg = jnp.zeros((CHUNK, padded), jnp.int32)
    for i in range(1, NSUB):
        cseg = cseg + (col >= offs[i][0]).astype(jnp.int32)
    mask = (rsub == cseg) & (col < total)
    a = jnp.where(mask, _dot(qt, kt, NT), 0.0)
    return offs, eq, ek, kt, vs, qt, mask, a


def _hgrn_fwd(proj, lbg, nw, *, nb, seq, name):
    t = proj.shape[0]
    w = lbg.shape[1]
    nh = w // HEAD
    nc = seq // CHUNK

    def body(q_ref, f_ref, i_ref, g_ref, lbg_ref, nw_ref, ohg_ref, opre_ref, st_ref, k_c, b_c, v_c, st):
        lb = _lower_bound(lbg_ref)
        nwv = nw_ref[...]
        st[...] = jnp.zeros_like(st)

        def chunk(c, carry):
            rows = pl.ds(pl.multiple_of(c * CHUNK, CHUNK), CHUNK)
            qr = q_ref[rows, :]
            q = qr * _sigmoid(qr)
            f = lb + (1.0 - lb) * _sigmoid(f_ref[rows, :])
            k = 1.0 - f
            b = _cumsum_fwd(jnp.log(f))
            v = i_ref[rows, :]
            k_c[...] = k
            b_c[...] = b
            v_c[...] = v
            s_t = st[...]
            st_ref[c] = s_t
            o = _dot(q * jnp.exp(b), s_t, NT)
            _, _, _, _, vs, _, _, a = _offdiag_setup(q, k, b, v, b_c)
            o = o + _dot(a, vs, NN)
            diag = []
            for i in range(NSUB):
                accs = [jnp.zeros((8, HEAD), F32) for _ in range(SUB // 8)]
                for j in range(SUB):
                    r = SUB * i + j
                    bj, kj, vj = b_c[r:r + 1, :], k_c[r:r + 1, :], v_c[r:r + 1, :]
                    for p in range(j // 8, SUB // 8):
                        lo = SUB * i + 8 * p
                        d = jnp.exp(b[lo:lo + 8] - bj)
                        if 8 * p < j:
                            d = jnp.where(_iota_rows(8) + 8 * p >= j, d, 0.0)
                        s = jnp.sum(q[lo:lo + 8] * d * kj, axis=-1, keepdims=True)
                        accs[p] = accs[p] + s * vj
                diag.extend(accs)
            o = o + jnp.concatenate(diag, axis=0)
            bl = b_c[CHUNK - 1:CHUNK, :]
            kb = k * jnp.exp(bl - b)
            st[...] = s_t * jnp.exp(bl) + _dot(v, kb, TN)
            opre_ref[rows, :] = o
            rn = lax.rsqrt(jnp.mean(o * o, axis=-1, keepdims=True) + EPS)
            gr = g_ref[rows, :]
            ohg_ref[rows, :] = (((o * rn) * nwv) * (gr * _sigmoid(gr))).astype(BF16)
            return carry

        lax.fori_loop(0, nc, chunk, 0)

    def col(off):
        return pl.BlockSpec((seq, HEAD), lambda h, b: (b, off * nh + h))

    vec = lambda r: pl.BlockSpec((r, HEAD), lambda h, b: (0, h))
    out_blk = pl.BlockSpec((seq, HEAD), lambda h, b: (b, h))
    return pl.pallas_call(
        body, name=name, grid=(nh, nb),
        in_specs=[col(0), col(1), col(2), col(3), vec(2), vec(1)],
        out_specs=[out_blk, out_blk, pl.BlockSpec((None, None, nc, HEAD, HEAD), lambda h, b: (b, h, 0, 0, 0))],
        out_shape=[jax.ShapeDtypeStruct((t, w), BF16), jax.ShapeDtypeStruct((t, w), F32),
                   jax.ShapeDtypeStruct((nb, nh, nc, HEAD, HEAD), F32)],
        scratch_shapes=[pltpu.VMEM((CHUNK, HEAD), F32)] * 3 + [pltpu.VMEM((HEAD, HEAD), F32)],
        compiler_params=_cp(dimension_semantics=("parallel", "parallel")),
    )(proj, proj, proj, proj, lbg, nw)


def _hgrn_bwd(proj, lbg, nw, opre, states, dmix, *, nb, seq, name):
    t = proj.shape[0]
    w = lbg.shape[1]
    nh = w // HEAD
    nc = seq // CHUNK

    def body(q_ref, f_ref, i_ref, g_ref, lbg_ref, nw_ref, opre_ref, st_ref, dm_ref,
             dq_ref, df_ref, di_ref, dg_ref, small_ref, k_c, b_c, v_c, dst, dlb_s, dwn_s):
        bi = pl.program_id(1)
        lb = _lower_bound(lbg_ref)
        nwv = nw_ref[...]
        dst[...] = jnp.zeros_like(dst)

        @pl.when(bi == 0)
        def _():
            dlb_s[...] = jnp.zeros_like(dlb_s)
            dwn_s[...] = jnp.zeros_like(dwn_s)

        def chunk(it, carry):
            c = nc - 1 - it
            rows = pl.ds(pl.multiple_of(c * CHUNK, CHUNK), CHUNK)
            qr = q_ref[rows, :]
            sq = _sigmoid(qr)
            q = qr * sq
            sg = _sigmoid(f_ref[rows, :])
            f = lb + (1.0 - lb) * sg
            k = 1.0 - f
            b = _cumsum_fwd(jnp.log(f))
            v = i_ref[rows, :]
            k_c[...] = k
            b_c[...] = b
            v_c[...] = v
            o = opre_ref[rows, :]
            rn = lax.rsqrt(jnp.mean(o * o, axis=-1, keepdims=True) + EPS)
            nhat = o * rn
            dm = dm_ref[rows, :]
            gr = g_ref[rows, :]
            sgr = _sigmoid(gr)
            dnw = dm * (gr * sgr)
            dg_ref[rows, :] = (dm * (nhat * nwv) * _dsilu(gr, sgr)).astype(BF16)
            dwn_s[...] += jnp.sum(dnw * nhat, axis=0, keepdims=True)
            dn = dnw * nwv
            do = rn * (dn - nhat * jnp.mean(dn * nhat, axis=-1, keepdims=True))
            s_t = st_ref[c]
            ds = dst[...]
            eb = jnp.exp(b)
            qb = q * eb
            bl = b_c[CHUNK - 1:CHUNK, :]
            ebl = jnp.exp(bl)
            kdec = jnp.exp(bl - b)
            kb = k * kdec
            dqb = _dot(do, s_t, NN)
            dkb = _dot(v, ds, NN)
            dv = _dot(kb, ds, NT)
            d_ebl = jnp.sum(ds * s_t, axis=0, keepdims=True)
            dst[...] = ds * ebl + _dot(do, qb, TN)
            dq = dqb * eb
            dk = dkb * kdec
            t_kb = dkb * kb
            db = dqb * qb - t_kb
            db_last = jnp.sum(t_kb, axis=0, keepdims=True) + d_ebl * ebl
            offs, eq, ek, kt, vs, qt, mask, a = _offdiag_setup(q, k, b, v, b_c)
            da = jnp.where(mask, _dot(do, vs, NT), 0.0)
            dvs = _dot(a, do, TN)
            dqt = _dot(da, kt, NN)
            dkt = _dot(da, qt, TN)
            dq = dq + dqt * eq
            db = db + dqt * qt
            for i in range(1, NSUB):
                lo, hi = offs[i]
                dk = dk + _pad_rows(dkt[lo:hi] * ek[lo:hi], CHUNK)
                db = db - _pad_rows(dkt[lo:hi] * kt[lo:hi], CHUNK)
                dv = dv + _pad_rows(dvs[lo:hi], CHUNK)
            npv = SUB // 8
            dq_d, dk_d, dv_d, db_d = [], [], [], []
            for i in range(NSUB):
                aq = [jnp.zeros((8, HEAD), F32) for _ in range(npv)]
                ak = [jnp.zeros((8, HEAD), F32) for _ in range(npv)]
                av = [jnp.zeros((8, HEAD), F32) for _ in range(npv)]
                ab = [jnp.zeros((8, HEAD), F32) for _ in range(npv)]
                for j in range(SUB):
                    r = SUB * i + j
                    bj, kj, vj = b_c[r:r + 1, :], k_c[r:r + 1, :], v_c[r:r + 1, :]
                    sk = jnp.zeros((1, HEAD), F32)
                    sv = jnp.zeros((1, HEAD), F32)
                    sb = jnp.zeros((1, HEAD), F32)
                    for p in range(j // 8, npv):
                        lo = SUB * i + 8 * p
                        d = jnp.exp(b[lo:lo + 8] - bj)
                        if 8 * p < j:
                            d = jnp.where(_iota_rows(8) + 8 * p >= j, d, 0.0)
                        qd = q[lo:lo + 8] * d
                        dop = do[lo:lo + 8]
                        a_j = jnp.sum(qd * kj, axis=-1, keepdims=True)
                        da_j = jnp.sum(dop * vj, axis=-1, keepdims=True)
                        aq[p] = aq[p] + (da_j * d) * kj
                        pm = da_j * qd
                        rk = pm * kj
                        ab[p] = ab[p] + rk
                        sk = sk + jnp.sum(pm, axis=0, keepdims=True)
                        sb = sb + jnp.sum(rk, axis=0, keepdims=True)
                        sv = sv + jnp.sum(a_j * dop, axis=0, keepdims=True)
                    pj = j // 8
                    here = _iota_rows(8) == (j - 8 * pj)
                    ak[pj] = ak[pj] + jnp.where(here, sk, 0.0)
                    av[pj] = av[pj] + jnp.where(here, sv, 0.0)
                    ab[pj] = ab[pj] - jnp.where(here, sb, 0.0)
                dq_d.extend(aq)
                dk_d.extend(ak)
                dv_d.extend(av)
                db_d.extend(ab)
            dq = dq + jnp.concatenate(dq_d, axis=0)
            dk = dk + jnp.concatenate(dk_d, axis=0)
            dv = dv + jnp.concatenate(dv_d, axis=0)
            db = db + jnp.concatenate(db_d, axis=0)
            db = db + jnp.where(_iota_rows(CHUNK) == CHUNK - 1, db_last, 0.0)
            dgl = _cumsum_bwd(db)
            dfv = dgl / f - dk
            dlb_s[...] += jnp.sum(dfv * (1.0 - sg), axis=0, keepdims=True)
            df_ref[rows, :] = (dfv * (1.0 - lb) * (sg * (1.0 - sg))).astype(BF16)
            dq_ref[rows, :] = (dq * _dsilu(qr, sq)).astype(BF16)
            di_ref[rows, :] = dv.astype(BF16)
            return carry

        lax.fori_loop(0, nc, chunk, 0)

        @pl.when(bi == nb - 1)
        def _():
            dgam = dlb_s[...] * lb * (1.0 - lb)
            small_ref[...] = jnp.zeros_like(small_ref)
            small_ref[0:1, :] = dgam
            small_ref[1:2, :] = -dgam
            small_ref[2:3, :] = dwn_s[...]

    def col(off):
        return pl.BlockSpec((seq, HEAD), lambda h, b: (b, off * nh + h))

    vec = lambda r: pl.BlockSpec((r, HEAD), lambda h, b: (0, h))
    blk = pl.BlockSpec((seq, HEAD), lambda h, b: (b, h))
    dshape = jax.ShapeDtypeStruct((t, w), BF16)
    return pl.pallas_call(
        body, name=name, grid=(nh, nb),
        in_specs=[col(0), col(1), col(2), col(3), vec(2), vec(1), blk,
                  pl.BlockSpec((None, None, nc, HEAD, HEAD), lambda h, b: (b, h, 0, 0, 0)), blk],
        out_specs=[blk, blk, blk, blk, vec(8)],
        out_shape=[dshape, dshape, dshape, dshape, jax.ShapeDtypeStruct((8, w), F32)],
        scratch_shapes=[pltpu.VMEM((CHUNK, HEAD), F32)] * 3 + [pltpu.VMEM((HEAD, HEAD), F32)]
        + [pltpu.VMEM((1, HEAD), F32)] * 2,
        compiler_params=_cp(dimension_semantics=("parallel", "arbitrary")),
    )(proj, proj, proj, proj, lbg, nw, opre, states, dmix)


def _expm1(x):
    poly = x * (1.0 + x * (0.5 + x * (1.0 / 6 + x * (1.0 / 24 + x * (1.0 / 120 + x * (1.0 / 720))))))
    return jnp.where(jnp.abs(x) < 0.25, poly, jnp.exp(x) - 1.0)


def _softplus_neg(lam):
    x = -lam
    e = jnp.exp(-jnp.abs(x))
    u = 1.0 + e
    l1p = jnp.where(u == 1.0, e, jnp.log(u) * (e / jnp.where(u == 1.0, 1.0, u - 1.0)))
    return jnp.maximum(x, 0.0) + l1p


_GELU_C = math.sqrt(2.0 / math.pi)


def _gelu(y):
    return 0.5 * y * (1.0 + jnp.tanh(_GELU_C * (y + 0.044715 * (y * y * y))))


def _dgelu(y):
    th = jnp.tanh(_GELU_C * (y + 0.044715 * (y * y * y)))
    return 0.5 * (1.0 + th) + 0.5 * y * (1.0 - th * th) * (_GELU_C * (1.0 + 3 * 0.044715 * (y * y)))


def _lru_gates(xr, prev8, cw_ref, cb, wa_ref, ba, wx_ref, bx, sp, first):
    x3, x2, x1 = _shift_down(prev8, xr, 3), _shift_down(prev8, xr, 2), _shift_down(prev8, xr, 1)
    xb = cb + x3 * cw_ref[0:1, :]
    xb = xb + x2 * cw_ref[1:2, :]
    xb = xb + x1 * cw_ref[2:3, :]
    xb = xb + xr * cw_ref[3:4, :]
    r = _sigmoid(_dot(xb, wa_ref[...], NN) + ba)
    ig = _sigmoid(_dot(xb, wx_ref[...], NN) + bx)
    la = (-LRU_C * r) * sp
    a = jnp.exp(la)
    start = jnp.logical_and(first, _iota_rows(xr.shape[0]) == 0)
    mult = jnp.where(start, 1.0, jnp.sqrt(-_expm1(2.0 * la)))
    return (x3, x2, x1), xb, r, ig, a, mult, start


def _lru_fwd(proj, cw, cb, wa, ba, wx, bx, lam, *, nb, seq, name):
    t = proj.shape[0]
    w = cb.shape[1]
    nblk = w // HEAD
    nc = seq // CHUNK

    def body(x_ref, y_ref, cw_ref, cb_ref, wa_ref, ba_ref, wx_ref, bx_ref, lam_ref, p_ref, h_ref):
        sp = _softplus_neg(lam_ref[...])
        cb_v, ba_v, bx_v = cb_ref[...], ba_ref[...], bx_ref[...]

        def tile(c, carry):
            hc, prev8 = carry
            base = pl.multiple_of(c * CHUNK, CHUNK)
            rows = pl.ds(base, CHUNK)
            xr = x_ref[rows, :]
            _, xb, _, ig, a, mult, _ = _lru_gates(xr, prev8, cw_ref, cb_v, wa_ref, ba_v, wx_ref, bx_v, sp, c == 0)
            ap, up = _scan_fwd(a, xb * ig * mult)
            h = up + ap * hc
            h_ref[rows, :] = h
            p_ref[rows, :] = h * _gelu(y_ref[rows, :])
            return h_ref[pl.ds(base + CHUNK - 1, 1), :], xr[CHUNK - 8:]

        lax.fori_loop(0, nc, tile, (jnp.zeros((1, HEAD), F32), jnp.zeros((8, HEAD), F32)))

    def col(off):
        return pl.BlockSpec((seq, HEAD), lambda n, b: (b, off * nblk + n))

    vec = lambda r: pl.BlockSpec((r, HEAD), lambda n, b: (0, n))
    mat = pl.BlockSpec((None, HEAD, HEAD), lambda n, b: (n, 0, 0))
    blk = pl.BlockSpec((seq, HEAD), lambda n, b: (b, n))
    return pl.pallas_call(
        body, name=name, grid=(nblk, nb),
        in_specs=[col(4), col(5), vec(4), vec(1), mat, vec(1), mat, vec(1), vec(1)],
        out_specs=[blk, blk],
        out_shape=[jax.ShapeDtypeStruct((t, w), F32), jax.ShapeDtypeStruct((t, w), F32)],
        compiler_params=_cp(dimension_semantics=("parallel", "parallel")),
    )(proj, proj, cw, cb, wa, ba, wx, bx, lam)


def _lru_bwd(proj, cw, cb, wa, ba, wx, bx, lam, hsv, dp, *, nb, seq, name):
    t = proj.shape[0]
    w = cb.shape[1]
    nblk = w // HEAD
    nc = seq // CHUNK

    def body(x_ref, y_ref, cw_ref, cb_ref, wa_ref, ba_ref, wx_ref, bx_ref, lam_ref, h_ref, dp_ref,
             dx_ref, dy_ref, small_ref, dwa_ref, dwx_ref, a_c, g_c, acc, dwa_s, dwx_s):
        bi = pl.program_id(1)
        lamv = lam_ref[...]
        sp = _softplus_neg(lamv)
        cb_v, ba_v, bx_v = cb_ref[...], ba_ref[...], bx_ref[...]

        @pl.when(bi == 0)
        def _():
            acc[...] = jnp.zeros_like(acc)
            dwa_s[...] = jnp.zeros_like(dwa_s)
            dwx_s[...] = jnp.zeros_like(dwx_s)

        def tile(it, carry):
            g_next, a_next, dxb_next8 = carry
            c = nc - 1 - it
            base = pl.multiple_of(c * CHUNK, CHUNK)
            rows = pl.ds(base, CHUNK)
            before = pl.ds(pl.multiple_of(jnp.maximum(base - 8, 0), 8), 8)
            inner = c > 0
            xr = x_ref[rows, :]
            prev8 = jnp.where(inner, x_ref[before, :], 0.0)
            (x3, x2, x1), xb, r, ig, a, mult, start = _lru_gates(
                xr, prev8, cw_ref, cb_v, wa_ref, ba_v, wx_ref, bx_v, sp, c == 0)
            h = h_ref[rows, :]
            h_m1 = _shift_down(jnp.where(inner, h_ref[before, :], 0.0), h, 1)
            yv = y_ref[rows, :]
            dpv = dp_ref[rows, :]
            dy_ref[rows, :] = (dpv * h * _dgelu(yv)).astype(BF16)
            dh = dpv * _gelu(yv)
            a_up = _shift_up(a, jnp.broadcast_to(a_next, (8, HEAD)), 1)
            ap, gp = _scan_bwd(a_up, dh)
            g = gp + ap * g_next
            a_c[...] = a
            g_c[...] = g
            da = g * h_m1
            gx = g * xb
            dxb = g * ig * mult
            dig = gx * mult
            dmult = jnp.where(start, 0.0, gx * ig)
            dla = da * a - dmult * (a * a) / mult
            dzr = (dla * (-LRU_C * sp)) * (r * (1.0 - r))
            dzi = dig * (ig * (1.0 - ig))
            dxb = dxb + _dot(dzr, wa_ref[...], NT) + _dot(dzi, wx_ref[...], NT)
            dwa_s[...] += _dot(xb, dzr, TN)
            dwx_s[...] += _dot(xb, dzi, TN)
            d1, d2, d3 = (_shift_up(dxb, dxb_next8, s) for s in (1, 2, 3))
            dx = dxb * cw_ref[3:4, :] + d1 * cw_ref[2:3, :] + d2 * cw_ref[1:2, :] + d3 * cw_ref[0:1, :]
            dx_ref[rows, :] = dx.astype(BF16)
            colsum = lambda z: jnp.sum(z, axis=0, keepdims=True)
            acc[0:1, :] += colsum(x3 * dxb)
            acc[1:2, :] += colsum(x2 * dxb)
            acc[2:3, :] += colsum(x1 * dxb)
            acc[3:4, :] += colsum(xr * dxb)
            acc[4:5, :] += colsum(dxb)
            acc[5:6, :] += colsum(dzr)
            acc[6:7, :] += colsum(dzi)
            acc[7:8, :] += colsum(dla * (-LRU_C * r))
            return g_c[0:1, :], a_c[0:1, :], dxb[0:8]

        zero = jnp.zeros((1, HEAD), F32)
        lax.fori_loop(0, nc, tile, (zero, zero, jnp.zeros((8, HEAD), F32)))

        @pl.when(bi == nb - 1)
        def _():
            small_ref[...] = acc[...]
            small_ref[7:8, :] = acc[7:8, :] * (-_sigmoid(-lamv))
            dwa_ref[...] = dwa_s[...]
            dwx_ref[...] = dwx_s[...]

    def col(off):
        return pl.BlockSpec((seq, HEAD), lambda n, b: (b, off * nblk + n))

    vec = lambda r: pl.BlockSpec((r, HEAD), lambda n, b: (0, n))
    mat = pl.BlockSpec((None, HEAD, HEAD), lambda n, b: (n, 0, 0))
    blk = pl.BlockSpec((seq, HEAD), lambda n, b: (b, n))
    dshape = jax.ShapeDtypeStruct((t, w), BF16)
    return pl.pallas_call(
        body, name=name, grid=(nblk, nb),
        in_specs=[col(4), col(5), vec(4), vec(1), mat, vec(1), mat, vec(1), vec(1), blk, blk],
        out_specs=[blk, blk, vec(8), mat, mat],
        out_shape=[dshape, dshape, jax.ShapeDtypeStruct((8, w), F32),
                   jax.ShapeDtypeStruct((nblk, HEAD, HEAD), F32), jax.ShapeDtypeStruct((nblk, HEAD, HEAD), F32)],
        scratch_shapes=[pltpu.VMEM((CHUNK, HEAD), F32)] * 2 + [pltpu.VMEM((8, HEAD), F32)]
        + [pltpu.VMEM((HEAD, HEAD), F32)] * 2,
        compiler_params=_cp(dimension_semantics=("parallel", "arbitrary")),
    )(proj, proj, cw, cb, wa, ba, wx, bx, lam, hsv, dp)


def _ffn_conv(x, prev8, cw_ref, cbv):
    x2, x1 = _shift_down(prev8, x, 2), _shift_down(prev8, x, 1)
    y = cbv + x2 * cw_ref[0:1, :]
    y = y + x1 * cw_ref[1:2, :]
    y = y + x * cw_ref[2:3, :]
    return (x2, x1), y


def _ffn_act_fwd(up, cw, cb, *, nb, seq, name):
    t, f2 = up.shape
    f = f2 // 2
    tc = _tile(f, 256)
    nj = f // tc
    rt = _tile(seq, 256, 16)
    nr = seq // rt

    def body(g_ref, v_ref, cwg_ref, cwv_ref, cbg_ref, cbv_ref, o_ref):
        cbg, cbv = cbg_ref[...], cbv_ref[...]

        def tile(c, carry):
            pg, pv = carry
            rows = pl.ds(pl.multiple_of(c * rt, rt), rt)
            xg, xv = g_ref[rows, :], v_ref[rows, :]
            _, gate = _ffn_conv(xg, pg, cwg_ref, cbg)
            _, val = _ffn_conv(xv, pv, cwv_ref, cbv)
            o_ref[rows, :] = ((gate * _sigmoid(gate)) * val).astype(BF16)
            return xg[rt - 8:], xv[rt - 8:]

        z = jnp.zeros((8, tc), F32)
        lax.fori_loop(0, nr, tile, (z, z))

    gcol = pl.BlockSpec((seq, tc), lambda j, b: (b, j))
    vcol = pl.BlockSpec((seq, tc), lambda j, b: (b, nj + j))
    gv = lambda r: pl.BlockSpec((r, tc), lambda j, b: (0, j))
    vv = lambda r: pl.BlockSpec((r, tc), lambda j, b: (0, nj + j))
    return pl.pallas_call(
        body, name=name, grid=(nj, nb),
        in_specs=[gcol, vcol, gv(3), vv(3), gv(1), vv(1)], out_specs=gcol,
        out_shape=jax.ShapeDtypeStruct((t, f), BF16),
        compiler_params=_cp(dimension_semantics=("parallel", "parallel")),
    )(up, up, cw, cw, cb, cb)


def _ffn_act_bwd(dact, up, cw, cb, *, nb, seq, name):
    t, f2 = up.shape
    f = f2 // 2
    tc = _tile(f, 256)
    nj = f // tc
    rt = _tile(seq, 256, 16)
    nr = seq // rt

    def body(da_ref, g_ref, v_ref, cwg_ref, cwv_ref, cbg_ref, cbv_ref,
             dg_ref, dv_ref, sg_ref, sv_ref, eg, ev, accg, accv):
        bi = pl.program_id(1)
        cbg, cbv = cbg_ref[...], cbv_ref[...]

        @pl.when(bi == 0)
        def _():
            accg[...] = jnp.zeros_like(accg)
            accv[...] = jnp.zeros_like(accv)

        colsum = lambda z: jnp.sum(z, axis=0, keepdims=True)

        def first(c, carry):
            pg, pv = carry
            rows = pl.ds(pl.multiple_of(c * rt, rt), rt)
            xg, xv = g_ref[rows, :], v_ref[rows, :]
            (g2, g1), gate = _ffn_conv(xg, pg, cwg_ref, cbg)
            (v2, v1), val = _ffn_conv(xv, pv, cwv_ref, cbv)
            s = _sigmoid(gate)
            da = da_ref[rows, :]
            dgate = da * val * _dsilu(gate, s)
            dval = da * (gate * s)
            eg[rows, :] = dgate
            ev[rows, :] = dval
            for acc, (s2, s1, s0), d in ((accg, (g2, g1, xg), dgate), (accv, (v2, v1, xv), dval)):
                acc[0:1, :] += colsum(s2 * d)
                acc[1:2, :] += colsum(s1 * d)
                acc[2:3, :] += colsum(s0 * d)
                acc[3:4, :] += colsum(d)
            return xg[rt - 8:], xv[rt - 8:]

        z = jnp.zeros((8, tc), F32)
        lax.fori_loop(0, nr, first, (z, z))

        def second(c, carry):
            base = pl.multiple_of(c * rt, rt)
            rows = pl.ds(base, rt)
            after = pl.ds(pl.multiple_of(jnp.minimum(base + rt, seq - 8), 8), 8)
            more = c < nr - 1
            for e, cw_ref, d_ref in ((eg, cwg_ref, dg_ref), (ev, cwv_ref, dv_ref)):
                d = e[rows, :]
                n8 = jnp.where(more, e[after, :], 0.0)
                d1, d2 = _shift_up(d, n8, 1), _shift_up(d, n8, 2)
                d_ref[rows, :] = (d * cw_ref[2:3, :] + d1 * cw_ref[1:2, :] + d2 * cw_ref[0:1, :]).astype(BF16)
            return carry

        lax.fori_loop(0, nr, second, 0)

        @pl.when(bi == nb - 1)
        def _():
            sg_ref[...] = accg[...]
            sv_ref[...] = accv[...]

    gcol = pl.BlockSpec((seq, tc), lambda j, b: (b, j))
    vcol = pl.BlockSpec((seq, tc), lambda j, b: (b, nj + j))
    gv = lambda r: pl.BlockSpec((r, tc), lambda j, b: (0, j))
    vv = lambda r: pl.BlockSpec((r, tc), lambda j, b: (0, nj + j))
    dshape = jax.ShapeDtypeStruct((t, f), BF16)
    sshape = jax.ShapeDtypeStruct((8, f), F32)
    return pl.pallas_call(
        body, name=name, grid=(nj, nb),
        in_specs=[gcol, gcol, vcol, gv(3), vv(3), gv(1), vv(1)],
        out_specs=[gcol, gcol, gv(8), gv(8)], out_shape=[dshape, dshape, sshape, sshape],
        scratch_shapes=[pltpu.VMEM((seq, tc), F32)] * 2 + [pltpu.VMEM((8, tc), F32)] * 2,
        compiler_params=_cp(dimension_semantics=("parallel", "arbitrary")),
    )(dact, up, up, cw, cw, cb, cb)


def _adamw_math(wv, g, mv, vv):
    m = ADAM_B1 * mv + (1.0 - ADAM_B1) * g
    v = ADAM_B2 * vv + (1.0 - ADAM_B2) * (g * g)
    m_hat = m / (1.0 - ADAM_B1 ** ADAM_STEP)
    v_hat = v / (1.0 - ADAM_B2 ** ADAM_STEP)
    delta = -ADAM_LR * (m_hat / (jnp.sqrt(v_hat) + ADAM_EPS) + ADAM_WD * wv)
    return delta, m, v


def _adamw_shard(wv, mv, vv, p_own, rb, *, name):
    r, cdim = wv.shape
    tr = _row_tile(r, cdim)

    def body(w_ref, m_ref, v_ref, p_ref, rb_ref, g_ref, d_ref, mo_ref, vo_ref):
        g = p_ref[...].astype(F32)
        for k in range(3):
            g = g + rb_ref[k].astype(F32)
        g_ref[...] = g
        d_ref[...], mo_ref[...], vo_ref[...] = _adamw_math(w_ref[...], g, m_ref[...], v_ref[...])

    row = pl.BlockSpec((tr, cdim), lambda i: (i, 0))
    shp = jax.ShapeDtypeStruct((r, cdim), F32)
    return pl.pallas_call(
        body, name=name, grid=(r // tr,),
        in_specs=[row, row, row, row, pl.BlockSpec((3, tr, cdim), lambda i: (0, i, 0))],
        out_specs=[row] * 4, out_shape=[shp] * 4, compiler_params=_cp(dimension_semantics=("parallel",)),
    )(wv, mv, vv, p_own, rb)


def _adamw_packed(wv, g, mv, vv, *, name):
    r = wv.shape[0]
    tr = _tile(r, 256, 8)

    def body(w_ref, g_ref, m_ref, v_ref, d_ref, mo_ref, vo_ref):
        d_ref[...], mo_ref[...], vo_ref[...] = _adamw_math(w_ref[...], g_ref[...], m_ref[...], v_ref[...])

    row = pl.BlockSpec((tr, LANE), lambda i: (i, 0))
    shp = jax.ShapeDtypeStruct((r, LANE), F32)
    return pl.pallas_call(
        body, name=name, grid=(r // tr,), in_specs=[row] * 4, out_specs=[row] * 3, out_shape=[shp] * 3,
        compiler_params=_cp(dimension_semantics=("parallel",)),
    )(wv, g, mv, vv)


def _place():
    return lax.axis_index("x"), lax.axis_index("y"), lax.axis_index("c")


def _all_gather(shards, *, name):
    n = len(shards)

    def body(*refs):
        ins, outs = refs[:n], refs[n:2 * n]
        send, recv, lsem = refs[2 * n:]
        x, y, c = _place()
        me, sibling = (x, y, c), (x, y, 1 - c)
        chips = [(1 - x, y), (x, 1 - y), (1 - x, 1 - y)]

        def slot(px, py, pc):
            return 4 * px + 2 * py + pc

        def copy(a, k, block, to, src=None):
            dst = outs[a].at[slot(*block)]
            return pltpu.make_async_remote_copy(
                src_ref=dst if src is None else src, dst_ref=dst, send_sem=send.at[a, k], recv_sem=recv.at[a, k],
                device_id=to, device_id_type=MESH)

        mine = [pltpu.make_async_copy(ins[a], outs[a].at[slot(*me)], lsem.at[a]) for a in range(n)]
        for cp in mine:
            cp.start()
        first = []
        for a in range(n):
            first.append(copy(a, 0, me, sibling, src=ins[a]))
            first += [copy(a, 1 + j, me, (*chip, c), src=ins[a]) for j, chip in enumerate(chips)]
        for cp in first:
            cp.start()
        passed = []
        for j, chip in enumerate(chips):
            for a in range(n):
                copy(a, 1 + j, (*chip, c), me).wait_recv()
                cp = copy(a, 4 + j, (*chip, c), sibling)
                cp.start()
                passed.append(cp)
        for a in range(n):
            copy(a, 0, sibling, me).wait_recv()
            for j, chip in enumerate(chips):
                copy(a, 4 + j, (*chip, 1 - c), me).wait_recv()
        for cp in first + passed:
            cp.wait_send()
        for cp in mine:
            cp.wait()

    return pl.pallas_call(
        body, name=name, in_specs=[ANY] * n, out_specs=[ANY] * n,
        out_shape=[jax.ShapeDtypeStruct((NDEV,) + s.shape, s.dtype) for s in shards],
        scratch_shapes=[pltpu.SemaphoreType.DMA((n, 7)), pltpu.SemaphoreType.DMA((n, 7)),
                        pltpu.SemaphoreType.DMA((n,))],
    )(*shards)


def _all_reduce_packed(p, *, name):
    r = p.shape[0]
    tr = _tile(r, 256, 8)

    def body(p_ref, o_ref, buf, send, recv, lsem):
        x, y, c = _place()
        me, sibling = (x, y, c), (x, y, 1 - c)
        chips = [(1 - x, y), (x, 1 - y), (1 - x, 1 - y)]

        def slot(px, py, pc):
            return 4 * px + 2 * py + pc

        def copy(k, block, to, src=None):
            dst = buf.at[slot(*block)]
            return pltpu.make_async_remote_copy(
                src_ref=dst if src is None else src, dst_ref=dst, send_sem=send.at[k], recv_sem=recv.at[k],
                device_id=to, device_id_type=MESH)

        mine = pltpu.make_async_copy(p_ref, buf.at[slot(*me)], lsem)
        mine.start()
        first = [copy(0, me, sibling, src=p_ref)]
        first += [copy(1 + j, me, (*chip, c), src=p_ref) for j, chip in enumerate(chips)]
        for cp in first:
            cp.start()
        passed = [copy(4 + j, (*chip, c), sibling) for j, chip in enumerate(chips)]
        for j, chip in enumerate(chips):
            copy(1 + j, (*chip, c), me).wait_recv()
            passed[j].start()
        copy(0, sibling, me).wait_recv()
        for j, chip in enumerate(chips):
            copy(4 + j, (*chip, 1 - c), me).wait_recv()
        for cp in first + passed:
            cp.wait_send()
        mine.wait()

        def add(i, carry):
            rows = pl.ds(pl.multiple_of(i * tr, tr), tr)
            s = buf[0, rows, :]
            for d in range(1, NDEV):
                s = s + buf[d, rows, :]
            o_ref[rows, :] = s
            return carry

        lax.fori_loop(0, r // tr, add, 0)

    vm = pl.BlockSpec(memory_space=pltpu.VMEM)
    return pl.pallas_call(
        body, name=name, in_specs=[vm], out_specs=vm, out_shape=jax.ShapeDtypeStruct(p.shape, p.dtype),
        scratch_shapes=[pltpu.VMEM((NDEV,) + p.shape, p.dtype), pltpu.SemaphoreType.DMA((7,)),
                        pltpu.SemaphoreType.DMA((7,)), pltpu.SemaphoreType.DMA],
        compiler_params=_cp(),
    )(p)


def _rs_sibling(gs, *, name):
    n = len(gs)

    def body(*refs):
        ins, outs = refs[:n], refs[n:2 * n]
        send, recv = refs[2 * n:]
        x, y, c = _place()
        copies = []
        for a in range(n):
            for j in range(4):
                copies.append(pltpu.make_async_remote_copy(
                    src_ref=ins[a].at[2 * j + (1 - c)], dst_ref=outs[a].at[j], send_sem=send.at[a, j],
                    recv_sem=recv.at[a, j], device_id=(x, y, 1 - c), device_id_type=MESH))
        for cp in copies:
            cp.start()
        for cp in copies:
            cp.wait()

    return pl.pallas_call(
        body, name=name, in_specs=[ANY] * n, out_specs=[ANY] * n,
        out_shape=[jax.ShapeDtypeStruct((4,) + g.shape[1:], g.dtype) for g in gs],
        scratch_shapes=[pltpu.SemaphoreType.DMA((n, 4)), pltpu.SemaphoreType.DMA((n, 4))],
    )(*gs)


def _rs_plane(ps, *, name):
    n = len(ps)

    def body(*refs):
        ins, outs = refs[:n], refs[n:2 * n]
        send, recv = refs[2 * n:]
        x, y, c = _place()
        chips = [(1 - x, y), (x, 1 - y), (1 - x, 1 - y)]
        copies = []
        for a in range(n):
            for k, (px, py) in enumerate(chips):
                copies.append(pltpu.make_async_remote_copy(
                    src_ref=ins[a].at[2 * px + py], dst_ref=outs[a].at[k], send_sem=send.at[a, k],
                    recv_sem=recv.at[a, k], device_id=(px, py, c), device_id_type=MESH))
        for cp in copies:
            cp.start()
        for cp in copies:
            cp.wait()

    return pl.pallas_call(
        body, name=name, in_specs=[ANY] * n, out_specs=[ANY] * n,
        out_shape=[jax.ShapeDtypeStruct((3,) + p.shape[1:], p.dtype) for p in ps],
        scratch_shapes=[pltpu.SemaphoreType.DMA((n, 3)), pltpu.SemaphoreType.DMA((n, 3))],
    )(*ps)


def _add_pairs(g, rcv, cidx, *, name):
    _, r, cdim = g.shape
    tr = _row_tile(r, cdim)

    def body(c_ref, g_ref, r_ref, o_ref):
        o_ref[...] = (g_ref[...].astype(F32) + r_ref[...].astype(F32)).astype(o_ref.dtype)

    grid_spec = pltpu.PrefetchScalarGridSpec(
        num_scalar_prefetch=1, grid=(4, r // tr),
        in_specs=[pl.BlockSpec((None, tr, cdim), lambda j, i, c_ref: (2 * j + c_ref[0], i, 0)),
                  pl.BlockSpec((None, tr, cdim), lambda j, i, c_ref: (j, i, 0))],
        out_specs=pl.BlockSpec((None, tr, cdim), lambda j, i, c_ref: (j, i, 0)))
    return pl.pallas_call(
        body, name=name, grid_spec=grid_spec, out_shape=jax.ShapeDtypeStruct((4, r, cdim), g.dtype),
        compiler_params=_cp(dimension_semantics=("parallel", "parallel")),
    )(cidx, g, rcv)


def _pack(arrs):
    flat = jnp.concatenate([a.reshape(-1).astype(F32) for a in arrs])
    n = flat.shape[0]
    rows = -(-n // LANE)
    rows = -(-rows // 256) * 256
    return jnp.pad(flat, (0, rows * LANE - n)).reshape(rows, LANE)


def _unpack(packed, shapes):
    flat = packed.reshape(-1)
    out, o = [], 0
    for s in shapes:
        n = math.prod(s)
        out.append(flat[o:o + n].reshape(s))
        o += n
    return out


def kernel(x, ln1_w, w_in, lb_gamma, hg_norm_w, lru_conv_w, lru_conv_b, lru_wa, lru_ba, lru_wx, lru_bx, lru_lambda, lru_norm_w, w_out, ln2_w, ffn_w_up, ffn_conv_w, ffn_conv_b, ffn_w_down, final_norm_w, loss_target, m_ln1_w, m_w_in, m_lb_gamma, m_hg_norm_w, m_lru_conv_w, m_lru_conv_b, m_lru_wa, m_lru_ba, m_lru_wx, m_lru_bx, m_lru_lambda, m_lru_norm_w, m_w_out, m_ln2_w, m_ffn_w_up, m_ffn_conv_w, m_ffn_conv_b, m_ffn_w_down, m_final_norm_w, v_ln1_w, v_w_in, v_lb_gamma, v_hg_norm_w, v_lru_conv_w, v_lru_conv_b, v_lru_wa, v_lru_ba, v_lru_wx, v_lru_bx, v_lru_lambda, v_lru_norm_w, v_w_out, v_ln2_w, v_ffn_w_up, v_ffn_conv_w, v_ffn_conv_b, v_ffn_w_down, v_final_norm_w):
    nb, seq, d = x.shape
    t = nb * seq
    wmix = d // 2
    in_sh = w_in.shape[2]
    up_sh = ffn_w_up.shape[2]
    up_pad = -(-up_sh // LANE) * LANE
    dn_sh = ffn_w_down.shape[1]
    fpad = 4 * up_pad
    cx, cy, cc = _place()
    me = 4 * cx + 2 * cy + cc
    plane = 2 * cx + cy

    win_s = w_in[0].astype(BF16)
    wout_s = w_out[0].astype(BF16)
    wup_s = jnp.pad(ffn_w_up[0], ((0, 0), (0, up_pad - up_sh))).astype(BF16)
    wdn_s = ffn_w_down[0].astype(BF16)
    fcw_s = jnp.pad(ffn_conv_w[0], ((0, 0), (0, up_pad - up_sh)))
    win_g, wout_g, wup_g, wdn_g, lcw_g, fcw_g = _all_gather(
        [win_s, wout_s, wup_s, wdn_s, lru_conv_w[0], fcw_s], name="gather_weights")
    wout_f = wout_g.reshape(1, d, d)
    wdn_f = jnp.pad(wdn_g.reshape(4, 2 * dn_sh, d), ((0, 0), (0, up_pad - up_sh), (0, 0))).reshape(1, fpad, d)
    lcw = lcw_g.transpose(1, 0, 2).reshape(lru_conv_w.shape[1], wmix)
    fcw = fcw_g.transpose(1, 0, 2).reshape(ffn_conv_w.shape[1], 2 * fpad)
    fcb = jnp.pad(ffn_conv_b.reshape(NDEV, up_sh), ((0, 0), (0, up_pad - up_sh))).reshape(1, 2 * fpad)
    wa_b, wx_b = lru_wa[0].astype(BF16), lru_wx[0].astype(BF16)

    xf = x.reshape(t, d)
    hn = _rms_fwd(xf, ln1_w, name="ln1_fwd")
    proj = _mm(hn, win_g, kind="nn", out_dtype=F32, name="in_proj")
    o_hg, o_pre, states = _hgrn_fwd(proj, lb_gamma, hg_norm_w, nb=nb, seq=seq, name="hgrn_fwd")
    p_lru, h_lru = _lru_fwd(proj, lcw, lru_conv_b, wa_b, lru_ba, wx_b, lru_bx, lru_lambda,
                            nb=nb, seq=seq, name="lru_fwd")
    o_lru = _rms_fwd(p_lru, lru_norm_w, name="lru_norm_fwd")
    mix = jnp.concatenate([o_hg, o_lru], axis=1)
    h1 = _mm(mix, wout_f, kind="nn", out_dtype=F32, name="out_proj", res=xf)
    hn2 = _rms_fwd(h1, ln2_w, name="ln2_fwd")
    up = _mm(hn2, wup_g, kind="nn", out_dtype=F32, name="ffn_up", tm_cap=512, tn_cap=up_pad, tk_cap=512)
    act = _ffn_act_fwd(up, fcw, fcb, nb=nb, seq=seq, name="ffn_act_fwd")
    h2 = _mm(act, wdn_f, kind="nn", out_dtype=F32, name="ffn_down", res=h1, tk_cap=up_pad // 2)

    dh2, dh2_b, g_fnw, loss_part = _loss_head(h2, final_norm_w.reshape(1, d), loss_target.reshape(t, d),
                                              name="loss_head")
    loss = lax.psum(loss_part[0, 0], ("x", "y", "c"))

    dact = _mm(dh2_b, wdn_f, kind="nt", out_dtype=F32, name="ffn_down_dx", tm_cap=512, tn_cap=up_pad)
    g_wdn = _mm(act, dh2_b, kind="tn", out_dtype=BF16, name="ffn_down_dw", tm_cap=up_pad // 2)[0]
    dup_g, dup_v, s_g, s_v = _ffn_act_bwd(dact, up, fcw, fcb, nb=nb, seq=seq, name="ffn_act_bwd")
    dup = jnp.concatenate([dup_g, dup_v], axis=1)
    dhn2 = _mm(dup, wup_g, kind="nt", out_dtype=F32, name="ffn_up_dx", tm_cap=512, tn_cap=1024, tk_cap=up_pad)
    g_wup = _mm(hn2, dup, kind="tn", out_dtype=BF16, name="ffn_up_dw", out_blocks=NDEV, tn_cap=up_pad, tm_cap=512)
    dh1, dh1_b, g_ln2 = _rms_bwd(dhn2, h1, ln2_w, name="ln2_bwd", extra=dh2, want_bf16=True)

    dmix = _mm(dh1_b, wout_f, kind="nt", out_dtype=F32, name="out_proj_dx")
    g_wout = _mm(mix, dh1_b, kind="tn", out_dtype=BF16, name="out_proj_dw")[0]
    dp_lru, g_lnw = _rms_bwd(dmix, p_lru, lru_norm_w, name="lru_norm_bwd", dy_cb=1)
    dxr, dyr, s_lru, g_wa, g_wx = _lru_bwd(proj, lcw, lru_conv_b, wa_b, lru_ba, wx_b, lru_bx, lru_lambda,
                                           h_lru, dp_lru, nb=nb, seq=seq, name="lru_bwd")
    dqr, dfr, dir_, dgr, s_hg = _hgrn_bwd(proj, lb_gamma, hg_norm_w, o_pre, states, dmix,
                                          nb=nb, seq=seq, name="hgrn_bwd")
    dproj = jnp.concatenate([dqr, dfr, dir_, dgr, dxr, dyr], axis=1)
    dhn = _mm(dproj, win_g, kind="nt", out_dtype=F32, name="in_proj_dx", tk_cap=768)
    g_win = _mm(hn, dproj, kind="tn", out_dtype=BF16, name="in_proj_dw", out_blocks=NDEV, tn_cap=in_sh)
    grad_x, g_ln1 = _rms_bwd(dhn, xf, ln1_w, name="ln1_bwd", extra=dh1)

    g_wout8 = g_wout.reshape(NDEV, d // NDEV, d)
    g_wdn8 = g_wdn.reshape(4, up_pad, d)[:, :2 * dn_sh].reshape(NDEV, dn_sh, d)
    big = [g_win, g_wout8, g_wup, g_wdn8]
    rcv = _rs_sibling(big, name="rs_sibling")
    cidx = jnp.reshape(cc, (1,)).astype(jnp.int32)
    sums = [_add_pairs(g, r_, cidx, name=f"rs_add_{i}") for i, (g, r_) in enumerate(zip(big, rcv))]
    rbs = _rs_plane(sums, name="rs_plane")

    def own(p, valid):
        return lax.dynamic_index_in_dim(p, plane, 0, keepdims=False)[:, :valid]

    upd_in = _adamw_shard(w_in[0], m_w_in[0], v_w_in[0], own(sums[0], in_sh), rbs[0], name="adamw_w_in")
    upd_out = _adamw_shard(w_out[0], m_w_out[0], v_w_out[0], own(sums[1], d), rbs[1], name="adamw_w_out")
    upd_up = _adamw_shard(ffn_w_up[0], m_ffn_w_up[0], v_ffn_w_up[0], own(sums[2], up_sh),
                          rbs[2][:, :, :up_sh], name="adamw_w_up")
    upd_dn = _adamw_shard(ffn_w_down[0], m_ffn_w_down[0], v_ffn_w_down[0], own(sums[3], d), rbs[3],
                          name="adamw_w_down")

    fcw_parts = jnp.concatenate([s_g[0:3], s_v[0:3]], axis=1).reshape(3, NDEV, up_pad)[:, :, :up_sh]
    fcb_part = jnp.concatenate([s_g[3:4], s_v[3:4]], axis=1).reshape(NDEV, up_pad)[:, :up_sh]
    small_parts = [g_ln1, s_hg[0:2], s_hg[2:3], s_lru[0:4], s_lru[4:5], g_wa, s_lru[5:6], g_wx, s_lru[6:7],
                   s_lru[7:8], g_lnw, g_ln2, fcw_parts, fcb_part, g_fnw]
    summed = _unpack(_all_reduce_packed(_pack(small_parts), name="allreduce_small"),
                     [p.shape for p in small_parts])
    (s_ln1, s_lbg, s_hgn, s_lcw, s_lcb, s_wa, s_ba, s_wx, s_bx, s_lam, s_lnw, s_ln2, s_fcw, s_fcb, s_fnw) = summed
    sh_lcw = lru_conv_w.shape[2]
    g_small = {
        "ln1_w": s_ln1, "lb_gamma": s_lbg, "hg_norm_w": s_hgn,
        "lru_conv_w": lax.dynamic_slice_in_dim(s_lcw, me * sh_lcw, sh_lcw, axis=1),
        "lru_conv_b": s_lcb, "lru_wa": s_wa, "lru_ba": s_ba, "lru_wx": s_wx, "lru_bx": s_bx,
        "lru_lambda": s_lam, "lru_norm_w": s_lnw, "ln2_w": s_ln2,
        "ffn_conv_w": lax.dynamic_index_in_dim(s_fcw, me, 1, keepdims=False),
        "ffn_conv_b": s_fcb, "final_norm_w": s_fnw,
    }
    w_small = {"ln1_w": ln1_w, "lb_gamma": lb_gamma, "hg_norm_w": hg_norm_w, "lru_conv_w": lru_conv_w,
               "lru_conv_b": lru_conv_b, "lru_wa": lru_wa, "lru_ba": lru_ba, "lru_wx": lru_wx, "lru_bx": lru_bx,
               "lru_lambda": lru_lambda, "lru_norm_w": lru_norm_w, "ln2_w": ln2_w, "ffn_conv_w": ffn_conv_w,
               "ffn_conv_b": ffn_conv_b, "final_norm_w": final_norm_w}
    m_small = {"ln1_w": m_ln1_w, "lb_gamma": m_lb_gamma, "hg_norm_w": m_hg_norm_w, "lru_conv_w": m_lru_conv_w,
               "lru_conv_b": m_lru_conv_b, "lru_wa": m_lru_wa, "lru_ba": m_lru_ba, "lru_wx": m_lru_wx,
               "lru_bx": m_lru_bx, "lru_lambda": m_lru_lambda, "lru_norm_w": m_lru_norm_w, "ln2_w": m_ln2_w,
               "ffn_conv_w": m_ffn_conv_w, "ffn_conv_b": m_ffn_conv_b, "final_norm_w": m_final_norm_w}
    v_small = {"ln1_w": v_ln1_w, "lb_gamma": v_lb_gamma, "hg_norm_w": v_hg_norm_w, "lru_conv_w": v_lru_conv_w,
               "lru_conv_b": v_lru_conv_b, "lru_wa": v_lru_wa, "lru_ba": v_lru_ba, "lru_wx": v_lru_wx,
               "lru_bx": v_lru_bx, "lru_lambda": v_lru_lambda, "lru_norm_w": v_lru_norm_w, "ln2_w": v_ln2_w,
               "ffn_conv_w": v_ffn_conv_w, "ffn_conv_b": v_ffn_conv_b, "final_norm_w": v_final_norm_w}
    names = list(w_small)
    shapes = [w_small[k].shape for k in names]
    g_small = {k: g_small[k].reshape(w_small[k].shape) for k in names}
    packed = _adamw_packed(_pack([w_small[k] for k in names]), _pack([g_small[k] for k in names]),
                           _pack([m_small[k] for k in names]), _pack([v_small[k] for k in names]),
                           name="adamw_small")
    d_small, nm_small, nv_small = ({k: a for k, a in zip(names, _unpack(pk, shapes))} for pk in packed)

    grads, deltas, new_m, new_v = dict(g_small), dict(d_small), dict(nm_small), dict(nv_small)
    for k, upd in (("w_in", upd_in), ("w_out", upd_out), ("ffn_w_up", upd_up), ("ffn_w_down", upd_dn)):
        grads[k], deltas[k], new_m[k], new_v[k] = (a[None] for a in upd)
    order = ["ln1_w", "w_in", "lb_gamma", "hg_norm_w", "lru_conv_w", "lru_conv_b", "lru_wa", "lru_ba", "lru_wx",
             "lru_bx", "lru_lambda", "lru_norm_w", "w_out", "ln2_w", "ffn_w_up", "ffn_conv_w", "ffn_conv_b",
             "ffn_w_down", "final_norm_w"]
    return (loss, grad_x.reshape(nb, seq, d), *[grads[k] for k in order], *[deltas[k] for k in order],
            *[new_m[k] for k in order], *[new_v[k] for k in order])
```

```python
import math

import jax
import jax.numpy as jnp
from jax import lax
from jax.experimental import pallas as pl
from jax.experimental.pallas import tpu as pltpu

F32, BF16 = jnp.float32, jnp.bfloat16
EPS = 1e-6
HEAD = 128
CHUNK = 64
SUB = 16
NSUB = CHUNK // SUB
LRU_C = 8.0
LANE = 128
NDEV = 8
ADAM_LR, ADAM_B1, ADAM_B2, ADAM_EPS, ADAM_WD, ADAM_STEP = 0.001, 0.9, 0.999, 1e-08, 0.01, 10
MESH = pl.DeviceIdType.MESH
ANY = pl.BlockSpec(memory_space=pl.ANY)
VMEM_LIMIT = 56 * 1024 * 1024


def _cp(**kw):
    return pltpu.CompilerParams(vmem_limit_bytes=VMEM_LIMIT, **kw)


def _tile(n, cap, mult=LANE):
    best = None
    for t in range(mult, min(n, cap) + 1, mult):
        if n % t == 0:
            best = t
    return best if best is not None else n


def _row_tile(r, cdim, budget=262144):
    return _tile(r, max(16, budget // cdim), 16)


def _sigmoid(x):
    return jax.nn.sigmoid(x)


def _dsilu(x, s):
    return s * (1.0 + x * (1.0 - s))


def _iota_rows(n, w=LANE):
    return lax.broadcasted_iota(jnp.int32, (n, w), 0)


def _shift_down(prev8, xt, k):
    cat = jnp.concatenate([prev8, xt], axis=0)
    return pltpu.roll(cat, k, 0)[8:]


def _shift_up(xt, next8, k):
    cat = jnp.concatenate([xt, next8], axis=0)
    n = cat.shape[0]
    return pltpu.roll(cat, n - k, 0)[: xt.shape[0]]


def _scan_fwd(a, u):
    n = a.shape[0]
    row = _iota_rows(n, a.shape[1])
    k = 1
    while k < n:
        keep = row >= k
        a_s = jnp.where(keep, pltpu.roll(a, k, 0), 1.0)
        u_s = jnp.where(keep, pltpu.roll(u, k, 0), 0.0)
        u = a * u_s + u
        a = a * a_s
        k *= 2
    return a, u


def _scan_bwd(a, u):
    n = a.shape[0]
    row = _iota_rows(n, a.shape[1])
    k = 1
    while k < n:
        keep = row < n - k
        a_s = jnp.where(keep, pltpu.roll(a, n - k, 0), 1.0)
        u_s = jnp.where(keep, pltpu.roll(u, n - k, 0), 0.0)
        u = a * u_s + u
        a = a * a_s
        k *= 2
    return a, u


def _cumsum_fwd(u):
    n = u.shape[0]
    row = _iota_rows(n, u.shape[1])
    k = 1
    while k < n:
        u = u + jnp.where(row >= k, pltpu.roll(u, k, 0), 0.0)
        k *= 2
    return u


def _cumsum_bwd(u):
    n = u.shape[0]
    row = _iota_rows(n, u.shape[1])
    k = 1
    while k < n:
        u = u + jnp.where(row < n - k, pltpu.roll(u, n - k, 0), 0.0)
        k *= 2
    return u


def _dot(a, b, dims):
    return lax.dot_general(a.astype(BF16), b.astype(BF16), (dims, ((), ())), preferred_element_type=F32)


NN = ((1,), (0,))
NT = ((1,), (1,))
TN = ((0,), (0,))


def _pcall(body, *, n_in, in_specs, args, deps=(), **kw):
    nd = len(deps)
    if nd:
        inner = body

        def body(*refs):
            return inner(*refs[:n_in], *refs[n_in + nd:])

        in_specs = list(in_specs) + [ANY] * nd
        args = list(args) + list(deps)
    return pl.pallas_call(body, in_specs=in_specs, **kw)(*args)


def _mm(a, b3, *, kind, out_dtype, name, res=None, tm_cap=1024, tn_cap=1536, tk_cap=1024, out_blocks=1, deps=()):
    if kind == "nn":
        m, kdim = a.shape
        nb, _, nsh = b3.shape
        n = nb * nsh
        tm, tn, tk = _tile(m, tm_cap), _tile(nsh, tn_cap), _tile(kdim, tk_cap)
        per = nsh // tn
        a_spec = pl.BlockSpec((tm, tk), lambda i, j, k: (i, k))
        b_spec = pl.BlockSpec((None, tk, tn), lambda i, j, k: (j // per, k, j % per))
        dims = NN
    elif kind == "nt":
        m, kdim = a.shape
        nb, n, ksh = b3.shape
        tm, tn, tk = _tile(m, tm_cap), _tile(n, tn_cap), _tile(ksh, tk_cap)
        per = ksh // tk
        a_spec = pl.BlockSpec((tm, tk), lambda i, j, k: (i, k))
        b_spec = pl.BlockSpec((None, tn, tk), lambda i, j, k: (k // per, j, k % per))
        dims = NT
    else:
        kdim, m = a.shape
        n = b3.shape[1]
        nsh = n // out_blocks
        tm, tn, tk = _tile(m, tm_cap), _tile(nsh, tn_cap), _tile(kdim, tk_cap)
        per = nsh // tn
        a_spec = pl.BlockSpec((tk, tm), lambda i, j, k: (k, i))
        b_spec = pl.BlockSpec((tk, tn), lambda i, j, k: (k, j))
        dims = TN
    nk = kdim // tk
    grid = (m // tm, n // tn, nk)

    def body(*refs):
        if res is None:
            a_ref, b_ref, o_ref, acc = refs
        else:
            a_ref, b_ref, r_ref, o_ref, acc = refs
        k = pl.program_id(2)

        @pl.when(k == 0)
        def _():
            acc[...] = jnp.zeros_like(acc)

        acc[...] += lax.dot_general(a_ref[...], b_ref[...], (dims, ((), ())), preferred_element_type=F32)

        @pl.when(k == nk - 1)
        def _():
            r = acc[...]
            if res is not None:
                r = r + r_ref[...]
            o_ref[...] = r.astype(o_ref.dtype)

    in_specs = [a_spec, b_spec]
    args = [a, b3]
    if res is not None:
        in_specs.append(pl.BlockSpec((tm, tn), lambda i, j, k: (i, j)))
        args.append(res)
    if kind == "tn":
        out_shape = jax.ShapeDtypeStruct((out_blocks, m, nsh), out_dtype)
        out_spec = pl.BlockSpec((None, tm, tn), lambda i, j, k: (j // per, i, j % per))
    else:
        out_shape = jax.ShapeDtypeStruct((m, n), out_dtype)
        out_spec = pl.BlockSpec((tm, tn), lambda i, j, k: (i, j))
    return _pcall(
        body, n_in=len(args), in_specs=in_specs, args=args, deps=deps,
        name=name, grid=grid, out_specs=out_spec, out_shape=out_shape,
        scratch_shapes=[pltpu.VMEM((tm, tn), F32)],
        compiler_params=_cp(dimension_semantics=("parallel", "parallel", "arbitrary")))


def _rms_fwd(x, w, *, name, tm=256, deps=()):
    t, d = x.shape
    tm = _tile(t, tm, 16)

    def body(x_ref, w_ref, o_ref):
        xv = x_ref[...]
        r = lax.rsqrt(jnp.mean(xv * xv, axis=-1, keepdims=True) + EPS)
        o_ref[...] = ((xv * r) * w_ref[...]).astype(o_ref.dtype)

    return _pcall(
        body, n_in=2, args=[x, w], deps=deps, name=name, grid=(t // tm,),
        in_specs=[pl.BlockSpec((tm, d), lambda i: (i, 0)), pl.BlockSpec((1, d), lambda i: (0, 0))],
        out_specs=pl.BlockSpec((tm, d), lambda i: (i, 0)),
        out_shape=jax.ShapeDtypeStruct((t, d), BF16), compiler_params=_cp())


def _rms_bwd(dy, x, w, *, name, extra=None, dy_cb=0, want_bf16=False, tm=256, deps=()):
    t, d = x.shape
    tm = _tile(t, tm, 16)

    def body(*refs):
        refs = list(refs)
        dy_ref, x_ref, w_ref = refs[:3]
        e_ref = refs[3] if extra is not None else None
        outs = refs[4:] if extra is not None else refs[3:]
        dx_ref = outs[0]
        dxb_ref = outs[1] if want_bf16 else None
        dw_ref = outs[-1]
        i = pl.program_id(0)
        xv = x_ref[...]
        r = lax.rsqrt(jnp.mean(xv * xv, axis=-1, keepdims=True) + EPS)
        nh = xv * r
        dyv = dy_ref[...]
        dn = dyv * w_ref[...]
        dx = r * (dn - nh * jnp.mean(dn * nh, axis=-1, keepdims=True))
        if extra is not None:
            dx = dx + e_ref[...]
        dx_ref[...] = dx
        if want_bf16:
            dxb_ref[...] = dx.astype(BF16)
        part = jnp.sum(dyv * nh, axis=0, keepdims=True)

        @pl.when(i == 0)
        def _():
            dw_ref[...] = part

        @pl.when(i > 0)
        def _():
            dw_ref[...] += part

    row = pl.BlockSpec((tm, d), lambda i: (i, 0))
    in_specs = [pl.BlockSpec((tm, d), lambda i: (i, dy_cb)), row, pl.BlockSpec((1, d), lambda i: (0, 0))]
    args = [dy, x, w]
    if extra is not None:
        in_specs.append(row)
        args.append(extra)
    out_shape = [jax.ShapeDtypeStruct((t, d), F32)]
    out_specs = [row]
    if want_bf16:
        out_shape.append(jax.ShapeDtypeStruct((t, d), BF16))
        out_specs.append(row)
    out_shape.append(jax.ShapeDtypeStruct((1, d), F32))
    out_specs.append(pl.BlockSpec((1, d), lambda i: (0, 0)))
    return _pcall(
        body, n_in=len(args), in_specs=in_specs, args=args, deps=deps,
        name=name, grid=(t // tm,), out_specs=out_specs, out_shape=out_shape,
        compiler_params=_cp(dimension_semantics=("arbitrary",)))


def _loss_head(h, w, tgt, *, name, tm=256):
    t, d = h.shape
    tm = _tile(t, tm, 16)

    def body(h_ref, w_ref, t_ref, dh_ref, dhb_ref, dw_ref, loss_ref):
        i = pl.program_id(0)
        xv = h_ref[...]
        wv = w_ref[...]
        r = lax.rsqrt(jnp.mean(xv * xv, axis=-1, keepdims=True) + EPS)
        nh = xv * r
        e = nh * wv - t_ref[...]
        part_loss = jnp.full((1, LANE), 0.5 * jnp.sum(jnp.mean(e * e, axis=-1, keepdims=True)), F32)
        dyv = e * (1.0 / d)
        dn = dyv * wv
        dx = r * (dn - nh * jnp.mean(dn * nh, axis=-1, keepdims=True))
        dh_ref[...] = dx
        dhb_ref[...] = dx.astype(BF16)
        part = jnp.sum(dyv * nh, axis=0, keepdims=True)

        @pl.when(i == 0)
        def _():
            dw_ref[...] = part
            loss_ref[...] = part_loss

        @pl.when(i > 0)
        def _():
            dw_ref[...] += part
            loss_ref[...] += part_loss

    row = pl.BlockSpec((tm, d), lambda i: (i, 0))
    vec = pl.BlockSpec((1, d), lambda i: (0, 0))
    return pl.pallas_call(
        body, name=name, grid=(t // tm,), in_specs=[row, vec, row],
        out_specs=[row, row, vec, pl.BlockSpec((1, LANE), lambda i: (0, 0))],
        out_shape=[jax.ShapeDtypeStruct((t, d), F32), jax.ShapeDtypeStruct((t, d), BF16),
                   jax.ShapeDtypeStruct((1, d), F32), jax.ShapeDtypeStruct((1, LANE), F32)],
        compiler_params=_cp(dimension_semantics=("arbitrary",)),
    )(h, w, tgt)


def _lower_bound(lbg_ref):
    g0, g1 = lbg_ref[0:1, :], lbg_ref[1:2, :]
    m = jnp.maximum(g0, g1)
    e0, e1 = jnp.exp(g0 - m), jnp.exp(g1 - m)
    return e0 / (e0 + e1)


def _seg_bounds():
    offs, o = {}, 0
    for i in range(1, NSUB):
        offs[i] = (o, o + SUB * i)
        o += SUB * i
    return offs, o


def _pad_rows(x, n):
    if x.shape[0] == n:
        return x
    return jnp.concatenate([x, jnp.zeros((n - x.shape[0], x.shape[1]), x.dtype)], axis=0)


def _offdiag_setup(q, k, b, v, b_c):
    offs, total = _seg_bounds()
    padded = -(-total // LANE) * LANE
    eq_parts = [jnp.zeros((SUB, HEAD), F32)]
    ek_parts, k_parts, v_parts = [], [], []
    for i in range(1, NSUB):
        r_i = b_c[SUB * i - 1:SUB * i, :]
        eq_parts.append(jnp.exp(b[SUB * i:SUB * (i + 1)] - r_i))
        ek_parts.append(jnp.exp(r_i - b[0:SUB * i]))
        k_parts.append(k[0:SUB * i])
        v_parts.append(v[0:SUB * i])
    eq = jnp.concatenate(eq_parts, axis=0)
    ek = _pad_rows(jnp.concatenate(ek_parts, axis=0), padded)
    kt = _pad_rows(jnp.concatenate(k_parts, axis=0), padded) * ek
    vs = _pad_rows(jnp.concatenate(v_parts, axis=0), padded)
    qt = q * eq
    rsub = lax.broadcasted_iota(jnp.int32, (CHUNK, padded), 0) // SUB
    col = lax.broadcasted_iota(jnp.int32, (CHUNK, padded), 1)
    cseg = jnp.zeros((CHUNK, padded), jnp.int32)
    for i in range(1, NSUB):
        cseg = cseg + (col >= offs[i][0]).astype(jnp.int32)
    mask = (rsub == cseg) & (col < total)
    a = jnp.where(mask, _dot(qt, kt, NT), 0.0)
    return offs, eq, ek, kt, vs, qt, mask, a


def _hgrn_fwd(proj, lbg, nw, *, nb, seq, name):
    t = proj.shape[0]
    w = lbg.shape[1]
    nh = w // HEAD
    nc = seq // CHUNK

    def body(q_ref, f_ref, i_ref, g_ref, lbg_ref, nw_ref, ohg_ref, opre_ref, st_ref, k_c, b_c, v_c, st):
        lb = _lower_bound(lbg_ref)
        nwv = nw_ref[...]
        st[...] = jnp.zeros_like(st)

        def chunk(c, carry):
            rows = pl.ds(pl.multiple_of(c * CHUNK, CHUNK), CHUNK)
            qr = q_ref[rows, :]
            q = qr * _sigmoid(qr)
            f = lb + (1.0 - lb) * _sigmoid(f_ref[rows, :])
            k = 1.0 - f
            b = _cumsum_fwd(jnp.log(f))
            v = i_ref[rows, :]
            k_c[...] = k
            b_c[...] = b
            v_c[...] = v
            s_t = st[...]
            st_ref[c] = s_t
            o = _dot(q * jnp.exp(b), s_t, NT)
            _, _, _, _, vs, _, _, a = _offdiag_setup(q, k, b, v, b_c)
            o = o + _dot(a, vs, NN)
            diag = []
            for i in range(NSUB):
                accs = [jnp.zeros((8, HEAD), F32) for _ in range(SUB // 8)]
                for j in range(SUB):
                    r = SUB * i + j
                    bj, kj, vj = b_c[r:r + 1, :], k_c[r:r + 1, :], v_c[r:r + 1, :]
                    for p in range(j // 8, SUB // 8):
                        lo = SUB * i + 8 * p
                        d = jnp.exp(b[lo:lo + 8] - bj)
                        if 8 * p < j:
                            d = jnp.where(_iota_rows(8) + 8 * p >= j, d, 0.0)
                        s = jnp.sum(q[lo:lo + 8] * d * kj, axis=-1, keepdims=True)
                        accs[p] = accs[p] + s * vj
                diag.extend(accs)
            o = o + jnp.concatenate(diag, axis=0)
            bl = b_c[CHUNK - 1:CHUNK, :]
            kb = k * jnp.exp(bl - b)
            st[...] = s_t * jnp.exp(bl) + _dot(v, kb, TN)
            opre_ref[rows, :] = o
            rn = lax.rsqrt(jnp.mean(o * o, axis=-1, keepdims=True) + EPS)
            gr = g_ref[rows, :]
            ohg_ref[rows, :] = (((o * rn) * nwv) * (gr * _sigmoid(gr))).astype(BF16)
            return carry

        lax.fori_loop(0, nc, chunk, 0)

    def col(off):
        return pl.BlockSpec((seq, HEAD), lambda h, b: (b, off * nh + h))

    vec = lambda r: pl.BlockSpec((r, HEAD), lambda h, b: (0, h))
    out_blk = pl.BlockSpec((seq, HEAD), lambda h, b: (b, h))
    return pl.pallas_call(
        body, name=name, grid=(nh, nb),
        in_specs=[col(0), col(1), col(2), col(3), vec(2), vec(1)],
        out_specs=[out_blk, out_blk, pl.BlockSpec((None, None, nc, HEAD, HEAD), lambda h, b: (b, h, 0, 0, 0))],
        out_shape=[jax.ShapeDtypeStruct((t, w), BF16), jax.ShapeDtypeStruct((t, w), F32),
                   jax.ShapeDtypeStruct((nb, nh, nc, HEAD, HEAD), F32)],
        scratch_shapes=[pltpu.VMEM((CHUNK, HEAD), F32)] * 3 + [pltpu.VMEM((HEAD, HEAD), F32)],
        compiler_params=_cp(dimension_semantics=("parallel", "parallel")),
    )(proj, proj, proj, proj, lbg, nw)


def _hgrn_bwd(proj, lbg, nw, opre, states, dmix, *, nb, seq, name, deps=()):
    t = proj.shape[0]
    w = lbg.shape[1]
    nh = w // HEAD
    nc = seq // CHUNK

    def body(q_ref, f_ref, i_ref, g_ref, lbg_ref, nw_ref, opre_ref, st_ref, dm_ref,
             dq_ref, df_ref, di_ref, dg_ref, small_ref, k_c, b_c, v_c, dst, dlb_s, dwn_s):
        bi = pl.program_id(1)
        lb = _lower_bound(lbg_ref)
        nwv = nw_ref[...]
        dst[...] = jnp.zeros_like(dst)

        @pl.when(bi == 0)
        def _():
            dlb_s[...] = jnp.zeros_like(dlb_s)
            dwn_s[...] = jnp.zeros_like(dwn_s)

        def chunk(it, carry):
            c = nc - 1 - it
            rows = pl.ds(pl.multiple_of(c * CHUNK, CHUNK), CHUNK)
            qr = q_ref[rows, :]
            sq = _sigmoid(qr)
            q = qr * sq
            sg = _sigmoid(f_ref[rows, :])
            f = lb + (1.0 - lb) * sg
            k = 1.0 - f
            b = _cumsum_fwd(jnp.log(f))
            v = i_ref[rows, :]
            k_c[...] = k
            b_c[...] = b
            v_c[...] = v
            o = opre_ref[rows, :]
            rn = lax.rsqrt(jnp.mean(o * o, axis=-1, keepdims=True) + EPS)
            nhat = o * rn
            dm = dm_ref[rows, :]
            gr = g_ref[rows, :]
            sgr = _sigmoid(gr)
            dnw = dm * (gr * sgr)
            dg_ref[rows, :] = (dm * (nhat * nwv) * _dsilu(gr, sgr)).astype(BF16)
            dwn_s[...] += jnp.sum(dnw * nhat, axis=0, keepdims=True)
            dn = dnw * nwv
            do = rn * (dn - nhat * jnp.mean(dn * nhat, axis=-1, keepdims=True))
            s_t = st_ref[c]
            ds = dst[...]
            eb = jnp.exp(b)
            qb = q * eb
            bl = b_c[CHUNK - 1:CHUNK, :]
            ebl = jnp.exp(bl)
            kdec = jnp.exp(bl - b)
            kb = k * kdec
            dqb = _dot(do, s_t, NN)
            dkb = _dot(v, ds, NN)
            dv = _dot(kb, ds, NT)
            d_ebl = jnp.sum(ds * s_t, axis=0, keepdims=True)
            dst[...] = ds * ebl + _dot(do, qb, TN)
            dq = dqb * eb
            dk = dkb * kdec
            t_kb = dkb * kb
            db = dqb * qb - t_kb
            db_last = jnp.sum(t_kb, axis=0, keepdims=True) + d_ebl * ebl
            offs, eq, ek, kt, vs, qt, mask, a = _offdiag_setup(q, k, b, v, b_c)
            da = jnp.where(mask, _dot(do, vs, NT), 0.0)
            dvs = _dot(a, do, TN)
            dqt = _dot(da, kt, NN)
            dkt = _dot(da, qt, TN)
            dq = dq + dqt * eq
            db = db + dqt * qt
            for i in range(1, NSUB):
                lo, hi = offs[i]
                dk = dk + _pad_rows(dkt[lo:hi] * ek[lo:hi], CHUNK)
                db = db - _pad_rows(dkt[lo:hi] * kt[lo:hi], CHUNK)
                dv = dv + _pad_rows(dvs[lo:hi], CHUNK)
            npv = SUB // 8
            dq_d, dk_d, dv_d, db_d = [], [], [], []
            for i in range(NSUB):
                aq = [jnp.zeros((8, HEAD), F32) for _ in range(npv)]
                ak = [jnp.zeros((8, HEAD), F32) for _ in range(npv)]
                av = [jnp.zeros((8, HEAD), F32) for _ in range(npv)]
                ab = [jnp.zeros((8, HEAD), F32) for _ in range(npv)]
                for j in range(SUB):
                    r = SUB * i + j
                    bj, kj, vj = b_c[r:r + 1, :], k_c[r:r + 1, :], v_c[r:r + 1, :]
                    sk = jnp.zeros((1, HEAD), F32)
                    sv = jnp.zeros((1, HEAD), F32)
                    sb = jnp.zeros((1, HEAD), F32)
                    for p in range(j // 8, npv):
                        lo = SUB * i + 8 * p
                        d = jnp.exp(b[lo:lo + 8] - bj)
                        if 8 * p < j:
                            d = jnp.where(_iota_rows(8) + 8 * p >= j, d, 0.0)
                        qd = q[lo:lo + 8] * d
                        dop = do[lo:lo + 8]
                        a_j = jnp.sum(qd * kj, axis=-1, keepdims=True)
                        da_j = jnp.sum(dop * vj, axis=-1, keepdims=True)
                        aq[p] = aq[p] + (da_j * d) * kj
                        pm = da_j * qd
                        rk = pm * kj
                        ab[p] = ab[p] + rk
                        sk = sk + jnp.sum(pm, axis=0, keepdims=True)
                        sb = sb + jnp.sum(rk, axis=0, keepdims=True)
                        sv = sv + jnp.sum(a_j * dop, axis=0, keepdims=True)
                    pj = j // 8
                    here = _iota_rows(8) == (j - 8 * pj)
                    ak[pj] = ak[pj] + jnp.where(here, sk, 0.0)
                    av[pj] = av[pj] + jnp.where(here, sv, 0.0)
                    ab[pj] = ab[pj] - jnp.where(here, sb, 0.0)
                dq_d.extend(aq)
                dk_d.extend(ak)
                dv_d.extend(av)
                db_d.extend(ab)
            dq = dq + jnp.concatenate(dq_d, axis=0)
            dk = dk + jnp.concatenate(dk_d, axis=0)
            dv = dv + jnp.concatenate(dv_d, axis=0)
            db = db + jnp.concatenate(db_d, axis=0)
            db = db + jnp.where(_iota_rows(CHUNK) == CHUNK - 1, db_last, 0.0)
            dgl = _cumsum_bwd(db)
            dfv = dgl / f - dk
            dlb_s[...] += jnp.sum(dfv * (1.0 - sg), axis=0, keepdims=True)
            df_ref[rows, :] = (dfv * (1.0 - lb) * (sg * (1.0 - sg))).astype(BF16)
            dq_ref[rows, :] = (dq * _dsilu(qr, sq)).astype(BF16)
            di_ref[rows, :] = dv.astype(BF16)
            return carry

        lax.fori_loop(0, nc, chunk, 0)

        @pl.when(bi == nb - 1)
        def _():
            dgam = dlb_s[...] * lb * (1.0 - lb)
            small_ref[...] = jnp.zeros_like(small_ref)
            small_ref[0:1, :] = dgam
            small_ref[1:2, :] = -dgam
            small_ref[2:3, :] = dwn_s[...]

    def col(off):
        return pl.BlockSpec((seq, HEAD), lambda h, b: (b, off * nh + h))

    vec = lambda r: pl.BlockSpec((r, HEAD), lambda h, b: (0, h))
    blk = pl.BlockSpec((seq, HEAD), lambda h, b: (b, h))
    dshape = jax.ShapeDtypeStruct((t, w), BF16)
    return _pcall(
        body, n_in=9, args=[proj, proj, proj, proj, lbg, nw, opre, states, dmix], deps=deps,
        name=name, grid=(nh, nb),
        in_specs=[col(0), col(1), col(2), col(3), vec(2), vec(1), blk,
                  pl.BlockSpec((None, None, nc, HEAD, HEAD), lambda h, b: (b, h, 0, 0, 0)), blk],
        out_specs=[blk, blk, blk, blk, vec(8)],
        out_shape=[dshape, dshape, dshape, dshape, jax.ShapeDtypeStruct((8, w), F32)],
        scratch_shapes=[pltpu.VMEM((CHUNK, HEAD), F32)] * 3 + [pltpu.VMEM((HEAD, HEAD), F32)]
        + [pltpu.VMEM((1, HEAD), F32)] * 2,
        compiler_params=_cp(dimension_semantics=("parallel", "arbitrary")))


def _expm1(x):
    poly = x * (1.0 + x * (0.5 + x * (1.0 / 6 + x * (1.0 / 24 + x * (1.0 / 120 + x * (1.0 / 720))))))
    return jnp.where(jnp.abs(x) < 0.25, poly, jnp.exp(x) - 1.0)


def _softplus_neg(lam):
    x = -lam
    e = jnp.exp(-jnp.abs(x))
    u = 1.0 + e
    l1p = jnp.where(u == 1.0, e, jnp.log(u) * (e / jnp.where(u == 1.0, 1.0, u - 1.0)))
    return jnp.maximum(x, 0.0) + l1p


_GELU_C = math.sqrt(2.0 / math.pi)


def _gelu(y):
    return 0.5 * y * (1.0 + jnp.tanh(_GELU_C * (y + 0.044715 * (y * y * y))))


def _dgelu(y):
    th = jnp.tanh(_GELU_C * (y + 0.044715 * (y * y * y)))
    return 0.5 * (1.0 + th) + 0.5 * y * (1.0 - th * th) * (_GELU_C * (1.0 + 3 * 0.044715 * (y * y)))


def _lru_gates(xr, prev8, cw_ref, cb, wa_ref, ba, wx_ref, bx, sp, first):
    x3, x2, x1 = _shift_down(prev8, xr, 3), _shift_down(prev8, xr, 2), _shift_down(prev8, xr, 1)
    xb = cb + x3 * cw_ref[0:1, :]
    xb = xb + x2 * cw_ref[1:2, :]
    xb = xb + x1 * cw_ref[2:3, :]
    xb = xb + xr * cw_ref[3:4, :]
    r = _sigmoid(_dot(xb, wa_ref[...], NN) + ba)
    ig = _sigmoid(_dot(xb, wx_ref[...], NN) + bx)
    la = (-LRU_C * r) * sp
    a = jnp.exp(la)
    start = jnp.logical_and(first, _iota_rows(xr.shape[0]) == 0)
    mult = jnp.where(start, 1.0, jnp.sqrt(-_expm1(2.0 * la)))
    return (x3, x2, x1), xb, r, ig, a, mult, start


def _lru_fwd(proj, cw, cb, wa, ba, wx, bx, lam, *, nb, seq, name, deps=()):
    t = proj.shape[0]
    w = cb.shape[1]
    nblk = w // HEAD
    nc = seq // CHUNK

    def body(x_ref, y_ref, cw_ref, cb_ref, wa_ref, ba_ref, wx_ref, bx_ref, lam_ref, p_ref, h_ref):
        sp = _softplus_neg(lam_ref[...])
        cb_v, ba_v, bx_v = cb_ref[...], ba_ref[...], bx_ref[...]

        def tile(c, carry):
            hc, prev8 = carry
            base = pl.multiple_of(c * CHUNK, CHUNK)
            rows = pl.ds(base, CHUNK)
            xr = x_ref[rows, :]
            _, xb, _, ig, a, mult, _ = _lru_gates(xr, prev8, cw_ref, cb_v, wa_ref, ba_v, wx_ref, bx_v, sp, c == 0)
            ap, up = _scan_fwd(a, xb * ig * mult)
            h = up + ap * hc
            h_ref[rows, :] = h
            p_ref[rows, :] = h * _gelu(y_ref[rows, :])
            return h_ref[pl.ds(base + CHUNK - 1, 1), :], xr[CHUNK - 8:]

        lax.fori_loop(0, nc, tile, (jnp.zeros((1, HEAD), F32), jnp.zeros((8, HEAD), F32)))

    def col(off):
        return pl.BlockSpec((seq, HEAD), lambda n, b: (b, off * nblk + n))

    vec = lambda r: pl.BlockSpec((r, HEAD), lambda n, b: (0, n))
    mat = pl.BlockSpec((None, HEAD, HEAD), lambda n, b: (n, 0, 0))
    blk = pl.BlockSpec((seq, HEAD), lambda n, b: (b, n))
    return _pcall(
        body, n_in=9, args=[proj, proj, cw, cb, wa, ba, wx, bx, lam], deps=deps, name=name, grid=(nblk, nb),
        in_specs=[col(4), col(5), vec(4), vec(1), mat, vec(1), mat, vec(1), vec(1)],
        out_specs=[blk, blk],
        out_shape=[jax.ShapeDtypeStruct((t, w), F32), jax.ShapeDtypeStruct((t, w), F32)],
        compiler_params=_cp(dimension_semantics=("parallel", "parallel")))


def _lru_bwd(proj, cw, cb, wa, ba, wx, bx, lam, hsv, dp, *, nb, seq, name):
    t = proj.shape[0]
    w = cb.shape[1]
    nblk = w // HEAD
    nc = seq // CHUNK

    def body(x_ref, y_ref, cw_ref, cb_ref, wa_ref, ba_ref, wx_ref, bx_ref, lam_ref, h_ref, dp_ref,
             dx_ref, dy_ref, small_ref, dwa_ref, dwx_ref, a_c, g_c, acc, dwa_s, dwx_s):
        bi = pl.program_id(1)
        lamv = lam_ref[...]
        sp = _softplus_neg(lamv)
        cb_v, ba_v, bx_v = cb_ref[...], ba_ref[...], bx_ref[...]

        @pl.when(bi == 0)
        def _():
            acc[...] = jnp.zeros_like(acc)
            dwa_s[...] = jnp.zeros_like(dwa_s)
            dwx_s[...] = jnp.zeros_like(dwx_s)

        def tile(it, carry):
            g_next, a_next, dxb_next8 = carry
            c = nc - 1 - it
            base = pl.multiple_of(c * CHUNK, CHUNK)
            rows = pl.ds(base, CHUNK)
            before = pl.ds(pl.multiple_of(jnp.maximum(base - 8, 0), 8), 8)
            inner = c > 0
            xr = x_ref[rows, :]
            prev8 = jnp.where(inner, x_ref[before, :], 0.0)
            (x3, x2, x1), xb, r, ig, a, mult, start = _lru_gates(
                xr, prev8, cw_ref, cb_v, wa_ref, ba_v, wx_ref, bx_v, sp, c == 0)
            h = h_ref[rows, :]
            h_m1 = _shift_down(jnp.where(inner, h_ref[before, :], 0.0), h, 1)
            yv = y_ref[rows, :]
            dpv = dp_ref[rows, :]
            dy_ref[rows, :] = (dpv * h * _dgelu(yv)).astype(BF16)
            dh = dpv * _gelu(yv)
            a_up = _shift_up(a, jnp.broadcast_to(a_next, (8, HEAD)), 1)
            ap, gp = _scan_bwd(a_up, dh)
            g = gp + ap * g_next
            a_c[...] = a
            g_c[...] = g
            da = g * h_m1
            gx = g * xb
            dxb = g * ig * mult
            dig = gx * mult
            dmult = jnp.where(start, 0.0, gx * ig)
            dla = da * a - dmult * (a * a) / mult
            dzr = (dla * (-LRU_C * sp)) * (r * (1.0 - r))
            dzi = dig * (ig * (1.0 - ig))
            dxb = dxb + _dot(dzr, wa_ref[...], NT) + _dot(dzi, wx_ref[...], NT)
            dwa_s[...] += _dot(xb, dzr, TN)
            dwx_s[...] += _dot(xb, dzi, TN)
            d1, d2, d3 = (_shift_up(dxb, dxb_next8, s) for s in (1, 2, 3))
            dx = dxb * cw_ref[3:4, :] + d1 * cw_ref[2:3, :] + d2 * cw_ref[1:2, :] + d3 * cw_ref[0:1, :]
            dx_ref[rows, :] = dx.astype(BF16)
            colsum = lambda z: jnp.sum(z, axis=0, keepdims=True)
            acc[0:1, :] += colsum(x3 * dxb)
            acc[1:2, :] += colsum(x2 * dxb)
            acc[2:3, :] += colsum(x1 * dxb)
            acc[3:4, :] += colsum(xr * dxb)
            acc[4:5, :] += colsum(dxb)
            acc[5:6, :] += colsum(dzr)
            acc[6:7, :] += colsum(dzi)
            acc[7:8, :] += colsum(dla * (-LRU_C * r))
            return g_c[0:1, :], a_c[0:1, :], dxb[0:8]

        zero = jnp.zeros((1, HEAD), F32)
        lax.fori_loop(0, nc, tile, (zero, zero, jnp.zeros((8, HEAD), F32)))

        @pl.when(bi == nb - 1)
        def _():
            small_ref[...] = acc[...]
            small_ref[7:8, :] = acc[7:8, :] * (-_sigmoid(-lamv))
            dwa_ref[...] = dwa_s[...]
            dwx_ref[...] = dwx_s[...]

    def col(off):
        return pl.BlockSpec((seq, HEAD), lambda n, b: (b, off * nblk + n))

    vec = lambda r: pl.BlockSpec((r, HEAD), lambda n, b: (0, n))
    mat = pl.BlockSpec((None, HEAD, HEAD), lambda n, b: (n, 0, 0))
    blk = pl.BlockSpec((seq, HEAD), lambda n, b: (b, n))
    dshape = jax.ShapeDtypeStruct((t, w), BF16)
    return pl.pallas_call(
        body, name=name, grid=(nblk, nb),
        in_specs=[col(4), col(5), vec(4), vec(1), mat, vec(1), mat, vec(1), vec(1), blk, blk],
        out_specs=[blk, blk, vec(8), mat, mat],
        out_shape=[dshape, dshape, jax.ShapeDtypeStruct((8, w), F32),
                   jax.ShapeDtypeStruct((nblk, HEAD, HEAD), F32), jax.ShapeDtypeStruct((nblk, HEAD, HEAD), F32)],
        scratch_shapes=[pltpu.VMEM((CHUNK, HEAD), F32)] * 2 + [pltpu.VMEM((8, HEAD), F32)]
        + [pltpu.VMEM((HEAD, HEAD), F32)] * 2,
        compiler_params=_cp(dimension_semantics=("parallel", "arbitrary")),
    )(proj, proj, cw, cb, wa, ba, wx, bx, lam, hsv, dp)


def _ffn_conv(x, prev8, cw_ref, cbv):
    x2, x1 = _shift_down(prev8, x, 2), _shift_down(prev8, x, 1)
    y = cbv + x2 * cw_ref[0:1, :]
    y = y + x1 * cw_ref[1:2, :]
    y = y + x * cw_ref[2:3, :]
    return (x2, x1), y


def _ffn_act_fwd(up, cw, cb, *, nb, seq, name):
    t, f2 = up.shape
    f = f2 // 2
    tc = _tile(f, 256)
    nj = f // tc
    rt = _tile(seq, 256, 16)
    nr = seq // rt

    def body(g_ref, v_ref, cwg_ref, cwv_ref, cbg_ref, cbv_ref, o_ref):
        cbg, cbv = cbg_ref[...], cbv_ref[...]

        def tile(c, carry):
            pg, pv = carry
            rows = pl.ds(pl.multiple_of(c * rt, rt), rt)
            xg, xv = g_ref[rows, :], v_ref[rows, :]
            _, gate = _ffn_conv(xg, pg, cwg_ref, cbg)
            _, val = _ffn_conv(xv, pv, cwv_ref, cbv)
            o_ref[rows, :] = ((gate * _sigmoid(gate)) * val).astype(BF16)
            return xg[rt - 8:], xv[rt - 8:]

        z = jnp.zeros((8, tc), F32)
        lax.fori_loop(0, nr, tile, (z, z))

    gcol = pl.BlockSpec((seq, tc), lambda j, b: (b, j))
    vcol = pl.BlockSpec((seq, tc), lambda j, b: (b, nj + j))
    gv = lambda r: pl.BlockSpec((r, tc), lambda j, b: (0, j))
    vv = lambda r: pl.BlockSpec((r, tc), lambda j, b: (0, nj + j))
    return pl.pallas_call(
        body, name=name, grid=(nj, nb),
        in_specs=[gcol, vcol, gv(3), vv(3), gv(1), vv(1)], out_specs=gcol,
        out_shape=jax.ShapeDtypeStruct((t, f), BF16),
        compiler_params=_cp(dimension_semantics=("parallel", "parallel")),
    )(up, up, cw, cw, cb, cb)


def _ffn_act_bwd(dact, up, cw, cb, *, nb, seq, name, deps=()):
    t, f2 = up.shape
    f = f2 // 2
    tc = _tile(f, 256)
    nj = f // tc
    rt = _tile(seq, 256, 16)
    nr = seq // rt

    def body(da_ref, g_ref, v_ref, cwg_ref, cwv_ref, cbg_ref, cbv_ref,
             dg_ref, dv_ref, sg_ref, sv_ref, eg, ev, accg, accv):
        bi = pl.program_id(1)
        cbg, cbv = cbg_ref[...], cbv_ref[...]

        @pl.when(bi == 0)
        def _():
            accg[...] = jnp.zeros_like(accg)
            accv[...] = jnp.zeros_like(accv)

        colsum = lambda z: jnp.sum(z, axis=0, keepdims=True)

        def first(c, carry):
            pg, pv = carry
            rows = pl.ds(pl.multiple_of(c * rt, rt), rt)
            xg, xv = g_ref[rows, :], v_ref[rows, :]
            (g2, g1), gate = _ffn_conv(xg, pg, cwg_ref, cbg)
            (v2, v1), val = _ffn_conv(xv, pv, cwv_ref, cbv)
            s = _sigmoid(gate)
            da = da_ref[rows, :]
            dgate = da * val * _dsilu(gate, s)
            dval = da * (gate * s)
            eg[rows, :] = dgate
            ev[rows, :] = dval
            for acc, (s2, s1, s0), d in ((accg, (g2, g1, xg), dgate), (accv, (v2, v1, xv), dval)):
                acc[0:1, :] += colsum(s2 * d)
                acc[1:2, :] += colsum(s1 * d)
                acc[2:3, :] += colsum(s0 * d)
                acc[3:4, :] += colsum(d)
            return xg[rt - 8:], xv[rt - 8:]

        z = jnp.zeros((8, tc), F32)
        lax.fori_loop(0, nr, first, (z, z))

        def second(c, carry):
            base = pl.multiple_of(c * rt, rt)
            rows = pl.ds(base, rt)
            after = pl.ds(pl.multiple_of(jnp.minimum(base + rt, seq - 8), 8), 8)
            more = c < nr - 1
            for e, cw_ref, d_ref in ((eg, cwg_ref, dg_ref), (ev, cwv_ref, dv_ref)):
                d = e[rows, :]
                n8 = jnp.where(more, e[after, :], 0.0)
                d1, d2 = _shift_up(d, n8, 1), _shift_up(d, n8, 2)
                d_ref[rows, :] = (d * cw_ref[2:3, :] + d1 * cw_ref[1:2, :] + d2 * cw_ref[0:1, :]).astype(BF16)
            return carry

        lax.fori_loop(0, nr, second, 0)

        @pl.when(bi == nb - 1)
        def _():
            sg_ref[...] = accg[...]
            sv_ref[...] = accv[...]

    gcol = pl.BlockSpec((seq, tc), lambda j, b: (b, j))
    vcol = pl.BlockSpec((seq, tc), lambda j, b: (b, nj + j))
    gv = lambda r: pl.BlockSpec((r, tc), lambda j, b: (0, j))
    vv = lambda r: pl.BlockSpec((r, tc), lambda j, b: (0, nj + j))
    dshape = jax.ShapeDtypeStruct((t, f), BF16)
    sshape = jax.ShapeDtypeStruct((8, f), F32)
    return _pcall(
        body, n_in=7, args=[dact, up, up, cw, cw, cb, cb], deps=deps, name=name, grid=(nj, nb),
        in_specs=[gcol, gcol, vcol, gv(3), vv(3), gv(1), vv(1)],
        out_specs=[gcol, gcol, gv(8), gv(8)], out_shape=[dshape, dshape, sshape, sshape],
        scratch_shapes=[pltpu.VMEM((seq, tc), F32)] * 2 + [pltpu.VMEM((8, tc), F32)] * 2,
        compiler_params=_cp(dimension_semantics=("parallel", "arbitrary")))


def _adamw_math(wv, g, mv, vv):
    m = ADAM_B1 * mv + (1.0 - ADAM_B1) * g
    v = ADAM_B2 * vv + (1.0 - ADAM_B2) * (g * g)
    m_hat = m / (1.0 - ADAM_B1 ** ADAM_STEP)
    v_hat = v / (1.0 - ADAM_B2 ** ADAM_STEP)
    delta = -ADAM_LR * (m_hat / (jnp.sqrt(v_hat) + ADAM_EPS) + ADAM_WD * wv)
    return delta, m, v


def _adamw_shard(wv, mv, vv, p_own, rb, *, name):
    r, cdim = wv.shape
    tr = _row_tile(r, cdim)

    def body(w_ref, m_ref, v_ref, p_ref, rb_ref, g_ref, d_ref, mo_ref, vo_ref):
        g = p_ref[...].astype(F32)
        for k in range(3):
            g = g + rb_ref[k].astype(F32)
        g_ref[...] = g
        d_ref[...], mo_ref[...], vo_ref[...] = _adamw_math(w_ref[...], g, m_ref[...], v_ref[...])

    row = pl.BlockSpec((tr, cdim), lambda i: (i, 0))
    shp = jax.ShapeDtypeStruct((r, cdim), F32)
    return pl.pallas_call(
        body, name=name, grid=(r // tr,),
        in_specs=[row, row, row, row, pl.BlockSpec((3, tr, cdim), lambda i: (0, i, 0))],
        out_specs=[row] * 4, out_shape=[shp] * 4, compiler_params=_cp(dimension_semantics=("parallel",)),
    )(wv, mv, vv, p_own, rb)


def _adamw_packed(wv, g, mv, vv, *, name):
    r = wv.shape[0]
    tr = _tile(r, 256, 8)

    def body(w_ref, g_ref, m_ref, v_ref, d_ref, mo_ref, vo_ref):
        d_ref[...], mo_ref[...], vo_ref[...] = _adamw_math(w_ref[...], g_ref[...], m_ref[...], v_ref[...])

    row = pl.BlockSpec((tr, LANE), lambda i: (i, 0))
    shp = jax.ShapeDtypeStruct((r, LANE), F32)
    return pl.pallas_call(
        body, name=name, grid=(r // tr,), in_specs=[row] * 4, out_specs=[row] * 3, out_shape=[shp] * 3,
        compiler_params=_cp(dimension_semantics=("parallel",)),
    )(wv, g, mv, vv)


def _place():
    return lax.axis_index("x"), lax.axis_index("y"), lax.axis_index("c")


def _all_reduce_packed(p, *, name):
    r = p.shape[0]
    tr = _tile(r, 256, 8)

    def body(p_ref, o_ref, buf, send, recv, lsem):
        x, y, c = _place()
        me, sibling = (x, y, c), (x, y, 1 - c)
        chips = [(1 - x, y), (x, 1 - y), (1 - x, 1 - y)]

        def slot(px, py, pc):
            return 4 * px + 2 * py + pc

        def copy(k, block, to, src=None):
            dst = buf.at[slot(*block)]
            return pltpu.make_async_remote_copy(
                src_ref=dst if src is None else src, dst_ref=dst, send_sem=send.at[k], recv_sem=recv.at[k],
                device_id=to, device_id_type=MESH)

        mine = pltpu.make_async_copy(p_ref, buf.at[slot(*me)], lsem)
        mine.start()
        first = [copy(0, me, sibling, src=p_ref)]
        first += [copy(1 + j, me, (*chip, c), src=p_ref) for j, chip in enumerate(chips)]
        for cp in first:
            cp.start()
        passed = [copy(4 + j, (*chip, c), sibling) for j, chip in enumerate(chips)]
        for j, chip in enumerate(chips):
            copy(1 + j, (*chip, c), me).wait_recv()
            passed[j].start()
        copy(0, sibling, me).wait_recv()
        for j, chip in enumerate(chips):
            copy(4 + j, (*chip, 1 - c), me).wait_recv()
        for cp in first + passed:
            cp.wait_send()
        mine.wait()

        def add(i, carry):
            rows = pl.ds(pl.multiple_of(i * tr, tr), tr)
            s = buf[0, rows, :]
            for d in range(1, NDEV):
                s = s + buf[d, rows, :]
            o_ref[rows, :] = s
            return carry

        lax.fori_loop(0, r // tr, add, 0)

    vm = pl.BlockSpec(memory_space=pltpu.VMEM)
    return pl.pallas_call(
        body, name=name, in_specs=[vm], out_specs=vm, out_shape=jax.ShapeDtypeStruct(p.shape, p.dtype),
        scratch_shapes=[pltpu.VMEM((NDEV,) + p.shape, p.dtype), pltpu.SemaphoreType.DMA((7,)),
                        pltpu.SemaphoreType.DMA((7,)), pltpu.SemaphoreType.DMA],
        compiler_params=_cp(),
    )(p)


HBM = pl.BlockSpec(memory_space=pltpu.HBM)
SEM = pl.BlockSpec(memory_space=pltpu.SEMAPHORE)
EFFECT = pltpu.SideEffectType.DATAFLOW_SIDE_EFFECTING


def _plan_copies(plan, s_refs, l_refs, send, recv):
    def pick(kind, a, idx):
        ref = (s_refs if kind == "s" else l_refs)[a]
        return ref if idx is None else ref.at[idx]

    return [pltpu.make_async_remote_copy(
        src_ref=pick(*src), dst_ref=pick(*dst), send_sem=send.at[i], recv_sem=recv.at[i],
        device_id=to, device_id_type=MESH) for i, (src, dst, to) in enumerate(plan(*_place()))]


def _xfer_start(srcs, lands, plan, *, name, deps=()):
    ns, nl = len(srcs), len(lands)
    nd = len(deps)
    ncopy = len(plan(0, 0, 0))

    def body(*refs):
        s_refs, l_refs = refs[:ns], refs[ns:ns + nl]
        send, recv = refs[ns + nl + nd], refs[ns + nl + nd + 1]
        token = refs[-1]
        for cp in _plan_copies(plan, s_refs, l_refs, send, recv):
            cp.start()
        token[...] = jnp.zeros_like(token)

    bufs = list(srcs) + list(lands)
    outs = pl.pallas_call(
        body, name=name,
        out_shape=(pltpu.SemaphoreType.DMA((ncopy,)), pltpu.SemaphoreType.DMA((ncopy,)),
                   *[pltpu.HBM(b.shape, b.dtype) for b in bufs], jax.ShapeDtypeStruct((8, LANE), F32)),
        in_specs=[HBM] * (ns + nl) + [ANY] * nd,
        out_specs=(SEM, SEM, *[HBM] * (ns + nl), pl.BlockSpec(memory_space=pltpu.VMEM)),
        input_output_aliases={i: 2 + i for i in range(ns + nl)},
        compiler_params=pltpu.CompilerParams(has_side_effects=EFFECT),
    )(*[pltpu.with_memory_space_constraint(b, pltpu.HBM) for b in bufs], *deps)
    return outs[0], outs[1], list(outs[2:2 + ns]), list(outs[2 + ns:2 + ns + nl]), outs[-1]


def _xfer_wait(send, recv, srcs, lands, plan, after, *, name):
    ns, nl = len(srcs), len(lands)

    def body(*refs):
        s_refs, l_refs = refs[:ns], refs[ns:ns + nl]
        send_ref, recv_ref = refs[ns + nl], refs[ns + nl + 1]
        for cp in _plan_copies(plan, s_refs, l_refs, send_ref, recv_ref):
            cp.wait_send()
            cp.wait_recv()

    bufs = list(srcs) + list(lands)
    outs = pl.pallas_call(
        body, name=name, out_shape=tuple(pltpu.HBM(b.shape, b.dtype) for b in bufs),
        in_specs=[HBM] * (ns + nl) + [SEM, SEM, ANY], out_specs=tuple([HBM] * (ns + nl)),
        input_output_aliases={i: i for i in range(ns + nl)},
        compiler_params=pltpu.CompilerParams(has_side_effects=EFFECT),
    )(*bufs, send, recv, after)
    return list(outs[:ns]), list(outs[ns:])


def _slot(px, py, pc):
    return 4 * px + 2 * py + pc


def _plan_gather_first(n):
    def plan(x, y, c):
        peers = [(x, y, 1 - c), (1 - x, y, c), (x, 1 - y, c), (1 - x, 1 - y, c)]
        return [(("s", a, None), ("l", a, _slot(x, y, c)), to) for a in range(n) for to in peers]
    return plan


def _plan_gather_pass(n):
    def plan(x, y, c):
        chips = [(1 - x, y), (x, 1 - y), (1 - x, 1 - y)]
        return [(("l", a, _slot(px, py, c)), ("l", a, _slot(px, py, c)), (x, y, 1 - c))
                for a in range(n) for px, py in chips]
    return plan


def _plan_rs_sibling(n):
    def plan(x, y, c):
        return [(("s", a, 2 * j + (1 - c)), ("l", a, j), (x, y, 1 - c)) for a in range(n) for j in range(4)]
    return plan


def _plan_rs_plane(n):
    def plan(x, y, c):
        chips = [(1 - x, y), (x, 1 - y), (1 - x, 1 - y)]
        return [(("s", a, 2 * px + py), ("l", a, k), (px, py, c)) for a in range(n) for k, (px, py) in enumerate(chips)]
    return plan


def _gather_start(shards, me, *, name, deps=()):
    lands = [lax.dynamic_update_index_in_dim(lax.empty((NDEV,) + s.shape, s.dtype), s, me, 0) for s in shards]
    return _xfer_start(shards, lands, _plan_gather_first(len(shards)), name=name, deps=deps)


def _add_pairs(g, rcv, cidx, *, name):
    _, r, cdim = g.shape
    tr = _row_tile(r, cdim)

    def body(c_ref, g_ref, r_ref, o_ref):
        o_ref[...] = (g_ref[...].astype(F32) + r_ref[...].astype(F32)).astype(o_ref.dtype)

    grid_spec = pltpu.PrefetchScalarGridSpec(
        num_scalar_prefetch=1, grid=(4, r // tr),
        in_specs=[pl.BlockSpec((None, tr, cdim), lambda j, i, c_ref: (2 * j + c_ref[0], i, 0)),
                  pl.BlockSpec((None, tr, cdim), lambda j, i, c_ref: (j, i, 0))],
        out_specs=pl.BlockSpec((None, tr, cdim), lambda j, i, c_ref: (j, i, 0)))
    return pl.pallas_call(
        body, name=name, grid_spec=grid_spec, out_shape=jax.ShapeDtypeStruct((4, r, cdim), g.dtype),
        compiler_params=_cp(dimension_semantics=("parallel", "parallel")),
    )(cidx, g, rcv)


def _pack(arrs):
    flat = jnp.concatenate([a.reshape(-1).astype(F32) for a in arrs])
    n = flat.shape[0]
    rows = -(-n // LANE)
    rows = -(-rows // 256) * 256
    return jnp.pad(flat, (0, rows * LANE - n)).reshape(rows, LANE)


def _unpack(packed, shapes):
    flat = packed.reshape(-1)
    out, o = [], 0
    for s in shapes:
        n = math.prod(s)
        out.append(flat[o:o + n].reshape(s))
        o += n
    return out


def _pad_halves(a, hs, hp):
    lead = a.shape[:-1]
    k = a.shape[-1] // hs
    pads = [(0, 0)] * (len(lead) + 1) + [(0, hp - hs)]
    return jnp.pad(a.reshape(*lead, k, hs), pads).reshape(*lead, k * hp)


def _unpad_halves(a, hs, hp):
    lead = a.shape[:-1]
    k = a.shape[-1] // hp
    return a.reshape(*lead, k, hp)[..., :hs].reshape(*lead, k * hs)


def kernel(x, ln1_w, w_in, lb_gamma, hg_norm_w, lru_conv_w, lru_conv_b, lru_wa, lru_ba, lru_wx, lru_bx, lru_lambda, lru_norm_w, w_out, ln2_w, ffn_w_up, ffn_conv_w, ffn_conv_b, ffn_w_down, final_norm_w, loss_target, m_ln1_w, m_w_in, m_lb_gamma, m_hg_norm_w, m_lru_conv_w, m_lru_conv_b, m_lru_wa, m_lru_ba, m_lru_wx, m_lru_bx, m_lru_lambda, m_lru_norm_w, m_w_out, m_ln2_w, m_ffn_w_up, m_ffn_conv_w, m_ffn_conv_b, m_ffn_w_down, m_final_norm_w, v_ln1_w, v_w_in, v_lb_gamma, v_hg_norm_w, v_lru_conv_w, v_lru_conv_b, v_lru_wa, v_lru_ba, v_lru_wx, v_lru_bx, v_lru_lambda, v_lru_norm_w, v_w_out, v_ln2_w, v_ffn_w_up, v_ffn_conv_w, v_ffn_conv_b, v_ffn_w_down, v_final_norm_w):
    nb, seq, d = x.shape
    t = nb * seq
    wmix = d // 2
    in_sh = w_in.shape[2]
    up_sh = ffn_w_up.shape[2]
    hs = ffn_w_down.shape[1]
    hp = -(-hs // LANE) * LANE
    up_pad = 2 * hp
    fpad = 4 * up_pad
    cx, cy, cc = _place()
    me = 4 * cx + 2 * cy + cc
    plane = 2 * cx + cy
    first, passed, to_sib, to_plane = _plan_gather_first, _plan_gather_pass, _plan_rs_sibling, _plan_rs_plane

    win_s = w_in[0].astype(BF16)
    wout_s = w_out[0].astype(BF16)
    wup_s = _pad_halves(ffn_w_up[0], hs, hp).astype(BF16)
    wdn_s = jnp.pad(ffn_w_down[0], ((0, hp - hs), (0, 0))).astype(BF16)
    fcw_s = _pad_halves(ffn_conv_w[0], hs, hp)
    sa = _gather_start([win_s, wout_s, lru_conv_w[0], fcw_s], me, name="gather_a_start")
    su = _gather_start([wup_s], me, name="gather_u_start", deps=(sa[4],))
    sd = _gather_start([wdn_s], me, name="gather_d_start", deps=(su[4],))
    fcb = _pad_halves(ffn_conv_b, hs, hp)
    wa_b, wx_b = lru_wa[0].astype(BF16), lru_wx[0].astype(BF16)

    xf = x.reshape(t, d)
    hn = _rms_fwd(xf, ln1_w, name="ln1_fwd", deps=(sd[4],))
    _, la = _xfer_wait(sa[0], sa[1], sa[2], sa[3], first(4), hn, name="gather_a_wait")
    sa2 = _xfer_start([], la, passed(4), name="gather_a_pass")
    _, (win_g, wout_g, lcw_g, fcw_g) = _xfer_wait(sa2[0], sa2[1], [], sa2[3], passed(4), sa2[4], name="gather_a_done")
    wout_f = wout_g.reshape(1, d, d)
    lcw = lcw_g.transpose(1, 0, 2).reshape(lru_conv_w.shape[1], wmix)
    fcw = fcw_g.transpose(1, 0, 2).reshape(ffn_conv_w.shape[1], 2 * fpad)
    proj = _mm(hn, win_g, kind="nn", out_dtype=F32, name="in_proj")
    o_hg, o_pre, states = _hgrn_fwd(proj, lb_gamma, hg_norm_w, nb=nb, seq=seq, name="hgrn_fwd")
    _, lu = _xfer_wait(su[0], su[1], su[2], su[3], first(1), o_pre, name="gather_u_wait")
    su2 = _xfer_start([], lu, passed(1), name="gather_u_pass")
    p_lru, h_lru = _lru_fwd(proj, lcw, lru_conv_b, wa_b, lru_ba, wx_b, lru_bx, lru_lambda,
                            nb=nb, seq=seq, name="lru_fwd", deps=(su2[4],))
    o_lru = _rms_fwd(p_lru, lru_norm_w, name="lru_norm_fwd")
    _, (wup_g,) = _xfer_wait(su2[0], su2[1], [], su2[3], passed(1), o_lru, name="gather_u_done")
    _, ld = _xfer_wait(sd[0], sd[1], sd[2], sd[3], first(1), o_lru, name="gather_d_wait")
    sd2 = _xfer_start([], ld, passed(1), name="gather_d_pass")
    mix = jnp.concatenate([o_hg, o_lru], axis=1)
    h1 = _mm(mix, wout_f, kind="nn", out_dtype=F32, name="out_proj", res=xf, deps=(sd2[4],))
    hn2 = _rms_fwd(h1, ln2_w, name="ln2_fwd")
    up = _mm(hn2, wup_g, kind="nn", out_dtype=F32, name="ffn_up", tm_cap=512, tn_cap=up_pad, tk_cap=512)
    _, (wdn_g,) = _xfer_wait(sd2[0], sd2[1], [], sd2[3], passed(1), up, name="gather_d_done")
    wdn_f = wdn_g.reshape(1, fpad, d)
    act = _ffn_act_fwd(up, fcw, fcb, nb=nb, seq=seq, name="ffn_act_fwd")
    h2 = _mm(act, wdn_f, kind="nn", out_dtype=F32, name="ffn_down", res=h1, tk_cap=hp)

    dh2, dh2_b, g_fnw, loss_part = _loss_head(h2, final_norm_w.reshape(1, d), loss_target.reshape(t, d),
                                              name="loss_head")
    loss = lax.psum(loss_part[0, 0], ("x", "y", "c"))

    cidx = jnp.reshape(cc, (1,)).astype(jnp.int32)

    def to_sibling(g8, tag, deps=()):
        land = lax.empty((4,) + g8.shape[1:], g8.dtype)
        return _xfer_start([g8], [land], to_sib(1), name=f"rs_sib_start_{tag}", deps=deps)

    def sibling_sum(st, after, tag):
        (g8,), (rcv,) = _xfer_wait(st[0], st[1], st[2], st[3], to_sib(1), after, name=f"rs_sib_wait_{tag}")
        psum = _add_pairs(g8, rcv, cidx, name=f"rs_add_{tag}")
        land = lax.empty((3,) + psum.shape[1:], psum.dtype)
        return _xfer_start([psum], [land], to_plane(1), name=f"rs_plane_start_{tag}")

    dact = _mm(dh2_b, wdn_f, kind="nt", out_dtype=F32, name="ffn_down_dx", tm_cap=512, tn_cap=up_pad)
    g_wdn = _mm(act, dh2_b, kind="tn", out_dtype=BF16, name="ffn_down_dw", tm_cap=hp)[0]
    rs_d = to_sibling(g_wdn.reshape(NDEV, hp, d), "down")
    dup_g, dup_v, s_g, s_v = _ffn_act_bwd(dact, up, fcw, fcb, nb=nb, seq=seq, name="ffn_act_bwd", deps=(rs_d[4],))
    dup = jnp.concatenate([dup_g, dup_v], axis=1)
    pl_d = sibling_sum(rs_d, dup, "down")
    dhn2 = _mm(dup, wup_g, kind="nt", out_dtype=F32, name="ffn_up_dx", tm_cap=512, tn_cap=1024, tk_cap=up_pad,
               deps=(pl_d[4],))
    g_wup = _mm(hn2, dup, kind="tn", out_dtype=BF16, name="ffn_up_dw", out_blocks=NDEV, tn_cap=up_pad, tm_cap=512)
    rs_u = to_sibling(g_wup, "up")
    dh1, dh1_b, g_ln2 = _rms_bwd(dhn2, h1, ln2_w, name="ln2_bwd", extra=dh2, want_bf16=True, deps=(rs_u[4],))
    dmix = _mm(dh1_b, wout_f, kind="nt", out_dtype=F32, name="out_proj_dx")
    pl_u = sibling_sum(rs_u, dmix, "up")
    g_wout = _mm(mix, dh1_b, kind="tn", out_dtype=BF16, name="out_proj_dw", deps=(pl_u[4],))[0]
    rs_o = to_sibling(g_wout.reshape(NDEV, d // NDEV, d), "out")
    dp_lru, g_lnw = _rms_bwd(dmix, p_lru, lru_norm_w, name="lru_norm_bwd", dy_cb=1, deps=(rs_o[4],))
    dxr, dyr, s_lru, g_wa, g_wx = _lru_bwd(proj, lcw, lru_conv_b, wa_b, lru_ba, wx_b, lru_bx, lru_lambda,
                                           h_lru, dp_lru, nb=nb, seq=seq, name="lru_bwd")
    pl_o = sibling_sum(rs_o, dxr, "out")
    dqr, dfr, dir_, dgr, s_hg = _hgrn_bwd(proj, lb_gamma, hg_norm_w, o_pre, states, dmix,
                                          nb=nb, seq=seq, name="hgrn_bwd", deps=(pl_o[4],))
    dproj = jnp.concatenate([dqr, dfr, dir_, dgr, dxr, dyr], axis=1)
    g_win = _mm(hn, dproj, kind="tn", out_dtype=BF16, name="in_proj_dw", out_blocks=NDEV, tn_cap=in_sh)
    rs_i = to_sibling(g_win, "in")
    dhn = _mm(dproj, win_g, kind="nt", out_dtype=F32, name="in_proj_dx", tk_cap=768, deps=(rs_i[4],))
    pl_i = sibling_sum(rs_i, dhn, "in")
    grad_x, g_ln1 = _rms_bwd(dhn, xf, ln1_w, name="ln1_bwd", extra=dh1, deps=(pl_i[4],))

    fcw_parts = _unpad_halves(jnp.concatenate([s_g[0:3], s_v[0:3]], axis=1), hs, hp).reshape(3, NDEV, up_sh)
    fcb_part = _unpad_halves(jnp.concatenate([s_g[3:4], s_v[3:4]], axis=1), hs, hp)
    small_parts = [g_ln1, s_hg[0:2], s_hg[2:3], s_lru[0:4], s_lru[4:5], g_wa, s_lru[5:6], g_wx, s_lru[6:7],
                   s_lru[7:8], g_lnw, g_ln2, fcw_parts, fcb_part, g_fnw]
    summed = _unpack(_all_reduce_packed(_pack(small_parts), name="allreduce_small"),
                     [p.shape for p in small_parts])
    (s_ln1, s_lbg, s_hgn, s_lcw, s_lcb, s_wa, s_ba, s_wx, s_bx, s_lam, s_lnw, s_ln2, s_fcw, s_fcb, s_fnw) = summed
    sh_lcw = lru_conv_w.shape[2]
    g_small = {
        "ln1_w": s_ln1, "lb_gamma": s_lbg, "hg_norm_w": s_hgn,
        "lru_conv_w": lax.dynamic_slice_in_dim(s_lcw, me * sh_lcw, sh_lcw, axis=1),
        "lru_conv_b": s_lcb, "lru_wa": s_wa, "lru_ba": s_ba, "lru_wx": s_wx, "lru_bx": s_bx,
        "lru_lambda": s_lam, "lru_norm_w": s_lnw, "ln2_w": s_ln2,
        "ffn_conv_w": lax.dynamic_index_in_dim(s_fcw, me, 1, keepdims=False),
        "ffn_conv_b": s_fcb, "final_norm_w": s_fnw,
    }
    w_small = {"ln1_w": ln1_w, "lb_gamma": lb_gamma, "hg_norm_w": hg_norm_w, "lru_conv_w": lru_conv_w,
               "lru_conv_b": lru_conv_b, "lru_wa": lru_wa, "lru_ba": lru_ba, "lru_wx": lru_wx, "lru_bx": lru_bx,
               "lru_lambda": lru_lambda, "lru_norm_w": lru_norm_w, "ln2_w": ln2_w, "ffn_conv_w": ffn_conv_w,
               "ffn_conv_b": ffn_conv_b, "final_norm_w": final_norm_w}
    m_small = {"ln1_w": m_ln1_w, "lb_gamma": m_lb_gamma, "hg_norm_w": m_hg_norm_w, "lru_conv_w": m_lru_conv_w,
               "lru_conv_b": m_lru_conv_b, "lru_wa": m_lru_wa, "lru_ba": m_lru_ba, "lru_wx": m_lru_wx,
               "lru_bx": m_lru_bx, "lru_lambda": m_lru_lambda, "lru_norm_w": m_lru_norm_w, "ln2_w": m_ln2_w,
               "ffn_conv_w": m_ffn_conv_w, "ffn_conv_b": m_ffn_conv_b, "final_norm_w": m_final_norm_w}
    v_small = {"ln1_w": v_ln1_w, "lb_gamma": v_lb_gamma, "hg_norm_w": v_hg_norm_w, "lru_conv_w": v_lru_conv_w,
               "lru_conv_b": v_lru_conv_b, "lru_wa": v_lru_wa, "lru_ba": v_lru_ba, "lru_wx": v_lru_wx,
               "lru_bx": v_lru_bx, "lru_lambda": v_lru_lambda, "lru_norm_w": v_lru_norm_w, "ln2_w": v_ln2_w,
               "ffn_conv_w": v_ffn_conv_w, "ffn_conv_b": v_ffn_conv_b, "final_norm_w": v_final_norm_w}
    names = list(w_small)
    shapes = [w_small[k].shape for k in names]
    g_small = {k: g_small[k].reshape(w_small[k].shape) for k in names}
    packed = _adamw_packed(_pack([w_small[k] for k in names]), _pack([g_small[k] for k in names]),
                           _pack([m_small[k] for k in names]), _pack([v_small[k] for k in names]),
                           name="adamw_small")
    d_small, nm_small, nv_small = ({k: a for k, a in zip(names, _unpack(pk, shapes))} for pk in packed)

    def finish(st, after, tag, trim):
        (psum,), (rb,) = _xfer_wait(st[0], st[1], st[2], st[3], to_plane(1), after, name=f"rs_plane_wait_{tag}")
        return trim(lax.dynamic_index_in_dim(psum, plane, 0, keepdims=False)), trim(rb)

    keep = lambda a: a
    upd_dn = _adamw_shard(ffn_w_down[0], m_ffn_w_down[0], v_ffn_w_down[0],
                          *finish(pl_d, packed[0], "down", lambda a: a[..., :hs, :]), name="adamw_w_down")
    upd_up = _adamw_shard(ffn_w_up[0], m_ffn_w_up[0], v_ffn_w_up[0],
                          *finish(pl_u, upd_dn[0], "up", lambda a: _unpad_halves(a, hs, hp)), name="adamw_w_up")
    upd_out = _adamw_shard(w_out[0], m_w_out[0], v_w_out[0], *finish(pl_o, upd_up[0], "out", keep),
                           name="adamw_w_out")
    upd_in = _adamw_shard(w_in[0], m_w_in[0], v_w_in[0], *finish(pl_i, upd_out[0], "in", keep), name="adamw_w_in")

    grads, deltas, new_m, new_v = dict(g_small), dict(d_small), dict(nm_small), dict(nv_small)
    for k, upd in (("w_in", upd_in), ("w_out", upd_out), ("ffn_w_up", upd_up), ("ffn_w_down", upd_dn)):
        grads[k], deltas[k], new_m[k], new_v[k] = (a[None] for a in upd)
    order = ["ln1_w", "w_in", "lb_gamma", "hg_norm_w", "lru_conv_w", "lru_conv_b", "lru_wa", "lru_ba", "lru_wx",
             "lru_bx", "lru_lambda", "lru_norm_w", "w_out", "ln2_w", "ffn_w_up", "ffn_conv_w", "ffn_conv_b",
             "ffn_w_down", "final_norm_w"]
    return (loss, grad_x.reshape(nb, seq, d), *[grads[k] for k in order], *[deltas[k] for k in order],
            *[new_m[k] for k in order], *[new_v[k] for k in order])
```

```python
import math

import jax
import jax.numpy as jnp
from jax import lax
from jax.experimental import pallas as pl
from jax.experimental.pallas import tpu as pltpu

F32, BF16 = jnp.float32, jnp.bfloat16
EPS = 1e-6
HEAD = 128
CHUNK = 64
SUB = 16
NSUB = CHUNK // SUB
LRU_C = 8.0
LANE = 128
NDEV = 8
ADAM_LR, ADAM_B1, ADAM_B2, ADAM_EPS, ADAM_WD, ADAM_STEP = 0.001, 0.9, 0.999, 1e-08, 0.01, 10
MESH = pl.DeviceIdType.MESH
ANY = pl.BlockSpec(memory_space=pl.ANY)
VMEM_LIMIT = 56 * 1024 * 1024


def _cp(**kw):
    return pltpu.CompilerParams(vmem_limit_bytes=VMEM_LIMIT, **kw)


def _tile(n, cap, mult=LANE):
    best = None
    for t in range(mult, min(n, cap) + 1, mult):
        if n % t == 0:
            best = t
    return best if best is not None else n


def _row_tile(r, cdim, budget=262144):
    return _tile(r, max(16, budget // cdim), 16)


def _sigmoid(x):
    return jax.nn.sigmoid(x)


def _dsilu(x, s):
    return s * (1.0 + x * (1.0 - s))


def _iota_rows(n, w=LANE):
    return lax.broadcasted_iota(jnp.int32, (n, w), 0)


def _shift_down(prev8, xt, k):
    cat = jnp.concatenate([prev8, xt], axis=0)
    return pltpu.roll(cat, k, 0)[8:]


def _shift_up(xt, next8, k):
    cat = jnp.concatenate([xt, next8], axis=0)
    n = cat.shape[0]
    return pltpu.roll(cat, n - k, 0)[: xt.shape[0]]


def _scan_fwd(a, u):
    n = a.shape[0]
    row = _iota_rows(n, a.shape[1])
    k = 1
    while k < n:
        keep = row >= k
        a_s = jnp.where(keep, pltpu.roll(a, k, 0), 1.0)
        u_s = jnp.where(keep, pltpu.roll(u, k, 0), 0.0)
        u = a * u_s + u
        a = a * a_s
        k *= 2
    return a, u


def _scan_bwd(a, u):
    n = a.shape[0]
    row = _iota_rows(n, a.shape[1])
    k = 1
    while k < n:
        keep = row < n - k
        a_s = jnp.where(keep, pltpu.roll(a, n - k, 0), 1.0)
        u_s = jnp.where(keep, pltpu.roll(u, n - k, 0), 0.0)
        u = a * u_s + u
        a = a * a_s
        k *= 2
    return a, u


def _cumsum_fwd(u):
    n = u.shape[0]
    row = _iota_rows(n, u.shape[1])
    k = 1
    while k < n:
        u = u + jnp.where(row >= k, pltpu.roll(u, k, 0), 0.0)
        k *= 2
    return u


def _cumsum_bwd(u):
    n = u.shape[0]
    row = _iota_rows(n, u.shape[1])
    k = 1
    while k < n:
        u = u + jnp.where(row < n - k, pltpu.roll(u, n - k, 0), 0.0)
        k *= 2
    return u


def _dot(a, b, dims):
    return lax.dot_general(a.astype(BF16), b.astype(BF16), (dims, ((), ())), preferred_element_type=F32)


NN = ((1,), (0,))
NT = ((1,), (1,))
TN = ((0,), (0,))


def _pcall(body, *, n_in, in_specs, args, deps=(), **kw):
    nd = len(deps)
    if nd:
        inner = body

        def body(*refs):
            return inner(*refs[:n_in], *refs[n_in + nd:])

        in_specs = list(in_specs) + [ANY] * nd
        args = list(args) + list(deps)
    return pl.pallas_call(body, in_specs=in_specs, **kw)(*args)


def _mm(a, b3, *, kind, out_dtype, name, res=None, tm_cap=1024, tn_cap=1536, tk_cap=1024, out_blocks=1, deps=()):
    if kind == "nn":
        m, kdim = a.shape
        nb, _, nsh = b3.shape
        n = nb * nsh
        tm, tn, tk = _tile(m, tm_cap), _tile(nsh, tn_cap), _tile(kdim, tk_cap)
        per = nsh // tn
        a_spec = pl.BlockSpec((tm, tk), lambda i, j, k: (i, k))
        b_spec = pl.BlockSpec((None, tk, tn), lambda i, j, k: (j // per, k, j % per))
        dims = NN
    elif kind == "nt":
        m, kdim = a.shape
        nb, n, ksh = b3.shape
        tm, tn, tk = _tile(m, tm_cap), _tile(n, tn_cap), _tile(ksh, tk_cap)
        per = ksh // tk
        a_spec = pl.BlockSpec((tm, tk), lambda i, j, k: (i, k))
        b_spec = pl.BlockSpec((None, tn, tk), lambda i, j, k: (k // per, j, k % per))
        dims = NT
    else:
        kdim, m = a.shape
        n = b3.shape[1]
        nsh = n // out_blocks
        tm, tn, tk = _tile(m, tm_cap), _tile(nsh, tn_cap), _tile(kdim, tk_cap)
        per = nsh // tn
        a_spec = pl.BlockSpec((tk, tm), lambda i, j, k: (k, i))
        b_spec = pl.BlockSpec((tk, tn), lambda i, j, k: (k, j))
        dims = TN
    nk = kdim // tk
    grid = (m // tm, n // tn, nk)

    def body(*refs):
        if res is None:
            a_ref, b_ref, o_ref, acc = refs
        else:
            a_ref, b_ref, r_ref, o_ref, acc = refs
        k = pl.program_id(2)

        @pl.when(k == 0)
        def _():
            acc[...] = jnp.zeros_like(acc)

        acc[...] += lax.dot_general(a_ref[...], b_ref[...], (dims, ((), ())), preferred_element_type=F32)

        @pl.when(k == nk - 1)
        def _():
            r = acc[...]
            if res is not None:
                r = r + r_ref[...]
            o_ref[...] = r.astype(o_ref.dtype)

    in_specs = [a_spec, b_spec]
    args = [a, b3]
    if res is not None:
        in_specs.append(pl.BlockSpec((tm, tn), lambda i, j, k: (i, j)))
        args.append(res)
    if kind == "tn":
        out_shape = jax.ShapeDtypeStruct((out_blocks, m, nsh), out_dtype)
        out_spec = pl.BlockSpec((None, tm, tn), lambda i, j, k: (j // per, i, j % per))
    else:
        out_shape = jax.ShapeDtypeStruct((m, n), out_dtype)
        out_spec = pl.BlockSpec((tm, tn), lambda i, j, k: (i, j))
    return _pcall(
        body, n_in=len(args), in_specs=in_specs, args=args, deps=deps,
        name=name, grid=grid, out_specs=out_spec, out_shape=out_shape,
        scratch_shapes=[pltpu.VMEM((tm, tn), F32)],
        compiler_params=_cp(dimension_semantics=("parallel", "parallel", "arbitrary")))


def _rms_fwd(x, w, *, name, tm=256, deps=()):
    t, d = x.shape
    tm = _tile(t, tm, 16)

    def body(x_ref, w_ref, o_ref):
        xv = x_ref[...]
        r = lax.rsqrt(jnp.mean(xv * xv, axis=-1, keepdims=True) + EPS)
        o_ref[...] = ((xv * r) * w_ref[...]).astype(o_ref.dtype)

    return _pcall(
        body, n_in=2, args=[x, w], deps=deps, name=name, grid=(t // tm,),
        in_specs=[pl.BlockSpec((tm, d), lambda i: (i, 0)), pl.BlockSpec((1, d), lambda i: (0, 0))],
        out_specs=pl.BlockSpec((tm, d), lambda i: (i, 0)),
        out_shape=jax.ShapeDtypeStruct((t, d), BF16), compiler_params=_cp())


def _rms_bwd(dy, x, w, *, name, extra=None, dy_cb=0, want_bf16=False, tm=256, deps=()):
    t, d = x.shape
    tm = _tile(t, tm, 16)

    def body(*refs):
        refs = list(refs)
        dy_ref, x_ref, w_ref = refs[:3]
        e_ref = refs[3] if extra is not None else None
        outs = refs[4:] if extra is not None else refs[3:]
        dx_ref = outs[0]
        dxb_ref = outs[1] if want_bf16 else None
        dw_ref = outs[-1]
        i = pl.program_id(0)
        xv = x_ref[...]
        r = lax.rsqrt(jnp.mean(xv * xv, axis=-1, keepdims=True) + EPS)
        nh = xv * r
        dyv = dy_ref[...]
        dn = dyv * w_ref[...]
        dx = r * (dn - nh * jnp.mean(dn * nh, axis=-1, keepdims=True))
        if extra is not None:
            dx = dx + e_ref[...]
        dx_ref[...] = dx
        if want_bf16:
            dxb_ref[...] = dx.astype(BF16)
        part = jnp.sum(dyv * nh, axis=0, keepdims=True)

        @pl.when(i == 0)
        def _():
            dw_ref[...] = part

        @pl.when(i > 0)
        def _():
            dw_ref[...] += part

    row = pl.BlockSpec((tm, d), lambda i: (i, 0))
    in_specs = [pl.BlockSpec((tm, d), lambda i: (i, dy_cb)), row, pl.BlockSpec((1, d), lambda i: (0, 0))]
    args = [dy, x, w]
    if extra is not None:
        in_specs.append(row)
        args.append(extra)
    out_shape = [jax.ShapeDtypeStruct((t, d), F32)]
    out_specs = [row]
    if want_bf16:
        out_shape.append(jax.ShapeDtypeStruct((t, d), BF16))
        out_specs.append(row)
    out_shape.append(jax.ShapeDtypeStruct((1, d), F32))
    out_specs.append(pl.BlockSpec((1, d), lambda i: (0, 0)))
    return _pcall(
        body, n_in=len(args), in_specs=in_specs, args=args, deps=deps,
        name=name, grid=(t // tm,), out_specs=out_specs, out_shape=out_shape,
        compiler_params=_cp(dimension_semantics=("arbitrary",)))


def _loss_head(h, w, tgt, *, name, tm=256):
    t, d = h.shape
    tm = _tile(t, tm, 16)

    def body(h_ref, w_ref, t_ref, dh_ref, dhb_ref, dw_ref, loss_ref):
        i = pl.program_id(0)
        xv = h_ref[...]
        wv = w_ref[...]
        r = lax.rsqrt(jnp.mean(xv * xv, axis=-1, keepdims=True) + EPS)
        nh = xv * r
        e = nh * wv - t_ref[...]
        part_loss = jnp.full((1, LANE), 0.5 * jnp.sum(jnp.mean(e * e, axis=-1, keepdims=True)), F32)
        dyv = e * (1.0 / d)
        dn = dyv * wv
        dx = r * (dn - nh * jnp.mean(dn * nh, axis=-1, keepdims=True))
        dh_ref[...] = dx
        dhb_ref[...] = dx.astype(BF16)
        part = jnp.sum(dyv * nh, axis=0, keepdims=True)

        @pl.when(i == 0)
        def _():
            dw_ref[...] = part
            loss_ref[...] = part_loss

        @pl.when(i > 0)
        def _():
            dw_ref[...] += part
            loss_ref[...] += part_loss

    row = pl.BlockSpec((tm, d), lambda i: (i, 0))
    vec = pl.BlockSpec((1, d), lambda i: (0, 0))
    return pl.pallas_call(
        body, name=name, grid=(t // tm,), in_specs=[row, vec, row],
        out_specs=[row, row, vec, pl.BlockSpec((1, LANE), lambda i: (0, 0))],
        out_shape=[jax.ShapeDtypeStruct((t, d), F32), jax.ShapeDtypeStruct((t, d), BF16),
                   jax.ShapeDtypeStruct((1, d), F32), jax.ShapeDtypeStruct((1, LANE), F32)],
        compiler_params=_cp(dimension_semantics=("arbitrary",)),
    )(h, w, tgt)


def _lower_bound(lbg_ref):
    g0, g1 = lbg_ref[0:1, :], lbg_ref[1:2, :]
    m = jnp.maximum(g0, g1)
    e0, e1 = jnp.exp(g0 - m), jnp.exp(g1 - m)
    return e0 / (e0 + e1)


def _seg_bounds():
    offs, o = {}, 0
    for i in range(1, NSUB):
        offs[i] = (o, o + SUB * i)
        o += SUB * i
    return offs, o


def _pad_rows(x, n):
    if x.shape[0] == n:
        return x
    return jnp.concatenate([x, jnp.zeros((n - x.shape[0], x.shape[1]), x.dtype)], axis=0)


def _offdiag_setup(q, k, b, v, b_c):
    offs, total = _seg_bounds()
    padded = -(-total // LANE) * LANE
    eq_parts = [jnp.zeros((SUB, HEAD), F32)]
    ek_parts, k_parts, v_parts = [], [], []
    for i in range(1, NSUB):
        r_i = b_c[SUB * i - 1:SUB * i, :]
        eq_parts.append(jnp.exp(b[SUB * i:SUB * (i + 1)] - r_i))
        ek_parts.append(jnp.exp(r_i - b[0:SUB * i]))
        k_parts.append(k[0:SUB * i])
        v_parts.append(v[0:SUB * i])
    eq = jnp.concatenate(eq_parts, axis=0)
    ek = _pad_rows(jnp.concatenate(ek_parts, axis=0), padded)
    kt = _pad_rows(jnp.concatenate(k_parts, axis=0), padded) * ek
    vs = _pad_rows(jnp.concatenate(v_parts, axis=0), padded)
    qt = q * eq
    rsub = lax.broadcasted_iota(jnp.int32, (CHUNK, padded), 0) // SUB
    col = lax.broadcasted_iota(jnp.int32, (CHUNK, padded), 1)
    cseg = jnp.zeros((CHUNK, padded), jnp.int32)
    for i in range(1, NSUB):
        cseg = cseg + (col >= offs[i][0]).astype(jnp.int32)
    mask = (rsub == cseg) & (col < total)
    a = jnp.where(mask, _dot(qt, kt, NT), 0.0)
    return offs, eq, ek, kt, vs, qt, mask, a


def _hgrn_fwd(proj, lbg, nw, *, nb, seq, name):
    t = proj.shape[0]
    w = lbg.shape[1]
    nh = w // HEAD
    nc = seq // CHUNK

    def body(q_ref, f_ref, i_ref, g_ref, lbg_ref, nw_ref, ohg_ref, opre_ref, st_ref, k_c, b_c, v_c, st):
        lb = _lower_bound(lbg_ref)
        nwv = nw_ref[...]
        st[...] = jnp.zeros_like(st)

        def chunk(c, carry):
            rows = pl.ds(pl.multiple_of(c * CHUNK, CHUNK), CHUNK)
            qr = q_ref[rows, :]
            q = qr * _sigmoid(qr)
            f = lb + (1.0 - lb) * _sigmoid(f_ref[rows, :])
            k = 1.0 - f
            b = _cumsum_fwd(jnp.log(f))
            v = i_ref[rows, :]
            k_c[...] = k
            b_c[...] = b
            v_c[...] = v
            s_t = st[...]
            st_ref[c] = s_t
            o = _dot(q * jnp.exp(b), s_t, NT)
            _, _, _, _, vs, _, _, a = _offdiag_setup(q, k, b, v, b_c)
            o = o + _dot(a, vs, NN)
            diag = []
            for i in range(NSUB):
                accs = [jnp.zeros((8, HEAD), F32) for _ in range(SUB // 8)]
                for j in range(SUB):
                    r = SUB * i + j
                    bj, kj, vj = b_c[r:r + 1, :], k_c[r:r + 1, :], v_c[r:r + 1, :]
                    for p in range(j // 8, SUB // 8):
                        lo = SUB * i + 8 * p
                        d = jnp.exp(b[lo:lo + 8] - bj)
                        if 8 * p < j:
                            d = jnp.where(_iota_rows(8) + 8 * p >= j, d, 0.0)
                        s = jnp.sum(q[lo:lo + 8] * d * kj, axis=-1, keepdims=True)
                        accs[p] = accs[p] + s * vj
                diag.extend(accs)
            o = o + jnp.concatenate(diag, axis=0)
            bl = b_c[CHUNK - 1:CHUNK, :]
            kb = k * jnp.exp(bl - b)
            st[...] = s_t * jnp.exp(bl) + _dot(v, kb, TN)
            opre_ref[rows, :] = o
            rn = lax.rsqrt(jnp.mean(o * o, axis=-1, keepdims=True) + EPS)
            gr = g_ref[rows, :]
            ohg_ref[rows, :] = (((o * rn) * nwv) * (gr * _sigmoid(gr))).astype(BF16)
            return carry

        lax.fori_loop(0, nc, chunk, 0)

    def col(off):
        return pl.BlockSpec((seq, HEAD), lambda h, b: (b, off * nh + h))

    vec = lambda r: pl.BlockSpec((r, HEAD), lambda h, b: (0, h))
    out_blk = pl.BlockSpec((seq, HEAD), lambda h, b: (b, h))
    return pl.pallas_call(
        body, name=name, grid=(nh, nb),
        in_specs=[col(0), col(1), col(2), col(3), vec(2), vec(1)],
        out_specs=[out_blk, out_blk, pl.BlockSpec((None, None, nc, HEAD, HEAD), lambda h, b: (b, h, 0, 0, 0))],
        out_shape=[jax.ShapeDtypeStruct((t, w), BF16), jax.ShapeDtypeStruct((t, w), F32),
                   jax.ShapeDtypeStruct((nb, nh, nc, HEAD, HEAD), F32)],
        scratch_shapes=[pltpu.VMEM((CHUNK, HEAD), F32)] * 3 + [pltpu.VMEM((HEAD, HEAD), F32)],
        compiler_params=_cp(dimension_semantics=("parallel", "parallel")),
    )(proj, proj, proj, proj, lbg, nw)


def _hgrn_bwd(proj, lbg, nw, opre, states, dmix, *, nb, seq, name, deps=()):
    t = proj.shape[0]
    w = lbg.shape[1]
    nh = w // HEAD
    nc = seq // CHUNK

    def body(q_ref, f_ref, i_ref, g_ref, lbg_ref, nw_ref, opre_ref, st_ref, dm_ref,
             dq_ref, df_ref, di_ref, dg_ref, small_ref, k_c, b_c, v_c, dst, dlb_s, dwn_s):
        bi = pl.program_id(1)
        lb = _lower_bound(lbg_ref)
        nwv = nw_ref[...]
        dst[...] = jnp.zeros_like(dst)

        @pl.when(bi == 0)
        def _():
            dlb_s[...] = jnp.zeros_like(dlb_s)
            dwn_s[...] = jnp.zeros_like(dwn_s)

        def chunk(it, carry):
            c = nc - 1 - it
            rows = pl.ds(pl.multiple_of(c * CHUNK, CHUNK), CHUNK)
            qr = q_ref[rows, :]
            sq = _sigmoid(qr)
            q = qr * sq
            sg = _sigmoid(f_ref[rows, :])
            f = lb + (1.0 - lb) * sg
            k = 1.0 - f
            b = _cumsum_fwd(jnp.log(f))
            v = i_ref[rows, :]
            k_c[...] = k
            b_c[...] = b
            v_c[...] = v
            o = opre_ref[rows, :]
            rn = lax.rsqrt(jnp.mean(o * o, axis=-1, keepdims=True) + EPS)
            nhat = o * rn
            dm = dm_ref[rows, :]
            gr = g_ref[rows, :]
            sgr = _sigmoid(gr)
            dnw = dm * (gr * sgr)
            dg_ref[rows, :] = (dm * (nhat * nwv) * _dsilu(gr, sgr)).astype(BF16)
            dwn_s[...] += jnp.sum(dnw * nhat, axis=0, keepdims=True)
            dn = dnw * nwv
            do = rn * (dn - nhat * jnp.mean(dn * nhat, axis=-1, keepdims=True))
            s_t = st_ref[c]
            ds = dst[...]
            eb = jnp.exp(b)
            qb = q * eb
            bl = b_c[CHUNK - 1:CHUNK, :]
            ebl = jnp.exp(bl)
            kdec = jnp.exp(bl - b)
            kb = k * kdec
            dqb = _dot(do, s_t, NN)
            dkb = _dot(v, ds, NN)
            dv = _dot(kb, ds, NT)
            d_ebl = jnp.sum(ds * s_t, axis=0, keepdims=True)
            dst[...] = ds * ebl + _dot(do, qb, TN)
            dq = dqb * eb
            dk = dkb * kdec
            t_kb = dkb * kb
            db = dqb * qb - t_kb
            db_last = jnp.sum(t_kb, axis=0, keepdims=True) + d_ebl * ebl
            offs, eq, ek, kt, vs, qt, mask, a = _offdiag_setup(q, k, b, v, b_c)
            da = jnp.where(mask, _dot(do, vs, NT), 0.0)
            dvs = _dot(a, do, TN)
            dqt = _dot(da, kt, NN)
            dkt = _dot(da, qt, TN)
            dq = dq + dqt * eq
            db = db + dqt * qt
            for i in range(1, NSUB):
                lo, hi = offs[i]
                dk = dk + _pad_rows(dkt[lo:hi] * ek[lo:hi], CHUNK)
                db = db - _pad_rows(dkt[lo:hi] * kt[lo:hi], CHUNK)
                dv = dv + _pad_rows(dvs[lo:hi], CHUNK)
            npv = SUB // 8
            dq_d, dk_d, dv_d, db_d = [], [], [], []
            for i in range(NSUB):
                aq = [jnp.zeros((8, HEAD), F32) for _ in range(npv)]
                ak = [jnp.zeros((8, HEAD), F32) for _ in range(npv)]
                av = [jnp.zeros((8, HEAD), F32) for _ in range(npv)]
                ab = [jnp.zeros((8, HEAD), F32) for _ in range(npv)]
                for j in range(SUB):
                    r = SUB * i + j
                    bj, kj, vj = b_c[r:r + 1, :], k_c[r:r + 1, :], v_c[r:r + 1, :]
                    sk = jnp.zeros((1, HEAD), F32)
                    sv = jnp.zeros((1, HEAD), F32)
                    sb = jnp.zeros((1, HEAD), F32)
                    for p in range(j // 8, npv):
                        lo = SUB * i + 8 * p
                        d = jnp.exp(b[lo:lo + 8] - bj)
                        if 8 * p < j:
                            d = jnp.where(_iota_rows(8) + 8 * p >= j, d, 0.0)
                        qd = q[lo:lo + 8] * d
                        dop = do[lo:lo + 8]
                        a_j = jnp.sum(qd * kj, axis=-1, keepdims=True)
                        da_j = jnp.sum(dop * vj, axis=-1, keepdims=True)
                        aq[p] = aq[p] + (da_j * d) * kj
                        pm = da_j * qd
                        rk = pm * kj
                        ab[p] = ab[p] + rk
                        sk = sk + jnp.sum(pm, axis=0, keepdims=True)
                        sb = sb + jnp.sum(rk, axis=0, keepdims=True)
                        sv = sv + jnp.sum(a_j * dop, axis=0, keepdims=True)
                    pj = j // 8
                    here = _iota_rows(8) == (j - 8 * pj)
                    ak[pj] = ak[pj] + jnp.where(here, sk, 0.0)
                    av[pj] = av[pj] + jnp.where(here, sv, 0.0)
                    ab[pj] = ab[pj] - jnp.where(here, sb, 0.0)
                dq_d.extend(aq)
                dk_d.extend(ak)
                dv_d.extend(av)
                db_d.extend(ab)
            dq = dq + jnp.concatenate(dq_d, axis=0)
            dk = dk + jnp.concatenate(dk_d, axis=0)
            dv = dv + jnp.concatenate(dv_d, axis=0)
            db = db + jnp.concatenate(db_d, axis=0)
            db = db + jnp.where(_iota_rows(CHUNK) == CHUNK - 1, db_last, 0.0)
            dgl = _cumsum_bwd(db)
            dfv = dgl / f - dk
            dlb_s[...] += jnp.sum(dfv * (1.0 - sg), axis=0, keepdims=True)
            df_ref[rows, :] = (dfv * (1.0 - lb) * (sg * (1.0 - sg))).astype(BF16)
            dq_ref[rows, :] = (dq * _dsilu(qr, sq)).astype(BF16)
            di_ref[rows, :] = dv.astype(BF16)
            return carry

        lax.fori_loop(0, nc, chunk, 0)

        @pl.when(bi == nb - 1)
        def _():
            dgam = dlb_s[...] * lb * (1.0 - lb)
            small_ref[...] = jnp.zeros_like(small_ref)
            small_ref[0:1, :] = dgam
            small_ref[1:2, :] = -dgam
            small_ref[2:3, :] = dwn_s[...]

    def col(off):
        return pl.BlockSpec((seq, HEAD), lambda h, b: (b, off * nh + h))

    vec = lambda r: pl.BlockSpec((r, HEAD), lambda h, b: (0, h))
    blk = pl.BlockSpec((seq, HEAD), lambda h, b: (b, h))
    dshape = jax.ShapeDtypeStruct((t, w), BF16)
    return _pcall(
        body, n_in=9, args=[proj, proj, proj, proj, lbg, nw, opre, states, dmix], deps=deps,
        name=name, grid=(nh, nb),
        in_specs=[col(0), col(1), col(2), col(3), vec(2), vec(1), blk,
                  pl.BlockSpec((None, None, nc, HEAD, HEAD), lambda h, b: (b, h, 0, 0, 0)), blk],
        out_specs=[blk, blk, blk, blk, vec(8)],
        out_shape=[dshape, dshape, dshape, dshape, jax.ShapeDtypeStruct((8, w), F32)],
        scratch_shapes=[pltpu.VMEM((CHUNK, HEAD), F32)] * 3 + [pltpu.VMEM((HEAD, HEAD), F32)]
        + [pltpu.VMEM((1, HEAD), F32)] * 2,
        compiler_params=_cp(dimension_semantics=("parallel", "arbitrary")))


def _expm1(x):
    poly = x * (1.0 + x * (0.5 + x * (1.0 / 6 + x * (1.0 / 24 + x * (1.0 / 120 + x * (1.0 / 720))))))
    return jnp.where(jnp.abs(x) < 0.25, poly, jnp.exp(x) - 1.0)


def _softplus_neg(lam):
    x = -lam
    e = jnp.exp(-jnp.abs(x))
    u = 1.0 + e
    l1p = jnp.where(u == 1.0, e, jnp.log(u) * (e / jnp.where(u == 1.0, 1.0, u - 1.0)))
    return jnp.maximum(x, 0.0) + l1p


_GELU_C = math.sqrt(2.0 / math.pi)


def _gelu(y):
    return 0.5 * y * (1.0 + jnp.tanh(_GELU_C * (y + 0.044715 * (y * y * y))))


def _dgelu(y):
    th = jnp.tanh(_GELU_C * (y + 0.044715 * (y * y * y)))
    return 0.5 * (1.0 + th) + 0.5 * y * (1.0 - th * th) * (_GELU_C * (1.0 + 3 * 0.044715 * (y * y)))


def _lru_gates(xr, prev8, cw_ref, cb, wa_ref, ba, wx_ref, bx, sp, first):
    x3, x2, x1 = _shift_down(prev8, xr, 3), _shift_down(prev8, xr, 2), _shift_down(prev8, xr, 1)
    xb = cb + x3 * cw_ref[0:1, :]
    xb = xb + x2 * cw_ref[1:2, :]
    xb = xb + x1 * cw_ref[2:3, :]
    xb = xb + xr * cw_ref[3:4, :]
    r = _sigmoid(_dot(xb, wa_ref[...], NN) + ba)
    ig = _sigmoid(_dot(xb, wx_ref[...], NN) + bx)
    la = (-LRU_C * r) * sp
    a = jnp.exp(la)
    start = jnp.logical_and(first, _iota_rows(xr.shape[0]) == 0)
    mult = jnp.where(start, 1.0, jnp.sqrt(-_expm1(2.0 * la)))
    return (x3, x2, x1), xb, r, ig, a, mult, start


def _lru_fwd(proj, cw, cb, wa, ba, wx, bx, lam, *, nb, seq, name, deps=()):
    t = proj.shape[0]
    w = cb.shape[1]
    nblk = w // HEAD
    nc = seq // CHUNK

    def body(x_ref, y_ref, cw_ref, cb_ref, wa_ref, ba_ref, wx_ref, bx_ref, lam_ref, p_ref, h_ref):
        sp = _softplus_neg(lam_ref[...])
        cb_v, ba_v, bx_v = cb_ref[...], ba_ref[...], bx_ref[...]

        def tile(c, carry):
            hc, prev8 = carry
            base = pl.multiple_of(c * CHUNK, CHUNK)
            rows = pl.ds(base, CHUNK)
            xr = x_ref[rows, :]
            _, xb, _, ig, a, mult, _ = _lru_gates(xr, prev8, cw_ref, cb_v, wa_ref, ba_v, wx_ref, bx_v, sp, c == 0)
            ap, up = _scan_fwd(a, xb * ig * mult)
            h = up + ap * hc
            h_ref[rows, :] = h
            p_ref[rows, :] = h * _gelu(y_ref[rows, :])
            return h_ref[pl.ds(base + CHUNK - 1, 1), :], xr[CHUNK - 8:]

        lax.fori_loop(0, nc, tile, (jnp.zeros((1, HEAD), F32), jnp.zeros((8, HEAD), F32)))

    def col(off):
        return pl.BlockSpec((seq, HEAD), lambda n, b: (b, off * nblk + n))

    vec = lambda r: pl.BlockSpec((r, HEAD), lambda n, b: (0, n))
    mat = pl.BlockSpec((None, HEAD, HEAD), lambda n, b: (n, 0, 0))
    blk = pl.BlockSpec((seq, HEAD), lambda n, b: (b, n))
    return _pcall(
        body, n_in=9, args=[proj, proj, cw, cb, wa, ba, wx, bx, lam], deps=deps, name=name, grid=(nblk, nb),
        in_specs=[col(4), col(5), vec(4), vec(1), mat, vec(1), mat, vec(1), vec(1)],
        out_specs=[blk, blk],
        out_shape=[jax.ShapeDtypeStruct((t, w), F32), jax.ShapeDtypeStruct((t, w), F32)],
        compiler_params=_cp(dimension_semantics=("parallel", "parallel")))


def _lru_bwd(proj, cw, cb, wa, ba, wx, bx, lam, hsv, dp, *, nb, seq, name):
    t = proj.shape[0]
    w = cb.shape[1]
    nblk = w // HEAD
    nc = seq // CHUNK

    def body(x_ref, y_ref, cw_ref, cb_ref, wa_ref, ba_ref, wx_ref, bx_ref, lam_ref, h_ref, dp_ref,
             dx_ref, dy_ref, small_ref, dwa_ref, dwx_ref, a_c, g_c, acc, dwa_s, dwx_s):
        bi = pl.program_id(1)
        lamv = lam_ref[...]
        sp = _softplus_neg(lamv)
        cb_v, ba_v, bx_v = cb_ref[...], ba_ref[...], bx_ref[...]

        @pl.when(bi == 0)
        def _():
            acc[...] = jnp.zeros_like(acc)
            dwa_s[...] = jnp.zeros_like(dwa_s)
            dwx_s[...] = jnp.zeros_like(dwx_s)

        def tile(it, carry):
            g_next, a_next, dxb_next8 = carry
            c = nc - 1 - it
            base = pl.multiple_of(c * CHUNK, CHUNK)
            rows = pl.ds(base, CHUNK)
            before = pl.ds(pl.multiple_of(jnp.maximum(base - 8, 0), 8), 8)
            inner = c > 0
            xr = x_ref[rows, :]
            prev8 = jnp.where(inner, x_ref[before, :], 0.0)
            (x3, x2, x1), xb, r, ig, a, mult, start = _lru_gates(
                xr, prev8, cw_ref, cb_v, wa_ref, ba_v, wx_ref, bx_v, sp, c == 0)
            h = h_ref[rows, :]
            h_m1 = _shift_down(jnp.where(inner, h_ref[before, :], 0.0), h, 1)
            yv = y_ref[rows, :]
            dpv = dp_ref[rows, :]
            dy_ref[rows, :] = (dpv * h * _dgelu(yv)).astype(BF16)
            dh = dpv * _gelu(yv)
            a_up = _shift_up(a, jnp.broadcast_to(a_next, (8, HEAD)), 1)
            ap, gp = _scan_bwd(a_up, dh)
            g = gp + ap * g_next
            a_c[...] = a
            g_c[...] = g
            da = g * h_m1
            gx = g * xb
            dxb = g * ig * mult
            dig = gx * mult
            dmult = jnp.where(start, 0.0, gx * ig)
            dla = da * a - dmult * (a * a) / mult
            dzr = (dla * (-LRU_C * sp)) * (r * (1.0 - r))
            dzi = dig * (ig * (1.0 - ig))
            dxb = dxb + _dot(dzr, wa_ref[...], NT) + _dot(dzi, wx_ref[...], NT)
            dwa_s[...] += _dot(xb, dzr, TN)
            dwx_s[...] += _dot(xb, dzi, TN)
            d1, d2, d3 = (_shift_up(dxb, dxb_next8, s) for s in (1, 2, 3))
            dx = dxb * cw_ref[3:4, :] + d1 * cw_ref[2:3, :] + d2 * cw_ref[1:2, :] + d3 * cw_ref[0:1, :]
            dx_ref[rows, :] = dx.astype(BF16)
            colsum = lambda z: jnp.sum(z, axis=0, keepdims=True)
            acc[0:1, :] += colsum(x3 * dxb)
            acc[1:2, :] += colsum(x2 * dxb)
            acc[2:3, :] += colsum(x1 * dxb)
            acc[3:4, :] += colsum(xr * dxb)
            acc[4:5, :] += colsum(dxb)
            acc[5:6, :] += colsum(dzr)
            acc[6:7, :] += colsum(dzi)
            acc[7:8, :] += colsum(dla * (-LRU_C * r))
            return g_c[0:1, :], a_c[0:1, :], dxb[0:8]

        zero = jnp.zeros((1, HEAD), F32)
        lax.fori_loop(0, nc, tile, (zero, zero, jnp.zeros((8, HEAD), F32)))

        @pl.when(bi == nb - 1)
        def _():
            small_ref[...] = acc[...]
            small_ref[7:8, :] = acc[7:8, :] * (-_sigmoid(-lamv))
            dwa_ref[...] = dwa_s[...]
            dwx_ref[...] = dwx_s[...]

    def col(off):
        return pl.BlockSpec((seq, HEAD), lambda n, b: (b, off * nblk + n))

    vec = lambda r: pl.BlockSpec((r, HEAD), lambda n, b: (0, n))
    mat = pl.BlockSpec((None, HEAD, HEAD), lambda n, b: (n, 0, 0))
    blk = pl.BlockSpec((seq, HEAD), lambda n, b: (b, n))
    dshape = jax.ShapeDtypeStruct((t, w), BF16)
    return pl.pallas_call(
        body, name=name, grid=(nblk, nb),
        in_specs=[col(4), col(5), vec(4), vec(1), mat, vec(1), mat, vec(1), vec(1), blk, blk],
        out_specs=[blk, blk, vec(8), mat, mat],
        out_shape=[dshape, dshape, jax.ShapeDtypeStruct((8, w), F32),
                   jax.ShapeDtypeStruct((nblk, HEAD, HEAD), F32), jax.ShapeDtypeStruct((nblk, HEAD, HEAD), F32)],
        scratch_shapes=[pltpu.VMEM((CHUNK, HEAD), F32)] * 2 + [pltpu.VMEM((8, HEAD), F32)]
        + [pltpu.VMEM((HEAD, HEAD), F32)] * 2,
        compiler_params=_cp(dimension_semantics=("parallel", "arbitrary")),
    )(proj, proj, cw, cb, wa, ba, wx, bx, lam, hsv, dp)


def _ffn_conv(x, prev8, cw_ref, cbv):
    x2, x1 = _shift_down(prev8, x, 2), _shift_down(prev8, x, 1)
    y = cbv + x2 * cw_ref[0:1, :]
    y = y + x1 * cw_ref[1:2, :]
    y = y + x * cw_ref[2:3, :]
    return (x2, x1), y


def _ffn_act_fwd(up, cw, cb, *, nb, seq, name):
    t, f2 = up.shape
    f = f2 // 2
    tc = _tile(f, 256)
    nj = f // tc
    rt = _tile(seq, 256, 16)
    nr = seq // rt

    def body(g_ref, v_ref, cwg_ref, cwv_ref, cbg_ref, cbv_ref, o_ref):
        cbg, cbv = cbg_ref[...], cbv_ref[...]

        def tile(c, carry):
            pg, pv = carry
            rows = pl.ds(pl.multiple_of(c * rt, rt), rt)
            xg, xv = g_ref[rows, :], v_ref[rows, :]
            _, gate = _ffn_conv(xg, pg, cwg_ref, cbg)
            _, val = _ffn_conv(xv, pv, cwv_ref, cbv)
            o_ref[rows, :] = ((gate * _sigmoid(gate)) * val).astype(BF16)
            return xg[rt - 8:], xv[rt - 8:]

        z = jnp.zeros((8, tc), F32)
        lax.fori_loop(0, nr, tile, (z, z))

    gcol = pl.BlockSpec((seq, tc), lambda j, b: (b, j))
    vcol = pl.BlockSpec((seq, tc), lambda j, b: (b, nj + j))
    gv = lambda r: pl.BlockSpec((r, tc), lambda j, b: (0, j))
    vv = lambda r: pl.BlockSpec((r, tc), lambda j, b: (0, nj + j))
    return pl.pallas_call(
        body, name=name, grid=(nj, nb),
        in_specs=[gcol, vcol, gv(3), vv(3), gv(1), vv(1)], out_specs=gcol,
        out_shape=jax.ShapeDtypeStruct((t, f), BF16),
        compiler_params=_cp(dimension_semantics=("parallel", "parallel")),
    )(up, up, cw, cw, cb, cb)


def _ffn_act_bwd(dact, up, cw, cb, *, nb, seq, name, deps=()):
    t, f2 = up.shape
    f = f2 // 2
    tc = _tile(f, 256)
    nj = f // tc
    rt = _tile(seq, 256, 16)
    nr = seq // rt

    def body(da_ref, g_ref, v_ref, cwg_ref, cwv_ref, cbg_ref, cbv_ref,
             dg_ref, dv_ref, sg_ref, sv_ref, eg, ev, accg, accv):
        bi = pl.program_id(1)
        cbg, cbv = cbg_ref[...], cbv_ref[...]

        @pl.when(bi == 0)
        def _():
            accg[...] = jnp.zeros_like(accg)
            accv[...] = jnp.zeros_like(accv)

        colsum = lambda z: jnp.sum(z, axis=0, keepdims=True)

        def first(c, carry):
            pg, pv = carry
            rows = pl.ds(pl.multiple_of(c * rt, rt), rt)
            xg, xv = g_ref[rows, :], v_ref[rows, :]
            (g2, g1), gate = _ffn_conv(xg, pg, cwg_ref, cbg)
            (v2, v1), val = _ffn_conv(xv, pv, cwv_ref, cbv)
            s = _sigmoid(gate)
            da = da_ref[rows, :]
            dgate = da * val * _dsilu(gate, s)
            dval = da * (gate * s)
            eg[rows, :] = dgate
            ev[rows, :] = dval
            for acc, (s2, s1, s0), d in ((accg, (g2, g1, xg), dgate), (accv, (v2, v1, xv), dval)):
                acc[0:1, :] += colsum(s2 * d)
                acc[1:2, :] += colsum(s1 * d)
                acc[2:3, :] += colsum(s0 * d)
                acc[3:4, :] += colsum(d)
            return xg[rt - 8:], xv[rt - 8:]

        z = jnp.zeros((8, tc), F32)
        lax.fori_loop(0, nr, first, (z, z))

        def second(c, carry):
            base = pl.multiple_of(c * rt, rt)
            rows = pl.ds(base, rt)
            after = pl.ds(pl.multiple_of(jnp.minimum(base + rt, seq - 8), 8), 8)
            more = c < nr - 1
            for e, cw_ref, d_ref in ((eg, cwg_ref, dg_ref), (ev, cwv_ref, dv_ref)):
                d = e[rows, :]
                n8 = jnp.where(more, e[after, :], 0.0)
                d1, d2 = _shift_up(d, n8, 1), _shift_up(d, n8, 2)
                d_ref[rows, :] = (d * cw_ref[2:3, :] + d1 * cw_ref[1:2, :] + d2 * cw_ref[0:1, :]).astype(BF16)
            return carry

        lax.fori_loop(0, nr, second, 0)

        @pl.when(bi == nb - 1)
        def _():
            sg_ref[...] = accg[...]
            sv_ref[...] = accv[...]

    gcol = pl.BlockSpec((seq, tc), lambda j, b: (b, j))
    vcol = pl.BlockSpec((seq, tc), lambda j, b: (b, nj + j))
    gv = lambda r: pl.BlockSpec((r, tc), lambda j, b: (0, j))
    vv = lambda r: pl.BlockSpec((r, tc), lambda j, b: (0, nj + j))
    dshape = jax.ShapeDtypeStruct((t, f), BF16)
    sshape = jax.ShapeDtypeStruct((8, f), F32)
    return _pcall(
        body, n_in=7, args=[dact, up, up, cw, cw, cb, cb], deps=deps, name=name, grid=(nj, nb),
        in_specs=[gcol, gcol, vcol, gv(3), vv(3), gv(1), vv(1)],
        out_specs=[gcol, gcol, gv(8), gv(8)], out_shape=[dshape, dshape, sshape, sshape],
        scratch_shapes=[pltpu.VMEM((seq, tc), F32)] * 2 + [pltpu.VMEM((8, tc), F32)] * 2,
        compiler_params=_cp(dimension_semantics=("parallel", "arbitrary")))


def _adamw_math(wv, g, mv, vv):
    m = ADAM_B1 * mv + (1.0 - ADAM_B1) * g
    v = ADAM_B2 * vv + (1.0 - ADAM_B2) * (g * g)
    m_hat = m / (1.0 - ADAM_B1 ** ADAM_STEP)
    v_hat = v / (1.0 - ADAM_B2 ** ADAM_STEP)
    delta = -ADAM_LR * (m_hat / (jnp.sqrt(v_hat) + ADAM_EPS) + ADAM_WD * wv)
    return delta, m, v


def _adamw_shard(wv, mv, vv, p_own, rb, *, name):
    _, r, cdim = wv.shape
    tr = _row_tile(r, cdim)

    def body(w_ref, m_ref, v_ref, p_ref, rb_ref, g_ref, d_ref, mo_ref, vo_ref):
        g = p_ref[...].astype(F32)
        for k in range(3):
            g = g + rb_ref[k].astype(F32)
        g_ref[...] = g
        d_ref[...], mo_ref[...], vo_ref[...] = _adamw_math(w_ref[...], g, m_ref[...], v_ref[...])

    row = pl.BlockSpec((tr, cdim), lambda i: (i, 0))
    row3 = pl.BlockSpec((None, tr, cdim), lambda i: (0, i, 0))
    shp = jax.ShapeDtypeStruct((1, r, cdim), F32)
    return pl.pallas_call(
        body, name=name, grid=(r // tr,),
        in_specs=[row3, row3, row3, row, pl.BlockSpec((3, tr, cdim), lambda i: (0, i, 0))],
        out_specs=[row3] * 4, out_shape=[shp] * 4, compiler_params=_cp(dimension_semantics=("parallel",)),
    )(wv, mv, vv, p_own, rb)


def _adamw_packed(wv, g, mv, vv, *, name):
    r = wv.shape[0]
    tr = _tile(r, 256, 8)

    def body(w_ref, g_ref, m_ref, v_ref, d_ref, mo_ref, vo_ref):
        d_ref[...], mo_ref[...], vo_ref[...] = _adamw_math(w_ref[...], g_ref[...], m_ref[...], v_ref[...])

    row = pl.BlockSpec((tr, LANE), lambda i: (i, 0))
    shp = jax.ShapeDtypeStruct((r, LANE), F32)
    return pl.pallas_call(
        body, name=name, grid=(r // tr,), in_specs=[row] * 4, out_specs=[row] * 3, out_shape=[shp] * 3,
        compiler_params=_cp(dimension_semantics=("parallel",)),
    )(wv, g, mv, vv)


def _place():
    return lax.axis_index("x"), lax.axis_index("y"), lax.axis_index("c")


def _all_reduce_packed(p, *, name):
    r = p.shape[0]
    tr = _tile(r, 256, 8)

    def body(p_ref, o_ref, buf, send, recv, lsem):
        x, y, c = _place()
        me, sibling = (x, y, c), (x, y, 1 - c)
        chips = [(1 - x, y), (x, 1 - y), (1 - x, 1 - y)]

        def slot(px, py, pc):
            return 4 * px + 2 * py + pc

        def copy(k, block, to, src=None):
            dst = buf.at[slot(*block)]
            return pltpu.make_async_remote_copy(
                src_ref=dst if src is None else src, dst_ref=dst, send_sem=send.at[k], recv_sem=recv.at[k],
                device_id=to, device_id_type=MESH)

        mine = pltpu.make_async_copy(p_ref, buf.at[slot(*me)], lsem)
        mine.start()
        first = [copy(0, me, sibling, src=p_ref)]
        first += [copy(1 + j, me, (*chip, c), src=p_ref) for j, chip in enumerate(chips)]
        for cp in first:
            cp.start()
        passed = [copy(4 + j, (*chip, c), sibling) for j, chip in enumerate(chips)]
        for j, chip in enumerate(chips):
            copy(1 + j, (*chip, c), me).wait_recv()
            passed[j].start()
        copy(0, sibling, me).wait_recv()
        for j, chip in enumerate(chips):
            copy(4 + j, (*chip, 1 - c), me).wait_recv()
        for cp in first + passed:
            cp.wait_send()
        mine.wait()

        def add(i, carry):
            rows = pl.ds(pl.multiple_of(i * tr, tr), tr)
            s = buf[0, rows, :]
            for d in range(1, NDEV):
                s = s + buf[d, rows, :]
            o_ref[rows, :] = s
            return carry

        lax.fori_loop(0, r // tr, add, 0)

    vm = pl.BlockSpec(memory_space=pltpu.VMEM)
    return pl.pallas_call(
        body, name=name, in_specs=[vm], out_specs=vm, out_shape=jax.ShapeDtypeStruct(p.shape, p.dtype),
        scratch_shapes=[pltpu.VMEM((NDEV,) + p.shape, p.dtype), pltpu.SemaphoreType.DMA((7,)),
                        pltpu.SemaphoreType.DMA((7,)), pltpu.SemaphoreType.DMA],
        compiler_params=_cp(),
    )(p)


HBM = pl.BlockSpec(memory_space=pltpu.HBM)
SEM = pl.BlockSpec(memory_space=pltpu.SEMAPHORE)
EFFECT = pltpu.SideEffectType.DATAFLOW_SIDE_EFFECTING


def _plan_copies(plan, s_refs, l_refs, send, recv):
    def pick(kind, a, idx):
        ref = (s_refs if kind == "s" else l_refs)[a]
        return ref if idx is None else ref.at[idx]

    return [pltpu.make_async_remote_copy(
        src_ref=pick(*src), dst_ref=pick(*dst), send_sem=send.at[i], recv_sem=recv.at[i],
        device_id=to, device_id_type=MESH) for i, (src, dst, to) in enumerate(plan(*_place()))]


def _xfer_start(srcs, lands, plan, *, name, deps=()):
    ns, nl = len(srcs), len(lands)
    nd = len(deps)
    ncopy = len(plan(0, 0, 0))

    def body(*refs):
        s_refs, l_refs = refs[:ns], refs[ns:ns + nl]
        send, recv = refs[ns + nl + nd], refs[ns + nl + nd + 1]
        token = refs[-1]
        for cp in _plan_copies(plan, s_refs, l_refs, send, recv):
            cp.start()
        token[...] = jnp.zeros_like(token)

    bufs = list(srcs) + list(lands)
    outs = pl.pallas_call(
        body, name=name,
        out_shape=(pltpu.SemaphoreType.DMA((ncopy,)), pltpu.SemaphoreType.DMA((ncopy,)),
                   *[pltpu.HBM(b.shape, b.dtype) for b in bufs], jax.ShapeDtypeStruct((8, LANE), F32)),
        in_specs=[HBM] * (ns + nl) + [ANY] * nd,
        out_specs=(SEM, SEM, *[HBM] * (ns + nl), pl.BlockSpec(memory_space=pltpu.VMEM)),
        input_output_aliases={i: 2 + i for i in range(ns + nl)},
        compiler_params=pltpu.CompilerParams(has_side_effects=EFFECT),
    )(*[pltpu.with_memory_space_constraint(b, pltpu.HBM) for b in bufs], *deps)
    return outs[0], outs[1], list(outs[2:2 + ns]), list(outs[2 + ns:2 + ns + nl]), outs[-1]


def _xfer_wait(send, recv, srcs, lands, plan, after, *, name):
    ns, nl = len(srcs), len(lands)

    def body(*refs):
        s_refs, l_refs = refs[:ns], refs[ns:ns + nl]
        send_ref, recv_ref = refs[ns + nl], refs[ns + nl + 1]
        for cp in _plan_copies(plan, s_refs, l_refs, send_ref, recv_ref):
            cp.wait_send()
            cp.wait_recv()

    bufs = list(srcs) + list(lands)
    outs = pl.pallas_call(
        body, name=name, out_shape=tuple(pltpu.HBM(b.shape, b.dtype) for b in bufs),
        in_specs=[HBM] * (ns + nl) + [SEM, SEM, ANY], out_specs=tuple([HBM] * (ns + nl)),
        input_output_aliases={i: i for i in range(ns + nl)},
        compiler_params=pltpu.CompilerParams(has_side_effects=EFFECT),
    )(*bufs, send, recv, after)
    return list(outs[:ns]), list(outs[ns:])


def _slot8(px, py, pc):
    return 4 * px + 2 * py + pc


def _slot4(r):
    return lambda px, py, pc: (2 * px + py, pl.ds(pc * r, r))


def _plan_gather_first(slots):
    def plan(x, y, c):
        peers = [(x, y, c), (x, y, 1 - c), (1 - x, y, c), (x, 1 - y, c), (1 - x, 1 - y, c)]
        return [(("s", a, None), ("l", a, slot(x, y, c)), to) for a, slot in enumerate(slots) for to in peers]
    return plan


def _plan_gather_pass(slots):
    def plan(x, y, c):
        chips = [(1 - x, y), (x, 1 - y), (1 - x, 1 - y)]
        return [(("l", a, slot(px, py, c)), ("l", a, slot(px, py, c)), (x, y, 1 - c))
                for a, slot in enumerate(slots) for px, py in chips]
    return plan


def _plan_rs_sibling(r):
    def plan(x, y, c):
        return [(("s", 0, (j, pl.ds((1 - c) * r, r))), ("l", 0, j), (x, y, 1 - c)) for j in range(4)]
    return plan


def _plan_rs_plane():
    def plan(x, y, c):
        chips = [(1 - x, y), (x, 1 - y), (1 - x, 1 - y)]
        return [(("s", 0, 2 * px + py), ("l", 0, k), (px, py, c)) for k, (px, py) in enumerate(chips)]
    return plan


def _add_pairs(g, rcv, cidx, *, name):
    _, r, cdim = rcv.shape
    tr = _row_tile(r, cdim, budget=1048576)
    per = r // tr

    def body(c_ref, g_ref, r_ref, o_ref):
        o_ref[...] = (g_ref[...].astype(F32) + r_ref[...].astype(F32)).astype(o_ref.dtype)

    grid_spec = pltpu.PrefetchScalarGridSpec(
        num_scalar_prefetch=1, grid=(4, per),
        in_specs=[pl.BlockSpec((None, tr, cdim), lambda j, i, c_ref: (j, c_ref[0] * per + i, 0)),
                  pl.BlockSpec((None, tr, cdim), lambda j, i, c_ref: (j, i, 0))],
        out_specs=pl.BlockSpec((None, tr, cdim), lambda j, i, c_ref: (j, i, 0)))
    return pl.pallas_call(
        body, name=name, grid_spec=grid_spec, out_shape=jax.ShapeDtypeStruct((4, r, cdim), g.dtype),
        compiler_params=_cp(dimension_semantics=("parallel", "parallel")),
    )(cidx, g, rcv)


def _pack(arrs):
    flat = jnp.concatenate([a.reshape(-1).astype(F32) for a in arrs])
    n = flat.shape[0]
    rows = -(-n // LANE)
    rows = -(-rows // 256) * 256
    return jnp.pad(flat, (0, rows * LANE - n)).reshape(rows, LANE)


def _unpack(packed, shapes):
    flat = packed.reshape(-1)
    out, o = [], 0
    for s in shapes:
        n = math.prod(s)
        out.append(flat[o:o + n].reshape(s))
        o += n
    return out


def _pad_blocks(a, w, wp):
    lead = a.shape[:-1]
    k = a.shape[-1] // w
    pads = [(0, 0)] * (len(lead) + 1) + [(0, wp - w)]
    return jnp.pad(a.reshape(*lead, k, w), pads).reshape(*lead, k * wp)


def kernel(x, ln1_w, w_in, lb_gamma, hg_norm_w, lru_conv_w, lru_conv_b, lru_wa, lru_ba, lru_wx, lru_bx, lru_lambda, lru_norm_w, w_out, ln2_w, ffn_w_up, ffn_conv_w, ffn_conv_b, ffn_w_down, final_norm_w, loss_target, m_ln1_w, m_w_in, m_lb_gamma, m_hg_norm_w, m_lru_conv_w, m_lru_conv_b, m_lru_wa, m_lru_ba, m_lru_wx, m_lru_bx, m_lru_lambda, m_lru_norm_w, m_w_out, m_ln2_w, m_ffn_w_up, m_ffn_conv_w, m_ffn_conv_b, m_ffn_w_down, m_final_norm_w, v_ln1_w, v_w_in, v_lb_gamma, v_hg_norm_w, v_lru_conv_w, v_lru_conv_b, v_lru_wa, v_lru_ba, v_lru_wx, v_lru_bx, v_lru_lambda, v_lru_norm_w, v_w_out, v_ln2_w, v_ffn_w_up, v_ffn_conv_w, v_ffn_conv_b, v_ffn_w_down, v_final_norm_w):
    nb, seq, d = x.shape
    t = nb * seq
    wmix = d // 2
    in_sh = w_in.shape[2]
    up_sh = ffn_w_up.shape[2]
    up_pad = -(-up_sh // LANE) * LANE
    hs = ffn_w_down.shape[1]
    fpad = 4 * up_pad
    cx, cy, cc = _place()
    me = 4 * cx + 2 * cy + cc
    plane = 2 * cx + cy

    def gather(shards, lands, slots, tag, deps=()):
        st = _xfer_start(shards, lands, _plan_gather_first(slots), name=f"gather_{tag}_start", deps=deps)
        return (st, slots, tag), st[4]

    def pass_on(g, after):
        st, slots, tag = g
        _, lands = _xfer_wait(st[0], st[1], st[2], st[3], _plan_gather_first(slots), after, name=f"gather_{tag}_wait")
        st2 = _xfer_start([], lands, _plan_gather_pass(slots), name=f"gather_{tag}_pass")
        return (st2, slots, tag), st2[4]

    def gathered(g, after):
        st2, slots, tag = g
        return _xfer_wait(st2[0], st2[1], [], st2[3], _plan_gather_pass(slots), after, name=f"gather_{tag}_done")[1]

    land8 = lambda s: lax.empty((NDEV,) + s.shape, s.dtype)

    win_s = w_in[0].astype(BF16)
    wout_s = w_out[0].astype(BF16)
    wup_s = jnp.pad(ffn_w_up[0], ((0, 0), (0, up_pad - up_sh))).astype(BF16)
    wdn_s = ffn_w_down[0].astype(BF16)
    fcw_s = jnp.pad(ffn_conv_w[0], ((0, 0), (0, up_pad - up_sh)))
    ga, tok = gather([win_s, lru_conv_w[0], fcw_s], [land8(win_s), land8(lru_conv_w[0]), land8(fcw_s)],
                     [_slot8] * 3, "a")
    go, tok = gather([wout_s], [land8(wout_s)], [_slot8], "o", deps=(tok,))
    gu, tok = gather([wup_s], [land8(wup_s)], [_slot8], "u", deps=(tok,))
    gd, tok = gather([wdn_s], [jnp.zeros((4, up_pad, d), BF16)], [_slot4(hs)], "d", deps=(tok,))
    fcb = _pad_blocks(ffn_conv_b, up_sh, up_pad)
    wa_b, wx_b = lru_wa[0].astype(BF16), lru_wx[0].astype(BF16)

    xf = x.reshape(t, d)
    hn = _rms_fwd(xf, ln1_w, name="ln1_fwd", deps=(tok,))
    ga, tok = pass_on(ga, hn)
    win_g, lcw_g, fcw_g = gathered(ga, tok)
    lcw = lcw_g.transpose(1, 0, 2).reshape(lru_conv_w.shape[1], wmix)
    fcw = fcw_g.transpose(1, 0, 2).reshape(ffn_conv_w.shape[1], 2 * fpad)
    proj = _mm(hn, win_g, kind="nn", out_dtype=F32, name="in_proj")
    go, tok = pass_on(go, proj)
    o_hg, o_pre, states = _hgrn_fwd(proj, lb_gamma, hg_norm_w, nb=nb, seq=seq, name="hgrn_fwd")
    (wout_g,) = gathered(go, o_pre)
    wout_f = wout_g.reshape(1, d, d)
    p_lru, h_lru = _lru_fwd(proj, lcw, lru_conv_b, wa_b, lru_ba, wx_b, lru_bx, lru_lambda,
                            nb=nb, seq=seq, name="lru_fwd", deps=(tok,))
    gu, tok = pass_on(gu, p_lru)
    o_lru = _rms_fwd(p_lru, lru_norm_w, name="lru_norm_fwd", deps=(tok,))
    mix = jnp.concatenate([o_hg, o_lru], axis=1)
    h1 = _mm(mix, wout_f, kind="nn", out_dtype=F32, name="out_proj", res=xf)
    hn2 = _rms_fwd(h1, ln2_w, name="ln2_fwd")
    (wup_g,) = gathered(gu, hn2)
    gd, tok = pass_on(gd, hn2)
    up = _mm(hn2, wup_g, kind="nn", out_dtype=F32, name="ffn_up", tn_cap=up_pad // 2, deps=(tok,))
    (wdn_g,) = gathered(gd, up)
    wdn_f = wdn_g.reshape(1, fpad, d)
    act = _ffn_act_fwd(up, fcw, fcb, nb=nb, seq=seq, name="ffn_act_fwd")
    h2 = _mm(act, wdn_f, kind="nn", out_dtype=F32, name="ffn_down", res=h1, tk_cap=up_pad // 2)

    dh2, dh2_b, g_fnw, loss_part = _loss_head(h2, final_norm_w.reshape(1, d), loss_target.reshape(t, d),
                                              name="loss_head")
    loss = lax.psum(loss_part[0, 0], ("x", "y", "c"))

    cidx = jnp.reshape(cc, (1,)).astype(jnp.int32)

    def to_sibling(g4, r, tag, deps=()):
        land = lax.empty((4, r, g4.shape[2]), g4.dtype)
        st = _xfer_start([g4], [land], _plan_rs_sibling(r), name=f"rs_sib_start_{tag}", deps=deps)
        return (st, r, tag), st[4]

    def sibling_sum(rs, after):
        st, r, tag = rs
        (g4,), (rcv,) = _xfer_wait(st[0], st[1], st[2], st[3], _plan_rs_sibling(r), after, name=f"rs_sib_wait_{tag}")
        psum = _add_pairs(g4, rcv, cidx, name=f"rs_add_{tag}")
        land = lax.empty((3,) + psum.shape[1:], psum.dtype)
        st2 = _xfer_start([psum], [land], _plan_rs_plane(), name=f"rs_plane_start_{tag}")
        return (st2, tag), st2[4]

    dact = _mm(dh2_b, wdn_f, kind="nt", out_dtype=F32, name="ffn_down_dx", tm_cap=512, tn_cap=up_pad)
    g_wdn = _mm(act, dh2_b, kind="tn", out_dtype=BF16, name="ffn_down_dw", tm_cap=up_pad // 2)
    rs_d, tok = to_sibling(g_wdn.reshape(4, up_pad, d), hs, "down")
    dup_g, dup_v, s_g, s_v = _ffn_act_bwd(dact, up, fcw, fcb, nb=nb, seq=seq, name="ffn_act_bwd", deps=(tok,))
    dup = jnp.concatenate([dup_g, dup_v], axis=1)
    pl_d, tok = sibling_sum(rs_d, dup)
    dhn2 = _mm(dup, wup_g, kind="nt", out_dtype=F32, name="ffn_up_dx", tm_cap=512, tn_cap=1024, tk_cap=up_pad,
               deps=(tok,))
    g_wup = _mm(hn2, dup, kind="tn", out_dtype=BF16, name="ffn_up_dw", out_blocks=NDEV, tn_cap=up_pad, tm_cap=512)
    rs_u, tok = to_sibling(g_wup.reshape(4, 2 * d, up_pad), d, "up")
    dh1, dh1_b, g_ln2 = _rms_bwd(dhn2, h1, ln2_w, name="ln2_bwd", extra=dh2, want_bf16=True, deps=(tok,))
    dmix = _mm(dh1_b, wout_f, kind="nt", out_dtype=F32, name="out_proj_dx")
    pl_u, tok = sibling_sum(rs_u, dmix)
    g_wout = _mm(mix, dh1_b, kind="tn", out_dtype=BF16, name="out_proj_dw", deps=(tok,))
    rs_o, tok = to_sibling(g_wout.reshape(4, d // 4, d), d // NDEV, "out")
    dp_lru, g_lnw = _rms_bwd(dmix, p_lru, lru_norm_w, name="lru_norm_bwd", dy_cb=1, deps=(tok,))
    dxr, dyr, s_lru, g_wa, g_wx = _lru_bwd(proj, lcw, lru_conv_b, wa_b, lru_ba, wx_b, lru_bx, lru_lambda,
                                           h_lru, dp_lru, nb=nb, seq=seq, name="lru_bwd")
    pl_o, tok = sibling_sum(rs_o, dxr)
    dqr, dfr, dir_, dgr, s_hg = _hgrn_bwd(proj, lb_gamma, hg_norm_w, o_pre, states, dmix,
                                          nb=nb, seq=seq, name="hgrn_bwd", deps=(tok,))
    dproj = jnp.concatenate([dqr, dfr, dir_, dgr, dxr, dyr], axis=1)
    g_win = _mm(hn, dproj, kind="tn", out_dtype=BF16, name="in_proj_dw", out_blocks=NDEV, tn_cap=in_sh)
    rs_i, tok = to_sibling(g_win.reshape(4, 2 * d, in_sh), d, "in")
    dhn = _mm(dproj, win_g, kind="nt", out_dtype=F32, name="in_proj_dx", tk_cap=768, deps=(tok,))
    grad_x, g_ln1 = _rms_bwd(dhn, xf, ln1_w, name="ln1_bwd", extra=dh1)

    fcw_parts = jnp.concatenate([s_g[0:3], s_v[0:3]], axis=1).reshape(3, NDEV, up_pad)[:, :, :up_sh]
    fcb_part = jnp.concatenate([s_g[3:4], s_v[3:4]], axis=1).reshape(NDEV, up_pad)[:, :up_sh]
    small_parts = [g_ln1, s_hg[0:2], s_hg[2:3], s_lru[0:4], s_lru[4:5], g_wa, s_lru[5:6], g_wx, s_lru[6:7],
                   s_lru[7:8], g_lnw, g_ln2, fcw_parts, fcb_part, g_fnw]
    reduced = _all_reduce_packed(_pack(small_parts), name="allreduce_small")
    pl_i, _ = sibling_sum(rs_i, reduced)
    summed = _unpack(reduced, [p.shape for p in small_parts])
    (s_ln1, s_lbg, s_hgn, s_lcw, s_lcb, s_wa, s_ba, s_wx, s_bx, s_lam, s_lnw, s_ln2, s_fcw, s_fcb, s_fnw) = summed
    sh_lcw = lru_conv_w.shape[2]
    g_small = {
        "ln1_w": s_ln1, "lb_gamma": s_lbg, "hg_norm_w": s_hgn,
        "lru_conv_w": lax.dynamic_slice_in_dim(s_lcw, me * sh_lcw, sh_lcw, axis=1),
        "lru_conv_b": s_lcb, "lru_wa": s_wa, "lru_ba": s_ba, "lru_wx": s_wx, "lru_bx": s_bx,
        "lru_lambda": s_lam, "lru_norm_w": s_lnw, "ln2_w": s_ln2,
        "ffn_conv_w": lax.dynamic_index_in_dim(s_fcw, me, 1, keepdims=False),
        "ffn_conv_b": s_fcb, "final_norm_w": s_fnw,
    }
    w_small = {"ln1_w": ln1_w, "lb_gamma": lb_gamma, "hg_norm_w": hg_norm_w, "lru_conv_w": lru_conv_w,
               "lru_conv_b": lru_conv_b, "lru_wa": lru_wa, "lru_ba": lru_ba, "lru_wx": lru_wx, "lru_bx": lru_bx,
               "lru_lambda": lru_lambda, "lru_norm_w": lru_norm_w, "ln2_w": ln2_w, "ffn_conv_w": ffn_conv_w,
               "ffn_conv_b": ffn_conv_b, "final_norm_w": final_norm_w}
    m_small = {"ln1_w": m_ln1_w, "lb_gamma": m_lb_gamma, "hg_norm_w": m_hg_norm_w, "lru_conv_w": m_lru_conv_w,
               "lru_conv_b": m_lru_conv_b, "lru_wa": m_lru_wa, "lru_ba": m_lru_ba, "lru_wx": m_lru_wx,
               "lru_bx": m_lru_bx, "lru_lambda": m_lru_lambda, "lru_norm_w": m_lru_norm_w, "ln2_w": m_ln2_w,
               "ffn_conv_w": m_ffn_conv_w, "ffn_conv_b": m_ffn_conv_b, "final_norm_w": m_final_norm_w}
    v_small = {"ln1_w": v_ln1_w, "lb_gamma": v_lb_gamma, "hg_norm_w": v_hg_norm_w, "lru_conv_w": v_lru_conv_w,
               "lru_conv_b": v_lru_conv_b, "lru_wa": v_lru_wa, "lru_ba": v_lru_ba, "lru_wx": v_lru_wx,
               "lru_bx": v_lru_bx, "lru_lambda": v_lru_lambda, "lru_norm_w": v_lru_norm_w, "ln2_w": v_ln2_w,
               "ffn_conv_w": v_ffn_conv_w, "ffn_conv_b": v_ffn_conv_b, "final_norm_w": v_final_norm_w}
    names = list(w_small)
    shapes = [w_small[k].shape for k in names]
    g_small = {k: g_small[k].reshape(w_small[k].shape) for k in names}
    packed = _adamw_packed(_pack([w_small[k] for k in names]), _pack([g_small[k] for k in names]),
                           _pack([m_small[k] for k in names]), _pack([v_small[k] for k in names]),
                           name="adamw_small")
    d_small, nm_small, nv_small = ({k: a for k, a in zip(names, _unpack(pk, shapes))} for pk in packed)

    def finish(pl_x, after, trim=lambda a: a):
        st, tag = pl_x
        (psum,), (rb,) = _xfer_wait(st[0], st[1], st[2], st[3], _plan_rs_plane(), after, name=f"rs_plane_wait_{tag}")
        return trim(lax.dynamic_index_in_dim(psum, plane, 0, keepdims=False)), trim(rb)

    upd_dn = _adamw_shard(ffn_w_down, m_ffn_w_down, v_ffn_w_down, *finish(pl_d, packed[0]), name="adamw_w_down")
    upd_up = _adamw_shard(ffn_w_up, m_ffn_w_up, v_ffn_w_up, *finish(pl_u, upd_dn[0], lambda a: a[..., :up_sh]),
                          name="adamw_w_up")
    upd_out = _adamw_shard(w_out, m_w_out, v_w_out, *finish(pl_o, upd_up[0]), name="adamw_w_out")
    upd_in = _adamw_shard(w_in, m_w_in, v_w_in, *finish(pl_i, upd_out[0]), name="adamw_w_in")

    grads, deltas, new_m, new_v = dict(g_small), dict(d_small), dict(nm_small), dict(nv_small)
    for k, upd in (("w_in", upd_in), ("w_out", upd_out), ("ffn_w_up", upd_up), ("ffn_w_down", upd_dn)):
        grads[k], deltas[k], new_m[k], new_v[k] = upd
    order = ["ln1_w", "w_in", "lb_gamma", "hg_norm_w", "lru_conv_w", "lru_conv_b", "lru_wa", "lru_ba", "lru_wx",
             "lru_bx", "lru_lambda", "lru_norm_w", "w_out", "ln2_w", "ffn_w_up", "ffn_conv_w", "ffn_conv_b",
             "ffn_w_down", "final_norm_w"]
    return (loss, grad_x.reshape(nb, seq, d), *[grads[k] for k in order], *[deltas[k] for k in order],
            *[new_m[k] for k in order], *[new_v[k] for k in order])
```

```python
import math

import jax
import jax.numpy as jnp
from jax import lax
from jax.experimental import pallas as pl
from jax.experimental.pallas import tpu as pltpu

F32, BF16 = jnp.float32, jnp.bfloat16
EPS = 1e-6
HEAD = 128
CHUNK = 64
SUB = 16
NSUB = CHUNK // SUB
LRU_C = 8.0
LANE = 128
NDEV = 8
ADAM_LR, ADAM_B1, ADAM_B2, ADAM_EPS, ADAM_WD, ADAM_STEP = 0.001, 0.9, 0.999, 1e-08, 0.01, 10
MESH = pl.DeviceIdType.MESH
ANY = pl.BlockSpec(memory_space=pl.ANY)
VMEM_LIMIT = 56 * 1024 * 1024


def _cp(**kw):
    return pltpu.CompilerParams(vmem_limit_bytes=VMEM_LIMIT, **kw)


def _tile(n, cap, mult=LANE):
    best = None
    for t in range(mult, min(n, cap) + 1, mult):
        if n % t == 0:
            best = t
    return best if best is not None else n


def _row_tile(r, cdim, budget=262144):
    return _tile(r, max(16, budget // cdim), 16)


def _sigmoid(x):
    return jax.nn.sigmoid(x)


def _dsilu(x, s):
    return s * (1.0 + x * (1.0 - s))


def _iota_rows(n, w=LANE):
    return lax.broadcasted_iota(jnp.int32, (n, w), 0)


def _lane_groups(n):
    return 2 if n % 2 == 0 else 1


def _group_views(refs, g, kinds):
    assert len(refs) == len(kinds)
    cols = pl.ds(g * HEAD, HEAD)
    return [r.at[:, cols] if kind == "l" else r.at[g] for r, kind in zip(refs, kinds)]


def _shift_down(prev8, xt, k):
    cat = jnp.concatenate([prev8, xt], axis=0)
    return pltpu.roll(cat, k, 0)[8:]


def _shift_up(xt, next8, k):
    cat = jnp.concatenate([xt, next8], axis=0)
    n = cat.shape[0]
    return pltpu.roll(cat, n - k, 0)[: xt.shape[0]]


def _scan_fwd(a, u):
    n = a.shape[0]
    row = _iota_rows(n, a.shape[1])
    k = 1
    while k < n:
        keep = row >= k
        a_s = jnp.where(keep, pltpu.roll(a, k, 0), 1.0)
        u_s = jnp.where(keep, pltpu.roll(u, k, 0), 0.0)
        u = a * u_s + u
        a = a * a_s
        k *= 2
    return a, u


def _scan_bwd(a, u):
    n = a.shape[0]
    row = _iota_rows(n, a.shape[1])
    k = 1
    while k < n:
        keep = row < n - k
        a_s = jnp.where(keep, pltpu.roll(a, n - k, 0), 1.0)
        u_s = jnp.where(keep, pltpu.roll(u, n - k, 0), 0.0)
        u = a * u_s + u
        a = a * a_s
        k *= 2
    return a, u


def _cumsum_fwd(u):
    n = u.shape[0]
    row = _iota_rows(n, u.shape[1])
    k = 1
    while k < n:
        u = u + jnp.where(row >= k, pltpu.roll(u, k, 0), 0.0)
        k *= 2
    return u


def _cumsum_bwd(u):
    n = u.shape[0]
    row = _iota_rows(n, u.shape[1])
    k = 1
    while k < n:
        u = u + jnp.where(row < n - k, pltpu.roll(u, n - k, 0), 0.0)
        k *= 2
    return u


def _dot(a, b, dims):
    return lax.dot_general(a.astype(BF16), b.astype(BF16), (dims, ((), ())), preferred_element_type=F32)


NN = ((1,), (0,))
NT = ((1,), (1,))
TN = ((0,), (0,))


def _pcall(body, *, n_in, in_specs, args, deps=(), **kw):
    nd = len(deps)
    if nd:
        inner = body

        def body(*refs):
            return inner(*refs[:n_in], *refs[n_in + nd:])

        in_specs = list(in_specs) + [ANY] * nd
        args = list(args) + list(deps)
    return pl.pallas_call(body, in_specs=in_specs, **kw)(*args)


def _mm(a, b3, *, kind, out_dtype, name, res=None, tm_cap=1024, tn_cap=1536, tk_cap=1024, out_blocks=1, deps=()):
    if kind == "nn":
        m, kdim = a.shape
        nb, _, nsh = b3.shape
        n = nb * nsh
        tm, tn, tk = _tile(m, tm_cap), _tile(nsh, tn_cap), _tile(kdim, tk_cap)
        per = nsh // tn
        a_spec = pl.BlockSpec((tm, tk), lambda i, j, k: (i, k))
        b_spec = pl.BlockSpec((None, tk, tn), lambda i, j, k: (j // per, k, j % per))
        dims = NN
    elif kind == "nt":
        m, kdim = a.shape
        nb, n, ksh = b3.shape
        tm, tn, tk = _tile(m, tm_cap), _tile(n, tn_cap), _tile(ksh, tk_cap)
        per = ksh // tk
        a_spec = pl.BlockSpec((tm, tk), lambda i, j, k: (i, k))
        b_spec = pl.BlockSpec((None, tn, tk), lambda i, j, k: (k // per, j, k % per))
        dims = NT
    else:
        kdim, m = a.shape
        n = b3.shape[1]
        nsh = n // out_blocks
        tm, tn, tk = _tile(m, tm_cap), _tile(nsh, tn_cap), _tile(kdim, tk_cap)
        per = nsh // tn
        a_spec = pl.BlockSpec((tk, tm), lambda i, j, k: (k, i))
        b_spec = pl.BlockSpec((tk, tn), lambda i, j, k: (k, j))
        dims = TN
    nk = kdim // tk
    grid = (m // tm, n // tn, nk)

    def body(*refs):
        if res is None:
            a_ref, b_ref, o_ref, acc = refs
        else:
            a_ref, b_ref, r_ref, o_ref, acc = refs
        k = pl.program_id(2)

        @pl.when(k == 0)
        def _():
            acc[...] = jnp.zeros_like(acc)

        acc[...] += lax.dot_general(a_ref[...], b_ref[...], (dims, ((), ())), preferred_element_type=F32)

        @pl.when(k == nk - 1)
        def _():
            r = acc[...]
            if res is not None:
                r = r + r_ref[...]
            o_ref[...] = r.astype(o_ref.dtype)

    in_specs = [a_spec, b_spec]
    args = [a, b3]
    if res is not None:
        in_specs.append(pl.BlockSpec((tm, tn), lambda i, j, k: (i, j)))
        args.append(res)
    if kind == "tn":
        out_shape = jax.ShapeDtypeStruct((out_blocks, m, nsh), out_dtype)
        out_spec = pl.BlockSpec((None, tm, tn), lambda i, j, k: (j // per, i, j % per))
    else:
        out_shape = jax.ShapeDtypeStruct((m, n), out_dtype)
        out_spec = pl.BlockSpec((tm, tn), lambda i, j, k: (i, j))
    return _pcall(
        body, n_in=len(args), in_specs=in_specs, args=args, deps=deps,
        name=name, grid=grid, out_specs=out_spec, out_shape=out_shape,
        scratch_shapes=[pltpu.VMEM((tm, tn), F32)],
        compiler_params=_cp(dimension_semantics=("parallel", "parallel", "arbitrary")))


def _rms_fwd(x, w, *, name, tm=256, deps=()):
    t, d = x.shape
    tm = _tile(t, tm, 16)

    def body(x_ref, w_ref, o_ref):
        xv = x_ref[...]
        r = lax.rsqrt(jnp.mean(xv * xv, axis=-1, keepdims=True) + EPS)
        o_ref[...] = ((xv * r) * w_ref[...]).astype(o_ref.dtype)

    return _pcall(
        body, n_in=2, args=[x, w], deps=deps, name=name, grid=(t // tm,),
        in_specs=[pl.BlockSpec((tm, d), lambda i: (i, 0)), pl.BlockSpec((1, d), lambda i: (0, 0))],
        out_specs=pl.BlockSpec((tm, d), lambda i: (i, 0)),
        out_shape=jax.ShapeDtypeStruct((t, d), BF16), compiler_params=_cp())


def _rms_bwd(dy, x, w, *, name, extra=None, dy_cb=0, want_bf16=False, tm=256, deps=()):
    t, d = x.shape
    tm = _tile(t, tm, 16)

    def body(*refs):
        refs = list(refs)
        dy_ref, x_ref, w_ref = refs[:3]
        e_ref = refs[3] if extra is not None else None
        outs = refs[4:] if extra is not None else refs[3:]
        dx_ref = outs[0]
        dxb_ref = outs[1] if want_bf16 else None
        dw_ref = outs[-1]
        i = pl.program_id(0)
        xv = x_ref[...]
        r = lax.rsqrt(jnp.mean(xv * xv, axis=-1, keepdims=True) + EPS)
        nh = xv * r
        dyv = dy_ref[...]
        dn = dyv * w_ref[...]
        dx = r * (dn - nh * jnp.mean(dn * nh, axis=-1, keepdims=True))
        if extra is not None:
            dx = dx + e_ref[...]
        dx_ref[...] = dx
        if want_bf16:
            dxb_ref[...] = dx.astype(BF16)
        part = jnp.sum(dyv * nh, axis=0, keepdims=True)

        @pl.when(i == 0)
        def _():
            dw_ref[...] = part

        @pl.when(i > 0)
        def _():
            dw_ref[...] += part

    row = pl.BlockSpec((tm, d), lambda i: (i, 0))
    in_specs = [pl.BlockSpec((tm, d), lambda i: (i, dy_cb)), row, pl.BlockSpec((1, d), lambda i: (0, 0))]
    args = [dy, x, w]
    if extra is not None:
        in_specs.append(row)
        args.append(extra)
    out_shape = [jax.ShapeDtypeStruct((t, d), F32)]
    out_specs = [row]
    if want_bf16:
        out_shape.append(jax.ShapeDtypeStruct((t, d), BF16))
        out_specs.append(row)
    out_shape.append(jax.ShapeDtypeStruct((1, d), F32))
    out_specs.append(pl.BlockSpec((1, d), lambda i: (0, 0)))
    return _pcall(
        body, n_in=len(args), in_specs=in_specs, args=args, deps=deps,
        name=name, grid=(t // tm,), out_specs=out_specs, out_shape=out_shape,
        compiler_params=_cp(dimension_semantics=("arbitrary",)))


def _loss_head(h, w, tgt, *, name, tm=256):
    t, d = h.shape
    tm = _tile(t, tm, 16)

    def body(h_ref, w_ref, t_ref, dh_ref, dhb_ref, dw_ref, loss_ref):
        i = pl.program_id(0)
        xv = h_ref[...]
        wv = w_ref[...]
        r = lax.rsqrt(jnp.mean(xv * xv, axis=-1, keepdims=True) + EPS)
        nh = xv * r
        e = nh * wv - t_ref[...]
        part_loss = jnp.full((1, LANE), 0.5 * jnp.sum(jnp.mean(e * e, axis=-1, keepdims=True)), F32)
        dyv = e * (1.0 / d)
        dn = dyv * wv
        dx = r * (dn - nh * jnp.mean(dn * nh, axis=-1, keepdims=True))
        dh_ref[...] = dx
        dhb_ref[...] = dx.astype(BF16)
        part = jnp.sum(dyv * nh, axis=0, keepdims=True)

        @pl.when(i == 0)
        def _():
            dw_ref[...] = part
            loss_ref[...] = part_loss

        @pl.when(i > 0)
        def _():
            dw_ref[...] += part
            loss_ref[...] += part_loss

    row = pl.BlockSpec((tm, d), lambda i: (i, 0))
    vec = pl.BlockSpec((1, d), lambda i: (0, 0))
    return pl.pallas_call(
        body, name=name, grid=(t // tm,), in_specs=[row, vec, row],
        out_specs=[row, row, vec, pl.BlockSpec((1, LANE), lambda i: (0, 0))],
        out_shape=[jax.ShapeDtypeStruct((t, d), F32), jax.ShapeDtypeStruct((t, d), BF16),
                   jax.ShapeDtypeStruct((1, d), F32), jax.ShapeDtypeStruct((1, LANE), F32)],
        compiler_params=_cp(dimension_semantics=("arbitrary",)),
    )(h, w, tgt)


def _lower_bound(lbg_ref):
    g0, g1 = lbg_ref[0:1, :], lbg_ref[1:2, :]
    m = jnp.maximum(g0, g1)
    e0, e1 = jnp.exp(g0 - m), jnp.exp(g1 - m)
    return e0 / (e0 + e1)


def _seg_bounds():
    offs, o = {}, 0
    for i in range(1, NSUB):
        offs[i] = (o, o + SUB * i)
        o += SUB * i
    return offs, o


def _pad_rows(x, n):
    if x.shape[0] == n:
        return x
    return jnp.concatenate([x, jnp.zeros((n - x.shape[0], x.shape[1]), x.dtype)], axis=0)


def _offdiag_mask():
    offs, total = _seg_bounds()
    padded = -(-total // LANE) * LANE
    rsub = lax.broadcasted_iota(jnp.int32, (CHUNK, padded), 0) // SUB
    col = lax.broadcasted_iota(jnp.int32, (CHUNK, padded), 1)
    cseg = jnp.zeros((CHUNK, padded), jnp.int32)
    for i in range(1, NSUB):
        cseg = cseg + (col >= offs[i][0]).astype(jnp.int32)
    return (rsub == cseg) & (col < total)


def _offdiag_setup(q, k, b, v, b_c, mask):
    offs, total = _seg_bounds()
    padded = -(-total // LANE) * LANE
    eq_parts = [jnp.zeros((SUB, HEAD), F32)]
    ek_parts, k_parts, v_parts = [], [], []
    for i in range(1, NSUB):
        r_i = b_c[SUB * i - 1:SUB * i, :]
        eq_parts.append(jnp.exp(b[SUB * i:SUB * (i + 1)] - r_i))
        ek_parts.append(jnp.exp(r_i - b[0:SUB * i]))
        k_parts.append(k[0:SUB * i])
        v_parts.append(v[0:SUB * i])
    eq = jnp.concatenate(eq_parts, axis=0)
    ek = _pad_rows(jnp.concatenate(ek_parts, axis=0), padded)
    kt = _pad_rows(jnp.concatenate(k_parts, axis=0), padded) * ek
    vs = _pad_rows(jnp.concatenate(v_parts, axis=0), padded)
    qt = q * eq
    a = jnp.where(mask, _dot(qt, kt, NT), 0.0)
    return offs, eq, ek, kt, vs, qt, a


def _hgrn_fwd(proj, lbg, nw, *, nb, seq, name):
    t = proj.shape[0]
    w = lbg.shape[1]
    nh = w // HEAD
    nc = seq // CHUNK
    ng = _lane_groups(nh)

    def head(mask, q_ref, f_ref, i_ref, g_ref, lbg_ref, nw_ref, ohg_ref, opre_ref, st_ref, k_c, b_c, v_c, st):
        lb = _lower_bound(lbg_ref)
        nwv = nw_ref[...]
        st[...] = jnp.zeros_like(st)

        def chunk(c, carry):
            rows = pl.ds(pl.multiple_of(c * CHUNK, CHUNK), CHUNK)
            qr = q_ref[rows, :]
            q = qr * _sigmoid(qr)
            f = lb + (1.0 - lb) * _sigmoid(f_ref[rows, :])
            k = 1.0 - f
            b = _cumsum_fwd(jnp.log(f))
            v = i_ref[rows, :]
            k_c[...] = k
            b_c[...] = b
            v_c[...] = v
            s_t = st[...]
            st_ref[c] = s_t
            o = _dot(q * jnp.exp(b), s_t, NT)
            _, _, _, _, vs, _, a = _offdiag_setup(q, k, b, v, b_c, mask)
            o = o + _dot(a, vs, NN)
            diag = []
            for i in range(NSUB):
                accs = [jnp.zeros((8, HEAD), F32) for _ in range(SUB // 8)]
                for j in range(SUB):
                    r = SUB * i + j
                    bj, kj, vj = b_c[r:r + 1, :], k_c[r:r + 1, :], v_c[r:r + 1, :]
                    for p in range(j // 8, SUB // 8):
                        lo = SUB * i + 8 * p
                        d = jnp.exp(b[lo:lo + 8] - bj)
                        if 8 * p < j:
                            d = jnp.where(_iota_rows(8) + 8 * p >= j, d, 0.0)
                        s = jnp.sum(q[lo:lo + 8] * d * kj, axis=-1, keepdims=True)
                        accs[p] = accs[p] + s * vj
                diag.extend(accs)
            o = o + jnp.concatenate(diag, axis=0)
            bl = b_c[CHUNK - 1:CHUNK, :]
            kb = k * jnp.exp(bl - b)
            st[...] = s_t * jnp.exp(bl) + _dot(v, kb, TN)
            opre_ref[rows, :] = o
            rn = lax.rsqrt(jnp.mean(o * o, axis=-1, keepdims=True) + EPS)
            gr = g_ref[rows, :]
            ohg_ref[rows, :] = (((o * rn) * nwv) * (gr * _sigmoid(gr))).astype(BF16)
            return carry

        return chunk

    def body(*refs):
        mask = _offdiag_mask()
        chunks = [head(mask, *_group_views(refs, g, "llllllll" + "ggggg")) for g in range(ng)]

        def step(c, carry):
            for chunk in chunks:
                chunk(c, carry)
            return carry

        lax.fori_loop(0, nc, step, 0)

    gw = ng * HEAD

    def col(off):
        return pl.BlockSpec((seq, gw), lambda h, b: (b, off * (nh // ng) + h))

    vec = lambda r: pl.BlockSpec((r, gw), lambda h, b: (0, h))
    out_blk = pl.BlockSpec((seq, gw), lambda h, b: (b, h))
    return pl.pallas_call(
        body, name=name, grid=(nh // ng, nb),
        in_specs=[col(0), col(1), col(2), col(3), vec(2), vec(1)],
        out_specs=[out_blk, out_blk, pl.BlockSpec((None, ng, nc, HEAD, HEAD), lambda h, b: (b, h, 0, 0, 0))],
        out_shape=[jax.ShapeDtypeStruct((t, w), BF16), jax.ShapeDtypeStruct((t, w), F32),
                   jax.ShapeDtypeStruct((nb, nh, nc, HEAD, HEAD), F32)],
        scratch_shapes=[pltpu.VMEM((ng, CHUNK, HEAD), F32)] * 3 + [pltpu.VMEM((ng, HEAD, HEAD), F32)],
        compiler_params=_cp(dimension_semantics=("parallel", "parallel")),
    )(proj, proj, proj, proj, lbg, nw)


def _hgrn_bwd(proj, lbg, nw, opre, states, dmix, *, nb, seq, name, deps=()):
    t = proj.shape[0]
    w = lbg.shape[1]
    nh = w // HEAD
    nc = seq // CHUNK
    ng = _lane_groups(nh)

    def head(mask, q_ref, f_ref, i_ref, g_ref, lbg_ref, nw_ref, opre_ref, st_ref, dm_ref,
             dq_ref, df_ref, di_ref, dg_ref, small_ref, k_c, b_c, v_c, dst, dlb_s, dwn_s):
        bi = pl.program_id(1)
        lb = _lower_bound(lbg_ref)
        nwv = nw_ref[...]
        dst[...] = jnp.zeros_like(dst)

        @pl.when(bi == 0)
        def _():
            dlb_s[...] = jnp.zeros_like(dlb_s)
            dwn_s[...] = jnp.zeros_like(dwn_s)

        def chunk(it, carry):
            c = nc - 1 - it
            rows = pl.ds(pl.multiple_of(c * CHUNK, CHUNK), CHUNK)
            qr = q_ref[rows, :]
            sq = _sigmoid(qr)
            q = qr * sq
            sg = _sigmoid(f_ref[rows, :])
            f = lb + (1.0 - lb) * sg
            k = 1.0 - f
            b = _cumsum_fwd(jnp.log(f))
            v = i_ref[rows, :]
            k_c[...] = k
            b_c[...] = b
            v_c[...] = v
            o = opre_ref[rows, :]
            rn = lax.rsqrt(jnp.mean(o * o, axis=-1, keepdims=True) + EPS)
            nhat = o * rn
            dm = dm_ref[rows, :]
            gr = g_ref[rows, :]
            sgr = _sigmoid(gr)
            dnw = dm * (gr * sgr)
            dg_ref[rows, :] = (dm * (nhat * nwv) * _dsilu(gr, sgr)).astype(BF16)
            dwn_s[...] += jnp.sum(dnw * nhat, axis=0, keepdims=True)
            dn = dnw * nwv
            do = rn * (dn - nhat * jnp.mean(dn * nhat, axis=-1, keepdims=True))
            s_t = st_ref[c]
            ds = dst[...]
            eb = jnp.exp(b)
            qb = q * eb
            bl = b_c[CHUNK - 1:CHUNK, :]
            ebl = jnp.exp(bl)
            kdec = jnp.exp(bl - b)
            kb = k * kdec
            dqb = _dot(do, s_t, NN)
            dkb = _dot(v, ds, NN)
            dv = _dot(kb, ds, NT)
            d_ebl = jnp.sum(ds * s_t, axis=0, keepdims=True)
            dst[...] = ds * ebl + _dot(do, qb, TN)
            dq = dqb * eb
            dk = dkb * kdec
            t_kb = dkb * kb
            db = dqb * qb - t_kb
            db_last = jnp.sum(t_kb, axis=0, keepdims=True) + d_ebl * ebl
            offs, eq, ek, kt, vs, qt, a = _offdiag_setup(q, k, b, v, b_c, mask)
            da = jnp.where(mask, _dot(do, vs, NT), 0.0)
            dvs = _dot(a, do, TN)
            dqt = _dot(da, kt, NN)
            dkt = _dot(da, qt, TN)
            dq = dq + dqt * eq
            db = db + dqt * qt
            gk, gb = dkt * ek, dkt * kt
            zero8 = jnp.zeros((8, HEAD), F32)
            off_k, off_v, off_b = ([zero8] * (CHUNK // 8) for _ in range(3))
            for i in range(1, NSUB):
                lo, hi = offs[i]
                for p in range((hi - lo) // 8):
                    rows8 = slice(lo + 8 * p, lo + 8 * p + 8)
                    off_k[p] = off_k[p] + gk[rows8]
                    off_b[p] = off_b[p] - gb[rows8]
                    off_v[p] = off_v[p] + dvs[rows8]
            npv = SUB // 8
            dq_d, dk_d, dv_d, db_d = [], [], [], []
            for i in range(NSUB):
                aq = [zero8 for _ in range(npv)]
                ak = [off_k[i * npv + p] for p in range(npv)]
                av = [off_v[i * npv + p] for p in range(npv)]
                ab = [off_b[i * npv + p] for p in range(npv)]
                for j in range(SUB):
                    r = SUB * i + j
                    bj, kj, vj = b_c[r:r + 1, :], k_c[r:r + 1, :], v_c[r:r + 1, :]
                    sk = jnp.zeros((1, HEAD), F32)
                    sv = jnp.zeros((1, HEAD), F32)
                    for p in range(j // 8, npv):
                        lo = SUB * i + 8 * p
                        d = jnp.exp(b[lo:lo + 8] - bj)
                        if 8 * p < j:
                            d = jnp.where(_iota_rows(8) + 8 * p >= j, d, 0.0)
                        qd = q[lo:lo + 8] * d
                        dop = do[lo:lo + 8]
                        a_j = jnp.sum(qd * kj, axis=-1, keepdims=True)
                        da_j = jnp.sum(dop * vj, axis=-1, keepdims=True)
                        aq[p] = aq[p] + (da_j * d) * kj
                        pm = da_j * qd
                        rk = pm * kj
                        ab[p] = ab[p] + rk
                        sk = sk + jnp.sum(pm, axis=0, keepdims=True)
                        sv = sv + jnp.sum(a_j * dop, axis=0, keepdims=True)
                    pj = j // 8
                    here = _iota_rows(8) == (j - 8 * pj)
                    ak[pj] = ak[pj] + jnp.where(here, sk, 0.0)
                    av[pj] = av[pj] + jnp.where(here, sv, 0.0)
                    ab[pj] = ab[pj] - jnp.where(here, sk * kj, 0.0)
                dq_d.extend(aq)
                dk_d.extend(ak)
                dv_d.extend(av)
                db_d.extend(ab)
            dq = dq + jnp.concatenate(dq_d, axis=0)
            dk = dk + jnp.concatenate(dk_d, axis=0)
            dv = dv + jnp.concatenate(dv_d, axis=0)
            db = db + jnp.concatenate(db_d, axis=0)
            db = db + jnp.where(_iota_rows(CHUNK) == CHUNK - 1, db_last, 0.0)
            dgl = _cumsum_bwd(db)
            dfv = dgl / f - dk
            dlb_s[...] += jnp.sum(dfv * (1.0 - sg), axis=0, keepdims=True)
            df_ref[rows, :] = (dfv * (1.0 - lb) * (sg * (1.0 - sg))).astype(BF16)
            dq_ref[rows, :] = (dq * _dsilu(qr, sq)).astype(BF16)
            di_ref[rows, :] = dv.astype(BF16)
            return carry

        def finish():
            @pl.when(bi == nb - 1)
            def _():
                dgam = dlb_s[...] * lb * (1.0 - lb)
                small_ref[...] = jnp.zeros_like(small_ref)
                small_ref[0:1, :] = dgam
                small_ref[1:2, :] = -dgam
                small_ref[2:3, :] = dwn_s[...]

        return chunk, finish

    def body(*refs):
        mask = _offdiag_mask()
        heads = [head(mask, *_group_views(refs, g, "lllllllgl" + "lllll" + "ggggll")) for g in range(ng)]

        def step(it, carry):
            for chunk, _ in heads:
                chunk(it, carry)
            return carry

        lax.fori_loop(0, nc, step, 0)
        for _, finish in heads:
            finish()

    gw = ng * HEAD

    def col(off):
        return pl.BlockSpec((seq, gw), lambda h, b: (b, off * (nh // ng) + h))

    vec = lambda r: pl.BlockSpec((r, gw), lambda h, b: (0, h))
    blk = pl.BlockSpec((seq, gw), lambda h, b: (b, h))
    dshape = jax.ShapeDtypeStruct((t, w), BF16)
    return _pcall(
        body, n_in=9, args=[proj, proj, proj, proj, lbg, nw, opre, states, dmix], deps=deps,
        name=name, grid=(nh // ng, nb),
        in_specs=[col(0), col(1), col(2), col(3), vec(2), vec(1), blk,
                  pl.BlockSpec((None, ng, nc, HEAD, HEAD), lambda h, b: (b, h, 0, 0, 0)), blk],
        out_specs=[blk, blk, blk, blk, vec(8)],
        out_shape=[dshape, dshape, dshape, dshape, jax.ShapeDtypeStruct((8, w), F32)],
        scratch_shapes=[pltpu.VMEM((ng, CHUNK, HEAD), F32)] * 3 + [pltpu.VMEM((ng, HEAD, HEAD), F32)]
        + [pltpu.VMEM((1, gw), F32)] * 2,
        compiler_params=_cp(dimension_semantics=("parallel", "arbitrary")))


def _expm1(x):
    poly = x * (1.0 + x * (0.5 + x * (1.0 / 6 + x * (1.0 / 24 + x * (1.0 / 120 + x * (1.0 / 720))))))
    return jnp.where(jnp.abs(x) < 0.25, poly, jnp.exp(x) - 1.0)


def _softplus_neg(lam):
    x = -lam
    e = jnp.exp(-jnp.abs(x))
    u = 1.0 + e
    l1p = jnp.where(u == 1.0, e, jnp.log(u) * (e / jnp.where(u == 1.0, 1.0, u - 1.0)))
    return jnp.maximum(x, 0.0) + l1p


_GELU_C = math.sqrt(2.0 / math.pi)


def _gelu(y):
    return 0.5 * y * (1.0 + jnp.tanh(_GELU_C * (y + 0.044715 * (y * y * y))))


def _dgelu(y):
    th = jnp.tanh(_GELU_C * (y + 0.044715 * (y * y * y)))
    return 0.5 * (1.0 + th) + 0.5 * y * (1.0 - th * th) * (_GELU_C * (1.0 + 3 * 0.044715 * (y * y)))


def _lru_gates(xr, prev8, cw_ref, cb, wa_ref, ba, wx_ref, bx, sp, first):
    x3, x2, x1 = _shift_down(prev8, xr, 3), _shift_down(prev8, xr, 2), _shift_down(prev8, xr, 1)
    xb = cb + x3 * cw_ref[0:1, :]
    xb = xb + x2 * cw_ref[1:2, :]
    xb = xb + x1 * cw_ref[2:3, :]
    xb = xb + xr * cw_ref[3:4, :]
    r = _sigmoid(_dot(xb, wa_ref[...], NN) + ba)
    ig = _sigmoid(_dot(xb, wx_ref[...], NN) + bx)
    la = (-LRU_C * r) * sp
    a = jnp.exp(la)
    start = jnp.logical_and(first, _iota_rows(xr.shape[0]) == 0)
    mult = jnp.where(start, 1.0, jnp.sqrt(-_expm1(2.0 * la)))
    return (x3, x2, x1), xb, r, ig, a, mult, start


def _lru_fwd(proj, cw, cb, wa, ba, wx, bx, lam, *, nb, seq, name, deps=()):
    t = proj.shape[0]
    w = cb.shape[1]
    nblk = w // HEAD
    nc = seq // CHUNK
    ng = _lane_groups(nblk)

    def block(x_ref, y_ref, cw_ref, cb_ref, wa_ref, ba_ref, wx_ref, bx_ref, lam_ref, p_ref, h_ref):
        sp = _softplus_neg(lam_ref[...])
        cb_v, ba_v, bx_v = cb_ref[...], ba_ref[...], bx_ref[...]

        def tile(c, carry):
            hc, prev8 = carry
            base = pl.multiple_of(c * CHUNK, CHUNK)
            rows = pl.ds(base, CHUNK)
            xr = x_ref[rows, :]
            _, xb, _, ig, a, mult, _ = _lru_gates(xr, prev8, cw_ref, cb_v, wa_ref, ba_v, wx_ref, bx_v, sp, c == 0)
            ap, up = _scan_fwd(a, xb * ig * mult)
            h = up + ap * hc
            h_ref[rows, :] = h
            p_ref[rows, :] = h * _gelu(y_ref[rows, :])
            h_last = jnp.sum(jnp.where(_iota_rows(8) == 7, h[CHUNK - 8:], 0.0), axis=0, keepdims=True)
            return h_last, xr[CHUNK - 8:]

        return tile

    def body(*refs):
        tiles = [block(*_group_views(refs, g, "llllglgll" + "ll")) for g in range(ng)]

        def step(c, carries):
            return tuple(tile(c, carry) for tile, carry in zip(tiles, carries))

        lax.fori_loop(0, nc, step, ((jnp.zeros((1, HEAD), F32), jnp.zeros((8, HEAD), F32)),) * ng)

    gw = ng * HEAD

    def col(off):
        return pl.BlockSpec((seq, gw), lambda n, b: (b, off * (nblk // ng) + n))

    vec = lambda r: pl.BlockSpec((r, gw), lambda n, b: (0, n))
    mat = pl.BlockSpec((ng, HEAD, HEAD), lambda n, b: (n, 0, 0))
    blk = pl.BlockSpec((seq, gw), lambda n, b: (b, n))
    return _pcall(
        body, n_in=9, args=[proj, proj, cw, cb, wa, ba, wx, bx, lam], deps=deps, name=name, grid=(nblk // ng, nb),
        in_specs=[col(4), col(5), vec(4), vec(1), mat, vec(1), mat, vec(1), vec(1)],
        out_specs=[blk, blk],
        out_shape=[jax.ShapeDtypeStruct((t, w), F32), jax.ShapeDtypeStruct((t, w), F32)],
        compiler_params=_cp(dimension_semantics=("parallel", "parallel")))


def _lru_bwd(proj, cw, cb, wa, ba, wx, bx, lam, hsv, dp, *, nb, seq, name):
    t = proj.shape[0]
    w = cb.shape[1]
    nblk = w // HEAD
    nc = seq // CHUNK
    ng = _lane_groups(nblk)

    def block(x_ref, y_ref, cw_ref, cb_ref, wa_ref, ba_ref, wx_ref, bx_ref, lam_ref, h_ref, dp_ref,
              dx_ref, dy_ref, small_ref, dwa_ref, dwx_ref, a_c, g_c, acc, dwa_s, dwx_s):
        bi = pl.program_id(1)
        lamv = lam_ref[...]
        sp = _softplus_neg(lamv)
        cb_v, ba_v, bx_v = cb_ref[...], ba_ref[...], bx_ref[...]

        @pl.when(bi == 0)
        def _():
            acc[...] = jnp.zeros_like(acc)
            dwa_s[...] = jnp.zeros_like(dwa_s)
            dwx_s[...] = jnp.zeros_like(dwx_s)

        def tile(it, carry):
            g_next, a_next, dxb_next8 = carry
            c = nc - 1 - it
            base = pl.multiple_of(c * CHUNK, CHUNK)
            rows = pl.ds(base, CHUNK)
            before = pl.ds(pl.multiple_of(jnp.maximum(base - 8, 0), 8), 8)
            inner = c > 0
            xr = x_ref[rows, :]
            prev8 = jnp.where(inner, x_ref[before, :], 0.0)
            (x3, x2, x1), xb, r, ig, a, mult, start = _lru_gates(
                xr, prev8, cw_ref, cb_v, wa_ref, ba_v, wx_ref, bx_v, sp, c == 0)
            h = h_ref[rows, :]
            h_m1 = _shift_down(jnp.where(inner, h_ref[before, :], 0.0), h, 1)
            yv = y_ref[rows, :]
            dpv = dp_ref[rows, :]
            dy_ref[rows, :] = (dpv * h * _dgelu(yv)).astype(BF16)
            dh = dpv * _gelu(yv)
            a_up = _shift_up(a, jnp.broadcast_to(a_next, (8, HEAD)), 1)
            ap, gp = _scan_bwd(a_up, dh)
            g = gp + ap * g_next
            a_c[...] = a
            g_c[...] = g
            da = g * h_m1
            gx = g * xb
            dxb = g * ig * mult
            dig = gx * mult
            dmult = jnp.where(start, 0.0, gx * ig)
            dla = da * a - dmult * (a * a) / mult
            dzr = (dla * (-LRU_C * sp)) * (r * (1.0 - r))
            dzi = dig * (ig * (1.0 - ig))
            dxb = dxb + _dot(dzr, wa_ref[...], NT) + _dot(dzi, wx_ref[...], NT)
            dwa_s[...] += _dot(xb, dzr, TN)
            dwx_s[...] += _dot(xb, dzi, TN)
            d1, d2, d3 = (_shift_up(dxb, dxb_next8, s) for s in (1, 2, 3))
            dx = dxb * cw_ref[3:4, :] + d1 * cw_ref[2:3, :] + d2 * cw_ref[1:2, :] + d3 * cw_ref[0:1, :]
            dx_ref[rows, :] = dx.astype(BF16)
            colsum = lambda z: jnp.sum(z, axis=0, keepdims=True)
            acc[0:1, :] += colsum(x3 * dxb)
            acc[1:2, :] += colsum(x2 * dxb)
            acc[2:3, :] += colsum(x1 * dxb)
            acc[3:4, :] += colsum(xr * dxb)
            acc[4:5, :] += colsum(dxb)
            acc[5:6, :] += colsum(dzr)
            acc[6:7, :] += colsum(dzi)
            acc[7:8, :] += colsum(dla * (-LRU_C * r))
            return g_c[0:1, :], a_c[0:1, :], dxb[0:8]

        def finish():
            @pl.when(bi == nb - 1)
            def _():
                small_ref[...] = acc[...]
                small_ref[7:8, :] = acc[7:8, :] * (-_sigmoid(-lamv))
                dwa_ref[...] = dwa_s[...]
                dwx_ref[...] = dwx_s[...]

        return tile, finish

    def body(*refs):
        blocks = [block(*_group_views(refs, g, "llllglgllll" + "lllgg" + "gglgg")) for g in range(ng)]

        def step(it, carries):
            return tuple(tile(it, carry) for (tile, _), carry in zip(blocks, carries))

        zero = jnp.zeros((1, HEAD), F32)
        lax.fori_loop(0, nc, step, ((zero, zero, jnp.zeros((8, HEAD), F32)),) * ng)
        for _, finish in blocks:
            finish()

    gw = ng * HEAD

    def col(off):
        return pl.BlockSpec((seq, gw), lambda n, b: (b, off * (nblk // ng) + n))

    vec = lambda r: pl.BlockSpec((r, gw), lambda n, b: (0, n))
    mat = pl.BlockSpec((ng, HEAD, HEAD), lambda n, b: (n, 0, 0))
    blk = pl.BlockSpec((seq, gw), lambda n, b: (b, n))
    dshape = jax.ShapeDtypeStruct((t, w), BF16)
    return pl.pallas_call(
        body, name=name, grid=(nblk // ng, nb),
        in_specs=[col(4), col(5), vec(4), vec(1), mat, vec(1), mat, vec(1), vec(1), blk, blk],
        out_specs=[blk, blk, vec(8), mat, mat],
        out_shape=[dshape, dshape, jax.ShapeDtypeStruct((8, w), F32),
                   jax.ShapeDtypeStruct((nblk, HEAD, HEAD), F32), jax.ShapeDtypeStruct((nblk, HEAD, HEAD), F32)],
        scratch_shapes=[pltpu.VMEM((ng, CHUNK, HEAD), F32)] * 2 + [pltpu.VMEM((8, gw), F32)]
        + [pltpu.VMEM((ng, HEAD, HEAD), F32)] * 2,
        compiler_params=_cp(dimension_semantics=("parallel", "arbitrary")),
    )(proj, proj, cw, cb, wa, ba, wx, bx, lam, hsv, dp)


def _ffn_conv(x, prev8, cw_ref, cbv):
    x2, x1 = _shift_down(prev8, x, 2), _shift_down(prev8, x, 1)
    y = cbv + x2 * cw_ref[0:1, :]
    y = y + x1 * cw_ref[1:2, :]
    y = y + x * cw_ref[2:3, :]
    return (x2, x1), y


def _ffn_act_fwd(up, cw, cb, *, nb, seq, name):
    t, f2 = up.shape
    f = f2 // 2
    tc = _tile(f, 256)
    nj = f // tc
    rt = _tile(seq, 256, 16)
    nr = seq // rt

    def body(g_ref, v_ref, cwg_ref, cwv_ref, cbg_ref, cbv_ref, o_ref):
        cbg, cbv = cbg_ref[...], cbv_ref[...]

        def tile(c, carry):
            pg, pv = carry
            rows = pl.ds(pl.multiple_of(c * rt, rt), rt)
            xg, xv = g_ref[rows, :], v_ref[rows, :]
            _, gate = _ffn_conv(xg, pg, cwg_ref, cbg)
            _, val = _ffn_conv(xv, pv, cwv_ref, cbv)
            o_ref[rows, :] = ((gate * _sigmoid(gate)) * val).astype(BF16)
            return xg[rt - 8:], xv[rt - 8:]

        z = jnp.zeros((8, tc), F32)
        lax.fori_loop(0, nr, tile, (z, z))

    gcol = pl.BlockSpec((seq, tc), lambda j, b: (b, j))
    vcol = pl.BlockSpec((seq, tc), lambda j, b: (b, nj + j))
    gv = lambda r: pl.BlockSpec((r, tc), lambda j, b: (0, j))
    vv = lambda r: pl.BlockSpec((r, tc), lambda j, b: (0, nj + j))
    return pl.pallas_call(
        body, name=name, grid=(nj, nb),
        in_specs=[gcol, vcol, gv(3), vv(3), gv(1), vv(1)], out_specs=gcol,
        out_shape=jax.ShapeDtypeStruct((t, f), BF16),
        compiler_params=_cp(dimension_semantics=("parallel", "parallel")),
    )(up, up, cw, cw, cb, cb)


def _ffn_act_bwd(dact, up, cw, cb, *, nb, seq, name, deps=()):
    t, f2 = up.shape
    f = f2 // 2
    tc = _tile(f, 256)
    nj = f // tc
    rt = _tile(seq, 256, 16)
    nr = seq // rt

    def body(da_ref, g_ref, v_ref, cwg_ref, cwv_ref, cbg_ref, cbv_ref,
             dg_ref, dv_ref, sg_ref, sv_ref, eg, ev, accg, accv):
        bi = pl.program_id(1)
        cbg, cbv = cbg_ref[...], cbv_ref[...]

        @pl.when(bi == 0)
        def _():
            accg[...] = jnp.zeros_like(accg)
            accv[...] = jnp.zeros_like(accv)

        colsum = lambda z: jnp.sum(z, axis=0, keepdims=True)

        def first(c, carry):
            pg, pv = carry
            rows = pl.ds(pl.multiple_of(c * rt, rt), rt)
            xg, xv = g_ref[rows, :], v_ref[rows, :]
            (g2, g1), gate = _ffn_conv(xg, pg, cwg_ref, cbg)
            (v2, v1), val = _ffn_conv(xv, pv, cwv_ref, cbv)
            s = _sigmoid(gate)
            da = da_ref[rows, :]
            dgate = da * val * _dsilu(gate, s)
            dval = da * (gate * s)
            eg[rows, :] = dgate
            ev[rows, :] = dval
            for acc, (s2, s1, s0), d in ((accg, (g2, g1, xg), dgate), (accv, (v2, v1, xv), dval)):
                acc[0:1, :] += colsum(s2 * d)
                acc[1:2, :] += colsum(s1 * d)
                acc[2:3, :] += colsum(s0 * d)
                acc[3:4, :] += colsum(d)
            return xg[rt - 8:], xv[rt - 8:]

        z = jnp.zeros((8, tc), F32)
        lax.fori_loop(0, nr, first, (z, z))

        def second(c, carry):
            base = pl.multiple_of(c * rt, rt)
            rows = pl.ds(base, rt)
            after = pl.ds(pl.multiple_of(jnp.minimum(base + rt, seq - 8), 8), 8)
            more = c < nr - 1
            for e, cw_ref, d_ref in ((eg, cwg_ref, dg_ref), (ev, cwv_ref, dv_ref)):
                d = e[rows, :]
                n8 = jnp.where(more, e[after, :], 0.0)
                d1, d2 = _shift_up(d, n8, 1), _shift_up(d, n8, 2)
                d_ref[rows, :] = (d * cw_ref[2:3, :] + d1 * cw_ref[1:2, :] + d2 * cw_ref[0:1, :]).astype(BF16)
            return carry

        lax.fori_loop(0, nr, second, 0)

        @pl.when(bi == nb - 1)
        def _():
            sg_ref[...] = accg[...]
            sv_ref[...] = accv[...]

    gcol = pl.BlockSpec((seq, tc), lambda j, b: (b, j))
    vcol = pl.BlockSpec((seq, tc), lambda j, b: (b, nj + j))
    gv = lambda r: pl.BlockSpec((r, tc), lambda j, b: (0, j))
    vv = lambda r: pl.BlockSpec((r, tc), lambda j, b: (0, nj + j))
    dshape = jax.ShapeDtypeStruct((t, f), BF16)
    sshape = jax.ShapeDtypeStruct((8, f), F32)
    return _pcall(
        body, n_in=7, args=[dact, up, up, cw, cw, cb, cb], deps=deps, name=name, grid=(nj, nb),
        in_specs=[gcol, gcol, vcol, gv(3), vv(3), gv(1), vv(1)],
        out_specs=[gcol, gcol, gv(8), gv(8)], out_shape=[dshape, dshape, sshape, sshape],
        scratch_shapes=[pltpu.VMEM((seq, tc), F32)] * 2 + [pltpu.VMEM((8, tc), F32)] * 2,
        compiler_params=_cp(dimension_semantics=("parallel", "arbitrary")))


def _adamw_math(wv, g, mv, vv):
    m = ADAM_B1 * mv + (1.0 - ADAM_B1) * g
    v = ADAM_B2 * vv + (1.0 - ADAM_B2) * (g * g)
    m_hat = m / (1.0 - ADAM_B1 ** ADAM_STEP)
    v_hat = v / (1.0 - ADAM_B2 ** ADAM_STEP)
    delta = -ADAM_LR * (m_hat / (jnp.sqrt(v_hat) + ADAM_EPS) + ADAM_WD * wv)
    return delta, m, v


def _adamw_shard(wv, mv, vv, p_own, rb, *, name):
    _, r, cdim = wv.shape
    tr = _row_tile(r, cdim)

    def body(w_ref, m_ref, v_ref, p_ref, rb_ref, g_ref, d_ref, mo_ref, vo_ref):
        g = p_ref[...].astype(F32)
        for k in range(3):
            g = g + rb_ref[k].astype(F32)
        g_ref[...] = g
        d_ref[...], mo_ref[...], vo_ref[...] = _adamw_math(w_ref[...], g, m_ref[...], v_ref[...])

    row = pl.BlockSpec((tr, cdim), lambda i: (i, 0))
    row3 = pl.BlockSpec((None, tr, cdim), lambda i: (0, i, 0))
    shp = jax.ShapeDtypeStruct((1, r, cdim), F32)
    return pl.pallas_call(
        body, name=name, grid=(r // tr,),
        in_specs=[row3, row3, row3, row, pl.BlockSpec((3, tr, cdim), lambda i: (0, i, 0))],
        out_specs=[row3] * 4, out_shape=[shp] * 4, compiler_params=_cp(dimension_semantics=("parallel",)),
    )(wv, mv, vv, p_own, rb)


def _adamw_packed(wv, g, mv, vv, *, name):
    r = wv.shape[0]
    tr = _tile(r, 256, 8)

    def body(w_ref, g_ref, m_ref, v_ref, d_ref, mo_ref, vo_ref):
        d_ref[...], mo_ref[...], vo_ref[...] = _adamw_math(w_ref[...], g_ref[...], m_ref[...], v_ref[...])

    row = pl.BlockSpec((tr, LANE), lambda i: (i, 0))
    shp = jax.ShapeDtypeStruct((r, LANE), F32)
    return pl.pallas_call(
        body, name=name, grid=(r // tr,), in_specs=[row] * 4, out_specs=[row] * 3, out_shape=[shp] * 3,
        compiler_params=_cp(dimension_semantics=("parallel",)),
    )(wv, g, mv, vv)


def _place():
    return lax.axis_index("x"), lax.axis_index("y"), lax.axis_index("c")


def _all_reduce_packed(p, *, name):
    r = p.shape[0]
    tr = _tile(r, 256, 8)

    def body(p_ref, o_ref, buf, send, recv, lsem):
        x, y, c = _place()
        me, sibling = (x, y, c), (x, y, 1 - c)
        chips = [(1 - x, y), (x, 1 - y), (1 - x, 1 - y)]

        def slot(px, py, pc):
            return 4 * px + 2 * py + pc

        def copy(k, block, to, src=None):
            dst = buf.at[slot(*block)]
            return pltpu.make_async_remote_copy(
                src_ref=dst if src is None else src, dst_ref=dst, send_sem=send.at[k], recv_sem=recv.at[k],
                device_id=to, device_id_type=MESH)

        mine = pltpu.make_async_copy(p_ref, buf.at[slot(*me)], lsem)
        mine.start()
        first = [copy(0, me, sibling, src=p_ref)]
        first += [copy(1 + j, me, (*chip, c), src=p_ref) for j, chip in enumerate(chips)]
        for cp in first:
            cp.start()
        passed = [copy(4 + j, (*chip, c), sibling) for j, chip in enumerate(chips)]
        for j, chip in enumerate(chips):
            copy(1 + j, (*chip, c), me).wait_recv()
            passed[j].start()
        copy(0, sibling, me).wait_recv()
        for j, chip in enumerate(chips):
            copy(4 + j, (*chip, 1 - c), me).wait_recv()
        for cp in first + passed:
            cp.wait_send()
        mine.wait()

        def add(i, carry):
            rows = pl.ds(pl.multiple_of(i * tr, tr), tr)
            s = buf[0, rows, :]
            for d in range(1, NDEV):
                s = s + buf[d, rows, :]
            o_ref[rows, :] = s
            return carry

        lax.fori_loop(0, r // tr, add, 0)

    vm = pl.BlockSpec(memory_space=pltpu.VMEM)
    return pl.pallas_call(
        body, name=name, in_specs=[vm], out_specs=vm, out_shape=jax.ShapeDtypeStruct(p.shape, p.dtype),
        scratch_shapes=[pltpu.VMEM((NDEV,) + p.shape, p.dtype), pltpu.SemaphoreType.DMA((7,)),
                        pltpu.SemaphoreType.DMA((7,)), pltpu.SemaphoreType.DMA],
        compiler_params=_cp(),
    )(p)


HBM = pl.BlockSpec(memory_space=pltpu.HBM)
SEM = pl.BlockSpec(memory_space=pltpu.SEMAPHORE)
EFFECT = pltpu.SideEffectType.DATAFLOW_SIDE_EFFECTING


def _plan_copies(plan, s_refs, l_refs, send, recv):
    def pick(kind, a, idx):
        ref = (s_refs if kind == "s" else l_refs)[a]
        return ref if idx is None else ref.at[idx]

    return [pltpu.make_async_remote_copy(
        src_ref=pick(*src), dst_ref=pick(*dst), send_sem=send.at[i], recv_sem=recv.at[i],
        device_id=to, device_id_type=MESH) for i, (src, dst, to) in enumerate(plan(*_place()))]


def _xfer_start(srcs, lands, plan, *, name, deps=()):
    ns, nl = len(srcs), len(lands)
    nd = len(deps)
    ncopy = len(plan(0, 0, 0))

    def body(*refs):
        s_refs, l_refs = refs[:ns], refs[ns:ns + nl]
        send, recv = refs[ns + nl + nd], refs[ns + nl + nd + 1]
        token = refs[-1]
        for cp in _plan_copies(plan, s_refs, l_refs, send, recv):
            cp.start()
        token[...] = jnp.zeros_like(token)

    bufs = list(srcs) + list(lands)
    outs = pl.pallas_call(
        body, name=name,
        out_shape=(pltpu.SemaphoreType.DMA((ncopy,)), pltpu.SemaphoreType.DMA((ncopy,)),
                   *[pltpu.HBM(b.shape, b.dtype) for b in bufs], jax.ShapeDtypeStruct((8, LANE), F32)),
        in_specs=[HBM] * (ns + nl) + [ANY] * nd,
        out_specs=(SEM, SEM, *[HBM] * (ns + nl), pl.BlockSpec(memory_space=pltpu.VMEM)),
        input_output_aliases={i: 2 + i for i in range(ns + nl)},
        compiler_params=pltpu.CompilerParams(has_side_effects=EFFECT),
    )(*[pltpu.with_memory_space_constraint(b, pltpu.HBM) for b in bufs], *deps)
    return outs[0], outs[1], list(outs[2:2 + ns]), list(outs[2 + ns:2 + ns + nl]), outs[-1]


def _xfer_wait(send, recv, srcs, lands, plan, after, *, name):
    ns, nl = len(srcs), len(lands)

    def body(*refs):
        s_refs, l_refs = refs[:ns], refs[ns:ns + nl]
        send_ref, recv_ref = refs[ns + nl], refs[ns + nl + 1]
        for cp in _plan_copies(plan, s_refs, l_refs, send_ref, recv_ref):
            cp.wait_send()
            cp.wait_recv()

    bufs = list(srcs) + list(lands)
    outs = pl.pallas_call(
        body, name=name, out_shape=tuple(pltpu.HBM(b.shape, b.dtype) for b in bufs),
        in_specs=[HBM] * (ns + nl) + [SEM, SEM, ANY], out_specs=tuple([HBM] * (ns + nl)),
        input_output_aliases={i: i for i in range(ns + nl)},
        compiler_params=pltpu.CompilerParams(has_side_effects=EFFECT),
    )(*bufs, send, recv, after)
    return list(outs[:ns]), list(outs[ns:])


def _slot8(px, py, pc):
    return 4 * px + 2 * py + pc


def _slot4(r):
    return lambda px, py, pc: (2 * px + py, pl.ds(pc * r, r))


def _plan_gather_first(slots):
    def plan(x, y, c):
        peers = [(x, y, c), (x, y, 1 - c), (1 - x, y, c), (x, 1 - y, c), (1 - x, 1 - y, c)]
        return [(("s", a, None), ("l", a, slot(x, y, c)), to) for a, slot in enumerate(slots) for to in peers]
    return plan


def _plan_gather_pass(slots):
    def plan(x, y, c):
        chips = [(1 - x, y), (x, 1 - y), (1 - x, 1 - y)]
        return [(("l", a, slot(px, py, c)), ("l", a, slot(px, py, c)), (x, y, 1 - c))
                for a, slot in enumerate(slots) for px, py in chips]
    return plan


def _plan_rs_sibling(r):
    def plan(x, y, c):
        return [(("s", 0, (j, pl.ds((1 - c) * r, r))), ("l", 0, j), (x, y, 1 - c)) for j in range(4)]
    return plan


def _plan_rs_plane():
    def plan(x, y, c):
        chips = [(1 - x, y), (x, 1 - y), (1 - x, 1 - y)]
        return [(("s", 0, 2 * px + py), ("l", 0, k), (px, py, c)) for k, (px, py) in enumerate(chips)]
    return plan


def _add_pairs(g, rcv, cidx, *, name):
    _, r, cdim = rcv.shape
    tr = _row_tile(r, cdim, budget=1048576)
    per = r // tr

    def body(c_ref, g_ref, r_ref, o_ref):
        o_ref[...] = (g_ref[...].astype(F32) + r_ref[...].astype(F32)).astype(o_ref.dtype)

    grid_spec = pltpu.PrefetchScalarGridSpec(
        num_scalar_prefetch=1, grid=(4, per),
        in_specs=[pl.BlockSpec((None, tr, cdim), lambda j, i, c_ref: (j, c_ref[0] * per + i, 0)),
                  pl.BlockSpec((None, tr, cdim), lambda j, i, c_ref: (j, i, 0))],
        out_specs=pl.BlockSpec((None, tr, cdim), lambda j, i, c_ref: (j, i, 0)))
    return pl.pallas_call(
        body, name=name, grid_spec=grid_spec, out_shape=jax.ShapeDtypeStruct((4, r, cdim), g.dtype),
        compiler_params=_cp(dimension_semantics=("parallel", "parallel")),
    )(cidx, g, rcv)


def _pack(arrs):
    flat = jnp.concatenate([a.reshape(-1).astype(F32) for a in arrs])
    n = flat.shape[0]
    rows = -(-n // LANE)
    rows = -(-rows // 256) * 256
    return jnp.pad(flat, (0, rows * LANE - n)).reshape(rows, LANE)


def _unpack(packed, shapes):
    flat = packed.reshape(-1)
    out, o = [], 0
    for s in shapes:
        n = math.prod(s)
        out.append(flat[o:o + n].reshape(s))
        o += n
    return out


def _pad_blocks(a, w, wp):
    lead = a.shape[:-1]
    k = a.shape[-1] // w
    pads = [(0, 0)] * (len(lead) + 1) + [(0, wp - w)]
    return jnp.pad(a.reshape(*lead, k, w), pads).reshape(*lead, k * wp)


def kernel(x, ln1_w, w_in, lb_gamma, hg_norm_w, lru_conv_w, lru_conv_b, lru_wa, lru_ba, lru_wx, lru_bx, lru_lambda, lru_norm_w, w_out, ln2_w, ffn_w_up, ffn_conv_w, ffn_conv_b, ffn_w_down, final_norm_w, loss_target, m_ln1_w, m_w_in, m_lb_gamma, m_hg_norm_w, m_lru_conv_w, m_lru_conv_b, m_lru_wa, m_lru_ba, m_lru_wx, m_lru_bx, m_lru_lambda, m_lru_norm_w, m_w_out, m_ln2_w, m_ffn_w_up, m_ffn_conv_w, m_ffn_conv_b, m_ffn_w_down, m_final_norm_w, v_ln1_w, v_w_in, v_lb_gamma, v_hg_norm_w, v_lru_conv_w, v_lru_conv_b, v_lru_wa, v_lru_ba, v_lru_wx, v_lru_bx, v_lru_lambda, v_lru_norm_w, v_w_out, v_ln2_w, v_ffn_w_up, v_ffn_conv_w, v_ffn_conv_b, v_ffn_w_down, v_final_norm_w):
    nb, seq, d = x.shape
    t = nb * seq
    wmix = d // 2
    in_sh = w_in.shape[2]
    up_sh = ffn_w_up.shape[2]
    up_pad = -(-up_sh // LANE) * LANE
    hs = ffn_w_down.shape[1]
    fpad = 4 * up_pad
    cx, cy, cc = _place()
    me = 4 * cx + 2 * cy + cc
    plane = 2 * cx + cy

    def gather(shards, lands, slots, tag, deps=()):
        st = _xfer_start(shards, lands, _plan_gather_first(slots), name=f"gather_{tag}_start", deps=deps)
        return (st, slots, tag), st[4]

    def pass_on(g, after):
        st, slots, tag = g
        _, lands = _xfer_wait(st[0], st[1], st[2], st[3], _plan_gather_first(slots), after, name=f"gather_{tag}_wait")
        st2 = _xfer_start([], lands, _plan_gather_pass(slots), name=f"gather_{tag}_pass")
        return (st2, slots, tag), st2[4]

    def gathered(g, after):
        st2, slots, tag = g
        return _xfer_wait(st2[0], st2[1], [], st2[3], _plan_gather_pass(slots), after, name=f"gather_{tag}_done")[1]

    land8 = lambda s: lax.empty((NDEV,) + s.shape, s.dtype)

    win_s = w_in[0].astype(BF16)
    wout_s = w_out[0].astype(BF16)
    wup_s = jnp.pad(ffn_w_up[0], ((0, 0), (0, up_pad - up_sh))).astype(BF16)
    wdn_s = ffn_w_down[0].astype(BF16)
    fcw_s = jnp.pad(ffn_conv_w[0], ((0, 0), (0, up_pad - up_sh)))
    ga, tok = gather([win_s, lru_conv_w[0], fcw_s], [land8(win_s), land8(lru_conv_w[0]), land8(fcw_s)],
                     [_slot8] * 3, "a")
    go, tok = gather([wout_s], [land8(wout_s)], [_slot8], "o", deps=(tok,))
    gu, tok = gather([wup_s], [land8(wup_s)], [_slot8], "u", deps=(tok,))
    gd, tok = gather([wdn_s], [jnp.zeros((4, up_pad, d), BF16)], [_slot4(hs)], "d", deps=(tok,))
    fcb = _pad_blocks(ffn_conv_b, up_sh, up_pad)
    wa_b, wx_b = lru_wa[0].astype(BF16), lru_wx[0].astype(BF16)

    xf = x.reshape(t, d)
    hn = _rms_fwd(xf, ln1_w, name="ln1_fwd", deps=(tok,))
    ga, tok = pass_on(ga, hn)
    win_g, lcw_g, fcw_g = gathered(ga, tok)
    lcw = lcw_g.transpose(1, 0, 2).reshape(lru_conv_w.shape[1], wmix)
    fcw = fcw_g.transpose(1, 0, 2).reshape(ffn_conv_w.shape[1], 2 * fpad)
    proj = _mm(hn, win_g, kind="nn", out_dtype=F32, name="in_proj")
    go, tok = pass_on(go, proj)
    o_hg, o_pre, states = _hgrn_fwd(proj, lb_gamma, hg_norm_w, nb=nb, seq=seq, name="hgrn_fwd")
    (wout_g,) = gathered(go, o_pre)
    wout_f = wout_g.reshape(1, d, d)
    p_lru, h_lru = _lru_fwd(proj, lcw, lru_conv_b, wa_b, lru_ba, wx_b, lru_bx, lru_lambda,
                            nb=nb, seq=seq, name="lru_fwd", deps=(tok,))
    o_lru = _rms_fwd(p_lru, lru_norm_w, name="lru_norm_fwd")
    gu, tok = pass_on(gu, o_lru)
    mix = jnp.concatenate([o_hg, o_lru], axis=1)
    h1 = _mm(mix, wout_f, kind="nn", out_dtype=F32, name="out_proj", res=xf, deps=(tok,))
    hn2 = _rms_fwd(h1, ln2_w, name="ln2_fwd")
    (wup_g,) = gathered(gu, hn2)
    gd, tok = pass_on(gd, hn2)
    up = _mm(hn2, wup_g, kind="nn", out_dtype=F32, name="ffn_up", tn_cap=up_pad // 2, deps=(tok,))
    (wdn_g,) = gathered(gd, up)
    wdn_f = wdn_g.reshape(1, fpad, d)
    act = _ffn_act_fwd(up, fcw, fcb, nb=nb, seq=seq, name="ffn_act_fwd")
    h2 = _mm(act, wdn_f, kind="nn", out_dtype=F32, name="ffn_down", res=h1, tk_cap=up_pad // 2)

    dh2, dh2_b, g_fnw, loss_part = _loss_head(h2, final_norm_w.reshape(1, d), loss_target.reshape(t, d),
                                              name="loss_head")
    loss = lax.psum(loss_part[0, 0], ("x", "y", "c"))

    cidx = jnp.reshape(cc, (1,)).astype(jnp.int32)

    def to_sibling(g4, r, tag, deps=()):
        land = lax.empty((4, r, g4.shape[2]), g4.dtype)
        st = _xfer_start([g4], [land], _plan_rs_sibling(r), name=f"rs_sib_start_{tag}", deps=deps)
        return (st, r, tag), st[4]

    def sibling_sum(rs, after):
        st, r, tag = rs
        (g4,), (rcv,) = _xfer_wait(st[0], st[1], st[2], st[3], _plan_rs_sibling(r), after, name=f"rs_sib_wait_{tag}")
        psum = _add_pairs(g4, rcv, cidx, name=f"rs_add_{tag}")
        land = lax.empty((3,) + psum.shape[1:], psum.dtype)
        st2 = _xfer_start([psum], [land], _plan_rs_plane(), name=f"rs_plane_start_{tag}")
        return (st2, tag), st2[4]

    dact = _mm(dh2_b, wdn_f, kind="nt", out_dtype=F32, name="ffn_down_dx", tm_cap=512, tn_cap=up_pad)
    g_wdn = _mm(act, dh2_b, kind="tn", out_dtype=BF16, name="ffn_down_dw", tm_cap=up_pad // 2)
    rs_d, tok = to_sibling(g_wdn.reshape(4, up_pad, d), hs, "down")
    dup_g, dup_v, s_g, s_v = _ffn_act_bwd(dact, up, fcw, fcb, nb=nb, seq=seq, name="ffn_act_bwd", deps=(tok,))
    dup = jnp.concatenate([dup_g, dup_v], axis=1)
    pl_d, tok = sibling_sum(rs_d, dup)
    dhn2 = _mm(dup, wup_g, kind="nt", out_dtype=F32, name="ffn_up_dx", tm_cap=512, tn_cap=1024, tk_cap=up_pad,
               deps=(tok,))
    g_wup = _mm(dup, hn2, kind="tn", out_dtype=BF16, name="ffn_up_dw", tm_cap=up_pad // 2)
    rs_u, tok = to_sibling(g_wup.reshape(4, 2 * up_pad, d), up_pad, "up")
    dh1, dh1_b, g_ln2 = _rms_bwd(dhn2, h1, ln2_w, name="ln2_bwd", extra=dh2, want_bf16=True, deps=(tok,))
    dmix = _mm(dh1_b, wout_f, kind="nt", out_dtype=F32, name="out_proj_dx")
    pl_u, tok = sibling_sum(rs_u, dmix)
    g_wout = _mm(mix, dh1_b, kind="tn", out_dtype=BF16, name="out_proj_dw", deps=(tok,))
    rs_o, tok = to_sibling(g_wout.reshape(4, d // 4, d), d // NDEV, "out")
    dp_lru, g_lnw = _rms_bwd(dmix, p_lru, lru_norm_w, name="lru_norm_bwd", dy_cb=1, deps=(tok,))
    dxr, dyr, s_lru, g_wa, g_wx = _lru_bwd(proj, lcw, lru_conv_b, wa_b, lru_ba, wx_b, lru_bx, lru_lambda,
                                           h_lru, dp_lru, nb=nb, seq=seq, name="lru_bwd")
    pl_o, tok = sibling_sum(rs_o, dxr)
    dqr, dfr, dir_, dgr, s_hg = _hgrn_bwd(proj, lb_gamma, hg_norm_w, o_pre, states, dmix,
                                          nb=nb, seq=seq, name="hgrn_bwd", deps=(tok,))
    dproj = jnp.concatenate([dqr, dfr, dir_, dgr, dxr, dyr], axis=1)
    g_win = _mm(hn, dproj, kind="tn", out_dtype=BF16, name="in_proj_dw", out_blocks=NDEV, tn_cap=in_sh)
    rs_i, tok = to_sibling(g_win.reshape(4, 2 * d, in_sh), d, "in")
    dhn = _mm(dproj, win_g, kind="nt", out_dtype=F32, name="in_proj_dx", tk_cap=768, deps=(tok,))
    grad_x, g_ln1 = _rms_bwd(dhn, xf, ln1_w, name="ln1_bwd", extra=dh1)

    fcw_parts = jnp.concatenate([s_g[0:3], s_v[0:3]], axis=1).reshape(3, NDEV, up_pad)[:, :, :up_sh]
    fcb_part = jnp.concatenate([s_g[3:4], s_v[3:4]], axis=1).reshape(NDEV, up_pad)[:, :up_sh]
    small_parts = [g_ln1, s_hg[0:2], s_hg[2:3], s_lru[0:4], s_lru[4:5], g_wa, s_lru[5:6], g_wx, s_lru[6:7],
                   s_lru[7:8], g_lnw, g_ln2, fcw_parts, fcb_part, g_fnw]
    reduced = _all_reduce_packed(_pack(small_parts), name="allreduce_small")
    pl_i, _ = sibling_sum(rs_i, reduced)
    summed = _unpack(reduced, [p.shape for p in small_parts])
    (s_ln1, s_lbg, s_hgn, s_lcw, s_lcb, s_wa, s_ba, s_wx, s_bx, s_lam, s_lnw, s_ln2, s_fcw, s_fcb, s_fnw) = summed
    sh_lcw = lru_conv_w.shape[2]
    g_small = {
        "ln1_w": s_ln1, "lb_gamma": s_lbg, "hg_norm_w": s_hgn,
        "lru_conv_w": lax.dynamic_slice_in_dim(s_lcw, me * sh_lcw, sh_lcw, axis=1),
        "lru_conv_b": s_lcb, "lru_wa": s_wa, "lru_ba": s_ba, "lru_wx": s_wx, "lru_bx": s_bx,
        "lru_lambda": s_lam, "lru_norm_w": s_lnw, "ln2_w": s_ln2,
        "ffn_conv_w": lax.dynamic_index_in_dim(s_fcw, me, 1, keepdims=False),
        "ffn_conv_b": s_fcb, "final_norm_w": s_fnw,
    }
    w_small = {"ln1_w": ln1_w, "lb_gamma": lb_gamma, "hg_norm_w": hg_norm_w, "lru_conv_w": lru_conv_w,
               "lru_conv_b": lru_conv_b, "lru_wa": lru_wa, "lru_ba": lru_ba, "lru_wx": lru_wx, "lru_bx": lru_bx,
               "lru_lambda": lru_lambda, "lru_norm_w": lru_norm_w, "ln2_w": ln2_w, "ffn_conv_w": ffn_conv_w,
               "ffn_conv_b": ffn_conv_b, "final_norm_w": final_norm_w}
    m_small = {"ln1_w": m_ln1_w, "lb_gamma": m_lb_gamma, "hg_norm_w": m_hg_norm_w, "lru_conv_w": m_lru_conv_w,
               "lru_conv_b": m_lru_conv_b, "lru_wa": m_lru_wa, "lru_ba": m_lru_ba, "lru_wx": m_lru_wx,
               "lru_bx": m_lru_bx, "lru_lambda": m_lru_lambda, "lru_norm_w": m_lru_norm_w, "ln2_w": m_ln2_w,
               "ffn_conv_w": m_ffn_conv_w, "ffn_conv_b": m_ffn_conv_b, "final_norm_w": m_final_norm_w}
    v_small = {"ln1_w": v_ln1_w, "lb_gamma": v_lb_gamma, "hg_norm_w": v_hg_norm_w, "lru_conv_w": v_lru_conv_w,
               "lru_conv_b": v_lru_conv_b, "lru_wa": v_lru_wa, "lru_ba": v_lru_ba, "lru_wx": v_lru_wx,
               "lru_bx": v_lru_bx, "lru_lambda": v_lru_lambda, "lru_norm_w": v_lru_norm_w, "ln2_w": v_ln2_w,
               "ffn_conv_w": v_ffn_conv_w, "ffn_conv_b": v_ffn_conv_b, "final_norm_w": v_final_norm_w}
    names = list(w_small)
    shapes = [w_small[k].shape for k in names]
    g_small = {k: g_small[k].reshape(w_small[k].shape) for k in names}
    packed = _adamw_packed(_pack([w_small[k] for k in names]), _pack([g_small[k] for k in names]),
                           _pack([m_small[k] for k in names]), _pack([v_small[k] for k in names]),
                           name="adamw_small")
    d_small, nm_small, nv_small = ({k: a for k, a in zip(names, _unpack(pk, shapes))} for pk in packed)

    def finish(pl_x, after, trim=lambda a: a):
        st, tag = pl_x
        (psum,), (rb,) = _xfer_wait(st[0], st[1], st[2], st[3], _plan_rs_plane(), after, name=f"rs_plane_wait_{tag}")
        return trim(lax.dynamic_index_in_dim(psum, plane, 0, keepdims=False)), trim(rb)

    upd_dn = _adamw_shard(ffn_w_down, m_ffn_w_down, v_ffn_w_down, *finish(pl_d, packed[0]), name="adamw_w_down")
    tr3 = lambda a: a.transpose(0, 2, 1)
    upd_up_t = _adamw_shard(tr3(ffn_w_up), tr3(m_ffn_w_up), tr3(v_ffn_w_up), *finish(pl_u, upd_dn[0]),
                            name="adamw_w_up")
    upd_up = [tr3(a) for a in upd_up_t]
    upd_out = _adamw_shard(w_out, m_w_out, v_w_out, *finish(pl_o, upd_up_t[0]), name="adamw_w_out")
    upd_in = _adamw_shard(w_in, m_w_in, v_w_in, *finish(pl_i, upd_out[0]), name="adamw_w_in")

    grads, deltas, new_m, new_v = dict(g_small), dict(d_small), dict(nm_small), dict(nv_small)
    for k, upd in (("w_in", upd_in), ("w_out", upd_out), ("ffn_w_up", upd_up), ("ffn_w_down", upd_dn)):
        grads[k], deltas[k], new_m[k], new_v[k] = upd
    order = ["ln1_w", "w_in", "lb_gamma", "hg_norm_w", "lru_conv_w", "lru_conv_b", "lru_wa", "lru_ba", "lru_wx",
             "lru_bx", "lru_lambda", "lru_norm_w", "w_out", "ln2_w", "ffn_w_up", "ffn_conv_w", "ffn_conv_b",
             "ffn_w_down", "final_norm_w"]
    return (loss, grad_x.reshape(nb, seq, d), *[grads[k] for k in order], *[deltas[k] for k in order],
            *[new_m[k] for k in order], *[new_v[k] for k in order])
```

```python
import math

import jax
import jax.numpy as jnp
from jax import lax
from jax.experimental import pallas as pl
from jax.experimental.pallas import tpu as pltpu

F32, BF16 = jnp.float32, jnp.bfloat16
EPS = 1e-6
HEAD = 128
CHUNK = 64
SUB = 16
NSUB = CHUNK // SUB
LRU_C = 8.0
LANE = 128
NDEV = 8
ADAM_LR, ADAM_B1, ADAM_B2, ADAM_EPS, ADAM_WD, ADAM_STEP = 0.001, 0.9, 0.999, 1e-08, 0.01, 10
MESH = pl.DeviceIdType.MESH
ANY = pl.BlockSpec(memory_space=pl.ANY)
VMEM_LIMIT = 56 * 1024 * 1024


def _cp(**kw):
    return pltpu.CompilerParams(vmem_limit_bytes=VMEM_LIMIT, **kw)


def _tile(n, cap, mult=LANE):
    best = None
    for t in range(mult, min(n, cap) + 1, mult):
        if n % t == 0:
            best = t
    return best if best is not None else n


def _row_tile(r, cdim, budget=262144):
    return _tile(r, max(16, budget // cdim), 16)


def _sigmoid(x):
    return jax.nn.sigmoid(x)


def _dsilu(x, s):
    return s * (1.0 + x * (1.0 - s))


def _iota_rows(n, w=LANE):
    return lax.broadcasted_iota(jnp.int32, (n, w), 0)


def _lane_groups(n):
    return 2 if n % 2 == 0 else 1


def _group_views(refs, g, kinds):
    assert len(refs) == len(kinds)
    cols = pl.ds(g * HEAD, HEAD)
    return [r.at[:, cols] if kind == "l" else r.at[g] for r, kind in zip(refs, kinds)]


def _shift_down(prev8, xt, k):
    cat = jnp.concatenate([prev8, xt], axis=0)
    return pltpu.roll(cat, k, 0)[8:]


def _shift_up(xt, next8, k):
    cat = jnp.concatenate([xt, next8], axis=0)
    n = cat.shape[0]
    return pltpu.roll(cat, n - k, 0)[: xt.shape[0]]


def _scan_fwd(a, u):
    n = a.shape[0]
    row = _iota_rows(n, a.shape[1])
    k = 1
    while k < n:
        keep = row >= k
        a_s = jnp.where(keep, pltpu.roll(a, k, 0), 1.0)
        u_s = jnp.where(keep, pltpu.roll(u, k, 0), 0.0)
        u = a * u_s + u
        a = a * a_s
        k *= 2
    return a, u


def _scan_bwd(a, u):
    n = a.shape[0]
    row = _iota_rows(n, a.shape[1])
    k = 1
    while k < n:
        keep = row < n - k
        a_s = jnp.where(keep, pltpu.roll(a, n - k, 0), 1.0)
        u_s = jnp.where(keep, pltpu.roll(u, n - k, 0), 0.0)
        u = a * u_s + u
        a = a * a_s
        k *= 2
    return a, u


def _cumsum_fwd(u):
    n = u.shape[0]
    row = _iota_rows(n, u.shape[1])
    k = 1
    while k < n:
        u = u + jnp.where(row >= k, pltpu.roll(u, k, 0), 0.0)
        k *= 2
    return u


def _cumsum_bwd(u):
    n = u.shape[0]
    row = _iota_rows(n, u.shape[1])
    k = 1
    while k < n:
        u = u + jnp.where(row < n - k, pltpu.roll(u, n - k, 0), 0.0)
        k *= 2
    return u


def _dot(a, b, dims):
    return lax.dot_general(a.astype(BF16), b.astype(BF16), (dims, ((), ())), preferred_element_type=F32)


NN = ((1,), (0,))
NT = ((1,), (1,))
TN = ((0,), (0,))


def _pcall(body, *, n_in, in_specs, args, deps=(), **kw):
    nd = len(deps)
    if nd:
        inner = body

        def body(*refs):
            return inner(*refs[:n_in], *refs[n_in + nd:])

        in_specs = list(in_specs) + [ANY] * nd
        args = list(args) + list(deps)
    return pl.pallas_call(body, in_specs=in_specs, **kw)(*args)


def _mm(a, b3, *, kind, out_dtype, name, res=None, tm_cap=1024, tn_cap=1536, tk_cap=1024, out_blocks=1, k_blocks=1,
        deps=()):
    if kind == "nn":
        m, kdim = a.shape
        nb, _, nsh = b3.shape
        n = nb * nsh
        tm, tn, tk = _tile(m, tm_cap), _tile(nsh, tn_cap), _tile(kdim, tk_cap)
        per = nsh // tn
        a_blk, a_ix = (tm, tk), lambda i, j, k: (i, k)
        b_blk, b_ix = (None, tk, tn), lambda i, j, k: (j // per, k, j % per)
        dims = NN
    elif kind == "nt":
        m, kdim = a.shape
        nb, n, ksh = b3.shape
        tm, tn, tk = _tile(m, tm_cap), _tile(n, tn_cap), _tile(ksh, tk_cap)
        per = ksh // tk
        a_blk, a_ix = (tm, tk), lambda i, j, k: (i, k)
        b_blk, b_ix = (None, tn, tk), lambda i, j, k: (k // per, j, k % per)
        if k_blocks > 1:
            assert tk == ksh and nb % k_blocks == 0
            tk = k_blocks * ksh
            a_blk, b_blk, b_ix = (tm, tk), (k_blocks, tn, ksh), lambda i, j, k: (k, j, 0)
        dims = NT
    else:
        kdim, m = a.shape
        n = b3.shape[1]
        nsh = n // out_blocks
        tm, tn, tk = _tile(m, tm_cap), _tile(nsh, tn_cap), _tile(kdim, tk_cap)
        per = nsh // tn
        a_blk, a_ix = (tk, tm), lambda i, j, k: (k, i)
        b_blk, b_ix = (tk, tn), lambda i, j, k: (k, j)
        dims = TN
    nk = kdim // tk
    j_outer = nk == 1
    grid = (n // tn, m // tm, nk) if j_outer else (m // tm, n // tn, nk)
    at = (lambda f: lambda g0, g1, k: f(g1, g0, k)) if j_outer else (lambda f: f)
    a_spec, b_spec = pl.BlockSpec(a_blk, at(a_ix)), pl.BlockSpec(b_blk, at(b_ix))

    def body(*refs):
        a_ref, b_ref = refs[:2]
        r_ref = refs[2] if res is not None else None
        o_ref = refs[3] if res is not None else refs[2]
        if k_blocks > 1:
            w = b_ref.shape[2]
            part = sum(lax.dot_general(a_ref[:, s * w:(s + 1) * w], b_ref[s], (dims, ((), ())),
                                       preferred_element_type=F32) for s in range(k_blocks))
        else:
            part = lax.dot_general(a_ref[...], b_ref[...], (dims, ((), ())), preferred_element_type=F32)
        if nk == 1:
            o_ref[...] = (part if res is None else part + r_ref[...]).astype(o_ref.dtype)
            return
        acc = refs[-1]
        k = pl.program_id(2)

        @pl.when(k == 0)
        def _():
            acc[...] = part

        @pl.when(k > 0)
        def _():
            acc[...] += part

        @pl.when(k == nk - 1)
        def _():
            r = acc[...]
            if res is not None:
                r = r + r_ref[...]
            o_ref[...] = r.astype(o_ref.dtype)

    in_specs = [a_spec, b_spec]
    args = [a, b3]
    if res is not None:
        in_specs.append(pl.BlockSpec((tm, tn), at(lambda i, j, k: (i, j))))
        args.append(res)
    if kind == "tn":
        out_shape = jax.ShapeDtypeStruct((out_blocks, m, nsh), out_dtype)
        out_spec = pl.BlockSpec((None, tm, tn), at(lambda i, j, k: (j // per, i, j % per)))
    else:
        out_shape = jax.ShapeDtypeStruct((m, n), out_dtype)
        out_spec = pl.BlockSpec((tm, tn), at(lambda i, j, k: (i, j)))
    return _pcall(
        body, n_in=len(args), in_specs=in_specs, args=args, deps=deps,
        name=name, grid=grid, out_specs=out_spec, out_shape=out_shape,
        scratch_shapes=[pltpu.VMEM((tm, tn), F32)] if nk > 1 else [],
        compiler_params=_cp(dimension_semantics=("parallel", "parallel", "arbitrary")))


def _rms_fwd(x, w, *, name, tm=256, deps=()):
    t, d = x.shape
    tm = _tile(t, tm, 16)

    def body(x_ref, w_ref, o_ref):
        xv = x_ref[...]
        r = lax.rsqrt(jnp.mean(xv * xv, axis=-1, keepdims=True) + EPS)
        o_ref[...] = ((xv * r) * w_ref[...]).astype(o_ref.dtype)

    return _pcall(
        body, n_in=2, args=[x, w], deps=deps, name=name, grid=(t // tm,),
        in_specs=[pl.BlockSpec((tm, d), lambda i: (i, 0)), pl.BlockSpec((1, d), lambda i: (0, 0))],
        out_specs=pl.BlockSpec((tm, d), lambda i: (i, 0)),
        out_shape=jax.ShapeDtypeStruct((t, d), BF16), compiler_params=_cp())


def _rms_bwd(dy, x, w, *, name, extra=None, dy_cb=0, want_bf16=False, tm=256, deps=()):
    t, d = x.shape
    tm = _tile(t, tm, 16)

    def body(*refs):
        refs = list(refs)
        dy_ref, x_ref, w_ref = refs[:3]
        e_ref = refs[3] if extra is not None else None
        outs = refs[4:] if extra is not None else refs[3:]
        dx_ref = outs[0]
        dxb_ref = outs[1] if want_bf16 else None
        dw_ref = outs[-1]
        i = pl.program_id(0)
        xv = x_ref[...]
        r = lax.rsqrt(jnp.mean(xv * xv, axis=-1, keepdims=True) + EPS)
        nh = xv * r
        dyv = dy_ref[...]
        dn = dyv * w_ref[...]
        dx = r * (dn - nh * jnp.mean(dn * nh, axis=-1, keepdims=True))
        if extra is not None:
            dx = dx + e_ref[...]
        dx_ref[...] = dx
        if want_bf16:
            dxb_ref[...] = dx.astype(BF16)
        part = jnp.sum(dyv * nh, axis=0, keepdims=True)

        @pl.when(i == 0)
        def _():
            dw_ref[...] = part

        @pl.when(i > 0)
        def _():
            dw_ref[...] += part

    row = pl.BlockSpec((tm, d), lambda i: (i, 0))
    in_specs = [pl.BlockSpec((tm, d), lambda i: (i, dy_cb)), row, pl.BlockSpec((1, d), lambda i: (0, 0))]
    args = [dy, x, w]
    if extra is not None:
        in_specs.append(row)
        args.append(extra)
    out_shape = [jax.ShapeDtypeStruct((t, d), F32)]
    out_specs = [row]
    if want_bf16:
        out_shape.append(jax.ShapeDtypeStruct((t, d), BF16))
        out_specs.append(row)
    out_shape.append(jax.ShapeDtypeStruct((1, d), F32))
    out_specs.append(pl.BlockSpec((1, d), lambda i: (0, 0)))
    return _pcall(
        body, n_in=len(args), in_specs=in_specs, args=args, deps=deps,
        name=name, grid=(t // tm,), out_specs=out_specs, out_shape=out_shape,
        compiler_params=_cp(dimension_semantics=("arbitrary",)))


def _loss_head(h, w, tgt, *, name, tm=256):
    t, d = h.shape
    tm = _tile(t, tm, 16)

    def body(h_ref, w_ref, t_ref, dh_ref, dhb_ref, dw_ref, loss_ref):
        i = pl.program_id(0)
        xv = h_ref[...]
        wv = w_ref[...]
        r = lax.rsqrt(jnp.mean(xv * xv, axis=-1, keepdims=True) + EPS)
        nh = xv * r
        e = nh * wv - t_ref[...]
        part_loss = jnp.full((1, LANE), 0.5 * jnp.sum(jnp.mean(e * e, axis=-1, keepdims=True)), F32)
        dyv = e * (1.0 / d)
        dn = dyv * wv
        dx = r * (dn - nh * jnp.mean(dn * nh, axis=-1, keepdims=True))
        dh_ref[...] = dx
        dhb_ref[...] = dx.astype(BF16)
        part = jnp.sum(dyv * nh, axis=0, keepdims=True)

        @pl.when(i == 0)
        def _():
            dw_ref[...] = part
            loss_ref[...] = part_loss

        @pl.when(i > 0)
        def _():
            dw_ref[...] += part
            loss_ref[...] += part_loss

    row = pl.BlockSpec((tm, d), lambda i: (i, 0))
    vec = pl.BlockSpec((1, d), lambda i: (0, 0))
    return pl.pallas_call(
        body, name=name, grid=(t // tm,), in_specs=[row, vec, row],
        out_specs=[row, row, vec, pl.BlockSpec((1, LANE), lambda i: (0, 0))],
        out_shape=[jax.ShapeDtypeStruct((t, d), F32), jax.ShapeDtypeStruct((t, d), BF16),
                   jax.ShapeDtypeStruct((1, d), F32), jax.ShapeDtypeStruct((1, LANE), F32)],
        compiler_params=_cp(dimension_semantics=("arbitrary",)),
    )(h, w, tgt)


def _lower_bound(lbg_ref):
    g0, g1 = lbg_ref[0:1, :], lbg_ref[1:2, :]
    m = jnp.maximum(g0, g1)
    e0, e1 = jnp.exp(g0 - m), jnp.exp(g1 - m)
    return e0 / (e0 + e1)


def _seg_bounds():
    offs, o = {}, 0
    for i in range(1, NSUB):
        offs[i] = (o, o + SUB * i)
        o += SUB * i
    return offs, o


def _pad_rows(x, n):
    if x.shape[0] == n:
        return x
    return jnp.concatenate([x, jnp.zeros((n - x.shape[0], x.shape[1]), x.dtype)], axis=0)


def _offdiag_mask():
    offs, total = _seg_bounds()
    padded = -(-total // LANE) * LANE
    rsub = lax.broadcasted_iota(jnp.int32, (CHUNK, padded), 0) // SUB
    col = lax.broadcasted_iota(jnp.int32, (CHUNK, padded), 1)
    cseg = jnp.zeros((CHUNK, padded), jnp.int32)
    for i in range(1, NSUB):
        cseg = cseg + (col >= offs[i][0]).astype(jnp.int32)
    return (rsub == cseg) & (col < total)


def _offdiag_setup(q, k, b, v, b_c, mask):
    offs, total = _seg_bounds()
    padded = -(-total // LANE) * LANE
    eq_parts = [jnp.zeros((SUB, HEAD), F32)]
    ek_parts, k_parts, v_parts = [], [], []
    for i in range(1, NSUB):
        r_i = b_c[SUB * i - 1:SUB * i, :]
        eq_parts.append(jnp.exp(b[SUB * i:SUB * (i + 1)] - r_i))
        ek_parts.append(jnp.exp(r_i - b[0:SUB * i]))
        k_parts.append(k[0:SUB * i])
        v_parts.append(v[0:SUB * i])
    eq = jnp.concatenate(eq_parts, axis=0)
    ek = _pad_rows(jnp.concatenate(ek_parts, axis=0), padded)
    kt = _pad_rows(jnp.concatenate(k_parts, axis=0), padded) * ek
    vs = _pad_rows(jnp.concatenate(v_parts, axis=0), padded)
    qt = q * eq
    a = jnp.where(mask, _dot(qt, kt, NT), 0.0)
    return offs, eq, ek, kt, vs, qt, a


def _hgrn_fwd(proj, lbg, nw, *, nb, seq, name):
    t = proj.shape[0]
    w = lbg.shape[1]
    nh = w // HEAD
    nc = seq // CHUNK
    ng = _lane_groups(nh)

    def head(mask, q_ref, f_ref, i_ref, g_ref, lbg_ref, nw_ref, ohg_ref, opre_ref, st_ref, k_c, b_c, v_c, st):
        lb = _lower_bound(lbg_ref)
        nwv = nw_ref[...]
        st[...] = jnp.zeros_like(st)

        def chunk(c, carry):
            rows = pl.ds(pl.multiple_of(c * CHUNK, CHUNK), CHUNK)
            qr = q_ref[rows, :]
            q = qr * _sigmoid(qr)
            f = lb + (1.0 - lb) * _sigmoid(f_ref[rows, :])
            k = 1.0 - f
            b = _cumsum_fwd(jnp.log(f))
            v = i_ref[rows, :]
            k_c[...] = k
            b_c[...] = b
            v_c[...] = v
            s_t = st[...]
            st_ref[c] = s_t
            o = _dot(q * jnp.exp(b), s_t, NT)
            _, _, _, _, vs, _, a = _offdiag_setup(q, k, b, v, b_c, mask)
            o = o + _dot(a, vs, NN)
            diag = []
            for i in range(NSUB):
                accs = [jnp.zeros((8, HEAD), F32) for _ in range(SUB // 8)]
                for j in range(SUB):
                    r = SUB * i + j
                    bj, kj, vj = b_c[r:r + 1, :], k_c[r:r + 1, :], v_c[r:r + 1, :]
                    for p in range(j // 8, SUB // 8):
                        lo = SUB * i + 8 * p
                        d = jnp.exp(b[lo:lo + 8] - bj)
                        if 8 * p < j:
                            d = jnp.where(_iota_rows(8) + 8 * p >= j, d, 0.0)
                        s = jnp.sum(q[lo:lo + 8] * d * kj, axis=-1, keepdims=True)
                        accs[p] = accs[p] + s * vj
                diag.extend(accs)
            o = o + jnp.concatenate(diag, axis=0)
            bl = b_c[CHUNK - 1:CHUNK, :]
            kb = k * jnp.exp(bl - b)
            st[...] = s_t * jnp.exp(bl) + _dot(v, kb, TN)
            opre_ref[rows, :] = o
            rn = lax.rsqrt(jnp.mean(o * o, axis=-1, keepdims=True) + EPS)
            gr = g_ref[rows, :]
            ohg_ref[rows, :] = (((o * rn) * nwv) * (gr * _sigmoid(gr))).astype(BF16)
            return carry

        return chunk

    def body(*refs):
        mask = _offdiag_mask()
        chunks = [head(mask, *_group_views(refs, g, "llllllll" + "ggggg")) for g in range(ng)]

        def step(c, carry):
            for chunk in chunks:
                chunk(c, carry)
            return carry

        lax.fori_loop(0, nc, step, 0)

    gw = ng * HEAD

    def col(off):
        return pl.BlockSpec((seq, gw), lambda h, b: (b, off * (nh // ng) + h))

    vec = lambda r: pl.BlockSpec((r, gw), lambda h, b: (0, h))
    out_blk = pl.BlockSpec((seq, gw), lambda h, b: (b, h))
    return pl.pallas_call(
        body, name=name, grid=(nh // ng, nb),
        in_specs=[col(0), col(1), col(2), col(3), vec(2), vec(1)],
        out_specs=[out_blk, out_blk, pl.BlockSpec((None, ng, nc, HEAD, HEAD), lambda h, b: (b, h, 0, 0, 0))],
        out_shape=[jax.ShapeDtypeStruct((t, w), BF16), jax.ShapeDtypeStruct((t, w), F32),
                   jax.ShapeDtypeStruct((nb, nh, nc, HEAD, HEAD), F32)],
        scratch_shapes=[pltpu.VMEM((ng, CHUNK, HEAD), F32)] * 3 + [pltpu.VMEM((ng, HEAD, HEAD), F32)],
        compiler_params=_cp(dimension_semantics=("parallel", "parallel")),
    )(proj, proj, proj, proj, lbg, nw)


def _hgrn_bwd(proj, lbg, nw, opre, states, dmix, *, nb, seq, name, deps=()):
    t = proj.shape[0]
    w = lbg.shape[1]
    nh = w // HEAD
    nc = seq // CHUNK
    ng = _lane_groups(nh)

    def head(mask, q_ref, f_ref, i_ref, g_ref, lbg_ref, nw_ref, opre_ref, st_ref, dm_ref,
             dq_ref, df_ref, di_ref, dg_ref, small_ref, k_c, b_c, v_c, dst, dlb_s, dwn_s):
        bi = pl.program_id(1)
        lb = _lower_bound(lbg_ref)
        nwv = nw_ref[...]
        dst[...] = jnp.zeros_like(dst)

        @pl.when(bi == 0)
        def _():
            dlb_s[...] = jnp.zeros_like(dlb_s)
            dwn_s[...] = jnp.zeros_like(dwn_s)

        def chunk(it, carry):
            c = nc - 1 - it
            rows = pl.ds(pl.multiple_of(c * CHUNK, CHUNK), CHUNK)
            qr = q_ref[rows, :]
            sq = _sigmoid(qr)
            q = qr * sq
            sg = _sigmoid(f_ref[rows, :])
            f = lb + (1.0 - lb) * sg
            k = 1.0 - f
            b = _cumsum_fwd(jnp.log(f))
            v = i_ref[rows, :]
            k_c[...] = k
            b_c[...] = b
            v_c[...] = v
            o = opre_ref[rows, :]
            rn = lax.rsqrt(jnp.mean(o * o, axis=-1, keepdims=True) + EPS)
            nhat = o * rn
            dm = dm_ref[rows, :]
            gr = g_ref[rows, :]
            sgr = _sigmoid(gr)
            dnw = dm * (gr * sgr)
            dg_ref[rows, :] = (dm * (nhat * nwv) * _dsilu(gr, sgr)).astype(BF16)
            dwn_s[...] += jnp.sum(dnw * nhat, axis=0, keepdims=True)
            dn = dnw * nwv
            do = rn * (dn - nhat * jnp.mean(dn * nhat, axis=-1, keepdims=True))
            s_t = st_ref[c]
            ds = dst[...]
            eb = jnp.exp(b)
            qb = q * eb
            bl = b_c[CHUNK - 1:CHUNK, :]
            ebl = jnp.exp(bl)
            kdec = jnp.exp(bl - b)
            kb = k * kdec
            dqb = _dot(do, s_t, NN)
            dkb = _dot(v, ds, NN)
            dv = _dot(kb, ds, NT)
            d_ebl = jnp.sum(ds * s_t, axis=0, keepdims=True)
            dst[...] = ds * ebl + _dot(do, qb, TN)
            dq = dqb * eb
            dk = dkb * kdec
            t_kb = dkb * kb
            db = dqb * qb - t_kb
            db_last = jnp.sum(t_kb, axis=0, keepdims=True) + d_ebl * ebl
            offs, eq, ek, kt, vs, qt, a = _offdiag_setup(q, k, b, v, b_c, mask)
            da = jnp.where(mask, _dot(do, vs, NT), 0.0)
            dvs = _dot(a, do, TN)
            dqt = _dot(da, kt, NN)
            dkt = _dot(da, qt, TN)
            dq = dq + dqt * eq
            db = db + dqt * qt
            gk, gb = dkt * ek, dkt * kt
            zero8 = jnp.zeros((8, HEAD), F32)
            off_k, off_v, off_b = ([zero8] * (CHUNK // 8) for _ in range(3))
            for i in range(1, NSUB):
                lo, hi = offs[i]
                for p in range((hi - lo) // 8):
                    rows8 = slice(lo + 8 * p, lo + 8 * p + 8)
                    off_k[p] = off_k[p] + gk[rows8]
                    off_b[p] = off_b[p] - gb[rows8]
                    off_v[p] = off_v[p] + dvs[rows8]
            npv = SUB // 8
            dq_d, dk_d, dv_d, db_d = [], [], [], []
            for i in range(NSUB):
                aq = [zero8 for _ in range(npv)]
                ak = [off_k[i * npv + p] for p in range(npv)]
                av = [off_v[i * npv + p] for p in range(npv)]
                ab = [off_b[i * npv + p] for p in range(npv)]
                for j in range(SUB):
                    r = SUB * i + j
                    bj, kj, vj = b_c[r:r + 1, :], k_c[r:r + 1, :], v_c[r:r + 1, :]
                    sk = jnp.zeros((1, HEAD), F32)
                    sv = jnp.zeros((1, HEAD), F32)
                    for p in range(j // 8, npv):
                        lo = SUB * i + 8 * p
                        d = jnp.exp(b[lo:lo + 8] - bj)
                        if 8 * p < j:
                            d = jnp.where(_iota_rows(8) + 8 * p >= j, d, 0.0)
                        qd = q[lo:lo + 8] * d
                        dop = do[lo:lo + 8]
                        a_j = jnp.sum(qd * kj, axis=-1, keepdims=True)
                        da_j = jnp.sum(dop * vj, axis=-1, keepdims=True)
                        aq[p] = aq[p] + (da_j * d) * kj
                        pm = da_j * qd
                        rk = pm * kj
                        ab[p] = ab[p] + rk
                        sk = sk + jnp.sum(pm, axis=0, keepdims=True)
                        sv = sv + jnp.sum(a_j * dop, axis=0, keepdims=True)
                    pj = j // 8
                    here = _iota_rows(8) == (j - 8 * pj)
                    ak[pj] = ak[pj] + jnp.where(here, sk, 0.0)
                    av[pj] = av[pj] + jnp.where(here, sv, 0.0)
                    ab[pj] = ab[pj] - jnp.where(here, sk * kj, 0.0)
                dq_d.extend(aq)
                dk_d.extend(ak)
                dv_d.extend(av)
                db_d.extend(ab)
            dq = dq + jnp.concatenate(dq_d, axis=0)
            dk = dk + jnp.concatenate(dk_d, axis=0)
            dv = dv + jnp.concatenate(dv_d, axis=0)
            db = db + jnp.concatenate(db_d, axis=0)
            db = db + jnp.where(_iota_rows(CHUNK) == CHUNK - 1, db_last, 0.0)
            dgl = _cumsum_bwd(db)
            dfv = dgl / f - dk
            dlb_s[...] += jnp.sum(dfv * (1.0 - sg), axis=0, keepdims=True)
            df_ref[rows, :] = (dfv * (1.0 - lb) * (sg * (1.0 - sg))).astype(BF16)
            dq_ref[rows, :] = (dq * _dsilu(qr, sq)).astype(BF16)
            di_ref[rows, :] = dv.astype(BF16)
            return carry

        def finish():
            @pl.when(bi == nb - 1)
            def _():
                dgam = dlb_s[...] * lb * (1.0 - lb)
                small_ref[...] = jnp.zeros_like(small_ref)
                small_ref[0:1, :] = dgam
                small_ref[1:2, :] = -dgam
                small_ref[2:3, :] = dwn_s[...]

        return chunk, finish

    def body(*refs):
        mask = _offdiag_mask()
        heads = [head(mask, *_group_views(refs, g, "lllllllgl" + "lllll" + "ggggll")) for g in range(ng)]

        def step(it, carry):
            for chunk, _ in heads:
                chunk(it, carry)
            return carry

        lax.fori_loop(0, nc, step, 0)
        for _, finish in heads:
            finish()

    gw = ng * HEAD

    def col(off):
        return pl.BlockSpec((seq, gw), lambda h, b: (b, off * (nh // ng) + h))

    vec = lambda r: pl.BlockSpec((r, gw), lambda h, b: (0, h))
    blk = pl.BlockSpec((seq, gw), lambda h, b: (b, h))
    dshape = jax.ShapeDtypeStruct((t, w), BF16)
    return _pcall(
        body, n_in=9, args=[proj, proj, proj, proj, lbg, nw, opre, states, dmix], deps=deps,
        name=name, grid=(nh // ng, nb),
        in_specs=[col(0), col(1), col(2), col(3), vec(2), vec(1), blk,
                  pl.BlockSpec((None, ng, nc, HEAD, HEAD), lambda h, b: (b, h, 0, 0, 0)), blk],
        out_specs=[blk, blk, blk, blk, vec(8)],
        out_shape=[dshape, dshape, dshape, dshape, jax.ShapeDtypeStruct((8, w), F32)],
        scratch_shapes=[pltpu.VMEM((ng, CHUNK, HEAD), F32)] * 3 + [pltpu.VMEM((ng, HEAD, HEAD), F32)]
        + [pltpu.VMEM((1, gw), F32)] * 2,
        compiler_params=_cp(dimension_semantics=("parallel", "arbitrary")))


def _expm1(x):
    poly = x * (1.0 + x * (0.5 + x * (1.0 / 6 + x * (1.0 / 24 + x * (1.0 / 120 + x * (1.0 / 720))))))
    return jnp.where(jnp.abs(x) < 0.25, poly, jnp.exp(x) - 1.0)


def _softplus_neg(lam):
    x = -lam
    e = jnp.exp(-jnp.abs(x))
    u = 1.0 + e
    l1p = jnp.where(u == 1.0, e, jnp.log(u) * (e / jnp.where(u == 1.0, 1.0, u - 1.0)))
    return jnp.maximum(x, 0.0) + l1p


_GELU_C = math.sqrt(2.0 / math.pi)


def _gelu(y):
    return 0.5 * y * (1.0 + jnp.tanh(_GELU_C * (y + 0.044715 * (y * y * y))))


def _dgelu(y):
    th = jnp.tanh(_GELU_C * (y + 0.044715 * (y * y * y)))
    return 0.5 * (1.0 + th) + 0.5 * y * (1.0 - th * th) * (_GELU_C * (1.0 + 3 * 0.044715 * (y * y)))


def _lru_gates(xr, prev8, cw_ref, cb, wa_ref, ba, wx_ref, bx, sp, first):
    x3, x2, x1 = _shift_down(prev8, xr, 3), _shift_down(prev8, xr, 2), _shift_down(prev8, xr, 1)
    xb = cb + x3 * cw_ref[0:1, :]
    xb = xb + x2 * cw_ref[1:2, :]
    xb = xb + x1 * cw_ref[2:3, :]
    xb = xb + xr * cw_ref[3:4, :]
    r = _sigmoid(_dot(xb, wa_ref[...], NN) + ba)
    ig = _sigmoid(_dot(xb, wx_ref[...], NN) + bx)
    la = (-LRU_C * r) * sp
    a = jnp.exp(la)
    start = jnp.logical_and(first, _iota_rows(xr.shape[0]) == 0)
    mult = jnp.where(start, 1.0, jnp.sqrt(-_expm1(2.0 * la)))
    return (x3, x2, x1), xb, r, ig, a, mult, start


def _lru_fwd(proj, cw, cb, wa, ba, wx, bx, lam, *, nb, seq, name, deps=()):
    t = proj.shape[0]
    w = cb.shape[1]
    nblk = w // HEAD
    nc = seq // CHUNK
    ng = _lane_groups(nblk)

    def block(x_ref, y_ref, cw_ref, cb_ref, wa_ref, ba_ref, wx_ref, bx_ref, lam_ref, p_ref, h_ref):
        sp = _softplus_neg(lam_ref[...])
        cb_v, ba_v, bx_v = cb_ref[...], ba_ref[...], bx_ref[...]

        def tile(c, carry):
            hc, prev8 = carry
            base = pl.multiple_of(c * CHUNK, CHUNK)
            rows = pl.ds(base, CHUNK)
            xr = x_ref[rows, :]
            _, xb, _, ig, a, mult, _ = _lru_gates(xr, prev8, cw_ref, cb_v, wa_ref, ba_v, wx_ref, bx_v, sp, c == 0)
            ap, up = _scan_fwd(a, xb * ig * mult)
            h = up + ap * hc
            h_ref[rows, :] = h
            p_ref[rows, :] = h * _gelu(y_ref[rows, :])
            h_last = jnp.sum(jnp.where(_iota_rows(8) == 7, h[CHUNK - 8:], 0.0), axis=0, keepdims=True)
            return h_last, xr[CHUNK - 8:]

        return tile

    def body(*refs):
        tiles = [block(*_group_views(refs, g, "llllglgll" + "ll")) for g in range(ng)]

        def step(c, carries):
            return tuple(tile(c, carry) for tile, carry in zip(tiles, carries))

        lax.fori_loop(0, nc, step, ((jnp.zeros((1, HEAD), F32), jnp.zeros((8, HEAD), F32)),) * ng)

    gw = ng * HEAD

    def col(off):
        return pl.BlockSpec((seq, gw), lambda n, b: (b, off * (nblk // ng) + n))

    vec = lambda r: pl.BlockSpec((r, gw), lambda n, b: (0, n))
    mat = pl.BlockSpec((ng, HEAD, HEAD), lambda n, b: (n, 0, 0))
    blk = pl.BlockSpec((seq, gw), lambda n, b: (b, n))
    return _pcall(
        body, n_in=9, args=[proj, proj, cw, cb, wa, ba, wx, bx, lam], deps=deps, name=name, grid=(nblk // ng, nb),
        in_specs=[col(4), col(5), vec(4), vec(1), mat, vec(1), mat, vec(1), vec(1)],
        out_specs=[blk, blk],
        out_shape=[jax.ShapeDtypeStruct((t, w), F32), jax.ShapeDtypeStruct((t, w), F32)],
        compiler_params=_cp(dimension_semantics=("parallel", "parallel")))


def _lru_bwd(proj, cw, cb, wa, ba, wx, bx, lam, hsv, dp, *, nb, seq, name):
    t = proj.shape[0]
    w = cb.shape[1]
    nblk = w // HEAD
    nc = seq // CHUNK
    ng = _lane_groups(nblk)

    def block(x_ref, y_ref, cw_ref, cb_ref, wa_ref, ba_ref, wx_ref, bx_ref, lam_ref, h_ref, dp_ref,
              dx_ref, dy_ref, small_ref, dwa_ref, dwx_ref, a_c, g_c, acc, dwa_s, dwx_s):
        bi = pl.program_id(1)
        lamv = lam_ref[...]
        sp = _softplus_neg(lamv)
        cb_v, ba_v, bx_v = cb_ref[...], ba_ref[...], bx_ref[...]

        @pl.when(bi == 0)
        def _():
            acc[...] = jnp.zeros_like(acc)
            dwa_s[...] = jnp.zeros_like(dwa_s)
            dwx_s[...] = jnp.zeros_like(dwx_s)

        def tile(it, carry):
            g_next, a_next, dxb_next8 = carry
            c = nc - 1 - it
            base = pl.multiple_of(c * CHUNK, CHUNK)
            rows = pl.ds(base, CHUNK)
            before = pl.ds(pl.multiple_of(jnp.maximum(base - 8, 0), 8), 8)
            inner = c > 0
            xr = x_ref[rows, :]
            prev8 = jnp.where(inner, x_ref[before, :], 0.0)
            (x3, x2, x1), xb, r, ig, a, mult, start = _lru_gates(
                xr, prev8, cw_ref, cb_v, wa_ref, ba_v, wx_ref, bx_v, sp, c == 0)
            h = h_ref[rows, :]
            h_m1 = _shift_down(jnp.where(inner, h_ref[before, :], 0.0), h, 1)
            yv = y_ref[rows, :]
            dpv = dp_ref[rows, :]
            dy_ref[rows, :] = (dpv * h * _dgelu(yv)).astype(BF16)
            dh = dpv * _gelu(yv)
            a_up = _shift_up(a, jnp.broadcast_to(a_next, (8, HEAD)), 1)
            ap, gp = _scan_bwd(a_up, dh)
            g = gp + ap * g_next
            a_c[...] = a
            g_c[...] = g
            da = g * h_m1
            gx = g * xb
            dxb = g * ig * mult
            dig = gx * mult
            dmult = jnp.where(start, 0.0, gx * ig)
            dla = da * a - dmult * (a * a) / mult
            dzr = (dla * (-LRU_C * sp)) * (r * (1.0 - r))
            dzi = dig * (ig * (1.0 - ig))
            dxb = dxb + _dot(dzr, wa_ref[...], NT) + _dot(dzi, wx_ref[...], NT)
            dwa_s[...] += _dot(xb, dzr, TN)
            dwx_s[...] += _dot(xb, dzi, TN)
            d1, d2, d3 = (_shift_up(dxb, dxb_next8, s) for s in (1, 2, 3))
            dx = dxb * cw_ref[3:4, :] + d1 * cw_ref[2:3, :] + d2 * cw_ref[1:2, :] + d3 * cw_ref[0:1, :]
            dx_ref[rows, :] = dx.astype(BF16)
            colsum = lambda z: jnp.sum(z, axis=0, keepdims=True)
            acc[0:1, :] += colsum(x3 * dxb)
            acc[1:2, :] += colsum(x2 * dxb)
            acc[2:3, :] += colsum(x1 * dxb)
            acc[3:4, :] += colsum(xr * dxb)
            acc[4:5, :] += colsum(dxb)
            acc[5:6, :] += colsum(dzr)
            acc[6:7, :] += colsum(dzi)
            acc[7:8, :] += colsum(dla * (-LRU_C * r))
            return g_c[0:1, :], a_c[0:1, :], dxb[0:8]

        def finish():
            @pl.when(bi == nb - 1)
            def _():
                small_ref[...] = acc[...]
                small_ref[7:8, :] = acc[7:8, :] * (-_sigmoid(-lamv))
                dwa_ref[...] = dwa_s[...]
                dwx_ref[...] = dwx_s[...]

        return tile, finish

    def body(*refs):
        blocks = [block(*_group_views(refs, g, "llllglgllll" + "lllgg" + "gglgg")) for g in range(ng)]

        def step(it, carries):
            return tuple(tile(it, carry) for (tile, _), carry in zip(blocks, carries))

        zero = jnp.zeros((1, HEAD), F32)
        lax.fori_loop(0, nc, step, ((zero, zero, jnp.zeros((8, HEAD), F32)),) * ng)
        for _, finish in blocks:
            finish()

    gw = ng * HEAD

    def col(off):
        return pl.BlockSpec((seq, gw), lambda n, b: (b, off * (nblk // ng) + n))

    vec = lambda r: pl.BlockSpec((r, gw), lambda n, b: (0, n))
    mat = pl.BlockSpec((ng, HEAD, HEAD), lambda n, b: (n, 0, 0))
    blk = pl.BlockSpec((seq, gw), lambda n, b: (b, n))
    dshape = jax.ShapeDtypeStruct((t, w), BF16)
    return pl.pallas_call(
        body, name=name, grid=(nblk // ng, nb),
        in_specs=[col(4), col(5), vec(4), vec(1), mat, vec(1), mat, vec(1), vec(1), blk, blk],
        out_specs=[blk, blk, vec(8), mat, mat],
        out_shape=[dshape, dshape, jax.ShapeDtypeStruct((8, w), F32),
                   jax.ShapeDtypeStruct((nblk, HEAD, HEAD), F32), jax.ShapeDtypeStruct((nblk, HEAD, HEAD), F32)],
        scratch_shapes=[pltpu.VMEM((ng, CHUNK, HEAD), F32)] * 2 + [pltpu.VMEM((8, gw), F32)]
        + [pltpu.VMEM((ng, HEAD, HEAD), F32)] * 2,
        compiler_params=_cp(dimension_semantics=("parallel", "arbitrary")),
    )(proj, proj, cw, cb, wa, ba, wx, bx, lam, hsv, dp)


def _ffn_conv(x, prev8, cw_ref, cbv):
    x2, x1 = _shift_down(prev8, x, 2), _shift_down(prev8, x, 1)
    y = cbv + x2 * cw_ref[0:1, :]
    y = y + x1 * cw_ref[1:2, :]
    y = y + x * cw_ref[2:3, :]
    return (x2, x1), y


def _ffn_act_fwd(up, cw, cb, *, nb, seq, name, deps=()):
    t, f2 = up.shape
    f = f2 // 2
    tc = _tile(f, 256)
    nj = f // tc
    rt = _tile(seq, 256, 16)
    nr = seq // rt

    def body(g_ref, v_ref, cwg_ref, cwv_ref, cbg_ref, cbv_ref, o_ref):
        cbg, cbv = cbg_ref[...], cbv_ref[...]

        def tile(c, carry):
            pg, pv = carry
            rows = pl.ds(pl.multiple_of(c * rt, rt), rt)
            xg, xv = g_ref[rows, :], v_ref[rows, :]
            _, gate = _ffn_conv(xg, pg, cwg_ref, cbg)
            _, val = _ffn_conv(xv, pv, cwv_ref, cbv)
            o_ref[rows, :] = ((gate * _sigmoid(gate)) * val).astype(BF16)
            return xg[rt - 8:], xv[rt - 8:]

        z = jnp.zeros((8, tc), F32)
        lax.fori_loop(0, nr, tile, (z, z))

    gcol = pl.BlockSpec((seq, tc), lambda j, b: (b, j))
    vcol = pl.BlockSpec((seq, tc), lambda j, b: (b, nj + j))
    gv = lambda r: pl.BlockSpec((r, tc), lambda j, b: (0, j))
    vv = lambda r: pl.BlockSpec((r, tc), lambda j, b: (0, nj + j))
    return _pcall(
        body, n_in=6, args=[up, up, cw, cw, cb, cb], deps=deps, name=name, grid=(nj, nb),
        in_specs=[gcol, vcol, gv(3), vv(3), gv(1), vv(1)], out_specs=gcol,
        out_shape=jax.ShapeDtypeStruct((t, f), BF16),
        compiler_params=_cp(dimension_semantics=("parallel", "parallel")))


def _ffn_act_bwd(dact, up, cw, cb, *, nb, seq, name, deps=()):
    t, f2 = up.shape
    f = f2 // 2
    tc = _tile(f, 256)
    nj = f // tc
    rt = _tile(seq, 256, 16)
    nr = seq // rt

    def body(da_ref, g_ref, v_ref, cwg_ref, cwv_ref, cbg_ref, cbv_ref,
             dg_ref, dv_ref, sg_ref, sv_ref, eg, ev, accg, accv):
        bi = pl.program_id(1)
        cbg, cbv = cbg_ref[...], cbv_ref[...]

        @pl.when(bi == 0)
        def _():
            accg[...] = jnp.zeros_like(accg)
            accv[...] = jnp.zeros_like(accv)

        colsum = lambda z: jnp.sum(z, axis=0, keepdims=True)

        def first(c, carry):
            pg, pv = carry
            rows = pl.ds(pl.multiple_of(c * rt, rt), rt)
            xg, xv = g_ref[rows, :], v_ref[rows, :]
            (g2, g1), gate = _ffn_conv(xg, pg, cwg_ref, cbg)
            (v2, v1), val = _ffn_conv(xv, pv, cwv_ref, cbv)
            s = _sigmoid(gate)
            da = da_ref[rows, :]
            dgate = da * val * _dsilu(gate, s)
            dval = da * (gate * s)
            eg[rows, :] = dgate
            ev[rows, :] = dval
            for acc, (s2, s1, s0), d in ((accg, (g2, g1, xg), dgate), (accv, (v2, v1, xv), dval)):
                acc[0:1, :] += colsum(s2 * d)
                acc[1:2, :] += colsum(s1 * d)
                acc[2:3, :] += colsum(s0 * d)
                acc[3:4, :] += colsum(d)
            return xg[rt - 8:], xv[rt - 8:]

        z = jnp.zeros((8, tc), F32)
        lax.fori_loop(0, nr, first, (z, z))

        def second(c, carry):
            base = pl.multiple_of(c * rt, rt)
            rows = pl.ds(base, rt)
            after = pl.ds(pl.multiple_of(jnp.minimum(base + rt, seq - 8), 8), 8)
            more = c < nr - 1
            for e, cw_ref, d_ref in ((eg, cwg_ref, dg_ref), (ev, cwv_ref, dv_ref)):
                d = e[rows, :]
                n8 = jnp.where(more, e[after, :], 0.0)
                d1, d2 = _shift_up(d, n8, 1), _shift_up(d, n8, 2)
                d_ref[rows, :] = (d * cw_ref[2:3, :] + d1 * cw_ref[1:2, :] + d2 * cw_ref[0:1, :]).astype(BF16)
            return carry

        lax.fori_loop(0, nr, second, 0)

        @pl.when(bi == nb - 1)
        def _():
            sg_ref[...] = accg[...]
            sv_ref[...] = accv[...]

    gcol = pl.BlockSpec((seq, tc), lambda j, b: (b, j))
    vcol = pl.BlockSpec((seq, tc), lambda j, b: (b, nj + j))
    gv = lambda r: pl.BlockSpec((r, tc), lambda j, b: (0, j))
    vv = lambda r: pl.BlockSpec((r, tc), lambda j, b: (0, nj + j))
    dshape = jax.ShapeDtypeStruct((t, f), BF16)
    sshape = jax.ShapeDtypeStruct((8, f), F32)
    return _pcall(
        body, n_in=7, args=[dact, up, up, cw, cw, cb, cb], deps=deps, name=name, grid=(nj, nb),
        in_specs=[gcol, gcol, vcol, gv(3), vv(3), gv(1), vv(1)],
        out_specs=[gcol, gcol, gv(8), gv(8)], out_shape=[dshape, dshape, sshape, sshape],
        scratch_shapes=[pltpu.VMEM((seq, tc), F32)] * 2 + [pltpu.VMEM((8, tc), F32)] * 2,
        compiler_params=_cp(dimension_semantics=("parallel", "arbitrary")))


def _adamw_math(wv, g, mv, vv):
    m = ADAM_B1 * mv + (1.0 - ADAM_B1) * g
    v = ADAM_B2 * vv + (1.0 - ADAM_B2) * (g * g)
    m_hat = m / (1.0 - ADAM_B1 ** ADAM_STEP)
    v_hat = v / (1.0 - ADAM_B2 ** ADAM_STEP)
    delta = -ADAM_LR * (m_hat / (jnp.sqrt(v_hat) + ADAM_EPS) + ADAM_WD * wv)
    return delta, m, v


def _adamw_shard(wv, mv, vv, p_own, rb, *, name):
    _, r, cdim = wv.shape
    tr = _row_tile(r, cdim)

    def body(w_ref, m_ref, v_ref, p_ref, rb_ref, g_ref, d_ref, mo_ref, vo_ref):
        g = p_ref[...].astype(F32)
        for k in range(3):
            g = g + rb_ref[k].astype(F32)
        g_ref[...] = g
        d_ref[...], mo_ref[...], vo_ref[...] = _adamw_math(w_ref[...], g, m_ref[...], v_ref[...])

    row = pl.BlockSpec((tr, cdim), lambda i: (i, 0))
    row3 = pl.BlockSpec((None, tr, cdim), lambda i: (0, i, 0))
    shp = jax.ShapeDtypeStruct((1, r, cdim), F32)
    return pl.pallas_call(
        body, name=name, grid=(r // tr,),
        in_specs=[row3, row3, row3, row, pl.BlockSpec((3, tr, cdim), lambda i: (0, i, 0))],
        out_specs=[row3] * 4, out_shape=[shp] * 4, compiler_params=_cp(dimension_semantics=("parallel",)),
    )(wv, mv, vv, p_own, rb)


def _adamw_packed(wv, g, mv, vv, *, name):
    r = wv.shape[0]
    tr = _tile(r, 256, 8)

    def body(w_ref, g_ref, m_ref, v_ref, d_ref, mo_ref, vo_ref):
        d_ref[...], mo_ref[...], vo_ref[...] = _adamw_math(w_ref[...], g_ref[...], m_ref[...], v_ref[...])

    row = pl.BlockSpec((tr, LANE), lambda i: (i, 0))
    shp = jax.ShapeDtypeStruct((r, LANE), F32)
    return pl.pallas_call(
        body, name=name, grid=(r // tr,), in_specs=[row] * 4, out_specs=[row] * 3, out_shape=[shp] * 3,
        compiler_params=_cp(dimension_semantics=("parallel",)),
    )(wv, g, mv, vv)


def _place():
    return lax.axis_index("x"), lax.axis_index("y"), lax.axis_index("c")


def _all_reduce_packed(p, *, name):
    r = p.shape[0]
    tr = _tile(r, 256, 8)

    def body(p_ref, o_ref, buf, send, recv, lsem):
        x, y, c = _place()
        me, sibling = (x, y, c), (x, y, 1 - c)
        chips = [(1 - x, y), (x, 1 - y), (1 - x, 1 - y)]

        def slot(px, py, pc):
            return 4 * px + 2 * py + pc

        def copy(k, block, to, src=None):
            dst = buf.at[slot(*block)]
            return pltpu.make_async_remote_copy(
                src_ref=dst if src is None else src, dst_ref=dst, send_sem=send.at[k], recv_sem=recv.at[k],
                device_id=to, device_id_type=MESH)

        mine = pltpu.make_async_copy(p_ref, buf.at[slot(*me)], lsem)
        mine.start()
        first = [copy(0, me, sibling, src=p_ref)]
        first += [copy(1 + j, me, (*chip, c), src=p_ref) for j, chip in enumerate(chips)]
        for cp in first:
            cp.start()
        passed = [copy(4 + j, (*chip, c), sibling) for j, chip in enumerate(chips)]
        for j, chip in enumerate(chips):
            copy(1 + j, (*chip, c), me).wait_recv()
            passed[j].start()
        copy(0, sibling, me).wait_recv()
        for j, chip in enumerate(chips):
            copy(4 + j, (*chip, 1 - c), me).wait_recv()
        for cp in first + passed:
            cp.wait_send()
        mine.wait()

        def add(i, carry):
            rows = pl.ds(pl.multiple_of(i * tr, tr), tr)
            s = buf[0, rows, :]
            for d in range(1, NDEV):
                s = s + buf[d, rows, :]
            o_ref[rows, :] = s
            return carry

        lax.fori_loop(0, r // tr, add, 0)

    vm = pl.BlockSpec(memory_space=pltpu.VMEM)
    return pl.pallas_call(
        body, name=name, in_specs=[vm], out_specs=vm, out_shape=jax.ShapeDtypeStruct(p.shape, p.dtype),
        scratch_shapes=[pltpu.VMEM((NDEV,) + p.shape, p.dtype), pltpu.SemaphoreType.DMA((7,)),
                        pltpu.SemaphoreType.DMA((7,)), pltpu.SemaphoreType.DMA],
        compiler_params=_cp(),
    )(p)


HBM = pl.BlockSpec(memory_space=pltpu.HBM)
SEM = pl.BlockSpec(memory_space=pltpu.SEMAPHORE)
EFFECT = pltpu.SideEffectType.DATAFLOW_SIDE_EFFECTING


def _plan_copies(plan, s_refs, l_refs, send, recv):
    def pick(kind, a, idx):
        ref = (s_refs if kind == "s" else l_refs)[a]
        return ref if idx is None else ref.at[idx]

    return [pltpu.make_async_remote_copy(
        src_ref=pick(*src), dst_ref=pick(*dst), send_sem=send.at[i], recv_sem=recv.at[i],
        device_id=to, device_id_type=MESH) for i, (src, dst, to) in enumerate(plan(*_place()))]


def _xfer_start(srcs, lands, plan, *, name, deps=()):
    ns, nl = len(srcs), len(lands)
    nd = len(deps)
    ncopy = len(plan(0, 0, 0))

    def body(*refs):
        s_refs, l_refs = refs[:ns], refs[ns:ns + nl]
        send, recv = refs[ns + nl + nd], refs[ns + nl + nd + 1]
        token = refs[-1]
        for cp in _plan_copies(plan, s_refs, l_refs, send, recv):
            cp.start()
        token[...] = jnp.zeros_like(token)

    bufs = list(srcs) + list(lands)
    outs = pl.pallas_call(
        body, name=name,
        out_shape=(pltpu.SemaphoreType.DMA((ncopy,)), pltpu.SemaphoreType.DMA((ncopy,)),
                   *[pltpu.HBM(b.shape, b.dtype) for b in bufs], jax.ShapeDtypeStruct((8, LANE), F32)),
        in_specs=[HBM] * (ns + nl) + [ANY] * nd,
        out_specs=(SEM, SEM, *[HBM] * (ns + nl), pl.BlockSpec(memory_space=pltpu.VMEM)),
        input_output_aliases={i: 2 + i for i in range(ns + nl)},
        compiler_params=pltpu.CompilerParams(has_side_effects=EFFECT),
    )(*[pltpu.with_memory_space_constraint(b, pltpu.HBM) for b in bufs], *deps)
    return outs[0], outs[1], list(outs[2:2 + ns]), list(outs[2 + ns:2 + ns + nl]), outs[-1]


def _xfer_wait(send, recv, srcs, lands, plan, after, *, name):
    ns, nl = len(srcs), len(lands)

    def body(*refs):
        s_refs, l_refs = refs[:ns], refs[ns:ns + nl]
        send_ref, recv_ref = refs[ns + nl], refs[ns + nl + 1]
        for cp in _plan_copies(plan, s_refs, l_refs, send_ref, recv_ref):
            cp.wait_send()
            cp.wait_recv()

    bufs = list(srcs) + list(lands)
    outs = pl.pallas_call(
        body, name=name, out_shape=tuple(pltpu.HBM(b.shape, b.dtype) for b in bufs),
        in_specs=[HBM] * (ns + nl) + [SEM, SEM, ANY], out_specs=tuple([HBM] * (ns + nl)),
        input_output_aliases={i: i for i in range(ns + nl)},
        compiler_params=pltpu.CompilerParams(has_side_effects=EFFECT),
    )(*bufs, send, recv, after)
    return list(outs[:ns]), list(outs[ns:])


def _slot8(px, py, pc):
    return 4 * px + 2 * py + pc


def _slot4(r):
    return lambda px, py, pc: (2 * px + py, pl.ds(pc * r, r))


def _plan_gather_first(slots):
    def plan(x, y, c):
        peers = [(x, y, c), (x, y, 1 - c), (1 - x, y, c), (x, 1 - y, c), (1 - x, 1 - y, c)]
        return [(("s", a, None), ("l", a, slot(x, y, c)), to) for a, slot in enumerate(slots) for to in peers]
    return plan


def _plan_gather_pass(slots):
    def plan(x, y, c):
        chips = [(1 - x, y), (x, 1 - y), (1 - x, 1 - y)]
        return [(("l", a, slot(px, py, c)), ("l", a, slot(px, py, c)), (x, y, 1 - c))
                for a, slot in enumerate(slots) for px, py in chips]
    return plan


def _plan_rs_sibling(r):
    def plan(x, y, c):
        return [(("s", 0, (j, pl.ds((1 - c) * r, r))), ("l", 0, j), (x, y, 1 - c)) for j in range(4)]
    return plan


def _plan_rs_plane():
    def plan(x, y, c):
        chips = [(1 - x, y), (x, 1 - y), (1 - x, 1 - y)]
        return [(("s", 0, 2 * px + py), ("l", 0, k), (px, py, c)) for k, (px, py) in enumerate(chips)]
    return plan


def _add_pairs(g, rcv, cidx, *, name):
    _, r, cdim = rcv.shape
    tr = _row_tile(r, cdim, budget=1048576)
    per = r // tr

    def body(c_ref, g_ref, r_ref, o_ref):
        o_ref[...] = (g_ref[...].astype(F32) + r_ref[...].astype(F32)).astype(o_ref.dtype)

    grid_spec = pltpu.PrefetchScalarGridSpec(
        num_scalar_prefetch=1, grid=(4, per),
        in_specs=[pl.BlockSpec((None, tr, cdim), lambda j, i, c_ref: (j, c_ref[0] * per + i, 0)),
                  pl.BlockSpec((None, tr, cdim), lambda j, i, c_ref: (j, i, 0))],
        out_specs=pl.BlockSpec((None, tr, cdim), lambda j, i, c_ref: (j, i, 0)))
    return pl.pallas_call(
        body, name=name, grid_spec=grid_spec, out_shape=jax.ShapeDtypeStruct((4, r, cdim), g.dtype),
        compiler_params=_cp(dimension_semantics=("parallel", "parallel")),
    )(cidx, g, rcv)


def _pack(arrs):
    flat = jnp.concatenate([a.reshape(-1).astype(F32) for a in arrs])
    n = flat.shape[0]
    rows = -(-n // LANE)
    rows = -(-rows // 256) * 256
    return jnp.pad(flat, (0, rows * LANE - n)).reshape(rows, LANE)


def _unpack(packed, shapes):
    flat = packed.reshape(-1)
    out, o = [], 0
    for s in shapes:
        n = math.prod(s)
        out.append(flat[o:o + n].reshape(s))
        o += n
    return out


def _pad_blocks(a, w, wp):
    lead = a.shape[:-1]
    k = a.shape[-1] // w
    pads = [(0, 0)] * (len(lead) + 1) + [(0, wp - w)]
    return jnp.pad(a.reshape(*lead, k, w), pads).reshape(*lead, k * wp)


def kernel(x, ln1_w, w_in, lb_gamma, hg_norm_w, lru_conv_w, lru_conv_b, lru_wa, lru_ba, lru_wx, lru_bx, lru_lambda, lru_norm_w, w_out, ln2_w, ffn_w_up, ffn_conv_w, ffn_conv_b, ffn_w_down, final_norm_w, loss_target, m_ln1_w, m_w_in, m_lb_gamma, m_hg_norm_w, m_lru_conv_w, m_lru_conv_b, m_lru_wa, m_lru_ba, m_lru_wx, m_lru_bx, m_lru_lambda, m_lru_norm_w, m_w_out, m_ln2_w, m_ffn_w_up, m_ffn_conv_w, m_ffn_conv_b, m_ffn_w_down, m_final_norm_w, v_ln1_w, v_w_in, v_lb_gamma, v_hg_norm_w, v_lru_conv_w, v_lru_conv_b, v_lru_wa, v_lru_ba, v_lru_wx, v_lru_bx, v_lru_lambda, v_lru_norm_w, v_w_out, v_ln2_w, v_ffn_w_up, v_ffn_conv_w, v_ffn_conv_b, v_ffn_w_down, v_final_norm_w):
    nb, seq, d = x.shape
    t = nb * seq
    wmix = d // 2
    in_sh = w_in.shape[2]
    up_sh = ffn_w_up.shape[2]
    up_pad = -(-up_sh // LANE) * LANE
    hs = ffn_w_down.shape[1]
    fpad = 4 * up_pad
    cx, cy, cc = _place()
    me = 4 * cx + 2 * cy + cc
    plane = 2 * cx + cy

    def gather(shards, lands, slots, tag, deps=()):
        st = _xfer_start(shards, lands, _plan_gather_first(slots), name=f"gather_{tag}_start", deps=deps)
        return (st, slots, tag), st[4]

    def pass_on(g, after):
        st, slots, tag = g
        _, lands = _xfer_wait(st[0], st[1], st[2], st[3], _plan_gather_first(slots), after, name=f"gather_{tag}_wait")
        st2 = _xfer_start([], lands, _plan_gather_pass(slots), name=f"gather_{tag}_pass")
        return (st2, slots, tag), st2[4]

    def gathered(g, after):
        st2, slots, tag = g
        return _xfer_wait(st2[0], st2[1], [], st2[3], _plan_gather_pass(slots), after, name=f"gather_{tag}_done")[1]

    land8 = lambda s: lax.empty((NDEV,) + s.shape, s.dtype)

    win_s = w_in[0].astype(BF16)
    wout_s = w_out[0].astype(BF16)
    wup_s = jnp.pad(ffn_w_up[0], ((0, 0), (0, up_pad - up_sh))).astype(BF16)
    wdn_s = ffn_w_down[0].astype(BF16)
    fcw_s = jnp.pad(ffn_conv_w[0], ((0, 0), (0, up_pad - up_sh)))
    ga, tok = gather([win_s, lru_conv_w[0], fcw_s], [land8(win_s), land8(lru_conv_w[0]), land8(fcw_s)],
                     [_slot8] * 3, "a")
    go, tok = gather([wout_s], [land8(wout_s)], [_slot8], "o", deps=(tok,))
    gu, tok = gather([wup_s], [land8(wup_s)], [_slot8], "u", deps=(tok,))
    gd, tok = gather([wdn_s], [jnp.zeros((4, up_pad, d), BF16)], [_slot4(hs)], "d", deps=(tok,))
    fcb = _pad_blocks(ffn_conv_b, up_sh, up_pad)
    wa_b, wx_b = lru_wa[0].astype(BF16), lru_wx[0].astype(BF16)

    xf = x.reshape(t, d)
    hn = _rms_fwd(xf, ln1_w, name="ln1_fwd", deps=(tok,))
    ga, tok = pass_on(ga, hn)
    win_g, lcw_g, fcw_g = gathered(ga, tok)
    lcw = lcw_g.transpose(1, 0, 2).reshape(lru_conv_w.shape[1], wmix)
    fcw = fcw_g.transpose(1, 0, 2).reshape(ffn_conv_w.shape[1], 2 * fpad)
    proj = _mm(hn, win_g, kind="nn", out_dtype=F32, name="in_proj", tm_cap=512, tk_cap=d)
    go, tok = pass_on(go, proj)
    o_hg, o_pre, states = _hgrn_fwd(proj, lb_gamma, hg_norm_w, nb=nb, seq=seq, name="hgrn_fwd")
    (wout_g,) = gathered(go, o_pre)
    wout_f = wout_g.reshape(1, d, d)
    p_lru, h_lru = _lru_fwd(proj, lcw, lru_conv_b, wa_b, lru_ba, wx_b, lru_bx, lru_lambda,
                            nb=nb, seq=seq, name="lru_fwd", deps=(tok,))
    o_lru = _rms_fwd(p_lru, lru_norm_w, name="lru_norm_fwd")
    gu, tok = pass_on(gu, o_lru)
    mix = jnp.concatenate([o_hg, o_lru], axis=1)
    h1 = _mm(mix, wout_f, kind="nn", out_dtype=F32, name="out_proj", res=xf, tm_cap=512, tk_cap=d, deps=(tok,))
    hn2 = _rms_fwd(h1, ln2_w, name="ln2_fwd")
    (wup_g,) = gathered(gu, hn2)
    up = _mm(hn2, wup_g, kind="nn", out_dtype=F32, name="ffn_up", tm_cap=512, tn_cap=up_pad // 2, tk_cap=d)
    gd, tok = pass_on(gd, up)
    act = _ffn_act_fwd(up, fcw, fcb, nb=nb, seq=seq, name="ffn_act_fwd", deps=(tok,))
    (wdn_g,) = gathered(gd, act)
    wdn_f = wdn_g.reshape(1, fpad, d)
    h2 = _mm(act, wdn_f, kind="nn", out_dtype=F32, name="ffn_down", res=h1, tm_cap=512, tn_cap=1024,
             tk_cap=2 * up_pad)

    dh2, dh2_b, g_fnw, loss_part = _loss_head(h2, final_norm_w.reshape(1, d), loss_target.reshape(t, d),
                                              name="loss_head")
    loss = lax.psum(loss_part[0, 0], ("x", "y", "c"))

    cidx = jnp.reshape(cc, (1,)).astype(jnp.int32)

    def to_sibling(g4, r, tag, deps=()):
        land = lax.empty((4, r, g4.shape[2]), g4.dtype)
        st = _xfer_start([g4], [land], _plan_rs_sibling(r), name=f"rs_sib_start_{tag}", deps=deps)
        return (st, r, tag), st[4]

    def sibling_sum(rs, after):
        st, r, tag = rs
        (g4,), (rcv,) = _xfer_wait(st[0], st[1], st[2], st[3], _plan_rs_sibling(r), after, name=f"rs_sib_wait_{tag}")
        psum = _add_pairs(g4, rcv, cidx, name=f"rs_add_{tag}")
        land = lax.empty((3,) + psum.shape[1:], psum.dtype)
        st2 = _xfer_start([psum], [land], _plan_rs_plane(), name=f"rs_plane_start_{tag}")
        return (st2, tag), st2[4]

    dact = _mm(dh2_b, wdn_f, kind="nt", out_dtype=F32, name="ffn_down_dx", tm_cap=512, tn_cap=up_pad // 2, tk_cap=d)
    g_wdn = _mm(act, dh2_b, kind="tn", out_dtype=BF16, name="ffn_down_dw", tm_cap=1024, tn_cap=1024, tk_cap=t)
    rs_d, tok = to_sibling(g_wdn.reshape(4, up_pad, d), hs, "down")
    dup_g, dup_v, s_g, s_v = _ffn_act_bwd(dact, up, fcw, fcb, nb=nb, seq=seq, name="ffn_act_bwd", deps=(tok,))
    dup = jnp.concatenate([dup_g, dup_v], axis=1)
    pl_d, tok = sibling_sum(rs_d, dup)
    dhn2 = _mm(dup, wup_g, kind="nt", out_dtype=F32, name="ffn_up_dx", tm_cap=512, tn_cap=1024, tk_cap=up_pad,
               k_blocks=2, deps=(tok,))
    g_wup = _mm(dup, hn2, kind="tn", out_dtype=BF16, name="ffn_up_dw", tm_cap=1024, tn_cap=1024, tk_cap=t)
    rs_u, tok = to_sibling(g_wup.reshape(4, 2 * up_pad, d), up_pad, "up")
    dh1, dh1_b, g_ln2 = _rms_bwd(dhn2, h1, ln2_w, name="ln2_bwd", extra=dh2, want_bf16=True, deps=(tok,))
    dmix = _mm(dh1_b, wout_f, kind="nt", out_dtype=F32, name="out_proj_dx", tm_cap=512, tn_cap=1024, tk_cap=d)
    pl_u, tok = sibling_sum(rs_u, dmix)
    g_wout = _mm(mix, dh1_b, kind="tn", out_dtype=BF16, name="out_proj_dw", tn_cap=1024, tk_cap=t, deps=(tok,))
    rs_o, tok = to_sibling(g_wout.reshape(4, d // 4, d), d // NDEV, "out")
    dp_lru, g_lnw = _rms_bwd(dmix, p_lru, lru_norm_w, name="lru_norm_bwd", dy_cb=1, deps=(tok,))
    dxr, dyr, s_lru, g_wa, g_wx = _lru_bwd(proj, lcw, lru_conv_b, wa_b, lru_ba, wx_b, lru_bx, lru_lambda,
                                           h_lru, dp_lru, nb=nb, seq=seq, name="lru_bwd")
    pl_o, tok = sibling_sum(rs_o, dxr)
    dqr, dfr, dir_, dgr, s_hg = _hgrn_bwd(proj, lb_gamma, hg_norm_w, o_pre, states, dmix,
                                          nb=nb, seq=seq, name="hgrn_bwd", deps=(tok,))
    dproj = jnp.concatenate([dqr, dfr, dir_, dgr, dxr, dyr], axis=1)
    g_win = _mm(hn, dproj, kind="tn", out_dtype=BF16, name="in_proj_dw", out_blocks=NDEV, tm_cap=512, tn_cap=in_sh,
                tk_cap=t)
    rs_i, tok = to_sibling(g_win.reshape(4, 2 * d, in_sh), d, "in")
    dhn = _mm(dproj, win_g, kind="nt", out_dtype=F32, name="in_proj_dx", tm_cap=512, tn_cap=1024, tk_cap=in_sh,
              k_blocks=4, deps=(tok,))
    grad_x, g_ln1 = _rms_bwd(dhn, xf, ln1_w, name="ln1_bwd", extra=dh1)

    fcw_parts = jnp.concatenate([s_g[0:3], s_v[0:3]], axis=1).reshape(3, NDEV, up_pad)[:, :, :up_sh]
    fcb_part = jnp.concatenate([s_g[3:4], s_v[3:4]], axis=1).reshape(NDEV, up_pad)[:, :up_sh]
    small_parts = [g_ln1, s_hg[0:2], s_hg[2:3], s_lru[0:4], s_lru[4:5], g_wa, s_lru[5:6], g_wx, s_lru[6:7],
                   s_lru[7:8], g_lnw, g_ln2, fcw_parts, fcb_part, g_fnw]
    reduced = _all_reduce_packed(_pack(small_parts), name="allreduce_small")
    pl_i, _ = sibling_sum(rs_i, reduced)
    summed = _unpack(reduced, [p.shape for p in small_parts])
    (s_ln1, s_lbg, s_hgn, s_lcw, s_lcb, s_wa, s_ba, s_wx, s_bx, s_lam, s_lnw, s_ln2, s_fcw, s_fcb, s_fnw) = summed
    sh_lcw = lru_conv_w.shape[2]
    g_small = {
        "ln1_w": s_ln1, "lb_gamma": s_lbg, "hg_norm_w": s_hgn,
        "lru_conv_w": lax.dynamic_slice_in_dim(s_lcw, me * sh_lcw, sh_lcw, axis=1),
        "lru_conv_b": s_lcb, "lru_wa": s_wa, "lru_ba": s_ba, "lru_wx": s_wx, "lru_bx": s_bx,
        "lru_lambda": s_lam, "lru_norm_w": s_lnw, "ln2_w": s_ln2,
        "ffn_conv_w": lax.dynamic_index_in_dim(s_fcw, me, 1, keepdims=False),
        "ffn_conv_b": s_fcb, "final_norm_w": s_fnw,
    }
    w_small = {"ln1_w": ln1_w, "lb_gamma": lb_gamma, "hg_norm_w": hg_norm_w, "lru_conv_w": lru_conv_w,
               "lru_conv_b": lru_conv_b, "lru_wa": lru_wa, "lru_ba": lru_ba, "lru_wx": lru_wx, "lru_bx": lru_bx,
               "lru_lambda": lru_lambda, "lru_norm_w": lru_norm_w, "ln2_w": ln2_w, "ffn_conv_w": ffn_conv_w,
               "ffn_conv_b": ffn_conv_b, "final_norm_w": final_norm_w}
    m_small = {"ln1_w": m_ln1_w, "lb_gamma": m_lb_gamma, "hg_norm_w": m_hg_norm_w, "lru_conv_w": m_lru_conv_w,
               "lru_conv_b": m_lru_conv_b, "lru_wa": m_lru_wa, "lru_ba": m_lru_ba, "lru_wx": m_lru_wx,
               "lru_bx": m_lru_bx, "lru_lambda": m_lru_lambda, "lru_norm_w": m_lru_norm_w, "ln2_w": m_ln2_w,
               "ffn_conv_w": m_ffn_conv_w, "ffn_conv_b": m_ffn_conv_b, "final_norm_w": m_final_norm_w}
    v_small = {"ln1_w": v_ln1_w, "lb_gamma": v_lb_gamma, "hg_norm_w": v_hg_norm_w, "lru_conv_w": v_lru_conv_w,
               "lru_conv_b": v_lru_conv_b, "lru_wa": v_lru_wa, "lru_ba": v_lru_ba, "lru_wx": v_lru_wx,
               "lru_bx": v_lru_bx, "lru_lambda": v_lru_lambda, "lru_norm_w": v_lru_norm_w, "ln2_w": v_ln2_w,
               "ffn_conv_w": v_ffn_conv_w, "ffn_conv_b": v_ffn_conv_b, "final_norm_w": v_final_norm_w}
    names = list(w_small)
    shapes = [w_small[k].shape for k in names]
    g_small = {k: g_small[k].reshape(w_small[k].shape) for k in names}
    packed = _adamw_packed(_pack([w_small[k] for k in names]), _pack([g_small[k] for k in names]),
                           _pack([m_small[k] for k in names]), _pack([v_small[k] for k in names]),
                           name="adamw_small")
    d_small, nm_small, nv_small = ({k: a for k, a in zip(names, _unpack(pk, shapes))} for pk in packed)

    def finish(pl_x, after, trim=lambda a: a):
        st, tag = pl_x
        (psum,), (rb,) = _xfer_wait(st[0], st[1], st[2], st[3], _plan_rs_plane(), after, name=f"rs_plane_wait_{tag}")
        return trim(lax.dynamic_index_in_dim(psum, plane, 0, keepdims=False)), trim(rb)

    upd_dn = _adamw_shard(ffn_w_down, m_ffn_w_down, v_ffn_w_down, *finish(pl_d, packed[0]), name="adamw_w_down")
    tr3 = lambda a: a.transpose(0, 2, 1)
    upd_up_t = _adamw_shard(tr3(ffn_w_up), tr3(m_ffn_w_up), tr3(v_ffn_w_up), *finish(pl_u, upd_dn[0]),
                            name="adamw_w_up")
    upd_up = [tr3(a) for a in upd_up_t]
    upd_out = _adamw_shard(w_out, m_w_out, v_w_out, *finish(pl_o, upd_up_t[0]), name="adamw_w_out")
    upd_in = _adamw_shard(w_in, m_w_in, v_w_in, *finish(pl_i, upd_out[0]), name="adamw_w_in")

    grads, deltas, new_m, new_v = dict(g_small), dict(d_small), dict(nm_small), dict(nv_small)
    for k, upd in (("w_in", upd_in), ("w_out", upd_out), ("ffn_w_up", upd_up), ("ffn_w_down", upd_dn)):
        grads[k], deltas[k], new_m[k], new_v[k] = upd
    order = ["ln1_w", "w_in", "lb_gamma", "hg_norm_w", "lru_conv_w", "lru_conv_b", "lru_wa", "lru_ba", "lru_wx",
             "lru_bx", "lru_lambda", "lru_norm_w", "w_out", "ln2_w", "ffn_w_up", "ffn_conv_w", "ffn_conv_b",
             "ffn_w_down", "final_norm_w"]
    return (loss, grad_x.reshape(nb, seq, d), *[grads[k] for k in order], *[deltas[k] for k in order],
            *[new_m[k] for k in order], *[new_v[k] for k in order])
```

```python
import math

import jax
import jax.numpy as jnp
from jax import lax
from jax.experimental import pallas as pl
from jax.experimental.pallas import tpu as pltpu

F32, BF16 = jnp.float32, jnp.bfloat16
EPS = 1e-6
HEAD = 128
CHUNK = 64
SUB = 16
NSUB = CHUNK // SUB
LRU_C = 8.0
LANE = 128
NDEV = 8
ADAM_LR, ADAM_B1, ADAM_B2, ADAM_EPS, ADAM_WD, ADAM_STEP = 0.001, 0.9, 0.999, 1e-08, 0.01, 10
MESH = pl.DeviceIdType.MESH
ANY = pl.BlockSpec(memory_space=pl.ANY)
VMEM_LIMIT = 56 * 1024 * 1024


def _cp(**kw):
    return pltpu.CompilerParams(vmem_limit_bytes=VMEM_LIMIT, **kw)


def _tile(n, cap, mult=LANE):
    best = None
    for t in range(mult, min(n, cap) + 1, mult):
        if n % t == 0:
            best = t
    return best if best is not None else n


def _row_tile(r, cdim, budget=262144):
    return _tile(r, max(16, budget // cdim), 16)


def _sigmoid(x):
    return jax.nn.sigmoid(x)


def _dsilu(x, s):
    return s * (1.0 + x * (1.0 - s))


def _iota_rows(n, w=LANE):
    return lax.broadcasted_iota(jnp.int32, (n, w), 0)


def _lane_groups(n):
    return 2 if n % 2 == 0 else 1


def _group_views(refs, g, kinds):
    assert len(refs) == len(kinds)
    cols = pl.ds(g * HEAD, HEAD)
    return [r.at[:, cols] if kind == "l" else r.at[g] for r, kind in zip(refs, kinds)]


def _shift_down(prev8, xt, k):
    cat = jnp.concatenate([prev8, xt], axis=0)
    return pltpu.roll(cat, k, 0)[8:]


def _shift_up(xt, next8, k):
    cat = jnp.concatenate([xt, next8], axis=0)
    n = cat.shape[0]
    return pltpu.roll(cat, n - k, 0)[: xt.shape[0]]


def _scan_fwd(a, u):
    n = a.shape[0]
    row = _iota_rows(n, a.shape[1])
    k = 1
    while k < n:
        keep = row >= k
        a_s = jnp.where(keep, pltpu.roll(a, k, 0), 1.0)
        u_s = jnp.where(keep, pltpu.roll(u, k, 0), 0.0)
        u = a * u_s + u
        a = a * a_s
        k *= 2
    return a, u


def _scan_bwd(a, u):
    n = a.shape[0]
    row = _iota_rows(n, a.shape[1])
    k = 1
    while k < n:
        keep = row < n - k
        a_s = jnp.where(keep, pltpu.roll(a, n - k, 0), 1.0)
        u_s = jnp.where(keep, pltpu.roll(u, n - k, 0), 0.0)
        u = a * u_s + u
        a = a * a_s
        k *= 2
    return a, u


def _cumsum_fwd(u):
    n = u.shape[0]
    row = _iota_rows(n, u.shape[1])
    k = 1
    while k < n:
        u = u + jnp.where(row >= k, pltpu.roll(u, k, 0), 0.0)
        k *= 2
    return u


def _cumsum_bwd(u):
    n = u.shape[0]
    row = _iota_rows(n, u.shape[1])
    k = 1
    while k < n:
        u = u + jnp.where(row < n - k, pltpu.roll(u, n - k, 0), 0.0)
        k *= 2
    return u


def _dot(a, b, dims):
    return lax.dot_general(a.astype(BF16), b.astype(BF16), (dims, ((), ())), preferred_element_type=F32)


NN = ((1,), (0,))
NT = ((1,), (1,))
TN = ((0,), (0,))


def _pcall(body, *, n_in, in_specs, args, deps=(), **kw):
    nd = len(deps)
    if nd:
        inner = body

        def body(*refs):
            return inner(*refs[:n_in], *refs[n_in + nd:])

        in_specs = list(in_specs) + [ANY] * nd
        args = list(args) + list(deps)
    return pl.pallas_call(body, in_specs=in_specs, **kw)(*args)


def _mm(a, b3, *, kind, out_dtype, name, res=None, tm_cap=1024, tn_cap=1536, tk_cap=1024, out_blocks=1, k_blocks=1,
        deps=()):
    if kind == "nn":
        m, kdim = a.shape
        nb, _, nsh = b3.shape
        n = nb * nsh
        tm, tn, tk = _tile(m, tm_cap), _tile(nsh, tn_cap), _tile(kdim, tk_cap)
        per = nsh // tn
        a_blk, a_ix = (tm, tk), lambda i, j, k: (i, k)
        b_blk, b_ix = (None, tk, tn), lambda i, j, k: (j // per, k, j % per)
        dims = NN
    elif kind == "nt":
        m, kdim = a.shape
        nb, n, ksh = b3.shape
        tm, tn, tk = _tile(m, tm_cap), _tile(n, tn_cap), _tile(ksh, tk_cap)
        per = ksh // tk
        a_blk, a_ix = (tm, tk), lambda i, j, k: (i, k)
        b_blk, b_ix = (None, tn, tk), lambda i, j, k: (k // per, j, k % per)
        if k_blocks > 1:
            assert tk == ksh and nb % k_blocks == 0
            tk = k_blocks * ksh
            a_blk, b_blk, b_ix = (tm, tk), (k_blocks, tn, ksh), lambda i, j, k: (k, j, 0)
        dims = NT
    else:
        kdim, m = a.shape
        n = b3.shape[1]
        nsh = n // out_blocks
        tm, tn, tk = _tile(m, tm_cap), _tile(nsh, tn_cap), _tile(kdim, tk_cap)
        per = nsh // tn
        a_blk, a_ix = (tk, tm), lambda i, j, k: (k, i)
        b_blk, b_ix = (tk, tn), lambda i, j, k: (k, j)
        dims = TN
    nk = kdim // tk
    j_outer = nk == 1
    grid = (n // tn, m // tm, nk) if j_outer else (m // tm, n // tn, nk)
    at = (lambda f: lambda g0, g1, k: f(g1, g0, k)) if j_outer else (lambda f: f)
    a_spec, b_spec = pl.BlockSpec(a_blk, at(a_ix)), pl.BlockSpec(b_blk, at(b_ix))

    def body(*refs):
        a_ref, b_ref = refs[:2]
        r_ref = refs[2] if res is not None else None
        o_ref = refs[3] if res is not None else refs[2]
        if k_blocks > 1:
            w = b_ref.shape[2]
            part = sum(lax.dot_general(a_ref[:, s * w:(s + 1) * w], b_ref[s], (dims, ((), ())),
                                       preferred_element_type=F32) for s in range(k_blocks))
        else:
            part = lax.dot_general(a_ref[...], b_ref[...], (dims, ((), ())), preferred_element_type=F32)
        if nk == 1:
            o_ref[...] = (part if res is None else part + r_ref[...]).astype(o_ref.dtype)
            return
        acc = refs[-1]
        k = pl.program_id(2)

        @pl.when(k == 0)
        def _():
            acc[...] = part

        @pl.when(k > 0)
        def _():
            acc[...] += part

        @pl.when(k == nk - 1)
        def _():
            r = acc[...]
            if res is not None:
                r = r + r_ref[...]
            o_ref[...] = r.astype(o_ref.dtype)

    in_specs = [a_spec, b_spec]
    args = [a, b3]
    if res is not None:
        in_specs.append(pl.BlockSpec((tm, tn), at(lambda i, j, k: (i, j))))
        args.append(res)
    if kind == "tn":
        out_shape = jax.ShapeDtypeStruct((out_blocks, m, nsh), out_dtype)
        out_spec = pl.BlockSpec((None, tm, tn), at(lambda i, j, k: (j // per, i, j % per)))
    else:
        out_shape = jax.ShapeDtypeStruct((m, n), out_dtype)
        out_spec = pl.BlockSpec((tm, tn), at(lambda i, j, k: (i, j)))
    return _pcall(
        body, n_in=len(args), in_specs=in_specs, args=args, deps=deps,
        name=name, grid=grid, out_specs=out_spec, out_shape=out_shape,
        scratch_shapes=[pltpu.VMEM((tm, tn), F32)] if nk > 1 else [],
        compiler_params=_cp(dimension_semantics=("parallel", "parallel", "arbitrary")))


def _rms_fwd(x, w, *, name, tm=256, deps=()):
    t, d = x.shape
    tm = _tile(t, tm, 16)

    def body(x_ref, w_ref, o_ref):
        xv = x_ref[...]
        r = lax.rsqrt(jnp.mean(xv * xv, axis=-1, keepdims=True) + EPS)
        o_ref[...] = ((xv * r) * w_ref[...]).astype(o_ref.dtype)

    return _pcall(
        body, n_in=2, args=[x, w], deps=deps, name=name, grid=(t // tm,),
        in_specs=[pl.BlockSpec((tm, d), lambda i: (i, 0)), pl.BlockSpec((1, d), lambda i: (0, 0))],
        out_specs=pl.BlockSpec((tm, d), lambda i: (i, 0)),
        out_shape=jax.ShapeDtypeStruct((t, d), BF16), compiler_params=_cp())


def _rms_bwd(dy, x, w, *, name, extra=None, dy_cb=0, want_bf16=False, tm=256, deps=()):
    t, d = x.shape
    tm = _tile(t, tm, 16)

    def body(*refs):
        refs = list(refs)
        dy_ref, x_ref, w_ref = refs[:3]
        e_ref = refs[3] if extra is not None else None
        outs = refs[4:] if extra is not None else refs[3:]
        dx_ref = outs[0]
        dxb_ref = outs[1] if want_bf16 else None
        dw_ref = outs[-1]
        i = pl.program_id(0)
        xv = x_ref[...]
        r = lax.rsqrt(jnp.mean(xv * xv, axis=-1, keepdims=True) + EPS)
        nh = xv * r
        dyv = dy_ref[...]
        dn = dyv * w_ref[...]
        dx = r * (dn - nh * jnp.mean(dn * nh, axis=-1, keepdims=True))
        if extra is not None:
            dx = dx + e_ref[...]
        dx_ref[...] = dx
        if want_bf16:
            dxb_ref[...] = dx.astype(BF16)
        part = jnp.sum(dyv * nh, axis=0, keepdims=True)

        @pl.when(i == 0)
        def _():
            dw_ref[...] = part

        @pl.when(i > 0)
        def _():
            dw_ref[...] += part

    row = pl.BlockSpec((tm, d), lambda i: (i, 0))
    in_specs = [pl.BlockSpec((tm, d), lambda i: (i, dy_cb)), row, pl.BlockSpec((1, d), lambda i: (0, 0))]
    args = [dy, x, w]
    if extra is not None:
        in_specs.append(row)
        args.append(extra)
    out_shape = [jax.ShapeDtypeStruct((t, d), F32)]
    out_specs = [row]
    if want_bf16:
        out_shape.append(jax.ShapeDtypeStruct((t, d), BF16))
        out_specs.append(row)
    out_shape.append(jax.ShapeDtypeStruct((1, d), F32))
    out_specs.append(pl.BlockSpec((1, d), lambda i: (0, 0)))
    return _pcall(
        body, n_in=len(args), in_specs=in_specs, args=args, deps=deps,
        name=name, grid=(t // tm,), out_specs=out_specs, out_shape=out_shape,
        compiler_params=_cp(dimension_semantics=("arbitrary",)))


def _loss_head(h, w, tgt, *, name, tm=256):
    t, d = h.shape
    tm = _tile(t, tm, 16)

    def body(h_ref, w_ref, t_ref, dh_ref, dhb_ref, dw_ref, loss_ref):
        i = pl.program_id(0)
        xv = h_ref[...]
        wv = w_ref[...]
        r = lax.rsqrt(jnp.mean(xv * xv, axis=-1, keepdims=True) + EPS)
        nh = xv * r
        e = nh * wv - t_ref[...]
        part_loss = jnp.full((1, LANE), 0.5 * jnp.sum(jnp.mean(e * e, axis=-1, keepdims=True)), F32)
        dyv = e * (1.0 / d)
        dn = dyv * wv
        dx = r * (dn - nh * jnp.mean(dn * nh, axis=-1, keepdims=True))
        dh_ref[...] = dx
        dhb_ref[...] = dx.astype(BF16)
        part = jnp.sum(dyv * nh, axis=0, keepdims=True)

        @pl.when(i == 0)
        def _():
            dw_ref[...] = part
            loss_ref[...] = part_loss

        @pl.when(i > 0)
        def _():
            dw_ref[...] += part
            loss_ref[...] += part_loss

    row = pl.BlockSpec((tm, d), lambda i: (i, 0))
    vec = pl.BlockSpec((1, d), lambda i: (0, 0))
    return pl.pallas_call(
        body, name=name, grid=(t // tm,), in_specs=[row, vec, row],
        out_specs=[row, row, vec, pl.BlockSpec((1, LANE), lambda i: (0, 0))],
        out_shape=[jax.ShapeDtypeStruct((t, d), F32), jax.ShapeDtypeStruct((t, d), BF16),
                   jax.ShapeDtypeStruct((1, d), F32), jax.ShapeDtypeStruct((1, LANE), F32)],
        compiler_params=_cp(dimension_semantics=("arbitrary",)),
    )(h, w, tgt)


def _lower_bound(lbg_ref):
    g0, g1 = lbg_ref[0:1, :], lbg_ref[1:2, :]
    m = jnp.maximum(g0, g1)
    e0, e1 = jnp.exp(g0 - m), jnp.exp(g1 - m)
    return e0 / (e0 + e1)


def _seg_bounds():
    offs, o = {}, 0
    for i in range(1, NSUB):
        offs[i] = (o, o + SUB * i)
        o += SUB * i
    return offs, o


def _pad_rows(x, n):
    if x.shape[0] == n:
        return x
    return jnp.concatenate([x, jnp.zeros((n - x.shape[0], x.shape[1]), x.dtype)], axis=0)


def _offdiag_mask():
    offs, total = _seg_bounds()
    padded = -(-total // LANE) * LANE
    rsub = lax.broadcasted_iota(jnp.int32, (CHUNK, padded), 0) // SUB
    col = lax.broadcasted_iota(jnp.int32, (CHUNK, padded), 1)
    cseg = jnp.zeros((CHUNK, padded), jnp.int32)
    for i in range(1, NSUB):
        cseg = cseg + (col >= offs[i][0]).astype(jnp.int32)
    return (rsub == cseg) & (col < total)


def _offdiag_setup(q, k, b, v, b_c, mask):
    offs, total = _seg_bounds()
    padded = -(-total // LANE) * LANE
    eq_parts = [jnp.zeros((SUB, HEAD), F32)]
    ek_parts, k_parts, v_parts = [], [], []
    for i in range(1, NSUB):
        r_i = b_c[SUB * i - 1:SUB * i, :]
        eq_parts.append(jnp.exp(b[SUB * i:SUB * (i + 1)] - r_i))
        ek_parts.append(jnp.exp(r_i - b[0:SUB * i]))
        k_parts.append(k[0:SUB * i])
        v_parts.append(v[0:SUB * i])
    eq = jnp.concatenate(eq_parts, axis=0)
    ek = _pad_rows(jnp.concatenate(ek_parts, axis=0), padded)
    kt = _pad_rows(jnp.concatenate(k_parts, axis=0), padded) * ek
    vs = _pad_rows(jnp.concatenate(v_parts, axis=0), padded)
    qt = q * eq
    a = jnp.where(mask, _dot(qt, kt, NT), 0.0)
    return offs, eq, ek, kt, vs, qt, a


def _hgrn_fwd(proj, lbg, nw, *, nb, seq, name):
    t = proj.shape[0]
    w = lbg.shape[1]
    nh = w // HEAD
    nc = seq // CHUNK
    ng = _lane_groups(nh)

    def head(mask, q_ref, f_ref, i_ref, g_ref, lbg_ref, nw_ref, ohg_ref, opre_ref, st_ref, k_c, b_c, v_c, st):
        lb = _lower_bound(lbg_ref)
        nwv = nw_ref[...]
        st[...] = jnp.zeros_like(st)

        def chunk(c, carry):
            rows = pl.ds(pl.multiple_of(c * CHUNK, CHUNK), CHUNK)
            qr = q_ref[rows, :]
            q = qr * _sigmoid(qr)
            f = lb + (1.0 - lb) * _sigmoid(f_ref[rows, :])
            k = 1.0 - f
            b = _cumsum_fwd(jnp.log(f))
            v = i_ref[rows, :]
            k_c[...] = k
            b_c[...] = b
            v_c[...] = v
            s_t = st[...]
            st_ref[c] = s_t
            o = _dot(q * jnp.exp(b), s_t, NT)
            _, _, _, _, vs, _, a = _offdiag_setup(q, k, b, v, b_c, mask)
            o = o + _dot(a, vs, NN)
            diag = []
            for i in range(NSUB):
                accs = [jnp.zeros((8, HEAD), F32) for _ in range(SUB // 8)]
                for j in range(SUB):
                    r = SUB * i + j
                    bj, kj, vj = b_c[r:r + 1, :], k_c[r:r + 1, :], v_c[r:r + 1, :]
                    for p in range(j // 8, SUB // 8):
                        lo = SUB * i + 8 * p
                        d = jnp.exp(b[lo:lo + 8] - bj)
                        if 8 * p < j:
                            d = jnp.where(_iota_rows(8) + 8 * p >= j, d, 0.0)
                        s = jnp.sum(q[lo:lo + 8] * d * kj, axis=-1, keepdims=True)
                        accs[p] = accs[p] + s * vj
                diag.extend(accs)
            o = o + jnp.concatenate(diag, axis=0)
            bl = b_c[CHUNK - 1:CHUNK, :]
            kb = k * jnp.exp(bl - b)
            st[...] = s_t * jnp.exp(bl) + _dot(v, kb, TN)
            opre_ref[rows, :] = o
            rn = lax.rsqrt(jnp.mean(o * o, axis=-1, keepdims=True) + EPS)
            gr = g_ref[rows, :]
            ohg_ref[rows, :] = (((o * rn) * nwv) * (gr * _sigmoid(gr))).astype(BF16)
            return carry

        return chunk

    def body(*refs):
        mask = _offdiag_mask()
        chunks = [head(mask, *_group_views(refs, g, "llllllll" + "ggggg")) for g in range(ng)]

        def step(c, carry):
            for chunk in chunks:
                chunk(c, carry)
            return carry

        lax.fori_loop(0, nc, step, 0)

    gw = ng * HEAD

    def col(off):
        return pl.BlockSpec((seq, gw), lambda h, b: (b, off * (nh // ng) + h))

    vec = lambda r: pl.BlockSpec((r, gw), lambda h, b: (0, h))
    out_blk = pl.BlockSpec((seq, gw), lambda h, b: (b, h))
    return pl.pallas_call(
        body, name=name, grid=(nh // ng, nb),
        in_specs=[col(0), col(1), col(2), col(3), vec(2), vec(1)],
        out_specs=[out_blk, out_blk, pl.BlockSpec((None, ng, nc, HEAD, HEAD), lambda h, b: (b, h, 0, 0, 0))],
        out_shape=[jax.ShapeDtypeStruct((t, w), BF16), jax.ShapeDtypeStruct((t, w), F32),
                   jax.ShapeDtypeStruct((nb, nh, nc, HEAD, HEAD), F32)],
        scratch_shapes=[pltpu.VMEM((ng, CHUNK, HEAD), F32)] * 3 + [pltpu.VMEM((ng, HEAD, HEAD), F32)],
        compiler_params=_cp(dimension_semantics=("parallel", "parallel")),
    )(proj, proj, proj, proj, lbg, nw)


def _hgrn_bwd(proj, lbg, nw, opre, states, dmix, *, nb, seq, name, deps=()):
    t = proj.shape[0]
    w = lbg.shape[1]
    nh = w // HEAD
    nc = seq // CHUNK
    ng = _lane_groups(nh)

    def head(mask, q_ref, f_ref, i_ref, g_ref, lbg_ref, nw_ref, opre_ref, st_ref, dm_ref,
             dq_ref, df_ref, di_ref, dg_ref, small_ref, k_c, b_c, v_c, dst, dlb_s, dwn_s):
        bi = pl.program_id(1)
        lb = _lower_bound(lbg_ref)
        nwv = nw_ref[...]
        dst[...] = jnp.zeros_like(dst)

        @pl.when(bi == 0)
        def _():
            dlb_s[...] = jnp.zeros_like(dlb_s)
            dwn_s[...] = jnp.zeros_like(dwn_s)

        def chunk(it, carry):
            c = nc - 1 - it
            rows = pl.ds(pl.multiple_of(c * CHUNK, CHUNK), CHUNK)
            qr = q_ref[rows, :]
            sq = _sigmoid(qr)
            q = qr * sq
            sg = _sigmoid(f_ref[rows, :])
            f = lb + (1.0 - lb) * sg
            k = 1.0 - f
            b = _cumsum_fwd(jnp.log(f))
            v = i_ref[rows, :]
            k_c[...] = k
            b_c[...] = b
            v_c[...] = v
            o = opre_ref[rows, :]
            rn = lax.rsqrt(jnp.mean(o * o, axis=-1, keepdims=True) + EPS)
            nhat = o * rn
            dm = dm_ref[rows, :]
            gr = g_ref[rows, :]
            sgr = _sigmoid(gr)
            dnw = dm * (gr * sgr)
            dg_ref[rows, :] = (dm * (nhat * nwv) * _dsilu(gr, sgr)).astype(BF16)
            dwn_s[...] += jnp.sum(dnw * nhat, axis=0, keepdims=True)
            dn = dnw * nwv
            do = rn * (dn - nhat * jnp.mean(dn * nhat, axis=-1, keepdims=True))
            s_t = st_ref[c]
            ds = dst[...]
            eb = jnp.exp(b)
            qb = q * eb
            bl = b_c[CHUNK - 1:CHUNK, :]
            ebl = jnp.exp(bl)
            kdec = jnp.exp(bl - b)
            kb = k * kdec
            dqb = _dot(do, s_t, NN)
            dkb = _dot(v, ds, NN)
            dv = _dot(kb, ds, NT)
            d_ebl = jnp.sum(ds * s_t, axis=0, keepdims=True)
            dst[...] = ds * ebl + _dot(do, qb, TN)
            dq = dqb * eb
            dk = dkb * kdec
            t_kb = dkb * kb
            db = dqb * qb - t_kb
            db_last = jnp.sum(t_kb, axis=0, keepdims=True) + d_ebl * ebl
            offs, eq, ek, kt, vs, qt, a = _offdiag_setup(q, k, b, v, b_c, mask)
            da = jnp.where(mask, _dot(do, vs, NT), 0.0)
            dvs = _dot(a, do, TN)
            dqt = _dot(da, kt, NN)
            dkt = _dot(da, qt, TN)
            dq = dq + dqt * eq
            db = db + dqt * qt
            gk, gb = dkt * ek, dkt * kt
            zero8 = jnp.zeros((8, HEAD), F32)
            off_k, off_v, off_b = ([zero8] * (CHUNK // 8) for _ in range(3))
            for i in range(1, NSUB):
                lo, hi = offs[i]
                for p in range((hi - lo) // 8):
                    rows8 = slice(lo + 8 * p, lo + 8 * p + 8)
                    off_k[p] = off_k[p] + gk[rows8]
                    off_b[p] = off_b[p] - gb[rows8]
                    off_v[p] = off_v[p] + dvs[rows8]
            npv = SUB // 8
            dq_d, dk_d, dv_d, db_d = [], [], [], []
            for i in range(NSUB):
                aq = [zero8 for _ in range(npv)]
                ak = [zero8 for _ in range(npv)]
                av = [off_v[i * npv + p] for p in range(npv)]
                for j in range(SUB):
                    r = SUB * i + j
                    bj, kj, vj = b_c[r:r + 1, :], k_c[r:r + 1, :], v_c[r:r + 1, :]
                    pm_sum, ad_sum = None, None
                    for p in range(j // 8, npv):
                        lo = SUB * i + 8 * p
                        d = jnp.exp(b[lo:lo + 8] - bj)
                        if 8 * p < j:
                            d = jnp.where(_iota_rows(8) + 8 * p >= j, d, 0.0)
                        qd = q[lo:lo + 8] * d
                        dop = do[lo:lo + 8]
                        a_j = jnp.sum(qd * kj, axis=-1, keepdims=True)
                        da_j = jnp.sum(dop * vj, axis=-1, keepdims=True)
                        aq[p] = aq[p] + (da_j * d) * kj
                        pm, ad = da_j * qd, a_j * dop
                        pm_sum = pm if pm_sum is None else pm_sum + pm
                        ad_sum = ad if ad_sum is None else ad_sum + ad
                    pj = j // 8
                    here = _iota_rows(8) == (j - 8 * pj)
                    ak[pj] = ak[pj] + jnp.where(here, jnp.sum(pm_sum, axis=0, keepdims=True), 0.0)
                    av[pj] = av[pj] + jnp.where(here, jnp.sum(ad_sum, axis=0, keepdims=True), 0.0)
                for p in range(npv):
                    lo = SUB * i + 8 * p
                    db_d.append(off_b[i * npv + p] + q[lo:lo + 8] * aq[p] - k[lo:lo + 8] * ak[p])
                    dk_d.append(off_k[i * npv + p] + ak[p])
                dq_d.extend(aq)
                dv_d.extend(av)
            dq = dq + jnp.concatenate(dq_d, axis=0)
            dk = dk + jnp.concatenate(dk_d, axis=0)
            dv = dv + jnp.concatenate(dv_d, axis=0)
            db = db + jnp.concatenate(db_d, axis=0)
            db = db + jnp.where(_iota_rows(CHUNK) == CHUNK - 1, db_last, 0.0)
            dgl = _cumsum_bwd(db)
            dfv = dgl / f - dk
            dlb_s[...] += jnp.sum(dfv * (1.0 - sg), axis=0, keepdims=True)
            df_ref[rows, :] = (dfv * (1.0 - lb) * (sg * (1.0 - sg))).astype(BF16)
            dq_ref[rows, :] = (dq * _dsilu(qr, sq)).astype(BF16)
            di_ref[rows, :] = dv.astype(BF16)
            return carry

        def finish():
            @pl.when(bi == nb - 1)
            def _():
                dgam = dlb_s[...] * lb * (1.0 - lb)
                small_ref[...] = jnp.zeros_like(small_ref)
                small_ref[0:1, :] = dgam
                small_ref[1:2, :] = -dgam
                small_ref[2:3, :] = dwn_s[...]

        return chunk, finish

    def body(*refs):
        mask = _offdiag_mask()
        heads = [head(mask, *_group_views(refs, g, "lllllllgl" + "lllll" + "ggggll")) for g in range(ng)]

        def step(it, carry):
            for chunk, _ in heads:
                chunk(it, carry)
            return carry

        lax.fori_loop(0, nc, step, 0)
        for _, finish in heads:
            finish()

    gw = ng * HEAD

    def col(off):
        return pl.BlockSpec((seq, gw), lambda h, b: (b, off * (nh // ng) + h))

    vec = lambda r: pl.BlockSpec((r, gw), lambda h, b: (0, h))
    blk = pl.BlockSpec((seq, gw), lambda h, b: (b, h))
    dshape = jax.ShapeDtypeStruct((t, w), BF16)
    return _pcall(
        body, n_in=9, args=[proj, proj, proj, proj, lbg, nw, opre, states, dmix], deps=deps,
        name=name, grid=(nh // ng, nb),
        in_specs=[col(0), col(1), col(2), col(3), vec(2), vec(1), blk,
                  pl.BlockSpec((None, ng, nc, HEAD, HEAD), lambda h, b: (b, h, 0, 0, 0)), blk],
        out_specs=[blk, blk, blk, blk, vec(8)],
        out_shape=[dshape, dshape, dshape, dshape, jax.ShapeDtypeStruct((8, w), F32)],
        scratch_shapes=[pltpu.VMEM((ng, CHUNK, HEAD), F32)] * 3 + [pltpu.VMEM((ng, HEAD, HEAD), F32)]
        + [pltpu.VMEM((1, gw), F32)] * 2,
        compiler_params=_cp(dimension_semantics=("parallel", "arbitrary")))


def _expm1(x):
    poly = x * (1.0 + x * (0.5 + x * (1.0 / 6 + x * (1.0 / 24 + x * (1.0 / 120 + x * (1.0 / 720))))))
    return jnp.where(jnp.abs(x) < 0.25, poly, jnp.exp(x) - 1.0)


def _softplus_neg(lam):
    x = -lam
    e = jnp.exp(-jnp.abs(x))
    u = 1.0 + e
    l1p = jnp.where(u == 1.0, e, jnp.log(u) * (e / jnp.where(u == 1.0, 1.0, u - 1.0)))
    return jnp.maximum(x, 0.0) + l1p


_GELU_C = math.sqrt(2.0 / math.pi)


def _gelu(y):
    return 0.5 * y * (1.0 + jnp.tanh(_GELU_C * (y + 0.044715 * (y * y * y))))


def _dgelu(y):
    th = jnp.tanh(_GELU_C * (y + 0.044715 * (y * y * y)))
    return 0.5 * (1.0 + th) + 0.5 * y * (1.0 - th * th) * (_GELU_C * (1.0 + 3 * 0.044715 * (y * y)))


def _lru_gates(xr, prev8, cw_ref, cb, wa_ref, ba, wx_ref, bx, sp, first):
    x3, x2, x1 = _shift_down(prev8, xr, 3), _shift_down(prev8, xr, 2), _shift_down(prev8, xr, 1)
    xb = cb + x3 * cw_ref[0:1, :]
    xb = xb + x2 * cw_ref[1:2, :]
    xb = xb + x1 * cw_ref[2:3, :]
    xb = xb + xr * cw_ref[3:4, :]
    r = _sigmoid(_dot(xb, wa_ref[...], NN) + ba)
    ig = _sigmoid(_dot(xb, wx_ref[...], NN) + bx)
    la = (-LRU_C * r) * sp
    a = jnp.exp(la)
    start = jnp.logical_and(first, _iota_rows(xr.shape[0]) == 0)
    mult = jnp.where(start, 1.0, jnp.sqrt(-_expm1(2.0 * la)))
    return (x3, x2, x1), xb, r, ig, a, mult, start


def _lru_fwd(proj, cw, cb, wa, ba, wx, bx, lam, *, nb, seq, name, deps=()):
    t = proj.shape[0]
    w = cb.shape[1]
    nblk = w // HEAD
    nc = seq // CHUNK
    ng = _lane_groups(nblk)

    def block(x_ref, y_ref, cw_ref, cb_ref, wa_ref, ba_ref, wx_ref, bx_ref, lam_ref, p_ref, h_ref):
        sp = _softplus_neg(lam_ref[...])
        cb_v, ba_v, bx_v = cb_ref[...], ba_ref[...], bx_ref[...]

        def tile(c, carry):
            hc, prev8 = carry
            base = pl.multiple_of(c * CHUNK, CHUNK)
            rows = pl.ds(base, CHUNK)
            xr = x_ref[rows, :]
            _, xb, _, ig, a, mult, _ = _lru_gates(xr, prev8, cw_ref, cb_v, wa_ref, ba_v, wx_ref, bx_v, sp, c == 0)
            ap, up = _scan_fwd(a, xb * ig * mult)
            h = up + ap * hc
            h_ref[rows, :] = h
            p_ref[rows, :] = h * _gelu(y_ref[rows, :])
            h_last = jnp.sum(jnp.where(_iota_rows(8) == 7, h[CHUNK - 8:], 0.0), axis=0, keepdims=True)
            return h_last, xr[CHUNK - 8:]

        return tile

    def body(*refs):
        tiles = [block(*_group_views(refs, g, "llllglgll" + "ll")) for g in range(ng)]

        def step(c, carries):
            return tuple(tile(c, carry) for tile, carry in zip(tiles, carries))

        lax.fori_loop(0, nc, step, ((jnp.zeros((1, HEAD), F32), jnp.zeros((8, HEAD), F32)),) * ng)

    gw = ng * HEAD

    def col(off):
        return pl.BlockSpec((seq, gw), lambda n, b: (b, off * (nblk // ng) + n))

    vec = lambda r: pl.BlockSpec((r, gw), lambda n, b: (0, n))
    mat = pl.BlockSpec((ng, HEAD, HEAD), lambda n, b: (n, 0, 0))
    blk = pl.BlockSpec((seq, gw), lambda n, b: (b, n))
    return _pcall(
        body, n_in=9, args=[proj, proj, cw, cb, wa, ba, wx, bx, lam], deps=deps, name=name, grid=(nblk // ng, nb),
        in_specs=[col(4), col(5), vec(4), vec(1), mat, vec(1), mat, vec(1), vec(1)],
        out_specs=[blk, blk],
        out_shape=[jax.ShapeDtypeStruct((t, w), F32), jax.ShapeDtypeStruct((t, w), F32)],
        compiler_params=_cp(dimension_semantics=("parallel", "parallel")))


def _lru_bwd(proj, cw, cb, wa, ba, wx, bx, lam, hsv, dp, *, nb, seq, name):
    t = proj.shape[0]
    w = cb.shape[1]
    nblk = w // HEAD
    nc = seq // CHUNK
    ng = _lane_groups(nblk)

    def block(x_ref, y_ref, cw_ref, cb_ref, wa_ref, ba_ref, wx_ref, bx_ref, lam_ref, h_ref, dp_ref,
              dx_ref, dy_ref, small_ref, dwa_ref, dwx_ref, a_c, g_c, acc, dwa_s, dwx_s):
        bi = pl.program_id(1)
        lamv = lam_ref[...]
        sp = _softplus_neg(lamv)
        cb_v, ba_v, bx_v = cb_ref[...], ba_ref[...], bx_ref[...]

        @pl.when(bi == 0)
        def _():
            acc[...] = jnp.zeros_like(acc)
            dwa_s[...] = jnp.zeros_like(dwa_s)
            dwx_s[...] = jnp.zeros_like(dwx_s)

        def tile(it, carry):
            g_next, a_next, dxb_next8 = carry
            c = nc - 1 - it
            base = pl.multiple_of(c * CHUNK, CHUNK)
            rows = pl.ds(base, CHUNK)
            before = pl.ds(pl.multiple_of(jnp.maximum(base - 8, 0), 8), 8)
            inner = c > 0
            xr = x_ref[rows, :]
            prev8 = jnp.where(inner, x_ref[before, :], 0.0)
            (x3, x2, x1), xb, r, ig, a, mult, start = _lru_gates(
                xr, prev8, cw_ref, cb_v, wa_ref, ba_v, wx_ref, bx_v, sp, c == 0)
            h = h_ref[rows, :]
            h_m1 = _shift_down(jnp.where(inner, h_ref[before, :], 0.0), h, 1)
            yv = y_ref[rows, :]
            dpv = dp_ref[rows, :]
            dy_ref[rows, :] = (dpv * h * _dgelu(yv)).astype(BF16)
            dh = dpv * _gelu(yv)
            a_up = _shift_up(a, jnp.broadcast_to(a_next, (8, HEAD)), 1)
            ap, gp = _scan_bwd(a_up, dh)
            g = gp + ap * g_next
            a_c[...] = a
            g_c[...] = g
            da = g * h_m1
            gx = g * xb
            dxb = g * ig * mult
            dig = gx * mult
            dmult = jnp.where(start, 0.0, gx * ig)
            dla = da * a - dmult * (a * a) / mult
            dzr = (dla * (-LRU_C * sp)) * (r * (1.0 - r))
            dzi = dig * (ig * (1.0 - ig))
            dxb = dxb + _dot(dzr, wa_ref[...], NT) + _dot(dzi, wx_ref[...], NT)
            dwa_s[...] += _dot(xb, dzr, TN)
            dwx_s[...] += _dot(xb, dzi, TN)
            d1, d2, d3 = (_shift_up(dxb, dxb_next8, s) for s in (1, 2, 3))
            dx = dxb * cw_ref[3:4, :] + d1 * cw_ref[2:3, :] + d2 * cw_ref[1:2, :] + d3 * cw_ref[0:1, :]
            dx_ref[rows, :] = dx.astype(BF16)
            colsum = lambda z: jnp.sum(z, axis=0, keepdims=True)
            acc[0:1, :] += colsum(x3 * dxb)
            acc[1:2, :] += colsum(x2 * dxb)
            acc[2:3, :] += colsum(x1 * dxb)
            acc[3:4, :] += colsum(xr * dxb)
            acc[4:5, :] += colsum(dxb)
            acc[5:6, :] += colsum(dzr)
            acc[6:7, :] += colsum(dzi)
            acc[7:8, :] += colsum(dla * (-LRU_C * r))
            return g_c[0:1, :], a_c[0:1, :], dxb[0:8]

        def finish():
            @pl.when(bi == nb - 1)
            def _():
                small_ref[...] = acc[...]
                small_ref[7:8, :] = acc[7:8, :] * (-_sigmoid(-lamv))
                dwa_ref[...] = dwa_s[...]
                dwx_ref[...] = dwx_s[...]

        return tile, finish

    def body(*refs):
        blocks = [block(*_group_views(refs, g, "llllglgllll" + "lllgg" + "gglgg")) for g in range(ng)]

        def step(it, carries):
            return tuple(tile(it, carry) for (tile, _), carry in zip(blocks, carries))

        zero = jnp.zeros((1, HEAD), F32)
        lax.fori_loop(0, nc, step, ((zero, zero, jnp.zeros((8, HEAD), F32)),) * ng)
        for _, finish in blocks:
            finish()

    gw = ng * HEAD

    def col(off):
        return pl.BlockSpec((seq, gw), lambda n, b: (b, off * (nblk // ng) + n))

    vec = lambda r: pl.BlockSpec((r, gw), lambda n, b: (0, n))
    mat = pl.BlockSpec((ng, HEAD, HEAD), lambda n, b: (n, 0, 0))
    blk = pl.BlockSpec((seq, gw), lambda n, b: (b, n))
    dshape = jax.ShapeDtypeStruct((t, w), BF16)
    return pl.pallas_call(
        body, name=name, grid=(nblk // ng, nb),
        in_specs=[col(4), col(5), vec(4), vec(1), mat, vec(1), mat, vec(1), vec(1), blk, blk],
        out_specs=[blk, blk, vec(8), mat, mat],
        out_shape=[dshape, dshape, jax.ShapeDtypeStruct((8, w), F32),
                   jax.ShapeDtypeStruct((nblk, HEAD, HEAD), F32), jax.ShapeDtypeStruct((nblk, HEAD, HEAD), F32)],
        scratch_shapes=[pltpu.VMEM((ng, CHUNK, HEAD), F32)] * 2 + [pltpu.VMEM((8, gw), F32)]
        + [pltpu.VMEM((ng, HEAD, HEAD), F32)] * 2,
        compiler_params=_cp(dimension_semantics=("parallel", "arbitrary")),
    )(proj, proj, cw, cb, wa, ba, wx, bx, lam, hsv, dp)


def _ffn_conv(x, prev8, cw_ref, cbv):
    x2, x1 = _shift_down(prev8, x, 2), _shift_down(prev8, x, 1)
    y = cbv + x2 * cw_ref[0:1, :]
    y = y + x1 * cw_ref[1:2, :]
    y = y + x * cw_ref[2:3, :]
    return (x2, x1), y


def _ffn_act_fwd(up, cw, cb, *, nb, seq, name, deps=()):
    t, f2 = up.shape
    f = f2 // 2
    tc = _tile(f, 256)
    nj = f // tc
    rt = _tile(seq, 256, 16)
    nr = seq // rt

    def body(g_ref, v_ref, cwg_ref, cwv_ref, cbg_ref, cbv_ref, o_ref):
        cbg, cbv = cbg_ref[...], cbv_ref[...]

        def tile(c, carry):
            pg, pv = carry
            rows = pl.ds(pl.multiple_of(c * rt, rt), rt)
            xg, xv = g_ref[rows, :], v_ref[rows, :]
            _, gate = _ffn_conv(xg, pg, cwg_ref, cbg)
            _, val = _ffn_conv(xv, pv, cwv_ref, cbv)
            o_ref[rows, :] = ((gate * _sigmoid(gate)) * val).astype(BF16)
            return xg[rt - 8:], xv[rt - 8:]

        z = jnp.zeros((8, tc), F32)
        lax.fori_loop(0, nr, tile, (z, z))

    gcol = pl.BlockSpec((seq, tc), lambda j, b: (b, j))
    vcol = pl.BlockSpec((seq, tc), lambda j, b: (b, nj + j))
    gv = lambda r: pl.BlockSpec((r, tc), lambda j, b: (0, j))
    vv = lambda r: pl.BlockSpec((r, tc), lambda j, b: (0, nj + j))
    return _pcall(
        body, n_in=6, args=[up, up, cw, cw, cb, cb], deps=deps, name=name, grid=(nj, nb),
        in_specs=[gcol, vcol, gv(3), vv(3), gv(1), vv(1)], out_specs=gcol,
        out_shape=jax.ShapeDtypeStruct((t, f), BF16),
        compiler_params=_cp(dimension_semantics=("parallel", "parallel")))


def _ffn_act_bwd(dact, up, cw, cb, *, nb, seq, name, deps=()):
    t, f2 = up.shape
    f = f2 // 2
    tc = _tile(f, 256)
    nj = f // tc
    rt = _tile(seq, 256, 16)
    nr = seq // rt

    def body(da_ref, g_ref, v_ref, cwg_ref, cwv_ref, cbg_ref, cbv_ref,
             dg_ref, dv_ref, sg_ref, sv_ref, eg, ev, accg, accv):
        bi = pl.program_id(1)
        cbg, cbv = cbg_ref[...], cbv_ref[...]

        @pl.when(bi == 0)
        def _():
            accg[...] = jnp.zeros_like(accg)
            accv[...] = jnp.zeros_like(accv)

        colsum = lambda z: jnp.sum(z, axis=0, keepdims=True)

        def first(c, carry):
            pg, pv = carry
            rows = pl.ds(pl.multiple_of(c * rt, rt), rt)
            xg, xv = g_ref[rows, :], v_ref[rows, :]
            (g2, g1), gate = _ffn_conv(xg, pg, cwg_ref, cbg)
            (v2, v1), val = _ffn_conv(xv, pv, cwv_ref, cbv)
            s = _sigmoid(gate)
            da = da_ref[rows, :]
            dgate = da * val * _dsilu(gate, s)
            dval = da * (gate * s)
            eg[rows, :] = dgate
            ev[rows, :] = dval
            for acc, (s2, s1, s0), d in ((accg, (g2, g1, xg), dgate), (accv, (v2, v1, xv), dval)):
                acc[0:1, :] += colsum(s2 * d)
                acc[1:2, :] += colsum(s1 * d)
                acc[2:3, :] += colsum(s0 * d)
                acc[3:4, :] += colsum(d)
            return xg[rt - 8:], xv[rt - 8:]

        z = jnp.zeros((8, tc), F32)
        lax.fori_loop(0, nr, first, (z, z))

        def second(c, carry):
            base = pl.multiple_of(c * rt, rt)
            rows = pl.ds(base, rt)
            after = pl.ds(pl.multiple_of(jnp.minimum(base + rt, seq - 8), 8), 8)
            more = c < nr - 1
            for e, cw_ref, d_ref in ((eg, cwg_ref, dg_ref), (ev, cwv_ref, dv_ref)):
                d = e[rows, :]
                n8 = jnp.where(more, e[after, :], 0.0)
                d1, d2 = _shift_up(d, n8, 1), _shift_up(d, n8, 2)
                d_ref[rows, :] = (d * cw_ref[2:3, :] + d1 * cw_ref[1:2, :] + d2 * cw_ref[0:1, :]).astype(BF16)
            return carry

        lax.fori_loop(0, nr, second, 0)

        @pl.when(bi == nb - 1)
        def _():
            sg_ref[...] = accg[...]
            sv_ref[...] = accv[...]

    gcol = pl.BlockSpec((seq, tc), lambda j, b: (b, j))
    vcol = pl.BlockSpec((seq, tc), lambda j, b: (b, nj + j))
    gv = lambda r: pl.BlockSpec((r, tc), lambda j, b: (0, j))
    vv = lambda r: pl.BlockSpec((r, tc), lambda j, b: (0, nj + j))
    dshape = jax.ShapeDtypeStruct((t, f), BF16)
    sshape = jax.ShapeDtypeStruct((8, f), F32)
    return _pcall(
        body, n_in=7, args=[dact, up, up, cw, cw, cb, cb], deps=deps, name=name, grid=(nj, nb),
        in_specs=[gcol, gcol, vcol, gv(3), vv(3), gv(1), vv(1)],
        out_specs=[gcol, gcol, gv(8), gv(8)], out_shape=[dshape, dshape, sshape, sshape],
        scratch_shapes=[pltpu.VMEM((seq, tc), F32)] * 2 + [pltpu.VMEM((8, tc), F32)] * 2,
        compiler_params=_cp(dimension_semantics=("parallel", "arbitrary")))


def _adamw_math(wv, g, mv, vv):
    m = ADAM_B1 * mv + (1.0 - ADAM_B1) * g
    v = ADAM_B2 * vv + (1.0 - ADAM_B2) * (g * g)
    m_hat = m / (1.0 - ADAM_B1 ** ADAM_STEP)
    v_hat = v / (1.0 - ADAM_B2 ** ADAM_STEP)
    delta = -ADAM_LR * (m_hat / (jnp.sqrt(v_hat) + ADAM_EPS) + ADAM_WD * wv)
    return delta, m, v


def _adamw_shard(wv, mv, vv, p_own, rb, *, name):
    _, r, cdim = wv.shape
    tr = _row_tile(r, cdim)

    def body(w_ref, m_ref, v_ref, p_ref, rb_ref, g_ref, d_ref, mo_ref, vo_ref):
        g = p_ref[...].astype(F32)
        for k in range(3):
            g = g + rb_ref[k].astype(F32)
        g_ref[...] = g
        d_ref[...], mo_ref[...], vo_ref[...] = _adamw_math(w_ref[...], g, m_ref[...], v_ref[...])

    row = pl.BlockSpec((tr, cdim), lambda i: (i, 0))
    row3 = pl.BlockSpec((None, tr, cdim), lambda i: (0, i, 0))
    shp = jax.ShapeDtypeStruct((1, r, cdim), F32)
    return pl.pallas_call(
        body, name=name, grid=(r // tr,),
        in_specs=[row3, row3, row3, row, pl.BlockSpec((3, tr, cdim), lambda i: (0, i, 0))],
        out_specs=[row3] * 4, out_shape=[shp] * 4, compiler_params=_cp(dimension_semantics=("parallel",)),
    )(wv, mv, vv, p_own, rb)


def _adamw_packed(wv, g, mv, vv, *, name):
    r = wv.shape[0]
    tr = _tile(r, 256, 8)

    def body(w_ref, g_ref, m_ref, v_ref, d_ref, mo_ref, vo_ref):
        d_ref[...], mo_ref[...], vo_ref[...] = _adamw_math(w_ref[...], g_ref[...], m_ref[...], v_ref[...])

    row = pl.BlockSpec((tr, LANE), lambda i: (i, 0))
    shp = jax.ShapeDtypeStruct((r, LANE), F32)
    return pl.pallas_call(
        body, name=name, grid=(r // tr,), in_specs=[row] * 4, out_specs=[row] * 3, out_shape=[shp] * 3,
        compiler_params=_cp(dimension_semantics=("parallel",)),
    )(wv, g, mv, vv)


def _place():
    return lax.axis_index("x"), lax.axis_index("y"), lax.axis_index("c")


def _all_reduce_packed(p, *, name):
    r = p.shape[0]
    tr = _tile(r, 256, 8)

    def body(p_ref, o_ref, buf, send, recv, lsem):
        x, y, c = _place()
        me, sibling = (x, y, c), (x, y, 1 - c)
        chips = [(1 - x, y), (x, 1 - y), (1 - x, 1 - y)]

        def slot(px, py, pc):
            return 4 * px + 2 * py + pc

        def copy(k, block, to, src=None):
            dst = buf.at[slot(*block)]
            return pltpu.make_async_remote_copy(
                src_ref=dst if src is None else src, dst_ref=dst, send_sem=send.at[k], recv_sem=recv.at[k],
                device_id=to, device_id_type=MESH)

        mine = pltpu.make_async_copy(p_ref, buf.at[slot(*me)], lsem)
        mine.start()
        first = [copy(0, me, sibling, src=p_ref)]
        first += [copy(1 + j, me, (*chip, c), src=p_ref) for j, chip in enumerate(chips)]
        for cp in first:
            cp.start()
        passed = [copy(4 + j, (*chip, c), sibling) for j, chip in enumerate(chips)]
        for j, chip in enumerate(chips):
            copy(1 + j, (*chip, c), me).wait_recv()
            passed[j].start()
        copy(0, sibling, me).wait_recv()
        for j, chip in enumerate(chips):
            copy(4 + j, (*chip, 1 - c), me).wait_recv()
        for cp in first + passed:
            cp.wait_send()
        mine.wait()

        def add(i, carry):
            rows = pl.ds(pl.multiple_of(i * tr, tr), tr)
            s = buf[0, rows, :]
            for d in range(1, NDEV):
                s = s + buf[d, rows, :]
            o_ref[rows, :] = s
            return carry

        lax.fori_loop(0, r // tr, add, 0)

    vm = pl.BlockSpec(memory_space=pltpu.VMEM)
    return pl.pallas_call(
        body, name=name, in_specs=[vm], out_specs=vm, out_shape=jax.ShapeDtypeStruct(p.shape, p.dtype),
        scratch_shapes=[pltpu.VMEM((NDEV,) + p.shape, p.dtype), pltpu.SemaphoreType.DMA((7,)),
                        pltpu.SemaphoreType.DMA((7,)), pltpu.SemaphoreType.DMA],
        compiler_params=_cp(),
    )(p)


HBM = pl.BlockSpec(memory_space=pltpu.HBM)
SEM = pl.BlockSpec(memory_space=pltpu.SEMAPHORE)
EFFECT = pltpu.SideEffectType.DATAFLOW_SIDE_EFFECTING


def _plan_copies(plan, s_refs, l_refs, send, recv):
    def pick(kind, a, idx):
        ref = (s_refs if kind == "s" else l_refs)[a]
        return ref if idx is None else ref.at[idx]

    return [pltpu.make_async_remote_copy(
        src_ref=pick(*src), dst_ref=pick(*dst), send_sem=send.at[i], recv_sem=recv.at[i],
        device_id=to, device_id_type=MESH) for i, (src, dst, to) in enumerate(plan(*_place()))]


def _xfer_start(srcs, lands, plan, *, name, deps=()):
    ns, nl = len(srcs), len(lands)
    nd = len(deps)
    ncopy = len(plan(0, 0, 0))

    def body(*refs):
        s_refs, l_refs = refs[:ns], refs[ns:ns + nl]
        send, recv = refs[ns + nl + nd], refs[ns + nl + nd + 1]
        token = refs[-1]
        for cp in _plan_copies(plan, s_refs, l_refs, send, recv):
            cp.start()
        token[...] = jnp.zeros_like(token)

    bufs = list(srcs) + list(lands)
    outs = pl.pallas_call(
        body, name=name,
        out_shape=(pltpu.SemaphoreType.DMA((ncopy,)), pltpu.SemaphoreType.DMA((ncopy,)),
                   *[pltpu.HBM(b.shape, b.dtype) for b in bufs], jax.ShapeDtypeStruct((8, LANE), F32)),
        in_specs=[HBM] * (ns + nl) + [ANY] * nd,
        out_specs=(SEM, SEM, *[HBM] * (ns + nl), pl.BlockSpec(memory_space=pltpu.VMEM)),
        input_output_aliases={i: 2 + i for i in range(ns + nl)},
        compiler_params=pltpu.CompilerParams(has_side_effects=EFFECT),
    )(*[pltpu.with_memory_space_constraint(b, pltpu.HBM) for b in bufs], *deps)
    return outs[0], outs[1], list(outs[2:2 + ns]), list(outs[2 + ns:2 + ns + nl]), outs[-1]


def _xfer_wait(send, recv, srcs, lands, plan, after, *, name):
    ns, nl = len(srcs), len(lands)

    def body(*refs):
        s_refs, l_refs = refs[:ns], refs[ns:ns + nl]
        send_ref, recv_ref = refs[ns + nl], refs[ns + nl + 1]
        for cp in _plan_copies(plan, s_refs, l_refs, send_ref, recv_ref):
            cp.wait_send()
            cp.wait_recv()

    bufs = list(srcs) + list(lands)
    outs = pl.pallas_call(
        body, name=name, out_shape=tuple(pltpu.HBM(b.shape, b.dtype) for b in bufs),
        in_specs=[HBM] * (ns + nl) + [SEM, SEM, ANY], out_specs=tuple([HBM] * (ns + nl)),
        input_output_aliases={i: i for i in range(ns + nl)},
        compiler_params=pltpu.CompilerParams(has_side_effects=EFFECT),
    )(*bufs, send, recv, after)
    return list(outs[:ns]), list(outs[ns:])


def _slot8(px, py, pc):
    return 4 * px + 2 * py + pc


def _slot4(r):
    return lambda px, py, pc: (2 * px + py, pl.ds(pc * r, r))


def _plan_gather_first(slots):
    def plan(x, y, c):
        peers = [(x, y, c), (x, y, 1 - c), (1 - x, y, c), (x, 1 - y, c), (1 - x, 1 - y, c)]
        return [(("s", a, None), ("l", a, slot(x, y, c)), to) for a, slot in enumerate(slots) for to in peers]
    return plan


def _plan_gather_pass(slots):
    def plan(x, y, c):
        chips = [(1 - x, y), (x, 1 - y), (1 - x, 1 - y)]
        return [(("l", a, slot(px, py, c)), ("l", a, slot(px, py, c)), (x, y, 1 - c))
                for a, slot in enumerate(slots) for px, py in chips]
    return plan


def _plan_rs_sibling(r):
    def plan(x, y, c):
        return [(("s", 0, (j, pl.ds((1 - c) * r, r))), ("l", 0, j), (x, y, 1 - c)) for j in range(4)]
    return plan


def _plan_rs_plane():
    def plan(x, y, c):
        chips = [(1 - x, y), (x, 1 - y), (1 - x, 1 - y)]
        return [(("s", 0, 2 * px + py), ("l", 0, k), (px, py, c)) for k, (px, py) in enumerate(chips)]
    return plan


def _add_pairs(g, rcv, cidx, *, name):
    _, r, cdim = rcv.shape
    tr = _row_tile(r, cdim, budget=4194304)
    per = r // tr

    def body(c_ref, g_ref, r_ref, o_ref):
        o_ref[...] = (g_ref[...].astype(F32) + r_ref[...].astype(F32)).astype(o_ref.dtype)

    grid_spec = pltpu.PrefetchScalarGridSpec(
        num_scalar_prefetch=1, grid=(4, per),
        in_specs=[pl.BlockSpec((None, tr, cdim), lambda j, i, c_ref: (j, c_ref[0] * per + i, 0)),
                  pl.BlockSpec((None, tr, cdim), lambda j, i, c_ref: (j, i, 0))],
        out_specs=pl.BlockSpec((None, tr, cdim), lambda j, i, c_ref: (j, i, 0)))
    return pl.pallas_call(
        body, name=name, grid_spec=grid_spec, out_shape=jax.ShapeDtypeStruct((4, r, cdim), g.dtype),
        compiler_params=_cp(dimension_semantics=("parallel", "parallel")),
    )(cidx, g, rcv)


def _pack(arrs):
    flat = jnp.concatenate([a.reshape(-1).astype(F32) for a in arrs])
    n = flat.shape[0]
    rows = -(-n // LANE)
    rows = -(-rows // 256) * 256
    return jnp.pad(flat, (0, rows * LANE - n)).reshape(rows, LANE)


def _unpack(packed, shapes):
    flat = packed.reshape(-1)
    out, o = [], 0
    for s in shapes:
        n = math.prod(s)
        out.append(flat[o:o + n].reshape(s))
        o += n
    return out


def _pad_blocks(a, w, wp):
    lead = a.shape[:-1]
    k = a.shape[-1] // w
    pads = [(0, 0)] * (len(lead) + 1) + [(0, wp - w)]
    return jnp.pad(a.reshape(*lead, k, w), pads).reshape(*lead, k * wp)


def kernel(x, ln1_w, w_in, lb_gamma, hg_norm_w, lru_conv_w, lru_conv_b, lru_wa, lru_ba, lru_wx, lru_bx, lru_lambda, lru_norm_w, w_out, ln2_w, ffn_w_up, ffn_conv_w, ffn_conv_b, ffn_w_down, final_norm_w, loss_target, m_ln1_w, m_w_in, m_lb_gamma, m_hg_norm_w, m_lru_conv_w, m_lru_conv_b, m_lru_wa, m_lru_ba, m_lru_wx, m_lru_bx, m_lru_lambda, m_lru_norm_w, m_w_out, m_ln2_w, m_ffn_w_up, m_ffn_conv_w, m_ffn_conv_b, m_ffn_w_down, m_final_norm_w, v_ln1_w, v_w_in, v_lb_gamma, v_hg_norm_w, v_lru_conv_w, v_lru_conv_b, v_lru_wa, v_lru_ba, v_lru_wx, v_lru_bx, v_lru_lambda, v_lru_norm_w, v_w_out, v_ln2_w, v_ffn_w_up, v_ffn_conv_w, v_ffn_conv_b, v_ffn_w_down, v_final_norm_w):
    nb, seq, d = x.shape
    t = nb * seq
    wmix = d // 2
    in_sh = w_in.shape[2]
    up_sh = ffn_w_up.shape[2]
    up_pad = -(-up_sh // LANE) * LANE
    hs = ffn_w_down.shape[1]
    fpad = 4 * up_pad
    cx, cy, cc = _place()
    me = 4 * cx + 2 * cy + cc
    plane = 2 * cx + cy

    def gather(shards, lands, slots, tag, deps=(), also=lambda x, y, c: []):
        first = _plan_gather_first(slots)
        plan = lambda x, y, c: first(x, y, c) + also(x, y, c)
        st = _xfer_start(shards, lands, plan, name=f"gather_{tag}_start", deps=deps)
        return (st, plan, slots, tag), st[4]

    def pass_on(g, after):
        st, plan, slots, tag = g
        _, lands = _xfer_wait(st[0], st[1], st[2], st[3], plan, after, name=f"gather_{tag}_wait")
        st2 = _xfer_start([], lands, _plan_gather_pass(slots), name=f"gather_{tag}_pass")
        return (st2, slots, tag), st2[4]

    def gathered(g, after):
        st2, slots, tag = g
        return _xfer_wait(st2[0], st2[1], [], st2[3], _plan_gather_pass(slots), after, name=f"gather_{tag}_done")[1]

    land8 = lambda s: lax.empty((NDEV,) + s.shape, s.dtype)

    win_s = w_in[0].astype(BF16)
    wout_s = w_out[0].astype(BF16)
    wup_s = jnp.pad(ffn_w_up[0], ((0, 0), (0, up_pad - up_sh))).astype(BF16)
    wdn_s = ffn_w_down[0].astype(BF16)
    fcw_s = jnp.pad(ffn_conv_w[0], ((0, 0), (0, up_pad - up_sh)))
    ga, tok = gather([win_s, lru_conv_w[0], fcw_s], [land8(win_s), land8(lru_conv_w[0]), land8(fcw_s)],
                     [_slot8] * 3, "a")
    go, tok = gather([wout_s], [land8(wout_s)], [_slot8], "o", deps=(tok,))
    gu, tok = gather([wup_s], [land8(wup_s)], [_slot8], "u", deps=(tok,))
    npad = up_pad - 2 * hs
    zero_pad = lambda x, y, c: [(("s", 1, None), ("l", 0, (j, pl.ds(2 * hs, npad))), (x, y, c)) for j in range(4)]
    assert npad > 0
    gd, tok = gather([wdn_s, jnp.zeros((npad, d), BF16)], [lax.empty((4, up_pad, d), BF16)], [_slot4(hs)], "d",
                     deps=(tok,), also=zero_pad)
    fcb = _pad_blocks(ffn_conv_b, up_sh, up_pad)
    wa_b, wx_b = lru_wa[0].astype(BF16), lru_wx[0].astype(BF16)

    xf = x.reshape(t, d)
    hn = _rms_fwd(xf, ln1_w, name="ln1_fwd", deps=(tok,))
    ga, tok = pass_on(ga, hn)
    win_g, lcw_g, fcw_g = gathered(ga, tok)
    lcw = lcw_g.transpose(1, 0, 2).reshape(lru_conv_w.shape[1], wmix)
    fcw = fcw_g.transpose(1, 0, 2).reshape(ffn_conv_w.shape[1], 2 * fpad)
    proj = _mm(hn, win_g, kind="nn", out_dtype=F32, name="in_proj", tm_cap=512, tk_cap=d)
    go, tok = pass_on(go, proj)
    o_hg, o_pre, states = _hgrn_fwd(proj, lb_gamma, hg_norm_w, nb=nb, seq=seq, name="hgrn_fwd")
    (wout_g,) = gathered(go, o_pre)
    wout_f = wout_g.reshape(1, d, d)
    p_lru, h_lru = _lru_fwd(proj, lcw, lru_conv_b, wa_b, lru_ba, wx_b, lru_bx, lru_lambda,
                            nb=nb, seq=seq, name="lru_fwd", deps=(tok,))
    o_lru = _rms_fwd(p_lru, lru_norm_w, name="lru_norm_fwd")
    gu, tok = pass_on(gu, o_lru)
    mix = jnp.concatenate([o_hg, o_lru], axis=1)
    h1 = _mm(mix, wout_f, kind="nn", out_dtype=F32, name="out_proj", res=xf, tm_cap=512, tk_cap=d, deps=(tok,))
    hn2 = _rms_fwd(h1, ln2_w, name="ln2_fwd")
    (wup_g,) = gathered(gu, hn2)
    up = _mm(hn2, wup_g, kind="nn", out_dtype=F32, name="ffn_up", tm_cap=512, tn_cap=up_pad // 2, tk_cap=d)
    gd, tok = pass_on(gd, up)
    act = _ffn_act_fwd(up, fcw, fcb, nb=nb, seq=seq, name="ffn_act_fwd", deps=(tok,))
    (wdn_g,) = gathered(gd, act)
    wdn_f = wdn_g.reshape(1, fpad, d)
    h2 = _mm(act, wdn_f, kind="nn", out_dtype=F32, name="ffn_down", res=h1, tm_cap=512, tn_cap=1024,
             tk_cap=2 * up_pad)

    dh2, dh2_b, g_fnw, loss_part = _loss_head(h2, final_norm_w.reshape(1, d), loss_target.reshape(t, d),
                                              name="loss_head")
    loss = lax.psum(loss_part[0, 0], ("x", "y", "c"))

    cidx = jnp.reshape(cc, (1,)).astype(jnp.int32)

    def to_sibling(g4, r, tag, deps=()):
        land = lax.empty((4, r, g4.shape[2]), g4.dtype)
        st = _xfer_start([g4], [land], _plan_rs_sibling(r), name=f"rs_sib_start_{tag}", deps=deps)
        return (st, r, tag), st[4]

    def sibling_sum(rs, after):
        st, r, tag = rs
        (g4,), (rcv,) = _xfer_wait(st[0], st[1], st[2], st[3], _plan_rs_sibling(r), after, name=f"rs_sib_wait_{tag}")
        psum = _add_pairs(g4, rcv, cidx, name=f"rs_add_{tag}")
        land = lax.empty((3,) + psum.shape[1:], psum.dtype)
        st2 = _xfer_start([psum], [land], _plan_rs_plane(), name=f"rs_plane_start_{tag}")
        return (st2, tag), st2[4]

    dact = _mm(dh2_b, wdn_f, kind="nt", out_dtype=F32, name="ffn_down_dx", tm_cap=512, tn_cap=up_pad // 2, tk_cap=d)
    g_wdn = _mm(act, dh2_b, kind="tn", out_dtype=BF16, name="ffn_down_dw", tm_cap=1024, tn_cap=1024, tk_cap=t)
    rs_d, tok = to_sibling(g_wdn.reshape(4, up_pad, d), hs, "down")
    dup_g, dup_v, s_g, s_v = _ffn_act_bwd(dact, up, fcw, fcb, nb=nb, seq=seq, name="ffn_act_bwd", deps=(tok,))
    dup = jnp.concatenate([dup_g, dup_v], axis=1)
    pl_d, tok = sibling_sum(rs_d, dup)
    dhn2 = _mm(dup, wup_g, kind="nt", out_dtype=F32, name="ffn_up_dx", tm_cap=512, tn_cap=1024, tk_cap=up_pad,
               k_blocks=2, deps=(tok,))
    g_wup = _mm(dup, hn2, kind="tn", out_dtype=BF16, name="ffn_up_dw", tm_cap=1024, tn_cap=1024, tk_cap=t)
    rs_u, tok = to_sibling(g_wup.reshape(4, 2 * up_pad, d), up_pad, "up")
    dh1, dh1_b, g_ln2 = _rms_bwd(dhn2, h1, ln2_w, name="ln2_bwd", extra=dh2, want_bf16=True, deps=(tok,))
    dmix = _mm(dh1_b, wout_f, kind="nt", out_dtype=F32, name="out_proj_dx", tm_cap=512, tn_cap=1024, tk_cap=d)
    pl_u, tok = sibling_sum(rs_u, dmix)
    g_wout = _mm(mix, dh1_b, kind="tn", out_dtype=BF16, name="out_proj_dw", tn_cap=1024, tk_cap=t, deps=(tok,))
    rs_o, tok = to_sibling(g_wout.reshape(4, d // 4, d), d // NDEV, "out")
    dp_lru, g_lnw = _rms_bwd(dmix, p_lru, lru_norm_w, name="lru_norm_bwd", dy_cb=1, deps=(tok,))
    dxr, dyr, s_lru, g_wa, g_wx = _lru_bwd(proj, lcw, lru_conv_b, wa_b, lru_ba, wx_b, lru_bx, lru_lambda,
                                           h_lru, dp_lru, nb=nb, seq=seq, name="lru_bwd")
    pl_o, tok = sibling_sum(rs_o, dxr)
    dqr, dfr, dir_, dgr, s_hg = _hgrn_bwd(proj, lb_gamma, hg_norm_w, o_pre, states, dmix,
                                          nb=nb, seq=seq, name="hgrn_bwd", deps=(tok,))
    dproj = jnp.concatenate([dqr, dfr, dir_, dgr, dxr, dyr], axis=1)
    g_win = _mm(hn, dproj, kind="tn", out_dtype=BF16, name="in_proj_dw", out_blocks=NDEV, tm_cap=512, tn_cap=in_sh,
                tk_cap=t)
    rs_i, tok = to_sibling(g_win.reshape(4, 2 * d, in_sh), d, "in")
    dhn = _mm(dproj, win_g, kind="nt", out_dtype=F32, name="in_proj_dx", tm_cap=512, tn_cap=1024, tk_cap=in_sh,
              k_blocks=4, deps=(tok,))
    grad_x, g_ln1 = _rms_bwd(dhn, xf, ln1_w, name="ln1_bwd", extra=dh1)

    fcw_parts = jnp.concatenate([s_g[0:3], s_v[0:3]], axis=1).reshape(3, NDEV, up_pad)[:, :, :up_sh]
    fcb_part = jnp.concatenate([s_g[3:4], s_v[3:4]], axis=1).reshape(NDEV, up_pad)[:, :up_sh]
    small_parts = [g_ln1, s_hg[0:2], s_hg[2:3], s_lru[0:4], s_lru[4:5], g_wa, s_lru[5:6], g_wx, s_lru[6:7],
                   s_lru[7:8], g_lnw, g_ln2, fcw_parts, fcb_part, g_fnw]
    reduced = _all_reduce_packed(_pack(small_parts), name="allreduce_small")
    pl_i, _ = sibling_sum(rs_i, reduced)
    summed = _unpack(reduced, [p.shape for p in small_parts])
    (s_ln1, s_lbg, s_hgn, s_lcw, s_lcb, s_wa, s_ba, s_wx, s_bx, s_lam, s_lnw, s_ln2, s_fcw, s_fcb, s_fnw) = summed
    sh_lcw = lru_conv_w.shape[2]
    g_small = {
        "ln1_w": s_ln1, "lb_gamma": s_lbg, "hg_norm_w": s_hgn,
        "lru_conv_w": lax.dynamic_slice_in_dim(s_lcw, me * sh_lcw, sh_lcw, axis=1),
        "lru_conv_b": s_lcb, "lru_wa": s_wa, "lru_ba": s_ba, "lru_wx": s_wx, "lru_bx": s_bx,
        "lru_lambda": s_lam, "lru_norm_w": s_lnw, "ln2_w": s_ln2,
        "ffn_conv_w": lax.dynamic_index_in_dim(s_fcw, me, 1, keepdims=False),
        "ffn_conv_b": s_fcb, "final_norm_w": s_fnw,
    }
    w_small = {"ln1_w": ln1_w, "lb_gamma": lb_gamma, "hg_norm_w": hg_norm_w, "lru_conv_w": lru_conv_w,
               "lru_conv_b": lru_conv_b, "lru_wa": lru_wa, "lru_ba": lru_ba, "lru_wx": lru_wx, "lru_bx": lru_bx,
               "lru_lambda": lru_lambda, "lru_norm_w": lru_norm_w, "ln2_w": ln2_w, "ffn_conv_w": ffn_conv_w,
               "ffn_conv_b": ffn_conv_b, "final_norm_w": final_norm_w}
    m_small = {"ln1_w": m_ln1_w, "lb_gamma": m_lb_gamma, "hg_norm_w": m_hg_norm_w, "lru_conv_w": m_lru_conv_w,
               "lru_conv_b": m_lru_conv_b, "lru_wa": m_lru_wa, "lru_ba": m_lru_ba, "lru_wx": m_lru_wx,
               "lru_bx": m_lru_bx, "lru_lambda": m_lru_lambda, "lru_norm_w": m_lru_norm_w, "ln2_w": m_ln2_w,
               "ffn_conv_w": m_ffn_conv_w, "ffn_conv_b": m_ffn_conv_b, "final_norm_w": m_final_norm_w}
    v_small = {"ln1_w": v_ln1_w, "lb_gamma": v_lb_gamma, "hg_norm_w": v_hg_norm_w, "lru_conv_w": v_lru_conv_w,
               "lru_conv_b": v_lru_conv_b, "lru_wa": v_lru_wa, "lru_ba": v_lru_ba, "lru_wx": v_lru_wx,
               "lru_bx": v_lru_bx, "lru_lambda": v_lru_lambda, "lru_norm_w": v_lru_norm_w, "ln2_w": v_ln2_w,
               "ffn_conv_w": v_ffn_conv_w, "ffn_conv_b": v_ffn_conv_b, "final_norm_w": v_final_norm_w}
    names = list(w_small)
    shapes = [w_small[k].shape for k in names]
    g_small = {k: g_small[k].reshape(w_small[k].shape) for k in names}
    packed = _adamw_packed(_pack([w_small[k] for k in names]), _pack([g_small[k] for k in names]),
                           _pack([m_small[k] for k in names]), _pack([v_small[k] for k in names]),
                           name="adamw_small")
    d_small, nm_small, nv_small = ({k: a for k, a in zip(names, _unpack(pk, shapes))} for pk in packed)

    def finish(pl_x, after, trim=lambda a: a):
        st, tag = pl_x
        (psum,), (rb,) = _xfer_wait(st[0], st[1], st[2], st[3], _plan_rs_plane(), after, name=f"rs_plane_wait_{tag}")
        return trim(lax.dynamic_index_in_dim(psum, plane, 0, keepdims=False)), trim(rb)

    upd_dn = _adamw_shard(ffn_w_down, m_ffn_w_down, v_ffn_w_down, *finish(pl_d, packed[0]), name="adamw_w_down")
    tr3 = lambda a: a.transpose(0, 2, 1)
    upd_up_t = _adamw_shard(tr3(ffn_w_up), tr3(m_ffn_w_up), tr3(v_ffn_w_up), *finish(pl_u, upd_dn[0]),
                            name="adamw_w_up")
    upd_up = [tr3(a) for a in upd_up_t]
    upd_out = _adamw_shard(w_out, m_w_out, v_w_out, *finish(pl_o, upd_up_t[0]), name="adamw_w_out")
    upd_in = _adamw_shard(w_in, m_w_in, v_w_in, *finish(pl_i, upd_out[0]), name="adamw_w_in")

    grads, deltas, new_m, new_v = dict(g_small), dict(d_small), dict(nm_small), dict(nv_small)
    for k, upd in (("w_in", upd_in), ("w_out", upd_out), ("ffn_w_up", upd_up), ("ffn_w_down", upd_dn)):
        grads[k], deltas[k], new_m[k], new_v[k] = upd
    order = ["ln1_w", "w_in", "lb_gamma", "hg_norm_w", "lru_conv_w", "lru_conv_b", "lru_wa", "lru_ba", "lru_wx",
             "lru_bx", "lru_lambda", "lru_norm_w", "w_out", "ln2_w", "ffn_w_up", "ffn_conv_w", "ffn_conv_b",
             "ffn_w_down", "final_norm_w"]
    return (loss, grad_x.reshape(nb, seq, d), *[grads[k] for k in order], *[deltas[k] for k in order],
            *[new_m[k] for k in order], *[new_v[k] for k in order])
```

```python
import math

import jax
import jax.numpy as jnp
from jax import lax
from jax.experimental import pallas as pl
from jax.experimental.pallas import tpu as pltpu

F32, BF16 = jnp.float32, jnp.bfloat16
EPS = 1e-6
HEAD = 128
CHUNK = 64
SUB = 16
NSUB = CHUNK // SUB
LRU_C = 8.0
LANE = 128
NDEV = 8
ADAM_LR, ADAM_B1, ADAM_B2, ADAM_EPS, ADAM_WD, ADAM_STEP = 0.001, 0.9, 0.999, 1e-08, 0.01, 10
MESH = pl.DeviceIdType.MESH
ANY = pl.BlockSpec(memory_space=pl.ANY)
VMEM_LIMIT = 56 * 1024 * 1024


def _cp(**kw):
    return pltpu.CompilerParams(vmem_limit_bytes=VMEM_LIMIT, **kw)


def _tile(n, cap, mult=LANE):
    best = None
    for t in range(mult, min(n, cap) + 1, mult):
        if n % t == 0:
            best = t
    return best if best is not None else n


def _row_tile(r, cdim, budget=262144):
    return _tile(r, max(16, budget // cdim), 16)


def _sigmoid(x):
    return jax.nn.sigmoid(x)


def _dsilu(x, s):
    return s * (1.0 + x * (1.0 - s))


def _iota_rows(n, w=LANE):
    return lax.broadcasted_iota(jnp.int32, (n, w), 0)


def _lane_groups(n, want=2):
    while n % want:
        want //= 2
    return want


def _group_views(refs, g, kinds):
    assert len(refs) == len(kinds)
    cols = pl.ds(g * HEAD, HEAD)
    return [r.at[:, cols] if kind == "l" else r.at[g] for r, kind in zip(refs, kinds)]


def _shift_down(prev8, xt, k):
    cat = jnp.concatenate([prev8, xt], axis=0)
    return pltpu.roll(cat, k, 0)[8:]


def _shift_up(xt, next8, k):
    cat = jnp.concatenate([xt, next8], axis=0)
    n = cat.shape[0]
    return pltpu.roll(cat, n - k, 0)[: xt.shape[0]]


def _scan_fwd(a, u):
    n = a.shape[0]
    row = _iota_rows(n, a.shape[1])
    k = 1
    while k < n:
        keep = row >= k
        a_s = jnp.where(keep, pltpu.roll(a, k, 0), 1.0)
        u_s = jnp.where(keep, pltpu.roll(u, k, 0), 0.0)
        u = a * u_s + u
        a = a * a_s
        k *= 2
    return a, u


def _scan_bwd(a, u):
    n = a.shape[0]
    row = _iota_rows(n, a.shape[1])
    k = 1
    while k < n:
        keep = row < n - k
        a_s = jnp.where(keep, pltpu.roll(a, n - k, 0), 1.0)
        u_s = jnp.where(keep, pltpu.roll(u, n - k, 0), 0.0)
        u = a * u_s + u
        a = a * a_s
        k *= 2
    return a, u


def _cumsum_fwd(u):
    n = u.shape[0]
    row = _iota_rows(n, u.shape[1])
    k = 1
    while k < n:
        u = u + jnp.where(row >= k, pltpu.roll(u, k, 0), 0.0)
        k *= 2
    return u


def _cumsum_bwd(u):
    n = u.shape[0]
    row = _iota_rows(n, u.shape[1])
    k = 1
    while k < n:
        u = u + jnp.where(row < n - k, pltpu.roll(u, n - k, 0), 0.0)
        k *= 2
    return u


def _dot(a, b, dims):
    return lax.dot_general(a.astype(BF16), b.astype(BF16), (dims, ((), ())), preferred_element_type=F32)


NN = ((1,), (0,))
NT = ((1,), (1,))
TN = ((0,), (0,))


def _pcall(body, *, n_in, in_specs, args, deps=(), **kw):
    nd = len(deps)
    if nd:
        inner = body

        def body(*refs):
            return inner(*refs[:n_in], *refs[n_in + nd:])

        in_specs = list(in_specs) + [ANY] * nd
        args = list(args) + list(deps)
    return pl.pallas_call(body, in_specs=in_specs, **kw)(*args)


def _mm(a, b3, *, kind, out_dtype, name, res=None, tm_cap=1024, tn_cap=1536, tk_cap=1024, out_blocks=1, k_blocks=1,
        deps=()):
    if kind == "nn":
        m, kdim = a.shape
        nb, _, nsh = b3.shape
        n = nb * nsh
        tm, tn, tk = _tile(m, tm_cap), _tile(nsh, tn_cap), _tile(kdim, tk_cap)
        per = nsh // tn
        a_blk, a_ix = (tm, tk), lambda i, j, k: (i, k)
        b_blk, b_ix = (None, tk, tn), lambda i, j, k: (j // per, k, j % per)
        dims = NN
    elif kind == "nt":
        m, kdim = a.shape
        nb, n, ksh = b3.shape
        tm, tn, tk = _tile(m, tm_cap), _tile(n, tn_cap), _tile(ksh, tk_cap)
        per = ksh // tk
        a_blk, a_ix = (tm, tk), lambda i, j, k: (i, k)
        b_blk, b_ix = (None, tn, tk), lambda i, j, k: (k // per, j, k % per)
        if k_blocks > 1:
            assert tk == ksh and nb % k_blocks == 0
            tk = k_blocks * ksh
            a_blk, b_blk, b_ix = (tm, tk), (k_blocks, tn, ksh), lambda i, j, k: (k, j, 0)
        dims = NT
    else:
        kdim, m = a.shape
        n = b3.shape[1]
        nsh = n // out_blocks
        tm, tn, tk = _tile(m, tm_cap), _tile(nsh, tn_cap), _tile(kdim, tk_cap)
        per = nsh // tn
        a_blk, a_ix = (tk, tm), lambda i, j, k: (k, i)
        b_blk, b_ix = (tk, tn), lambda i, j, k: (k, j)
        dims = TN
    nk = kdim // tk
    j_outer = nk == 1
    grid = (n // tn, m // tm, nk) if j_outer else (m // tm, n // tn, nk)
    at = (lambda f: lambda g0, g1, k: f(g1, g0, k)) if j_outer else (lambda f: f)
    a_spec, b_spec = pl.BlockSpec(a_blk, at(a_ix)), pl.BlockSpec(b_blk, at(b_ix))

    def body(*refs):
        a_ref, b_ref = refs[:2]
        r_ref = refs[2] if res is not None else None
        o_ref = refs[3] if res is not None else refs[2]
        if k_blocks > 1:
            w = b_ref.shape[2]
            part = sum(lax.dot_general(a_ref[:, s * w:(s + 1) * w], b_ref[s], (dims, ((), ())),
                                       preferred_element_type=F32) for s in range(k_blocks))
        else:
            part = lax.dot_general(a_ref[...], b_ref[...], (dims, ((), ())), preferred_element_type=F32)
        if nk == 1:
            o_ref[...] = (part if res is None else part + r_ref[...]).astype(o_ref.dtype)
            return
        acc = refs[-1]
        k = pl.program_id(2)

        @pl.when(k == 0)
        def _():
            acc[...] = part

        @pl.when(k > 0)
        def _():
            acc[...] += part

        @pl.when(k == nk - 1)
        def _():
            r = acc[...]
            if res is not None:
                r = r + r_ref[...]
            o_ref[...] = r.astype(o_ref.dtype)

    in_specs = [a_spec, b_spec]
    args = [a, b3]
    if res is not None:
        in_specs.append(pl.BlockSpec((tm, tn), at(lambda i, j, k: (i, j))))
        args.append(res)
    if kind == "tn":
        out_shape = jax.ShapeDtypeStruct((out_blocks, m, nsh), out_dtype)
        out_spec = pl.BlockSpec((None, tm, tn), at(lambda i, j, k: (j // per, i, j % per)))
    else:
        out_shape = jax.ShapeDtypeStruct((m, n), out_dtype)
        out_spec = pl.BlockSpec((tm, tn), at(lambda i, j, k: (i, j)))
    return _pcall(
        body, n_in=len(args), in_specs=in_specs, args=args, deps=deps,
        name=name, grid=grid, out_specs=out_spec, out_shape=out_shape,
        scratch_shapes=[pltpu.VMEM((tm, tn), F32)] if nk > 1 else [],
        compiler_params=_cp(dimension_semantics=("parallel", "parallel", "arbitrary")))


def _rms_fwd(x, w, *, name, tm=256, deps=()):
    t, d = x.shape
    tm = _tile(t, tm, 16)

    def body(x_ref, w_ref, o_ref):
        xv = x_ref[...]
        r = lax.rsqrt(jnp.mean(xv * xv, axis=-1, keepdims=True) + EPS)
        o_ref[...] = ((xv * r) * w_ref[...]).astype(o_ref.dtype)

    return _pcall(
        body, n_in=2, args=[x, w], deps=deps, name=name, grid=(t // tm,),
        in_specs=[pl.BlockSpec((tm, d), lambda i: (i, 0)), pl.BlockSpec((1, d), lambda i: (0, 0))],
        out_specs=pl.BlockSpec((tm, d), lambda i: (i, 0)),
        out_shape=jax.ShapeDtypeStruct((t, d), BF16), compiler_params=_cp())


def _rms_bwd(dy, x, w, *, name, extra=None, dy_cb=0, want_bf16=False, tm=256, deps=()):
    t, d = x.shape
    tm = _tile(t, tm, 16)

    def body(*refs):
        refs = list(refs)
        dy_ref, x_ref, w_ref = refs[:3]
        e_ref = refs[3] if extra is not None else None
        outs = refs[4:] if extra is not None else refs[3:]
        dx_ref = outs[0]
        dxb_ref = outs[1] if want_bf16 else None
        dw_ref = outs[-1]
        i = pl.program_id(0)
        xv = x_ref[...]
        r = lax.rsqrt(jnp.mean(xv * xv, axis=-1, keepdims=True) + EPS)
        nh = xv * r
        dyv = dy_ref[...]
        dn = dyv * w_ref[...]
        dx = r * (dn - nh * jnp.mean(dn * nh, axis=-1, keepdims=True))
        if extra is not None:
            dx = dx + e_ref[...]
        dx_ref[...] = dx
        if want_bf16:
            dxb_ref[...] = dx.astype(BF16)
        part = jnp.sum(dyv * nh, axis=0, keepdims=True)

        @pl.when(i == 0)
        def _():
            dw_ref[...] = part

        @pl.when(i > 0)
        def _():
            dw_ref[...] += part

    row = pl.BlockSpec((tm, d), lambda i: (i, 0))
    in_specs = [pl.BlockSpec((tm, d), lambda i: (i, dy_cb)), row, pl.BlockSpec((1, d), lambda i: (0, 0))]
    args = [dy, x, w]
    if extra is not None:
        in_specs.append(row)
        args.append(extra)
    out_shape = [jax.ShapeDtypeStruct((t, d), F32)]
    out_specs = [row]
    if want_bf16:
        out_shape.append(jax.ShapeDtypeStruct((t, d), BF16))
        out_specs.append(row)
    out_shape.append(jax.ShapeDtypeStruct((1, d), F32))
    out_specs.append(pl.BlockSpec((1, d), lambda i: (0, 0)))
    return _pcall(
        body, n_in=len(args), in_specs=in_specs, args=args, deps=deps,
        name=name, grid=(t // tm,), out_specs=out_specs, out_shape=out_shape,
        compiler_params=_cp(dimension_semantics=("arbitrary",)))


def _loss_head(h, w, tgt, *, name, tm=256):
    t, d = h.shape
    tm = _tile(t, tm, 16)

    def body(h_ref, w_ref, t_ref, dh_ref, dhb_ref, dw_ref, loss_ref):
        i = pl.program_id(0)
        xv = h_ref[...]
        wv = w_ref[...]
        r = lax.rsqrt(jnp.mean(xv * xv, axis=-1, keepdims=True) + EPS)
        nh = xv * r
        e = nh * wv - t_ref[...]
        part_loss = jnp.full((1, LANE), 0.5 * jnp.sum(jnp.mean(e * e, axis=-1, keepdims=True)), F32)
        dyv = e * (1.0 / d)
        dn = dyv * wv
        dx = r * (dn - nh * jnp.mean(dn * nh, axis=-1, keepdims=True))
        dh_ref[...] = dx
        dhb_ref[...] = dx.astype(BF16)
        part = jnp.sum(dyv * nh, axis=0, keepdims=True)

        @pl.when(i == 0)
        def _():
            dw_ref[...] = part
            loss_ref[...] = part_loss

        @pl.when(i > 0)
        def _():
            dw_ref[...] += part
            loss_ref[...] += part_loss

    row = pl.BlockSpec((tm, d), lambda i: (i, 0))
    vec = pl.BlockSpec((1, d), lambda i: (0, 0))
    return pl.pallas_call(
        body, name=name, grid=(t // tm,), in_specs=[row, vec, row],
        out_specs=[row, row, vec, pl.BlockSpec((1, LANE), lambda i: (0, 0))],
        out_shape=[jax.ShapeDtypeStruct((t, d), F32), jax.ShapeDtypeStruct((t, d), BF16),
                   jax.ShapeDtypeStruct((1, d), F32), jax.ShapeDtypeStruct((1, LANE), F32)],
        compiler_params=_cp(dimension_semantics=("arbitrary",)),
    )(h, w, tgt)


def _lower_bound(lbg_ref):
    g0, g1 = lbg_ref[0:1, :], lbg_ref[1:2, :]
    m = jnp.maximum(g0, g1)
    e0, e1 = jnp.exp(g0 - m), jnp.exp(g1 - m)
    return e0 / (e0 + e1)


def _seg_bounds():
    offs, o = {}, 0
    for i in range(1, NSUB):
        offs[i] = (o, o + SUB * i)
        o += SUB * i
    return offs, o


def _pad_rows(x, n):
    if x.shape[0] == n:
        return x
    return jnp.concatenate([x, jnp.zeros((n - x.shape[0], x.shape[1]), x.dtype)], axis=0)


def _offdiag_mask():
    offs, total = _seg_bounds()
    padded = -(-total // LANE) * LANE
    rsub = lax.broadcasted_iota(jnp.int32, (CHUNK, padded), 0) // SUB
    col = lax.broadcasted_iota(jnp.int32, (CHUNK, padded), 1)
    cseg = jnp.zeros((CHUNK, padded), jnp.int32)
    for i in range(1, NSUB):
        cseg = cseg + (col >= offs[i][0]).astype(jnp.int32)
    return (rsub == cseg) & (col < total)


def _offdiag_setup(q, k, b, v, b_c, mask):
    offs, total = _seg_bounds()
    padded = -(-total // LANE) * LANE
    eq_parts = [jnp.zeros((SUB, HEAD), F32)]
    ek_parts, k_parts, v_parts = [], [], []
    for i in range(1, NSUB):
        r_i = b_c[SUB * i - 1:SUB * i, :]
        eq_parts.append(jnp.exp(b[SUB * i:SUB * (i + 1)] - r_i))
        ek_parts.append(jnp.exp(r_i - b[0:SUB * i]))
        k_parts.append(k[0:SUB * i])
        v_parts.append(v[0:SUB * i])
    eq = jnp.concatenate(eq_parts, axis=0)
    ek = _pad_rows(jnp.concatenate(ek_parts, axis=0), padded)
    kt = _pad_rows(jnp.concatenate(k_parts, axis=0), padded) * ek
    vs = _pad_rows(jnp.concatenate(v_parts, axis=0), padded)
    qt = q * eq
    a = jnp.where(mask, _dot(qt, kt, NT), 0.0)
    return offs, eq, ek, kt, vs, qt, a


def _hgrn_fwd(proj, lbg, nw, *, nb, seq, name):
    t = proj.shape[0]
    w = lbg.shape[1]
    nh = w // HEAD
    nc = seq // CHUNK
    ng = _lane_groups(nh)

    def head(mask, q_ref, f_ref, i_ref, g_ref, lbg_ref, nw_ref, ohg_ref, opre_ref, st_ref, k_c, b_c, v_c, st):
        lb = _lower_bound(lbg_ref)
        nwv = nw_ref[...]
        st[...] = jnp.zeros_like(st)

        def chunk(c, carry):
            rows = pl.ds(pl.multiple_of(c * CHUNK, CHUNK), CHUNK)
            qr = q_ref[rows, :]
            q = qr * _sigmoid(qr)
            f = lb + (1.0 - lb) * _sigmoid(f_ref[rows, :])
            k = 1.0 - f
            b = _cumsum_fwd(jnp.log(f))
            v = i_ref[rows, :]
            k_c[...] = k
            b_c[...] = b
            v_c[...] = v
            s_t = st[...]
            st_ref[c] = s_t
            o = _dot(q * jnp.exp(b), s_t, NT)
            _, _, _, _, vs, _, a = _offdiag_setup(q, k, b, v, b_c, mask)
            o = o + _dot(a, vs, NN)
            diag = []
            for i in range(NSUB):
                accs = [jnp.zeros((8, HEAD), F32) for _ in range(SUB // 8)]
                for j in range(SUB):
                    r = SUB * i + j
                    bj, kj, vj = b_c[r:r + 1, :], k_c[r:r + 1, :], v_c[r:r + 1, :]
                    for p in range(j // 8, SUB // 8):
                        lo = SUB * i + 8 * p
                        d = jnp.exp(b[lo:lo + 8] - bj)
                        if 8 * p < j:
                            d = jnp.where(_iota_rows(8) + 8 * p >= j, d, 0.0)
                        s = jnp.sum(q[lo:lo + 8] * d * kj, axis=-1, keepdims=True)
                        accs[p] = accs[p] + s * vj
                diag.extend(accs)
            o = o + jnp.concatenate(diag, axis=0)
            bl = b_c[CHUNK - 1:CHUNK, :]
            kb = k * jnp.exp(bl - b)
            st[...] = s_t * jnp.exp(bl) + _dot(v, kb, TN)
            opre_ref[rows, :] = o
            rn = lax.rsqrt(jnp.mean(o * o, axis=-1, keepdims=True) + EPS)
            gr = g_ref[rows, :]
            ohg_ref[rows, :] = (((o * rn) * nwv) * (gr * _sigmoid(gr))).astype(BF16)
            return carry

        return chunk

    def body(*refs):
        mask = _offdiag_mask()
        chunks = [head(mask, *_group_views(refs, g, "llllllll" + "ggggg")) for g in range(ng)]

        def step(c, carry):
            for chunk in chunks:
                chunk(c, carry)
            return carry

        lax.fori_loop(0, nc, step, 0)

    gw = ng * HEAD

    def col(off):
        return pl.BlockSpec((seq, gw), lambda h, b: (b, off * (nh // ng) + h))

    vec = lambda r: pl.BlockSpec((r, gw), lambda h, b: (0, h))
    out_blk = pl.BlockSpec((seq, gw), lambda h, b: (b, h))
    return pl.pallas_call(
        body, name=name, grid=(nh // ng, nb),
        in_specs=[col(0), col(1), col(2), col(3), vec(2), vec(1)],
        out_specs=[out_blk, out_blk, pl.BlockSpec((None, ng, nc, HEAD, HEAD), lambda h, b: (b, h, 0, 0, 0))],
        out_shape=[jax.ShapeDtypeStruct((t, w), BF16), jax.ShapeDtypeStruct((t, w), F32),
                   jax.ShapeDtypeStruct((nb, nh, nc, HEAD, HEAD), F32)],
        scratch_shapes=[pltpu.VMEM((ng, CHUNK, HEAD), F32)] * 3 + [pltpu.VMEM((ng, HEAD, HEAD), F32)],
        compiler_params=_cp(dimension_semantics=("parallel", "parallel")),
    )(proj, proj, proj, proj, lbg, nw)


def _hgrn_bwd(proj, lbg, nw, opre, states, dmix, *, nb, seq, name, deps=()):
    t = proj.shape[0]
    w = lbg.shape[1]
    nh = w // HEAD
    nc = seq // CHUNK
    ng = _lane_groups(nh)

    def head(mask, q_ref, f_ref, i_ref, g_ref, lbg_ref, nw_ref, opre_ref, st_ref, dm_ref,
             dq_ref, df_ref, di_ref, dg_ref, small_ref, k_c, b_c, v_c, dst, dlb_s, dwn_s):
        bi = pl.program_id(1)
        lb = _lower_bound(lbg_ref)
        nwv = nw_ref[...]
        dst[...] = jnp.zeros_like(dst)

        @pl.when(bi == 0)
        def _():
            dlb_s[...] = jnp.zeros_like(dlb_s)
            dwn_s[...] = jnp.zeros_like(dwn_s)

        def chunk(it, carry):
            c = nc - 1 - it
            rows = pl.ds(pl.multiple_of(c * CHUNK, CHUNK), CHUNK)
            qr = q_ref[rows, :]
            sq = _sigmoid(qr)
            q = qr * sq
            sg = _sigmoid(f_ref[rows, :])
            f = lb + (1.0 - lb) * sg
            k = 1.0 - f
            b = _cumsum_fwd(jnp.log(f))
            v = i_ref[rows, :]
            k_c[...] = k
            b_c[...] = b
            v_c[...] = v
            o = opre_ref[rows, :]
            rn = lax.rsqrt(jnp.mean(o * o, axis=-1, keepdims=True) + EPS)
            nhat = o * rn
            dm = dm_ref[rows, :]
            gr = g_ref[rows, :]
            sgr = _sigmoid(gr)
            dnw = dm * (gr * sgr)
            dg_ref[rows, :] = (dm * (nhat * nwv) * _dsilu(gr, sgr)).astype(BF16)
            dwn_s[...] += jnp.sum(dnw * nhat, axis=0, keepdims=True)
            dn = dnw * nwv
            do = rn * (dn - nhat * jnp.mean(dn * nhat, axis=-1, keepdims=True))
            s_t = st_ref[c]
            ds = dst[...]
            eb = jnp.exp(b)
            qb = q * eb
            bl = b_c[CHUNK - 1:CHUNK, :]
            ebl = jnp.exp(bl)
            kdec = jnp.exp(bl - b)
            kb = k * kdec
            dqb = _dot(do, s_t, NN)
            dkb = _dot(v, ds, NN)
            dv = _dot(kb, ds, NT)
            d_ebl = jnp.sum(ds * s_t, axis=0, keepdims=True)
            dst[...] = ds * ebl + _dot(do, qb, TN)
            dq = dqb * eb
            dk = dkb * kdec
            t_kb = dkb * kb
            db = dqb * qb - t_kb
            db_last = jnp.sum(t_kb, axis=0, keepdims=True) + d_ebl * ebl
            offs, eq, ek, kt, vs, qt, a = _offdiag_setup(q, k, b, v, b_c, mask)
            da = jnp.where(mask, _dot(do, vs, NT), 0.0)
            dvs = _dot(a, do, TN)
            dqt = _dot(da, kt, NN)
            dkt = _dot(da, qt, TN)
            dq = dq + dqt * eq
            db = db + dqt * qt
            gk, gb = dkt * ek, dkt * kt
            zero8 = jnp.zeros((8, HEAD), F32)
            off_k, off_v, off_b = ([zero8] * (CHUNK // 8) for _ in range(3))
            for i in range(1, NSUB):
                lo, hi = offs[i]
                for p in range((hi - lo) // 8):
                    rows8 = slice(lo + 8 * p, lo + 8 * p + 8)
                    off_k[p] = off_k[p] + gk[rows8]
                    off_b[p] = off_b[p] - gb[rows8]
                    off_v[p] = off_v[p] + dvs[rows8]
            npv = SUB // 8
            dq_d, dk_d, dv_d, db_d = [], [], [], []
            for i in range(NSUB):
                aq = [zero8 for _ in range(npv)]
                ak = [zero8 for _ in range(npv)]
                av = [off_v[i * npv + p] for p in range(npv)]
                for j in range(SUB):
                    r = SUB * i + j
                    bj, kj, vj = b_c[r:r + 1, :], k_c[r:r + 1, :], v_c[r:r + 1, :]
                    pm_sum, ad_sum = None, None
                    for p in range(j // 8, npv):
                        lo = SUB * i + 8 * p
                        d = jnp.exp(b[lo:lo + 8] - bj)
                        if 8 * p < j:
                            d = jnp.where(_iota_rows(8) + 8 * p >= j, d, 0.0)
                        qd = q[lo:lo + 8] * d
                        dop = do[lo:lo + 8]
                        a_j = jnp.sum(qd * kj, axis=-1, keepdims=True)
                        da_j = jnp.sum(dop * vj, axis=-1, keepdims=True)
                        aq[p] = aq[p] + (da_j * d) * kj
                        pm, ad = da_j * qd, a_j * dop
                        pm_sum = pm if pm_sum is None else pm_sum + pm
                        ad_sum = ad if ad_sum is None else ad_sum + ad
                    pj = j // 8
                    here = _iota_rows(8) == (j - 8 * pj)
                    ak[pj] = ak[pj] + jnp.where(here, jnp.sum(pm_sum, axis=0, keepdims=True), 0.0)
                    av[pj] = av[pj] + jnp.where(here, jnp.sum(ad_sum, axis=0, keepdims=True), 0.0)
                for p in range(npv):
                    lo = SUB * i + 8 * p
                    db_d.append(off_b[i * npv + p] + q[lo:lo + 8] * aq[p] - k[lo:lo + 8] * ak[p])
                    dk_d.append(off_k[i * npv + p] + ak[p])
                dq_d.extend(aq)
                dv_d.extend(av)
            dq = dq + jnp.concatenate(dq_d, axis=0)
            dk = dk + jnp.concatenate(dk_d, axis=0)
            dv = dv + jnp.concatenate(dv_d, axis=0)
            db = db + jnp.concatenate(db_d, axis=0)
            db = db + jnp.where(_iota_rows(CHUNK) == CHUNK - 1, db_last, 0.0)
            dgl = _cumsum_bwd(db)
            dfv = dgl / f - dk
            dlb_s[...] += jnp.sum(dfv * (1.0 - sg), axis=0, keepdims=True)
            df_ref[rows, :] = (dfv * (1.0 - lb) * (sg * (1.0 - sg))).astype(BF16)
            dq_ref[rows, :] = (dq * _dsilu(qr, sq)).astype(BF16)
            di_ref[rows, :] = dv.astype(BF16)
            return carry

        def finish():
            @pl.when(bi == nb - 1)
            def _():
                dgam = dlb_s[...] * lb * (1.0 - lb)
                small_ref[...] = jnp.zeros_like(small_ref)
                small_ref[0:1, :] = dgam
                small_ref[1:2, :] = -dgam
                small_ref[2:3, :] = dwn_s[...]

        return chunk, finish

    def body(*refs):
        mask = _offdiag_mask()
        heads = [head(mask, *_group_views(refs, g, "lllllllgl" + "lllll" + "ggggll")) for g in range(ng)]

        def step(it, carry):
            for chunk, _ in heads:
                chunk(it, carry)
            return carry

        lax.fori_loop(0, nc, step, 0)
        for _, finish in heads:
            finish()

    gw = ng * HEAD

    def col(off):
        return pl.BlockSpec((seq, gw), lambda h, b: (b, off * (nh // ng) + h))

    vec = lambda r: pl.BlockSpec((r, gw), lambda h, b: (0, h))
    blk = pl.BlockSpec((seq, gw), lambda h, b: (b, h))
    dshape = jax.ShapeDtypeStruct((t, w), BF16)
    return _pcall(
        body, n_in=9, args=[proj, proj, proj, proj, lbg, nw, opre, states, dmix], deps=deps,
        name=name, grid=(nh // ng, nb),
        in_specs=[col(0), col(1), col(2), col(3), vec(2), vec(1), blk,
                  pl.BlockSpec((None, ng, nc, HEAD, HEAD), lambda h, b: (b, h, 0, 0, 0)), blk],
        out_specs=[blk, blk, blk, blk, vec(8)],
        out_shape=[dshape, dshape, dshape, dshape, jax.ShapeDtypeStruct((8, w), F32)],
        scratch_shapes=[pltpu.VMEM((ng, CHUNK, HEAD), F32)] * 3 + [pltpu.VMEM((ng, HEAD, HEAD), F32)]
        + [pltpu.VMEM((1, gw), F32)] * 2,
        compiler_params=_cp(dimension_semantics=("parallel", "arbitrary")))


def _expm1(x):
    poly = x * (1.0 + x * (0.5 + x * (1.0 / 6 + x * (1.0 / 24 + x * (1.0 / 120 + x * (1.0 / 720))))))
    return jnp.where(jnp.abs(x) < 0.25, poly, jnp.exp(x) - 1.0)


def _softplus_neg(lam):
    x = -lam
    e = jnp.exp(-jnp.abs(x))
    u = 1.0 + e
    l1p = jnp.where(u == 1.0, e, jnp.log(u) * (e / jnp.where(u == 1.0, 1.0, u - 1.0)))
    return jnp.maximum(x, 0.0) + l1p


_GELU_C = math.sqrt(2.0 / math.pi)


def _gelu(y):
    return 0.5 * y * (1.0 + jnp.tanh(_GELU_C * (y + 0.044715 * (y * y * y))))


def _dgelu(y):
    th = jnp.tanh(_GELU_C * (y + 0.044715 * (y * y * y)))
    return 0.5 * (1.0 + th) + 0.5 * y * (1.0 - th * th) * (_GELU_C * (1.0 + 3 * 0.044715 * (y * y)))


def _lru_gates(xr, prev8, cw_ref, cb, wa_ref, ba, wx_ref, bx, sp, first):
    x3, x2, x1 = _shift_down(prev8, xr, 3), _shift_down(prev8, xr, 2), _shift_down(prev8, xr, 1)
    xb = cb + x3 * cw_ref[0:1, :]
    xb = xb + x2 * cw_ref[1:2, :]
    xb = xb + x1 * cw_ref[2:3, :]
    xb = xb + xr * cw_ref[3:4, :]
    r = _sigmoid(_dot(xb, wa_ref[...], NN) + ba)
    ig = _sigmoid(_dot(xb, wx_ref[...], NN) + bx)
    la = (-LRU_C * r) * sp
    a = jnp.exp(la)
    start = jnp.logical_and(first, _iota_rows(xr.shape[0]) == 0)
    mult = jnp.where(start, 1.0, jnp.sqrt(-_expm1(2.0 * la)))
    return (x3, x2, x1), xb, r, ig, a, mult, start


def _lru_fwd(proj, cw, cb, wa, ba, wx, bx, lam, *, nb, seq, name, deps=()):
    t = proj.shape[0]
    w = cb.shape[1]
    nblk = w // HEAD
    nc = seq // CHUNK
    ng = _lane_groups(nblk, 4)

    def block(x_ref, y_ref, cw_ref, cb_ref, wa_ref, ba_ref, wx_ref, bx_ref, lam_ref, p_ref, h_ref):
        sp = _softplus_neg(lam_ref[...])
        cb_v, ba_v, bx_v = cb_ref[...], ba_ref[...], bx_ref[...]

        def tile(c, carry):
            hc, prev8 = carry
            base = pl.multiple_of(c * CHUNK, CHUNK)
            rows = pl.ds(base, CHUNK)
            xr = x_ref[rows, :]
            _, xb, _, ig, a, mult, _ = _lru_gates(xr, prev8, cw_ref, cb_v, wa_ref, ba_v, wx_ref, bx_v, sp, c == 0)
            ap, up = _scan_fwd(a, xb * ig * mult)
            h = up + ap * hc
            h_ref[rows, :] = h
            p_ref[rows, :] = h * _gelu(y_ref[rows, :])
            h_last = jnp.sum(jnp.where(_iota_rows(8) == 7, h[CHUNK - 8:], 0.0), axis=0, keepdims=True)
            return h_last, xr[CHUNK - 8:]

        return tile

    def body(*refs):
        tiles = [block(*_group_views(refs, g, "llllglgll" + "ll")) for g in range(ng)]

        def step(c, carries):
            return tuple(tile(c, carry) for tile, carry in zip(tiles, carries))

        lax.fori_loop(0, nc, step, ((jnp.zeros((1, HEAD), F32), jnp.zeros((8, HEAD), F32)),) * ng)

    gw = ng * HEAD

    def col(off):
        return pl.BlockSpec((seq, gw), lambda n, b: (b, off * (nblk // ng) + n))

    vec = lambda r: pl.BlockSpec((r, gw), lambda n, b: (0, n))
    mat = pl.BlockSpec((ng, HEAD, HEAD), lambda n, b: (n, 0, 0))
    blk = pl.BlockSpec((seq, gw), lambda n, b: (b, n))
    return _pcall(
        body, n_in=9, args=[proj, proj, cw, cb, wa, ba, wx, bx, lam], deps=deps, name=name, grid=(nblk // ng, nb),
        in_specs=[col(4), col(5), vec(4), vec(1), mat, vec(1), mat, vec(1), vec(1)],
        out_specs=[blk, blk],
        out_shape=[jax.ShapeDtypeStruct((t, w), F32), jax.ShapeDtypeStruct((t, w), F32)],
        compiler_params=_cp(dimension_semantics=("parallel", "parallel")))


def _lru_bwd(proj, cw, cb, wa, ba, wx, bx, lam, hsv, dp, *, nb, seq, name):
    t = proj.shape[0]
    w = cb.shape[1]
    nblk = w // HEAD
    nc = seq // CHUNK
    ng = _lane_groups(nblk, 4)

    def block(x_ref, y_ref, cw_ref, cb_ref, wa_ref, ba_ref, wx_ref, bx_ref, lam_ref, h_ref, dp_ref,
              dx_ref, dy_ref, small_ref, dwa_ref, dwx_ref, a_c, g_c, acc, dwa_s, dwx_s):
        bi = pl.program_id(1)
        lamv = lam_ref[...]
        sp = _softplus_neg(lamv)
        cb_v, ba_v, bx_v = cb_ref[...], ba_ref[...], bx_ref[...]

        @pl.when(bi == 0)
        def _():
            acc[...] = jnp.zeros_like(acc)
            dwa_s[...] = jnp.zeros_like(dwa_s)
            dwx_s[...] = jnp.zeros_like(dwx_s)

        def tile(it, carry):
            g_next, a_next, dxb_next8 = carry
            c = nc - 1 - it
            base = pl.multiple_of(c * CHUNK, CHUNK)
            rows = pl.ds(base, CHUNK)
            before = pl.ds(pl.multiple_of(jnp.maximum(base - 8, 0), 8), 8)
            inner = c > 0
            xr = x_ref[rows, :]
            prev8 = jnp.where(inner, x_ref[before, :], 0.0)
            (x3, x2, x1), xb, r, ig, a, mult, start = _lru_gates(
                xr, prev8, cw_ref, cb_v, wa_ref, ba_v, wx_ref, bx_v, sp, c == 0)
            h = h_ref[rows, :]
            h_m1 = _shift_down(jnp.where(inner, h_ref[before, :], 0.0), h, 1)
            yv = y_ref[rows, :]
            dpv = dp_ref[rows, :]
            dy_ref[rows, :] = (dpv * h * _dgelu(yv)).astype(BF16)
            dh = dpv * _gelu(yv)
            a_up = _shift_up(a, jnp.broadcast_to(a_next, (8, HEAD)), 1)
            ap, gp = _scan_bwd(a_up, dh)
            g = gp + ap * g_next
            a_c[...] = a
            g_c[...] = g
            da = g * h_m1
            gx = g * xb
            dxb = g * ig * mult
            dig = gx * mult
            dmult = jnp.where(start, 0.0, gx * ig)
            dla = da * a - dmult * (a * a) / mult
            dzr = (dla * (-LRU_C * sp)) * (r * (1.0 - r))
            dzi = dig * (ig * (1.0 - ig))
            dxb = dxb + _dot(dzr, wa_ref[...], NT) + _dot(dzi, wx_ref[...], NT)
            dwa_s[...] += _dot(xb, dzr, TN)
            dwx_s[...] += _dot(xb, dzi, TN)
            d1, d2, d3 = (_shift_up(dxb, dxb_next8, s) for s in (1, 2, 3))
            dx = dxb * cw_ref[3:4, :] + d1 * cw_ref[2:3, :] + d2 * cw_ref[1:2, :] + d3 * cw_ref[0:1, :]
            dx_ref[rows, :] = dx.astype(BF16)
            colsum = lambda z: jnp.sum(z, axis=0, keepdims=True)
            acc[0:1, :] += colsum(x3 * dxb)
            acc[1:2, :] += colsum(x2 * dxb)
            acc[2:3, :] += colsum(x1 * dxb)
            acc[3:4, :] += colsum(xr * dxb)
            acc[4:5, :] += colsum(dxb)
            acc[5:6, :] += colsum(dzr)
            acc[6:7, :] += colsum(dzi)
            acc[7:8, :] += colsum(dla * (-LRU_C * r))
            return g_c[0:1, :], a_c[0:1, :], dxb[0:8]

        def finish():
            @pl.when(bi == nb - 1)
            def _():
                small_ref[...] = acc[...]
                small_ref[7:8, :] = acc[7:8, :] * (-_sigmoid(-lamv))
                dwa_ref[...] = dwa_s[...]
                dwx_ref[...] = dwx_s[...]

        return tile, finish

    def body(*refs):
        blocks = [block(*_group_views(refs, g, "llllglgllll" + "lllgg" + "gglgg")) for g in range(ng)]

        def step(it, carries):
            return tuple(tile(it, carry) for (tile, _), carry in zip(blocks, carries))

        zero = jnp.zeros((1, HEAD), F32)
        lax.fori_loop(0, nc, step, ((zero, zero, jnp.zeros((8, HEAD), F32)),) * ng)
        for _, finish in blocks:
            finish()

    gw = ng * HEAD

    def col(off):
        return pl.BlockSpec((seq, gw), lambda n, b: (b, off * (nblk // ng) + n))

    vec = lambda r: pl.BlockSpec((r, gw), lambda n, b: (0, n))
    mat = pl.BlockSpec((ng, HEAD, HEAD), lambda n, b: (n, 0, 0))
    blk = pl.BlockSpec((seq, gw), lambda n, b: (b, n))
    dshape = jax.ShapeDtypeStruct((t, w), BF16)
    return pl.pallas_call(
        body, name=name, grid=(nblk // ng, nb),
        in_specs=[col(4), col(5), vec(4), vec(1), mat, vec(1), mat, vec(1), vec(1), blk, blk],
        out_specs=[blk, blk, vec(8), mat, mat],
        out_shape=[dshape, dshape, jax.ShapeDtypeStruct((8, w), F32),
                   jax.ShapeDtypeStruct((nblk, HEAD, HEAD), F32), jax.ShapeDtypeStruct((nblk, HEAD, HEAD), F32)],
        scratch_shapes=[pltpu.VMEM((ng, CHUNK, HEAD), F32)] * 2 + [pltpu.VMEM((8, gw), F32)]
        + [pltpu.VMEM((ng, HEAD, HEAD), F32)] * 2,
        compiler_params=_cp(dimension_semantics=("parallel", "arbitrary")),
    )(proj, proj, cw, cb, wa, ba, wx, bx, lam, hsv, dp)


def _ffn_conv(x, prev8, cw_ref, cbv):
    x2, x1 = _shift_down(prev8, x, 2), _shift_down(prev8, x, 1)
    y = cbv + x2 * cw_ref[0:1, :]
    y = y + x1 * cw_ref[1:2, :]
    y = y + x * cw_ref[2:3, :]
    return (x2, x1), y


def _ffn_conv_at(x_ref, base, rt, zero8, cw_ref, cbv):
    x = x_ref[base:base + rt, :]
    if base == 0:
        return _ffn_conv(x, zero8, cw_ref, cbv)[1]
    y = cbv + x_ref[base - 2:base - 2 + rt, :] * cw_ref[0:1, :]
    y = y + x_ref[base - 1:base - 1 + rt, :] * cw_ref[1:2, :]
    return y + x * cw_ref[2:3, :]


def _ffn_act_fwd(up, cw, cb, *, nb, seq, name, deps=()):
    t, f2 = up.shape
    f = f2 // 2
    tc = _tile(f, 256)
    nj = f // tc
    rt = _tile(seq, 256, 16)
    nr = seq // rt

    def body(g_ref, v_ref, cwg_ref, cwv_ref, cbg_ref, cbv_ref, o_ref):
        cbg, cbv = cbg_ref[...], cbv_ref[...]
        z = jnp.zeros((8, tc), F32)
        for c in range(nr):
            gate = _ffn_conv_at(g_ref, c * rt, rt, z, cwg_ref, cbg)
            val = _ffn_conv_at(v_ref, c * rt, rt, z, cwv_ref, cbv)
            o_ref[c * rt:(c + 1) * rt, :] = ((gate * _sigmoid(gate)) * val).astype(BF16)

    gcol = pl.BlockSpec((seq, tc), lambda j, b: (b, j))
    vcol = pl.BlockSpec((seq, tc), lambda j, b: (b, nj + j))
    gv = lambda r: pl.BlockSpec((r, tc), lambda j, b: (0, j))
    vv = lambda r: pl.BlockSpec((r, tc), lambda j, b: (0, nj + j))
    return _pcall(
        body, n_in=6, args=[up, up, cw, cw, cb, cb], deps=deps, name=name, grid=(nj, nb),
        in_specs=[gcol, vcol, gv(3), vv(3), gv(1), vv(1)], out_specs=gcol,
        out_shape=jax.ShapeDtypeStruct((t, f), BF16),
        compiler_params=_cp(dimension_semantics=("parallel", "parallel")))


def _ffn_act_bwd(dact, up, cw, cb, *, nb, seq, name, deps=()):
    t, f2 = up.shape
    f = f2 // 2
    tc = _tile(f, 256)
    nj = f // tc
    rt = _tile(seq, 256, 16)
    nr = seq // rt

    def body(da_ref, g_ref, v_ref, cwg_ref, cwv_ref, cbg_ref, cbv_ref,
             dg_ref, dv_ref, sg_ref, sv_ref, eg, ev, accg, accv):
        bi = pl.program_id(1)
        cbg, cbv = cbg_ref[...], cbv_ref[...]

        @pl.when(bi == 0)
        def _():
            accg[...] = jnp.zeros_like(accg)
            accv[...] = jnp.zeros_like(accv)

        colsum = lambda z: jnp.sum(z, axis=0, keepdims=True)

        z = jnp.zeros((8, tc), F32)
        eg[seq:seq + 8, :] = z
        ev[seq:seq + 8, :] = z

        def first(c, carry):
            pg, pv = carry
            rows = pl.ds(pl.multiple_of(c * rt, rt), rt)
            xg, xv = g_ref[rows, :], v_ref[rows, :]
            (g2, g1), gate = _ffn_conv(xg, pg, cwg_ref, cbg)
            (v2, v1), val = _ffn_conv(xv, pv, cwv_ref, cbv)
            s = _sigmoid(gate)
            da = da_ref[rows, :]
            dgate = da * val * _dsilu(gate, s)
            dval = da * (gate * s)
            eg[rows, :] = dgate
            ev[rows, :] = dval
            for acc, (s2, s1, s0), d in ((accg, (g2, g1, xg), dgate), (accv, (v2, v1, xv), dval)):
                acc[0:1, :] += colsum(s2 * d)
                acc[1:2, :] += colsum(s1 * d)
                acc[2:3, :] += colsum(s0 * d)
                acc[3:4, :] += colsum(d)
            return xg[rt - 8:], xv[rt - 8:]

        lax.fori_loop(0, nr, first, (z, z))

        for c in range(nr):
            base = c * rt
            for e, cw_ref, d_ref in ((eg, cwg_ref, dg_ref), (ev, cwv_ref, dv_ref)):
                d, d1, d2 = e[base:base + rt, :], e[base + 1:base + 1 + rt, :], e[base + 2:base + 2 + rt, :]
                d_ref[base:base + rt, :] = (
                    d * cw_ref[2:3, :] + d1 * cw_ref[1:2, :] + d2 * cw_ref[0:1, :]).astype(BF16)

        @pl.when(bi == nb - 1)
        def _():
            sg_ref[...] = accg[...]
            sv_ref[...] = accv[...]

    gcol = pl.BlockSpec((seq, tc), lambda j, b: (b, j))
    vcol = pl.BlockSpec((seq, tc), lambda j, b: (b, nj + j))
    gv = lambda r: pl.BlockSpec((r, tc), lambda j, b: (0, j))
    vv = lambda r: pl.BlockSpec((r, tc), lambda j, b: (0, nj + j))
    dshape = jax.ShapeDtypeStruct((t, f), BF16)
    sshape = jax.ShapeDtypeStruct((8, f), F32)
    return _pcall(
        body, n_in=7, args=[dact, up, up, cw, cw, cb, cb], deps=deps, name=name, grid=(nj, nb),
        in_specs=[gcol, gcol, vcol, gv(3), vv(3), gv(1), vv(1)],
        out_specs=[gcol, gcol, gv(8), gv(8)], out_shape=[dshape, dshape, sshape, sshape],
        scratch_shapes=[pltpu.VMEM((seq + 8, tc), F32)] * 2 + [pltpu.VMEM((8, tc), F32)] * 2,
        compiler_params=_cp(dimension_semantics=("parallel", "arbitrary")))


def _adamw_math(wv, g, mv, vv):
    m = ADAM_B1 * mv + (1.0 - ADAM_B1) * g
    v = ADAM_B2 * vv + (1.0 - ADAM_B2) * (g * g)
    m_hat = m / (1.0 - ADAM_B1 ** ADAM_STEP)
    v_hat = v / (1.0 - ADAM_B2 ** ADAM_STEP)
    delta = -ADAM_LR * (m_hat / (jnp.sqrt(v_hat) + ADAM_EPS) + ADAM_WD * wv)
    return delta, m, v


def _adamw_shard(wv, mv, vv, p_own, rb, *, name):
    _, r, cdim = wv.shape
    tr = _row_tile(r, cdim)

    def body(w_ref, m_ref, v_ref, p_ref, rb_ref, g_ref, d_ref, mo_ref, vo_ref):
        g = p_ref[...].astype(F32)
        for k in range(3):
            g = g + rb_ref[k].astype(F32)
        g_ref[...] = g
        d_ref[...], mo_ref[...], vo_ref[...] = _adamw_math(w_ref[...], g, m_ref[...], v_ref[...])

    row = pl.BlockSpec((tr, cdim), lambda i: (i, 0))
    row3 = pl.BlockSpec((None, tr, cdim), lambda i: (0, i, 0))
    shp = jax.ShapeDtypeStruct((1, r, cdim), F32)
    return pl.pallas_call(
        body, name=name, grid=(r // tr,),
        in_specs=[row3, row3, row3, row, pl.BlockSpec((3, tr, cdim), lambda i: (0, i, 0))],
        out_specs=[row3] * 4, out_shape=[shp] * 4, compiler_params=_cp(dimension_semantics=("parallel",)),
    )(wv, mv, vv, p_own, rb)


def _adamw_packed(wv, g, mv, vv, *, name):
    r = wv.shape[0]
    tr = _tile(r, 256, 8)

    def body(w_ref, g_ref, m_ref, v_ref, d_ref, mo_ref, vo_ref):
        d_ref[...], mo_ref[...], vo_ref[...] = _adamw_math(w_ref[...], g_ref[...], m_ref[...], v_ref[...])

    row = pl.BlockSpec((tr, LANE), lambda i: (i, 0))
    shp = jax.ShapeDtypeStruct((r, LANE), F32)
    return pl.pallas_call(
        body, name=name, grid=(r // tr,), in_specs=[row] * 4, out_specs=[row] * 3, out_shape=[shp] * 3,
        compiler_params=_cp(dimension_semantics=("parallel",)),
    )(wv, g, mv, vv)


def _place():
    return lax.axis_index("x"), lax.axis_index("y"), lax.axis_index("c")


def _all_reduce_packed(p, *, name):
    r = p.shape[0]
    tr = _tile(r, 256, 8)

    def body(p_ref, o_ref, buf, send, recv, lsem):
        x, y, c = _place()
        me, sibling = (x, y, c), (x, y, 1 - c)
        chips = [(1 - x, y), (x, 1 - y), (1 - x, 1 - y)]

        def slot(px, py, pc):
            return 4 * px + 2 * py + pc

        def copy(k, block, to, src=None):
            dst = buf.at[slot(*block)]
            return pltpu.make_async_remote_copy(
                src_ref=dst if src is None else src, dst_ref=dst, send_sem=send.at[k], recv_sem=recv.at[k],
                device_id=to, device_id_type=MESH)

        mine = pltpu.make_async_copy(p_ref, buf.at[slot(*me)], lsem)
        mine.start()
        first = [copy(0, me, sibling, src=p_ref)]
        first += [copy(1 + j, me, (*chip, c), src=p_ref) for j, chip in enumerate(chips)]
        for cp in first:
            cp.start()
        passed = [copy(4 + j, (*chip, c), sibling) for j, chip in enumerate(chips)]
        for j, chip in enumerate(chips):
            copy(1 + j, (*chip, c), me).wait_recv()
            passed[j].start()
        copy(0, sibling, me).wait_recv()
        for j, chip in enumerate(chips):
            copy(4 + j, (*chip, 1 - c), me).wait_recv()
        for cp in first + passed:
            cp.wait_send()
        mine.wait()

        def add(i, carry):
            rows = pl.ds(pl.multiple_of(i * tr, tr), tr)
            s = buf[0, rows, :]
            for d in range(1, NDEV):
                s = s + buf[d, rows, :]
            o_ref[rows, :] = s
            return carry

        lax.fori_loop(0, r // tr, add, 0)

    vm = pl.BlockSpec(memory_space=pltpu.VMEM)
    return pl.pallas_call(
        body, name=name, in_specs=[vm], out_specs=vm, out_shape=jax.ShapeDtypeStruct(p.shape, p.dtype),
        scratch_shapes=[pltpu.VMEM((NDEV,) + p.shape, p.dtype), pltpu.SemaphoreType.DMA((7,)),
                        pltpu.SemaphoreType.DMA((7,)), pltpu.SemaphoreType.DMA],
        compiler_params=_cp(),
    )(p)


HBM = pl.BlockSpec(memory_space=pltpu.HBM)
SEM = pl.BlockSpec(memory_space=pltpu.SEMAPHORE)
EFFECT = pltpu.SideEffectType.DATAFLOW_SIDE_EFFECTING


def _plan_copies(plan, s_refs, l_refs, send, recv):
    def pick(kind, a, idx):
        ref = (s_refs if kind == "s" else l_refs)[a]
        return ref if idx is None else ref.at[idx]

    return [pltpu.make_async_remote_copy(
        src_ref=pick(*src), dst_ref=pick(*dst), send_sem=send.at[i], recv_sem=recv.at[i],
        device_id=to, device_id_type=MESH) for i, (src, dst, to) in enumerate(plan(*_place()))]


def _xfer_start(srcs, lands, plan, *, name, deps=()):
    ns, nl = len(srcs), len(lands)
    nd = len(deps)
    ncopy = len(plan(0, 0, 0))

    def body(*refs):
        s_refs, l_refs = refs[:ns], refs[ns:ns + nl]
        send, recv = refs[ns + nl + nd], refs[ns + nl + nd + 1]
        token = refs[-1]
        for cp in _plan_copies(plan, s_refs, l_refs, send, recv):
            cp.start()
        token[...] = jnp.zeros_like(token)

    bufs = list(srcs) + list(lands)
    outs = pl.pallas_call(
        body, name=name,
        out_shape=(pltpu.SemaphoreType.DMA((ncopy,)), pltpu.SemaphoreType.DMA((ncopy,)),
                   *[pltpu.HBM(b.shape, b.dtype) for b in bufs], jax.ShapeDtypeStruct((8, LANE), F32)),
        in_specs=[HBM] * (ns + nl) + [ANY] * nd,
        out_specs=(SEM, SEM, *[HBM] * (ns + nl), pl.BlockSpec(memory_space=pltpu.VMEM)),
        input_output_aliases={i: 2 + i for i in range(ns + nl)},
        compiler_params=pltpu.CompilerParams(has_side_effects=EFFECT),
    )(*[pltpu.with_memory_space_constraint(b, pltpu.HBM) for b in bufs], *deps)
    return outs[0], outs[1], list(outs[2:2 + ns]), list(outs[2 + ns:2 + ns + nl]), outs[-1]


def _xfer_wait(send, recv, srcs, lands, plan, after, *, name):
    ns, nl = len(srcs), len(lands)

    def body(*refs):
        s_refs, l_refs = refs[:ns], refs[ns:ns + nl]
        send_ref, recv_ref = refs[ns + nl], refs[ns + nl + 1]
        for cp in _plan_copies(plan, s_refs, l_refs, send_ref, recv_ref):
            cp.wait_send()
            cp.wait_recv()

    bufs = list(srcs) + list(lands)
    outs = pl.pallas_call(
        body, name=name, out_shape=tuple(pltpu.HBM(b.shape, b.dtype) for b in bufs),
        in_specs=[HBM] * (ns + nl) + [SEM, SEM, ANY], out_specs=tuple([HBM] * (ns + nl)),
        input_output_aliases={i: i for i in range(ns + nl)},
        compiler_params=pltpu.CompilerParams(has_side_effects=EFFECT),
    )(*bufs, send, recv, after)
    return list(outs[:ns]), list(outs[ns:])


def _slot8(px, py, pc):
    return 4 * px + 2 * py + pc


def _slot4(r):
    return lambda px, py, pc: (2 * px + py, pl.ds(pc * r, r))


def _plan_gather_first(slots):
    def plan(x, y, c):
        peers = [(x, y, c), (x, y, 1 - c), (1 - x, y, c), (x, 1 - y, c), (1 - x, 1 - y, c)]
        return [(("s", a, None), ("l", a, slot(x, y, c)), to) for a, slot in enumerate(slots) for to in peers]
    return plan


def _plan_gather_pass(slots):
    def plan(x, y, c):
        chips = [(1 - x, y), (x, 1 - y), (1 - x, 1 - y)]
        return [(("l", a, slot(px, py, c)), ("l", a, slot(px, py, c)), (x, y, 1 - c))
                for a, slot in enumerate(slots) for px, py in chips]
    return plan


def _plan_rs_sibling(r):
    def plan(x, y, c):
        return [(("s", 0, (j, pl.ds((1 - c) * r, r))), ("l", 0, j), (x, y, 1 - c)) for j in range(4)]
    return plan


def _plan_rs_plane():
    def plan(x, y, c):
        chips = [(1 - x, y), (x, 1 - y), (1 - x, 1 - y)]
        return [(("s", 0, 2 * px + py), ("l", 0, k), (px, py, c)) for k, (px, py) in enumerate(chips)]
    return plan


def _add_pairs(g, rcv, cidx, *, name):
    _, r, cdim = rcv.shape
    tr = _row_tile(r, cdim, budget=4194304)
    per = r // tr

    def body(c_ref, g_ref, r_ref, o_ref):
        o_ref[...] = (g_ref[...].astype(F32) + r_ref[...].astype(F32)).astype(o_ref.dtype)

    grid_spec = pltpu.PrefetchScalarGridSpec(
        num_scalar_prefetch=1, grid=(4, per),
        in_specs=[pl.BlockSpec((None, tr, cdim), lambda j, i, c_ref: (j, c_ref[0] * per + i, 0)),
                  pl.BlockSpec((None, tr, cdim), lambda j, i, c_ref: (j, i, 0))],
        out_specs=pl.BlockSpec((None, tr, cdim), lambda j, i, c_ref: (j, i, 0)))
    return pl.pallas_call(
        body, name=name, grid_spec=grid_spec, out_shape=jax.ShapeDtypeStruct((4, r, cdim), g.dtype),
        compiler_params=_cp(dimension_semantics=("parallel", "parallel")),
    )(cidx, g, rcv)


def _pack(arrs):
    flat = jnp.concatenate([a.reshape(-1).astype(F32) for a in arrs])
    n = flat.shape[0]
    rows = -(-n // LANE)
    rows = -(-rows // 256) * 256
    return jnp.pad(flat, (0, rows * LANE - n)).reshape(rows, LANE)


def _unpack(packed, shapes):
    flat = packed.reshape(-1)
    out, o = [], 0
    for s in shapes:
        n = math.prod(s)
        out.append(flat[o:o + n].reshape(s))
        o += n
    return out


def _pad_blocks(a, w, wp):
    lead = a.shape[:-1]
    k = a.shape[-1] // w
    pads = [(0, 0)] * (len(lead) + 1) + [(0, wp - w)]
    return jnp.pad(a.reshape(*lead, k, w), pads).reshape(*lead, k * wp)


def kernel(x, ln1_w, w_in, lb_gamma, hg_norm_w, lru_conv_w, lru_conv_b, lru_wa, lru_ba, lru_wx, lru_bx, lru_lambda, lru_norm_w, w_out, ln2_w, ffn_w_up, ffn_conv_w, ffn_conv_b, ffn_w_down, final_norm_w, loss_target, m_ln1_w, m_w_in, m_lb_gamma, m_hg_norm_w, m_lru_conv_w, m_lru_conv_b, m_lru_wa, m_lru_ba, m_lru_wx, m_lru_bx, m_lru_lambda, m_lru_norm_w, m_w_out, m_ln2_w, m_ffn_w_up, m_ffn_conv_w, m_ffn_conv_b, m_ffn_w_down, m_final_norm_w, v_ln1_w, v_w_in, v_lb_gamma, v_hg_norm_w, v_lru_conv_w, v_lru_conv_b, v_lru_wa, v_lru_ba, v_lru_wx, v_lru_bx, v_lru_lambda, v_lru_norm_w, v_w_out, v_ln2_w, v_ffn_w_up, v_ffn_conv_w, v_ffn_conv_b, v_ffn_w_down, v_final_norm_w):
    nb, seq, d = x.shape
    t = nb * seq
    wmix = d // 2
    in_sh = w_in.shape[2]
    up_sh = ffn_w_up.shape[2]
    up_pad = -(-up_sh // LANE) * LANE
    hs = ffn_w_down.shape[1]
    fpad = 4 * up_pad
    cx, cy, cc = _place()
    me = 4 * cx + 2 * cy + cc
    plane = 2 * cx + cy

    def gather(shards, lands, slots, tag, deps=(), also=lambda x, y, c: []):
        first = _plan_gather_first(slots)
        plan = lambda x, y, c: first(x, y, c) + also(x, y, c)
        st = _xfer_start(shards, lands, plan, name=f"gather_{tag}_start", deps=deps)
        return (st, plan, slots, tag), st[4]

    def pass_on(g, after):
        st, plan, slots, tag = g
        _, lands = _xfer_wait(st[0], st[1], st[2], st[3], plan, after, name=f"gather_{tag}_wait")
        st2 = _xfer_start([], lands, _plan_gather_pass(slots), name=f"gather_{tag}_pass")
        return (st2, slots, tag), st2[4]

    def gathered(g, after):
        st2, slots, tag = g
        return _xfer_wait(st2[0], st2[1], [], st2[3], _plan_gather_pass(slots), after, name=f"gather_{tag}_done")[1]

    land8 = lambda s: lax.empty((NDEV,) + s.shape, s.dtype)

    win_s = w_in[0].astype(BF16)
    wout_s = w_out[0].astype(BF16)
    wup_s = jnp.pad(ffn_w_up[0], ((0, 0), (0, up_pad - up_sh))).astype(BF16)
    wdn_s = ffn_w_down[0].astype(BF16)
    fcw_s = jnp.pad(ffn_conv_w[0], ((0, 0), (0, up_pad - up_sh)))
    ga, tok = gather([win_s, lru_conv_w[0], fcw_s], [land8(win_s), land8(lru_conv_w[0]), land8(fcw_s)],
                     [_slot8] * 3, "a")
    go, tok = gather([wout_s], [land8(wout_s)], [_slot8], "o", deps=(tok,))
    gu, tok = gather([wup_s], [land8(wup_s)], [_slot8], "u", deps=(tok,))
    npad = up_pad - 2 * hs
    zero_pad = lambda x, y, c: [(("s", 1, None), ("l", 0, (j, pl.ds(2 * hs, npad))), (x, y, c)) for j in range(4)]
    assert npad > 0
    gd, tok = gather([wdn_s, jnp.zeros((npad, d), BF16)], [lax.empty((4, up_pad, d), BF16)], [_slot4(hs)], "d",
                     deps=(tok,), also=zero_pad)
    fcb = _pad_blocks(ffn_conv_b, up_sh, up_pad)
    wa_b, wx_b = lru_wa[0].astype(BF16), lru_wx[0].astype(BF16)

    xf = x.reshape(t, d)
    hn = _rms_fwd(xf, ln1_w, name="ln1_fwd", deps=(tok,))
    ga, tok = pass_on(ga, hn)
    win_g, lcw_g, fcw_g = gathered(ga, tok)
    lcw = lcw_g.transpose(1, 0, 2).reshape(lru_conv_w.shape[1], wmix)
    fcw = fcw_g.transpose(1, 0, 2).reshape(ffn_conv_w.shape[1], 2 * fpad)
    proj = _mm(hn, win_g, kind="nn", out_dtype=F32, name="in_proj", tm_cap=512, tk_cap=d)
    go, tok = pass_on(go, proj)
    o_hg, o_pre, states = _hgrn_fwd(proj, lb_gamma, hg_norm_w, nb=nb, seq=seq, name="hgrn_fwd")
    (wout_g,) = gathered(go, o_pre)
    wout_f = wout_g.reshape(1, d, d)
    p_lru, h_lru = _lru_fwd(proj, lcw, lru_conv_b, wa_b, lru_ba, wx_b, lru_bx, lru_lambda,
                            nb=nb, seq=seq, name="lru_fwd", deps=(tok,))
    o_lru = _rms_fwd(p_lru, lru_norm_w, name="lru_norm_fwd")
    gu, tok = pass_on(gu, o_lru)
    mix = jnp.concatenate([o_hg, o_lru], axis=1)
    h1 = _mm(mix, wout_f, kind="nn", out_dtype=F32, name="out_proj", res=xf, tm_cap=512, tk_cap=d, deps=(tok,))
    hn2 = _rms_fwd(h1, ln2_w, name="ln2_fwd")
    (wup_g,) = gathered(gu, hn2)
    up = _mm(hn2, wup_g, kind="nn", out_dtype=F32, name="ffn_up", tm_cap=512, tn_cap=up_pad // 2, tk_cap=d)
    gd, tok = pass_on(gd, up)
    act = _ffn_act_fwd(up, fcw, fcb, nb=nb, seq=seq, name="ffn_act_fwd", deps=(tok,))
    (wdn_g,) = gathered(gd, act)
    wdn_f = wdn_g.reshape(1, fpad, d)
    h2 = _mm(act, wdn_f, kind="nn", out_dtype=F32, name="ffn_down", res=h1, tm_cap=512, tn_cap=1024,
             tk_cap=2 * up_pad)

    dh2, dh2_b, g_fnw, loss_part = _loss_head(h2, final_norm_w.reshape(1, d), loss_target.reshape(t, d),
                                              name="loss_head")
    loss = lax.psum(loss_part[0, 0], ("x", "y", "c"))

    cidx = jnp.reshape(cc, (1,)).astype(jnp.int32)

    def to_sibling(g4, r, tag, deps=()):
        land = lax.empty((4, r, g4.shape[2]), g4.dtype)
        st = _xfer_start([g4], [land], _plan_rs_sibling(r), name=f"rs_sib_start_{tag}", deps=deps)
        return (st, r, tag), st[4]

    def sibling_sum(rs, after):
        st, r, tag = rs
        (g4,), (rcv,) = _xfer_wait(st[0], st[1], st[2], st[3], _plan_rs_sibling(r), after, name=f"rs_sib_wait_{tag}")
        psum = _add_pairs(g4, rcv, cidx, name=f"rs_add_{tag}")
        land = lax.empty((3,) + psum.shape[1:], psum.dtype)
        st2 = _xfer_start([psum], [land], _plan_rs_plane(), name=f"rs_plane_start_{tag}")
        return (st2, tag), st2[4]

    dact = _mm(dh2_b, wdn_f, kind="nt", out_dtype=F32, name="ffn_down_dx", tm_cap=512, tn_cap=up_pad // 2, tk_cap=d)
    g_wdn = _mm(act, dh2_b, kind="tn", out_dtype=BF16, name="ffn_down_dw", tm_cap=1024, tn_cap=1024, tk_cap=t)
    rs_d, tok = to_sibling(g_wdn.reshape(4, up_pad, d), hs, "down")
    dup_g, dup_v, s_g, s_v = _ffn_act_bwd(dact, up, fcw, fcb, nb=nb, seq=seq, name="ffn_act_bwd", deps=(tok,))
    dup = jnp.concatenate([dup_g, dup_v], axis=1)
    pl_d, tok = sibling_sum(rs_d, dup)
    dhn2 = _mm(dup, wup_g, kind="nt", out_dtype=F32, name="ffn_up_dx", tm_cap=512, tn_cap=1024, tk_cap=up_pad,
               k_blocks=2, deps=(tok,))
    g_wup = _mm(dup, hn2, kind="tn", out_dtype=BF16, name="ffn_up_dw", tm_cap=1024, tn_cap=1024, tk_cap=t)
    rs_u, tok = to_sibling(g_wup.reshape(4, 2 * up_pad, d), up_pad, "up")
    dh1, dh1_b, g_ln2 = _rms_bwd(dhn2, h1, ln2_w, name="ln2_bwd", extra=dh2, want_bf16=True, deps=(tok,))
    dmix = _mm(dh1_b, wout_f, kind="nt", out_dtype=F32, name="out_proj_dx", tm_cap=512, tn_cap=1024, tk_cap=d)
    pl_u, tok = sibling_sum(rs_u, dmix)
    g_wout = _mm(mix, dh1_b, kind="tn", out_dtype=BF16, name="out_proj_dw", tn_cap=1024, tk_cap=t, deps=(tok,))
    rs_o, tok = to_sibling(g_wout.reshape(4, d // 4, d), d // NDEV, "out")
    dp_lru, g_lnw = _rms_bwd(dmix, p_lru, lru_norm_w, name="lru_norm_bwd", dy_cb=1, deps=(tok,))
    dxr, dyr, s_lru, g_wa, g_wx = _lru_bwd(proj, lcw, lru_conv_b, wa_b, lru_ba, wx_b, lru_bx, lru_lambda,
                                           h_lru, dp_lru, nb=nb, seq=seq, name="lru_bwd")
    pl_o, tok = sibling_sum(rs_o, dxr)
    dqr, dfr, dir_, dgr, s_hg = _hgrn_bwd(proj, lb_gamma, hg_norm_w, o_pre, states, dmix,
                                          nb=nb, seq=seq, name="hgrn_bwd", deps=(tok,))
    dproj = jnp.concatenate([dqr, dfr, dir_, dgr, dxr, dyr], axis=1)
    g_win = _mm(hn, dproj, kind="tn", out_dtype=BF16, name="in_proj_dw", out_blocks=NDEV, tm_cap=512, tn_cap=in_sh,
                tk_cap=t)
    rs_i, tok = to_sibling(g_win.reshape(4, 2 * d, in_sh), d, "in")
    dhn = _mm(dproj, win_g, kind="nt", out_dtype=F32, name="in_proj_dx", tm_cap=512, tn_cap=1024, tk_cap=in_sh,
              k_blocks=4, deps=(tok,))
    grad_x, g_ln1 = _rms_bwd(dhn, xf, ln1_w, name="ln1_bwd", extra=dh1)

    fcw_parts = jnp.concatenate([s_g[0:3], s_v[0:3]], axis=1).reshape(3, NDEV, up_pad)[:, :, :up_sh]
    fcb_part = jnp.concatenate([s_g[3:4], s_v[3:4]], axis=1).reshape(NDEV, up_pad)[:, :up_sh]
    small_parts = [g_ln1, s_hg[0:2], s_hg[2:3], s_lru[0:4], s_lru[4:5], g_wa, s_lru[5:6], g_wx, s_lru[6:7],
                   s_lru[7:8], g_lnw, g_ln2, fcw_parts, fcb_part, g_fnw]
    reduced = _all_reduce_packed(_pack(small_parts), name="allreduce_small")
    pl_i, _ = sibling_sum(rs_i, reduced)
    summed = _unpack(reduced, [p.shape for p in small_parts])
    (s_ln1, s_lbg, s_hgn, s_lcw, s_lcb, s_wa, s_ba, s_wx, s_bx, s_lam, s_lnw, s_ln2, s_fcw, s_fcb, s_fnw) = summed
    sh_lcw = lru_conv_w.shape[2]
    g_small = {
        "ln1_w": s_ln1, "lb_gamma": s_lbg, "hg_norm_w": s_hgn,
        "lru_conv_w": lax.dynamic_slice_in_dim(s_lcw, me * sh_lcw, sh_lcw, axis=1),
        "lru_conv_b": s_lcb, "lru_wa": s_wa, "lru_ba": s_ba, "lru_wx": s_wx, "lru_bx": s_bx,
        "lru_lambda": s_lam, "lru_norm_w": s_lnw, "ln2_w": s_ln2,
        "ffn_conv_w": lax.dynamic_index_in_dim(s_fcw, me, 1, keepdims=False),
        "ffn_conv_b": s_fcb, "final_norm_w": s_fnw,
    }
    w_small = {"ln1_w": ln1_w, "lb_gamma": lb_gamma, "hg_norm_w": hg_norm_w, "lru_conv_w": lru_conv_w,
               "lru_conv_b": lru_conv_b, "lru_wa": lru_wa, "lru_ba": lru_ba, "lru_wx": lru_wx, "lru_bx": lru_bx,
               "lru_lambda": lru_lambda, "lru_norm_w": lru_norm_w, "ln2_w": ln2_w, "ffn_conv_w": ffn_conv_w,
               "ffn_conv_b": ffn_conv_b, "final_norm_w": final_norm_w}
    m_small = {"ln1_w": m_ln1_w, "lb_gamma": m_lb_gamma, "hg_norm_w": m_hg_norm_w, "lru_conv_w": m_lru_conv_w,
               "lru_conv_b": m_lru_conv_b, "lru_wa": m_lru_wa, "lru_ba": m_lru_ba, "lru_wx": m_lru_wx,
               "lru_bx": m_lru_bx, "lru_lambda": m_lru_lambda, "lru_norm_w": m_lru_norm_w, "ln2_w": m_ln2_w,
               "ffn_conv_w": m_ffn_conv_w, "ffn_conv_b": m_ffn_conv_b, "final_norm_w": m_final_norm_w}
    v_small = {"ln1_w": v_ln1_w, "lb_gamma": v_lb_gamma, "hg_norm_w": v_hg_norm_w, "lru_conv_w": v_lru_conv_w,
               "lru_conv_b": v_lru_conv_b, "lru_wa": v_lru_wa, "lru_ba": v_lru_ba, "lru_wx": v_lru_wx,
               "lru_bx": v_lru_bx, "lru_lambda": v_lru_lambda, "lru_norm_w": v_lru_norm_w, "ln2_w": v_ln2_w,
               "ffn_conv_w": v_ffn_conv_w, "ffn_conv_b": v_ffn_conv_b, "final_norm_w": v_final_norm_w}
    names = list(w_small)
    shapes = [w_small[k].shape for k in names]
    g_small = {k: g_small[k].reshape(w_small[k].shape) for k in names}
    packed = _adamw_packed(_pack([w_small[k] for k in names]), _pack([g_small[k] for k in names]),
                           _pack([m_small[k] for k in names]), _pack([v_small[k] for k in names]),
                           name="adamw_small")
    d_small, nm_small, nv_small = ({k: a for k, a in zip(names, _unpack(pk, shapes))} for pk in packed)

    def finish(pl_x, after, trim=lambda a: a):
        st, tag = pl_x
        (psum,), (rb,) = _xfer_wait(st[0], st[1], st[2], st[3], _plan_rs_plane(), after, name=f"rs_plane_wait_{tag}")
        return trim(lax.dynamic_index_in_dim(psum, plane, 0, keepdims=False)), trim(rb)

    upd_dn = _adamw_shard(ffn_w_down, m_ffn_w_down, v_ffn_w_down, *finish(pl_d, packed[0]), name="adamw_w_down")
    tr3 = lambda a: a.transpose(0, 2, 1)
    upd_up_t = _adamw_shard(tr3(ffn_w_up), tr3(m_ffn_w_up), tr3(v_ffn_w_up), *finish(pl_u, upd_dn[0]),
                            name="adamw_w_up")
    upd_up = [tr3(a) for a in upd_up_t]
    upd_out = _adamw_shard(w_out, m_w_out, v_w_out, *finish(pl_o, upd_up_t[0]), name="adamw_w_out")
    upd_in = _adamw_shard(w_in, m_w_in, v_w_in, *finish(pl_i, upd_out[0]), name="adamw_w_in")

    grads, deltas, new_m, new_v = dict(g_small), dict(d_small), dict(nm_small), dict(nv_small)
    for k, upd in (("w_in", upd_in), ("w_out", upd_out), ("ffn_w_up", upd_up), ("ffn_w_down", upd_dn)):
        grads[k], deltas[k], new_m[k], new_v[k] = upd
    order = ["ln1_w", "w_in", "lb_gamma", "hg_norm_w", "lru_conv_w", "lru_conv_b", "lru_wa", "lru_ba", "lru_wx",
             "lru_bx", "lru_lambda", "lru_norm_w", "w_out", "ln2_w", "ffn_w_up", "ffn_conv_w", "ffn_conv_b",
             "ffn_w_down", "final_norm_w"]
    return (loss, grad_x.reshape(nb, seq, d), *[grads[k] for k in order], *[deltas[k] for k in order],
            *[new_m[k] for k in order], *[new_v[k] for k in order])
```

```python
import math

import jax
import jax.numpy as jnp
from jax import lax
from jax.experimental import pallas as pl
from jax.experimental.pallas import tpu as pltpu

F32, BF16 = jnp.float32, jnp.bfloat16
EPS = 1e-6
HEAD = 128
CHUNK = 64
SUB = 16
NSUB = CHUNK // SUB
LRU_C = 8.0
LANE = 128
NDEV = 8
ADAM_LR, ADAM_B1, ADAM_B2, ADAM_EPS, ADAM_WD, ADAM_STEP = 0.001, 0.9, 0.999, 1e-08, 0.01, 10
MESH = pl.DeviceIdType.MESH
ANY = pl.BlockSpec(memory_space=pl.ANY)
VMEM_LIMIT = 56 * 1024 * 1024


def _cp(**kw):
    return pltpu.CompilerParams(vmem_limit_bytes=VMEM_LIMIT, **kw)


def _tile(n, cap, mult=LANE):
    best = None
    for t in range(mult, min(n, cap) + 1, mult):
        if n % t == 0:
            best = t
    return best if best is not None else n


def _row_tile(r, cdim, budget=262144):
    return _tile(r, max(16, budget // cdim), 16)


def _sigmoid(x):
    return jax.nn.sigmoid(x)


def _dsilu(x, s):
    return s * (1.0 + x * (1.0 - s))


def _iota_rows(n, w=LANE):
    return lax.broadcasted_iota(jnp.int32, (n, w), 0)


def _lane_groups(n, want=2):
    while n % want:
        want //= 2
    return want


def _group_views(refs, g, kinds):
    assert len(refs) == len(kinds)
    cols = pl.ds(g * HEAD, HEAD)
    return [r.at[:, cols] if kind == "l" else r.at[g] for r, kind in zip(refs, kinds)]


def _shift_down(prev8, xt, k):
    cat = jnp.concatenate([prev8, xt], axis=0)
    return pltpu.roll(cat, k, 0)[8:]


def _shift_up(xt, next8, k):
    cat = jnp.concatenate([xt, next8], axis=0)
    n = cat.shape[0]
    return pltpu.roll(cat, n - k, 0)[: xt.shape[0]]


def _scan_fwd(a, u):
    n = a.shape[0]
    row = _iota_rows(n, a.shape[1])
    k = 1
    while k < n:
        keep = row >= k
        a_s = jnp.where(keep, pltpu.roll(a, k, 0), 1.0)
        u_s = jnp.where(keep, pltpu.roll(u, k, 0), 0.0)
        u = a * u_s + u
        a = a * a_s
        k *= 2
    return a, u


def _scan_bwd(a, u):
    n = a.shape[0]
    row = _iota_rows(n, a.shape[1])
    k = 1
    while k < n:
        keep = row < n - k
        a_s = jnp.where(keep, pltpu.roll(a, n - k, 0), 1.0)
        u_s = jnp.where(keep, pltpu.roll(u, n - k, 0), 0.0)
        u = a * u_s + u
        a = a * a_s
        k *= 2
    return a, u


def _cumsum_fwd(u):
    n = u.shape[0]
    row = _iota_rows(n, u.shape[1])
    k = 1
    while k < n:
        u = u + jnp.where(row >= k, pltpu.roll(u, k, 0), 0.0)
        k *= 2
    return u


def _cumsum_bwd(u):
    n = u.shape[0]
    row = _iota_rows(n, u.shape[1])
    k = 1
    while k < n:
        u = u + jnp.where(row < n - k, pltpu.roll(u, n - k, 0), 0.0)
        k *= 2
    return u


def _dot(a, b, dims):
    return lax.dot_general(a.astype(BF16), b.astype(BF16), (dims, ((), ())), preferred_element_type=F32)


NN = ((1,), (0,))
NT = ((1,), (1,))
TN = ((0,), (0,))


def _pcall(body, *, n_in, in_specs, args, deps=(), **kw):
    nd = len(deps)
    if nd:
        inner = body

        def body(*refs):
            return inner(*refs[:n_in], *refs[n_in + nd:])

        in_specs = list(in_specs) + [ANY] * nd
        args = list(args) + list(deps)
    return pl.pallas_call(body, in_specs=in_specs, **kw)(*args)


def _mm(a, b3, *, kind, out_dtype, name, res=None, tm_cap=1024, tn_cap=1536, tk_cap=1024, out_blocks=1, k_blocks=1,
        deps=()):
    if kind == "nn":
        m, kdim = a.shape
        nb, _, nsh = b3.shape
        n = nb * nsh
        tm, tn, tk = _tile(m, tm_cap), _tile(nsh, tn_cap), _tile(kdim, tk_cap)
        per = nsh // tn
        a_blk, a_ix = (tm, tk), lambda i, j, k: (i, k)
        b_blk, b_ix = (None, tk, tn), lambda i, j, k: (j // per, k, j % per)
        dims = NN
    elif kind == "nt":
        m, kdim = a.shape[-2], a.shape[-1] * (a.shape[0] if a.ndim == 3 else 1)
        nb, n, ksh = b3.shape
        tm, tn, tk = _tile(m, tm_cap), _tile(n, tn_cap), _tile(ksh, tk_cap)
        per = ksh // tk
        a_blk, a_ix = (tm, tk), lambda i, j, k: (i, k)
        b_blk, b_ix = (None, tn, tk), lambda i, j, k: (k // per, j, k % per)
        if k_blocks > 1:
            assert tk == ksh and nb % k_blocks == 0
            tk = k_blocks * ksh
            a_blk, b_blk, b_ix = (tm, tk), (k_blocks, tn, ksh), lambda i, j, k: (k, j, 0)
        if a.ndim == 3:
            aper = a.shape[2] // tk
            a_blk, a_ix = (None,) + a_blk, lambda i, j, k: (k // aper, i, k % aper)
        dims = NT
    else:
        kdim, m = a.shape[-2], a.shape[-1] * (a.shape[0] if a.ndim == 3 else 1)
        n = b3.shape[1]
        nsh = n // out_blocks
        tm, tn, tk = _tile(a.shape[-1], tm_cap), _tile(nsh, tn_cap), _tile(kdim, tk_cap)
        per = nsh // tn
        a_blk, a_ix = (tk, tm), lambda i, j, k: (k, i)
        b_blk, b_ix = (tk, tn), lambda i, j, k: (k, j)
        if a.ndim == 3:
            aper = a.shape[2] // tm
            a_blk, a_ix = (None,) + a_blk, lambda i, j, k: (i // aper, k, i % aper)
        dims = TN
    nk = kdim // tk
    j_outer = nk == 1
    grid = (n // tn, m // tm, nk) if j_outer else (m // tm, n // tn, nk)
    at = (lambda f: lambda g0, g1, k: f(g1, g0, k)) if j_outer else (lambda f: f)
    a_spec, b_spec = pl.BlockSpec(a_blk, at(a_ix)), pl.BlockSpec(b_blk, at(b_ix))

    def body(*refs):
        a_ref, b_ref = refs[:2]
        r_ref = refs[2] if res is not None else None
        o_ref = refs[3] if res is not None else refs[2]
        if k_blocks > 1:
            w = b_ref.shape[2]
            part = sum(lax.dot_general(a_ref[:, s * w:(s + 1) * w], b_ref[s], (dims, ((), ())),
                                       preferred_element_type=F32) for s in range(k_blocks))
        else:
            part = lax.dot_general(a_ref[...], b_ref[...], (dims, ((), ())), preferred_element_type=F32)
        if nk == 1:
            o_ref[...] = (part if res is None else part + r_ref[...]).astype(o_ref.dtype)
            return
        acc = refs[-1]
        k = pl.program_id(2)

        @pl.when(k == 0)
        def _():
            acc[...] = part

        @pl.when(k > 0)
        def _():
            acc[...] += part

        @pl.when(k == nk - 1)
        def _():
            r = acc[...]
            if res is not None:
                r = r + r_ref[...]
            o_ref[...] = r.astype(o_ref.dtype)

    in_specs = [a_spec, b_spec]
    args = [a, b3]
    if res is not None:
        in_specs.append(pl.BlockSpec((tm, tn), at(lambda i, j, k: (i, j))))
        args.append(res)
    if kind == "tn":
        out_shape = jax.ShapeDtypeStruct((out_blocks, m, nsh), out_dtype)
        out_spec = pl.BlockSpec((None, tm, tn), at(lambda i, j, k: (j // per, i, j % per)))
    else:
        out_shape = jax.ShapeDtypeStruct((m, n), out_dtype)
        out_spec = pl.BlockSpec((tm, tn), at(lambda i, j, k: (i, j)))
    return _pcall(
        body, n_in=len(args), in_specs=in_specs, args=args, deps=deps,
        name=name, grid=grid, out_specs=out_spec, out_shape=out_shape,
        scratch_shapes=[pltpu.VMEM((tm, tn), F32)] if nk > 1 else [],
        compiler_params=_cp(dimension_semantics=("parallel", "parallel", "arbitrary")))


def _rms_fwd(x, w, *, name, tm=256, deps=()):
    t, d = x.shape
    tm = _tile(t, tm, 16)

    def body(x_ref, w_ref, o_ref):
        xv = x_ref[...]
        r = lax.rsqrt(jnp.mean(xv * xv, axis=-1, keepdims=True) + EPS)
        o_ref[...] = ((xv * r) * w_ref[...]).astype(o_ref.dtype)

    return _pcall(
        body, n_in=2, args=[x, w], deps=deps, name=name, grid=(t // tm,),
        in_specs=[pl.BlockSpec((tm, d), lambda i: (i, 0)), pl.BlockSpec((1, d), lambda i: (0, 0))],
        out_specs=pl.BlockSpec((tm, d), lambda i: (i, 0)),
        out_shape=jax.ShapeDtypeStruct((t, d), BF16), compiler_params=_cp())


def _rms_bwd(dy, x, w, *, name, extra=None, dy_cb=0, want_bf16=False, tm=256, deps=()):
    t, d = x.shape
    tm = _tile(t, tm, 16)

    def body(*refs):
        refs = list(refs)
        dy_ref, x_ref, w_ref = refs[:3]
        e_ref = refs[3] if extra is not None else None
        outs = refs[4:] if extra is not None else refs[3:]
        dx_ref = outs[0]
        dxb_ref = outs[1] if want_bf16 else None
        dw_ref = outs[-1]
        i = pl.program_id(0)
        xv = x_ref[...]
        r = lax.rsqrt(jnp.mean(xv * xv, axis=-1, keepdims=True) + EPS)
        nh = xv * r
        dyv = dy_ref[...]
        dn = dyv * w_ref[...]
        dx = r * (dn - nh * jnp.mean(dn * nh, axis=-1, keepdims=True))
        if extra is not None:
            dx = dx + e_ref[...]
        dx_ref[...] = dx
        if want_bf16:
            dxb_ref[...] = dx.astype(BF16)
        part = jnp.sum(dyv * nh, axis=0, keepdims=True)

        @pl.when(i == 0)
        def _():
            dw_ref[...] = part

        @pl.when(i > 0)
        def _():
            dw_ref[...] += part

    row = pl.BlockSpec((tm, d), lambda i: (i, 0))
    in_specs = [pl.BlockSpec((tm, d), lambda i: (i, dy_cb)), row, pl.BlockSpec((1, d), lambda i: (0, 0))]
    args = [dy, x, w]
    if extra is not None:
        in_specs.append(row)
        args.append(extra)
    out_shape = [jax.ShapeDtypeStruct((t, d), F32)]
    out_specs = [row]
    if want_bf16:
        out_shape.append(jax.ShapeDtypeStruct((t, d), BF16))
        out_specs.append(row)
    out_shape.append(jax.ShapeDtypeStruct((1, d), F32))
    out_specs.append(pl.BlockSpec((1, d), lambda i: (0, 0)))
    return _pcall(
        body, n_in=len(args), in_specs=in_specs, args=args, deps=deps,
        name=name, grid=(t // tm,), out_specs=out_specs, out_shape=out_shape,
        compiler_params=_cp(dimension_semantics=("arbitrary",)))


def _loss_head(h, w, tgt, *, name, tm=256):
    t, d = h.shape
    tm = _tile(t, tm, 16)

    def body(h_ref, w_ref, t_ref, dh_ref, dhb_ref, dw_ref, loss_ref):
        i = pl.program_id(0)
        xv = h_ref[...]
        wv = w_ref[...]
        r = lax.rsqrt(jnp.mean(xv * xv, axis=-1, keepdims=True) + EPS)
        nh = xv * r
        e = nh * wv - t_ref[...]
        part_loss = jnp.full((1, LANE), 0.5 * jnp.sum(jnp.mean(e * e, axis=-1, keepdims=True)), F32)
        dyv = e * (1.0 / d)
        dn = dyv * wv
        dx = r * (dn - nh * jnp.mean(dn * nh, axis=-1, keepdims=True))
        dh_ref[...] = dx
        dhb_ref[...] = dx.astype(BF16)
        part = jnp.sum(dyv * nh, axis=0, keepdims=True)

        @pl.when(i == 0)
        def _():
            dw_ref[...] = part
            loss_ref[...] = part_loss

        @pl.when(i > 0)
        def _():
            dw_ref[...] += part
            loss_ref[...] += part_loss

    row = pl.BlockSpec((tm, d), lambda i: (i, 0))
    vec = pl.BlockSpec((1, d), lambda i: (0, 0))
    return pl.pallas_call(
        body, name=name, grid=(t // tm,), in_specs=[row, vec, row],
        out_specs=[row, row, vec, pl.BlockSpec((1, LANE), lambda i: (0, 0))],
        out_shape=[jax.ShapeDtypeStruct((t, d), F32), jax.ShapeDtypeStruct((t, d), BF16),
                   jax.ShapeDtypeStruct((1, d), F32), jax.ShapeDtypeStruct((1, LANE), F32)],
        compiler_params=_cp(dimension_semantics=("arbitrary",)),
    )(h, w, tgt)


def _lower_bound(lbg_ref):
    g0, g1 = lbg_ref[0:1, :], lbg_ref[1:2, :]
    m = jnp.maximum(g0, g1)
    e0, e1 = jnp.exp(g0 - m), jnp.exp(g1 - m)
    return e0 / (e0 + e1)


def _seg_bounds():
    offs, o = {}, 0
    for i in range(1, NSUB):
        offs[i] = (o, o + SUB * i)
        o += SUB * i
    return offs, o


def _pad_rows(x, n):
    if x.shape[0] == n:
        return x
    return jnp.concatenate([x, jnp.zeros((n - x.shape[0], x.shape[1]), x.dtype)], axis=0)


def _offdiag_mask():
    offs, total = _seg_bounds()
    padded = -(-total // LANE) * LANE
    rsub = lax.broadcasted_iota(jnp.int32, (CHUNK, padded), 0) // SUB
    col = lax.broadcasted_iota(jnp.int32, (CHUNK, padded), 1)
    cseg = jnp.zeros((CHUNK, padded), jnp.int32)
    for i in range(1, NSUB):
        cseg = cseg + (col >= offs[i][0]).astype(jnp.int32)
    return (rsub == cseg) & (col < total)


def _offdiag_setup(q, k, b, v, b_c, mask):
    offs, total = _seg_bounds()
    padded = -(-total // LANE) * LANE
    eq_parts = [jnp.zeros((SUB, HEAD), F32)]
    ek_parts, k_parts, v_parts = [], [], []
    for i in range(1, NSUB):
        r_i = b_c[SUB * i - 1:SUB * i, :]
        eq_parts.append(jnp.exp(b[SUB * i:SUB * (i + 1)] - r_i))
        ek_parts.append(jnp.exp(r_i - b[0:SUB * i]))
        k_parts.append(k[0:SUB * i])
        v_parts.append(v[0:SUB * i])
    eq = jnp.concatenate(eq_parts, axis=0)
    ek = _pad_rows(jnp.concatenate(ek_parts, axis=0), padded)
    kt = _pad_rows(jnp.concatenate(k_parts, axis=0), padded) * ek
    vs = _pad_rows(jnp.concatenate(v_parts, axis=0), padded)
    qt = q * eq
    a = jnp.where(mask, _dot(qt, kt, NT), 0.0)
    return offs, eq, ek, kt, vs, qt, a


def _hgrn_fwd(proj, lbg, nw, *, nb, seq, name):
    t = proj.shape[0]
    w = lbg.shape[1]
    nh = w // HEAD
    nc = seq // CHUNK
    ng = _lane_groups(nh)

    def head(mask, q_ref, f_ref, i_ref, g_ref, lbg_ref, nw_ref, ohg_ref, opre_ref, st_ref, k_c, b_c, v_c, st):
        lb = _lower_bound(lbg_ref)
        nwv = nw_ref[...]
        st[...] = jnp.zeros_like(st)

        def chunk(c, carry):
            rows = pl.ds(pl.multiple_of(c * CHUNK, CHUNK), CHUNK)
            qr = q_ref[rows, :]
            q = qr * _sigmoid(qr)
            f = lb + (1.0 - lb) * _sigmoid(f_ref[rows, :])
            k = 1.0 - f
            b = _cumsum_fwd(jnp.log(f))
            v = i_ref[rows, :]
            k_c[...] = k
            b_c[...] = b
            v_c[...] = v
            s_t = st[...]
            st_ref[c] = s_t
            o = _dot(q * jnp.exp(b), s_t, NT)
            _, _, _, _, vs, _, a = _offdiag_setup(q, k, b, v, b_c, mask)
            o = o + _dot(a, vs, NN)
            diag = []
            for i in range(NSUB):
                accs = [jnp.zeros((8, HEAD), F32) for _ in range(SUB // 8)]
                for j in range(SUB):
                    r = SUB * i + j
                    bj, kj, vj = b_c[r:r + 1, :], k_c[r:r + 1, :], v_c[r:r + 1, :]
                    for p in range(j // 8, SUB // 8):
                        lo = SUB * i + 8 * p
                        d = jnp.exp(b[lo:lo + 8] - bj)
                        if 8 * p < j:
                            d = jnp.where(_iota_rows(8) + 8 * p >= j, d, 0.0)
                        s = jnp.sum(q[lo:lo + 8] * d * kj, axis=-1, keepdims=True)
                        accs[p] = accs[p] + s * vj
                diag.extend(accs)
            o = o + jnp.concatenate(diag, axis=0)
            bl = b_c[CHUNK - 1:CHUNK, :]
            kb = k * jnp.exp(bl - b)
            st[...] = s_t * jnp.exp(bl) + _dot(v, kb, TN)
            opre_ref[rows, :] = o
            rn = lax.rsqrt(jnp.mean(o * o, axis=-1, keepdims=True) + EPS)
            gr = g_ref[rows, :]
            ohg_ref[rows, :] = (((o * rn) * nwv) * (gr * _sigmoid(gr))).astype(BF16)
            return carry

        return chunk

    def body(*refs):
        mask = _offdiag_mask()
        chunks = [head(mask, *_group_views(refs, g, "llllllll" + "ggggg")) for g in range(ng)]

        def step(c, carry):
            for chunk in chunks:
                chunk(c, carry)
            return carry

        lax.fori_loop(0, nc, step, 0)

    gw = ng * HEAD

    def col(off):
        return pl.BlockSpec((seq, gw), lambda h, b: (b, off * (nh // ng) + h))

    vec = lambda r: pl.BlockSpec((r, gw), lambda h, b: (0, h))
    out_blk = pl.BlockSpec((seq, gw), lambda h, b: (b, h))
    return pl.pallas_call(
        body, name=name, grid=(nh // ng, nb),
        in_specs=[col(0), col(1), col(2), col(3), vec(2), vec(1)],
        out_specs=[out_blk, out_blk, pl.BlockSpec((None, ng, nc, HEAD, HEAD), lambda h, b: (b, h, 0, 0, 0))],
        out_shape=[jax.ShapeDtypeStruct((t, w), BF16), jax.ShapeDtypeStruct((t, w), F32),
                   jax.ShapeDtypeStruct((nb, nh, nc, HEAD, HEAD), F32)],
        scratch_shapes=[pltpu.VMEM((ng, CHUNK, HEAD), F32)] * 3 + [pltpu.VMEM((ng, HEAD, HEAD), F32)],
        compiler_params=_cp(dimension_semantics=("parallel", "parallel")),
    )(proj, proj, proj, proj, lbg, nw)


def _hgrn_bwd(proj, lbg, nw, opre, states, dmix, *, nb, seq, name, deps=()):
    t = proj.shape[0]
    w = lbg.shape[1]
    nh = w // HEAD
    nc = seq // CHUNK
    ng = _lane_groups(nh)

    def head(mask, q_ref, f_ref, i_ref, g_ref, lbg_ref, nw_ref, opre_ref, st_ref, dm_ref,
             dq_ref, df_ref, di_ref, dg_ref, small_ref, k_c, b_c, v_c, dst, dlb_s, dwn_s):
        bi = pl.program_id(1)
        lb = _lower_bound(lbg_ref)
        nwv = nw_ref[...]
        dst[...] = jnp.zeros_like(dst)

        @pl.when(bi == 0)
        def _():
            dlb_s[...] = jnp.zeros_like(dlb_s)
            dwn_s[...] = jnp.zeros_like(dwn_s)

        def chunk(it, carry):
            c = nc - 1 - it
            rows = pl.ds(pl.multiple_of(c * CHUNK, CHUNK), CHUNK)
            qr = q_ref[rows, :]
            sq = _sigmoid(qr)
            q = qr * sq
            sg = _sigmoid(f_ref[rows, :])
            f = lb + (1.0 - lb) * sg
            k = 1.0 - f
            b = _cumsum_fwd(jnp.log(f))
            v = i_ref[rows, :]
            k_c[...] = k
            b_c[...] = b
            v_c[...] = v
            o = opre_ref[rows, :]
            rn = lax.rsqrt(jnp.mean(o * o, axis=-1, keepdims=True) + EPS)
            nhat = o * rn
            dm = dm_ref[rows, :]
            gr = g_ref[rows, :]
            sgr = _sigmoid(gr)
            dnw = dm * (gr * sgr)
            dg_ref[rows, :] = (dm * (nhat * nwv) * _dsilu(gr, sgr)).astype(BF16)
            dwn_s[...] += jnp.sum(dnw * nhat, axis=0, keepdims=True)
            dn = dnw * nwv
            do = rn * (dn - nhat * jnp.mean(dn * nhat, axis=-1, keepdims=True))
            s_t = st_ref[c]
            ds = dst[...]
            eb = jnp.exp(b)
            qb = q * eb
            bl = b_c[CHUNK - 1:CHUNK, :]
            ebl = jnp.exp(bl)
            kdec = jnp.exp(bl - b)
            kb = k * kdec
            dqb = _dot(do, s_t, NN)
            dkb = _dot(v, ds, NN)
            dv = _dot(kb, ds, NT)
            d_ebl = jnp.sum(ds * s_t, axis=0, keepdims=True)
            dst[...] = ds * ebl + _dot(do, qb, TN)
            dq = dqb * eb
            dk = dkb * kdec
            t_kb = dkb * kb
            db = dqb * qb - t_kb
            db_last = jnp.sum(t_kb, axis=0, keepdims=True) + d_ebl * ebl
            offs, eq, ek, kt, vs, qt, a = _offdiag_setup(q, k, b, v, b_c, mask)
            da = jnp.where(mask, _dot(do, vs, NT), 0.0)
            dvs = _dot(a, do, TN)
            dqt = _dot(da, kt, NN)
            dkt = _dot(da, qt, TN)
            dq = dq + dqt * eq
            db = db + dqt * qt
            gk, gb = dkt * ek, dkt * kt
            zero8 = jnp.zeros((8, HEAD), F32)
            off_k, off_v, off_b = ([zero8] * (CHUNK // 8) for _ in range(3))
            for i in range(1, NSUB):
                lo, hi = offs[i]
                for p in range((hi - lo) // 8):
                    rows8 = slice(lo + 8 * p, lo + 8 * p + 8)
                    off_k[p] = off_k[p] + gk[rows8]
                    off_b[p] = off_b[p] - gb[rows8]
                    off_v[p] = off_v[p] + dvs[rows8]
            npv = SUB // 8
            dq_d, dk_d, dv_d, db_d = [], [], [], []
            for i in range(NSUB):
                aq = [zero8 for _ in range(npv)]
                ak = [zero8 for _ in range(npv)]
                av = [off_v[i * npv + p] for p in range(npv)]
                for j in range(SUB):
                    r = SUB * i + j
                    bj, kj, vj = b_c[r:r + 1, :], k_c[r:r + 1, :], v_c[r:r + 1, :]
                    pm_sum, ad_sum = None, None
                    for p in range(j // 8, npv):
                        lo = SUB * i + 8 * p
                        d = jnp.exp(b[lo:lo + 8] - bj)
                        if 8 * p < j:
                            d = jnp.where(_iota_rows(8) + 8 * p >= j, d, 0.0)
                        qd = q[lo:lo + 8] * d
                        dop = do[lo:lo + 8]
                        a_j = jnp.sum(qd * kj, axis=-1, keepdims=True)
                        da_j = jnp.sum(dop * vj, axis=-1, keepdims=True)
                        aq[p] = aq[p] + (da_j * d) * kj
                        pm, ad = da_j * qd, a_j * dop
                        pm_sum = pm if pm_sum is None else pm_sum + pm
                        ad_sum = ad if ad_sum is None else ad_sum + ad
                    pj = j // 8
                    here = _iota_rows(8) == (j - 8 * pj)
                    ak[pj] = ak[pj] + jnp.where(here, jnp.sum(pm_sum, axis=0, keepdims=True), 0.0)
                    av[pj] = av[pj] + jnp.where(here, jnp.sum(ad_sum, axis=0, keepdims=True), 0.0)
                for p in range(npv):
                    lo = SUB * i + 8 * p
                    db_d.append(off_b[i * npv + p] + q[lo:lo + 8] * aq[p] - k[lo:lo + 8] * ak[p])
                    dk_d.append(off_k[i * npv + p] + ak[p])
                dq_d.extend(aq)
                dv_d.extend(av)
            dq = dq + jnp.concatenate(dq_d, axis=0)
            dk = dk + jnp.concatenate(dk_d, axis=0)
            dv = dv + jnp.concatenate(dv_d, axis=0)
            db = db + jnp.concatenate(db_d, axis=0)
            db = db + jnp.where(_iota_rows(CHUNK) == CHUNK - 1, db_last, 0.0)
            dgl = _cumsum_bwd(db)
            dfv = dgl / f - dk
            dlb_s[...] += jnp.sum(dfv * (1.0 - sg), axis=0, keepdims=True)
            df_ref[rows, :] = (dfv * (1.0 - lb) * (sg * (1.0 - sg))).astype(BF16)
            dq_ref[rows, :] = (dq * _dsilu(qr, sq)).astype(BF16)
            di_ref[rows, :] = dv.astype(BF16)
            return carry

        def finish():
            @pl.when(bi == nb - 1)
            def _():
                dgam = dlb_s[...] * lb * (1.0 - lb)
                small_ref[...] = jnp.zeros_like(small_ref)
                small_ref[0:1, :] = dgam
                small_ref[1:2, :] = -dgam
                small_ref[2:3, :] = dwn_s[...]

        return chunk, finish

    def body(*refs):
        mask = _offdiag_mask()
        heads = [head(mask, *_group_views(refs, g, "lllllllgl" + "lllll" + "ggggll")) for g in range(ng)]

        def step(it, carry):
            for chunk, _ in heads:
                chunk(it, carry)
            return carry

        lax.fori_loop(0, nc, step, 0)
        for _, finish in heads:
            finish()

    gw = ng * HEAD

    def col(off):
        return pl.BlockSpec((seq, gw), lambda h, b: (b, off * (nh // ng) + h))

    vec = lambda r: pl.BlockSpec((r, gw), lambda h, b: (0, h))
    blk = pl.BlockSpec((seq, gw), lambda h, b: (b, h))
    dshape = jax.ShapeDtypeStruct((t, w), BF16)
    return _pcall(
        body, n_in=9, args=[proj, proj, proj, proj, lbg, nw, opre, states, dmix], deps=deps,
        name=name, grid=(nh // ng, nb),
        in_specs=[col(0), col(1), col(2), col(3), vec(2), vec(1), blk,
                  pl.BlockSpec((None, ng, nc, HEAD, HEAD), lambda h, b: (b, h, 0, 0, 0)), blk],
        out_specs=[blk, blk, blk, blk, vec(8)],
        out_shape=[dshape, dshape, dshape, dshape, jax.ShapeDtypeStruct((8, w), F32)],
        scratch_shapes=[pltpu.VMEM((ng, CHUNK, HEAD), F32)] * 3 + [pltpu.VMEM((ng, HEAD, HEAD), F32)]
        + [pltpu.VMEM((1, gw), F32)] * 2,
        compiler_params=_cp(dimension_semantics=("parallel", "arbitrary")))


def _expm1(x):
    poly = x * (1.0 + x * (0.5 + x * (1.0 / 6 + x * (1.0 / 24 + x * (1.0 / 120 + x * (1.0 / 720))))))
    return jnp.where(jnp.abs(x) < 0.25, poly, jnp.exp(x) - 1.0)


def _softplus_neg(lam):
    x = -lam
    e = jnp.exp(-jnp.abs(x))
    u = 1.0 + e
    l1p = jnp.where(u == 1.0, e, jnp.log(u) * (e / jnp.where(u == 1.0, 1.0, u - 1.0)))
    return jnp.maximum(x, 0.0) + l1p


_GELU_C = math.sqrt(2.0 / math.pi)


def _gelu(y):
    return 0.5 * y * (1.0 + jnp.tanh(_GELU_C * (y + 0.044715 * (y * y * y))))


def _dgelu(y):
    th = jnp.tanh(_GELU_C * (y + 0.044715 * (y * y * y)))
    return 0.5 * (1.0 + th) + 0.5 * y * (1.0 - th * th) * (_GELU_C * (1.0 + 3 * 0.044715 * (y * y)))


def _lru_gates(xr, prev8, cw_ref, cb, wa_ref, ba, wx_ref, bx, sp, first):
    x3, x2, x1 = _shift_down(prev8, xr, 3), _shift_down(prev8, xr, 2), _shift_down(prev8, xr, 1)
    xb = cb + x3 * cw_ref[0:1, :]
    xb = xb + x2 * cw_ref[1:2, :]
    xb = xb + x1 * cw_ref[2:3, :]
    xb = xb + xr * cw_ref[3:4, :]
    r = _sigmoid(_dot(xb, wa_ref[...], NN) + ba)
    ig = _sigmoid(_dot(xb, wx_ref[...], NN) + bx)
    la = (-LRU_C * r) * sp
    a = jnp.exp(la)
    start = jnp.logical_and(first, _iota_rows(xr.shape[0]) == 0)
    mult = jnp.where(start, 1.0, jnp.sqrt(-_expm1(2.0 * la)))
    return (x3, x2, x1), xb, r, ig, a, mult, start


def _lru_fwd(proj, cw, cb, wa, ba, wx, bx, lam, *, nb, seq, name, deps=()):
    t = proj.shape[0]
    w = cb.shape[1]
    nblk = w // HEAD
    nc = seq // CHUNK
    ng = _lane_groups(nblk, 4)

    def block(x_ref, y_ref, cw_ref, cb_ref, wa_ref, ba_ref, wx_ref, bx_ref, lam_ref, p_ref, h_ref):
        sp = _softplus_neg(lam_ref[...])
        cb_v, ba_v, bx_v = cb_ref[...], ba_ref[...], bx_ref[...]

        def tile(c, carry):
            hc, prev8 = carry
            base = pl.multiple_of(c * CHUNK, CHUNK)
            rows = pl.ds(base, CHUNK)
            xr = x_ref[rows, :]
            _, xb, _, ig, a, mult, _ = _lru_gates(xr, prev8, cw_ref, cb_v, wa_ref, ba_v, wx_ref, bx_v, sp, c == 0)
            ap, up = _scan_fwd(a, xb * ig * mult)
            h = up + ap * hc
            h_ref[rows, :] = h
            p_ref[rows, :] = h * _gelu(y_ref[rows, :])
            h_last = jnp.sum(jnp.where(_iota_rows(8) == 7, h[CHUNK - 8:], 0.0), axis=0, keepdims=True)
            return h_last, xr[CHUNK - 8:]

        return tile

    def body(*refs):
        tiles = [block(*_group_views(refs, g, "llllglgll" + "ll")) for g in range(ng)]

        def step(c, carries):
            return tuple(tile(c, carry) for tile, carry in zip(tiles, carries))

        lax.fori_loop(0, nc, step, ((jnp.zeros((1, HEAD), F32), jnp.zeros((8, HEAD), F32)),) * ng)

    gw = ng * HEAD

    def col(off):
        return pl.BlockSpec((seq, gw), lambda n, b: (b, off * (nblk // ng) + n))

    vec = lambda r: pl.BlockSpec((r, gw), lambda n, b: (0, n))
    mat = pl.BlockSpec((ng, HEAD, HEAD), lambda n, b: (n, 0, 0))
    blk = pl.BlockSpec((seq, gw), lambda n, b: (b, n))
    return _pcall(
        body, n_in=9, args=[proj, proj, cw, cb, wa, ba, wx, bx, lam], deps=deps, name=name, grid=(nblk // ng, nb),
        in_specs=[col(4), col(5), vec(4), vec(1), mat, vec(1), mat, vec(1), vec(1)],
        out_specs=[blk, blk],
        out_shape=[jax.ShapeDtypeStruct((t, w), F32), jax.ShapeDtypeStruct((t, w), F32)],
        compiler_params=_cp(dimension_semantics=("parallel", "parallel")))


def _lru_bwd(proj, cw, cb, wa, ba, wx, bx, lam, hsv, dp, *, nb, seq, name):
    t = proj.shape[0]
    w = cb.shape[1]
    nblk = w // HEAD
    nc = seq // CHUNK
    ng = _lane_groups(nblk, 4)

    def block(x_ref, y_ref, cw_ref, cb_ref, wa_ref, ba_ref, wx_ref, bx_ref, lam_ref, h_ref, dp_ref,
              dx_ref, dy_ref, small_ref, dwa_ref, dwx_ref, a_c, g_c, acc, dwa_s, dwx_s):
        bi = pl.program_id(1)
        lamv = lam_ref[...]
        sp = _softplus_neg(lamv)
        cb_v, ba_v, bx_v = cb_ref[...], ba_ref[...], bx_ref[...]

        @pl.when(bi == 0)
        def _():
            acc[...] = jnp.zeros_like(acc)
            dwa_s[...] = jnp.zeros_like(dwa_s)
            dwx_s[...] = jnp.zeros_like(dwx_s)

        def tile(it, carry):
            g_next, a_next, dxb_next8 = carry
            c = nc - 1 - it
            base = pl.multiple_of(c * CHUNK, CHUNK)
            rows = pl.ds(base, CHUNK)
            before = pl.ds(pl.multiple_of(jnp.maximum(base - 8, 0), 8), 8)
            inner = c > 0
            xr = x_ref[rows, :]
            prev8 = jnp.where(inner, x_ref[before, :], 0.0)
            (x3, x2, x1), xb, r, ig, a, mult, start = _lru_gates(
                xr, prev8, cw_ref, cb_v, wa_ref, ba_v, wx_ref, bx_v, sp, c == 0)
            h = h_ref[rows, :]
            h_m1 = _shift_down(jnp.where(inner, h_ref[before, :], 0.0), h, 1)
            yv = y_ref[rows, :]
            dpv = dp_ref[rows, :]
            dy_ref[rows, :] = (dpv * h * _dgelu(yv)).astype(BF16)
            dh = dpv * _gelu(yv)
            a_up = _shift_up(a, jnp.broadcast_to(a_next, (8, HEAD)), 1)
            ap, gp = _scan_bwd(a_up, dh)
            g = gp + ap * g_next
            a_c[...] = a
            g_c[...] = g
            da = g * h_m1
            gx = g * xb
            dxb = g * ig * mult
            dig = gx * mult
            dmult = jnp.where(start, 0.0, gx * ig)
            dla = da * a - dmult * (a * a) / mult
            dzr = (dla * (-LRU_C * sp)) * (r * (1.0 - r))
            dzi = dig * (ig * (1.0 - ig))
            dxb = dxb + _dot(dzr, wa_ref[...], NT) + _dot(dzi, wx_ref[...], NT)
            dwa_s[...] += _dot(xb, dzr, TN)
            dwx_s[...] += _dot(xb, dzi, TN)
            d1, d2, d3 = (_shift_up(dxb, dxb_next8, s) for s in (1, 2, 3))
            dx = dxb * cw_ref[3:4, :] + d1 * cw_ref[2:3, :] + d2 * cw_ref[1:2, :] + d3 * cw_ref[0:1, :]
            dx_ref[rows, :] = dx.astype(BF16)
            colsum = lambda z: jnp.sum(z, axis=0, keepdims=True)
            acc[0:1, :] += colsum(x3 * dxb)
            acc[1:2, :] += colsum(x2 * dxb)
            acc[2:3, :] += colsum(x1 * dxb)
            acc[3:4, :] += colsum(xr * dxb)
            acc[4:5, :] += colsum(dxb)
            acc[5:6, :] += colsum(dzr)
            acc[6:7, :] += colsum(dzi)
            acc[7:8, :] += colsum(dla * (-LRU_C * r))
            return g_c[0:1, :], a_c[0:1, :], dxb[0:8]

        def finish():
            @pl.when(bi == nb - 1)
            def _():
                small_ref[...] = acc[...]
                small_ref[7:8, :] = acc[7:8, :] * (-_sigmoid(-lamv))
                dwa_ref[...] = dwa_s[...]
                dwx_ref[...] = dwx_s[...]

        return tile, finish

    def body(*refs):
        blocks = [block(*_group_views(refs, g, "llllglgllll" + "lllgg" + "gglgg")) for g in range(ng)]

        def step(it, carries):
            return tuple(tile(it, carry) for (tile, _), carry in zip(blocks, carries))

        zero = jnp.zeros((1, HEAD), F32)
        lax.fori_loop(0, nc, step, ((zero, zero, jnp.zeros((8, HEAD), F32)),) * ng)
        for _, finish in blocks:
            finish()

    gw = ng * HEAD

    def col(off):
        return pl.BlockSpec((seq, gw), lambda n, b: (b, off * (nblk // ng) + n))

    vec = lambda r: pl.BlockSpec((r, gw), lambda n, b: (0, n))
    mat = pl.BlockSpec((ng, HEAD, HEAD), lambda n, b: (n, 0, 0))
    blk = pl.BlockSpec((seq, gw), lambda n, b: (b, n))
    dshape = jax.ShapeDtypeStruct((t, w), BF16)
    return pl.pallas_call(
        body, name=name, grid=(nblk // ng, nb),
        in_specs=[col(4), col(5), vec(4), vec(1), mat, vec(1), mat, vec(1), vec(1), blk, blk],
        out_specs=[blk, blk, vec(8), mat, mat],
        out_shape=[dshape, dshape, jax.ShapeDtypeStruct((8, w), F32),
                   jax.ShapeDtypeStruct((nblk, HEAD, HEAD), F32), jax.ShapeDtypeStruct((nblk, HEAD, HEAD), F32)],
        scratch_shapes=[pltpu.VMEM((ng, CHUNK, HEAD), F32)] * 2 + [pltpu.VMEM((8, gw), F32)]
        + [pltpu.VMEM((ng, HEAD, HEAD), F32)] * 2,
        compiler_params=_cp(dimension_semantics=("parallel", "arbitrary")),
    )(proj, proj, cw, cb, wa, ba, wx, bx, lam, hsv, dp)


def _ffn_conv(x, prev8, cw_ref, cbv):
    x2, x1 = _shift_down(prev8, x, 2), _shift_down(prev8, x, 1)
    y = cbv + x2 * cw_ref[0:1, :]
    y = y + x1 * cw_ref[1:2, :]
    y = y + x * cw_ref[2:3, :]
    return (x2, x1), y


def _ffn_conv_at(x_ref, base, rt, zero8, cw_ref, cbv):
    x = x_ref[base:base + rt, :]
    if base == 0:
        return _ffn_conv(x, zero8, cw_ref, cbv)[1]
    y = cbv + x_ref[base - 2:base - 2 + rt, :] * cw_ref[0:1, :]
    y = y + x_ref[base - 1:base - 1 + rt, :] * cw_ref[1:2, :]
    return y + x * cw_ref[2:3, :]


def _ffn_act_fwd(up, cw, cb, *, nb, seq, name, deps=()):
    t, f2 = up.shape
    f = f2 // 2
    tc = _tile(f, 512)
    nj = f // tc
    rt = _tile(seq, 65536 // tc, 16)
    nr = seq // rt

    def body(g_ref, v_ref, cwg_ref, cwv_ref, cbg_ref, cbv_ref, o_ref):
        cbg, cbv = cbg_ref[...], cbv_ref[...]
        z = jnp.zeros((8, tc), F32)
        for c in range(nr):
            gate = _ffn_conv_at(g_ref, c * rt, rt, z, cwg_ref, cbg)
            val = _ffn_conv_at(v_ref, c * rt, rt, z, cwv_ref, cbv)
            o_ref[c * rt:(c + 1) * rt, :] = ((gate * _sigmoid(gate)) * val).astype(BF16)

    gcol = pl.BlockSpec((seq, tc), lambda j, b: (b, j))
    vcol = pl.BlockSpec((seq, tc), lambda j, b: (b, nj + j))
    gv = lambda r: pl.BlockSpec((r, tc), lambda j, b: (0, j))
    vv = lambda r: pl.BlockSpec((r, tc), lambda j, b: (0, nj + j))
    return _pcall(
        body, n_in=6, args=[up, up, cw, cw, cb, cb], deps=deps, name=name, grid=(nj, nb),
        in_specs=[gcol, vcol, gv(3), vv(3), gv(1), vv(1)], out_specs=gcol,
        out_shape=jax.ShapeDtypeStruct((t, f), BF16),
        compiler_params=_cp(dimension_semantics=("parallel", "parallel")))


def _ffn_act_bwd(dact, up, cw, cb, *, nb, seq, name, deps=()):
    t, f2 = up.shape
    f = f2 // 2
    tc = _tile(f, 512)
    nj = f // tc
    rt = _tile(seq, 65536 // tc, 16)
    nr = seq // rt

    def body(da_ref, g_ref, v_ref, cwg_ref, cwv_ref, cbg_ref, cbv_ref,
             d_ref, sg_ref, sv_ref, eg, ev, accg, accv):
        bi = pl.program_id(1)
        cbg, cbv = cbg_ref[...], cbv_ref[...]

        @pl.when(bi == 0)
        def _():
            accg[...] = jnp.zeros_like(accg)
            accv[...] = jnp.zeros_like(accv)

        colsum = lambda z: jnp.sum(z, axis=0, keepdims=True)

        z = jnp.zeros((8, tc), F32)
        eg[seq:seq + 8, :] = z
        ev[seq:seq + 8, :] = z

        def first(c, carry):
            pg, pv = carry
            rows = pl.ds(pl.multiple_of(c * rt, rt), rt)
            xg, xv = g_ref[rows, :], v_ref[rows, :]
            (g2, g1), gate = _ffn_conv(xg, pg, cwg_ref, cbg)
            (v2, v1), val = _ffn_conv(xv, pv, cwv_ref, cbv)
            s = _sigmoid(gate)
            da = da_ref[rows, :]
            dgate = da * val * _dsilu(gate, s)
            dval = da * (gate * s)
            eg[rows, :] = dgate
            ev[rows, :] = dval
            for acc, (s2, s1, s0), d in ((accg, (g2, g1, xg), dgate), (accv, (v2, v1, xv), dval)):
                acc[0:1, :] += colsum(s2 * d)
                acc[1:2, :] += colsum(s1 * d)
                acc[2:3, :] += colsum(s0 * d)
                acc[3:4, :] += colsum(d)
            return xg[rt - 8:], xv[rt - 8:]

        lax.fori_loop(0, nr, first, (z, z))

        for c in range(nr):
            base = c * rt
            for half, (e, cw_ref) in enumerate(((eg, cwg_ref), (ev, cwv_ref))):
                d, d1, d2 = e[base:base + rt, :], e[base + 1:base + 1 + rt, :], e[base + 2:base + 2 + rt, :]
                d_ref[half, base:base + rt, :] = (
                    d * cw_ref[2:3, :] + d1 * cw_ref[1:2, :] + d2 * cw_ref[0:1, :]).astype(BF16)

        @pl.when(bi == nb - 1)
        def _():
            sg_ref[...] = accg[...]
            sv_ref[...] = accv[...]

    gcol = pl.BlockSpec((seq, tc), lambda j, b: (b, j))
    vcol = pl.BlockSpec((seq, tc), lambda j, b: (b, nj + j))
    gv = lambda r: pl.BlockSpec((r, tc), lambda j, b: (0, j))
    vv = lambda r: pl.BlockSpec((r, tc), lambda j, b: (0, nj + j))
    dshape = jax.ShapeDtypeStruct((2, t, f), BF16)
    sshape = jax.ShapeDtypeStruct((8, f), F32)
    return _pcall(
        body, n_in=7, args=[dact, up, up, cw, cw, cb, cb], deps=deps, name=name, grid=(nj, nb),
        in_specs=[gcol, gcol, vcol, gv(3), vv(3), gv(1), vv(1)],
        out_specs=[pl.BlockSpec((2, seq, tc), lambda j, b: (0, b, j)), gv(8), gv(8)],
        out_shape=[dshape, sshape, sshape],
        scratch_shapes=[pltpu.VMEM((seq + 8, tc), F32)] * 2 + [pltpu.VMEM((8, tc), F32)] * 2,
        compiler_params=_cp(dimension_semantics=("parallel", "arbitrary")))


def _adamw_math(wv, g, mv, vv):
    m = ADAM_B1 * mv + (1.0 - ADAM_B1) * g
    v = ADAM_B2 * vv + (1.0 - ADAM_B2) * (g * g)
    m_hat = m / (1.0 - ADAM_B1 ** ADAM_STEP)
    v_hat = v / (1.0 - ADAM_B2 ** ADAM_STEP)
    delta = -ADAM_LR * (m_hat / (jnp.sqrt(v_hat) + ADAM_EPS) + ADAM_WD * wv)
    return delta, m, v


def _adamw_shard(wv, mv, vv, p_own, rb, *, name):
    _, r, cdim = wv.shape
    tr = _row_tile(r, cdim)

    def body(w_ref, m_ref, v_ref, p_ref, rb_ref, g_ref, d_ref, mo_ref, vo_ref):
        g = p_ref[...].astype(F32)
        for k in range(3):
            g = g + rb_ref[k].astype(F32)
        g_ref[...] = g
        d_ref[...], mo_ref[...], vo_ref[...] = _adamw_math(w_ref[...], g, m_ref[...], v_ref[...])

    row = pl.BlockSpec((tr, cdim), lambda i: (i, 0))
    row3 = pl.BlockSpec((None, tr, cdim), lambda i: (0, i, 0))
    shp = jax.ShapeDtypeStruct((1, r, cdim), F32)
    return pl.pallas_call(
        body, name=name, grid=(r // tr,),
        in_specs=[row3, row3, row3, row, pl.BlockSpec((3, tr, cdim), lambda i: (0, i, 0))],
        out_specs=[row3] * 4, out_shape=[shp] * 4, compiler_params=_cp(dimension_semantics=("parallel",)),
    )(wv, mv, vv, p_own, rb)


def _adamw_packed(wv, g, mv, vv, *, name):
    r = wv.shape[0]
    tr = _tile(r, 256, 8)

    def body(w_ref, g_ref, m_ref, v_ref, d_ref, mo_ref, vo_ref):
        d_ref[...], mo_ref[...], vo_ref[...] = _adamw_math(w_ref[...], g_ref[...], m_ref[...], v_ref[...])

    row = pl.BlockSpec((tr, LANE), lambda i: (i, 0))
    shp = jax.ShapeDtypeStruct((r, LANE), F32)
    return pl.pallas_call(
        body, name=name, grid=(r // tr,), in_specs=[row] * 4, out_specs=[row] * 3, out_shape=[shp] * 3,
        compiler_params=_cp(dimension_semantics=("parallel",)),
    )(wv, g, mv, vv)


def _place():
    return lax.axis_index("x"), lax.axis_index("y"), lax.axis_index("c")


def _all_reduce_packed(p, *, name):
    r = p.shape[0]
    tr = _tile(r, 256, 8)

    def body(p_ref, o_ref, buf, send, recv, lsem):
        x, y, c = _place()
        me, sibling = (x, y, c), (x, y, 1 - c)
        chips = [(1 - x, y), (x, 1 - y), (1 - x, 1 - y)]

        def slot(px, py, pc):
            return 4 * px + 2 * py + pc

        def copy(k, block, to, src=None):
            dst = buf.at[slot(*block)]
            return pltpu.make_async_remote_copy(
                src_ref=dst if src is None else src, dst_ref=dst, send_sem=send.at[k], recv_sem=recv.at[k],
                device_id=to, device_id_type=MESH)

        mine = pltpu.make_async_copy(p_ref, buf.at[slot(*me)], lsem)
        mine.start()
        first = [copy(0, me, sibling, src=p_ref)]
        first += [copy(1 + j, me, (*chip, c), src=p_ref) for j, chip in enumerate(chips)]
        for cp in first:
            cp.start()
        passed = [copy(4 + j, (*chip, c), sibling) for j, chip in enumerate(chips)]
        for j, chip in enumerate(chips):
            copy(1 + j, (*chip, c), me).wait_recv()
            passed[j].start()
        copy(0, sibling, me).wait_recv()
        for j, chip in enumerate(chips):
            copy(4 + j, (*chip, 1 - c), me).wait_recv()
        for cp in first + passed:
            cp.wait_send()
        mine.wait()

        def add(i, carry):
            rows = pl.ds(pl.multiple_of(i * tr, tr), tr)
            s = buf[0, rows, :]
            for d in range(1, NDEV):
                s = s + buf[d, rows, :]
            o_ref[rows, :] = s
            return carry

        lax.fori_loop(0, r // tr, add, 0)

    vm = pl.BlockSpec(memory_space=pltpu.VMEM)
    return pl.pallas_call(
        body, name=name, in_specs=[vm], out_specs=vm, out_shape=jax.ShapeDtypeStruct(p.shape, p.dtype),
        scratch_shapes=[pltpu.VMEM((NDEV,) + p.shape, p.dtype), pltpu.SemaphoreType.DMA((7,)),
                        pltpu.SemaphoreType.DMA((7,)), pltpu.SemaphoreType.DMA],
        compiler_params=_cp(),
    )(p)


HBM = pl.BlockSpec(memory_space=pltpu.HBM)
SEM = pl.BlockSpec(memory_space=pltpu.SEMAPHORE)
EFFECT = pltpu.SideEffectType.DATAFLOW_SIDE_EFFECTING


def _plan_copies(plan, s_refs, l_refs, send, recv):
    def pick(kind, a, idx):
        ref = (s_refs if kind == "s" else l_refs)[a]
        return ref if idx is None else ref.at[idx]

    return [pltpu.make_async_remote_copy(
        src_ref=pick(*src), dst_ref=pick(*dst), send_sem=send.at[i], recv_sem=recv.at[i],
        device_id=to, device_id_type=MESH) for i, (src, dst, to) in enumerate(plan(*_place()))]


def _xfer_start(srcs, lands, plan, *, name, deps=()):
    ns, nl = len(srcs), len(lands)
    nd = len(deps)
    ncopy = len(plan(0, 0, 0))

    def body(*refs):
        s_refs, l_refs = refs[:ns], refs[ns:ns + nl]
        send, recv = refs[ns + nl + nd], refs[ns + nl + nd + 1]
        token = refs[-1]
        for cp in _plan_copies(plan, s_refs, l_refs, send, recv):
            cp.start()
        token[...] = jnp.zeros_like(token)

    bufs = list(srcs) + list(lands)
    outs = pl.pallas_call(
        body, name=name,
        out_shape=(pltpu.SemaphoreType.DMA((ncopy,)), pltpu.SemaphoreType.DMA((ncopy,)),
                   *[pltpu.HBM(b.shape, b.dtype) for b in bufs], jax.ShapeDtypeStruct((8, LANE), F32)),
        in_specs=[HBM] * (ns + nl) + [ANY] * nd,
        out_specs=(SEM, SEM, *[HBM] * (ns + nl), pl.BlockSpec(memory_space=pltpu.VMEM)),
        input_output_aliases={i: 2 + i for i in range(ns + nl)},
        compiler_params=pltpu.CompilerParams(has_side_effects=EFFECT),
    )(*[pltpu.with_memory_space_constraint(b, pltpu.HBM) for b in bufs], *deps)
    return outs[0], outs[1], list(outs[2:2 + ns]), list(outs[2 + ns:2 + ns + nl]), outs[-1]


def _xfer_wait(send, recv, srcs, lands, plan, after, *, name):
    ns, nl = len(srcs), len(lands)

    def body(*refs):
        s_refs, l_refs = refs[:ns], refs[ns:ns + nl]
        send_ref, recv_ref = refs[ns + nl], refs[ns + nl + 1]
        for cp in _plan_copies(plan, s_refs, l_refs, send_ref, recv_ref):
            cp.wait_send()
            cp.wait_recv()

    bufs = list(srcs) + list(lands)
    outs = pl.pallas_call(
        body, name=name, out_shape=tuple(pltpu.HBM(b.shape, b.dtype) for b in bufs),
        in_specs=[HBM] * (ns + nl) + [SEM, SEM, ANY], out_specs=tuple([HBM] * (ns + nl)),
        input_output_aliases={i: i for i in range(ns + nl)},
        compiler_params=pltpu.CompilerParams(has_side_effects=EFFECT),
    )(*bufs, send, recv, after)
    return list(outs[:ns]), list(outs[ns:])


def _slot8(px, py, pc):
    return 4 * px + 2 * py + pc


def _slot4(r):
    return lambda px, py, pc: (2 * px + py, pl.ds(pc * r, r))


def _plan_gather_first(slots):
    def plan(x, y, c):
        peers = [(x, y, c), (x, y, 1 - c), (1 - x, y, c), (x, 1 - y, c), (1 - x, 1 - y, c)]
        return [(("s", a, None), ("l", a, slot(x, y, c)), to) for a, slot in enumerate(slots) for to in peers]
    return plan


def _plan_gather_pass(slots):
    def plan(x, y, c):
        chips = [(1 - x, y), (x, 1 - y), (1 - x, 1 - y)]
        return [(("l", a, slot(px, py, c)), ("l", a, slot(px, py, c)), (x, y, 1 - c))
                for a, slot in enumerate(slots) for px, py in chips]
    return plan


def _plan_rs_sibling(r):
    def plan(x, y, c):
        return [(("s", 0, (j, pl.ds((1 - c) * r, r))), ("l", 0, j), (x, y, 1 - c)) for j in range(4)]
    return plan


def _plan_rs_plane():
    def plan(x, y, c):
        chips = [(1 - x, y), (x, 1 - y), (1 - x, 1 - y)]
        return [(("s", 0, 2 * px + py), ("l", 0, k), (px, py, c)) for k, (px, py) in enumerate(chips)]
    return plan


def _add_pairs(g, rcv, cidx, *, name):
    _, r, cdim = rcv.shape
    tr = _row_tile(r, cdim, budget=4194304)
    per = r // tr

    def body(c_ref, g_ref, r_ref, o_ref):
        o_ref[...] = (g_ref[...].astype(F32) + r_ref[...].astype(F32)).astype(o_ref.dtype)

    grid_spec = pltpu.PrefetchScalarGridSpec(
        num_scalar_prefetch=1, grid=(4, per),
        in_specs=[pl.BlockSpec((None, tr, cdim), lambda j, i, c_ref: (j, c_ref[0] * per + i, 0)),
                  pl.BlockSpec((None, tr, cdim), lambda j, i, c_ref: (j, i, 0))],
        out_specs=pl.BlockSpec((None, tr, cdim), lambda j, i, c_ref: (j, i, 0)))
    return pl.pallas_call(
        body, name=name, grid_spec=grid_spec, out_shape=jax.ShapeDtypeStruct((4, r, cdim), g.dtype),
        compiler_params=_cp(dimension_semantics=("parallel", "parallel")),
    )(cidx, g, rcv)


def _pack(arrs):
    flat = jnp.concatenate([a.reshape(-1).astype(F32) for a in arrs])
    n = flat.shape[0]
    rows = -(-n // LANE)
    rows = -(-rows // 256) * 256
    return jnp.pad(flat, (0, rows * LANE - n)).reshape(rows, LANE)


def _unpack(packed, shapes):
    flat = packed.reshape(-1)
    out, o = [], 0
    for s in shapes:
        n = math.prod(s)
        out.append(flat[o:o + n].reshape(s))
        o += n
    return out


def _pad_blocks(a, w, wp):
    lead = a.shape[:-1]
    k = a.shape[-1] // w
    pads = [(0, 0)] * (len(lead) + 1) + [(0, wp - w)]
    return jnp.pad(a.reshape(*lead, k, w), pads).reshape(*lead, k * wp)


def kernel(x, ln1_w, w_in, lb_gamma, hg_norm_w, lru_conv_w, lru_conv_b, lru_wa, lru_ba, lru_wx, lru_bx, lru_lambda, lru_norm_w, w_out, ln2_w, ffn_w_up, ffn_conv_w, ffn_conv_b, ffn_w_down, final_norm_w, loss_target, m_ln1_w, m_w_in, m_lb_gamma, m_hg_norm_w, m_lru_conv_w, m_lru_conv_b, m_lru_wa, m_lru_ba, m_lru_wx, m_lru_bx, m_lru_lambda, m_lru_norm_w, m_w_out, m_ln2_w, m_ffn_w_up, m_ffn_conv_w, m_ffn_conv_b, m_ffn_w_down, m_final_norm_w, v_ln1_w, v_w_in, v_lb_gamma, v_hg_norm_w, v_lru_conv_w, v_lru_conv_b, v_lru_wa, v_lru_ba, v_lru_wx, v_lru_bx, v_lru_lambda, v_lru_norm_w, v_w_out, v_ln2_w, v_ffn_w_up, v_ffn_conv_w, v_ffn_conv_b, v_ffn_w_down, v_final_norm_w):
    nb, seq, d = x.shape
    t = nb * seq
    wmix = d // 2
    in_sh = w_in.shape[2]
    up_sh = ffn_w_up.shape[2]
    up_pad = -(-up_sh // LANE) * LANE
    hs = ffn_w_down.shape[1]
    fpad = 4 * up_pad
    cx, cy, cc = _place()
    me = 4 * cx + 2 * cy + cc
    plane = 2 * cx + cy

    def gather(shards, lands, slots, tag, deps=(), also=lambda x, y, c: []):
        first = _plan_gather_first(slots)
        plan = lambda x, y, c: first(x, y, c) + also(x, y, c)
        st = _xfer_start(shards, lands, plan, name=f"gather_{tag}_start", deps=deps)
        return (st, plan, slots, tag), st[4]

    def pass_on(g, after):
        st, plan, slots, tag = g
        _, lands = _xfer_wait(st[0], st[1], st[2], st[3], plan, after, name=f"gather_{tag}_wait")
        st2 = _xfer_start([], lands, _plan_gather_pass(slots), name=f"gather_{tag}_pass")
        return (st2, slots, tag), st2[4]

    def gathered(g, after):
        st2, slots, tag = g
        return _xfer_wait(st2[0], st2[1], [], st2[3], _plan_gather_pass(slots), after, name=f"gather_{tag}_done")[1]

    land8 = lambda s: lax.empty((NDEV,) + s.shape, s.dtype)

    win_s = w_in[0].astype(BF16)
    wout_s = w_out[0].astype(BF16)
    wup_s = jnp.pad(ffn_w_up[0], ((0, 0), (0, up_pad - up_sh))).astype(BF16)
    wdn_s = ffn_w_down[0].astype(BF16)
    fcw_s = jnp.pad(ffn_conv_w[0], ((0, 0), (0, up_pad - up_sh)))
    ga, tok = gather([win_s, lru_conv_w[0], fcw_s], [land8(win_s), land8(lru_conv_w[0]), land8(fcw_s)],
                     [_slot8] * 3, "a")
    go, tok = gather([wout_s], [land8(wout_s)], [_slot8], "o", deps=(tok,))
    gu, tok = gather([wup_s], [land8(wup_s)], [_slot8], "u", deps=(tok,))
    npad = up_pad - 2 * hs
    zero_pad = lambda x, y, c: [(("s", 1, None), ("l", 0, (j, pl.ds(2 * hs, npad))), (x, y, c)) for j in range(4)]
    assert npad > 0
    gd, tok = gather([wdn_s, jnp.zeros((npad, d), BF16)], [lax.empty((4, up_pad, d), BF16)], [_slot4(hs)], "d",
                     deps=(tok,), also=zero_pad)
    fcb = _pad_blocks(ffn_conv_b, up_sh, up_pad)
    wa_b, wx_b = lru_wa[0].astype(BF16), lru_wx[0].astype(BF16)

    xf = x.reshape(t, d)
    hn = _rms_fwd(xf, ln1_w, name="ln1_fwd", deps=(tok,))
    ga, tok = pass_on(ga, hn)
    win_g, lcw_g, fcw_g = gathered(ga, tok)
    lcw = lcw_g.transpose(1, 0, 2).reshape(lru_conv_w.shape[1], wmix)
    fcw = fcw_g.transpose(1, 0, 2).reshape(ffn_conv_w.shape[1], 2 * fpad)
    proj = _mm(hn, win_g, kind="nn", out_dtype=F32, name="in_proj", tm_cap=512, tk_cap=d)
    go, tok = pass_on(go, proj)
    o_hg, o_pre, states = _hgrn_fwd(proj, lb_gamma, hg_norm_w, nb=nb, seq=seq, name="hgrn_fwd")
    (wout_g,) = gathered(go, o_pre)
    wout_f = wout_g.reshape(1, d, d)
    p_lru, h_lru = _lru_fwd(proj, lcw, lru_conv_b, wa_b, lru_ba, wx_b, lru_bx, lru_lambda,
                            nb=nb, seq=seq, name="lru_fwd", deps=(tok,))
    o_lru = _rms_fwd(p_lru, lru_norm_w, name="lru_norm_fwd")
    gu, tok = pass_on(gu, o_lru)
    mix = jnp.concatenate([o_hg, o_lru], axis=1)
    h1 = _mm(mix, wout_f, kind="nn", out_dtype=F32, name="out_proj", res=xf, tm_cap=512, tk_cap=d, deps=(tok,))
    hn2 = _rms_fwd(h1, ln2_w, name="ln2_fwd")
    (wup_g,) = gathered(gu, hn2)
    up = _mm(hn2, wup_g, kind="nn", out_dtype=F32, name="ffn_up", tm_cap=512, tn_cap=up_pad // 2, tk_cap=d)
    gd, tok = pass_on(gd, up)
    act = _ffn_act_fwd(up, fcw, fcb, nb=nb, seq=seq, name="ffn_act_fwd", deps=(tok,))
    (wdn_g,) = gathered(gd, act)
    wdn_f = wdn_g.reshape(1, fpad, d)
    h2 = _mm(act, wdn_f, kind="nn", out_dtype=F32, name="ffn_down", res=h1, tm_cap=512, tn_cap=1024,
             tk_cap=2 * up_pad)

    dh2, dh2_b, g_fnw, loss_part = _loss_head(h2, final_norm_w.reshape(1, d), loss_target.reshape(t, d),
                                              name="loss_head")
    loss = lax.psum(loss_part[0, 0], ("x", "y", "c"))

    cidx = jnp.reshape(cc, (1,)).astype(jnp.int32)

    def to_sibling(g4, r, tag, deps=()):
        land = lax.empty((4, r, g4.shape[2]), g4.dtype)
        st = _xfer_start([g4], [land], _plan_rs_sibling(r), name=f"rs_sib_start_{tag}", deps=deps)
        return (st, r, tag), st[4]

    def sibling_sum(rs, after):
        st, r, tag = rs
        (g4,), (rcv,) = _xfer_wait(st[0], st[1], st[2], st[3], _plan_rs_sibling(r), after, name=f"rs_sib_wait_{tag}")
        psum = _add_pairs(g4, rcv, cidx, name=f"rs_add_{tag}")
        land = lax.empty((3,) + psum.shape[1:], psum.dtype)
        st2 = _xfer_start([psum], [land], _plan_rs_plane(), name=f"rs_plane_start_{tag}")
        return (st2, tag), st2[4]

    dact = _mm(dh2_b, wdn_f, kind="nt", out_dtype=F32, name="ffn_down_dx", tm_cap=512, tn_cap=up_pad // 2, tk_cap=d)
    g_wdn = _mm(act, dh2_b, kind="tn", out_dtype=BF16, name="ffn_down_dw", tm_cap=1024, tn_cap=1024, tk_cap=t)
    rs_d, tok = to_sibling(g_wdn.reshape(4, up_pad, d), hs, "down")
    dup, s_g, s_v = _ffn_act_bwd(dact, up, fcw, fcb, nb=nb, seq=seq, name="ffn_act_bwd", deps=(tok,))
    pl_d, tok = sibling_sum(rs_d, dup)
    dhn2 = _mm(dup, wup_g, kind="nt", out_dtype=F32, name="ffn_up_dx", tm_cap=512, tn_cap=1024, tk_cap=up_pad,
               k_blocks=2, deps=(tok,))
    g_wup = _mm(dup, hn2, kind="tn", out_dtype=BF16, name="ffn_up_dw", tm_cap=1024, tn_cap=1024, tk_cap=t)
    rs_u, tok = to_sibling(g_wup.reshape(4, 2 * up_pad, d), up_pad, "up")
    dh1, dh1_b, g_ln2 = _rms_bwd(dhn2, h1, ln2_w, name="ln2_bwd", extra=dh2, want_bf16=True, deps=(tok,))
    dmix = _mm(dh1_b, wout_f, kind="nt", out_dtype=F32, name="out_proj_dx", tm_cap=512, tn_cap=1024, tk_cap=d)
    pl_u, tok = sibling_sum(rs_u, dmix)
    g_wout = _mm(mix, dh1_b, kind="tn", out_dtype=BF16, name="out_proj_dw", tn_cap=1024, tk_cap=t, deps=(tok,))
    rs_o, tok = to_sibling(g_wout.reshape(4, d // 4, d), d // NDEV, "out")
    dp_lru, g_lnw = _rms_bwd(dmix, p_lru, lru_norm_w, name="lru_norm_bwd", dy_cb=1, deps=(tok,))
    dxr, dyr, s_lru, g_wa, g_wx = _lru_bwd(proj, lcw, lru_conv_b, wa_b, lru_ba, wx_b, lru_bx, lru_lambda,
                                           h_lru, dp_lru, nb=nb, seq=seq, name="lru_bwd")
    pl_o, tok = sibling_sum(rs_o, dxr)
    dqr, dfr, dir_, dgr, s_hg = _hgrn_bwd(proj, lb_gamma, hg_norm_w, o_pre, states, dmix,
                                          nb=nb, seq=seq, name="hgrn_bwd", deps=(tok,))
    dproj = jnp.concatenate([dqr, dfr, dir_, dgr, dxr, dyr], axis=1)
    g_win = _mm(hn, dproj, kind="tn", out_dtype=BF16, name="in_proj_dw", out_blocks=NDEV, tm_cap=512, tn_cap=in_sh,
                tk_cap=t)
    rs_i, tok = to_sibling(g_win.reshape(4, 2 * d, in_sh), d, "in")
    dhn = _mm(dproj, win_g, kind="nt", out_dtype=F32, name="in_proj_dx", tm_cap=512, tn_cap=1024, tk_cap=in_sh,
              k_blocks=4, deps=(tok,))
    grad_x, g_ln1 = _rms_bwd(dhn, xf, ln1_w, name="ln1_bwd", extra=dh1)

    fcw_parts = jnp.concatenate([s_g[0:3], s_v[0:3]], axis=1).reshape(3, NDEV, up_pad)[:, :, :up_sh]
    fcb_part = jnp.concatenate([s_g[3:4], s_v[3:4]], axis=1).reshape(NDEV, up_pad)[:, :up_sh]
    small_parts = [g_ln1, s_hg[0:2], s_hg[2:3], s_lru[0:4], s_lru[4:5], g_wa, s_lru[5:6], g_wx, s_lru[6:7],
                   s_lru[7:8], g_lnw, g_ln2, fcw_parts, fcb_part, g_fnw]
    reduced = _all_reduce_packed(_pack(small_parts), name="allreduce_small")
    pl_i, _ = sibling_sum(rs_i, reduced)
    summed = _unpack(reduced, [p.shape for p in small_parts])
    (s_ln1, s_lbg, s_hgn, s_lcw, s_lcb, s_wa, s_ba, s_wx, s_bx, s_lam, s_lnw, s_ln2, s_fcw, s_fcb, s_fnw) = summed
    sh_lcw = lru_conv_w.shape[2]
    g_small = {
        "ln1_w": s_ln1, "lb_gamma": s_lbg, "hg_norm_w": s_hgn,
        "lru_conv_w": lax.dynamic_slice_in_dim(s_lcw, me * sh_lcw, sh_lcw, axis=1),
        "lru_conv_b": s_lcb, "lru_wa": s_wa, "lru_ba": s_ba, "lru_wx": s_wx, "lru_bx": s_bx,
        "lru_lambda": s_lam, "lru_norm_w": s_lnw, "ln2_w": s_ln2,
        "ffn_conv_w": lax.dynamic_index_in_dim(s_fcw, me, 1, keepdims=False),
        "ffn_conv_b": s_fcb, "final_norm_w": s_fnw,
    }
    w_small = {"ln1_w": ln1_w, "lb_gamma": lb_gamma, "hg_norm_w": hg_norm_w, "lru_conv_w": lru_conv_w,
               "lru_conv_b": lru_conv_b, "lru_wa": lru_wa, "lru_ba": lru_ba, "lru_wx": lru_wx, "lru_bx": lru_bx,
               "lru_lambda": lru_lambda, "lru_norm_w": lru_norm_w, "ln2_w": ln2_w, "ffn_conv_w": ffn_conv_w,
               "ffn_conv_b": ffn_conv_b, "final_norm_w": final_norm_w}
    m_small = {"ln1_w": m_ln1_w, "lb_gamma": m_lb_gamma, "hg_norm_w": m_hg_norm_w, "lru_conv_w": m_lru_conv_w,
               "lru_conv_b": m_lru_conv_b, "lru_wa": m_lru_wa, "lru_ba": m_lru_ba, "lru_wx": m_lru_wx,
               "lru_bx": m_lru_bx, "lru_lambda": m_lru_lambda, "lru_norm_w": m_lru_norm_w, "ln2_w": m_ln2_w,
               "ffn_conv_w": m_ffn_conv_w, "ffn_conv_b": m_ffn_conv_b, "final_norm_w": m_final_norm_w}
    v_small = {"ln1_w": v_ln1_w, "lb_gamma": v_lb_gamma, "hg_norm_w": v_hg_norm_w, "lru_conv_w": v_lru_conv_w,
               "lru_conv_b": v_lru_conv_b, "lru_wa": v_lru_wa, "lru_ba": v_lru_ba, "lru_wx": v_lru_wx,
               "lru_bx": v_lru_bx, "lru_lambda": v_lru_lambda, "lru_norm_w": v_lru_norm_w, "ln2_w": v_ln2_w,
               "ffn_conv_w": v_ffn_conv_w, "ffn_conv_b": v_ffn_conv_b, "final_norm_w": v_final_norm_w}
    names = list(w_small)
    shapes = [w_small[k].shape for k in names]
    g_small = {k: g_small[k].reshape(w_small[k].shape) for k in names}
    packed = _adamw_packed(_pack([w_small[k] for k in names]), _pack([g_small[k] for k in names]),
                           _pack([m_small[k] for k in names]), _pack([v_small[k] for k in names]),
                           name="adamw_small")
    d_small, nm_small, nv_small = ({k: a for k, a in zip(names, _unpack(pk, shapes))} for pk in packed)

    def finish(pl_x, after, trim=lambda a: a):
        st, tag = pl_x
        (psum,), (rb,) = _xfer_wait(st[0], st[1], st[2], st[3], _plan_rs_plane(), after, name=f"rs_plane_wait_{tag}")
        return trim(lax.dynamic_index_in_dim(psum, plane, 0, keepdims=False)), trim(rb)

    upd_dn = _adamw_shard(ffn_w_down, m_ffn_w_down, v_ffn_w_down, *finish(pl_d, packed[0]), name="adamw_w_down")
    tr3 = lambda a: a.transpose(0, 2, 1)
    upd_up_t = _adamw_shard(tr3(ffn_w_up), tr3(m_ffn_w_up), tr3(v_ffn_w_up), *finish(pl_u, upd_dn[0]),
                            name="adamw_w_up")
    upd_up = [tr3(a) for a in upd_up_t]
    upd_out = _adamw_shard(w_out, m_w_out, v_w_out, *finish(pl_o, upd_up_t[0]), name="adamw_w_out")
    upd_in = _adamw_shard(w_in, m_w_in, v_w_in, *finish(pl_i, upd_out[0]), name="adamw_w_in")

    grads, deltas, new_m, new_v = dict(g_small), dict(d_small), dict(nm_small), dict(nv_small)
    for k, upd in (("w_in", upd_in), ("w_out", upd_out), ("ffn_w_up", upd_up), ("ffn_w_down", upd_dn)):
        grads[k], deltas[k], new_m[k], new_v[k] = upd
    order = ["ln1_w", "w_in", "lb_gamma", "hg_norm_w", "lru_conv_w", "lru_conv_b", "lru_wa", "lru_ba", "lru_wx",
             "lru_bx", "lru_lambda", "lru_norm_w", "w_out", "ln2_w", "ffn_w_up", "ffn_conv_w", "ffn_conv_b",
             "ffn_w_down", "final_norm_w"]
    return (loss, grad_x.reshape(nb, seq, d), *[grads[k] for k in order], *[deltas[k] for k in order],
            *[new_m[k] for k in order], *[new_v[k] for k in order])
```

```python
import math

import jax
import jax.numpy as jnp
from jax import lax
from jax.experimental import pallas as pl
from jax.experimental.pallas import tpu as pltpu

F32, BF16 = jnp.float32, jnp.bfloat16
EPS = 1e-6
HEAD = 128
CHUNK = 64
SUB = 16
NSUB = CHUNK // SUB
LRU_C = 8.0
LANE = 128
NDEV = 8
ADAM_LR, ADAM_B1, ADAM_B2, ADAM_EPS, ADAM_WD, ADAM_STEP = 0.001, 0.9, 0.999, 1e-08, 0.01, 10
MESH = pl.DeviceIdType.MESH
ANY = pl.BlockSpec(memory_space=pl.ANY)
VMEM_LIMIT = 56 * 1024 * 1024


def _cp(**kw):
    return pltpu.CompilerParams(vmem_limit_bytes=VMEM_LIMIT, **kw)


def _tile(n, cap, mult=LANE):
    best = None
    for t in range(mult, min(n, cap) + 1, mult):
        if n % t == 0:
            best = t
    return best if best is not None else n


def _row_tile(r, cdim, budget=262144):
    return _tile(r, max(16, budget // cdim), 16)


def _sigmoid(x):
    return jax.nn.sigmoid(x)


def _dsilu(x, s):
    return s * (1.0 + x * (1.0 - s))


def _iota_rows(n, w=LANE):
    return lax.broadcasted_iota(jnp.int32, (n, w), 0)


def _lane_groups(n, want=2):
    while n % want:
        want //= 2
    return want


def _group_views(refs, g, kinds):
    assert len(refs) == len(kinds)
    cols = pl.ds(g * HEAD, HEAD)
    return [r.at[:, cols] if kind == "l" else r.at[g] for r, kind in zip(refs, kinds)]


def _shift_down(prev8, xt, k):
    cat = jnp.concatenate([prev8, xt], axis=0)
    return pltpu.roll(cat, k, 0)[8:]


def _shift_up(xt, next8, k):
    cat = jnp.concatenate([xt, next8], axis=0)
    n = cat.shape[0]
    return pltpu.roll(cat, n - k, 0)[: xt.shape[0]]


def _scan_fwd(a, u):
    n = a.shape[0]
    row = _iota_rows(n, a.shape[1])
    k = 1
    while k < n:
        keep = row >= k
        a_s = jnp.where(keep, pltpu.roll(a, k, 0), 1.0)
        u_s = jnp.where(keep, pltpu.roll(u, k, 0), 0.0)
        u = a * u_s + u
        a = a * a_s
        k *= 2
    return a, u


def _scan_bwd(a, u):
    n = a.shape[0]
    row = _iota_rows(n, a.shape[1])
    k = 1
    while k < n:
        keep = row < n - k
        a_s = jnp.where(keep, pltpu.roll(a, n - k, 0), 1.0)
        u_s = jnp.where(keep, pltpu.roll(u, n - k, 0), 0.0)
        u = a * u_s + u
        a = a * a_s
        k *= 2
    return a, u


def _cumsum_fwd(u):
    n = u.shape[0]
    row = _iota_rows(n, u.shape[1])
    k = 1
    while k < n:
        u = u + jnp.where(row >= k, pltpu.roll(u, k, 0), 0.0)
        k *= 2
    return u


def _cumsum_bwd(u):
    n = u.shape[0]
    row = _iota_rows(n, u.shape[1])
    k = 1
    while k < n:
        u = u + jnp.where(row < n - k, pltpu.roll(u, n - k, 0), 0.0)
        k *= 2
    return u


def _dot(a, b, dims):
    return lax.dot_general(a.astype(BF16), b.astype(BF16), (dims, ((), ())), preferred_element_type=F32)


NN = ((1,), (0,))
NT = ((1,), (1,))
TN = ((0,), (0,))


def _pcall(body, *, n_in, in_specs, args, deps=(), **kw):
    nd = len(deps)
    if nd:
        inner = body

        def body(*refs):
            return inner(*refs[:n_in], *refs[n_in + nd:])

        in_specs = list(in_specs) + [ANY] * nd
        args = list(args) + list(deps)
    return pl.pallas_call(body, in_specs=in_specs, **kw)(*args)


def _mm(a, b3, *, kind, out_dtype, name, res=None, tm_cap=1024, tn_cap=1536, tk_cap=1024, out_blocks=1, k_blocks=1,
        col_part=None, into=None, deps=()):
    out_ix = lambda i, j, k: (i, j)
    if kind == "nn":
        m, kdim = a.shape
        nb, _, nsh = b3.shape
        n = nb * nsh
        tm, tn, tk = _tile(m, tm_cap), _tile(nsh, tn_cap), _tile(kdim, tk_cap)
        per = nsh // tn
        a_blk, a_ix = (tm, tk), lambda i, j, k: (i, k)
        b_blk, b_ix = (None, tk, tn), lambda i, j, k: (j // per, k, j % per)
        if col_part is not None:
            h, parts = col_part
            tn = nsh // parts
            b_blk, b_ix, out_ix = (None, tk, tn), lambda i, j, k: (j, k, h), lambda i, j, k: (i, j * parts + h)
        dims = NN
    elif kind == "nt":
        m, kdim = a.shape[-2], a.shape[-1] * (a.shape[0] if a.ndim == 3 else 1)
        nb, n, ksh = b3.shape
        tm, tn, tk = _tile(m, tm_cap), _tile(n, tn_cap), _tile(ksh, tk_cap)
        per = ksh // tk
        a_blk, a_ix = (tm, tk), lambda i, j, k: (i, k)
        b_blk, b_ix = (None, tn, tk), lambda i, j, k: (k // per, j, k % per)
        if k_blocks > 1:
            assert tk == ksh and nb % k_blocks == 0
            tk = k_blocks * ksh
            a_blk, b_blk, b_ix = (tm, tk), (k_blocks, tn, ksh), lambda i, j, k: (k, j, 0)
        if a.ndim == 3:
            aper = a.shape[2] // tk
            a_blk, a_ix = (None,) + a_blk, lambda i, j, k: (k // aper, i, k % aper)
        dims = NT
    else:
        kdim, m = a.shape[-2], a.shape[-1] * (a.shape[0] if a.ndim == 3 else 1)
        n = b3.shape[1]
        nsh = n // out_blocks
        tm, tn, tk = _tile(a.shape[-1], tm_cap), _tile(nsh, tn_cap), _tile(kdim, tk_cap)
        per = nsh // tn
        a_blk, a_ix = (tk, tm), lambda i, j, k: (k, i)
        b_blk, b_ix = (tk, tn), lambda i, j, k: (k, j)
        if a.ndim == 3:
            aper = a.shape[2] // tm
            a_blk, a_ix = (None,) + a_blk, lambda i, j, k: (i // aper, k, i % aper)
        dims = TN
    nk = kdim // tk
    j_outer = nk == 1
    nj = nb if col_part is not None else n // tn
    grid = (nj, m // tm, nk) if j_outer else (m // tm, nj, nk)
    at = (lambda f: lambda g0, g1, k: f(g1, g0, k)) if j_outer else (lambda f: f)
    a_spec, b_spec = pl.BlockSpec(a_blk, at(a_ix)), pl.BlockSpec(b_blk, at(b_ix))
    n_args = 2 + (res is not None) + (into is not None)

    def body(*refs):
        a_ref, b_ref = refs[:2]
        r_ref = refs[2] if res is not None else None
        o_ref = refs[n_args]
        if k_blocks > 1:
            w = b_ref.shape[2]
            part = sum(lax.dot_general(a_ref[:, s * w:(s + 1) * w], b_ref[s], (dims, ((), ())),
                                       preferred_element_type=F32) for s in range(k_blocks))
        else:
            part = lax.dot_general(a_ref[...], b_ref[...], (dims, ((), ())), preferred_element_type=F32)
        if nk == 1:
            o_ref[...] = (part if res is None else part + r_ref[...]).astype(o_ref.dtype)
            return
        acc = refs[-1]
        k = pl.program_id(2)

        @pl.when(k == 0)
        def _():
            acc[...] = part

        @pl.when(k > 0)
        def _():
            acc[...] += part

        @pl.when(k == nk - 1)
        def _():
            r = acc[...]
            if res is not None:
                r = r + r_ref[...]
            o_ref[...] = r.astype(o_ref.dtype)

    in_specs = [a_spec, b_spec]
    args = [a, b3]
    if res is not None:
        in_specs.append(pl.BlockSpec((tm, tn), at(out_ix)))
        args.append(res)
    aliases = {}
    if into is not None:
        aliases = {len(args): 0}
        in_specs.append(ANY)
        args.append(into)
    if kind == "tn":
        out_shape = jax.ShapeDtypeStruct((out_blocks, m, nsh), out_dtype)
        out_spec = pl.BlockSpec((None, tm, tn), at(lambda i, j, k: (j // per, i, j % per)))
    else:
        out_shape = jax.ShapeDtypeStruct((m, n), out_dtype)
        out_spec = pl.BlockSpec((tm, tn), at(out_ix))
    return _pcall(
        body, n_in=len(args), in_specs=in_specs, args=args, deps=deps, input_output_aliases=aliases,
        name=name, grid=grid, out_specs=out_spec, out_shape=out_shape,
        scratch_shapes=[pltpu.VMEM((tm, tn), F32)] if nk > 1 else [],
        compiler_params=_cp(dimension_semantics=("parallel", "parallel", "arbitrary")))


def _rms_fwd(x, w, *, name, tm=256, deps=()):
    t, d = x.shape
    tm = _tile(t, tm, 16)

    def body(x_ref, w_ref, o_ref):
        xv = x_ref[...]
        r = lax.rsqrt(jnp.mean(xv * xv, axis=-1, keepdims=True) + EPS)
        o_ref[...] = ((xv * r) * w_ref[...]).astype(o_ref.dtype)

    return _pcall(
        body, n_in=2, args=[x, w], deps=deps, name=name, grid=(t // tm,),
        in_specs=[pl.BlockSpec((tm, d), lambda i: (i, 0)), pl.BlockSpec((1, d), lambda i: (0, 0))],
        out_specs=pl.BlockSpec((tm, d), lambda i: (i, 0)),
        out_shape=jax.ShapeDtypeStruct((t, d), BF16), compiler_params=_cp())


def _rms_bwd(dy, x, w, *, name, extra=None, dy_cb=0, want_bf16=False, tm=256, deps=()):
    t, d = x.shape
    tm = _tile(t, tm, 16)

    def body(*refs):
        refs = list(refs)
        dy_ref, x_ref, w_ref = refs[:3]
        e_ref = refs[3] if extra is not None else None
        outs = refs[4:] if extra is not None else refs[3:]
        dx_ref = outs[0]
        dxb_ref = outs[1] if want_bf16 else None
        dw_ref = outs[-1]
        i = pl.program_id(0)
        xv = x_ref[...]
        r = lax.rsqrt(jnp.mean(xv * xv, axis=-1, keepdims=True) + EPS)
        nh = xv * r
        dyv = dy_ref[...]
        dn = dyv * w_ref[...]
        dx = r * (dn - nh * jnp.mean(dn * nh, axis=-1, keepdims=True))
        if extra is not None:
            dx = dx + e_ref[...]
        dx_ref[...] = dx
        if want_bf16:
            dxb_ref[...] = dx.astype(BF16)
        part = jnp.sum(dyv * nh, axis=0, keepdims=True)

        @pl.when(i == 0)
        def _():
            dw_ref[...] = part

        @pl.when(i > 0)
        def _():
            dw_ref[...] += part

    row = pl.BlockSpec((tm, d), lambda i: (i, 0))
    in_specs = [pl.BlockSpec((tm, d), lambda i: (i, dy_cb)), row, pl.BlockSpec((1, d), lambda i: (0, 0))]
    args = [dy, x, w]
    if extra is not None:
        in_specs.append(row)
        args.append(extra)
    out_shape = [jax.ShapeDtypeStruct((t, d), F32)]
    out_specs = [row]
    if want_bf16:
        out_shape.append(jax.ShapeDtypeStruct((t, d), BF16))
        out_specs.append(row)
    out_shape.append(jax.ShapeDtypeStruct((1, d), F32))
    out_specs.append(pl.BlockSpec((1, d), lambda i: (0, 0)))
    return _pcall(
        body, n_in=len(args), in_specs=in_specs, args=args, deps=deps,
        name=name, grid=(t // tm,), out_specs=out_specs, out_shape=out_shape,
        compiler_params=_cp(dimension_semantics=("arbitrary",)))


def _loss_head(h, w, tgt, *, name, tm=256):
    t, d = h.shape
    tm = _tile(t, tm, 16)

    def body(h_ref, w_ref, t_ref, dh_ref, dhb_ref, dw_ref, loss_ref):
        i = pl.program_id(0)
        xv = h_ref[...]
        wv = w_ref[...]
        r = lax.rsqrt(jnp.mean(xv * xv, axis=-1, keepdims=True) + EPS)
        nh = xv * r
        e = nh * wv - t_ref[...]
        part_loss = jnp.full((1, LANE), 0.5 * jnp.sum(jnp.mean(e * e, axis=-1, keepdims=True)), F32)
        dyv = e * (1.0 / d)
        dn = dyv * wv
        dx = r * (dn - nh * jnp.mean(dn * nh, axis=-1, keepdims=True))
        dh_ref[...] = dx
        dhb_ref[...] = dx.astype(BF16)
        part = jnp.sum(dyv * nh, axis=0, keepdims=True)

        @pl.when(i == 0)
        def _():
            dw_ref[...] = part
            loss_ref[...] = part_loss

        @pl.when(i > 0)
        def _():
            dw_ref[...] += part
            loss_ref[...] += part_loss

    row = pl.BlockSpec((tm, d), lambda i: (i, 0))
    vec = pl.BlockSpec((1, d), lambda i: (0, 0))
    return pl.pallas_call(
        body, name=name, grid=(t // tm,), in_specs=[row, vec, row],
        out_specs=[row, row, vec, pl.BlockSpec((1, LANE), lambda i: (0, 0))],
        out_shape=[jax.ShapeDtypeStruct((t, d), F32), jax.ShapeDtypeStruct((t, d), BF16),
                   jax.ShapeDtypeStruct((1, d), F32), jax.ShapeDtypeStruct((1, LANE), F32)],
        compiler_params=_cp(dimension_semantics=("arbitrary",)),
    )(h, w, tgt)


def _lower_bound(lbg_ref):
    g0, g1 = lbg_ref[0:1, :], lbg_ref[1:2, :]
    m = jnp.maximum(g0, g1)
    e0, e1 = jnp.exp(g0 - m), jnp.exp(g1 - m)
    return e0 / (e0 + e1)


def _seg_bounds():
    offs, o = {}, 0
    for i in range(1, NSUB):
        offs[i] = (o, o + SUB * i)
        o += SUB * i
    return offs, o


def _pad_rows(x, n):
    if x.shape[0] == n:
        return x
    return jnp.concatenate([x, jnp.zeros((n - x.shape[0], x.shape[1]), x.dtype)], axis=0)


def _offdiag_mask():
    offs, total = _seg_bounds()
    padded = -(-total // LANE) * LANE
    rsub = lax.broadcasted_iota(jnp.int32, (CHUNK, padded), 0) // SUB
    col = lax.broadcasted_iota(jnp.int32, (CHUNK, padded), 1)
    cseg = jnp.zeros((CHUNK, padded), jnp.int32)
    for i in range(1, NSUB):
        cseg = cseg + (col >= offs[i][0]).astype(jnp.int32)
    return (rsub == cseg) & (col < total)


def _offdiag_setup(q, k, b, v, b_c, mask):
    offs, total = _seg_bounds()
    padded = -(-total // LANE) * LANE
    eq_parts = [jnp.zeros((SUB, HEAD), F32)]
    ek_parts, k_parts, v_parts = [], [], []
    for i in range(1, NSUB):
        r_i = b_c[SUB * i - 1:SUB * i, :]
        eq_parts.append(jnp.exp(b[SUB * i:SUB * (i + 1)] - r_i))
        ek_parts.append(jnp.exp(r_i - b[0:SUB * i]))
        k_parts.append(k[0:SUB * i])
        v_parts.append(v[0:SUB * i])
    eq = jnp.concatenate(eq_parts, axis=0)
    ek = _pad_rows(jnp.concatenate(ek_parts, axis=0), padded)
    kt = _pad_rows(jnp.concatenate(k_parts, axis=0), padded) * ek
    vs = _pad_rows(jnp.concatenate(v_parts, axis=0), padded)
    qt = q * eq
    a = jnp.where(mask, _dot(qt, kt, NT), 0.0)
    return offs, eq, ek, kt, vs, qt, a


def _hgrn_fwd(proj, lbg, nw, *, nb, seq, name):
    t = proj.shape[0]
    w = lbg.shape[1]
    nh = w // HEAD
    nc = seq // CHUNK
    ng = _lane_groups(nh)

    def head(mask, q_ref, f_ref, i_ref, g_ref, lbg_ref, nw_ref, ohg_ref, opre_ref, st_ref, k_c, b_c, v_c, st):
        lb = _lower_bound(lbg_ref)
        nwv = nw_ref[...]
        st[...] = jnp.zeros_like(st)

        def chunk(c, carry):
            rows = pl.ds(pl.multiple_of(c * CHUNK, CHUNK), CHUNK)
            qr = q_ref[rows, :]
            q = qr * _sigmoid(qr)
            f = lb + (1.0 - lb) * _sigmoid(f_ref[rows, :])
            k = 1.0 - f
            b = _cumsum_fwd(jnp.log(f))
            v = i_ref[rows, :]
            k_c[...] = k
            b_c[...] = b
            v_c[...] = v
            s_t = st[...]
            st_ref[c] = s_t
            o = _dot(q * jnp.exp(b), s_t, NT)
            _, _, _, _, vs, _, a = _offdiag_setup(q, k, b, v, b_c, mask)
            o = o + _dot(a, vs, NN)
            diag = []
            for i in range(NSUB):
                accs = [jnp.zeros((8, HEAD), F32) for _ in range(SUB // 8)]
                for j in range(SUB):
                    r = SUB * i + j
                    bj, kj, vj = b_c[r:r + 1, :], k_c[r:r + 1, :], v_c[r:r + 1, :]
                    for p in range(j // 8, SUB // 8):
                        lo = SUB * i + 8 * p
                        d = jnp.exp(b[lo:lo + 8] - bj)
                        if 8 * p < j:
                            d = jnp.where(_iota_rows(8) + 8 * p >= j, d, 0.0)
                        s = jnp.sum(q[lo:lo + 8] * d * kj, axis=-1, keepdims=True)
                        accs[p] = accs[p] + s * vj
                diag.extend(accs)
            o = o + jnp.concatenate(diag, axis=0)
            bl = b_c[CHUNK - 1:CHUNK, :]
            kb = k * jnp.exp(bl - b)
            st[...] = s_t * jnp.exp(bl) + _dot(v, kb, TN)
            opre_ref[rows, :] = o
            rn = lax.rsqrt(jnp.mean(o * o, axis=-1, keepdims=True) + EPS)
            gr = g_ref[rows, :]
            ohg_ref[rows, :] = (((o * rn) * nwv) * (gr * _sigmoid(gr))).astype(BF16)
            return carry

        return chunk

    def body(*refs):
        mask = _offdiag_mask()
        chunks = [head(mask, *_group_views(refs, g, "llllllll" + "ggggg")) for g in range(ng)]

        def step(c, carry):
            for chunk in chunks:
                chunk(c, carry)
            return carry

        lax.fori_loop(0, nc, step, 0)

    gw = ng * HEAD

    def col(off):
        return pl.BlockSpec((seq, gw), lambda h, b: (b, off * (nh // ng) + h))

    vec = lambda r: pl.BlockSpec((r, gw), lambda h, b: (0, h))
    out_blk = pl.BlockSpec((seq, gw), lambda h, b: (b, h))
    return pl.pallas_call(
        body, name=name, grid=(nh // ng, nb),
        in_specs=[col(0), col(1), col(2), col(3), vec(2), vec(1)],
        out_specs=[out_blk, out_blk, pl.BlockSpec((None, ng, nc, HEAD, HEAD), lambda h, b: (b, h, 0, 0, 0))],
        out_shape=[jax.ShapeDtypeStruct((t, w), BF16), jax.ShapeDtypeStruct((t, w), F32),
                   jax.ShapeDtypeStruct((nb, nh, nc, HEAD, HEAD), F32)],
        scratch_shapes=[pltpu.VMEM((ng, CHUNK, HEAD), F32)] * 3 + [pltpu.VMEM((ng, HEAD, HEAD), F32)],
        compiler_params=_cp(dimension_semantics=("parallel", "parallel")),
    )(proj, proj, proj, proj, lbg, nw)


def _hgrn_bwd(proj, lbg, nw, opre, states, dmix, *, nb, seq, name, deps=()):
    t = proj.shape[0]
    w = lbg.shape[1]
    nh = w // HEAD
    nc = seq // CHUNK
    ng = _lane_groups(nh)

    def head(mask, q_ref, f_ref, i_ref, g_ref, lbg_ref, nw_ref, opre_ref, st_ref, dm_ref,
             dq_ref, df_ref, di_ref, dg_ref, small_ref, k_c, b_c, v_c, dst, dlb_s, dwn_s):
        bi = pl.program_id(1)
        lb = _lower_bound(lbg_ref)
        nwv = nw_ref[...]
        dst[...] = jnp.zeros_like(dst)

        @pl.when(bi == 0)
        def _():
            dlb_s[...] = jnp.zeros_like(dlb_s)
            dwn_s[...] = jnp.zeros_like(dwn_s)

        def chunk(it, carry):
            c = nc - 1 - it
            rows = pl.ds(pl.multiple_of(c * CHUNK, CHUNK), CHUNK)
            qr = q_ref[rows, :]
            sq = _sigmoid(qr)
            q = qr * sq
            sg = _sigmoid(f_ref[rows, :])
            f = lb + (1.0 - lb) * sg
            k = 1.0 - f
            b = _cumsum_fwd(jnp.log(f))
            v = i_ref[rows, :]
            k_c[...] = k
            b_c[...] = b
            v_c[...] = v
            o = opre_ref[rows, :]
            rn = lax.rsqrt(jnp.mean(o * o, axis=-1, keepdims=True) + EPS)
            nhat = o * rn
            dm = dm_ref[rows, :]
            gr = g_ref[rows, :]
            sgr = _sigmoid(gr)
            dnw = dm * (gr * sgr)
            dg_ref[rows, :] = (dm * (nhat * nwv) * _dsilu(gr, sgr)).astype(BF16)
            dwn_s[...] += jnp.sum(dnw * nhat, axis=0, keepdims=True)
            dn = dnw * nwv
            do = rn * (dn - nhat * jnp.mean(dn * nhat, axis=-1, keepdims=True))
            s_t = st_ref[c]
            ds = dst[...]
            eb = jnp.exp(b)
            qb = q * eb
            bl = b_c[CHUNK - 1:CHUNK, :]
            ebl = jnp.exp(bl)
            kdec = jnp.exp(bl - b)
            kb = k * kdec
            dqb = _dot(do, s_t, NN)
            dkb = _dot(v, ds, NN)
            dv = _dot(kb, ds, NT)
            d_ebl = jnp.sum(ds * s_t, axis=0, keepdims=True)
            dst[...] = ds * ebl + _dot(do, qb, TN)
            dq = dqb * eb
            dk = dkb * kdec
            t_kb = dkb * kb
            db = dqb * qb - t_kb
            db_last = jnp.sum(t_kb, axis=0, keepdims=True) + d_ebl * ebl
            offs, eq, ek, kt, vs, qt, a = _offdiag_setup(q, k, b, v, b_c, mask)
            da = jnp.where(mask, _dot(do, vs, NT), 0.0)
            dvs = _dot(a, do, TN)
            dqt = _dot(da, kt, NN)
            dkt = _dot(da, qt, TN)
            dq = dq + dqt * eq
            db = db + dqt * qt
            gk, gb = dkt * ek, dkt * kt
            zero8 = jnp.zeros((8, HEAD), F32)
            off_k, off_v, off_b = ([zero8] * (CHUNK // 8) for _ in range(3))
            for i in range(1, NSUB):
                lo, hi = offs[i]
                for p in range((hi - lo) // 8):
                    rows8 = slice(lo + 8 * p, lo + 8 * p + 8)
                    off_k[p] = off_k[p] + gk[rows8]
                    off_b[p] = off_b[p] - gb[rows8]
                    off_v[p] = off_v[p] + dvs[rows8]
            npv = SUB // 8
            dq_d, dk_d, dv_d, db_d = [], [], [], []
            for i in range(NSUB):
                aq = [zero8 for _ in range(npv)]
                ak = [zero8 for _ in range(npv)]
                av = [off_v[i * npv + p] for p in range(npv)]
                for j in range(SUB):
                    r = SUB * i + j
                    bj, kj, vj = b_c[r:r + 1, :], k_c[r:r + 1, :], v_c[r:r + 1, :]
                    pm_sum, ad_sum = None, None
                    for p in range(j // 8, npv):
                        lo = SUB * i + 8 * p
                        d = jnp.exp(b[lo:lo + 8] - bj)
                        if 8 * p < j:
                            d = jnp.where(_iota_rows(8) + 8 * p >= j, d, 0.0)
                        qd = q[lo:lo + 8] * d
                        dop = do[lo:lo + 8]
                        a_j = jnp.sum(qd * kj, axis=-1, keepdims=True)
                        da_j = jnp.sum(dop * vj, axis=-1, keepdims=True)
                        aq[p] = aq[p] + (da_j * d) * kj
                        pm, ad = da_j * qd, a_j * dop
                        pm_sum = pm if pm_sum is None else pm_sum + pm
                        ad_sum = ad if ad_sum is None else ad_sum + ad
                    pj = j // 8
                    here = _iota_rows(8) == (j - 8 * pj)
                    ak[pj] = ak[pj] + jnp.where(here, jnp.sum(pm_sum, axis=0, keepdims=True), 0.0)
                    av[pj] = av[pj] + jnp.where(here, jnp.sum(ad_sum, axis=0, keepdims=True), 0.0)
                for p in range(npv):
                    lo = SUB * i + 8 * p
                    db_d.append(off_b[i * npv + p] + q[lo:lo + 8] * aq[p] - k[lo:lo + 8] * ak[p])
                    dk_d.append(off_k[i * npv + p] + ak[p])
                dq_d.extend(aq)
                dv_d.extend(av)
            dq = dq + jnp.concatenate(dq_d, axis=0)
            dk = dk + jnp.concatenate(dk_d, axis=0)
            dv = dv + jnp.concatenate(dv_d, axis=0)
            db = db + jnp.concatenate(db_d, axis=0)
            db = db + jnp.where(_iota_rows(CHUNK) == CHUNK - 1, db_last, 0.0)
            dgl = _cumsum_bwd(db)
            dfv = dgl / f - dk
            dlb_s[...] += jnp.sum(dfv * (1.0 - sg), axis=0, keepdims=True)
            df_ref[rows, :] = (dfv * (1.0 - lb) * (sg * (1.0 - sg))).astype(BF16)
            dq_ref[rows, :] = (dq * _dsilu(qr, sq)).astype(BF16)
            di_ref[rows, :] = dv.astype(BF16)
            return carry

        def finish():
            @pl.when(bi == nb - 1)
            def _():
                dgam = dlb_s[...] * lb * (1.0 - lb)
                small_ref[...] = jnp.zeros_like(small_ref)
                small_ref[0:1, :] = dgam
                small_ref[1:2, :] = -dgam
                small_ref[2:3, :] = dwn_s[...]

        return chunk, finish

    def body(*refs):
        mask = _offdiag_mask()
        heads = [head(mask, *_group_views(refs, g, "lllllllgl" + "lllll" + "ggggll")) for g in range(ng)]

        def step(it, carry):
            for chunk, _ in heads:
                chunk(it, carry)
            return carry

        lax.fori_loop(0, nc, step, 0)
        for _, finish in heads:
            finish()

    gw = ng * HEAD

    def col(off):
        return pl.BlockSpec((seq, gw), lambda h, b: (b, off * (nh // ng) + h))

    vec = lambda r: pl.BlockSpec((r, gw), lambda h, b: (0, h))
    blk = pl.BlockSpec((seq, gw), lambda h, b: (b, h))
    dshape = jax.ShapeDtypeStruct((t, w), BF16)
    return _pcall(
        body, n_in=9, args=[proj, proj, proj, proj, lbg, nw, opre, states, dmix], deps=deps,
        name=name, grid=(nh // ng, nb),
        in_specs=[col(0), col(1), col(2), col(3), vec(2), vec(1), blk,
                  pl.BlockSpec((None, ng, nc, HEAD, HEAD), lambda h, b: (b, h, 0, 0, 0)), blk],
        out_specs=[blk, blk, blk, blk, vec(8)],
        out_shape=[dshape, dshape, dshape, dshape, jax.ShapeDtypeStruct((8, w), F32)],
        scratch_shapes=[pltpu.VMEM((ng, CHUNK, HEAD), F32)] * 3 + [pltpu.VMEM((ng, HEAD, HEAD), F32)]
        + [pltpu.VMEM((1, gw), F32)] * 2,
        compiler_params=_cp(dimension_semantics=("parallel", "arbitrary")))


def _expm1(x):
    poly = x * (1.0 + x * (0.5 + x * (1.0 / 6 + x * (1.0 / 24 + x * (1.0 / 120 + x * (1.0 / 720))))))
    return jnp.where(jnp.abs(x) < 0.25, poly, jnp.exp(x) - 1.0)


def _softplus_neg(lam):
    x = -lam
    e = jnp.exp(-jnp.abs(x))
    u = 1.0 + e
    l1p = jnp.where(u == 1.0, e, jnp.log(u) * (e / jnp.where(u == 1.0, 1.0, u - 1.0)))
    return jnp.maximum(x, 0.0) + l1p


_GELU_C = math.sqrt(2.0 / math.pi)


def _gelu(y):
    return 0.5 * y * (1.0 + jnp.tanh(_GELU_C * (y + 0.044715 * (y * y * y))))


def _dgelu(y):
    th = jnp.tanh(_GELU_C * (y + 0.044715 * (y * y * y)))
    return 0.5 * (1.0 + th) + 0.5 * y * (1.0 - th * th) * (_GELU_C * (1.0 + 3 * 0.044715 * (y * y)))


def _lru_gates(xr, prev8, cw_ref, cb, wa_ref, ba, wx_ref, bx, sp, first):
    x3, x2, x1 = _shift_down(prev8, xr, 3), _shift_down(prev8, xr, 2), _shift_down(prev8, xr, 1)
    xb = cb + x3 * cw_ref[0:1, :]
    xb = xb + x2 * cw_ref[1:2, :]
    xb = xb + x1 * cw_ref[2:3, :]
    xb = xb + xr * cw_ref[3:4, :]
    r = _sigmoid(_dot(xb, wa_ref[...], NN) + ba)
    ig = _sigmoid(_dot(xb, wx_ref[...], NN) + bx)
    la = (-LRU_C * r) * sp
    a = jnp.exp(la)
    start = jnp.logical_and(first, _iota_rows(xr.shape[0]) == 0)
    mult = jnp.where(start, 1.0, jnp.sqrt(-_expm1(2.0 * la)))
    return (x3, x2, x1), xb, r, ig, a, mult, start


def _lru_fwd(proj, cw, cb, wa, ba, wx, bx, lam, *, nb, seq, name, deps=()):
    t = proj.shape[0]
    w = cb.shape[1]
    nblk = w // HEAD
    nc = seq // CHUNK
    ng = _lane_groups(nblk, 4)

    def block(x_ref, y_ref, cw_ref, cb_ref, wa_ref, ba_ref, wx_ref, bx_ref, lam_ref, p_ref, h_ref):
        sp = _softplus_neg(lam_ref[...])
        cb_v, ba_v, bx_v = cb_ref[...], ba_ref[...], bx_ref[...]

        def tile(c, carry):
            hc, prev8 = carry
            base = pl.multiple_of(c * CHUNK, CHUNK)
            rows = pl.ds(base, CHUNK)
            xr = x_ref[rows, :]
            _, xb, _, ig, a, mult, _ = _lru_gates(xr, prev8, cw_ref, cb_v, wa_ref, ba_v, wx_ref, bx_v, sp, c == 0)
            ap, up = _scan_fwd(a, xb * ig * mult)
            h = up + ap * hc
            h_ref[rows, :] = h
            p_ref[rows, :] = h * _gelu(y_ref[rows, :])
            h_last = jnp.sum(jnp.where(_iota_rows(8) == 7, h[CHUNK - 8:], 0.0), axis=0, keepdims=True)
            return h_last, xr[CHUNK - 8:]

        return tile

    def body(*refs):
        tiles = [block(*_group_views(refs, g, "llllglgll" + "ll")) for g in range(ng)]

        def step(c, carries):
            return tuple(tile(c, carry) for tile, carry in zip(tiles, carries))

        lax.fori_loop(0, nc, step, ((jnp.zeros((1, HEAD), F32), jnp.zeros((8, HEAD), F32)),) * ng)

    gw = ng * HEAD

    def col(off):
        return pl.BlockSpec((seq, gw), lambda n, b: (b, off * (nblk // ng) + n))

    vec = lambda r: pl.BlockSpec((r, gw), lambda n, b: (0, n))
    mat = pl.BlockSpec((ng, HEAD, HEAD), lambda n, b: (n, 0, 0))
    blk = pl.BlockSpec((seq, gw), lambda n, b: (b, n))
    return _pcall(
        body, n_in=9, args=[proj, proj, cw, cb, wa, ba, wx, bx, lam], deps=deps, name=name, grid=(nblk // ng, nb),
        in_specs=[col(4), col(5), vec(4), vec(1), mat, vec(1), mat, vec(1), vec(1)],
        out_specs=[blk, blk],
        out_shape=[jax.ShapeDtypeStruct((t, w), F32), jax.ShapeDtypeStruct((t, w), F32)],
        compiler_params=_cp(dimension_semantics=("parallel", "parallel")))


def _lru_bwd(proj, cw, cb, wa, ba, wx, bx, lam, hsv, dp, *, nb, seq, name):
    t = proj.shape[0]
    w = cb.shape[1]
    nblk = w // HEAD
    nc = seq // CHUNK
    ng = _lane_groups(nblk, 4)

    def block(x_ref, y_ref, cw_ref, cb_ref, wa_ref, ba_ref, wx_ref, bx_ref, lam_ref, h_ref, dp_ref,
              dx_ref, dy_ref, small_ref, dwa_ref, dwx_ref, a_c, g_c, acc, dwa_s, dwx_s):
        bi = pl.program_id(1)
        lamv = lam_ref[...]
        sp = _softplus_neg(lamv)
        cb_v, ba_v, bx_v = cb_ref[...], ba_ref[...], bx_ref[...]

        @pl.when(bi == 0)
        def _():
            acc[...] = jnp.zeros_like(acc)
            dwa_s[...] = jnp.zeros_like(dwa_s)
            dwx_s[...] = jnp.zeros_like(dwx_s)

        def tile(it, carry):
            g_next, a_next, dxb_next8 = carry
            c = nc - 1 - it
            base = pl.multiple_of(c * CHUNK, CHUNK)
            rows = pl.ds(base, CHUNK)
            before = pl.ds(pl.multiple_of(jnp.maximum(base - 8, 0), 8), 8)
            inner = c > 0
            xr = x_ref[rows, :]
            prev8 = jnp.where(inner, x_ref[before, :], 0.0)
            (x3, x2, x1), xb, r, ig, a, mult, start = _lru_gates(
                xr, prev8, cw_ref, cb_v, wa_ref, ba_v, wx_ref, bx_v, sp, c == 0)
            h = h_ref[rows, :]
            h_m1 = _shift_down(jnp.where(inner, h_ref[before, :], 0.0), h, 1)
            yv = y_ref[rows, :]
            dpv = dp_ref[rows, :]
            dy_ref[rows, :] = (dpv * h * _dgelu(yv)).astype(BF16)
            dh = dpv * _gelu(yv)
            a_up = _shift_up(a, jnp.broadcast_to(a_next, (8, HEAD)), 1)
            ap, gp = _scan_bwd(a_up, dh)
            g = gp + ap * g_next
            a_c[...] = a
            g_c[...] = g
            da = g * h_m1
            gx = g * xb
            dxb = g * ig * mult
            dig = gx * mult
            dmult = jnp.where(start, 0.0, gx * ig)
            dla = da * a - dmult * (a * a) / mult
            dzr = (dla * (-LRU_C * sp)) * (r * (1.0 - r))
            dzi = dig * (ig * (1.0 - ig))
            dxb = dxb + _dot(dzr, wa_ref[...], NT) + _dot(dzi, wx_ref[...], NT)
            dwa_s[...] += _dot(xb, dzr, TN)
            dwx_s[...] += _dot(xb, dzi, TN)
            d1, d2, d3 = (_shift_up(dxb, dxb_next8, s) for s in (1, 2, 3))
            dx = dxb * cw_ref[3:4, :] + d1 * cw_ref[2:3, :] + d2 * cw_ref[1:2, :] + d3 * cw_ref[0:1, :]
            dx_ref[rows, :] = dx.astype(BF16)
            colsum = lambda z: jnp.sum(z, axis=0, keepdims=True)
            acc[0:1, :] += colsum(x3 * dxb)
            acc[1:2, :] += colsum(x2 * dxb)
            acc[2:3, :] += colsum(x1 * dxb)
            acc[3:4, :] += colsum(xr * dxb)
            acc[4:5, :] += colsum(dxb)
            acc[5:6, :] += colsum(dzr)
            acc[6:7, :] += colsum(dzi)
            acc[7:8, :] += colsum(dla * (-LRU_C * r))
            return g_c[0:1, :], a_c[0:1, :], dxb[0:8]

        def finish():
            @pl.when(bi == nb - 1)
            def _():
                small_ref[...] = acc[...]
                small_ref[7:8, :] = acc[7:8, :] * (-_sigmoid(-lamv))
                dwa_ref[...] = dwa_s[...]
                dwx_ref[...] = dwx_s[...]

        return tile, finish

    def body(*refs):
        blocks = [block(*_group_views(refs, g, "llllglgllll" + "lllgg" + "gglgg")) for g in range(ng)]

        def step(it, carries):
            return tuple(tile(it, carry) for (tile, _), carry in zip(blocks, carries))

        zero = jnp.zeros((1, HEAD), F32)
        lax.fori_loop(0, nc, step, ((zero, zero, jnp.zeros((8, HEAD), F32)),) * ng)
        for _, finish in blocks:
            finish()

    gw = ng * HEAD

    def col(off):
        return pl.BlockSpec((seq, gw), lambda n, b: (b, off * (nblk // ng) + n))

    vec = lambda r: pl.BlockSpec((r, gw), lambda n, b: (0, n))
    mat = pl.BlockSpec((ng, HEAD, HEAD), lambda n, b: (n, 0, 0))
    blk = pl.BlockSpec((seq, gw), lambda n, b: (b, n))
    dshape = jax.ShapeDtypeStruct((t, w), BF16)
    return pl.pallas_call(
        body, name=name, grid=(nblk // ng, nb),
        in_specs=[col(4), col(5), vec(4), vec(1), mat, vec(1), mat, vec(1), vec(1), blk, blk],
        out_specs=[blk, blk, vec(8), mat, mat],
        out_shape=[dshape, dshape, jax.ShapeDtypeStruct((8, w), F32),
                   jax.ShapeDtypeStruct((nblk, HEAD, HEAD), F32), jax.ShapeDtypeStruct((nblk, HEAD, HEAD), F32)],
        scratch_shapes=[pltpu.VMEM((ng, CHUNK, HEAD), F32)] * 2 + [pltpu.VMEM((8, gw), F32)]
        + [pltpu.VMEM((ng, HEAD, HEAD), F32)] * 2,
        compiler_params=_cp(dimension_semantics=("parallel", "arbitrary")),
    )(proj, proj, cw, cb, wa, ba, wx, bx, lam, hsv, dp)


def _ffn_conv(x, prev8, cw_ref, cbv):
    x2, x1 = _shift_down(prev8, x, 2), _shift_down(prev8, x, 1)
    y = cbv + x2 * cw_ref[0:1, :]
    y = y + x1 * cw_ref[1:2, :]
    y = y + x * cw_ref[2:3, :]
    return (x2, x1), y


def _ffn_conv_at(x_ref, base, rt, zero8, cw_ref, cbv):
    x = x_ref[base:base + rt, :]
    if base == 0:
        return _ffn_conv(x, zero8, cw_ref, cbv)[1]
    y = cbv + x_ref[base - 2:base - 2 + rt, :] * cw_ref[0:1, :]
    y = y + x_ref[base - 1:base - 1 + rt, :] * cw_ref[1:2, :]
    return y + x * cw_ref[2:3, :]


def _ffn_act_fwd(up, cw, cb, *, nb, seq, name, deps=()):
    t, f2 = up.shape
    f = f2 // 2
    tc = _tile(f, 512)
    nj = f // tc
    rt = _tile(seq, 65536 // tc, 16)
    nr = seq // rt

    def body(g_ref, v_ref, cwg_ref, cwv_ref, cbg_ref, cbv_ref, o_ref):
        cbg, cbv = cbg_ref[...], cbv_ref[...]
        z = jnp.zeros((8, tc), F32)
        for c in range(nr):
            gate = _ffn_conv_at(g_ref, c * rt, rt, z, cwg_ref, cbg)
            val = _ffn_conv_at(v_ref, c * rt, rt, z, cwv_ref, cbv)
            o_ref[c * rt:(c + 1) * rt, :] = ((gate * _sigmoid(gate)) * val).astype(BF16)

    gcol = pl.BlockSpec((seq, tc), lambda j, b: (b, j))
    vcol = pl.BlockSpec((seq, tc), lambda j, b: (b, nj + j))
    gv = lambda r: pl.BlockSpec((r, tc), lambda j, b: (0, j))
    vv = lambda r: pl.BlockSpec((r, tc), lambda j, b: (0, nj + j))
    return _pcall(
        body, n_in=6, args=[up, up, cw, cw, cb, cb], deps=deps, name=name, grid=(nj, nb),
        in_specs=[gcol, vcol, gv(3), vv(3), gv(1), vv(1)], out_specs=gcol,
        out_shape=jax.ShapeDtypeStruct((t, f), BF16),
        compiler_params=_cp(dimension_semantics=("parallel", "parallel")))


def _ffn_act_bwd(dact, up, cw, cb, *, nb, seq, name, deps=()):
    t, f2 = up.shape
    f = f2 // 2
    tc = _tile(f, 512)
    nj = f // tc
    rt = _tile(seq, 65536 // tc, 16)
    nr = seq // rt

    def body(da_ref, g_ref, v_ref, cwg_ref, cwv_ref, cbg_ref, cbv_ref,
             d_ref, sg_ref, sv_ref, eg, ev, accg, accv):
        bi = pl.program_id(1)
        cbg, cbv = cbg_ref[...], cbv_ref[...]

        @pl.when(bi == 0)
        def _():
            accg[...] = jnp.zeros_like(accg)
            accv[...] = jnp.zeros_like(accv)

        colsum = lambda z: jnp.sum(z, axis=0, keepdims=True)

        z = jnp.zeros((8, tc), F32)
        eg[seq:seq + 8, :] = z
        ev[seq:seq + 8, :] = z

        def first(c, carry):
            pg, pv = carry
            rows = pl.ds(pl.multiple_of(c * rt, rt), rt)
            xg, xv = g_ref[rows, :], v_ref[rows, :]
            (g2, g1), gate = _ffn_conv(xg, pg, cwg_ref, cbg)
            (v2, v1), val = _ffn_conv(xv, pv, cwv_ref, cbv)
            s = _sigmoid(gate)
            da = da_ref[rows, :]
            dgate = da * val * _dsilu(gate, s)
            dval = da * (gate * s)
            eg[rows, :] = dgate
            ev[rows, :] = dval
            for acc, (s2, s1, s0), d in ((accg, (g2, g1, xg), dgate), (accv, (v2, v1, xv), dval)):
                acc[0:1, :] += colsum(s2 * d)
                acc[1:2, :] += colsum(s1 * d)
                acc[2:3, :] += colsum(s0 * d)
                acc[3:4, :] += colsum(d)
            return xg[rt - 8:], xv[rt - 8:]

        lax.fori_loop(0, nr, first, (z, z))

        for c in range(nr):
            base = c * rt
            for half, (e, cw_ref) in enumerate(((eg, cwg_ref), (ev, cwv_ref))):
                d, d1, d2 = e[base:base + rt, :], e[base + 1:base + 1 + rt, :], e[base + 2:base + 2 + rt, :]
                d_ref[half, base:base + rt, :] = (
                    d * cw_ref[2:3, :] + d1 * cw_ref[1:2, :] + d2 * cw_ref[0:1, :]).astype(BF16)

        @pl.when(bi == nb - 1)
        def _():
            sg_ref[...] = accg[...]
            sv_ref[...] = accv[...]

    gcol = pl.BlockSpec((seq, tc), lambda j, b: (b, j))
    vcol = pl.BlockSpec((seq, tc), lambda j, b: (b, nj + j))
    gv = lambda r: pl.BlockSpec((r, tc), lambda j, b: (0, j))
    vv = lambda r: pl.BlockSpec((r, tc), lambda j, b: (0, nj + j))
    dshape = jax.ShapeDtypeStruct((2, t, f), BF16)
    sshape = jax.ShapeDtypeStruct((8, f), F32)
    return _pcall(
        body, n_in=7, args=[dact, up, up, cw, cw, cb, cb], deps=deps, name=name, grid=(nj, nb),
        in_specs=[gcol, gcol, vcol, gv(3), vv(3), gv(1), vv(1)],
        out_specs=[pl.BlockSpec((2, seq, tc), lambda j, b: (0, b, j)), gv(8), gv(8)],
        out_shape=[dshape, sshape, sshape],
        scratch_shapes=[pltpu.VMEM((seq + 8, tc), F32)] * 2 + [pltpu.VMEM((8, tc), F32)] * 2,
        compiler_params=_cp(dimension_semantics=("parallel", "arbitrary")))


def _adamw_math(wv, g, mv, vv):
    m = ADAM_B1 * mv + (1.0 - ADAM_B1) * g
    v = ADAM_B2 * vv + (1.0 - ADAM_B2) * (g * g)
    m_hat = m / (1.0 - ADAM_B1 ** ADAM_STEP)
    v_hat = v / (1.0 - ADAM_B2 ** ADAM_STEP)
    delta = -ADAM_LR * (m_hat / (jnp.sqrt(v_hat) + ADAM_EPS) + ADAM_WD * wv)
    return delta, m, v


def _adamw_shard(wv, mv, vv, p_own, rb, *, name):
    _, r, cdim = wv.shape
    tr = _row_tile(r, cdim)

    def body(w_ref, m_ref, v_ref, p_ref, rb_ref, g_ref, d_ref, mo_ref, vo_ref):
        g = p_ref[...].astype(F32)
        for k in range(3):
            g = g + rb_ref[k].astype(F32)
        g_ref[...] = g
        d_ref[...], mo_ref[...], vo_ref[...] = _adamw_math(w_ref[...], g, m_ref[...], v_ref[...])

    row = pl.BlockSpec((tr, cdim), lambda i: (i, 0))
    row3 = pl.BlockSpec((None, tr, cdim), lambda i: (0, i, 0))
    shp = jax.ShapeDtypeStruct((1, r, cdim), F32)
    return pl.pallas_call(
        body, name=name, grid=(r // tr,),
        in_specs=[row3, row3, row3, row, pl.BlockSpec((3, tr, cdim), lambda i: (0, i, 0))],
        out_specs=[row3] * 4, out_shape=[shp] * 4, compiler_params=_cp(dimension_semantics=("parallel",)),
    )(wv, mv, vv, p_own, rb)


def _adamw_packed(wv, g, mv, vv, *, name):
    r = wv.shape[0]
    tr = _tile(r, 256, 8)

    def body(w_ref, g_ref, m_ref, v_ref, d_ref, mo_ref, vo_ref):
        d_ref[...], mo_ref[...], vo_ref[...] = _adamw_math(w_ref[...], g_ref[...], m_ref[...], v_ref[...])

    row = pl.BlockSpec((tr, LANE), lambda i: (i, 0))
    shp = jax.ShapeDtypeStruct((r, LANE), F32)
    return pl.pallas_call(
        body, name=name, grid=(r // tr,), in_specs=[row] * 4, out_specs=[row] * 3, out_shape=[shp] * 3,
        compiler_params=_cp(dimension_semantics=("parallel",)),
    )(wv, g, mv, vv)


def _place():
    return lax.axis_index("x"), lax.axis_index("y"), lax.axis_index("c")


def _all_reduce_packed(p, *, name):
    r = p.shape[0]
    tr = _tile(r, 256, 8)

    def body(p_ref, o_ref, buf, send, recv, lsem):
        x, y, c = _place()
        me, sibling = (x, y, c), (x, y, 1 - c)
        chips = [(1 - x, y), (x, 1 - y), (1 - x, 1 - y)]

        def slot(px, py, pc):
            return 4 * px + 2 * py + pc

        def copy(k, block, to, src=None):
            dst = buf.at[slot(*block)]
            return pltpu.make_async_remote_copy(
                src_ref=dst if src is None else src, dst_ref=dst, send_sem=send.at[k], recv_sem=recv.at[k],
                device_id=to, device_id_type=MESH)

        mine = pltpu.make_async_copy(p_ref, buf.at[slot(*me)], lsem)
        mine.start()
        first = [copy(0, me, sibling, src=p_ref)]
        first += [copy(1 + j, me, (*chip, c), src=p_ref) for j, chip in enumerate(chips)]
        for cp in first:
            cp.start()
        passed = [copy(4 + j, (*chip, c), sibling) for j, chip in enumerate(chips)]
        for j, chip in enumerate(chips):
            copy(1 + j, (*chip, c), me).wait_recv()
            passed[j].start()
        copy(0, sibling, me).wait_recv()
        for j, chip in enumerate(chips):
            copy(4 + j, (*chip, 1 - c), me).wait_recv()
        for cp in first + passed:
            cp.wait_send()
        mine.wait()

        def add(i, carry):
            rows = pl.ds(pl.multiple_of(i * tr, tr), tr)
            s = buf[0, rows, :]
            for d in range(1, NDEV):
                s = s + buf[d, rows, :]
            o_ref[rows, :] = s
            return carry

        lax.fori_loop(0, r // tr, add, 0)

    vm = pl.BlockSpec(memory_space=pltpu.VMEM)
    return pl.pallas_call(
        body, name=name, in_specs=[vm], out_specs=vm, out_shape=jax.ShapeDtypeStruct(p.shape, p.dtype),
        scratch_shapes=[pltpu.VMEM((NDEV,) + p.shape, p.dtype), pltpu.SemaphoreType.DMA((7,)),
                        pltpu.SemaphoreType.DMA((7,)), pltpu.SemaphoreType.DMA],
        compiler_params=_cp(),
    )(p)


HBM = pl.BlockSpec(memory_space=pltpu.HBM)
SEM = pl.BlockSpec(memory_space=pltpu.SEMAPHORE)
EFFECT = pltpu.SideEffectType.DATAFLOW_SIDE_EFFECTING


def _plan_copies(plan, s_refs, l_refs, send, recv):
    def pick(kind, a, idx):
        ref = (s_refs if kind == "s" else l_refs)[a]
        return ref if idx is None else ref.at[idx]

    return [pltpu.make_async_remote_copy(
        src_ref=pick(*src), dst_ref=pick(*dst), send_sem=send.at[i], recv_sem=recv.at[i],
        device_id=to, device_id_type=MESH) for i, (src, dst, to) in enumerate(plan(*_place()))]


def _xfer_start(srcs, lands, plan, *, name, deps=()):
    ns, nl = len(srcs), len(lands)
    nd = len(deps)
    ncopy = len(plan(0, 0, 0))

    def body(*refs):
        s_refs, l_refs = refs[:ns], refs[ns:ns + nl]
        send, recv = refs[ns + nl + nd], refs[ns + nl + nd + 1]
        token = refs[-1]
        for cp in _plan_copies(plan, s_refs, l_refs, send, recv):
            cp.start()
        token[...] = jnp.zeros_like(token)

    bufs = list(srcs) + list(lands)
    outs = pl.pallas_call(
        body, name=name,
        out_shape=(pltpu.SemaphoreType.DMA((ncopy,)), pltpu.SemaphoreType.DMA((ncopy,)),
                   *[pltpu.HBM(b.shape, b.dtype) for b in bufs], jax.ShapeDtypeStruct((8, LANE), F32)),
        in_specs=[HBM] * (ns + nl) + [ANY] * nd,
        out_specs=(SEM, SEM, *[HBM] * (ns + nl), pl.BlockSpec(memory_space=pltpu.VMEM)),
        input_output_aliases={i: 2 + i for i in range(ns + nl)},
        compiler_params=pltpu.CompilerParams(has_side_effects=EFFECT),
    )(*[pltpu.with_memory_space_constraint(b, pltpu.HBM) for b in bufs], *deps)
    return outs[0], outs[1], list(outs[2:2 + ns]), list(outs[2 + ns:2 + ns + nl]), outs[-1]


def _xfer_wait(send, recv, srcs, lands, plan, after, *, name):
    ns, nl = len(srcs), len(lands)

    def body(*refs):
        s_refs, l_refs = refs[:ns], refs[ns:ns + nl]
        send_ref, recv_ref = refs[ns + nl], refs[ns + nl + 1]
        for cp in _plan_copies(plan, s_refs, l_refs, send_ref, recv_ref):
            cp.wait_send()
            cp.wait_recv()

    bufs = list(srcs) + list(lands)
    outs = pl.pallas_call(
        body, name=name, out_shape=tuple(pltpu.HBM(b.shape, b.dtype) for b in bufs),
        in_specs=[HBM] * (ns + nl) + [SEM, SEM, ANY], out_specs=tuple([HBM] * (ns + nl)),
        input_output_aliases={i: i for i in range(ns + nl)},
        compiler_params=pltpu.CompilerParams(has_side_effects=EFFECT),
    )(*bufs, send, recv, after)
    return list(outs[:ns]), list(outs[ns:])


def _slot8(px, py, pc):
    return 4 * px + 2 * py + pc


def _slot4(r):
    return lambda px, py, pc: (2 * px + py, pl.ds(pc * r, r))


def _plan_gather_first(slots, windows=None):
    windows = windows or [None] * len(slots)

    def plan(x, y, c):
        peers = [(x, y, c), (x, y, 1 - c), (1 - x, y, c), (x, 1 - y, c), (1 - x, 1 - y, c)]
        return [(("s", a, windows[a]), ("l", a, slot(x, y, c)), to) for a, slot in enumerate(slots) for to in peers]
    return plan


def _plan_gather_pass(slots):
    def plan(x, y, c):
        chips = [(1 - x, y), (x, 1 - y), (1 - x, 1 - y)]
        return [(("l", a, slot(px, py, c)), ("l", a, slot(px, py, c)), (x, y, 1 - c))
                for a, slot in enumerate(slots) for px, py in chips]
    return plan


def _plan_rs_sibling(r):
    def plan(x, y, c):
        return [(("s", 0, (j, pl.ds((1 - c) * r, r))), ("l", 0, j), (x, y, 1 - c)) for j in range(4)]
    return plan


def _plan_rs_plane():
    def plan(x, y, c):
        chips = [(1 - x, y), (x, 1 - y), (1 - x, 1 - y)]
        return [(("s", 0, 2 * px + py), ("l", 0, k), (px, py, c)) for k, (px, py) in enumerate(chips)]
    return plan


def _add_pairs(g, rcv, cidx, *, name):
    _, r, cdim = rcv.shape
    tr = _row_tile(r, cdim, budget=4194304)
    per = r // tr

    def body(c_ref, g_ref, r_ref, o_ref):
        o_ref[...] = (g_ref[...].astype(F32) + r_ref[...].astype(F32)).astype(o_ref.dtype)

    grid_spec = pltpu.PrefetchScalarGridSpec(
        num_scalar_prefetch=1, grid=(4, per),
        in_specs=[pl.BlockSpec((None, tr, cdim), lambda j, i, c_ref: (j, c_ref[0] * per + i, 0)),
                  pl.BlockSpec((None, tr, cdim), lambda j, i, c_ref: (j, i, 0))],
        out_specs=pl.BlockSpec((None, tr, cdim), lambda j, i, c_ref: (j, i, 0)))
    return pl.pallas_call(
        body, name=name, grid_spec=grid_spec, out_shape=jax.ShapeDtypeStruct((4, r, cdim), g.dtype),
        compiler_params=_cp(dimension_semantics=("parallel", "parallel")),
    )(cidx, g, rcv)


def _pack(arrs):
    flat = jnp.concatenate([a.reshape(-1).astype(F32) for a in arrs])
    n = flat.shape[0]
    rows = -(-n // LANE)
    rows = -(-rows // 256) * 256
    return jnp.pad(flat, (0, rows * LANE - n)).reshape(rows, LANE)


def _unpack(packed, shapes):
    flat = packed.reshape(-1)
    out, o = [], 0
    for s in shapes:
        n = math.prod(s)
        out.append(flat[o:o + n].reshape(s))
        o += n
    return out


def _pad_blocks(a, w, wp):
    lead = a.shape[:-1]
    k = a.shape[-1] // w
    pads = [(0, 0)] * (len(lead) + 1) + [(0, wp - w)]
    return jnp.pad(a.reshape(*lead, k, w), pads).reshape(*lead, k * wp)


def kernel(x, ln1_w, w_in, lb_gamma, hg_norm_w, lru_conv_w, lru_conv_b, lru_wa, lru_ba, lru_wx, lru_bx, lru_lambda, lru_norm_w, w_out, ln2_w, ffn_w_up, ffn_conv_w, ffn_conv_b, ffn_w_down, final_norm_w, loss_target, m_ln1_w, m_w_in, m_lb_gamma, m_hg_norm_w, m_lru_conv_w, m_lru_conv_b, m_lru_wa, m_lru_ba, m_lru_wx, m_lru_bx, m_lru_lambda, m_lru_norm_w, m_w_out, m_ln2_w, m_ffn_w_up, m_ffn_conv_w, m_ffn_conv_b, m_ffn_w_down, m_final_norm_w, v_ln1_w, v_w_in, v_lb_gamma, v_hg_norm_w, v_lru_conv_w, v_lru_conv_b, v_lru_wa, v_lru_ba, v_lru_wx, v_lru_bx, v_lru_lambda, v_lru_norm_w, v_w_out, v_ln2_w, v_ffn_w_up, v_ffn_conv_w, v_ffn_conv_b, v_ffn_w_down, v_final_norm_w):
    nb, seq, d = x.shape
    t = nb * seq
    wmix = d // 2
    in_sh = w_in.shape[2]
    up_sh = ffn_w_up.shape[2]
    up_pad = -(-up_sh // LANE) * LANE
    hs = ffn_w_down.shape[1]
    fpad = 4 * up_pad
    cx, cy, cc = _place()
    me = 4 * cx + 2 * cy + cc
    plane = 2 * cx + cy

    def gather(shards, lands, slots, tag, deps=(), also=lambda x, y, c: []):
        first = _plan_gather_first(slots)
        plan = lambda x, y, c: first(x, y, c) + also(x, y, c)
        st = _xfer_start(shards, lands, plan, name=f"gather_{tag}_start", deps=deps)
        return (st, plan, slots, tag), st[4]

    def pass_on(g, after):
        st, plan, slots, tag = g
        _, lands = _xfer_wait(st[0], st[1], st[2], st[3], plan, after, name=f"gather_{tag}_wait")
        st2 = _xfer_start([], lands, _plan_gather_pass(slots), name=f"gather_{tag}_pass")
        return (st2, slots, tag), st2[4]

    def gathered(g, after):
        st2, slots, tag = g
        return _xfer_wait(st2[0], st2[1], [], st2[3], _plan_gather_pass(slots), after, name=f"gather_{tag}_done")[1]

    land8 = lambda s: lax.empty((NDEV,) + s.shape, s.dtype)

    win_s = w_in[0].astype(BF16)
    wout_s = w_out[0].astype(BF16)
    wup_s = jnp.pad(ffn_w_up[0], ((0, 0), (0, up_pad - up_sh))).astype(BF16)
    wdn_s = ffn_w_down[0].astype(BF16)
    fcw_s = jnp.pad(ffn_conv_w[0], ((0, 0), (0, up_pad - up_sh)))
    half = in_sh // 2
    all_rows = pl.ds(0, d)
    win_slot = lambda h: lambda px, py, pc: (_slot8(px, py, pc), all_rows, pl.ds(h * half, half))
    first_a = _plan_gather_first([win_slot(0), _slot8, _slot8], [(all_rows, pl.ds(0, half)), None, None])
    first_b = _plan_gather_first([win_slot(1)], [(all_rows, pl.ds(half, half))])
    pass_a, pass_b = _plan_gather_pass([win_slot(0), _slot8, _slot8]), _plan_gather_pass([win_slot(1)])
    sa = _xfer_start([win_s, lru_conv_w[0], fcw_s], [land8(win_s), land8(lru_conv_w[0]), land8(fcw_s)], first_a,
                     name="gather_a_start")
    sb = _xfer_start([sa[2][0]], [sa[3][0]], first_b, name="gather_b_start", deps=(sa[4],))
    go, tok = gather([wout_s], [land8(wout_s)], [_slot8], "o", deps=(sb[4],))
    gu, tok = gather([wup_s], [land8(wup_s)], [_slot8], "u", deps=(tok,))
    npad = up_pad - 2 * hs
    zero_pad = lambda x, y, c: [(("s", 1, None), ("l", 0, (j, pl.ds(2 * hs, npad))), (x, y, c)) for j in range(4)]
    gd, tok = gather([wdn_s] + ([jnp.zeros((npad, d), BF16)] if npad else []), [lax.empty((4, up_pad, d), BF16)],
                     [_slot4(hs)], "d", deps=(tok,), also=zero_pad if npad else lambda x, y, c: [])
    fcb = _pad_blocks(ffn_conv_b, up_sh, up_pad)
    wa_b, wx_b = lru_wa[0].astype(BF16), lru_wx[0].astype(BF16)

    xf = x.reshape(t, d)
    hn = _rms_fwd(xf, ln1_w, name="ln1_fwd", deps=(tok,))
    srcs_a, lands_a = _xfer_wait(sa[0], sa[1], [sb[2][0], sa[2][1], sa[2][2]], [sb[3][0], sa[3][1], sa[3][2]],
                                 first_a, hn, name="gather_a_wait")
    pa = _xfer_start([], lands_a, pass_a, name="gather_a_pass")
    _, (win_l, lcw_g, fcw_g) = _xfer_wait(pa[0], pa[1], [], pa[3], pass_a, pa[4], name="gather_a_done")
    lcw = lcw_g.transpose(1, 0, 2).reshape(lru_conv_w.shape[1], wmix)
    fcw = fcw_g.transpose(1, 0, 2).reshape(ffn_conv_w.shape[1], 2 * fpad)
    proj = _mm(hn, win_l, kind="nn", out_dtype=F32, name="in_proj_a", tm_cap=512, tk_cap=d, col_part=(0, 2))
    _, (win_l,) = _xfer_wait(sb[0], sb[1], [srcs_a[0]], [win_l], first_b, proj, name="gather_b_wait")
    pb = _xfer_start([], [win_l], pass_b, name="gather_b_pass")
    _, (win_g,) = _xfer_wait(pb[0], pb[1], [], pb[3], pass_b, pb[4], name="gather_b_done")
    proj = _mm(hn, win_g, kind="nn", out_dtype=F32, name="in_proj_b", tm_cap=512, tk_cap=d, col_part=(1, 2),
               into=proj)
    go, tok = pass_on(go, proj)
    o_hg, o_pre, states = _hgrn_fwd(proj, lb_gamma, hg_norm_w, nb=nb, seq=seq, name="hgrn_fwd")
    (wout_g,) = gathered(go, o_pre)
    wout_f = wout_g.reshape(1, d, d)
    p_lru, h_lru = _lru_fwd(proj, lcw, lru_conv_b, wa_b, lru_ba, wx_b, lru_bx, lru_lambda,
                            nb=nb, seq=seq, name="lru_fwd", deps=(tok,))
    o_lru = _rms_fwd(p_lru, lru_norm_w, name="lru_norm_fwd")
    gu, tok = pass_on(gu, o_lru)
    mix = jnp.concatenate([o_hg, o_lru], axis=1)
    h1 = _mm(mix, wout_f, kind="nn", out_dtype=F32, name="out_proj", res=xf, tm_cap=512, tk_cap=d, deps=(tok,))
    hn2 = _rms_fwd(h1, ln2_w, name="ln2_fwd")
    (wup_g,) = gathered(gu, hn2)
    up = _mm(hn2, wup_g, kind="nn", out_dtype=F32, name="ffn_up", tm_cap=512, tn_cap=up_pad // 2, tk_cap=d)
    gd, tok = pass_on(gd, up)
    act = _ffn_act_fwd(up, fcw, fcb, nb=nb, seq=seq, name="ffn_act_fwd", deps=(tok,))
    (wdn_g,) = gathered(gd, act)
    wdn_f = wdn_g.reshape(1, fpad, d)
    h2 = _mm(act, wdn_f, kind="nn", out_dtype=F32, name="ffn_down", res=h1, tm_cap=512, tn_cap=1024,
             tk_cap=2 * up_pad)

    dh2, dh2_b, g_fnw, loss_part = _loss_head(h2, final_norm_w.reshape(1, d), loss_target.reshape(t, d),
                                              name="loss_head")
    loss = lax.psum(loss_part[0, 0], ("x", "y", "c"))

    cidx = jnp.reshape(cc, (1,)).astype(jnp.int32)

    def to_sibling(g4, r, tag, deps=()):
        land = lax.empty((4, r, g4.shape[2]), g4.dtype)
        st = _xfer_start([g4], [land], _plan_rs_sibling(r), name=f"rs_sib_start_{tag}", deps=deps)
        return (st, r, tag), st[4]

    def sibling_sum(rs, after):
        st, r, tag = rs
        (g4,), (rcv,) = _xfer_wait(st[0], st[1], st[2], st[3], _plan_rs_sibling(r), after, name=f"rs_sib_wait_{tag}")
        psum = _add_pairs(g4, rcv, cidx, name=f"rs_add_{tag}")
        land = lax.empty((3,) + psum.shape[1:], psum.dtype)
        st2 = _xfer_start([psum], [land], _plan_rs_plane(), name=f"rs_plane_start_{tag}")
        return (st2, tag), st2[4]

    dact = _mm(dh2_b, wdn_f, kind="nt", out_dtype=F32, name="ffn_down_dx", tm_cap=512, tn_cap=up_pad // 2, tk_cap=d)
    g_wdn = _mm(act, dh2_b, kind="tn", out_dtype=BF16, name="ffn_down_dw", tm_cap=1024, tn_cap=1024, tk_cap=t)
    rs_d, tok = to_sibling(g_wdn.reshape(4, up_pad, d), hs, "down")
    dup, s_g, s_v = _ffn_act_bwd(dact, up, fcw, fcb, nb=nb, seq=seq, name="ffn_act_bwd", deps=(tok,))
    pl_d, tok = sibling_sum(rs_d, dup)
    dhn2 = _mm(dup, wup_g, kind="nt", out_dtype=F32, name="ffn_up_dx", tm_cap=512, tn_cap=1024, tk_cap=up_pad,
               k_blocks=2, deps=(tok,))
    g_wup = _mm(dup, hn2, kind="tn", out_dtype=BF16, name="ffn_up_dw", tm_cap=1024, tn_cap=1024, tk_cap=t)
    rs_u, tok = to_sibling(g_wup.reshape(4, 2 * up_pad, d), up_pad, "up")
    dh1, dh1_b, g_ln2 = _rms_bwd(dhn2, h1, ln2_w, name="ln2_bwd", extra=dh2, want_bf16=True, deps=(tok,))
    dmix = _mm(dh1_b, wout_f, kind="nt", out_dtype=F32, name="out_proj_dx", tm_cap=512, tn_cap=1024, tk_cap=d)
    pl_u, tok = sibling_sum(rs_u, dmix)
    g_wout = _mm(mix, dh1_b, kind="tn", out_dtype=BF16, name="out_proj_dw", tn_cap=1024, tk_cap=t, deps=(tok,))
    rs_o, tok = to_sibling(g_wout.reshape(4, d // 4, d), d // NDEV, "out")
    dp_lru, g_lnw = _rms_bwd(dmix, p_lru, lru_norm_w, name="lru_norm_bwd", dy_cb=1, deps=(tok,))
    dxr, dyr, s_lru, g_wa, g_wx = _lru_bwd(proj, lcw, lru_conv_b, wa_b, lru_ba, wx_b, lru_bx, lru_lambda,
                                           h_lru, dp_lru, nb=nb, seq=seq, name="lru_bwd")
    pl_o, tok = sibling_sum(rs_o, dxr)
    dqr, dfr, dir_, dgr, s_hg = _hgrn_bwd(proj, lb_gamma, hg_norm_w, o_pre, states, dmix,
                                          nb=nb, seq=seq, name="hgrn_bwd", deps=(tok,))
    dproj = jnp.concatenate([dqr, dfr, dir_, dgr, dxr, dyr], axis=1)
    g_win = _mm(hn, dproj, kind="tn", out_dtype=BF16, name="in_proj_dw", out_blocks=NDEV, tm_cap=512, tn_cap=in_sh,
                tk_cap=t)
    rs_i, tok = to_sibling(g_win.reshape(4, 2 * d, in_sh), d, "in")
    dhn = _mm(dproj, win_g, kind="nt", out_dtype=F32, name="in_proj_dx", tm_cap=512, tn_cap=1024, tk_cap=in_sh,
              k_blocks=4, deps=(tok,))
    grad_x, g_ln1 = _rms_bwd(dhn, xf, ln1_w, name="ln1_bwd", extra=dh1)

    fcw_parts = jnp.concatenate([s_g[0:3], s_v[0:3]], axis=1).reshape(3, NDEV, up_pad)[:, :, :up_sh]
    fcb_part = jnp.concatenate([s_g[3:4], s_v[3:4]], axis=1).reshape(NDEV, up_pad)[:, :up_sh]
    small_parts = [g_ln1, s_hg[0:2], s_hg[2:3], s_lru[0:4], s_lru[4:5], g_wa, s_lru[5:6], g_wx, s_lru[6:7],
                   s_lru[7:8], g_lnw, g_ln2, fcw_parts, fcb_part, g_fnw]
    reduced = _all_reduce_packed(_pack(small_parts), name="allreduce_small")
    pl_i, _ = sibling_sum(rs_i, reduced)
    summed = _unpack(reduced, [p.shape for p in small_parts])
    (s_ln1, s_lbg, s_hgn, s_lcw, s_lcb, s_wa, s_ba, s_wx, s_bx, s_lam, s_lnw, s_ln2, s_fcw, s_fcb, s_fnw) = summed
    sh_lcw = lru_conv_w.shape[2]
    g_small = {
        "ln1_w": s_ln1, "lb_gamma": s_lbg, "hg_norm_w": s_hgn,
        "lru_conv_w": lax.dynamic_slice_in_dim(s_lcw, me * sh_lcw, sh_lcw, axis=1),
        "lru_conv_b": s_lcb, "lru_wa": s_wa, "lru_ba": s_ba, "lru_wx": s_wx, "lru_bx": s_bx,
        "lru_lambda": s_lam, "lru_norm_w": s_lnw, "ln2_w": s_ln2,
        "ffn_conv_w": lax.dynamic_index_in_dim(s_fcw, me, 1, keepdims=False),
        "ffn_conv_b": s_fcb, "final_norm_w": s_fnw,
    }
    w_small = {"ln1_w": ln1_w, "lb_gamma": lb_gamma, "hg_norm_w": hg_norm_w, "lru_conv_w": lru_conv_w,
               "lru_conv_b": lru_conv_b, "lru_wa": lru_wa, "lru_ba": lru_ba, "lru_wx": lru_wx, "lru_bx": lru_bx,
               "lru_lambda": lru_lambda, "lru_norm_w": lru_norm_w, "ln2_w": ln2_w, "ffn_conv_w": ffn_conv_w,
               "ffn_conv_b": ffn_conv_b, "final_norm_w": final_norm_w}
    m_small = {"ln1_w": m_ln1_w, "lb_gamma": m_lb_gamma, "hg_norm_w": m_hg_norm_w, "lru_conv_w": m_lru_conv_w,
               "lru_conv_b": m_lru_conv_b, "lru_wa": m_lru_wa, "lru_ba": m_lru_ba, "lru_wx": m_lru_wx,
               "lru_bx": m_lru_bx, "lru_lambda": m_lru_lambda, "lru_norm_w": m_lru_norm_w, "ln2_w": m_ln2_w,
               "ffn_conv_w": m_ffn_conv_w, "ffn_conv_b": m_ffn_conv_b, "final_norm_w": m_final_norm_w}
    v_small = {"ln1_w": v_ln1_w, "lb_gamma": v_lb_gamma, "hg_norm_w": v_hg_norm_w, "lru_conv_w": v_lru_conv_w,
               "lru_conv_b": v_lru_conv_b, "lru_wa": v_lru_wa, "lru_ba": v_lru_ba, "lru_wx": v_lru_wx,
               "lru_bx": v_lru_bx, "lru_lambda": v_lru_lambda, "lru_norm_w": v_lru_norm_w, "ln2_w": v_ln2_w,
               "ffn_conv_w": v_ffn_conv_w, "ffn_conv_b": v_ffn_conv_b, "final_norm_w": v_final_norm_w}
    names = list(w_small)
    shapes = [w_small[k].shape for k in names]
    g_small = {k: g_small[k].reshape(w_small[k].shape) for k in names}
    packed = _adamw_packed(_pack([w_small[k] for k in names]), _pack([g_small[k] for k in names]),
                           _pack([m_small[k] for k in names]), _pack([v_small[k] for k in names]),
                           name="adamw_small")
    d_small, nm_small, nv_small = ({k: a for k, a in zip(names, _unpack(pk, shapes))} for pk in packed)

    def finish(pl_x, after, trim=lambda a: a):
        st, tag = pl_x
        (psum,), (rb,) = _xfer_wait(st[0], st[1], st[2], st[3], _plan_rs_plane(), after, name=f"rs_plane_wait_{tag}")
        return trim(lax.dynamic_index_in_dim(psum, plane, 0, keepdims=False)), trim(rb)

    upd_dn = _adamw_shard(ffn_w_down, m_ffn_w_down, v_ffn_w_down, *finish(pl_d, packed[0]), name="adamw_w_down")
    tr3 = lambda a: a.transpose(0, 2, 1)
    upd_up_t = _adamw_shard(tr3(ffn_w_up), tr3(m_ffn_w_up), tr3(v_ffn_w_up), *finish(pl_u, upd_dn[0]),
                            name="adamw_w_up")
    upd_up = [tr3(a) for a in upd_up_t]
    upd_out = _adamw_shard(w_out, m_w_out, v_w_out, *finish(pl_o, upd_up_t[0]), name="adamw_w_out")
    upd_in = _adamw_shard(w_in, m_w_in, v_w_in, *finish(pl_i, upd_out[0]), name="adamw_w_in")

    grads, deltas, new_m, new_v = dict(g_small), dict(d_small), dict(nm_small), dict(nv_small)
    for k, upd in (("w_in", upd_in), ("w_out", upd_out), ("ffn_w_up", upd_up), ("ffn_w_down", upd_dn)):
        grads[k], deltas[k], new_m[k], new_v[k] = upd
    order = ["ln1_w", "w_in", "lb_gamma", "hg_norm_w", "lru_conv_w", "lru_conv_b", "lru_wa", "lru_ba", "lru_wx",
             "lru_bx", "lru_lambda", "lru_norm_w", "w_out", "ln2_w", "ffn_w_up", "ffn_conv_w", "ffn_conv_b",
             "ffn_w_down", "final_norm_w"]
    return (loss, grad_x.reshape(nb, seq, d), *[grads[k] for k in order], *[deltas[k] for k in order],
            *[new_m[k] for k in order], *[new_v[k] for k in order])
```

```python
import math

import jax
import jax.numpy as jnp
from jax import lax
from jax.experimental import pallas as pl
from jax.experimental.pallas import tpu as pltpu

F32, BF16 = jnp.float32, jnp.bfloat16
EPS = 1e-6
HEAD = 128
CHUNK = 64
SUB = 16
NSUB = CHUNK // SUB
LRU_C = 8.0
LANE = 128
NDEV = 8
ADAM_LR, ADAM_B1, ADAM_B2, ADAM_EPS, ADAM_WD, ADAM_STEP = 0.001, 0.9, 0.999, 1e-08, 0.01, 10
MESH = pl.DeviceIdType.MESH
ANY = pl.BlockSpec(memory_space=pl.ANY)
VMEM_LIMIT = 56 * 1024 * 1024


def _cp(**kw):
    return pltpu.CompilerParams(vmem_limit_bytes=VMEM_LIMIT, **kw)


def _tile(n, cap, mult=LANE):
    best = None
    for t in range(mult, min(n, cap) + 1, mult):
        if n % t == 0:
            best = t
    return best if best is not None else n


def _row_tile(r, cdim, budget=262144):
    return _tile(r, max(16, budget // cdim), 16)


def _sigmoid(x):
    return jax.nn.sigmoid(x)


def _dsilu(x, s):
    return s * (1.0 + x * (1.0 - s))


def _iota_rows(n, w=LANE):
    return lax.broadcasted_iota(jnp.int32, (n, w), 0)


def _lane_groups(n, want=2):
    while n % want:
        want //= 2
    return want


def _group_views(refs, g, kinds):
    assert len(refs) == len(kinds)
    cols = pl.ds(g * HEAD, HEAD)
    return [r.at[:, cols] if kind == "l" else r.at[g] for r, kind in zip(refs, kinds)]


def _shift_down(prev8, xt, k):
    cat = jnp.concatenate([prev8, xt], axis=0)
    return pltpu.roll(cat, k, 0)[8:]


def _shift_up(xt, next8, k):
    cat = jnp.concatenate([xt, next8], axis=0)
    n = cat.shape[0]
    return pltpu.roll(cat, n - k, 0)[: xt.shape[0]]


def _scan_fwd(a, u):
    n = a.shape[0]
    row = _iota_rows(n, a.shape[1])
    k = 1
    while k < n:
        keep = row >= k
        a_s = jnp.where(keep, pltpu.roll(a, k, 0), 1.0)
        u_s = jnp.where(keep, pltpu.roll(u, k, 0), 0.0)
        u = a * u_s + u
        a = a * a_s
        k *= 2
    return a, u


def _scan_bwd(a, u):
    n = a.shape[0]
    row = _iota_rows(n, a.shape[1])
    k = 1
    while k < n:
        keep = row < n - k
        a_s = jnp.where(keep, pltpu.roll(a, n - k, 0), 1.0)
        u_s = jnp.where(keep, pltpu.roll(u, n - k, 0), 0.0)
        u = a * u_s + u
        a = a * a_s
        k *= 2
    return a, u


def _cumsum_fwd(u):
    n = u.shape[0]
    row = _iota_rows(n, u.shape[1])
    k = 1
    while k < n:
        u = u + jnp.where(row >= k, pltpu.roll(u, k, 0), 0.0)
        k *= 2
    return u


def _cumsum_bwd(u):
    n = u.shape[0]
    row = _iota_rows(n, u.shape[1])
    k = 1
    while k < n:
        u = u + jnp.where(row < n - k, pltpu.roll(u, n - k, 0), 0.0)
        k *= 2
    return u


def _dot(a, b, dims):
    return lax.dot_general(a.astype(BF16), b.astype(BF16), (dims, ((), ())), preferred_element_type=F32)


NN = ((1,), (0,))
NT = ((1,), (1,))
TN = ((0,), (0,))


def _pcall(body, *, n_in, in_specs, args, deps=(), **kw):
    nd = len(deps)
    if nd:
        inner = body

        def body(*refs):
            return inner(*refs[:n_in], *refs[n_in + nd:])

        in_specs = list(in_specs) + [ANY] * nd
        args = list(args) + list(deps)
    return pl.pallas_call(body, in_specs=in_specs, **kw)(*args)


def _mm(a, b3, *, kind, out_dtype, name, res=None, tm_cap=1024, tn_cap=1536, tk_cap=1024, out_blocks=1, k_blocks=1,
        col_part=None, into=None, deps=()):
    out_ix = lambda i, j, k: (i, j)
    if kind == "nn":
        m, kdim = a.shape
        nb, _, nsh = b3.shape
        n = nb * nsh
        tm, tn, tk = _tile(m, tm_cap), _tile(nsh, tn_cap), _tile(kdim, tk_cap)
        per = nsh // tn
        a_blk, a_ix = (tm, tk), lambda i, j, k: (i, k)
        b_blk, b_ix = (None, tk, tn), lambda i, j, k: (j // per, k, j % per)
        if col_part is not None:
            h, parts = col_part
            tn = nsh // parts
            b_blk, b_ix, out_ix = (None, tk, tn), lambda i, j, k: (j, k, h), lambda i, j, k: (i, j * parts + h)
        dims = NN
    elif kind == "nt":
        m, kdim = a.shape[-2], a.shape[-1] * (a.shape[0] if a.ndim == 3 else 1)
        nb, n, ksh = b3.shape
        tm, tn, tk = _tile(m, tm_cap), _tile(n, tn_cap), _tile(ksh, tk_cap)
        per = ksh // tk
        a_blk, a_ix = (tm, tk), lambda i, j, k: (i, k)
        b_blk, b_ix = (None, tn, tk), lambda i, j, k: (k // per, j, k % per)
        if k_blocks > 1:
            assert tk == ksh and nb % k_blocks == 0
            tk = k_blocks * ksh
            a_blk, b_blk, b_ix = (tm, tk), (k_blocks, tn, ksh), lambda i, j, k: (k, j, 0)
        if a.ndim == 3:
            aper = a.shape[2] // tk
            a_blk, a_ix = (None,) + a_blk, lambda i, j, k: (k // aper, i, k % aper)
        dims = NT
    else:
        kdim, m = a.shape[-2], a.shape[-1] * (a.shape[0] if a.ndim == 3 else 1)
        n = b3.shape[1]
        nsh = n // out_blocks
        tm, tn, tk = _tile(a.shape[-1], tm_cap), _tile(nsh, tn_cap), _tile(kdim, tk_cap)
        per = nsh // tn
        a_blk, a_ix = (tk, tm), lambda i, j, k: (k, i)
        b_blk, b_ix = (tk, tn), lambda i, j, k: (k, j)
        if a.ndim == 3:
            aper = a.shape[2] // tm
            a_blk, a_ix = (None,) + a_blk, lambda i, j, k: (i // aper, k, i % aper)
        dims = TN
    nk = kdim // tk
    j_outer = nk == 1
    nj = nb if col_part is not None else n // tn
    grid = (nj, m // tm, nk) if j_outer else (m // tm, nj, nk)
    at = (lambda f: lambda g0, g1, k: f(g1, g0, k)) if j_outer else (lambda f: f)
    a_spec, b_spec = pl.BlockSpec(a_blk, at(a_ix)), pl.BlockSpec(b_blk, at(b_ix))
    n_args = 2 + (res is not None) + (into is not None)

    def body(*refs):
        a_ref, b_ref = refs[:2]
        r_ref = refs[2] if res is not None else None
        o_ref = refs[n_args]
        if k_blocks > 1:
            w = b_ref.shape[2]
            part = sum(lax.dot_general(a_ref[:, s * w:(s + 1) * w], b_ref[s], (dims, ((), ())),
                                       preferred_element_type=F32) for s in range(k_blocks))
        else:
            part = lax.dot_general(a_ref[...], b_ref[...], (dims, ((), ())), preferred_element_type=F32)
        if nk == 1:
            o_ref[...] = (part if res is None else part + r_ref[...]).astype(o_ref.dtype)
            return
        acc = refs[-1]
        k = pl.program_id(2)

        @pl.when(k == 0)
        def _():
            acc[...] = part

        @pl.when(k > 0)
        def _():
            acc[...] += part

        @pl.when(k == nk - 1)
        def _():
            r = acc[...]
            if res is not None:
                r = r + r_ref[...]
            o_ref[...] = r.astype(o_ref.dtype)

    in_specs = [a_spec, b_spec]
    args = [a, b3]
    if res is not None:
        in_specs.append(pl.BlockSpec((tm, tn), at(out_ix)))
        args.append(res)
    aliases = {}
    if into is not None:
        aliases = {len(args): 0}
        in_specs.append(ANY)
        args.append(into)
    if kind == "tn":
        out_shape = jax.ShapeDtypeStruct((out_blocks, m, nsh), out_dtype)
        out_spec = pl.BlockSpec((None, tm, tn), at(lambda i, j, k: (j // per, i, j % per)))
    else:
        out_shape = jax.ShapeDtypeStruct((m, n), out_dtype)
        out_spec = pl.BlockSpec((tm, tn), at(out_ix))
    return _pcall(
        body, n_in=len(args), in_specs=in_specs, args=args, deps=deps, input_output_aliases=aliases,
        name=name, grid=grid, out_specs=out_spec, out_shape=out_shape,
        scratch_shapes=[pltpu.VMEM((tm, tn), F32)] if nk > 1 else [],
        compiler_params=_cp(dimension_semantics=("parallel", "parallel", "arbitrary")))


def _rms_fwd(x, w, *, name, tm=256, deps=()):
    t, d = x.shape
    tm = _tile(t, tm, 16)

    def body(x_ref, w_ref, o_ref):
        xv = x_ref[...]
        r = lax.rsqrt(jnp.mean(xv * xv, axis=-1, keepdims=True) + EPS)
        o_ref[...] = ((xv * r) * w_ref[...]).astype(o_ref.dtype)

    return _pcall(
        body, n_in=2, args=[x, w], deps=deps, name=name, grid=(t // tm,),
        in_specs=[pl.BlockSpec((tm, d), lambda i: (i, 0)), pl.BlockSpec((1, d), lambda i: (0, 0))],
        out_specs=pl.BlockSpec((tm, d), lambda i: (i, 0)),
        out_shape=jax.ShapeDtypeStruct((t, d), BF16), compiler_params=_cp())


def _rms_bwd(dy, x, w, *, name, extra=None, dy_cb=0, want_bf16=False, tm=256, deps=()):
    t, d = x.shape
    tm = _tile(t, tm, 16)

    def body(*refs):
        refs = list(refs)
        dy_ref, x_ref, w_ref = refs[:3]
        e_ref = refs[3] if extra is not None else None
        outs = refs[4:] if extra is not None else refs[3:]
        dx_ref = outs[0]
        dxb_ref = outs[1] if want_bf16 else None
        dw_ref = outs[-1]
        i = pl.program_id(0)
        xv = x_ref[...]
        r = lax.rsqrt(jnp.mean(xv * xv, axis=-1, keepdims=True) + EPS)
        nh = xv * r
        dyv = dy_ref[...]
        dn = dyv * w_ref[...]
        dx = r * (dn - nh * jnp.mean(dn * nh, axis=-1, keepdims=True))
        if extra is not None:
            dx = dx + e_ref[...]
        dx_ref[...] = dx
        if want_bf16:
            dxb_ref[...] = dx.astype(BF16)
        part = jnp.sum(dyv * nh, axis=0, keepdims=True)

        @pl.when(i == 0)
        def _():
            dw_ref[...] = part

        @pl.when(i > 0)
        def _():
            dw_ref[...] += part

    row = pl.BlockSpec((tm, d), lambda i: (i, 0))
    in_specs = [pl.BlockSpec((tm, d), lambda i: (i, dy_cb)), row, pl.BlockSpec((1, d), lambda i: (0, 0))]
    args = [dy, x, w]
    if extra is not None:
        in_specs.append(row)
        args.append(extra)
    out_shape = [jax.ShapeDtypeStruct((t, d), F32)]
    out_specs = [row]
    if want_bf16:
        out_shape.append(jax.ShapeDtypeStruct((t, d), BF16))
        out_specs.append(row)
    out_shape.append(jax.ShapeDtypeStruct((1, d), F32))
    out_specs.append(pl.BlockSpec((1, d), lambda i: (0, 0)))
    return _pcall(
        body, n_in=len(args), in_specs=in_specs, args=args, deps=deps,
        name=name, grid=(t // tm,), out_specs=out_specs, out_shape=out_shape,
        compiler_params=_cp(dimension_semantics=("arbitrary",)))


def _loss_head(h, w, tgt, *, name, tm=256):
    t, d = h.shape
    tm = _tile(t, tm, 16)

    def body(h_ref, w_ref, t_ref, dh_ref, dhb_ref, dw_ref, loss_ref):
        i = pl.program_id(0)
        xv = h_ref[...]
        wv = w_ref[...]
        r = lax.rsqrt(jnp.mean(xv * xv, axis=-1, keepdims=True) + EPS)
        nh = xv * r
        e = nh * wv - t_ref[...]
        part_loss = jnp.full((1, LANE), 0.5 * jnp.sum(jnp.mean(e * e, axis=-1, keepdims=True)), F32)
        dyv = e * (1.0 / d)
        dn = dyv * wv
        dx = r * (dn - nh * jnp.mean(dn * nh, axis=-1, keepdims=True))
        dh_ref[...] = dx
        dhb_ref[...] = dx.astype(BF16)
        part = jnp.sum(dyv * nh, axis=0, keepdims=True)

        @pl.when(i == 0)
        def _():
            dw_ref[...] = part
            loss_ref[...] = part_loss

        @pl.when(i > 0)
        def _():
            dw_ref[...] += part
            loss_ref[...] += part_loss

    row = pl.BlockSpec((tm, d), lambda i: (i, 0))
    vec = pl.BlockSpec((1, d), lambda i: (0, 0))
    return pl.pallas_call(
        body, name=name, grid=(t // tm,), in_specs=[row, vec, row],
        out_specs=[row, row, vec, pl.BlockSpec((1, LANE), lambda i: (0, 0))],
        out_shape=[jax.ShapeDtypeStruct((t, d), F32), jax.ShapeDtypeStruct((t, d), BF16),
                   jax.ShapeDtypeStruct((1, d), F32), jax.ShapeDtypeStruct((1, LANE), F32)],
        compiler_params=_cp(dimension_semantics=("arbitrary",)),
    )(h, w, tgt)


def _lower_bound(lbg_ref):
    g0, g1 = lbg_ref[0:1, :], lbg_ref[1:2, :]
    m = jnp.maximum(g0, g1)
    e0, e1 = jnp.exp(g0 - m), jnp.exp(g1 - m)
    return e0 / (e0 + e1)


def _seg_bounds():
    offs, o = {}, 0
    for i in range(1, NSUB):
        offs[i] = (o, o + SUB * i)
        o += SUB * i
    return offs, o


def _pad_rows(x, n):
    if x.shape[0] == n:
        return x
    return jnp.concatenate([x, jnp.zeros((n - x.shape[0], x.shape[1]), x.dtype)], axis=0)


def _offdiag_mask():
    offs, total = _seg_bounds()
    padded = -(-total // LANE) * LANE
    rsub = lax.broadcasted_iota(jnp.int32, (CHUNK, padded), 0) // SUB
    col = lax.broadcasted_iota(jnp.int32, (CHUNK, padded), 1)
    cseg = jnp.zeros((CHUNK, padded), jnp.int32)
    for i in range(1, NSUB):
        cseg = cseg + (col >= offs[i][0]).astype(jnp.int32)
    return (rsub == cseg) & (col < total)


def _offdiag_setup(q, k, b, v, b_c, mask):
    offs, total = _seg_bounds()
    padded = -(-total // LANE) * LANE
    eq_parts = [jnp.zeros((SUB, HEAD), F32)]
    ek_parts, k_parts, v_parts = [], [], []
    for i in range(1, NSUB):
        r_i = b_c[SUB * i - 1:SUB * i, :]
        eq_parts.append(jnp.exp(b[SUB * i:SUB * (i + 1)] - r_i))
        ek_parts.append(jnp.exp(r_i - b[0:SUB * i]))
        k_parts.append(k[0:SUB * i])
        v_parts.append(v[0:SUB * i])
    eq = jnp.concatenate(eq_parts, axis=0)
    ek = _pad_rows(jnp.concatenate(ek_parts, axis=0), padded)
    kt = _pad_rows(jnp.concatenate(k_parts, axis=0), padded) * ek
    vs = _pad_rows(jnp.concatenate(v_parts, axis=0), padded)
    qt = q * eq
    a = jnp.where(mask, _dot(qt, kt, NT), 0.0)
    return offs, eq, ek, kt, vs, qt, a


def _hgrn_fwd(proj, lbg, nw, *, nb, seq, name):
    t = proj.shape[0]
    w = lbg.shape[1]
    nh = w // HEAD
    nc = seq // CHUNK
    ng = _lane_groups(nh)

    def head(mask, q_ref, f_ref, i_ref, g_ref, lbg_ref, nw_ref, ohg_ref, opre_ref, st_ref, k_c, b_c, v_c, st):
        lb = _lower_bound(lbg_ref)
        nwv = nw_ref[...]
        st[...] = jnp.zeros_like(st)

        def chunk(c, carry):
            rows = pl.ds(pl.multiple_of(c * CHUNK, CHUNK), CHUNK)
            qr = q_ref[rows, :]
            q = qr * _sigmoid(qr)
            f = lb + (1.0 - lb) * _sigmoid(f_ref[rows, :])
            k = 1.0 - f
            b = _cumsum_fwd(jnp.log(f))
            v = i_ref[rows, :]
            k_c[...] = k
            b_c[...] = b
            v_c[...] = v
            s_t = st[...]
            st_ref[c] = s_t
            o = _dot(q * jnp.exp(b), s_t, NT)
            _, _, _, _, vs, _, a = _offdiag_setup(q, k, b, v, b_c, mask)
            o = o + _dot(a, vs, NN)
            diag = []
            for i in range(NSUB):
                accs = [jnp.zeros((8, HEAD), F32) for _ in range(SUB // 8)]
                for j in range(SUB):
                    r = SUB * i + j
                    bj, kj, vj = b_c[r:r + 1, :], k_c[r:r + 1, :], v_c[r:r + 1, :]
                    for p in range(j // 8, SUB // 8):
                        lo = SUB * i + 8 * p
                        d = jnp.exp(b[lo:lo + 8] - bj)
                        if 8 * p < j:
                            d = jnp.where(_iota_rows(8) + 8 * p >= j, d, 0.0)
                        s = jnp.sum(q[lo:lo + 8] * d * kj, axis=-1, keepdims=True)
                        accs[p] = accs[p] + s * vj
                diag.extend(accs)
            o = o + jnp.concatenate(diag, axis=0)
            bl = b_c[CHUNK - 1:CHUNK, :]
            kb = k * jnp.exp(bl - b)
            st[...] = s_t * jnp.exp(bl) + _dot(v, kb, TN)
            opre_ref[rows, :] = o
            rn = lax.rsqrt(jnp.mean(o * o, axis=-1, keepdims=True) + EPS)
            gr = g_ref[rows, :]
            ohg_ref[rows, :] = (((o * rn) * nwv) * (gr * _sigmoid(gr))).astype(BF16)
            return carry

        return chunk

    def body(*refs):
        mask = _offdiag_mask()
        chunks = [head(mask, *_group_views(refs, g, "llllllll" + "ggggg")) for g in range(ng)]

        def step(c, carry):
            for chunk in chunks:
                chunk(c, carry)
            return carry

        lax.fori_loop(0, nc, step, 0)

    gw = ng * HEAD

    def col(off):
        return pl.BlockSpec((seq, gw), lambda h, b: (b, off * (nh // ng) + h))

    vec = lambda r: pl.BlockSpec((r, gw), lambda h, b: (0, h))
    out_blk = pl.BlockSpec((seq, gw), lambda h, b: (b, h))
    return pl.pallas_call(
        body, name=name, grid=(nh // ng, nb),
        in_specs=[col(0), col(1), col(2), col(3), vec(2), vec(1)],
        out_specs=[out_blk, out_blk, pl.BlockSpec((None, ng, nc, HEAD, HEAD), lambda h, b: (b, h, 0, 0, 0))],
        out_shape=[jax.ShapeDtypeStruct((t, w), BF16), jax.ShapeDtypeStruct((t, w), F32),
                   jax.ShapeDtypeStruct((nb, nh, nc, HEAD, HEAD), F32)],
        scratch_shapes=[pltpu.VMEM((ng, CHUNK, HEAD), F32)] * 3 + [pltpu.VMEM((ng, HEAD, HEAD), F32)],
        compiler_params=_cp(dimension_semantics=("parallel", "parallel")),
    )(proj, proj, proj, proj, lbg, nw)


def _hgrn_bwd(proj, lbg, nw, opre, states, dmix, *, nb, seq, name, deps=()):
    t = proj.shape[0]
    w = lbg.shape[1]
    nh = w // HEAD
    nc = seq // CHUNK
    ng = _lane_groups(nh)

    def head(mask, q_ref, f_ref, i_ref, g_ref, lbg_ref, nw_ref, opre_ref, st_ref, dm_ref,
             dq_ref, df_ref, di_ref, dg_ref, small_ref, k_c, b_c, v_c, dst, dlb_s, dwn_s):
        bi = pl.program_id(1)
        lb = _lower_bound(lbg_ref)
        nwv = nw_ref[...]
        dst[...] = jnp.zeros_like(dst)

        @pl.when(bi == 0)
        def _():
            dlb_s[...] = jnp.zeros_like(dlb_s)
            dwn_s[...] = jnp.zeros_like(dwn_s)

        def chunk(it, carry):
            c = nc - 1 - it
            rows = pl.ds(pl.multiple_of(c * CHUNK, CHUNK), CHUNK)
            qr = q_ref[rows, :]
            sq = _sigmoid(qr)
            q = qr * sq
            sg = _sigmoid(f_ref[rows, :])
            f = lb + (1.0 - lb) * sg
            k = 1.0 - f
            b = _cumsum_fwd(jnp.log(f))
            v = i_ref[rows, :]
            k_c[...] = k
            b_c[...] = b
            v_c[...] = v
            o = opre_ref[rows, :]
            rn = lax.rsqrt(jnp.mean(o * o, axis=-1, keepdims=True) + EPS)
            nhat = o * rn
            dm = dm_ref[rows, :]
            gr = g_ref[rows, :]
            sgr = _sigmoid(gr)
            dnw = dm * (gr * sgr)
            dg_ref[rows, :] = (dm * (nhat * nwv) * _dsilu(gr, sgr)).astype(BF16)
            dwn_s[...] += jnp.sum(dnw * nhat, axis=0, keepdims=True)
            dn = dnw * nwv
            do = rn * (dn - nhat * jnp.mean(dn * nhat, axis=-1, keepdims=True))
            s_t = st_ref[c]
            ds = dst[...]
            eb = jnp.exp(b)
            qb = q * eb
            bl = b_c[CHUNK - 1:CHUNK, :]
            ebl = jnp.exp(bl)
            kdec = jnp.exp(bl - b)
            kb = k * kdec
            dqb = _dot(do, s_t, NN)
            dkb = _dot(v, ds, NN)
            dv = _dot(kb, ds, NT)
            d_ebl = jnp.sum(ds * s_t, axis=0, keepdims=True)
            dst[...] = ds * ebl + _dot(do, qb, TN)
            dq = dqb * eb
            dk = dkb * kdec
            t_kb = dkb * kb
            db = dqb * qb - t_kb
            db_last = jnp.sum(t_kb, axis=0, keepdims=True) + d_ebl * ebl
            offs, eq, ek, kt, vs, qt, a = _offdiag_setup(q, k, b, v, b_c, mask)
            da = jnp.where(mask, _dot(do, vs, NT), 0.0)
            dvs = _dot(a, do, TN)
            dqt = _dot(da, kt, NN)
            dkt = _dot(da, qt, TN)
            dq = dq + dqt * eq
            db = db + dqt * qt
            gk, gb = dkt * ek, dkt * kt
            zero8 = jnp.zeros((8, HEAD), F32)
            off_k, off_v, off_b = ([zero8] * (CHUNK // 8) for _ in range(3))
            for i in range(1, NSUB):
                lo, hi = offs[i]
                for p in range((hi - lo) // 8):
                    rows8 = slice(lo + 8 * p, lo + 8 * p + 8)
                    off_k[p] = off_k[p] + gk[rows8]
                    off_b[p] = off_b[p] - gb[rows8]
                    off_v[p] = off_v[p] + dvs[rows8]
            npv = SUB // 8
            dq_d, dk_d, dv_d, db_d = [], [], [], []
            for i in range(NSUB):
                aq = [zero8 for _ in range(npv)]
                ak = [zero8 for _ in range(npv)]
                av = [off_v[i * npv + p] for p in range(npv)]
                for j in range(SUB):
                    r = SUB * i + j
                    bj, kj, vj = b_c[r:r + 1, :], k_c[r:r + 1, :], v_c[r:r + 1, :]
                    pm_sum, ad_sum = None, None
                    for p in range(j // 8, npv):
                        lo = SUB * i + 8 * p
                        d = jnp.exp(b[lo:lo + 8] - bj)
                        if 8 * p < j:
                            d = jnp.where(_iota_rows(8) + 8 * p >= j, d, 0.0)
                        qd = q[lo:lo + 8] * d
                        dop = do[lo:lo + 8]
                        a_j = jnp.sum(qd * kj, axis=-1, keepdims=True)
                        da_j = jnp.sum(dop * vj, axis=-1, keepdims=True)
                        aq[p] = aq[p] + (da_j * d) * kj
                        pm, ad = da_j * qd, a_j * dop
                        pm_sum = pm if pm_sum is None else pm_sum + pm
                        ad_sum = ad if ad_sum is None else ad_sum + ad
                    pj = j // 8
                    here = _iota_rows(8) == (j - 8 * pj)
                    ak[pj] = ak[pj] + jnp.where(here, jnp.sum(pm_sum, axis=0, keepdims=True), 0.0)
                    av[pj] = av[pj] + jnp.where(here, jnp.sum(ad_sum, axis=0, keepdims=True), 0.0)
                for p in range(npv):
                    lo = SUB * i + 8 * p
                    db_d.append(off_b[i * npv + p] + q[lo:lo + 8] * aq[p] - k[lo:lo + 8] * ak[p])
                    dk_d.append(off_k[i * npv + p] + ak[p])
                dq_d.extend(aq)
                dv_d.extend(av)
            dq = dq + jnp.concatenate(dq_d, axis=0)
            dk = dk + jnp.concatenate(dk_d, axis=0)
            dv = dv + jnp.concatenate(dv_d, axis=0)
            db = db + jnp.concatenate(db_d, axis=0)
            db = db + jnp.where(_iota_rows(CHUNK) == CHUNK - 1, db_last, 0.0)
            dgl = _cumsum_bwd(db)
            dfv = dgl / f - dk
            dlb_s[...] += jnp.sum(dfv * (1.0 - sg), axis=0, keepdims=True)
            df_ref[rows, :] = (dfv * (1.0 - lb) * (sg * (1.0 - sg))).astype(BF16)
            dq_ref[rows, :] = (dq * _dsilu(qr, sq)).astype(BF16)
            di_ref[rows, :] = dv.astype(BF16)
            return carry

        def finish():
            @pl.when(bi == nb - 1)
            def _():
                dgam = dlb_s[...] * lb * (1.0 - lb)
                small_ref[...] = jnp.zeros_like(small_ref)
                small_ref[0:1, :] = dgam
                small_ref[1:2, :] = -dgam
                small_ref[2:3, :] = dwn_s[...]

        return chunk, finish

    def body(*refs):
        mask = _offdiag_mask()
        heads = [head(mask, *_group_views(refs, g, "lllllllgl" + "lllll" + "ggggll")) for g in range(ng)]

        def step(it, carry):
            for chunk, _ in heads:
                chunk(it, carry)
            return carry

        lax.fori_loop(0, nc, step, 0)
        for _, finish in heads:
            finish()

    gw = ng * HEAD

    def col(off):
        return pl.BlockSpec((seq, gw), lambda h, b: (b, off * (nh // ng) + h))

    vec = lambda r: pl.BlockSpec((r, gw), lambda h, b: (0, h))
    blk = pl.BlockSpec((seq, gw), lambda h, b: (b, h))
    dshape = jax.ShapeDtypeStruct((t, w), BF16)
    return _pcall(
        body, n_in=9, args=[proj, proj, proj, proj, lbg, nw, opre, states, dmix], deps=deps,
        name=name, grid=(nh // ng, nb),
        in_specs=[col(0), col(1), col(2), col(3), vec(2), vec(1), blk,
                  pl.BlockSpec((None, ng, nc, HEAD, HEAD), lambda h, b: (b, h, 0, 0, 0)), blk],
        out_specs=[blk, blk, blk, blk, vec(8)],
        out_shape=[dshape, dshape, dshape, dshape, jax.ShapeDtypeStruct((8, w), F32)],
        scratch_shapes=[pltpu.VMEM((ng, CHUNK, HEAD), F32)] * 3 + [pltpu.VMEM((ng, HEAD, HEAD), F32)]
        + [pltpu.VMEM((1, gw), F32)] * 2,
        compiler_params=_cp(dimension_semantics=("parallel", "arbitrary")))


def _expm1(x):
    poly = x * (1.0 + x * (0.5 + x * (1.0 / 6 + x * (1.0 / 24 + x * (1.0 / 120 + x * (1.0 / 720))))))
    return jnp.where(jnp.abs(x) < 0.25, poly, jnp.exp(x) - 1.0)


def _softplus_neg(lam):
    x = -lam
    e = jnp.exp(-jnp.abs(x))
    u = 1.0 + e
    l1p = jnp.where(u == 1.0, e, jnp.log(u) * (e / jnp.where(u == 1.0, 1.0, u - 1.0)))
    return jnp.maximum(x, 0.0) + l1p


_GELU_C = math.sqrt(2.0 / math.pi)


def _gelu(y):
    return 0.5 * y * (1.0 + jnp.tanh(_GELU_C * (y + 0.044715 * (y * y * y))))


def _dgelu(y):
    th = jnp.tanh(_GELU_C * (y + 0.044715 * (y * y * y)))
    return 0.5 * (1.0 + th) + 0.5 * y * (1.0 - th * th) * (_GELU_C * (1.0 + 3 * 0.044715 * (y * y)))


def _lru_gates(xr, prev8, cw_ref, cb, wa_ref, ba, wx_ref, bx, sp, first):
    x3, x2, x1 = _shift_down(prev8, xr, 3), _shift_down(prev8, xr, 2), _shift_down(prev8, xr, 1)
    xb = cb + x3 * cw_ref[0:1, :]
    xb = xb + x2 * cw_ref[1:2, :]
    xb = xb + x1 * cw_ref[2:3, :]
    xb = xb + xr * cw_ref[3:4, :]
    r = _sigmoid(_dot(xb, wa_ref[...], NN) + ba)
    ig = _sigmoid(_dot(xb, wx_ref[...], NN) + bx)
    la = (-LRU_C * r) * sp
    a = jnp.exp(la)
    start = jnp.logical_and(first, _iota_rows(xr.shape[0]) == 0)
    mult = jnp.where(start, 1.0, jnp.sqrt(-_expm1(2.0 * la)))
    return (x3, x2, x1), xb, r, ig, a, mult, start


def _lru_fwd(proj, cw, cb, wa, ba, wx, bx, lam, *, nb, seq, name, deps=()):
    t = proj.shape[0]
    w = cb.shape[1]
    nblk = w // HEAD
    nc = seq // CHUNK
    ng = _lane_groups(nblk, 4)

    def block(x_ref, y_ref, cw_ref, cb_ref, wa_ref, ba_ref, wx_ref, bx_ref, lam_ref, p_ref, h_ref):
        sp = _softplus_neg(lam_ref[...])
        cb_v, ba_v, bx_v = cb_ref[...], ba_ref[...], bx_ref[...]

        def tile(c, carry):
            hc, prev8 = carry
            base = pl.multiple_of(c * CHUNK, CHUNK)
            rows = pl.ds(base, CHUNK)
            xr = x_ref[rows, :]
            _, xb, _, ig, a, mult, _ = _lru_gates(xr, prev8, cw_ref, cb_v, wa_ref, ba_v, wx_ref, bx_v, sp, c == 0)
            ap, up = _scan_fwd(a, xb * ig * mult)
            h = up + ap * hc
            h_ref[rows, :] = h
            p_ref[rows, :] = h * _gelu(y_ref[rows, :])
            h_last = jnp.sum(jnp.where(_iota_rows(8) == 7, h[CHUNK - 8:], 0.0), axis=0, keepdims=True)
            return h_last, xr[CHUNK - 8:]

        return tile

    def body(*refs):
        tiles = [block(*_group_views(refs, g, "llllglgll" + "ll")) for g in range(ng)]

        def step(c, carries):
            return tuple(tile(c, carry) for tile, carry in zip(tiles, carries))

        lax.fori_loop(0, nc, step, ((jnp.zeros((1, HEAD), F32), jnp.zeros((8, HEAD), F32)),) * ng)

    gw = ng * HEAD

    def col(off):
        return pl.BlockSpec((seq, gw), lambda n, b: (b, off * (nblk // ng) + n))

    vec = lambda r: pl.BlockSpec((r, gw), lambda n, b: (0, n))
    mat = pl.BlockSpec((ng, HEAD, HEAD), lambda n, b: (n, 0, 0))
    blk = pl.BlockSpec((seq, gw), lambda n, b: (b, n))
    return _pcall(
        body, n_in=9, args=[proj, proj, cw, cb, wa, ba, wx, bx, lam], deps=deps, name=name, grid=(nblk // ng, nb),
        in_specs=[col(4), col(5), vec(4), vec(1), mat, vec(1), mat, vec(1), vec(1)],
        out_specs=[blk, blk],
        out_shape=[jax.ShapeDtypeStruct((t, w), F32), jax.ShapeDtypeStruct((t, w), F32)],
        compiler_params=_cp(dimension_semantics=("parallel", "parallel")))


def _lru_bwd(proj, cw, cb, wa, ba, wx, bx, lam, hsv, dp, *, nb, seq, name):
    t = proj.shape[0]
    w = cb.shape[1]
    nblk = w // HEAD
    nc = seq // CHUNK
    ng = _lane_groups(nblk, 4)

    def block(x_ref, y_ref, cw_ref, cb_ref, wa_ref, ba_ref, wx_ref, bx_ref, lam_ref, h_ref, dp_ref,
              dx_ref, dy_ref, small_ref, dwa_ref, dwx_ref, a_c, g_c, acc, dwa_s, dwx_s):
        bi = pl.program_id(1)
        lamv = lam_ref[...]
        sp = _softplus_neg(lamv)
        cb_v, ba_v, bx_v = cb_ref[...], ba_ref[...], bx_ref[...]

        @pl.when(bi == 0)
        def _():
            acc[...] = jnp.zeros_like(acc)
            dwa_s[...] = jnp.zeros_like(dwa_s)
            dwx_s[...] = jnp.zeros_like(dwx_s)

        def tile(it, carry):
            g_next, a_next, dxb_next8 = carry
            c = nc - 1 - it
            base = pl.multiple_of(c * CHUNK, CHUNK)
            rows = pl.ds(base, CHUNK)
            before = pl.ds(pl.multiple_of(jnp.maximum(base - 8, 0), 8), 8)
            inner = c > 0
            xr = x_ref[rows, :]
            prev8 = jnp.where(inner, x_ref[before, :], 0.0)
            (x3, x2, x1), xb, r, ig, a, mult, start = _lru_gates(
                xr, prev8, cw_ref, cb_v, wa_ref, ba_v, wx_ref, bx_v, sp, c == 0)
            h = h_ref[rows, :]
            h_m1 = _shift_down(jnp.where(inner, h_ref[before, :], 0.0), h, 1)
            yv = y_ref[rows, :]
            dpv = dp_ref[rows, :]
            dy_ref[rows, :] = (dpv * h * _dgelu(yv)).astype(BF16)
            dh = dpv * _gelu(yv)
            a_up = _shift_up(a, jnp.broadcast_to(a_next, (8, HEAD)), 1)
            ap, gp = _scan_bwd(a_up, dh)
            g = gp + ap * g_next
            a_c[...] = a
            g_c[...] = g
            da = g * h_m1
            gx = g * xb
            dxb = g * ig * mult
            dig = gx * mult
            dmult = jnp.where(start, 0.0, gx * ig)
            dla = da * a - dmult * (a * a) / mult
            dzr = (dla * (-LRU_C * sp)) * (r * (1.0 - r))
            dzi = dig * (ig * (1.0 - ig))
            dxb = dxb + _dot(dzr, wa_ref[...], NT) + _dot(dzi, wx_ref[...], NT)
            dwa_s[...] += _dot(xb, dzr, TN)
            dwx_s[...] += _dot(xb, dzi, TN)
            d1, d2, d3 = (_shift_up(dxb, dxb_next8, s) for s in (1, 2, 3))
            dx = dxb * cw_ref[3:4, :] + d1 * cw_ref[2:3, :] + d2 * cw_ref[1:2, :] + d3 * cw_ref[0:1, :]
            dx_ref[rows, :] = dx.astype(BF16)
            colsum = lambda z: jnp.sum(z, axis=0, keepdims=True)
            acc[0:1, :] += colsum(x3 * dxb)
            acc[1:2, :] += colsum(x2 * dxb)
            acc[2:3, :] += colsum(x1 * dxb)
            acc[3:4, :] += colsum(xr * dxb)
            acc[4:5, :] += colsum(dxb)
            acc[5:6, :] += colsum(dzr)
            acc[6:7, :] += colsum(dzi)
            acc[7:8, :] += colsum(dla * (-LRU_C * r))
            return g_c[0:1, :], a_c[0:1, :], dxb[0:8]

        def finish():
            @pl.when(bi == nb - 1)
            def _():
                small_ref[...] = acc[...]
                small_ref[7:8, :] = acc[7:8, :] * (-_sigmoid(-lamv))
                dwa_ref[...] = dwa_s[...]
                dwx_ref[...] = dwx_s[...]

        return tile, finish

    def body(*refs):
        blocks = [block(*_group_views(refs, g, "llllglgllll" + "lllgg" + "gglgg")) for g in range(ng)]

        def step(it, carries):
            return tuple(tile(it, carry) for (tile, _), carry in zip(blocks, carries))

        zero = jnp.zeros((1, HEAD), F32)
        lax.fori_loop(0, nc, step, ((zero, zero, jnp.zeros((8, HEAD), F32)),) * ng)
        for _, finish in blocks:
            finish()

    gw = ng * HEAD

    def col(off):
        return pl.BlockSpec((seq, gw), lambda n, b: (b, off * (nblk // ng) + n))

    vec = lambda r: pl.BlockSpec((r, gw), lambda n, b: (0, n))
    mat = pl.BlockSpec((ng, HEAD, HEAD), lambda n, b: (n, 0, 0))
    blk = pl.BlockSpec((seq, gw), lambda n, b: (b, n))
    dshape = jax.ShapeDtypeStruct((t, w), BF16)
    return pl.pallas_call(
        body, name=name, grid=(nblk // ng, nb),
        in_specs=[col(4), col(5), vec(4), vec(1), mat, vec(1), mat, vec(1), vec(1), blk, blk],
        out_specs=[blk, blk, vec(8), mat, mat],
        out_shape=[dshape, dshape, jax.ShapeDtypeStruct((8, w), F32),
                   jax.ShapeDtypeStruct((nblk, HEAD, HEAD), F32), jax.ShapeDtypeStruct((nblk, HEAD, HEAD), F32)],
        scratch_shapes=[pltpu.VMEM((ng, CHUNK, HEAD), F32)] * 2 + [pltpu.VMEM((8, gw), F32)]
        + [pltpu.VMEM((ng, HEAD, HEAD), F32)] * 2,
        compiler_params=_cp(dimension_semantics=("parallel", "arbitrary")),
    )(proj, proj, cw, cb, wa, ba, wx, bx, lam, hsv, dp)


def _ffn_conv(x, prev8, cw_ref, cbv):
    x2, x1 = _shift_down(prev8, x, 2), _shift_down(prev8, x, 1)
    y = cbv + x2 * cw_ref[0:1, :]
    y = y + x1 * cw_ref[1:2, :]
    y = y + x * cw_ref[2:3, :]
    return (x2, x1), y


def _ffn_conv_at(x_ref, base, rt, zero8, cw_ref, cbv):
    x = x_ref[base:base + rt, :]
    if base == 0:
        return _ffn_conv(x, zero8, cw_ref, cbv)[1]
    y = cbv + x_ref[base - 2:base - 2 + rt, :] * cw_ref[0:1, :]
    y = y + x_ref[base - 1:base - 1 + rt, :] * cw_ref[1:2, :]
    return y + x * cw_ref[2:3, :]


def _ffn_act_fwd(up, cw, cb, *, nb, seq, name, deps=()):
    t, f2 = up.shape
    f = f2 // 2
    tc = _tile(f, 512)
    nj = f // tc
    rt = _tile(seq, 65536 // tc, 16)
    nr = seq // rt

    def body(g_ref, v_ref, cwg_ref, cwv_ref, cbg_ref, cbv_ref, o_ref):
        cbg, cbv = cbg_ref[...], cbv_ref[...]
        z = jnp.zeros((8, tc), F32)
        for c in range(nr):
            gate = _ffn_conv_at(g_ref, c * rt, rt, z, cwg_ref, cbg)
            val = _ffn_conv_at(v_ref, c * rt, rt, z, cwv_ref, cbv)
            o_ref[c * rt:(c + 1) * rt, :] = ((gate * _sigmoid(gate)) * val).astype(BF16)

    gcol = pl.BlockSpec((seq, tc), lambda j, b: (b, j))
    vcol = pl.BlockSpec((seq, tc), lambda j, b: (b, nj + j))
    gv = lambda r: pl.BlockSpec((r, tc), lambda j, b: (0, j))
    vv = lambda r: pl.BlockSpec((r, tc), lambda j, b: (0, nj + j))
    return _pcall(
        body, n_in=6, args=[up, up, cw, cw, cb, cb], deps=deps, name=name, grid=(nj, nb),
        in_specs=[gcol, vcol, gv(3), vv(3), gv(1), vv(1)], out_specs=gcol,
        out_shape=jax.ShapeDtypeStruct((t, f), BF16),
        compiler_params=_cp(dimension_semantics=("parallel", "parallel")))


def _ffn_act_bwd(dact, up, cw, cb, *, nb, seq, name, deps=()):
    t, f2 = up.shape
    f = f2 // 2
    tc = _tile(f, 512)
    nj = f // tc
    rt = _tile(seq, 65536 // tc, 16)
    nr = seq // rt

    def body(da_ref, g_ref, v_ref, cwg_ref, cwv_ref, cbg_ref, cbv_ref,
             d_ref, sg_ref, sv_ref, eg, ev, accg, accv):
        bi = pl.program_id(1)
        cbg, cbv = cbg_ref[...], cbv_ref[...]

        @pl.when(bi == 0)
        def _():
            accg[...] = jnp.zeros_like(accg)
            accv[...] = jnp.zeros_like(accv)

        colsum = lambda z: jnp.sum(z, axis=0, keepdims=True)

        z = jnp.zeros((8, tc), F32)
        eg[seq:seq + 8, :] = z
        ev[seq:seq + 8, :] = z

        def first(c, carry):
            pg, pv = carry
            rows = pl.ds(pl.multiple_of(c * rt, rt), rt)
            xg, xv = g_ref[rows, :], v_ref[rows, :]
            (g2, g1), gate = _ffn_conv(xg, pg, cwg_ref, cbg)
            (v2, v1), val = _ffn_conv(xv, pv, cwv_ref, cbv)
            s = _sigmoid(gate)
            da = da_ref[rows, :]
            dgate = da * val * _dsilu(gate, s)
            dval = da * (gate * s)
            eg[rows, :] = dgate
            ev[rows, :] = dval
            for acc, (s2, s1, s0), d in ((accg, (g2, g1, xg), dgate), (accv, (v2, v1, xv), dval)):
                acc[0:1, :] += colsum(s2 * d)
                acc[1:2, :] += colsum(s1 * d)
                acc[2:3, :] += colsum(s0 * d)
                acc[3:4, :] += colsum(d)
            return xg[rt - 8:], xv[rt - 8:]

        lax.fori_loop(0, nr, first, (z, z))

        for c in range(nr):
            base = c * rt
            for half, (e, cw_ref) in enumerate(((eg, cwg_ref), (ev, cwv_ref))):
                d, d1, d2 = e[base:base + rt, :], e[base + 1:base + 1 + rt, :], e[base + 2:base + 2 + rt, :]
                d_ref[half, base:base + rt, :] = (
                    d * cw_ref[2:3, :] + d1 * cw_ref[1:2, :] + d2 * cw_ref[0:1, :]).astype(BF16)

        @pl.when(bi == nb - 1)
        def _():
            sg_ref[...] = accg[...]
            sv_ref[...] = accv[...]

    gcol = pl.BlockSpec((seq, tc), lambda j, b: (b, j))
    vcol = pl.BlockSpec((seq, tc), lambda j, b: (b, nj + j))
    gv = lambda r: pl.BlockSpec((r, tc), lambda j, b: (0, j))
    vv = lambda r: pl.BlockSpec((r, tc), lambda j, b: (0, nj + j))
    dshape = jax.ShapeDtypeStruct((2, t, f), BF16)
    sshape = jax.ShapeDtypeStruct((8, f), F32)
    return _pcall(
        body, n_in=7, args=[dact, up, up, cw, cw, cb, cb], deps=deps, name=name, grid=(nj, nb),
        in_specs=[gcol, gcol, vcol, gv(3), vv(3), gv(1), vv(1)],
        out_specs=[pl.BlockSpec((2, seq, tc), lambda j, b: (0, b, j)), gv(8), gv(8)],
        out_shape=[dshape, sshape, sshape],
        scratch_shapes=[pltpu.VMEM((seq + 8, tc), F32)] * 2 + [pltpu.VMEM((8, tc), F32)] * 2,
        compiler_params=_cp(dimension_semantics=("parallel", "arbitrary")))


def _adamw_math(wv, g, mv, vv):
    m = ADAM_B1 * mv + (1.0 - ADAM_B1) * g
    v = ADAM_B2 * vv + (1.0 - ADAM_B2) * (g * g)
    m_hat = m / (1.0 - ADAM_B1 ** ADAM_STEP)
    v_hat = v / (1.0 - ADAM_B2 ** ADAM_STEP)
    delta = -ADAM_LR * (m_hat / (jnp.sqrt(v_hat) + ADAM_EPS) + ADAM_WD * wv)
    return delta, m, v


def _adamw_shard(wv, mv, vv, p_own, rb, *, name):
    _, r, cdim = wv.shape
    tr = _row_tile(r, cdim)

    def body(w_ref, m_ref, v_ref, p_ref, rb_ref, g_ref, d_ref, mo_ref, vo_ref):
        g = p_ref[...].astype(F32)
        for k in range(3):
            g = g + rb_ref[k].astype(F32)
        g_ref[...] = g
        d_ref[...], mo_ref[...], vo_ref[...] = _adamw_math(w_ref[...], g, m_ref[...], v_ref[...])

    row = pl.BlockSpec((tr, cdim), lambda i: (i, 0))
    row3 = pl.BlockSpec((None, tr, cdim), lambda i: (0, i, 0))
    shp = jax.ShapeDtypeStruct((1, r, cdim), F32)
    return pl.pallas_call(
        body, name=name, grid=(r // tr,),
        in_specs=[row3, row3, row3, row, pl.BlockSpec((3, tr, cdim), lambda i: (0, i, 0))],
        out_specs=[row3] * 4, out_shape=[shp] * 4, compiler_params=_cp(dimension_semantics=("parallel",)),
    )(wv, mv, vv, p_own, rb)


def _adamw_packed(wv, g, mv, vv, *, name):
    r = wv.shape[0]
    tr = _tile(r, 256, 8)

    def body(w_ref, g_ref, m_ref, v_ref, d_ref, mo_ref, vo_ref):
        d_ref[...], mo_ref[...], vo_ref[...] = _adamw_math(w_ref[...], g_ref[...], m_ref[...], v_ref[...])

    row = pl.BlockSpec((tr, LANE), lambda i: (i, 0))
    shp = jax.ShapeDtypeStruct((r, LANE), F32)
    return pl.pallas_call(
        body, name=name, grid=(r // tr,), in_specs=[row] * 4, out_specs=[row] * 3, out_shape=[shp] * 3,
        compiler_params=_cp(dimension_semantics=("parallel",)),
    )(wv, g, mv, vv)


def _place():
    return lax.axis_index("x"), lax.axis_index("y"), lax.axis_index("c")


def _all_reduce_packed(p, *, name):
    r = p.shape[0]
    tr = _tile(r, 256, 8)

    def body(p_ref, o_ref, buf, send, recv, lsem):
        x, y, c = _place()
        me, sibling = (x, y, c), (x, y, 1 - c)
        chips = [(1 - x, y), (x, 1 - y), (1 - x, 1 - y)]

        def slot(px, py, pc):
            return 4 * px + 2 * py + pc

        def copy(k, block, to, src=None):
            dst = buf.at[slot(*block)]
            return pltpu.make_async_remote_copy(
                src_ref=dst if src is None else src, dst_ref=dst, send_sem=send.at[k], recv_sem=recv.at[k],
                device_id=to, device_id_type=MESH)

        mine = pltpu.make_async_copy(p_ref, buf.at[slot(*me)], lsem)
        mine.start()
        first = [copy(0, me, sibling, src=p_ref)]
        first += [copy(1 + j, me, (*chip, c), src=p_ref) for j, chip in enumerate(chips)]
        for cp in first:
            cp.start()
        passed = [copy(4 + j, (*chip, c), sibling) for j, chip in enumerate(chips)]
        for j, chip in enumerate(chips):
            copy(1 + j, (*chip, c), me).wait_recv()
            passed[j].start()
        copy(0, sibling, me).wait_recv()
        for j, chip in enumerate(chips):
            copy(4 + j, (*chip, 1 - c), me).wait_recv()
        for cp in first + passed:
            cp.wait_send()
        mine.wait()

        def add(i, carry):
            rows = pl.ds(pl.multiple_of(i * tr, tr), tr)
            s = buf[0, rows, :]
            for d in range(1, NDEV):
                s = s + buf[d, rows, :]
            o_ref[rows, :] = s
            return carry

        lax.fori_loop(0, r // tr, add, 0)

    vm = pl.BlockSpec(memory_space=pltpu.VMEM)
    return pl.pallas_call(
        body, name=name, in_specs=[vm], out_specs=vm, out_shape=jax.ShapeDtypeStruct(p.shape, p.dtype),
        scratch_shapes=[pltpu.VMEM((NDEV,) + p.shape, p.dtype), pltpu.SemaphoreType.DMA((7,)),
                        pltpu.SemaphoreType.DMA((7,)), pltpu.SemaphoreType.DMA],
        compiler_params=_cp(),
    )(p)


HBM = pl.BlockSpec(memory_space=pltpu.HBM)
SEM = pl.BlockSpec(memory_space=pltpu.SEMAPHORE)
EFFECT = pltpu.SideEffectType.DATAFLOW_SIDE_EFFECTING


def _plan_copies(plan, s_refs, l_refs, send, recv):
    def pick(kind, a, idx):
        ref = (s_refs if kind == "s" else l_refs)[a]
        return ref if idx is None else ref.at[idx]

    return [pltpu.make_async_remote_copy(
        src_ref=pick(*src), dst_ref=pick(*dst), send_sem=send.at[i], recv_sem=recv.at[i],
        device_id=to, device_id_type=MESH) for i, (src, dst, to) in enumerate(plan(*_place()))]


def _xfer_start(srcs, lands, plan, *, name, deps=()):
    ns, nl = len(srcs), len(lands)
    nd = len(deps)
    ncopy = len(plan(0, 0, 0))

    def body(*refs):
        s_refs, l_refs = refs[:ns], refs[ns:ns + nl]
        send, recv = refs[ns + nl + nd], refs[ns + nl + nd + 1]
        token = refs[-1]
        for cp in _plan_copies(plan, s_refs, l_refs, send, recv):
            cp.start()
        token[...] = jnp.zeros_like(token)

    bufs = list(srcs) + list(lands)
    outs = pl.pallas_call(
        body, name=name,
        out_shape=(pltpu.SemaphoreType.DMA((ncopy,)), pltpu.SemaphoreType.DMA((ncopy,)),
                   *[pltpu.HBM(b.shape, b.dtype) for b in bufs], jax.ShapeDtypeStruct((8, LANE), F32)),
        in_specs=[HBM] * (ns + nl) + [ANY] * nd,
        out_specs=(SEM, SEM, *[HBM] * (ns + nl), pl.BlockSpec(memory_space=pltpu.VMEM)),
        input_output_aliases={i: 2 + i for i in range(ns + nl)},
        compiler_params=pltpu.CompilerParams(has_side_effects=EFFECT),
    )(*[pltpu.with_memory_space_constraint(b, pltpu.HBM) for b in bufs], *deps)
    return outs[0], outs[1], list(outs[2:2 + ns]), list(outs[2 + ns:2 + ns + nl]), outs[-1]


def _xfer_wait(send, recv, srcs, lands, plan, after, *, name):
    ns, nl = len(srcs), len(lands)

    def body(*refs):
        s_refs, l_refs = refs[:ns], refs[ns:ns + nl]
        send_ref, recv_ref = refs[ns + nl], refs[ns + nl + 1]
        for cp in _plan_copies(plan, s_refs, l_refs, send_ref, recv_ref):
            cp.wait_send()
            cp.wait_recv()

    bufs = list(srcs) + list(lands)
    outs = pl.pallas_call(
        body, name=name, out_shape=tuple(pltpu.HBM(b.shape, b.dtype) for b in bufs),
        in_specs=[HBM] * (ns + nl) + [SEM, SEM, ANY], out_specs=tuple([HBM] * (ns + nl)),
        input_output_aliases={i: i for i in range(ns + nl)},
        compiler_params=pltpu.CompilerParams(has_side_effects=EFFECT),
    )(*bufs, send, recv, after)
    return list(outs[:ns]), list(outs[ns:])


def _slot8(px, py, pc):
    return 4 * px + 2 * py + pc


def _slot4(r):
    return lambda px, py, pc: (2 * px + py, pl.ds(pc * r, r))


def _plan_gather_first(slots, windows=None):
    windows = windows or [None] * len(slots)

    def plan(x, y, c):
        peers = [(x, y, c), (x, y, 1 - c), (1 - x, y, c), (x, 1 - y, c), (1 - x, 1 - y, c)]
        return [(("s", a, windows[a]), ("l", a, slot(x, y, c)), to) for a, slot in enumerate(slots) for to in peers]
    return plan


def _plan_gather_pass(slots):
    def plan(x, y, c):
        chips = [(1 - x, y), (x, 1 - y), (1 - x, 1 - y)]
        return [(("l", a, slot(px, py, c)), ("l", a, slot(px, py, c)), (x, y, 1 - c))
                for a, slot in enumerate(slots) for px, py in chips]
    return plan


def _plan_rs_sibling(r):
    def plan(x, y, c):
        return [(("s", 0, (j, pl.ds((1 - c) * r, r))), ("l", 0, j), (x, y, 1 - c)) for j in range(4)]
    return plan


def _plan_rs_plane():
    def plan(x, y, c):
        chips = [(1 - x, y), (x, 1 - y), (1 - x, 1 - y)]
        return [(("s", 0, 2 * px + py), ("l", 0, k), (px, py, c)) for k, (px, py) in enumerate(chips)]
    return plan


def _add_pairs(g, rcv, cidx, *, name):
    _, r, cdim = rcv.shape
    tr = _row_tile(r, cdim, budget=4194304)
    per = r // tr

    def body(c_ref, g_ref, r_ref, o_ref):
        o_ref[...] = (g_ref[...].astype(F32) + r_ref[...].astype(F32)).astype(o_ref.dtype)

    grid_spec = pltpu.PrefetchScalarGridSpec(
        num_scalar_prefetch=1, grid=(4, per),
        in_specs=[pl.BlockSpec((None, tr, cdim), lambda j, i, c_ref: (j, c_ref[0] * per + i, 0)),
                  pl.BlockSpec((None, tr, cdim), lambda j, i, c_ref: (j, i, 0))],
        out_specs=pl.BlockSpec((None, tr, cdim), lambda j, i, c_ref: (j, i, 0)))
    return pl.pallas_call(
        body, name=name, grid_spec=grid_spec, out_shape=jax.ShapeDtypeStruct((4, r, cdim), g.dtype),
        compiler_params=_cp(dimension_semantics=("parallel", "parallel")),
    )(cidx, g, rcv)


def _pack(arrs):
    flat = jnp.concatenate([a.reshape(-1).astype(F32) for a in arrs])
    n = flat.shape[0]
    rows = -(-n // LANE)
    rows = -(-rows // 256) * 256
    return jnp.pad(flat, (0, rows * LANE - n)).reshape(rows, LANE)


def _unpack(packed, shapes):
    flat = packed.reshape(-1)
    out, o = [], 0
    for s in shapes:
        n = math.prod(s)
        out.append(flat[o:o + n].reshape(s))
        o += n
    return out


def _pad_blocks(a, w, wp):
    lead = a.shape[:-1]
    k = a.shape[-1] // w
    pads = [(0, 0)] * (len(lead) + 1) + [(0, wp - w)]
    return jnp.pad(a.reshape(*lead, k, w), pads).reshape(*lead, k * wp)


def kernel(x, ln1_w, w_in, lb_gamma, hg_norm_w, lru_conv_w, lru_conv_b, lru_wa, lru_ba, lru_wx, lru_bx, lru_lambda, lru_norm_w, w_out, ln2_w, ffn_w_up, ffn_conv_w, ffn_conv_b, ffn_w_down, final_norm_w, loss_target, m_ln1_w, m_w_in, m_lb_gamma, m_hg_norm_w, m_lru_conv_w, m_lru_conv_b, m_lru_wa, m_lru_ba, m_lru_wx, m_lru_bx, m_lru_lambda, m_lru_norm_w, m_w_out, m_ln2_w, m_ffn_w_up, m_ffn_conv_w, m_ffn_conv_b, m_ffn_w_down, m_final_norm_w, v_ln1_w, v_w_in, v_lb_gamma, v_hg_norm_w, v_lru_conv_w, v_lru_conv_b, v_lru_wa, v_lru_ba, v_lru_wx, v_lru_bx, v_lru_lambda, v_lru_norm_w, v_w_out, v_ln2_w, v_ffn_w_up, v_ffn_conv_w, v_ffn_conv_b, v_ffn_w_down, v_final_norm_w):
    nb, seq, d = x.shape
    t = nb * seq
    wmix = d // 2
    in_sh = w_in.shape[2]
    up_sh = ffn_w_up.shape[2]
    up_pad = -(-up_sh // LANE) * LANE
    hs = ffn_w_down.shape[1]
    fpad = 4 * up_pad
    cx, cy, cc = _place()
    me = 4 * cx + 2 * cy + cc
    plane = 2 * cx + cy

    def gather(shards, lands, slots, tag, deps=(), also=lambda x, y, c: []):
        first = _plan_gather_first(slots)
        plan = lambda x, y, c: first(x, y, c) + also(x, y, c)
        st = _xfer_start(shards, lands, plan, name=f"gather_{tag}_start", deps=deps)
        return (st, plan, slots, tag), st[4]

    def pass_on(g, after):
        st, plan, slots, tag = g
        _, lands = _xfer_wait(st[0], st[1], st[2], st[3], plan, after, name=f"gather_{tag}_wait")
        st2 = _xfer_start([], lands, _plan_gather_pass(slots), name=f"gather_{tag}_pass")
        return (st2, slots, tag), st2[4]

    def gathered(g, after):
        st2, slots, tag = g
        return _xfer_wait(st2[0], st2[1], [], st2[3], _plan_gather_pass(slots), after, name=f"gather_{tag}_done")[1]

    land8 = lambda s: lax.empty((NDEV,) + s.shape, s.dtype)

    win_s = w_in[0].astype(BF16)
    wout_s = w_out[0].astype(BF16)
    wup_s = jnp.pad(ffn_w_up[0], ((0, 0), (0, up_pad - up_sh))).astype(BF16)
    wdn_s = ffn_w_down[0].astype(BF16)
    fcw_s = jnp.pad(ffn_conv_w[0], ((0, 0), (0, up_pad - up_sh)))
    half = in_sh // 2
    all_rows = pl.ds(0, d)
    win_slot = lambda h: lambda px, py, pc: (_slot8(px, py, pc), all_rows, pl.ds(h * half, half))
    first_a = _plan_gather_first([win_slot(0), _slot8, _slot8], [(all_rows, pl.ds(0, half)), None, None])
    first_b = _plan_gather_first([win_slot(1)], [(all_rows, pl.ds(half, half))])
    pass_a, pass_b = _plan_gather_pass([win_slot(0), _slot8, _slot8]), _plan_gather_pass([win_slot(1)])
    sa = _xfer_start([win_s, lru_conv_w[0], fcw_s], [land8(win_s), land8(lru_conv_w[0]), land8(fcw_s)], first_a,
                     name="gather_a_start")
    sb = _xfer_start([sa[2][0]], [sa[3][0]], first_b, name="gather_b_start", deps=(sa[4],))
    go, tok = gather([wout_s], [land8(wout_s)], [_slot8], "o", deps=(sb[4],))
    uhalf = up_pad // 2
    up_slot = lambda h: lambda px, py, pc: (_slot8(px, py, pc), all_rows, pl.ds(h * uhalf, uhalf))
    first_u = [_plan_gather_first([up_slot(h)], [(all_rows, pl.ds(h * uhalf, uhalf))]) for h in range(2)]
    pass_u = [_plan_gather_pass([up_slot(h)]) for h in range(2)]
    su = _xfer_start([wup_s], [land8(wup_s)], first_u[0], name="gather_u_start", deps=(tok,))
    sv = _xfer_start([su[2][0]], [su[3][0]], first_u[1], name="gather_v_start", deps=(su[4],))
    tok = sv[4]
    npad = up_pad - 2 * hs
    zero_pad = lambda x, y, c: [(("s", 1, None), ("l", 0, (j, pl.ds(2 * hs, npad))), (x, y, c)) for j in range(4)]
    gd, tok = gather([wdn_s] + ([jnp.zeros((npad, d), BF16)] if npad else []), [lax.empty((4, up_pad, d), BF16)],
                     [_slot4(hs)], "d", deps=(tok,), also=zero_pad if npad else lambda x, y, c: [])
    fcb = _pad_blocks(ffn_conv_b, up_sh, up_pad)
    wa_b, wx_b = lru_wa[0].astype(BF16), lru_wx[0].astype(BF16)

    xf = x.reshape(t, d)
    hn = _rms_fwd(xf, ln1_w, name="ln1_fwd", deps=(tok,))
    srcs_a, lands_a = _xfer_wait(sa[0], sa[1], [sb[2][0], sa[2][1], sa[2][2]], [sb[3][0], sa[3][1], sa[3][2]],
                                 first_a, hn, name="gather_a_wait")
    pa = _xfer_start([], lands_a, pass_a, name="gather_a_pass")
    _, (win_l, lcw_g, fcw_g) = _xfer_wait(pa[0], pa[1], [], pa[3], pass_a, pa[4], name="gather_a_done")
    lcw = lcw_g.transpose(1, 0, 2).reshape(lru_conv_w.shape[1], wmix)
    fcw = fcw_g.transpose(1, 0, 2).reshape(ffn_conv_w.shape[1], 2 * fpad)
    proj = _mm(hn, win_l, kind="nn", out_dtype=F32, name="in_proj_a", tm_cap=512, tk_cap=d, col_part=(0, 2))
    _, (win_l,) = _xfer_wait(sb[0], sb[1], [srcs_a[0]], [win_l], first_b, proj, name="gather_b_wait")
    pb = _xfer_start([], [win_l], pass_b, name="gather_b_pass")
    _, (win_g,) = _xfer_wait(pb[0], pb[1], [], pb[3], pass_b, pb[4], name="gather_b_done")
    proj = _mm(hn, win_g, kind="nn", out_dtype=F32, name="in_proj_b", tm_cap=512, tk_cap=d, col_part=(1, 2),
               into=proj)
    go, tok = pass_on(go, proj)
    o_hg, o_pre, states = _hgrn_fwd(proj, lb_gamma, hg_norm_w, nb=nb, seq=seq, name="hgrn_fwd")
    (wout_g,) = gathered(go, o_pre)
    wout_f = wout_g.reshape(1, d, d)
    p_lru, h_lru = _lru_fwd(proj, lcw, lru_conv_b, wa_b, lru_ba, wx_b, lru_bx, lru_lambda,
                            nb=nb, seq=seq, name="lru_fwd", deps=(tok,))
    o_lru = _rms_fwd(p_lru, lru_norm_w, name="lru_norm_fwd")
    srcs_u, lands_u = _xfer_wait(su[0], su[1], [sv[2][0]], [sv[3][0]], first_u[0], o_lru, name="gather_u_wait")
    pu = _xfer_start([], lands_u, pass_u[0], name="gather_u_pass")
    mix = jnp.concatenate([o_hg, o_lru], axis=1)
    h1 = _mm(mix, wout_f, kind="nn", out_dtype=F32, name="out_proj", res=xf, tm_cap=512, tk_cap=d, deps=(pu[4],))
    hn2 = _rms_fwd(h1, ln2_w, name="ln2_fwd")
    _, (wup_l,) = _xfer_wait(pu[0], pu[1], [], pu[3], pass_u[0], hn2, name="gather_u_done")
    up = _mm(hn2, wup_l, kind="nn", out_dtype=F32, name="ffn_up_a", tm_cap=512, tk_cap=d, col_part=(0, 2))
    _, (wup_l,) = _xfer_wait(sv[0], sv[1], [srcs_u[0]], [wup_l], first_u[1], up, name="gather_v_wait")
    pv = _xfer_start([], [wup_l], pass_u[1], name="gather_v_pass")
    _, (wup_g,) = _xfer_wait(pv[0], pv[1], [], pv[3], pass_u[1], pv[4], name="gather_v_done")
    up = _mm(hn2, wup_g, kind="nn", out_dtype=F32, name="ffn_up_b", tm_cap=512, tk_cap=d, col_part=(1, 2), into=up)
    gd, tok = pass_on(gd, up)
    act = _ffn_act_fwd(up, fcw, fcb, nb=nb, seq=seq, name="ffn_act_fwd", deps=(tok,))
    (wdn_g,) = gathered(gd, act)
    wdn_f = wdn_g.reshape(1, fpad, d)
    h2 = _mm(act, wdn_f, kind="nn", out_dtype=F32, name="ffn_down", res=h1, tm_cap=512, tn_cap=1024,
             tk_cap=2 * up_pad)

    dh2, dh2_b, g_fnw, loss_part = _loss_head(h2, final_norm_w.reshape(1, d), loss_target.reshape(t, d),
                                              name="loss_head")
    loss = lax.psum(loss_part[0, 0], ("x", "y", "c"))

    cidx = jnp.reshape(cc, (1,)).astype(jnp.int32)

    def to_sibling(g4, r, tag, deps=()):
        land = lax.empty((4, r, g4.shape[2]), g4.dtype)
        st = _xfer_start([g4], [land], _plan_rs_sibling(r), name=f"rs_sib_start_{tag}", deps=deps)
        return (st, r, tag), st[4]

    def sibling_sum(rs, after):
        st, r, tag = rs
        (g4,), (rcv,) = _xfer_wait(st[0], st[1], st[2], st[3], _plan_rs_sibling(r), after, name=f"rs_sib_wait_{tag}")
        psum = _add_pairs(g4, rcv, cidx, name=f"rs_add_{tag}")
        land = lax.empty((3,) + psum.shape[1:], psum.dtype)
        st2 = _xfer_start([psum], [land], _plan_rs_plane(), name=f"rs_plane_start_{tag}")
        return (st2, tag), st2[4]

    dact = _mm(dh2_b, wdn_f, kind="nt", out_dtype=F32, name="ffn_down_dx", tm_cap=512, tn_cap=up_pad // 2, tk_cap=d)
    g_wdn = _mm(act, dh2_b, kind="tn", out_dtype=BF16, name="ffn_down_dw", tm_cap=1024, tn_cap=1024, tk_cap=t)
    rs_d, tok = to_sibling(g_wdn.reshape(4, up_pad, d), hs, "down")
    dup, s_g, s_v = _ffn_act_bwd(dact, up, fcw, fcb, nb=nb, seq=seq, name="ffn_act_bwd", deps=(tok,))
    pl_d, tok = sibling_sum(rs_d, dup)
    dhn2 = _mm(dup, wup_g, kind="nt", out_dtype=F32, name="ffn_up_dx", tm_cap=512, tn_cap=1024, tk_cap=up_pad,
               k_blocks=2, deps=(tok,))
    g_wup = _mm(dup, hn2, kind="tn", out_dtype=BF16, name="ffn_up_dw", tm_cap=1024, tn_cap=1024, tk_cap=t)
    rs_u, tok = to_sibling(g_wup.reshape(4, 2 * up_pad, d), up_pad, "up")
    dh1, dh1_b, g_ln2 = _rms_bwd(dhn2, h1, ln2_w, name="ln2_bwd", extra=dh2, want_bf16=True, deps=(tok,))
    dmix = _mm(dh1_b, wout_f, kind="nt", out_dtype=F32, name="out_proj_dx", tm_cap=512, tn_cap=1024, tk_cap=d)
    pl_u, tok = sibling_sum(rs_u, dmix)
    g_wout = _mm(mix, dh1_b, kind="tn", out_dtype=BF16, name="out_proj_dw", tn_cap=1024, tk_cap=t, deps=(tok,))
    rs_o, tok = to_sibling(g_wout.reshape(4, d // 4, d), d // NDEV, "out")
    dp_lru, g_lnw = _rms_bwd(dmix, p_lru, lru_norm_w, name="lru_norm_bwd", dy_cb=1, deps=(tok,))
    dxr, dyr, s_lru, g_wa, g_wx = _lru_bwd(proj, lcw, lru_conv_b, wa_b, lru_ba, wx_b, lru_bx, lru_lambda,
                                           h_lru, dp_lru, nb=nb, seq=seq, name="lru_bwd")
    pl_o, tok = sibling_sum(rs_o, dxr)
    dqr, dfr, dir_, dgr, s_hg = _hgrn_bwd(proj, lb_gamma, hg_norm_w, o_pre, states, dmix,
                                          nb=nb, seq=seq, name="hgrn_bwd", deps=(tok,))
    dproj = jnp.concatenate([dqr, dfr, dir_, dgr, dxr, dyr], axis=1)
    g_win = _mm(hn, dproj, kind="tn", out_dtype=BF16, name="in_proj_dw", out_blocks=NDEV, tm_cap=512, tn_cap=in_sh,
                tk_cap=t)
    rs_i, tok = to_sibling(g_win.reshape(4, 2 * d, in_sh), d, "in")
    dhn = _mm(dproj, win_g, kind="nt", out_dtype=F32, name="in_proj_dx", tm_cap=512, tn_cap=1024, tk_cap=in_sh,
              k_blocks=4, deps=(tok,))
    grad_x, g_ln1 = _rms_bwd(dhn, xf, ln1_w, name="ln1_bwd", extra=dh1)

    fcw_parts = jnp.concatenate([s_g[0:3], s_v[0:3]], axis=1).reshape(3, NDEV, up_pad)[:, :, :up_sh]
    fcb_part = jnp.concatenate([s_g[3:4], s_v[3:4]], axis=1).reshape(NDEV, up_pad)[:, :up_sh]
    small_parts = [g_ln1, s_hg[0:2], s_hg[2:3], s_lru[0:4], s_lru[4:5], g_wa, s_lru[5:6], g_wx, s_lru[6:7],
                   s_lru[7:8], g_lnw, g_ln2, fcw_parts, fcb_part, g_fnw]
    reduced = _all_reduce_packed(_pack(small_parts), name="allreduce_small")
    pl_i, _ = sibling_sum(rs_i, reduced)
    summed = _unpack(reduced, [p.shape for p in small_parts])
    (s_ln1, s_lbg, s_hgn, s_lcw, s_lcb, s_wa, s_ba, s_wx, s_bx, s_lam, s_lnw, s_ln2, s_fcw, s_fcb, s_fnw) = summed
    sh_lcw = lru_conv_w.shape[2]
    g_small = {
        "ln1_w": s_ln1, "lb_gamma": s_lbg, "hg_norm_w": s_hgn,
        "lru_conv_w": lax.dynamic_slice_in_dim(s_lcw, me * sh_lcw, sh_lcw, axis=1),
        "lru_conv_b": s_lcb, "lru_wa": s_wa, "lru_ba": s_ba, "lru_wx": s_wx, "lru_bx": s_bx,
        "lru_lambda": s_lam, "lru_norm_w": s_lnw, "ln2_w": s_ln2,
        "ffn_conv_w": lax.dynamic_index_in_dim(s_fcw, me, 1, keepdims=False),
        "ffn_conv_b": s_fcb, "final_norm_w": s_fnw,
    }
    w_small = {"ln1_w": ln1_w, "lb_gamma": lb_gamma, "hg_norm_w": hg_norm_w, "lru_conv_w": lru_conv_w,
               "lru_conv_b": lru_conv_b, "lru_wa": lru_wa, "lru_ba": lru_ba, "lru_wx": lru_wx, "lru_bx": lru_bx,
               "lru_lambda": lru_lambda, "lru_norm_w": lru_norm_w, "ln2_w": ln2_w, "ffn_conv_w": ffn_conv_w,
               "ffn_conv_b": ffn_conv_b, "final_norm_w": final_norm_w}
    m_small = {"ln1_w": m_ln1_w, "lb_gamma": m_lb_gamma, "hg_norm_w": m_hg_norm_w, "lru_conv_w": m_lru_conv_w,
               "lru_conv_b": m_lru_conv_b, "lru_wa": m_lru_wa, "lru_ba": m_lru_ba, "lru_wx": m_lru_wx,
               "lru_bx": m_lru_bx, "lru_lambda": m_lru_lambda, "lru_norm_w": m_lru_norm_w, "ln2_w": m_ln2_w,
               "ffn_conv_w": m_ffn_conv_w, "ffn_conv_b": m_ffn_conv_b, "final_norm_w": m_final_norm_w}
    v_small = {"ln1_w": v_ln1_w, "lb_gamma": v_lb_gamma, "hg_norm_w": v_hg_norm_w, "lru_conv_w": v_lru_conv_w,
               "lru_conv_b": v_lru_conv_b, "lru_wa": v_lru_wa, "lru_ba": v_lru_ba, "lru_wx": v_lru_wx,
               "lru_bx": v_lru_bx, "lru_lambda": v_lru_lambda, "lru_norm_w": v_lru_norm_w, "ln2_w": v_ln2_w,
               "ffn_conv_w": v_ffn_conv_w, "ffn_conv_b": v_ffn_conv_b, "final_norm_w": v_final_norm_w}
    names = list(w_small)
    shapes = [w_small[k].shape for k in names]
    g_small = {k: g_small[k].reshape(w_small[k].shape) for k in names}
    packed = _adamw_packed(_pack([w_small[k] for k in names]), _pack([g_small[k] for k in names]),
                           _pack([m_small[k] for k in names]), _pack([v_small[k] for k in names]),
                           name="adamw_small")
    d_small, nm_small, nv_small = ({k: a for k, a in zip(names, _unpack(pk, shapes))} for pk in packed)

    def finish(pl_x, after, trim=lambda a: a):
        st, tag = pl_x
        (psum,), (rb,) = _xfer_wait(st[0], st[1], st[2], st[3], _plan_rs_plane(), after, name=f"rs_plane_wait_{tag}")
        return trim(lax.dynamic_index_in_dim(psum, plane, 0, keepdims=False)), trim(rb)

    upd_dn = _adamw_shard(ffn_w_down, m_ffn_w_down, v_ffn_w_down, *finish(pl_d, packed[0]), name="adamw_w_down")
    tr3 = lambda a: a.transpose(0, 2, 1)
    upd_up_t = _adamw_shard(tr3(ffn_w_up), tr3(m_ffn_w_up), tr3(v_ffn_w_up), *finish(pl_u, upd_dn[0]),
                            name="adamw_w_up")
    upd_up = [tr3(a) for a in upd_up_t]
    upd_out = _adamw_shard(w_out, m_w_out, v_w_out, *finish(pl_o, upd_up_t[0]), name="adamw_w_out")
    upd_in = _adamw_shard(w_in, m_w_in, v_w_in, *finish(pl_i, upd_out[0]), name="adamw_w_in")

    grads, deltas, new_m, new_v = dict(g_small), dict(d_small), dict(nm_small), dict(nv_small)
    for k, upd in (("w_in", upd_in), ("w_out", upd_out), ("ffn_w_up", upd_up), ("ffn_w_down", upd_dn)):
        grads[k], deltas[k], new_m[k], new_v[k] = upd
    order = ["ln1_w", "w_in", "lb_gamma", "hg_norm_w", "lru_conv_w", "lru_conv_b", "lru_wa", "lru_ba", "lru_wx",
             "lru_bx", "lru_lambda", "lru_norm_w", "w_out", "ln2_w", "ffn_w_up", "ffn_conv_w", "ffn_conv_b",
             "ffn_w_down", "final_norm_w"]
    return (loss, grad_x.reshape(nb, seq, d), *[grads[k] for k in order], *[deltas[k] for k in order],
            *[new_m[k] for k in order], *[new_v[k] for k in order])
```

```python
import math

import jax
import jax.numpy as jnp
from jax import lax
from jax.experimental import pallas as pl
from jax.experimental.pallas import tpu as pltpu

F32, BF16 = jnp.float32, jnp.bfloat16
EPS = 1e-6
HEAD = 128
CHUNK = 64
SUB = 16
NSUB = CHUNK // SUB
LRU_C = 8.0
LANE = 128
NDEV = 8
ADAM_LR, ADAM_B1, ADAM_B2, ADAM_EPS, ADAM_WD, ADAM_STEP = 0.001, 0.9, 0.999, 1e-08, 0.01, 10
MESH = pl.DeviceIdType.MESH
ANY = pl.BlockSpec(memory_space=pl.ANY)
VMEM_LIMIT = 56 * 1024 * 1024


def _cp(**kw):
    return pltpu.CompilerParams(vmem_limit_bytes=VMEM_LIMIT, **kw)


def _tile(n, cap, mult=LANE):
    best = None
    for t in range(mult, min(n, cap) + 1, mult):
        if n % t == 0:
            best = t
    return best if best is not None else n


def _row_tile(r, cdim, budget=262144):
    return _tile(r, max(16, budget // cdim), 16)


def _sigmoid(x):
    return jax.nn.sigmoid(x)


def _dsilu(x, s):
    return s * (1.0 + x * (1.0 - s))


def _iota_rows(n, w=LANE):
    return lax.broadcasted_iota(jnp.int32, (n, w), 0)


def _lane_groups(n, want=2):
    while n % want:
        want //= 2
    return want


def _group_views(refs, g, kinds):
    assert len(refs) == len(kinds)
    cols = pl.ds(g * HEAD, HEAD)
    return [r.at[:, cols] if kind == "l" else r.at[g] for r, kind in zip(refs, kinds)]


def _shift_down(prev8, xt, k):
    cat = jnp.concatenate([prev8, xt], axis=0)
    return pltpu.roll(cat, k, 0)[8:]


def _shift_up(xt, next8, k):
    cat = jnp.concatenate([xt, next8], axis=0)
    n = cat.shape[0]
    return pltpu.roll(cat, n - k, 0)[: xt.shape[0]]


def _scan_fwd(a, u):
    n = a.shape[0]
    row = _iota_rows(n, a.shape[1])
    k = 1
    while k < n:
        keep = row >= k
        a_s = jnp.where(keep, pltpu.roll(a, k, 0), 1.0)
        u_s = jnp.where(keep, pltpu.roll(u, k, 0), 0.0)
        u = a * u_s + u
        a = a * a_s
        k *= 2
    return a, u


def _scan_bwd(a, u):
    n = a.shape[0]
    row = _iota_rows(n, a.shape[1])
    k = 1
    while k < n:
        keep = row < n - k
        a_s = jnp.where(keep, pltpu.roll(a, n - k, 0), 1.0)
        u_s = jnp.where(keep, pltpu.roll(u, n - k, 0), 0.0)
        u = a * u_s + u
        a = a * a_s
        k *= 2
    return a, u


def _cumsum_fwd(u):
    n = u.shape[0]
    row = _iota_rows(n, u.shape[1])
    k = 1
    while k < n:
        u = u + jnp.where(row >= k, pltpu.roll(u, k, 0), 0.0)
        k *= 2
    return u


def _cumsum_bwd(u):
    n = u.shape[0]
    row = _iota_rows(n, u.shape[1])
    k = 1
    while k < n:
        u = u + jnp.where(row < n - k, pltpu.roll(u, n - k, 0), 0.0)
        k *= 2
    return u


def _dot(a, b, dims):
    return lax.dot_general(a.astype(BF16), b.astype(BF16), (dims, ((), ())), preferred_element_type=F32)


NN = ((1,), (0,))
NT = ((1,), (1,))
TN = ((0,), (0,))


def _pcall(body, *, n_in, in_specs, args, deps=(), **kw):
    nd = len(deps)
    if nd:
        inner = body

        def body(*refs):
            return inner(*refs[:n_in], *refs[n_in + nd:])

        in_specs = list(in_specs) + [ANY] * nd
        args = list(args) + list(deps)
    return pl.pallas_call(body, in_specs=in_specs, **kw)(*args)


def _mm(a, b3, *, kind, out_dtype, name, res=None, tm_cap=1024, tn_cap=1536, tk_cap=1024, out_blocks=1, k_blocks=1,
        col_part=None, into=None, deps=()):
    out_ix = lambda i, j, k: (i, j)
    if kind == "nn":
        m, kdim = a.shape
        nb, _, nsh = b3.shape
        n = nb * nsh
        tm, tn, tk = _tile(m, tm_cap), _tile(nsh, tn_cap), _tile(kdim, tk_cap)
        per = nsh // tn
        a_blk, a_ix = (tm, tk), lambda i, j, k: (i, k)
        b_blk, b_ix = (None, tk, tn), lambda i, j, k: (j // per, k, j % per)
        if col_part is not None:
            h, parts = col_part
            tn = nsh // parts
            b_blk, b_ix, out_ix = (None, tk, tn), lambda i, j, k: (j, k, h), lambda i, j, k: (i, j * parts + h)
        dims = NN
    elif kind == "nt":
        m, kdim = a.shape[-2], a.shape[-1] * (a.shape[0] if a.ndim == 3 else 1)
        nb, n, ksh = b3.shape
        tm, tn, tk = _tile(m, tm_cap), _tile(n, tn_cap), _tile(ksh, tk_cap)
        per = ksh // tk
        a_blk, a_ix = (tm, tk), lambda i, j, k: (i, k)
        b_blk, b_ix = (None, tn, tk), lambda i, j, k: (k // per, j, k % per)
        if k_blocks > 1:
            assert tk == ksh and nb % k_blocks == 0
            tk = k_blocks * ksh
            a_blk, b_blk, b_ix = (tm, tk), (k_blocks, tn, ksh), lambda i, j, k: (k, j, 0)
        if a.ndim == 3:
            aper = a.shape[2] // tk
            a_blk, a_ix = (None,) + a_blk, lambda i, j, k: (k // aper, i, k % aper)
        dims = NT
    else:
        kdim, m = a.shape[-2], a.shape[-1] * (a.shape[0] if a.ndim == 3 else 1)
        n = b3.shape[1]
        nsh = n // out_blocks
        tm, tn, tk = _tile(a.shape[-1], tm_cap), _tile(nsh, tn_cap), _tile(kdim, tk_cap)
        per = nsh // tn
        a_blk, a_ix = (tk, tm), lambda i, j, k: (k, i)
        b_blk, b_ix = (tk, tn), lambda i, j, k: (k, j)
        if a.ndim == 3:
            aper = a.shape[2] // tm
            a_blk, a_ix = (None,) + a_blk, lambda i, j, k: (i // aper, k, i % aper)
        dims = TN
    nk = kdim // tk
    j_outer = nk == 1
    nj = nb if col_part is not None else n // tn
    grid = (nj, m // tm, nk) if j_outer else (m // tm, nj, nk)
    at = (lambda f: lambda g0, g1, k: f(g1, g0, k)) if j_outer else (lambda f: f)
    a_spec, b_spec = pl.BlockSpec(a_blk, at(a_ix)), pl.BlockSpec(b_blk, at(b_ix))
    n_args = 2 + (res is not None) + (into is not None)

    def body(*refs):
        a_ref, b_ref = refs[:2]
        r_ref = refs[2] if res is not None else None
        o_ref = refs[n_args]
        if k_blocks > 1:
            w = b_ref.shape[2]
            part = sum(lax.dot_general(a_ref[:, s * w:(s + 1) * w], b_ref[s], (dims, ((), ())),
                                       preferred_element_type=F32) for s in range(k_blocks))
        else:
            part = lax.dot_general(a_ref[...], b_ref[...], (dims, ((), ())), preferred_element_type=F32)
        if nk == 1:
            o_ref[...] = (part if res is None else part + r_ref[...]).astype(o_ref.dtype)
            return
        acc = refs[-1]
        k = pl.program_id(2)

        @pl.when(k == 0)
        def _():
            acc[...] = part

        @pl.when(k > 0)
        def _():
            acc[...] += part

        @pl.when(k == nk - 1)
        def _():
            r = acc[...]
            if res is not None:
                r = r + r_ref[...]
            o_ref[...] = r.astype(o_ref.dtype)

    in_specs = [a_spec, b_spec]
    args = [a, b3]
    if res is not None:
        in_specs.append(pl.BlockSpec((tm, tn), at(out_ix)))
        args.append(res)
    aliases = {}
    if into is not None:
        aliases = {len(args): 0}
        in_specs.append(ANY)
        args.append(into)
    if kind == "tn":
        out_shape = jax.ShapeDtypeStruct((out_blocks, m, nsh), out_dtype)
        out_spec = pl.BlockSpec((None, tm, tn), at(lambda i, j, k: (j // per, i, j % per)))
    else:
        out_shape = jax.ShapeDtypeStruct((m, n), out_dtype)
        out_spec = pl.BlockSpec((tm, tn), at(out_ix))
    return _pcall(
        body, n_in=len(args), in_specs=in_specs, args=args, deps=deps, input_output_aliases=aliases,
        name=name, grid=grid, out_specs=out_spec, out_shape=out_shape,
        scratch_shapes=[pltpu.VMEM((tm, tn), F32)] if nk > 1 else [],
        compiler_params=_cp(dimension_semantics=("parallel", "parallel", "arbitrary")))


def _rms_fwd(x, w, *, name, tm=256, deps=()):
    t, d = x.shape
    tm = _tile(t, tm, 16)

    def body(x_ref, w_ref, o_ref):
        xv = x_ref[...]
        r = lax.rsqrt(jnp.mean(xv * xv, axis=-1, keepdims=True) + EPS)
        o_ref[...] = ((xv * r) * w_ref[...]).astype(o_ref.dtype)

    return _pcall(
        body, n_in=2, args=[x, w], deps=deps, name=name, grid=(t // tm,),
        in_specs=[pl.BlockSpec((tm, d), lambda i: (i, 0)), pl.BlockSpec((1, d), lambda i: (0, 0))],
        out_specs=pl.BlockSpec((tm, d), lambda i: (i, 0)),
        out_shape=jax.ShapeDtypeStruct((t, d), BF16), compiler_params=_cp())


def _rms_bwd(dy, x, w, *, name, extra=None, dy_cb=0, want_bf16=False, tm=256, deps=()):
    t, d = x.shape
    tm = _tile(t, tm, 16)

    def body(*refs):
        refs = list(refs)
        dy_ref, x_ref, w_ref = refs[:3]
        e_ref = refs[3] if extra is not None else None
        outs = refs[4:] if extra is not None else refs[3:]
        dx_ref = outs[0]
        dxb_ref = outs[1] if want_bf16 else None
        dw_ref = outs[-1]
        i = pl.program_id(0)
        xv = x_ref[...]
        r = lax.rsqrt(jnp.mean(xv * xv, axis=-1, keepdims=True) + EPS)
        nh = xv * r
        dyv = dy_ref[...]
        dn = dyv * w_ref[...]
        dx = r * (dn - nh * jnp.mean(dn * nh, axis=-1, keepdims=True))
        if extra is not None:
            dx = dx + e_ref[...]
        dx_ref[...] = dx
        if want_bf16:
            dxb_ref[...] = dx.astype(BF16)
        part = jnp.sum(dyv * nh, axis=0, keepdims=True)

        @pl.when(i == 0)
        def _():
            dw_ref[...] = part

        @pl.when(i > 0)
        def _():
            dw_ref[...] += part

    row = pl.BlockSpec((tm, d), lambda i: (i, 0))
    in_specs = [pl.BlockSpec((tm, d), lambda i: (i, dy_cb)), row, pl.BlockSpec((1, d), lambda i: (0, 0))]
    args = [dy, x, w]
    if extra is not None:
        in_specs.append(row)
        args.append(extra)
    out_shape = [jax.ShapeDtypeStruct((t, d), F32)]
    out_specs = [row]
    if want_bf16:
        out_shape.append(jax.ShapeDtypeStruct((t, d), BF16))
        out_specs.append(row)
    out_shape.append(jax.ShapeDtypeStruct((1, d), F32))
    out_specs.append(pl.BlockSpec((1, d), lambda i: (0, 0)))
    return _pcall(
        body, n_in=len(args), in_specs=in_specs, args=args, deps=deps,
        name=name, grid=(t // tm,), out_specs=out_specs, out_shape=out_shape,
        compiler_params=_cp(dimension_semantics=("arbitrary",)))


def _loss_head(h, w, tgt, *, name, tm=256):
    t, d = h.shape
    tm = _tile(t, tm, 16)

    def body(h_ref, w_ref, t_ref, dh_ref, dhb_ref, dw_ref, loss_ref):
        i = pl.program_id(0)
        xv = h_ref[...]
        wv = w_ref[...]
        r = lax.rsqrt(jnp.mean(xv * xv, axis=-1, keepdims=True) + EPS)
        nh = xv * r
        e = nh * wv - t_ref[...]
        part_loss = jnp.full((1, LANE), 0.5 * jnp.sum(jnp.mean(e * e, axis=-1, keepdims=True)), F32)
        dyv = e * (1.0 / d)
        dn = dyv * wv
        dx = r * (dn - nh * jnp.mean(dn * nh, axis=-1, keepdims=True))
        dh_ref[...] = dx
        dhb_ref[...] = dx.astype(BF16)
        part = jnp.sum(dyv * nh, axis=0, keepdims=True)

        @pl.when(i == 0)
        def _():
            dw_ref[...] = part
            loss_ref[...] = part_loss

        @pl.when(i > 0)
        def _():
            dw_ref[...] += part
            loss_ref[...] += part_loss

    row = pl.BlockSpec((tm, d), lambda i: (i, 0))
    vec = pl.BlockSpec((1, d), lambda i: (0, 0))
    return pl.pallas_call(
        body, name=name, grid=(t // tm,), in_specs=[row, vec, row],
        out_specs=[row, row, vec, pl.BlockSpec((1, LANE), lambda i: (0, 0))],
        out_shape=[jax.ShapeDtypeStruct((t, d), F32), jax.ShapeDtypeStruct((t, d), BF16),
                   jax.ShapeDtypeStruct((1, d), F32), jax.ShapeDtypeStruct((1, LANE), F32)],
        compiler_params=_cp(dimension_semantics=("arbitrary",)),
    )(h, w, tgt)


def _lower_bound(lbg_ref):
    g0, g1 = lbg_ref[0:1, :], lbg_ref[1:2, :]
    m = jnp.maximum(g0, g1)
    e0, e1 = jnp.exp(g0 - m), jnp.exp(g1 - m)
    return e0 / (e0 + e1)


def _seg_bounds():
    offs, o = {}, 0
    for i in range(1, NSUB):
        offs[i] = (o, o + SUB * i)
        o += SUB * i
    return offs, o


def _pad_rows(x, n):
    if x.shape[0] == n:
        return x
    return jnp.concatenate([x, jnp.zeros((n - x.shape[0], x.shape[1]), x.dtype)], axis=0)


def _offdiag_mask():
    offs, total = _seg_bounds()
    padded = -(-total // LANE) * LANE
    rsub = lax.broadcasted_iota(jnp.int32, (CHUNK, padded), 0) // SUB
    col = lax.broadcasted_iota(jnp.int32, (CHUNK, padded), 1)
    cseg = jnp.zeros((CHUNK, padded), jnp.int32)
    for i in range(1, NSUB):
        cseg = cseg + (col >= offs[i][0]).astype(jnp.int32)
    return (rsub == cseg) & (col < total)


def _offdiag_setup(q, k, b, v, b_c, mask):
    offs, total = _seg_bounds()
    padded = -(-total // LANE) * LANE
    eq_parts = [jnp.zeros((SUB, HEAD), F32)]
    ek_parts, k_parts, v_parts = [], [], []
    for i in range(1, NSUB):
        r_i = b_c[SUB * i - 1:SUB * i, :]
        eq_parts.append(jnp.exp(b[SUB * i:SUB * (i + 1)] - r_i))
        ek_parts.append(jnp.exp(r_i - b[0:SUB * i]))
        k_parts.append(k[0:SUB * i])
        v_parts.append(v[0:SUB * i])
    eq = jnp.concatenate(eq_parts, axis=0)
    ek = _pad_rows(jnp.concatenate(ek_parts, axis=0), padded)
    kt = _pad_rows(jnp.concatenate(k_parts, axis=0), padded) * ek
    vs = _pad_rows(jnp.concatenate(v_parts, axis=0), padded)
    qt = q * eq
    a = jnp.where(mask, _dot(qt, kt, NT), 0.0)
    return offs, eq, ek, kt, vs, qt, a


def _hgrn_fwd(proj, lbg, nw, *, nb, seq, name):
    t = proj.shape[0]
    w = lbg.shape[1]
    nh = w // HEAD
    nc = seq // CHUNK
    ng = _lane_groups(nh)

    def head(mask, q_ref, f_ref, i_ref, g_ref, lbg_ref, nw_ref, ohg_ref, opre_ref, st_ref, k_c, b_c, v_c, st):
        lb = _lower_bound(lbg_ref)
        nwv = nw_ref[...]
        st[...] = jnp.zeros_like(st)

        def chunk(c, carry):
            rows = pl.ds(pl.multiple_of(c * CHUNK, CHUNK), CHUNK)
            qr = q_ref[rows, :]
            q = qr * _sigmoid(qr)
            f = lb + (1.0 - lb) * _sigmoid(f_ref[rows, :])
            k = 1.0 - f
            b = _cumsum_fwd(jnp.log(f))
            v = i_ref[rows, :]
            k_c[...] = k
            b_c[...] = b
            v_c[...] = v
            s_t = st[...]
            st_ref[c] = s_t
            o = _dot(q * jnp.exp(b), s_t, NT)
            _, _, _, _, vs, _, a = _offdiag_setup(q, k, b, v, b_c, mask)
            o = o + _dot(a, vs, NN)
            diag = []
            for i in range(NSUB):
                accs = [jnp.zeros((8, HEAD), F32) for _ in range(SUB // 8)]
                for j in range(SUB):
                    r = SUB * i + j
                    bj, kj, vj = b_c[r:r + 1, :], k_c[r:r + 1, :], v_c[r:r + 1, :]
                    for p in range(j // 8, SUB // 8):
                        lo = SUB * i + 8 * p
                        d = jnp.exp(b[lo:lo + 8] - bj)
                        if 8 * p < j:
                            d = jnp.where(_iota_rows(8) + 8 * p >= j, d, 0.0)
                        s = jnp.sum(q[lo:lo + 8] * d * kj, axis=-1, keepdims=True)
                        accs[p] = accs[p] + s * vj
                diag.extend(accs)
            o = o + jnp.concatenate(diag, axis=0)
            bl = b_c[CHUNK - 1:CHUNK, :]
            kb = k * jnp.exp(bl - b)
            st[...] = s_t * jnp.exp(bl) + _dot(v, kb, TN)
            opre_ref[rows, :] = o
            rn = lax.rsqrt(jnp.mean(o * o, axis=-1, keepdims=True) + EPS)
            gr = g_ref[rows, :]
            ohg_ref[rows, :] = (((o * rn) * nwv) * (gr * _sigmoid(gr))).astype(BF16)
            return carry

        return chunk

    def body(*refs):
        mask = _offdiag_mask()
        chunks = [head(mask, *_group_views(refs, g, "llllllll" + "ggggg")) for g in range(ng)]

        def step(c, carry):
            for chunk in chunks:
                chunk(c, carry)
            return carry

        lax.fori_loop(0, nc, step, 0)

    gw = ng * HEAD

    def col(off):
        return pl.BlockSpec((seq, gw), lambda h, b: (b, off * (nh // ng) + h))

    vec = lambda r: pl.BlockSpec((r, gw), lambda h, b: (0, h))
    out_blk = pl.BlockSpec((seq, gw), lambda h, b: (b, h))
    return pl.pallas_call(
        body, name=name, grid=(nh // ng, nb),
        in_specs=[col(0), col(1), col(2), col(3), vec(2), vec(1)],
        out_specs=[out_blk, out_blk, pl.BlockSpec((None, ng, nc, HEAD, HEAD), lambda h, b: (b, h, 0, 0, 0))],
        out_shape=[jax.ShapeDtypeStruct((t, w), BF16), jax.ShapeDtypeStruct((t, w), F32),
                   jax.ShapeDtypeStruct((nb, nh, nc, HEAD, HEAD), F32)],
        scratch_shapes=[pltpu.VMEM((ng, CHUNK, HEAD), F32)] * 3 + [pltpu.VMEM((ng, HEAD, HEAD), F32)],
        compiler_params=_cp(dimension_semantics=("parallel", "parallel")),
    )(proj, proj, proj, proj, lbg, nw)


def _hgrn_bwd(proj, lbg, nw, opre, states, dmix, *, nb, seq, name, deps=()):
    t = proj.shape[0]
    w = lbg.shape[1]
    nh = w // HEAD
    nc = seq // CHUNK
    ng = _lane_groups(nh)

    def head(mask, q_ref, f_ref, i_ref, g_ref, lbg_ref, nw_ref, opre_ref, st_ref, dm_ref,
             dq_ref, df_ref, di_ref, dg_ref, small_ref, k_c, b_c, v_c, dst, dlb_s, dwn_s):
        bi = pl.program_id(1)
        lb = _lower_bound(lbg_ref)
        nwv = nw_ref[...]
        dst[...] = jnp.zeros_like(dst)

        @pl.when(bi == 0)
        def _():
            dlb_s[...] = jnp.zeros_like(dlb_s)
            dwn_s[...] = jnp.zeros_like(dwn_s)

        def chunk(it, carry):
            c = nc - 1 - it
            rows = pl.ds(pl.multiple_of(c * CHUNK, CHUNK), CHUNK)
            qr = q_ref[rows, :]
            sq = _sigmoid(qr)
            q = qr * sq
            sg = _sigmoid(f_ref[rows, :])
            f = lb + (1.0 - lb) * sg
            k = 1.0 - f
            b = _cumsum_fwd(jnp.log(f))
            v = i_ref[rows, :]
            k_c[...] = k
            b_c[...] = b
            v_c[...] = v
            o = opre_ref[rows, :]
            rn = lax.rsqrt(jnp.mean(o * o, axis=-1, keepdims=True) + EPS)
            nhat = o * rn
            dm = dm_ref[rows, :]
            gr = g_ref[rows, :]
            sgr = _sigmoid(gr)
            dnw = dm * (gr * sgr)
            dg_ref[rows, :] = (dm * (nhat * nwv) * _dsilu(gr, sgr)).astype(BF16)
            dwn_s[...] += jnp.sum(dnw * nhat, axis=0, keepdims=True)
            dn = dnw * nwv
            do = rn * (dn - nhat * jnp.mean(dn * nhat, axis=-1, keepdims=True))
            s_t = st_ref[c]
            ds = dst[...]
            eb = jnp.exp(b)
            qb = q * eb
            bl = b_c[CHUNK - 1:CHUNK, :]
            ebl = jnp.exp(bl)
            kdec = jnp.exp(bl - b)
            kb = k * kdec
            dqb = _dot(do, s_t, NN)
            dkb = _dot(v, ds, NN)
            dv = _dot(kb, ds, NT)
            d_ebl = jnp.sum(ds * s_t, axis=0, keepdims=True)
            dst[...] = ds * ebl + _dot(do, qb, TN)
            dq = dqb * eb
            dk = dkb * kdec
            t_kb = dkb * kb
            db = dqb * qb - t_kb
            db_last = jnp.sum(t_kb, axis=0, keepdims=True) + d_ebl * ebl
            offs, eq, ek, kt, vs, qt, a = _offdiag_setup(q, k, b, v, b_c, mask)
            da = jnp.where(mask, _dot(do, vs, NT), 0.0)
            dvs = _dot(a, do, TN)
            dqt = _dot(da, kt, NN)
            dkt = _dot(da, qt, TN)
            dq = dq + dqt * eq
            db = db + dqt * qt
            gk, gb = dkt * ek, dkt * kt
            zero8 = jnp.zeros((8, HEAD), F32)
            off_k, off_v, off_b = ([zero8] * (CHUNK // 8) for _ in range(3))
            for i in range(1, NSUB):
                lo, hi = offs[i]
                for p in range((hi - lo) // 8):
                    rows8 = slice(lo + 8 * p, lo + 8 * p + 8)
                    off_k[p] = off_k[p] + gk[rows8]
                    off_b[p] = off_b[p] - gb[rows8]
                    off_v[p] = off_v[p] + dvs[rows8]
            npv = SUB // 8
            dq_d, dk_d, dv_d, db_d = [], [], [], []
            for i in range(NSUB):
                aq = [zero8 for _ in range(npv)]
                ak = [zero8 for _ in range(npv)]
                av = [off_v[i * npv + p] for p in range(npv)]
                for j in range(SUB):
                    r = SUB * i + j
                    bj, kj, vj = b_c[r:r + 1, :], k_c[r:r + 1, :], v_c[r:r + 1, :]
                    pm_sum, ad_sum = None, None
                    for p in range(j // 8, npv):
                        lo = SUB * i + 8 * p
                        d = jnp.exp(b[lo:lo + 8] - bj)
                        if 8 * p < j:
                            d = jnp.where(_iota_rows(8) + 8 * p >= j, d, 0.0)
                        qd = q[lo:lo + 8] * d
                        dop = do[lo:lo + 8]
                        a_j = jnp.sum(qd * kj, axis=-1, keepdims=True)
                        da_j = jnp.sum(dop * vj, axis=-1, keepdims=True)
                        aq[p] = aq[p] + (da_j * d) * kj
                        pm, ad = da_j * qd, a_j * dop
                        pm_sum = pm if pm_sum is None else pm_sum + pm
                        ad_sum = ad if ad_sum is None else ad_sum + ad
                    pj = j // 8
                    here = _iota_rows(8) == (j - 8 * pj)
                    ak[pj] = ak[pj] + jnp.where(here, jnp.sum(pm_sum, axis=0, keepdims=True), 0.0)
                    av[pj] = av[pj] + jnp.where(here, jnp.sum(ad_sum, axis=0, keepdims=True), 0.0)
                for p in range(npv):
                    lo = SUB * i + 8 * p
                    db_d.append(off_b[i * npv + p] + q[lo:lo + 8] * aq[p] - k[lo:lo + 8] * ak[p])
                    dk_d.append(off_k[i * npv + p] + ak[p])
                dq_d.extend(aq)
                dv_d.extend(av)
            dq = dq + jnp.concatenate(dq_d, axis=0)
            dk = dk + jnp.concatenate(dk_d, axis=0)
            dv = dv + jnp.concatenate(dv_d, axis=0)
            db = db + jnp.concatenate(db_d, axis=0)
            db = db + jnp.where(_iota_rows(CHUNK) == CHUNK - 1, db_last, 0.0)
            dgl = _cumsum_bwd(db)
            dfv = dgl / f - dk
            dlb_s[...] += jnp.sum(dfv * (1.0 - sg), axis=0, keepdims=True)
            df_ref[rows, :] = (dfv * (1.0 - lb) * (sg * (1.0 - sg))).astype(BF16)
            dq_ref[rows, :] = (dq * _dsilu(qr, sq)).astype(BF16)
            di_ref[rows, :] = dv.astype(BF16)
            return carry

        def finish():
            @pl.when(bi == nb - 1)
            def _():
                dgam = dlb_s[...] * lb * (1.0 - lb)
                small_ref[...] = jnp.zeros_like(small_ref)
                small_ref[0:1, :] = dgam
                small_ref[1:2, :] = -dgam
                small_ref[2:3, :] = dwn_s[...]

        return chunk, finish

    def body(*refs):
        mask = _offdiag_mask()
        heads = [head(mask, *_group_views(refs, g, "lllllllgl" + "lllll" + "ggggll")) for g in range(ng)]

        def step(it, carry):
            for chunk, _ in heads:
                chunk(it, carry)
            return carry

        lax.fori_loop(0, nc, step, 0)
        for _, finish in heads:
            finish()

    gw = ng * HEAD

    def col(off):
        return pl.BlockSpec((seq, gw), lambda h, b: (b, off * (nh // ng) + h))

    vec = lambda r: pl.BlockSpec((r, gw), lambda h, b: (0, h))
    blk = pl.BlockSpec((seq, gw), lambda h, b: (b, h))
    dshape = jax.ShapeDtypeStruct((t, w), BF16)
    return _pcall(
        body, n_in=9, args=[proj, proj, proj, proj, lbg, nw, opre, states, dmix], deps=deps,
        name=name, grid=(nh // ng, nb),
        in_specs=[col(0), col(1), col(2), col(3), vec(2), vec(1), blk,
                  pl.BlockSpec((None, ng, nc, HEAD, HEAD), lambda h, b: (b, h, 0, 0, 0)), blk],
        out_specs=[blk, blk, blk, blk, vec(8)],
        out_shape=[dshape, dshape, dshape, dshape, jax.ShapeDtypeStruct((8, w), F32)],
        scratch_shapes=[pltpu.VMEM((ng, CHUNK, HEAD), F32)] * 3 + [pltpu.VMEM((ng, HEAD, HEAD), F32)]
        + [pltpu.VMEM((1, gw), F32)] * 2,
        compiler_params=_cp(dimension_semantics=("parallel", "arbitrary")))


def _expm1(x):
    poly = x * (1.0 + x * (0.5 + x * (1.0 / 6 + x * (1.0 / 24 + x * (1.0 / 120 + x * (1.0 / 720))))))
    return jnp.where(jnp.abs(x) < 0.25, poly, jnp.exp(x) - 1.0)


def _softplus_neg(lam):
    x = -lam
    e = jnp.exp(-jnp.abs(x))
    u = 1.0 + e
    l1p = jnp.where(u == 1.0, e, jnp.log(u) * (e / jnp.where(u == 1.0, 1.0, u - 1.0)))
    return jnp.maximum(x, 0.0) + l1p


_GELU_C = math.sqrt(2.0 / math.pi)


def _gelu(y):
    return 0.5 * y * (1.0 + jnp.tanh(_GELU_C * (y + 0.044715 * (y * y * y))))


def _dgelu(y):
    th = jnp.tanh(_GELU_C * (y + 0.044715 * (y * y * y)))
    return 0.5 * (1.0 + th) + 0.5 * y * (1.0 - th * th) * (_GELU_C * (1.0 + 3 * 0.044715 * (y * y)))


def _lru_gates(xr, prev8, cw_ref, cb, wa_ref, ba, wx_ref, bx, sp, first):
    x3, x2, x1 = _shift_down(prev8, xr, 3), _shift_down(prev8, xr, 2), _shift_down(prev8, xr, 1)
    xb = cb + x3 * cw_ref[0:1, :]
    xb = xb + x2 * cw_ref[1:2, :]
    xb = xb + x1 * cw_ref[2:3, :]
    xb = xb + xr * cw_ref[3:4, :]
    r = _sigmoid(_dot(xb, wa_ref[...], NN) + ba)
    ig = _sigmoid(_dot(xb, wx_ref[...], NN) + bx)
    la = (-LRU_C * r) * sp
    a = jnp.exp(la)
    start = jnp.logical_and(first, _iota_rows(xr.shape[0]) == 0)
    mult = jnp.where(start, 1.0, jnp.sqrt(-_expm1(2.0 * la)))
    return (x3, x2, x1), xb, r, ig, a, mult, start


def _lru_fwd(proj, cw, cb, wa, ba, wx, bx, lam, *, nb, seq, name, deps=()):
    t = proj.shape[0]
    w = cb.shape[1]
    nblk = w // HEAD
    nc = seq // CHUNK
    ng = _lane_groups(nblk, 4)

    def block(x_ref, y_ref, cw_ref, cb_ref, wa_ref, ba_ref, wx_ref, bx_ref, lam_ref, p_ref, h_ref):
        sp = _softplus_neg(lam_ref[...])
        cb_v, ba_v, bx_v = cb_ref[...], ba_ref[...], bx_ref[...]

        def tile(c, carry):
            hc, prev8 = carry
            base = pl.multiple_of(c * CHUNK, CHUNK)
            rows = pl.ds(base, CHUNK)
            xr = x_ref[rows, :]
            _, xb, _, ig, a, mult, _ = _lru_gates(xr, prev8, cw_ref, cb_v, wa_ref, ba_v, wx_ref, bx_v, sp, c == 0)
            ap, up = _scan_fwd(a, xb * ig * mult)
            h = up + ap * hc
            h_ref[rows, :] = h
            p_ref[rows, :] = h * _gelu(y_ref[rows, :])
            h_last = jnp.sum(jnp.where(_iota_rows(8) == 7, h[CHUNK - 8:], 0.0), axis=0, keepdims=True)
            return h_last, xr[CHUNK - 8:]

        return tile

    def body(*refs):
        tiles = [block(*_group_views(refs, g, "llllglgll" + "ll")) for g in range(ng)]

        def step(c, carries):
            return tuple(tile(c, carry) for tile, carry in zip(tiles, carries))

        lax.fori_loop(0, nc, step, ((jnp.zeros((1, HEAD), F32), jnp.zeros((8, HEAD), F32)),) * ng)

    gw = ng * HEAD

    def col(off):
        return pl.BlockSpec((seq, gw), lambda n, b: (b, off * (nblk // ng) + n))

    vec = lambda r: pl.BlockSpec((r, gw), lambda n, b: (0, n))
    mat = pl.BlockSpec((ng, HEAD, HEAD), lambda n, b: (n, 0, 0))
    blk = pl.BlockSpec((seq, gw), lambda n, b: (b, n))
    return _pcall(
        body, n_in=9, args=[proj, proj, cw, cb, wa, ba, wx, bx, lam], deps=deps, name=name, grid=(nblk // ng, nb),
        in_specs=[col(4), col(5), vec(4), vec(1), mat, vec(1), mat, vec(1), vec(1)],
        out_specs=[blk, blk],
        out_shape=[jax.ShapeDtypeStruct((t, w), F32), jax.ShapeDtypeStruct((t, w), F32)],
        compiler_params=_cp(dimension_semantics=("parallel", "parallel")))


def _lru_bwd(proj, cw, cb, wa, ba, wx, bx, lam, hsv, dp, *, nb, seq, name):
    t = proj.shape[0]
    w = cb.shape[1]
    nblk = w // HEAD
    nc = seq // CHUNK
    ng = _lane_groups(nblk, 4)

    def block(x_ref, y_ref, cw_ref, cb_ref, wa_ref, ba_ref, wx_ref, bx_ref, lam_ref, h_ref, dp_ref,
              dx_ref, dy_ref, small_ref, dwa_ref, dwx_ref, a_c, g_c, acc, dwa_s, dwx_s):
        bi = pl.program_id(1)
        lamv = lam_ref[...]
        sp = _softplus_neg(lamv)
        cb_v, ba_v, bx_v = cb_ref[...], ba_ref[...], bx_ref[...]

        @pl.when(bi == 0)
        def _():
            acc[...] = jnp.zeros_like(acc)
            dwa_s[...] = jnp.zeros_like(dwa_s)
            dwx_s[...] = jnp.zeros_like(dwx_s)

        def tile(it, carry):
            g_next, a_next, dxb_next8 = carry
            c = nc - 1 - it
            base = pl.multiple_of(c * CHUNK, CHUNK)
            rows = pl.ds(base, CHUNK)
            before = pl.ds(pl.multiple_of(jnp.maximum(base - 8, 0), 8), 8)
            inner = c > 0
            xr = x_ref[rows, :]
            prev8 = jnp.where(inner, x_ref[before, :], 0.0)
            (x3, x2, x1), xb, r, ig, a, mult, start = _lru_gates(
                xr, prev8, cw_ref, cb_v, wa_ref, ba_v, wx_ref, bx_v, sp, c == 0)
            h = h_ref[rows, :]
            h_m1 = _shift_down(jnp.where(inner, h_ref[before, :], 0.0), h, 1)
            yv = y_ref[rows, :]
            dpv = dp_ref[rows, :]
            dy_ref[rows, :] = (dpv * h * _dgelu(yv)).astype(BF16)
            dh = dpv * _gelu(yv)
            a_up = _shift_up(a, jnp.broadcast_to(a_next, (8, HEAD)), 1)
            ap, gp = _scan_bwd(a_up, dh)
            g = gp + ap * g_next
            a_c[...] = a
            g_c[...] = g
            da = g * h_m1
            gx = g * xb
            dxb = g * ig * mult
            dig = gx * mult
            dmult = jnp.where(start, 0.0, gx * ig)
            dla = da * a - dmult * (a * a) / mult
            dzr = (dla * (-LRU_C * sp)) * (r * (1.0 - r))
            dzi = dig * (ig * (1.0 - ig))
            dxb = dxb + _dot(dzr, wa_ref[...], NT) + _dot(dzi, wx_ref[...], NT)
            dwa_s[...] += _dot(xb, dzr, TN)
            dwx_s[...] += _dot(xb, dzi, TN)
            d1, d2, d3 = (_shift_up(dxb, dxb_next8, s) for s in (1, 2, 3))
            dx = dxb * cw_ref[3:4, :] + d1 * cw_ref[2:3, :] + d2 * cw_ref[1:2, :] + d3 * cw_ref[0:1, :]
            dx_ref[rows, :] = dx.astype(BF16)
            colsum = lambda z: jnp.sum(z, axis=0, keepdims=True)
            acc[0:1, :] += colsum(x3 * dxb)
            acc[1:2, :] += colsum(x2 * dxb)
            acc[2:3, :] += colsum(x1 * dxb)
            acc[3:4, :] += colsum(xr * dxb)
            acc[4:5, :] += colsum(dxb)
            acc[5:6, :] += colsum(dzr)
            acc[6:7, :] += colsum(dzi)
            acc[7:8, :] += colsum(dla * (-LRU_C * r))
            return g_c[0:1, :], a_c[0:1, :], dxb[0:8]

        def finish():
            @pl.when(bi == nb - 1)
            def _():
                small_ref[...] = acc[...]
                small_ref[7:8, :] = acc[7:8, :] * (-_sigmoid(-lamv))
                dwa_ref[...] = dwa_s[...]
                dwx_ref[...] = dwx_s[...]

        return tile, finish

    def body(*refs):
        blocks = [block(*_group_views(refs, g, "llllglgllll" + "lllgg" + "gglgg")) for g in range(ng)]

        def step(it, carries):
            return tuple(tile(it, carry) for (tile, _), carry in zip(blocks, carries))

        zero = jnp.zeros((1, HEAD), F32)
        lax.fori_loop(0, nc, step, ((zero, zero, jnp.zeros((8, HEAD), F32)),) * ng)
        for _, finish in blocks:
            finish()

    gw = ng * HEAD

    def col(off):
        return pl.BlockSpec((seq, gw), lambda n, b: (b, off * (nblk // ng) + n))

    vec = lambda r: pl.BlockSpec((r, gw), lambda n, b: (0, n))
    mat = pl.BlockSpec((ng, HEAD, HEAD), lambda n, b: (n, 0, 0))
    blk = pl.BlockSpec((seq, gw), lambda n, b: (b, n))
    dshape = jax.ShapeDtypeStruct((t, w), BF16)
    return pl.pallas_call(
        body, name=name, grid=(nblk // ng, nb),
        in_specs=[col(4), col(5), vec(4), vec(1), mat, vec(1), mat, vec(1), vec(1), blk, blk],
        out_specs=[blk, blk, vec(8), mat, mat],
        out_shape=[dshape, dshape, jax.ShapeDtypeStruct((8, w), F32),
                   jax.ShapeDtypeStruct((nblk, HEAD, HEAD), F32), jax.ShapeDtypeStruct((nblk, HEAD, HEAD), F32)],
        scratch_shapes=[pltpu.VMEM((ng, CHUNK, HEAD), F32)] * 2 + [pltpu.VMEM((8, gw), F32)]
        + [pltpu.VMEM((ng, HEAD, HEAD), F32)] * 2,
        compiler_params=_cp(dimension_semantics=("parallel", "arbitrary")),
    )(proj, proj, cw, cb, wa, ba, wx, bx, lam, hsv, dp)


def _ffn_conv(x, prev8, cw_ref, cbv):
    x2, x1 = _shift_down(prev8, x, 2), _shift_down(prev8, x, 1)
    y = cbv + x2 * cw_ref[0:1, :]
    y = y + x1 * cw_ref[1:2, :]
    y = y + x * cw_ref[2:3, :]
    return (x2, x1), y


def _ffn_conv_at(x_ref, base, rt, zero8, cw_ref, cbv):
    x = x_ref[base:base + rt, :]
    if base == 0:
        return _ffn_conv(x, zero8, cw_ref, cbv)[1]
    y = cbv + x_ref[base - 2:base - 2 + rt, :] * cw_ref[0:1, :]
    y = y + x_ref[base - 1:base - 1 + rt, :] * cw_ref[1:2, :]
    return y + x * cw_ref[2:3, :]


def _ffn_act_fwd(up, cw, cb, *, nb, seq, name, deps=()):
    t, f2 = up.shape
    f = f2 // 2
    tc = _tile(f, 512)
    nj = f // tc
    rt = _tile(seq, 65536 // tc, 16)
    nr = seq // rt

    def body(g_ref, v_ref, cwg_ref, cwv_ref, cbg_ref, cbv_ref, o_ref):
        cbg, cbv = cbg_ref[...], cbv_ref[...]
        z = jnp.zeros((8, tc), F32)
        for c in range(nr):
            gate = _ffn_conv_at(g_ref, c * rt, rt, z, cwg_ref, cbg)
            val = _ffn_conv_at(v_ref, c * rt, rt, z, cwv_ref, cbv)
            o_ref[c * rt:(c + 1) * rt, :] = ((gate * _sigmoid(gate)) * val).astype(BF16)

    gcol = pl.BlockSpec((seq, tc), lambda j, b: (b, j))
    vcol = pl.BlockSpec((seq, tc), lambda j, b: (b, nj + j))
    gv = lambda r: pl.BlockSpec((r, tc), lambda j, b: (0, j))
    vv = lambda r: pl.BlockSpec((r, tc), lambda j, b: (0, nj + j))
    return _pcall(
        body, n_in=6, args=[up, up, cw, cw, cb, cb], deps=deps, name=name, grid=(nj, nb),
        in_specs=[gcol, vcol, gv(3), vv(3), gv(1), vv(1)], out_specs=gcol,
        out_shape=jax.ShapeDtypeStruct((t, f), BF16),
        compiler_params=_cp(dimension_semantics=("parallel", "parallel")))


def _ffn_act_bwd(dact, up, cw, cb, *, nb, seq, name, deps=()):
    t, f2 = up.shape
    f = f2 // 2
    tc = _tile(f, 512)
    nj = f // tc
    rt = _tile(seq, 65536 // tc, 16)
    nr = seq // rt

    def body(da_ref, g_ref, v_ref, cwg_ref, cwv_ref, cbg_ref, cbv_ref,
             d_ref, sg_ref, sv_ref, eg, ev, accg, accv):
        bi = pl.program_id(1)
        cbg, cbv = cbg_ref[...], cbv_ref[...]

        @pl.when(bi == 0)
        def _():
            accg[...] = jnp.zeros_like(accg)
            accv[...] = jnp.zeros_like(accv)

        colsum = lambda z: jnp.sum(z, axis=0, keepdims=True)

        z = jnp.zeros((8, tc), F32)
        eg[seq:seq + 8, :] = z
        ev[seq:seq + 8, :] = z

        def first(c, carry):
            pg, pv = carry
            rows = pl.ds(pl.multiple_of(c * rt, rt), rt)
            xg, xv = g_ref[rows, :], v_ref[rows, :]
            (g2, g1), gate = _ffn_conv(xg, pg, cwg_ref, cbg)
            (v2, v1), val = _ffn_conv(xv, pv, cwv_ref, cbv)
            s = _sigmoid(gate)
            da = da_ref[rows, :]
            dgate = da * val * _dsilu(gate, s)
            dval = da * (gate * s)
            eg[rows, :] = dgate
            ev[rows, :] = dval
            for acc, (s2, s1, s0), d in ((accg, (g2, g1, xg), dgate), (accv, (v2, v1, xv), dval)):
                acc[0:1, :] += colsum(s2 * d)
                acc[1:2, :] += colsum(s1 * d)
                acc[2:3, :] += colsum(s0 * d)
                acc[3:4, :] += colsum(d)
            return xg[rt - 8:], xv[rt - 8:]

        lax.fori_loop(0, nr, first, (z, z))

        for c in range(nr):
            base = c * rt
            for half, (e, cw_ref) in enumerate(((eg, cwg_ref), (ev, cwv_ref))):
                d, d1, d2 = e[base:base + rt, :], e[base + 1:base + 1 + rt, :], e[base + 2:base + 2 + rt, :]
                d_ref[half, base:base + rt, :] = (
                    d * cw_ref[2:3, :] + d1 * cw_ref[1:2, :] + d2 * cw_ref[0:1, :]).astype(BF16)

        @pl.when(bi == nb - 1)
        def _():
            sg_ref[...] = accg[...]
            sv_ref[...] = accv[...]

    gcol = pl.BlockSpec((seq, tc), lambda j, b: (b, j))
    vcol = pl.BlockSpec((seq, tc), lambda j, b: (b, nj + j))
    gv = lambda r: pl.BlockSpec((r, tc), lambda j, b: (0, j))
    vv = lambda r: pl.BlockSpec((r, tc), lambda j, b: (0, nj + j))
    dshape = jax.ShapeDtypeStruct((2, t, f), BF16)
    sshape = jax.ShapeDtypeStruct((8, f), F32)
    return _pcall(
        body, n_in=7, args=[dact, up, up, cw, cw, cb, cb], deps=deps, name=name, grid=(nj, nb),
        in_specs=[gcol, gcol, vcol, gv(3), vv(3), gv(1), vv(1)],
        out_specs=[pl.BlockSpec((2, seq, tc), lambda j, b: (0, b, j)), gv(8), gv(8)],
        out_shape=[dshape, sshape, sshape],
        scratch_shapes=[pltpu.VMEM((seq + 8, tc), F32)] * 2 + [pltpu.VMEM((8, tc), F32)] * 2,
        compiler_params=_cp(dimension_semantics=("parallel", "arbitrary")))


def _adamw_math(wv, g, mv, vv):
    m = ADAM_B1 * mv + (1.0 - ADAM_B1) * g
    v = ADAM_B2 * vv + (1.0 - ADAM_B2) * (g * g)
    m_hat = m / (1.0 - ADAM_B1 ** ADAM_STEP)
    v_hat = v / (1.0 - ADAM_B2 ** ADAM_STEP)
    delta = -ADAM_LR * (m_hat / (jnp.sqrt(v_hat) + ADAM_EPS) + ADAM_WD * wv)
    return delta, m, v


def _adamw_shard(wv, mv, vv, p_own, rb, *, name):
    _, r, cdim = wv.shape
    tr = _row_tile(r, cdim)

    def body(w_ref, m_ref, v_ref, p_ref, rb_ref, g_ref, d_ref, mo_ref, vo_ref):
        g = p_ref[...].astype(F32)
        for k in range(3):
            g = g + rb_ref[k].astype(F32)
        g_ref[...] = g
        d_ref[...], mo_ref[...], vo_ref[...] = _adamw_math(w_ref[...], g, m_ref[...], v_ref[...])

    row = pl.BlockSpec((tr, cdim), lambda i: (i, 0))
    row3 = pl.BlockSpec((None, tr, cdim), lambda i: (0, i, 0))
    shp = jax.ShapeDtypeStruct((1, r, cdim), F32)
    return pl.pallas_call(
        body, name=name, grid=(r // tr,),
        in_specs=[row3, row3, row3, row, pl.BlockSpec((3, tr, cdim), lambda i: (0, i, 0))],
        out_specs=[row3] * 4, out_shape=[shp] * 4, compiler_params=_cp(dimension_semantics=("parallel",)),
    )(wv, mv, vv, p_own, rb)


def _adamw_packed(wv, g, mv, vv, *, name):
    r = wv.shape[0]
    tr = _tile(r, 256, 8)

    def body(w_ref, g_ref, m_ref, v_ref, d_ref, mo_ref, vo_ref):
        d_ref[...], mo_ref[...], vo_ref[...] = _adamw_math(w_ref[...], g_ref[...], m_ref[...], v_ref[...])

    row = pl.BlockSpec((tr, LANE), lambda i: (i, 0))
    shp = jax.ShapeDtypeStruct((r, LANE), F32)
    return pl.pallas_call(
        body, name=name, grid=(r // tr,), in_specs=[row] * 4, out_specs=[row] * 3, out_shape=[shp] * 3,
        compiler_params=_cp(dimension_semantics=("parallel",)),
    )(wv, g, mv, vv)


def _place():
    return lax.axis_index("x"), lax.axis_index("y"), lax.axis_index("c")


def _all_reduce_packed(p, *, name):
    r = p.shape[0]
    tr = _tile(r, 256, 8)

    def body(p_ref, o_ref, buf, send, recv, lsem):
        x, y, c = _place()
        me, sibling = (x, y, c), (x, y, 1 - c)
        chips = [(1 - x, y), (x, 1 - y), (1 - x, 1 - y)]

        def slot(px, py, pc):
            return 4 * px + 2 * py + pc

        def copy(k, block, to, src=None):
            dst = buf.at[slot(*block)]
            return pltpu.make_async_remote_copy(
                src_ref=dst if src is None else src, dst_ref=dst, send_sem=send.at[k], recv_sem=recv.at[k],
                device_id=to, device_id_type=MESH)

        mine = pltpu.make_async_copy(p_ref, buf.at[slot(*me)], lsem)
        mine.start()
        first = [copy(0, me, sibling, src=p_ref)]
        first += [copy(1 + j, me, (*chip, c), src=p_ref) for j, chip in enumerate(chips)]
        for cp in first:
            cp.start()
        passed = [copy(4 + j, (*chip, c), sibling) for j, chip in enumerate(chips)]
        for j, chip in enumerate(chips):
            copy(1 + j, (*chip, c), me).wait_recv()
            passed[j].start()
        copy(0, sibling, me).wait_recv()
        for j, chip in enumerate(chips):
            copy(4 + j, (*chip, 1 - c), me).wait_recv()
        for cp in first + passed:
            cp.wait_send()
        mine.wait()

        def add(i, carry):
            rows = pl.ds(pl.multiple_of(i * tr, tr), tr)
            s = buf[0, rows, :]
            for d in range(1, NDEV):
                s = s + buf[d, rows, :]
            o_ref[rows, :] = s
            return carry

        lax.fori_loop(0, r // tr, add, 0)

    vm = pl.BlockSpec(memory_space=pltpu.VMEM)
    return pl.pallas_call(
        body, name=name, in_specs=[vm], out_specs=vm, out_shape=jax.ShapeDtypeStruct(p.shape, p.dtype),
        scratch_shapes=[pltpu.VMEM((NDEV,) + p.shape, p.dtype), pltpu.SemaphoreType.DMA((7,)),
                        pltpu.SemaphoreType.DMA((7,)), pltpu.SemaphoreType.DMA],
        compiler_params=_cp(),
    )(p)


HBM = pl.BlockSpec(memory_space=pltpu.HBM)
SEM = pl.BlockSpec(memory_space=pltpu.SEMAPHORE)
EFFECT = pltpu.SideEffectType.DATAFLOW_SIDE_EFFECTING


def _plan_copies(plan, s_refs, l_refs, send, recv):
    def pick(kind, a, idx):
        ref = (s_refs if kind == "s" else l_refs)[a]
        return ref if idx is None else ref.at[idx]

    return [pltpu.make_async_remote_copy(
        src_ref=pick(*src), dst_ref=pick(*dst), send_sem=send.at[i], recv_sem=recv.at[i],
        device_id=to, device_id_type=MESH) for i, (src, dst, to) in enumerate(plan(*_place()))]


def _xfer_start(srcs, lands, plan, *, name, deps=()):
    ns, nl = len(srcs), len(lands)
    nd = len(deps)
    ncopy = len(plan(0, 0, 0))

    def body(*refs):
        s_refs, l_refs = refs[:ns], refs[ns:ns + nl]
        send, recv = refs[ns + nl + nd], refs[ns + nl + nd + 1]
        token = refs[-1]
        for cp in _plan_copies(plan, s_refs, l_refs, send, recv):
            cp.start()
        token[...] = jnp.zeros_like(token)

    bufs = list(srcs) + list(lands)
    outs = pl.pallas_call(
        body, name=name,
        out_shape=(pltpu.SemaphoreType.DMA((ncopy,)), pltpu.SemaphoreType.DMA((ncopy,)),
                   *[pltpu.HBM(b.shape, b.dtype) for b in bufs], jax.ShapeDtypeStruct((8, LANE), F32)),
        in_specs=[HBM] * (ns + nl) + [ANY] * nd,
        out_specs=(SEM, SEM, *[HBM] * (ns + nl), pl.BlockSpec(memory_space=pltpu.VMEM)),
        input_output_aliases={i: 2 + i for i in range(ns + nl)},
        compiler_params=pltpu.CompilerParams(has_side_effects=EFFECT),
    )(*[pltpu.with_memory_space_constraint(b, pltpu.HBM) for b in bufs], *deps)
    return outs[0], outs[1], list(outs[2:2 + ns]), list(outs[2 + ns:2 + ns + nl]), outs[-1]


def _xfer_wait(send, recv, srcs, lands, plan, after, *, name):
    ns, nl = len(srcs), len(lands)

    def body(*refs):
        s_refs, l_refs = refs[:ns], refs[ns:ns + nl]
        send_ref, recv_ref = refs[ns + nl], refs[ns + nl + 1]
        for cp in _plan_copies(plan, s_refs, l_refs, send_ref, recv_ref):
            cp.wait_send()
            cp.wait_recv()

    bufs = list(srcs) + list(lands)
    outs = pl.pallas_call(
        body, name=name, out_shape=tuple(pltpu.HBM(b.shape, b.dtype) for b in bufs),
        in_specs=[HBM] * (ns + nl) + [SEM, SEM, ANY], out_specs=tuple([HBM] * (ns + nl)),
        input_output_aliases={i: i for i in range(ns + nl)},
        compiler_params=pltpu.CompilerParams(has_side_effects=EFFECT),
    )(*bufs, send, recv, after)
    return list(outs[:ns]), list(outs[ns:])


def _slot8(px, py, pc):
    return 4 * px + 2 * py + pc


def _slot4(r):
    return lambda px, py, pc: (2 * px + py, pl.ds(pc * r, r))


def _plan_gather_first(slots, windows=None):
    windows = windows or [None] * len(slots)

    def plan(x, y, c):
        peers = [(x, y, c), (x, y, 1 - c), (1 - x, y, c), (x, 1 - y, c), (1 - x, 1 - y, c)]
        return [(("s", a, windows[a]), ("l", a, slot(x, y, c)), to) for a, slot in enumerate(slots) for to in peers]
    return plan


def _plan_gather_pass(slots):
    def plan(x, y, c):
        chips = [(1 - x, y), (x, 1 - y), (1 - x, 1 - y)]
        return [(("l", a, slot(px, py, c)), ("l", a, slot(px, py, c)), (x, y, 1 - c))
                for a, slot in enumerate(slots) for px, py in chips]
    return plan


def _plan_rs_sibling(r):
    def plan(x, y, c):
        return [(("s", 0, (j, pl.ds((1 - c) * r, r))), ("l", 0, j), (x, y, 1 - c)) for j in range(4)]
    return plan


def _plan_rs_plane():
    def plan(x, y, c):
        chips = [(1 - x, y), (x, 1 - y), (1 - x, 1 - y)]
        return [(("s", 0, 2 * px + py), ("l", 0, k), (px, py, c)) for k, (px, py) in enumerate(chips)]
    return plan


def _add_pairs(g, rcv, cidx, *, name):
    _, r, cdim = rcv.shape
    tr = _row_tile(r, cdim, budget=4194304)
    per = r // tr

    def body(c_ref, g_ref, r_ref, o_ref):
        o_ref[...] = (g_ref[...].astype(F32) + r_ref[...].astype(F32)).astype(o_ref.dtype)

    grid_spec = pltpu.PrefetchScalarGridSpec(
        num_scalar_prefetch=1, grid=(4, per),
        in_specs=[pl.BlockSpec((None, tr, cdim), lambda j, i, c_ref: (j, c_ref[0] * per + i, 0)),
                  pl.BlockSpec((None, tr, cdim), lambda j, i, c_ref: (j, i, 0))],
        out_specs=pl.BlockSpec((None, tr, cdim), lambda j, i, c_ref: (j, i, 0)))
    return pl.pallas_call(
        body, name=name, grid_spec=grid_spec, out_shape=jax.ShapeDtypeStruct((4, r, cdim), g.dtype),
        compiler_params=_cp(dimension_semantics=("parallel", "parallel")),
    )(cidx, g, rcv)


def _pack(arrs):
    flat = jnp.concatenate([a.reshape(-1).astype(F32) for a in arrs])
    n = flat.shape[0]
    rows = -(-n // LANE)
    rows = -(-rows // 256) * 256
    return jnp.pad(flat, (0, rows * LANE - n)).reshape(rows, LANE)


def _unpack(packed, shapes):
    flat = packed.reshape(-1)
    out, o = [], 0
    for s in shapes:
        n = math.prod(s)
        out.append(flat[o:o + n].reshape(s))
        o += n
    return out


def _pad_blocks(a, w, wp):
    lead = a.shape[:-1]
    k = a.shape[-1] // w
    pads = [(0, 0)] * (len(lead) + 1) + [(0, wp - w)]
    return jnp.pad(a.reshape(*lead, k, w), pads).reshape(*lead, k * wp)


def kernel(x, ln1_w, w_in, lb_gamma, hg_norm_w, lru_conv_w, lru_conv_b, lru_wa, lru_ba, lru_wx, lru_bx, lru_lambda, lru_norm_w, w_out, ln2_w, ffn_w_up, ffn_conv_w, ffn_conv_b, ffn_w_down, final_norm_w, loss_target, m_ln1_w, m_w_in, m_lb_gamma, m_hg_norm_w, m_lru_conv_w, m_lru_conv_b, m_lru_wa, m_lru_ba, m_lru_wx, m_lru_bx, m_lru_lambda, m_lru_norm_w, m_w_out, m_ln2_w, m_ffn_w_up, m_ffn_conv_w, m_ffn_conv_b, m_ffn_w_down, m_final_norm_w, v_ln1_w, v_w_in, v_lb_gamma, v_hg_norm_w, v_lru_conv_w, v_lru_conv_b, v_lru_wa, v_lru_ba, v_lru_wx, v_lru_bx, v_lru_lambda, v_lru_norm_w, v_w_out, v_ln2_w, v_ffn_w_up, v_ffn_conv_w, v_ffn_conv_b, v_ffn_w_down, v_final_norm_w):
    nb, seq, d = x.shape
    t = nb * seq
    wmix = d // 2
    in_sh = w_in.shape[2]
    up_sh = ffn_w_up.shape[2]
    up_pad = -(-up_sh // LANE) * LANE
    hs = ffn_w_down.shape[1]
    fpad = 4 * up_pad
    cx, cy, cc = _place()
    me = 4 * cx + 2 * cy + cc
    plane = 2 * cx + cy

    def gather(shards, lands, slots, tag, deps=(), also=lambda x, y, c: []):
        first = _plan_gather_first(slots)
        plan = lambda x, y, c: first(x, y, c) + also(x, y, c)
        st = _xfer_start(shards, lands, plan, name=f"gather_{tag}_start", deps=deps)
        return (st, plan, slots, tag), st[4]

    def pass_on(g, after):
        st, plan, slots, tag = g
        _, lands = _xfer_wait(st[0], st[1], st[2], st[3], plan, after, name=f"gather_{tag}_wait")
        st2 = _xfer_start([], lands, _plan_gather_pass(slots), name=f"gather_{tag}_pass")
        return (st2, slots, tag), st2[4]

    def gathered(g, after):
        st2, slots, tag = g
        return _xfer_wait(st2[0], st2[1], [], st2[3], _plan_gather_pass(slots), after, name=f"gather_{tag}_done")[1]

    land8 = lambda s: lax.empty((NDEV,) + s.shape, s.dtype)

    win_s = w_in[0].astype(BF16)
    wout_s = w_out[0].astype(BF16)
    wup_s = jnp.pad(ffn_w_up[0], ((0, 0), (0, up_pad - up_sh))).astype(BF16)
    wdn_s = ffn_w_down[0].astype(BF16)
    fcw_s = jnp.pad(ffn_conv_w[0], ((0, 0), (0, up_pad - up_sh)))
    half = in_sh // 2
    all_rows = pl.ds(0, d)
    win_slot = lambda h: lambda px, py, pc: (_slot8(px, py, pc), all_rows, pl.ds(h * half, half))
    first_a = _plan_gather_first([win_slot(0), _slot8, _slot8], [(all_rows, pl.ds(0, half)), None, None])
    first_b = _plan_gather_first([win_slot(1)], [(all_rows, pl.ds(half, half))])
    pass_a, pass_b = _plan_gather_pass([win_slot(0), _slot8, _slot8]), _plan_gather_pass([win_slot(1)])
    sa = _xfer_start([win_s, lru_conv_w[0], fcw_s], [land8(win_s), land8(lru_conv_w[0]), land8(fcw_s)], first_a,
                     name="gather_a_start")
    sb = _xfer_start([sa[2][0]], [sa[3][0]], first_b, name="gather_b_start", deps=(sa[4],))
    go, tok = gather([wout_s], [land8(wout_s)], [_slot8], "o", deps=(sb[4],))
    uhalf = up_pad // 2
    up_slot = lambda h: lambda px, py, pc: (_slot8(px, py, pc), all_rows, pl.ds(h * uhalf, uhalf))
    first_u = [_plan_gather_first([up_slot(h)], [(all_rows, pl.ds(h * uhalf, uhalf))]) for h in range(2)]
    pass_u = [_plan_gather_pass([up_slot(h)]) for h in range(2)]
    su = _xfer_start([wup_s], [land8(wup_s)], first_u[0], name="gather_u_start", deps=(tok,))
    sv = _xfer_start([su[2][0]], [su[3][0]], first_u[1], name="gather_v_start", deps=(su[4],))
    tok = sv[4]
    npad = up_pad - 2 * hs
    zero_pad = lambda x, y, c: [(("s", 1, None), ("l", 0, (j, pl.ds(2 * hs, npad))), (x, y, c)) for j in range(4)]
    gd, tok = gather([wdn_s] + ([jnp.zeros((npad, d), BF16)] if npad else []), [lax.empty((4, up_pad, d), BF16)],
                     [_slot4(hs)], "d", deps=(tok,), also=zero_pad if npad else lambda x, y, c: [])
    fcb = _pad_blocks(ffn_conv_b, up_sh, up_pad)
    wa_b, wx_b = lru_wa[0].astype(BF16), lru_wx[0].astype(BF16)

    xf = x.reshape(t, d)
    hn = _rms_fwd(xf, ln1_w, name="ln1_fwd", deps=(tok,))
    srcs_a, lands_a = _xfer_wait(sa[0], sa[1], [sb[2][0], sa[2][1], sa[2][2]], [sb[3][0], sa[3][1], sa[3][2]],
                                 first_a, hn, name="gather_a_wait")
    pa = _xfer_start([], lands_a, pass_a, name="gather_a_pass")
    _, (win_l, lcw_g, fcw_g) = _xfer_wait(pa[0], pa[1], [], pa[3], pass_a, pa[4], name="gather_a_done")
    lcw = lcw_g.transpose(1, 0, 2).reshape(lru_conv_w.shape[1], wmix)
    fcw = fcw_g.transpose(1, 0, 2).reshape(ffn_conv_w.shape[1], 2 * fpad)
    proj = _mm(hn, win_l, kind="nn", out_dtype=F32, name="in_proj_a", tm_cap=512, tk_cap=d, col_part=(0, 2))
    _, (win_l,) = _xfer_wait(sb[0], sb[1], [srcs_a[0]], [win_l], first_b, proj, name="gather_b_wait")
    pb = _xfer_start([], [win_l], pass_b, name="gather_b_pass")
    _, (win_g,) = _xfer_wait(pb[0], pb[1], [], pb[3], pass_b, pb[4], name="gather_b_done")
    proj = _mm(hn, win_g, kind="nn", out_dtype=F32, name="in_proj_b", tm_cap=512, tk_cap=d, col_part=(1, 2),
               into=proj)
    go, tok = pass_on(go, proj)
    o_hg, o_pre, states = _hgrn_fwd(proj, lb_gamma, hg_norm_w, nb=nb, seq=seq, name="hgrn_fwd")
    (wout_g,) = gathered(go, o_pre)
    wout_f = wout_g.reshape(1, d, d)
    p_lru, h_lru = _lru_fwd(proj, lcw, lru_conv_b, wa_b, lru_ba, wx_b, lru_bx, lru_lambda,
                            nb=nb, seq=seq, name="lru_fwd", deps=(tok,))
    o_lru = _rms_fwd(p_lru, lru_norm_w, name="lru_norm_fwd")
    srcs_u, lands_u = _xfer_wait(su[0], su[1], [sv[2][0]], [sv[3][0]], first_u[0], o_lru, name="gather_u_wait")
    pu = _xfer_start([], lands_u, pass_u[0], name="gather_u_pass")
    mix = jnp.concatenate([o_hg, o_lru], axis=1)
    h1 = _mm(mix, wout_f, kind="nn", out_dtype=F32, name="out_proj", res=xf, tm_cap=512, tk_cap=d, deps=(pu[4],))
    hn2 = _rms_fwd(h1, ln2_w, name="ln2_fwd")
    _, (wup_l,) = _xfer_wait(pu[0], pu[1], [], pu[3], pass_u[0], hn2, name="gather_u_done")
    up = _mm(hn2, wup_l, kind="nn", out_dtype=F32, name="ffn_up_a", tm_cap=512, tk_cap=d, col_part=(0, 2))
    _, (wup_l,) = _xfer_wait(sv[0], sv[1], [srcs_u[0]], [wup_l], first_u[1], up, name="gather_v_wait")
    pv = _xfer_start([], [wup_l], pass_u[1], name="gather_v_pass")
    _, (wup_g,) = _xfer_wait(pv[0], pv[1], [], pv[3], pass_u[1], pv[4], name="gather_v_done")
    up = _mm(hn2, wup_g, kind="nn", out_dtype=F32, name="ffn_up_b", tm_cap=512, tk_cap=d, col_part=(1, 2), into=up)
    gd, tok = pass_on(gd, up)
    act = _ffn_act_fwd(up, fcw, fcb, nb=nb, seq=seq, name="ffn_act_fwd", deps=(tok,))
    (wdn_g,) = gathered(gd, act)
    wdn_f = wdn_g.reshape(1, fpad, d)
    h2 = _mm(act, wdn_f, kind="nn", out_dtype=F32, name="ffn_down", res=h1, tm_cap=512, tn_cap=1024,
             tk_cap=2 * up_pad)

    dh2, dh2_b, g_fnw, loss_part = _loss_head(h2, final_norm_w.reshape(1, d), loss_target.reshape(t, d),
                                              name="loss_head")
    loss = lax.psum(loss_part[0, 0], ("x", "y", "c"))

    cidx = jnp.reshape(cc, (1,)).astype(jnp.int32)

    def to_sibling(g4, r, tag, deps=()):
        land = lax.empty((4, r, g4.shape[2]), g4.dtype)
        st = _xfer_start([g4], [land], _plan_rs_sibling(r), name=f"rs_sib_start_{tag}", deps=deps)
        return (st, r, tag), st[4]

    def sibling_sum(rs, after):
        st, r, tag = rs
        (g4,), (rcv,) = _xfer_wait(st[0], st[1], st[2], st[3], _plan_rs_sibling(r), after, name=f"rs_sib_wait_{tag}")
        psum = _add_pairs(g4, rcv, cidx, name=f"rs_add_{tag}")
        land = lax.empty((3,) + psum.shape[1:], psum.dtype)
        st2 = _xfer_start([psum], [land], _plan_rs_plane(), name=f"rs_plane_start_{tag}")
        return (st2, tag), st2[4]

    dact = _mm(dh2_b, wdn_f, kind="nt", out_dtype=F32, name="ffn_down_dx", tm_cap=512, tn_cap=up_pad // 2, tk_cap=d)
    g_wdn = _mm(act, dh2_b, kind="tn", out_dtype=BF16, name="ffn_down_dw", tm_cap=1024, tn_cap=1024, tk_cap=t)
    rs_d, tok = to_sibling(g_wdn.reshape(4, up_pad, d), hs, "down")
    dup, s_g, s_v = _ffn_act_bwd(dact, up, fcw, fcb, nb=nb, seq=seq, name="ffn_act_bwd", deps=(tok,))
    pl_d, tok = sibling_sum(rs_d, dup)
    dhn2 = _mm(dup, wup_g, kind="nt", out_dtype=F32, name="ffn_up_dx", tm_cap=512, tn_cap=1024, tk_cap=up_pad,
               k_blocks=2, deps=(tok,))
    g_wup = _mm(dup, hn2, kind="tn", out_dtype=BF16, name="ffn_up_dw", tm_cap=1024, tn_cap=1024, tk_cap=t)
    rs_u, tok = to_sibling(g_wup.reshape(4, 2 * up_pad, d), up_pad, "up")
    dh1, dh1_b, g_ln2 = _rms_bwd(dhn2, h1, ln2_w, name="ln2_bwd", extra=dh2, want_bf16=True, deps=(tok,))
    dmix = _mm(dh1_b, wout_f, kind="nt", out_dtype=F32, name="out_proj_dx", tm_cap=512, tn_cap=1024, tk_cap=d)
    pl_u, tok = sibling_sum(rs_u, dmix)
    g_wout = _mm(mix, dh1_b, kind="tn", out_dtype=BF16, name="out_proj_dw", tn_cap=1024, tk_cap=t, deps=(tok,))
    rs_o, tok = to_sibling(g_wout.reshape(4, d // 4, d), d // NDEV, "out")
    dp_lru, g_lnw = _rms_bwd(dmix, p_lru, lru_norm_w, name="lru_norm_bwd", dy_cb=1, deps=(tok,))
    dxr, dyr, s_lru, g_wa, g_wx = _lru_bwd(proj, lcw, lru_conv_b, wa_b, lru_ba, wx_b, lru_bx, lru_lambda,
                                           h_lru, dp_lru, nb=nb, seq=seq, name="lru_bwd")
    pl_o, tok = sibling_sum(rs_o, dxr)
    dqr, dfr, dir_, dgr, s_hg = _hgrn_bwd(proj, lb_gamma, hg_norm_w, o_pre, states, dmix,
                                          nb=nb, seq=seq, name="hgrn_bwd", deps=(tok,))
    dproj = jnp.concatenate([dqr, dfr, dir_, dgr, dxr, dyr], axis=1)
    g_win = _mm(hn, dproj, kind="tn", out_dtype=BF16, name="in_proj_dw", out_blocks=NDEV, tm_cap=512, tn_cap=in_sh,
                tk_cap=t)
    rs_i, tok = to_sibling(g_win.reshape(4, 2 * d, in_sh), d, "in")
    pl_i, tok = sibling_sum(rs_i, tok)
    dhn = _mm(dproj, win_g, kind="nt", out_dtype=F32, name="in_proj_dx", tm_cap=512, tn_cap=1024, tk_cap=in_sh,
              k_blocks=4, deps=(tok,))
    grad_x, g_ln1 = _rms_bwd(dhn, xf, ln1_w, name="ln1_bwd", extra=dh1)

    fcw_parts = jnp.concatenate([s_g[0:3], s_v[0:3]], axis=1).reshape(3, NDEV, up_pad)[:, :, :up_sh]
    fcb_part = jnp.concatenate([s_g[3:4], s_v[3:4]], axis=1).reshape(NDEV, up_pad)[:, :up_sh]
    small_parts = [g_ln1, s_hg[0:2], s_hg[2:3], s_lru[0:4], s_lru[4:5], g_wa, s_lru[5:6], g_wx, s_lru[6:7],
                   s_lru[7:8], g_lnw, g_ln2, fcw_parts, fcb_part, g_fnw]
    reduced = _all_reduce_packed(_pack(small_parts), name="allreduce_small")
    summed = _unpack(reduced, [p.shape for p in small_parts])
    (s_ln1, s_lbg, s_hgn, s_lcw, s_lcb, s_wa, s_ba, s_wx, s_bx, s_lam, s_lnw, s_ln2, s_fcw, s_fcb, s_fnw) = summed
    sh_lcw = lru_conv_w.shape[2]
    g_small = {
        "ln1_w": s_ln1, "lb_gamma": s_lbg, "hg_norm_w": s_hgn,
        "lru_conv_w": lax.dynamic_slice_in_dim(s_lcw, me * sh_lcw, sh_lcw, axis=1),
        "lru_conv_b": s_lcb, "lru_wa": s_wa, "lru_ba": s_ba, "lru_wx": s_wx, "lru_bx": s_bx,
        "lru_lambda": s_lam, "lru_norm_w": s_lnw, "ln2_w": s_ln2,
        "ffn_conv_w": lax.dynamic_index_in_dim(s_fcw, me, 1, keepdims=False),
        "ffn_conv_b": s_fcb, "final_norm_w": s_fnw,
    }
    w_small = {"ln1_w": ln1_w, "lb_gamma": lb_gamma, "hg_norm_w": hg_norm_w, "lru_conv_w": lru_conv_w,
               "lru_conv_b": lru_conv_b, "lru_wa": lru_wa, "lru_ba": lru_ba, "lru_wx": lru_wx, "lru_bx": lru_bx,
               "lru_lambda": lru_lambda, "lru_norm_w": lru_norm_w, "ln2_w": ln2_w, "ffn_conv_w": ffn_conv_w,
               "ffn_conv_b": ffn_conv_b, "final_norm_w": final_norm_w}
    m_small = {"ln1_w": m_ln1_w, "lb_gamma": m_lb_gamma, "hg_norm_w": m_hg_norm_w, "lru_conv_w": m_lru_conv_w,
               "lru_conv_b": m_lru_conv_b, "lru_wa": m_lru_wa, "lru_ba": m_lru_ba, "lru_wx": m_lru_wx,
               "lru_bx": m_lru_bx, "lru_lambda": m_lru_lambda, "lru_norm_w": m_lru_norm_w, "ln2_w": m_ln2_w,
               "ffn_conv_w": m_ffn_conv_w, "ffn_conv_b": m_ffn_conv_b, "final_norm_w": m_final_norm_w}
    v_small = {"ln1_w": v_ln1_w, "lb_gamma": v_lb_gamma, "hg_norm_w": v_hg_norm_w, "lru_conv_w": v_lru_conv_w,
               "lru_conv_b": v_lru_conv_b, "lru_wa": v_lru_wa, "lru_ba": v_lru_ba, "lru_wx": v_lru_wx,
               "lru_bx": v_lru_bx, "lru_lambda": v_lru_lambda, "lru_norm_w": v_lru_norm_w, "ln2_w": v_ln2_w,
               "ffn_conv_w": v_ffn_conv_w, "ffn_conv_b": v_ffn_conv_b, "final_norm_w": v_final_norm_w}
    names = list(w_small)
    shapes = [w_small[k].shape for k in names]
    g_small = {k: g_small[k].reshape(w_small[k].shape) for k in names}
    packed = _adamw_packed(_pack([w_small[k] for k in names]), _pack([g_small[k] for k in names]),
                           _pack([m_small[k] for k in names]), _pack([v_small[k] for k in names]),
                           name="adamw_small")
    d_small, nm_small, nv_small = ({k: a for k, a in zip(names, _unpack(pk, shapes))} for pk in packed)

    def finish(pl_x, after, trim=lambda a: a):
        st, tag = pl_x
        (psum,), (rb,) = _xfer_wait(st[0], st[1], st[2], st[3], _plan_rs_plane(), after, name=f"rs_plane_wait_{tag}")
        return trim(lax.dynamic_index_in_dim(psum, plane, 0, keepdims=False)), trim(rb)

    upd_dn = _adamw_shard(ffn_w_down, m_ffn_w_down, v_ffn_w_down, *finish(pl_d, packed[0]), name="adamw_w_down")
    tr3 = lambda a: a.transpose(0, 2, 1)
    upd_up_t = _adamw_shard(tr3(ffn_w_up), tr3(m_ffn_w_up), tr3(v_ffn_w_up), *finish(pl_u, upd_dn[0]),
                            name="adamw_w_up")
    upd_up = [tr3(a) for a in upd_up_t]
    upd_out = _adamw_shard(w_out, m_w_out, v_w_out, *finish(pl_o, upd_up_t[0]), name="adamw_w_out")
    upd_in = _adamw_shard(w_in, m_w_in, v_w_in, *finish(pl_i, upd_out[0]), name="adamw_w_in")

    grads, deltas, new_m, new_v = dict(g_small), dict(d_small), dict(nm_small), dict(nv_small)
    for k, upd in (("w_in", upd_in), ("w_out", upd_out), ("ffn_w_up", upd_up), ("ffn_w_down", upd_dn)):
        grads[k], deltas[k], new_m[k], new_v[k] = upd
    order = ["ln1_w", "w_in", "lb_gamma", "hg_norm_w", "lru_conv_w", "lru_conv_b", "lru_wa", "lru_ba", "lru_wx",
             "lru_bx", "lru_lambda", "lru_norm_w", "w_out", "ln2_w", "ffn_w_up", "ffn_conv_w", "ffn_conv_b",
             "ffn_w_down", "final_norm_w"]
    return (loss, grad_x.reshape(nb, seq, d), *[grads[k] for k in order], *[deltas[k] for k in order],
            *[new_m[k] for k in order], *[new_v[k] for k in order])
```

```python
import math

import jax
import jax.numpy as jnp
from jax import lax
from jax.experimental import pallas as pl
from jax.experimental.pallas import tpu as pltpu

F32, BF16 = jnp.float32, jnp.bfloat16
EPS = 1e-6
HEAD = 128
CHUNK = 64
SUB = 16
NSUB = CHUNK // SUB
LRU_C = 8.0
LANE = 128
NDEV = 8
ADAM_LR, ADAM_B1, ADAM_B2, ADAM_EPS, ADAM_WD, ADAM_STEP = 0.001, 0.9, 0.999, 1e-08, 0.01, 10
MESH = pl.DeviceIdType.MESH
ANY = pl.BlockSpec(memory_space=pl.ANY)
VMEM_LIMIT = 56 * 1024 * 1024


def _cp(**kw):
    return pltpu.CompilerParams(vmem_limit_bytes=VMEM_LIMIT, **kw)


def _tile(n, cap, mult=LANE):
    best = None
    for t in range(mult, min(n, cap) + 1, mult):
        if n % t == 0:
            best = t
    return best if best is not None else n


def _row_tile(r, cdim, budget=262144):
    return _tile(r, max(16, budget // cdim), 16)


def _sigmoid(x):
    return jax.nn.sigmoid(x)


def _dsilu(x, s):
    return s * (1.0 + x * (1.0 - s))


def _iota_rows(n, w=LANE):
    return lax.broadcasted_iota(jnp.int32, (n, w), 0)


def _lane_groups(n, want=2):
    while n % want:
        want //= 2
    return want


def _group_views(refs, g, kinds):
    assert len(refs) == len(kinds)
    cols = pl.ds(g * HEAD, HEAD)
    return [r.at[:, cols] if kind == "l" else r.at[g] for r, kind in zip(refs, kinds)]


def _shift_down(prev8, xt, k):
    cat = jnp.concatenate([prev8, xt], axis=0)
    return pltpu.roll(cat, k, 0)[8:]


def _shift_up(xt, next8, k):
    cat = jnp.concatenate([xt, next8], axis=0)
    n = cat.shape[0]
    return pltpu.roll(cat, n - k, 0)[: xt.shape[0]]


def _scan_fwd(a, u):
    n = a.shape[0]
    row = _iota_rows(n, a.shape[1])
    k = 1
    while k < n:
        keep = row >= k
        a_s = jnp.where(keep, pltpu.roll(a, k, 0), 1.0)
        u_s = jnp.where(keep, pltpu.roll(u, k, 0), 0.0)
        u = a * u_s + u
        a = a * a_s
        k *= 2
    return a, u


def _scan_bwd(a, u):
    n = a.shape[0]
    row = _iota_rows(n, a.shape[1])
    k = 1
    while k < n:
        keep = row < n - k
        a_s = jnp.where(keep, pltpu.roll(a, n - k, 0), 1.0)
        u_s = jnp.where(keep, pltpu.roll(u, n - k, 0), 0.0)
        u = a * u_s + u
        a = a * a_s
        k *= 2
    return a, u


def _cumsum_fwd(u):
    n = u.shape[0]
    row = _iota_rows(n, u.shape[1])
    k = 1
    while k < n:
        u = u + jnp.where(row >= k, pltpu.roll(u, k, 0), 0.0)
        k *= 2
    return u


def _cumsum_bwd(u):
    n = u.shape[0]
    row = _iota_rows(n, u.shape[1])
    k = 1
    while k < n:
        u = u + jnp.where(row < n - k, pltpu.roll(u, n - k, 0), 0.0)
        k *= 2
    return u


def _dot(a, b, dims):
    return lax.dot_general(a.astype(BF16), b.astype(BF16), (dims, ((), ())), preferred_element_type=F32)


NN = ((1,), (0,))
NT = ((1,), (1,))
TN = ((0,), (0,))


def _pcall(body, *, n_in, in_specs, args, deps=(), **kw):
    nd = len(deps)
    if nd:
        inner = body

        def body(*refs):
            return inner(*refs[:n_in], *refs[n_in + nd:])

        in_specs = list(in_specs) + [ANY] * nd
        args = list(args) + list(deps)
    return pl.pallas_call(body, in_specs=in_specs, **kw)(*args)


def _mm(a, b3, *, kind, out_dtype, name, res=None, tm_cap=1024, tn_cap=1536, tk_cap=1024, out_blocks=1, k_blocks=1,
        col_part=None, into=None, deps=()):
    out_ix = lambda i, j, k: (i, j)
    if kind == "nn":
        m, kdim = a.shape
        nb, _, nsh = b3.shape
        n = nb * nsh
        tm, tn, tk = _tile(m, tm_cap), _tile(nsh, tn_cap), _tile(kdim, tk_cap)
        per = nsh // tn
        a_blk, a_ix = (tm, tk), lambda i, j, k: (i, k)
        b_blk, b_ix = (None, tk, tn), lambda i, j, k: (j // per, k, j % per)
        if col_part is not None:
            h, parts = col_part
            tn = nsh // parts
            b_blk, b_ix, out_ix = (None, tk, tn), lambda i, j, k: (j, k, h), lambda i, j, k: (i, j * parts + h)
        dims = NN
    elif kind == "nt":
        m, kdim = a.shape[-2], a.shape[-1] * (a.shape[0] if a.ndim == 3 else 1)
        nb, n, ksh = b3.shape
        tm, tn, tk = _tile(m, tm_cap), _tile(n, tn_cap), _tile(ksh, tk_cap)
        per = ksh // tk
        a_blk, a_ix = (tm, tk), lambda i, j, k: (i, k)
        b_blk, b_ix = (None, tn, tk), lambda i, j, k: (k // per, j, k % per)
        if k_blocks > 1:
            assert tk == ksh and nb % k_blocks == 0
            tk = k_blocks * ksh
            a_blk, b_blk, b_ix = (tm, tk), (k_blocks, tn, ksh), lambda i, j, k: (k, j, 0)
        if a.ndim == 3:
            aper = a.shape[2] // tk
            a_blk, a_ix = (None,) + a_blk, lambda i, j, k: (k // aper, i, k % aper)
        dims = NT
    else:
        kdim, m = a.shape[-2], a.shape[-1] * (a.shape[0] if a.ndim == 3 else 1)
        n = b3.shape[1]
        nsh = n // out_blocks
        tm, tn, tk = _tile(a.shape[-1], tm_cap), _tile(nsh, tn_cap), _tile(kdim, tk_cap)
        per = nsh // tn
        a_blk, a_ix = (tk, tm), lambda i, j, k: (k, i)
        b_blk, b_ix = (tk, tn), lambda i, j, k: (k, j)
        if a.ndim == 3:
            aper = a.shape[2] // tm
            a_blk, a_ix = (None,) + a_blk, lambda i, j, k: (i // aper, k, i % aper)
        dims = TN
    nk = kdim // tk
    j_outer = nk == 1
    nj = nb if col_part is not None else n // tn
    grid = (nj, m // tm, nk) if j_outer else (m // tm, nj, nk)
    at = (lambda f: lambda g0, g1, k: f(g1, g0, k)) if j_outer else (lambda f: f)
    a_spec, b_spec = pl.BlockSpec(a_blk, at(a_ix)), pl.BlockSpec(b_blk, at(b_ix))
    n_args = 2 + (res is not None) + (into is not None)

    def body(*refs):
        a_ref, b_ref = refs[:2]
        r_ref = refs[2] if res is not None else None
        o_ref = refs[n_args]
        if k_blocks > 1:
            w = b_ref.shape[2]
            part = sum(lax.dot_general(a_ref[:, s * w:(s + 1) * w], b_ref[s], (dims, ((), ())),
                                       preferred_element_type=F32) for s in range(k_blocks))
        else:
            part = lax.dot_general(a_ref[...], b_ref[...], (dims, ((), ())), preferred_element_type=F32)
        if nk == 1:
            o_ref[...] = (part if res is None else part + r_ref[...]).astype(o_ref.dtype)
            return
        acc = refs[-1]
        k = pl.program_id(2)

        @pl.when(k == 0)
        def _():
            acc[...] = part

        @pl.when(k > 0)
        def _():
            acc[...] += part

        @pl.when(k == nk - 1)
        def _():
            r = acc[...]
            if res is not None:
                r = r + r_ref[...]
            o_ref[...] = r.astype(o_ref.dtype)

    in_specs = [a_spec, b_spec]
    args = [a, b3]
    if res is not None:
        in_specs.append(pl.BlockSpec((tm, tn), at(out_ix)))
        args.append(res)
    aliases = {}
    if into is not None:
        aliases = {len(args): 0}
        in_specs.append(ANY)
        args.append(into)
    if kind == "tn":
        out_shape = jax.ShapeDtypeStruct((out_blocks, m, nsh), out_dtype)
        out_spec = pl.BlockSpec((None, tm, tn), at(lambda i, j, k: (j // per, i, j % per)))
    else:
        out_shape = jax.ShapeDtypeStruct((m, n), out_dtype)
        out_spec = pl.BlockSpec((tm, tn), at(out_ix))
    return _pcall(
        body, n_in=len(args), in_specs=in_specs, args=args, deps=deps, input_output_aliases=aliases,
        name=name, grid=grid, out_specs=out_spec, out_shape=out_shape,
        scratch_shapes=[pltpu.VMEM((tm, tn), F32)] if nk > 1 else [],
        compiler_params=_cp(dimension_semantics=("parallel", "parallel", "arbitrary")))


def _rms_fwd(x, w, *, name, tm=256, deps=()):
    t, d = x.shape
    tm = _tile(t, tm, 16)

    def body(x_ref, w_ref, o_ref):
        xv = x_ref[...]
        r = lax.rsqrt(jnp.mean(xv * xv, axis=-1, keepdims=True) + EPS)
        o_ref[...] = ((xv * r) * w_ref[...]).astype(o_ref.dtype)

    return _pcall(
        body, n_in=2, args=[x, w], deps=deps, name=name, grid=(t // tm,),
        in_specs=[pl.BlockSpec((tm, d), lambda i: (i, 0)), pl.BlockSpec((1, d), lambda i: (0, 0))],
        out_specs=pl.BlockSpec((tm, d), lambda i: (i, 0)),
        out_shape=jax.ShapeDtypeStruct((t, d), BF16), compiler_params=_cp())


def _rms_bwd(dy, x, w, *, name, extra=None, dy_cb=0, want_bf16=False, tm=256, deps=()):
    t, d = x.shape
    tm = _tile(t, tm, 16)

    def body(*refs):
        refs = list(refs)
        dy_ref, x_ref, w_ref = refs[:3]
        e_ref = refs[3] if extra is not None else None
        outs = refs[4:] if extra is not None else refs[3:]
        dx_ref = outs[0]
        dxb_ref = outs[1] if want_bf16 else None
        dw_ref = outs[-1]
        i = pl.program_id(0)
        xv = x_ref[...]
        r = lax.rsqrt(jnp.mean(xv * xv, axis=-1, keepdims=True) + EPS)
        nh = xv * r
        dyv = dy_ref[...]
        dn = dyv * w_ref[...]
        dx = r * (dn - nh * jnp.mean(dn * nh, axis=-1, keepdims=True))
        if extra is not None:
            dx = dx + e_ref[...]
        dx_ref[...] = dx
        if want_bf16:
            dxb_ref[...] = dx.astype(BF16)
        part = jnp.sum(dyv * nh, axis=0, keepdims=True)

        @pl.when(i == 0)
        def _():
            dw_ref[...] = part

        @pl.when(i > 0)
        def _():
            dw_ref[...] += part

    row = pl.BlockSpec((tm, d), lambda i: (i, 0))
    in_specs = [pl.BlockSpec((tm, d), lambda i: (i, dy_cb)), row, pl.BlockSpec((1, d), lambda i: (0, 0))]
    args = [dy, x, w]
    if extra is not None:
        in_specs.append(row)
        args.append(extra)
    out_shape = [jax.ShapeDtypeStruct((t, d), F32)]
    out_specs = [row]
    if want_bf16:
        out_shape.append(jax.ShapeDtypeStruct((t, d), BF16))
        out_specs.append(row)
    out_shape.append(jax.ShapeDtypeStruct((1, d), F32))
    out_specs.append(pl.BlockSpec((1, d), lambda i: (0, 0)))
    return _pcall(
        body, n_in=len(args), in_specs=in_specs, args=args, deps=deps,
        name=name, grid=(t // tm,), out_specs=out_specs, out_shape=out_shape,
        compiler_params=_cp(dimension_semantics=("arbitrary",)))


def _loss_head(h, w, tgt, *, name, tm=256):
    t, d = h.shape
    tm = _tile(t, tm, 16)

    def body(h_ref, w_ref, t_ref, dh_ref, dhb_ref, dw_ref, loss_ref):
        i = pl.program_id(0)
        xv = h_ref[...]
        wv = w_ref[...]
        r = lax.rsqrt(jnp.mean(xv * xv, axis=-1, keepdims=True) + EPS)
        nh = xv * r
        e = nh * wv - t_ref[...]
        part_loss = jnp.full((1, LANE), 0.5 * jnp.sum(jnp.mean(e * e, axis=-1, keepdims=True)), F32)
        dyv = e * (1.0 / d)
        dn = dyv * wv
        dx = r * (dn - nh * jnp.mean(dn * nh, axis=-1, keepdims=True))
        dh_ref[...] = dx
        dhb_ref[...] = dx.astype(BF16)
        part = jnp.sum(dyv * nh, axis=0, keepdims=True)

        @pl.when(i == 0)
        def _():
            dw_ref[...] = part
            loss_ref[...] = part_loss

        @pl.when(i > 0)
        def _():
            dw_ref[...] += part
            loss_ref[...] += part_loss

    row = pl.BlockSpec((tm, d), lambda i: (i, 0))
    vec = pl.BlockSpec((1, d), lambda i: (0, 0))
    return pl.pallas_call(
        body, name=name, grid=(t // tm,), in_specs=[row, vec, row],
        out_specs=[row, row, vec, pl.BlockSpec((1, LANE), lambda i: (0, 0))],
        out_shape=[jax.ShapeDtypeStruct((t, d), F32), jax.ShapeDtypeStruct((t, d), BF16),
                   jax.ShapeDtypeStruct((1, d), F32), jax.ShapeDtypeStruct((1, LANE), F32)],
        compiler_params=_cp(dimension_semantics=("arbitrary",)),
    )(h, w, tgt)


def _lower_bound(lbg_ref):
    g0, g1 = lbg_ref[0:1, :], lbg_ref[1:2, :]
    m = jnp.maximum(g0, g1)
    e0, e1 = jnp.exp(g0 - m), jnp.exp(g1 - m)
    return e0 / (e0 + e1)


def _seg_bounds():
    offs, o = {}, 0
    for i in range(1, NSUB):
        offs[i] = (o, o + SUB * i)
        o += SUB * i
    return offs, o


def _pad_rows(x, n):
    if x.shape[0] == n:
        return x
    return jnp.concatenate([x, jnp.zeros((n - x.shape[0], x.shape[1]), x.dtype)], axis=0)


def _offdiag_mask():
    offs, total = _seg_bounds()
    padded = -(-total // LANE) * LANE
    rsub = lax.broadcasted_iota(jnp.int32, (CHUNK, padded), 0) // SUB
    col = lax.broadcasted_iota(jnp.int32, (CHUNK, padded), 1)
    cseg = jnp.zeros((CHUNK, padded), jnp.int32)
    for i in range(1, NSUB):
        cseg = cseg + (col >= offs[i][0]).astype(jnp.int32)
    return (rsub == cseg) & (col < total)


def _offdiag_setup(q, k, b, v, b_c, mask):
    offs, total = _seg_bounds()
    padded = -(-total // LANE) * LANE
    eq_parts = [jnp.zeros((SUB, HEAD), F32)]
    ek_parts, k_parts, v_parts = [], [], []
    for i in range(1, NSUB):
        r_i = b_c[SUB * i - 1:SUB * i, :]
        eq_parts.append(jnp.exp(b[SUB * i:SUB * (i + 1)] - r_i))
        ek_parts.append(jnp.exp(r_i - b[0:SUB * i]))
        k_parts.append(k[0:SUB * i])
        v_parts.append(v[0:SUB * i])
    eq = jnp.concatenate(eq_parts, axis=0)
    ek = _pad_rows(jnp.concatenate(ek_parts, axis=0), padded)
    kt = _pad_rows(jnp.concatenate(k_parts, axis=0), padded) * ek
    vs = _pad_rows(jnp.concatenate(v_parts, axis=0), padded)
    qt = q * eq
    a = jnp.where(mask, _dot(qt, kt, NT), 0.0)
    return offs, eq, ek, kt, vs, qt, a


def _hgrn_fwd(proj, lbg, nw, *, nb, seq, name):
    t = proj.shape[0]
    w = lbg.shape[1]
    nh = w // HEAD
    nc = seq // CHUNK
    ng = _lane_groups(nh)

    def head(mask, q_ref, f_ref, i_ref, g_ref, lbg_ref, nw_ref, ohg_ref, opre_ref, st_ref, k_c, b_c, v_c, st):
        lb = _lower_bound(lbg_ref)
        nwv = nw_ref[...]
        st[...] = jnp.zeros_like(st)

        def chunk(c, carry):
            rows = pl.ds(pl.multiple_of(c * CHUNK, CHUNK), CHUNK)
            qr = q_ref[rows, :]
            q = qr * _sigmoid(qr)
            f = lb + (1.0 - lb) * _sigmoid(f_ref[rows, :])
            k = 1.0 - f
            b = _cumsum_fwd(jnp.log(f))
            v = i_ref[rows, :]
            k_c[...] = k
            b_c[...] = b
            v_c[...] = v
            s_t = st[...]
            st_ref[c] = s_t
            o = _dot(q * jnp.exp(b), s_t, NT)
            _, _, _, _, vs, _, a = _offdiag_setup(q, k, b, v, b_c, mask)
            o = o + _dot(a, vs, NN)
            diag = []
            for i in range(NSUB):
                accs = [jnp.zeros((8, HEAD), F32) for _ in range(SUB // 8)]
                for j in range(SUB):
                    r = SUB * i + j
                    bj, kj, vj = b_c[r:r + 1, :], k_c[r:r + 1, :], v_c[r:r + 1, :]
                    for p in range(j // 8, SUB // 8):
                        lo = SUB * i + 8 * p
                        d = jnp.exp(b[lo:lo + 8] - bj)
                        if 8 * p < j:
                            d = jnp.where(_iota_rows(8) + 8 * p >= j, d, 0.0)
                        s = jnp.sum(q[lo:lo + 8] * d * kj, axis=-1, keepdims=True)
                        accs[p] = accs[p] + s * vj
                diag.extend(accs)
            o = o + jnp.concatenate(diag, axis=0)
            bl = b_c[CHUNK - 1:CHUNK, :]
            kb = k * jnp.exp(bl - b)
            st[...] = s_t * jnp.exp(bl) + _dot(v, kb, TN)
            opre_ref[rows, :] = o
            rn = lax.rsqrt(jnp.mean(o * o, axis=-1, keepdims=True) + EPS)
            gr = g_ref[rows, :]
            ohg_ref[rows, :] = (((o * rn) * nwv) * (gr * _sigmoid(gr))).astype(BF16)
            return carry

        return chunk

    def body(*refs):
        mask = _offdiag_mask()
        chunks = [head(mask, *_group_views(refs, g, "llllllll" + "ggggg")) for g in range(ng)]

        def step(c, carry):
            for chunk in chunks:
                chunk(c, carry)
            return carry

        lax.fori_loop(0, nc, step, 0)

    gw = ng * HEAD

    def col(off):
        return pl.BlockSpec((seq, gw), lambda h, b: (b, off * (nh // ng) + h))

    vec = lambda r: pl.BlockSpec((r, gw), lambda h, b: (0, h))
    out_blk = pl.BlockSpec((seq, gw), lambda h, b: (b, h))
    return pl.pallas_call(
        body, name=name, grid=(nh // ng, nb),
        in_specs=[col(0), col(1), col(2), col(3), vec(2), vec(1)],
        out_specs=[out_blk, out_blk, pl.BlockSpec((None, ng, nc, HEAD, HEAD), lambda h, b: (b, h, 0, 0, 0))],
        out_shape=[jax.ShapeDtypeStruct((t, w), BF16), jax.ShapeDtypeStruct((t, w), F32),
                   jax.ShapeDtypeStruct((nb, nh, nc, HEAD, HEAD), F32)],
        scratch_shapes=[pltpu.VMEM((ng, CHUNK, HEAD), F32)] * 3 + [pltpu.VMEM((ng, HEAD, HEAD), F32)],
        compiler_params=_cp(dimension_semantics=("parallel", "parallel")),
    )(proj, proj, proj, proj, lbg, nw)


def _hgrn_bwd(proj, lbg, nw, opre, states, dmix, *, nb, seq, name, deps=()):
    t = proj.shape[0]
    w = lbg.shape[1]
    nh = w // HEAD
    nc = seq // CHUNK
    ng = _lane_groups(nh)

    def head(mask, q_ref, f_ref, i_ref, g_ref, lbg_ref, nw_ref, opre_ref, st_ref, dm_ref,
             dq_ref, df_ref, di_ref, dg_ref, small_ref, k_c, b_c, v_c, dst, dlb_s, dwn_s):
        bi = pl.program_id(1)
        lb = _lower_bound(lbg_ref)
        nwv = nw_ref[...]
        dst[...] = jnp.zeros_like(dst)

        @pl.when(bi == 0)
        def _():
            dlb_s[...] = jnp.zeros_like(dlb_s)
            dwn_s[...] = jnp.zeros_like(dwn_s)

        def chunk(it, carry):
            c = nc - 1 - it
            rows = pl.ds(pl.multiple_of(c * CHUNK, CHUNK), CHUNK)
            qr = q_ref[rows, :]
            sq = _sigmoid(qr)
            q = qr * sq
            sg = _sigmoid(f_ref[rows, :])
            f = lb + (1.0 - lb) * sg
            k = 1.0 - f
            b = _cumsum_fwd(jnp.log(f))
            v = i_ref[rows, :]
            k_c[...] = k
            b_c[...] = b
            v_c[...] = v
            o = opre_ref[rows, :]
            rn = lax.rsqrt(jnp.mean(o * o, axis=-1, keepdims=True) + EPS)
            nhat = o * rn
            dm = dm_ref[rows, :]
            gr = g_ref[rows, :]
            sgr = _sigmoid(gr)
            dnw = dm * (gr * sgr)
            dg_ref[rows, :] = (dm * (nhat * nwv) * _dsilu(gr, sgr)).astype(BF16)
            dwn_s[...] += jnp.sum(dnw * nhat, axis=0, keepdims=True)
            dn = dnw * nwv
            do = rn * (dn - nhat * jnp.mean(dn * nhat, axis=-1, keepdims=True))
            s_t = st_ref[c]
            ds = dst[...]
            eb = jnp.exp(b)
            qb = q * eb
            bl = b_c[CHUNK - 1:CHUNK, :]
            ebl = jnp.exp(bl)
            kdec = jnp.exp(bl - b)
            kb = k * kdec
            dqb = _dot(do, s_t, NN)
            dkb = _dot(v, ds, NN)
            dv = _dot(kb, ds, NT)
            d_ebl = jnp.sum(ds * s_t, axis=0, keepdims=True)
            dst[...] = ds * ebl + _dot(do, qb, TN)
            dq = dqb * eb
            dk = dkb * kdec
            t_kb = dkb * kb
            db = dqb * qb - t_kb
            db_last = jnp.sum(t_kb, axis=0, keepdims=True) + d_ebl * ebl
            offs, eq, ek, kt, vs, qt, a = _offdiag_setup(q, k, b, v, b_c, mask)
            da = jnp.where(mask, _dot(do, vs, NT), 0.0)
            dvs = _dot(a, do, TN)
            dqt = _dot(da, kt, NN)
            dkt = _dot(da, qt, TN)
            dq = dq + dqt * eq
            db = db + dqt * qt
            gk, gb = dkt * ek, dkt * kt
            zero8 = jnp.zeros((8, HEAD), F32)
            off_k, off_v, off_b = ([zero8] * (CHUNK // 8) for _ in range(3))
            for i in range(1, NSUB):
                lo, hi = offs[i]
                for p in range((hi - lo) // 8):
                    rows8 = slice(lo + 8 * p, lo + 8 * p + 8)
                    off_k[p] = off_k[p] + gk[rows8]
                    off_b[p] = off_b[p] - gb[rows8]
                    off_v[p] = off_v[p] + dvs[rows8]
            npv = SUB // 8
            dq_d, dk_d, dv_d, db_d = [], [], [], []
            for i in range(NSUB):
                aq = [zero8 for _ in range(npv)]
                ak = [zero8 for _ in range(npv)]
                av = [off_v[i * npv + p] for p in range(npv)]
                for j in range(SUB):
                    r = SUB * i + j
                    bj, kj, vj = b_c[r:r + 1, :], k_c[r:r + 1, :], v_c[r:r + 1, :]
                    pm_sum, ad_sum = None, None
                    for p in range(j // 8, npv):
                        lo = SUB * i + 8 * p
                        d = jnp.exp(b[lo:lo + 8] - bj)
                        if 8 * p < j:
                            d = jnp.where(_iota_rows(8) + 8 * p >= j, d, 0.0)
                        qd = q[lo:lo + 8] * d
                        dop = do[lo:lo + 8]
                        a_j = jnp.sum(qd * kj, axis=-1, keepdims=True)
                        da_j = jnp.sum(dop * vj, axis=-1, keepdims=True)
                        aq[p] = aq[p] + (da_j * d) * kj
                        pm, ad = da_j * qd, a_j * dop
                        pm_sum = pm if pm_sum is None else pm_sum + pm
                        ad_sum = ad if ad_sum is None else ad_sum + ad
                    pj = j // 8
                    here = _iota_rows(8) == (j - 8 * pj)
                    ak[pj] = ak[pj] + jnp.where(here, jnp.sum(pm_sum, axis=0, keepdims=True), 0.0)
                    av[pj] = av[pj] + jnp.where(here, jnp.sum(ad_sum, axis=0, keepdims=True), 0.0)
                for p in range(npv):
                    lo = SUB * i + 8 * p
                    db_d.append(off_b[i * npv + p] + q[lo:lo + 8] * aq[p] - k[lo:lo + 8] * ak[p])
                    dk_d.append(off_k[i * npv + p] + ak[p])
                dq_d.extend(aq)
                dv_d.extend(av)
            dq = dq + jnp.concatenate(dq_d, axis=0)
            dk = dk + jnp.concatenate(dk_d, axis=0)
            dv = dv + jnp.concatenate(dv_d, axis=0)
            db = db + jnp.concatenate(db_d, axis=0)
            db = db + jnp.where(_iota_rows(CHUNK) == CHUNK - 1, db_last, 0.0)
            dgl = _cumsum_bwd(db)
            dfv = dgl / f - dk
            dlb_s[...] += jnp.sum(dfv * (1.0 - sg), axis=0, keepdims=True)
            df_ref[rows, :] = (dfv * (1.0 - lb) * (sg * (1.0 - sg))).astype(BF16)
            dq_ref[rows, :] = (dq * _dsilu(qr, sq)).astype(BF16)
            di_ref[rows, :] = dv.astype(BF16)
            return carry

        def finish():
            @pl.when(bi == nb - 1)
            def _():
                dgam = dlb_s[...] * lb * (1.0 - lb)
                small_ref[...] = jnp.zeros_like(small_ref)
                small_ref[0:1, :] = dgam
                small_ref[1:2, :] = -dgam
                small_ref[2:3, :] = dwn_s[...]

        return chunk, finish

    def body(*refs):
        mask = _offdiag_mask()
        heads = [head(mask, *_group_views(refs, g, "lllllllgl" + "lllll" + "ggggll")) for g in range(ng)]

        def step(it, carry):
            for chunk, _ in heads:
                chunk(it, carry)
            return carry

        lax.fori_loop(0, nc, step, 0)
        for _, finish in heads:
            finish()

    gw = ng * HEAD

    def col(off):
        return pl.BlockSpec((seq, gw), lambda h, b: (b, off * (nh // ng) + h))

    vec = lambda r: pl.BlockSpec((r, gw), lambda h, b: (0, h))
    blk = pl.BlockSpec((seq, gw), lambda h, b: (b, h))
    dshape = jax.ShapeDtypeStruct((t, w), BF16)
    return _pcall(
        body, n_in=9, args=[proj, proj, proj, proj, lbg, nw, opre, states, dmix], deps=deps,
        name=name, grid=(nh // ng, nb),
        in_specs=[col(0), col(1), col(2), col(3), vec(2), vec(1), blk,
                  pl.BlockSpec((None, ng, nc, HEAD, HEAD), lambda h, b: (b, h, 0, 0, 0)), blk],
        out_specs=[blk, blk, blk, blk, vec(8)],
        out_shape=[dshape, dshape, dshape, dshape, jax.ShapeDtypeStruct((8, w), F32)],
        scratch_shapes=[pltpu.VMEM((ng, CHUNK, HEAD), F32)] * 3 + [pltpu.VMEM((ng, HEAD, HEAD), F32)]
        + [pltpu.VMEM((1, gw), F32)] * 2,
        compiler_params=_cp(dimension_semantics=("parallel", "arbitrary")))


def _expm1(x):
    poly = x * (1.0 + x * (0.5 + x * (1.0 / 6 + x * (1.0 / 24 + x * (1.0 / 120 + x * (1.0 / 720))))))
    return jnp.where(jnp.abs(x) < 0.25, poly, jnp.exp(x) - 1.0)


def _softplus_neg(lam):
    x = -lam
    e = jnp.exp(-jnp.abs(x))
    u = 1.0 + e
    l1p = jnp.where(u == 1.0, e, jnp.log(u) * (e / jnp.where(u == 1.0, 1.0, u - 1.0)))
    return jnp.maximum(x, 0.0) + l1p


_GELU_C = math.sqrt(2.0 / math.pi)


def _gelu(y):
    return 0.5 * y * (1.0 + jnp.tanh(_GELU_C * (y + 0.044715 * (y * y * y))))


def _dgelu(y):
    th = jnp.tanh(_GELU_C * (y + 0.044715 * (y * y * y)))
    return 0.5 * (1.0 + th) + 0.5 * y * (1.0 - th * th) * (_GELU_C * (1.0 + 3 * 0.044715 * (y * y)))


def _lru_gates(xr, prev8, cw_ref, cb, wa_ref, ba, wx_ref, bx, sp, first):
    x3, x2, x1 = _shift_down(prev8, xr, 3), _shift_down(prev8, xr, 2), _shift_down(prev8, xr, 1)
    xb = cb + x3 * cw_ref[0:1, :]
    xb = xb + x2 * cw_ref[1:2, :]
    xb = xb + x1 * cw_ref[2:3, :]
    xb = xb + xr * cw_ref[3:4, :]
    r = _sigmoid(_dot(xb, wa_ref[...], NN) + ba)
    ig = _sigmoid(_dot(xb, wx_ref[...], NN) + bx)
    la = (-LRU_C * r) * sp
    a = jnp.exp(la)
    start = jnp.logical_and(first, _iota_rows(xr.shape[0]) == 0)
    mult = jnp.where(start, 1.0, jnp.sqrt(-_expm1(2.0 * la)))
    return (x3, x2, x1), xb, r, ig, a, mult, start


def _lru_fwd(proj, cw, cb, wa, ba, wx, bx, lam, *, nb, seq, name, deps=()):
    t = proj.shape[0]
    w = cb.shape[1]
    nblk = w // HEAD
    nc = seq // CHUNK
    ng = _lane_groups(nblk, 4)

    def block(x_ref, y_ref, cw_ref, cb_ref, wa_ref, ba_ref, wx_ref, bx_ref, lam_ref, p_ref, h_ref):
        sp = _softplus_neg(lam_ref[...])
        cb_v, ba_v, bx_v = cb_ref[...], ba_ref[...], bx_ref[...]

        def tile(c, carry):
            hc, prev8 = carry
            base = pl.multiple_of(c * CHUNK, CHUNK)
            rows = pl.ds(base, CHUNK)
            xr = x_ref[rows, :]
            _, xb, _, ig, a, mult, _ = _lru_gates(xr, prev8, cw_ref, cb_v, wa_ref, ba_v, wx_ref, bx_v, sp, c == 0)
            ap, up = _scan_fwd(a, xb * ig * mult)
            h = up + ap * hc
            h_ref[rows, :] = h
            p_ref[rows, :] = h * _gelu(y_ref[rows, :])
            h_last = jnp.sum(jnp.where(_iota_rows(8) == 7, h[CHUNK - 8:], 0.0), axis=0, keepdims=True)
            return h_last, xr[CHUNK - 8:]

        return tile

    def body(*refs):
        tiles = [block(*_group_views(refs, g, "llllglgll" + "ll")) for g in range(ng)]

        def step(c, carries):
            return tuple(tile(c, carry) for tile, carry in zip(tiles, carries))

        lax.fori_loop(0, nc, step, ((jnp.zeros((1, HEAD), F32), jnp.zeros((8, HEAD), F32)),) * ng)

    gw = ng * HEAD

    def col(off):
        return pl.BlockSpec((seq, gw), lambda n, b: (b, off * (nblk // ng) + n))

    vec = lambda r: pl.BlockSpec((r, gw), lambda n, b: (0, n))
    mat = pl.BlockSpec((ng, HEAD, HEAD), lambda n, b: (n, 0, 0))
    blk = pl.BlockSpec((seq, gw), lambda n, b: (b, n))
    return _pcall(
        body, n_in=9, args=[proj, proj, cw, cb, wa, ba, wx, bx, lam], deps=deps, name=name, grid=(nblk // ng, nb),
        in_specs=[col(4), col(5), vec(4), vec(1), mat, vec(1), mat, vec(1), vec(1)],
        out_specs=[blk, blk],
        out_shape=[jax.ShapeDtypeStruct((t, w), F32), jax.ShapeDtypeStruct((t, w), F32)],
        compiler_params=_cp(dimension_semantics=("parallel", "parallel")))


def _lru_bwd(proj, cw, cb, wa, ba, wx, bx, lam, hsv, dp, *, nb, seq, name):
    t = proj.shape[0]
    w = cb.shape[1]
    nblk = w // HEAD
    nc = seq // CHUNK
    ng = _lane_groups(nblk, 4)

    def block(x_ref, y_ref, cw_ref, cb_ref, wa_ref, ba_ref, wx_ref, bx_ref, lam_ref, h_ref, dp_ref,
              dx_ref, dy_ref, small_ref, dwa_ref, dwx_ref, a_c, g_c, acc, dwa_s, dwx_s):
        bi = pl.program_id(1)
        lamv = lam_ref[...]
        sp = _softplus_neg(lamv)
        cb_v, ba_v, bx_v = cb_ref[...], ba_ref[...], bx_ref[...]

        @pl.when(bi == 0)
        def _():
            acc[...] = jnp.zeros_like(acc)
            dwa_s[...] = jnp.zeros_like(dwa_s)
            dwx_s[...] = jnp.zeros_like(dwx_s)

        def tile(it, carry):
            g_next, a_next, dxb_next8 = carry
            c = nc - 1 - it
            base = pl.multiple_of(c * CHUNK, CHUNK)
            rows = pl.ds(base, CHUNK)
            before = pl.ds(pl.multiple_of(jnp.maximum(base - 8, 0), 8), 8)
            inner = c > 0
            xr = x_ref[rows, :]
            prev8 = jnp.where(inner, x_ref[before, :], 0.0)
            (x3, x2, x1), xb, r, ig, a, mult, start = _lru_gates(
                xr, prev8, cw_ref, cb_v, wa_ref, ba_v, wx_ref, bx_v, sp, c == 0)
            h = h_ref[rows, :]
            h_m1 = _shift_down(jnp.where(inner, h_ref[before, :], 0.0), h, 1)
            yv = y_ref[rows, :]
            dpv = dp_ref[rows, :]
            dy_ref[rows, :] = (dpv * h * _dgelu(yv)).astype(BF16)
            dh = dpv * _gelu(yv)
            a_up = _shift_up(a, jnp.broadcast_to(a_next, (8, HEAD)), 1)
            ap, gp = _scan_bwd(a_up, dh)
            g = gp + ap * g_next
            a_c[...] = a
            g_c[...] = g
            da = g * h_m1
            gx = g * xb
            dxb = g * ig * mult
            dig = gx * mult
            dmult = jnp.where(start, 0.0, gx * ig)
            dla = da * a - dmult * (a * a) / mult
            dzr = (dla * (-LRU_C * sp)) * (r * (1.0 - r))
            dzi = dig * (ig * (1.0 - ig))
            dxb = dxb + _dot(dzr, wa_ref[...], NT) + _dot(dzi, wx_ref[...], NT)
            dwa_s[...] += _dot(xb, dzr, TN)
            dwx_s[...] += _dot(xb, dzi, TN)
            d1, d2, d3 = (_shift_up(dxb, dxb_next8, s) for s in (1, 2, 3))
            dx = dxb * cw_ref[3:4, :] + d1 * cw_ref[2:3, :] + d2 * cw_ref[1:2, :] + d3 * cw_ref[0:1, :]
            dx_ref[rows, :] = dx.astype(BF16)
            colsum = lambda z: jnp.sum(z, axis=0, keepdims=True)
            acc[0:1, :] += colsum(x3 * dxb)
            acc[1:2, :] += colsum(x2 * dxb)
            acc[2:3, :] += colsum(x1 * dxb)
            acc[3:4, :] += colsum(xr * dxb)
            acc[4:5, :] += colsum(dxb)
            acc[5:6, :] += colsum(dzr)
            acc[6:7, :] += colsum(dzi)
            acc[7:8, :] += colsum(dla * (-LRU_C * r))
            return g_c[0:1, :], a_c[0:1, :], dxb[0:8]

        def finish():
            @pl.when(bi == nb - 1)
            def _():
                small_ref[...] = acc[...]
                small_ref[7:8, :] = acc[7:8, :] * (-_sigmoid(-lamv))
                dwa_ref[...] = dwa_s[...]
                dwx_ref[...] = dwx_s[...]

        return tile, finish

    def body(*refs):
        blocks = [block(*_group_views(refs, g, "llllglgllll" + "lllgg" + "gglgg")) for g in range(ng)]

        def step(it, carries):
            return tuple(tile(it, carry) for (tile, _), carry in zip(blocks, carries))

        zero = jnp.zeros((1, HEAD), F32)
        lax.fori_loop(0, nc, step, ((zero, zero, jnp.zeros((8, HEAD), F32)),) * ng)
        for _, finish in blocks:
            finish()

    gw = ng * HEAD

    def col(off):
        return pl.BlockSpec((seq, gw), lambda n, b: (b, off * (nblk // ng) + n))

    vec = lambda r: pl.BlockSpec((r, gw), lambda n, b: (0, n))
    mat = pl.BlockSpec((ng, HEAD, HEAD), lambda n, b: (n, 0, 0))
    blk = pl.BlockSpec((seq, gw), lambda n, b: (b, n))
    dshape = jax.ShapeDtypeStruct((t, w), BF16)
    return pl.pallas_call(
        body, name=name, grid=(nblk // ng, nb),
        in_specs=[col(4), col(5), vec(4), vec(1), mat, vec(1), mat, vec(1), vec(1), blk, blk],
        out_specs=[blk, blk, vec(8), mat, mat],
        out_shape=[dshape, dshape, jax.ShapeDtypeStruct((8, w), F32),
                   jax.ShapeDtypeStruct((nblk, HEAD, HEAD), F32), jax.ShapeDtypeStruct((nblk, HEAD, HEAD), F32)],
        scratch_shapes=[pltpu.VMEM((ng, CHUNK, HEAD), F32)] * 2 + [pltpu.VMEM((8, gw), F32)]
        + [pltpu.VMEM((ng, HEAD, HEAD), F32)] * 2,
        compiler_params=_cp(dimension_semantics=("parallel", "arbitrary")),
    )(proj, proj, cw, cb, wa, ba, wx, bx, lam, hsv, dp)


def _ffn_conv(x, prev8, cw_ref, cbv):
    x2, x1 = _shift_down(prev8, x, 2), _shift_down(prev8, x, 1)
    y = cbv + x2 * cw_ref[0:1, :]
    y = y + x1 * cw_ref[1:2, :]
    y = y + x * cw_ref[2:3, :]
    return (x2, x1), y


def _ffn_conv_at(x_ref, base, rt, zero8, cw_ref, cbv):
    x = x_ref[base:base + rt, :]
    if base == 0:
        return _ffn_conv(x, zero8, cw_ref, cbv)[1]
    y = cbv + x_ref[base - 2:base - 2 + rt, :] * cw_ref[0:1, :]
    y = y + x_ref[base - 1:base - 1 + rt, :] * cw_ref[1:2, :]
    return y + x * cw_ref[2:3, :]


def _ffn_act_fwd(up, cw, cb, *, nb, seq, name, deps=()):
    t, f2 = up.shape
    f = f2 // 2
    tc = _tile(f, 512)
    nj = f // tc
    rt = _tile(seq, 65536 // tc, 16)
    nr = seq // rt

    def body(g_ref, v_ref, cwg_ref, cwv_ref, cbg_ref, cbv_ref, o_ref):
        cbg, cbv = cbg_ref[...], cbv_ref[...]
        z = jnp.zeros((8, tc), F32)
        for c in range(nr):
            gate = _ffn_conv_at(g_ref, c * rt, rt, z, cwg_ref, cbg)
            val = _ffn_conv_at(v_ref, c * rt, rt, z, cwv_ref, cbv)
            o_ref[c * rt:(c + 1) * rt, :] = ((gate * _sigmoid(gate)) * val).astype(BF16)

    gcol = pl.BlockSpec((seq, tc), lambda j, b: (b, j))
    vcol = pl.BlockSpec((seq, tc), lambda j, b: (b, nj + j))
    gv = lambda r: pl.BlockSpec((r, tc), lambda j, b: (0, j))
    vv = lambda r: pl.BlockSpec((r, tc), lambda j, b: (0, nj + j))
    return _pcall(
        body, n_in=6, args=[up, up, cw, cw, cb, cb], deps=deps, name=name, grid=(nj, nb),
        in_specs=[gcol, vcol, gv(3), vv(3), gv(1), vv(1)], out_specs=gcol,
        out_shape=jax.ShapeDtypeStruct((t, f), BF16),
        compiler_params=_cp(dimension_semantics=("parallel", "parallel")))


def _ffn_act_bwd(dact, up, cw, cb, *, nb, seq, name, deps=()):
    t, f2 = up.shape
    f = f2 // 2
    tc = _tile(f, 512)
    nj = f // tc
    rt = _tile(seq, 65536 // tc, 16)
    nr = seq // rt

    def body(da_ref, g_ref, v_ref, cwg_ref, cwv_ref, cbg_ref, cbv_ref,
             d_ref, sg_ref, sv_ref, eg, ev, accg, accv):
        bi = pl.program_id(1)
        cbg, cbv = cbg_ref[...], cbv_ref[...]

        @pl.when(bi == 0)
        def _():
            accg[...] = jnp.zeros_like(accg)
            accv[...] = jnp.zeros_like(accv)

        colsum = lambda z: jnp.sum(z, axis=0, keepdims=True)

        z = jnp.zeros((8, tc), F32)
        eg[seq:seq + 8, :] = z
        ev[seq:seq + 8, :] = z

        def first(c, carry):
            pg, pv = carry
            rows = pl.ds(pl.multiple_of(c * rt, rt), rt)
            xg, xv = g_ref[rows, :], v_ref[rows, :]
            (g2, g1), gate = _ffn_conv(xg, pg, cwg_ref, cbg)
            (v2, v1), val = _ffn_conv(xv, pv, cwv_ref, cbv)
            s = _sigmoid(gate)
            da = da_ref[rows, :]
            dgate = da * val * _dsilu(gate, s)
            dval = da * (gate * s)
            eg[rows, :] = dgate
            ev[rows, :] = dval
            for acc, (s2, s1, s0), d in ((accg, (g2, g1, xg), dgate), (accv, (v2, v1, xv), dval)):
                acc[0:1, :] += colsum(s2 * d)
                acc[1:2, :] += colsum(s1 * d)
                acc[2:3, :] += colsum(s0 * d)
                acc[3:4, :] += colsum(d)
            return xg[rt - 8:], xv[rt - 8:]

        lax.fori_loop(0, nr, first, (z, z))

        for c in range(nr):
            base = c * rt
            for half, (e, cw_ref) in enumerate(((eg, cwg_ref), (ev, cwv_ref))):
                d, d1, d2 = e[base:base + rt, :], e[base + 1:base + 1 + rt, :], e[base + 2:base + 2 + rt, :]
                d_ref[half, base:base + rt, :] = (
                    d * cw_ref[2:3, :] + d1 * cw_ref[1:2, :] + d2 * cw_ref[0:1, :]).astype(BF16)

        @pl.when(bi == nb - 1)
        def _():
            sg_ref[...] = accg[...]
            sv_ref[...] = accv[...]

    gcol = pl.BlockSpec((seq, tc), lambda j, b: (b, j))
    vcol = pl.BlockSpec((seq, tc), lambda j, b: (b, nj + j))
    gv = lambda r: pl.BlockSpec((r, tc), lambda j, b: (0, j))
    vv = lambda r: pl.BlockSpec((r, tc), lambda j, b: (0, nj + j))
    dshape = jax.ShapeDtypeStruct((2, t, f), BF16)
    sshape = jax.ShapeDtypeStruct((8, f), F32)
    return _pcall(
        body, n_in=7, args=[dact, up, up, cw, cw, cb, cb], deps=deps, name=name, grid=(nj, nb),
        in_specs=[gcol, gcol, vcol, gv(3), vv(3), gv(1), vv(1)],
        out_specs=[pl.BlockSpec((2, seq, tc), lambda j, b: (0, b, j)), gv(8), gv(8)],
        out_shape=[dshape, sshape, sshape],
        scratch_shapes=[pltpu.VMEM((seq + 8, tc), F32)] * 2 + [pltpu.VMEM((8, tc), F32)] * 2,
        compiler_params=_cp(dimension_semantics=("parallel", "arbitrary")))


def _adamw_math(wv, g, mv, vv):
    m = ADAM_B1 * mv + (1.0 - ADAM_B1) * g
    v = ADAM_B2 * vv + (1.0 - ADAM_B2) * (g * g)
    m_hat = m / (1.0 - ADAM_B1 ** ADAM_STEP)
    v_hat = v / (1.0 - ADAM_B2 ** ADAM_STEP)
    delta = -ADAM_LR * (m_hat / (jnp.sqrt(v_hat) + ADAM_EPS) + ADAM_WD * wv)
    return delta, m, v


def _adamw_shard(wv, mv, vv, p_own, rb, *, name):
    _, r, cdim = wv.shape
    tr = _row_tile(r, cdim)

    def body(w_ref, m_ref, v_ref, p_ref, rb_ref, g_ref, d_ref, mo_ref, vo_ref):
        g = p_ref[...].astype(F32)
        for k in range(3):
            g = g + rb_ref[k].astype(F32)
        g_ref[...] = g
        d_ref[...], mo_ref[...], vo_ref[...] = _adamw_math(w_ref[...], g, m_ref[...], v_ref[...])

    row = pl.BlockSpec((tr, cdim), lambda i: (i, 0))
    row3 = pl.BlockSpec((None, tr, cdim), lambda i: (0, i, 0))
    shp = jax.ShapeDtypeStruct((1, r, cdim), F32)
    return pl.pallas_call(
        body, name=name, grid=(r // tr,),
        in_specs=[row3, row3, row3, row, pl.BlockSpec((3, tr, cdim), lambda i: (0, i, 0))],
        out_specs=[row3] * 4, out_shape=[shp] * 4, compiler_params=_cp(dimension_semantics=("parallel",)),
    )(wv, mv, vv, p_own, rb)


def _adamw_packed(wv, g, mv, vv, *, name):
    r = wv.shape[0]
    tr = _tile(r, 256, 8)

    def body(w_ref, g_ref, m_ref, v_ref, d_ref, mo_ref, vo_ref):
        d_ref[...], mo_ref[...], vo_ref[...] = _adamw_math(w_ref[...], g_ref[...], m_ref[...], v_ref[...])

    row = pl.BlockSpec((tr, LANE), lambda i: (i, 0))
    shp = jax.ShapeDtypeStruct((r, LANE), F32)
    return pl.pallas_call(
        body, name=name, grid=(r // tr,), in_specs=[row] * 4, out_specs=[row] * 3, out_shape=[shp] * 3,
        compiler_params=_cp(dimension_semantics=("parallel",)),
    )(wv, g, mv, vv)


def _place():
    return lax.axis_index("x"), lax.axis_index("y"), lax.axis_index("c")


def _all_reduce_packed(p, *, name):
    r = p.shape[0]
    tr = _tile(r, 256, 8)

    def body(p_ref, o_ref, buf, send, recv, lsem):
        x, y, c = _place()
        me, sibling = (x, y, c), (x, y, 1 - c)
        chips = [(1 - x, y), (x, 1 - y), (1 - x, 1 - y)]

        def slot(px, py, pc):
            return 4 * px + 2 * py + pc

        def copy(k, block, to, src=None):
            dst = buf.at[slot(*block)]
            return pltpu.make_async_remote_copy(
                src_ref=dst if src is None else src, dst_ref=dst, send_sem=send.at[k], recv_sem=recv.at[k],
                device_id=to, device_id_type=MESH)

        mine = pltpu.make_async_copy(p_ref, buf.at[slot(*me)], lsem)
        mine.start()
        first = [copy(0, me, sibling, src=p_ref)]
        first += [copy(1 + j, me, (*chip, c), src=p_ref) for j, chip in enumerate(chips)]
        for cp in first:
            cp.start()
        passed = [copy(4 + j, (*chip, c), sibling) for j, chip in enumerate(chips)]
        for j, chip in enumerate(chips):
            copy(1 + j, (*chip, c), me).wait_recv()
            passed[j].start()
        copy(0, sibling, me).wait_recv()
        for j, chip in enumerate(chips):
            copy(4 + j, (*chip, 1 - c), me).wait_recv()
        for cp in first + passed:
            cp.wait_send()
        mine.wait()

        def add(i, carry):
            rows = pl.ds(pl.multiple_of(i * tr, tr), tr)
            s = buf[0, rows, :]
            for d in range(1, NDEV):
                s = s + buf[d, rows, :]
            o_ref[rows, :] = s
            return carry

        lax.fori_loop(0, r // tr, add, 0)

    vm = pl.BlockSpec(memory_space=pltpu.VMEM)
    return pl.pallas_call(
        body, name=name, in_specs=[vm], out_specs=vm, out_shape=jax.ShapeDtypeStruct(p.shape, p.dtype),
        scratch_shapes=[pltpu.VMEM((NDEV,) + p.shape, p.dtype), pltpu.SemaphoreType.DMA((7,)),
                        pltpu.SemaphoreType.DMA((7,)), pltpu.SemaphoreType.DMA],
        compiler_params=_cp(),
    )(p)


HBM = pl.BlockSpec(memory_space=pltpu.HBM)
SEM = pl.BlockSpec(memory_space=pltpu.SEMAPHORE)
EFFECT = pltpu.SideEffectType.DATAFLOW_SIDE_EFFECTING


def _plan_copies(plan, s_refs, l_refs, send, recv):
    def pick(kind, a, idx):
        ref = (s_refs if kind == "s" else l_refs)[a]
        return ref if idx is None else ref.at[idx]

    return [pltpu.make_async_remote_copy(
        src_ref=pick(*src), dst_ref=pick(*dst), send_sem=send.at[i], recv_sem=recv.at[i],
        device_id=to, device_id_type=MESH) for i, (src, dst, to) in enumerate(plan(*_place()))]


def _xfer_start(srcs, lands, plan, *, name, deps=()):
    ns, nl = len(srcs), len(lands)
    nd = len(deps)
    ncopy = len(plan(0, 0, 0))

    def body(*refs):
        s_refs, l_refs = refs[:ns], refs[ns:ns + nl]
        send, recv = refs[ns + nl + nd], refs[ns + nl + nd + 1]
        token = refs[-1]
        for cp in _plan_copies(plan, s_refs, l_refs, send, recv):
            cp.start()
        token[...] = jnp.zeros_like(token)

    bufs = list(srcs) + list(lands)
    outs = pl.pallas_call(
        body, name=name,
        out_shape=(pltpu.SemaphoreType.DMA((ncopy,)), pltpu.SemaphoreType.DMA((ncopy,)),
                   *[pltpu.HBM(b.shape, b.dtype) for b in bufs], jax.ShapeDtypeStruct((8, LANE), F32)),
        in_specs=[HBM] * (ns + nl) + [ANY] * nd,
        out_specs=(SEM, SEM, *[HBM] * (ns + nl), pl.BlockSpec(memory_space=pltpu.VMEM)),
        input_output_aliases={i: 2 + i for i in range(ns + nl)},
        compiler_params=pltpu.CompilerParams(has_side_effects=EFFECT),
    )(*[pltpu.with_memory_space_constraint(b, pltpu.HBM) for b in bufs], *deps)
    return outs[0], outs[1], list(outs[2:2 + ns]), list(outs[2 + ns:2 + ns + nl]), outs[-1]


def _xfer_wait(send, recv, srcs, lands, plan, after, *, name):
    ns, nl = len(srcs), len(lands)

    def body(*refs):
        s_refs, l_refs = refs[:ns], refs[ns:ns + nl]
        send_ref, recv_ref = refs[ns + nl], refs[ns + nl + 1]
        for cp in _plan_copies(plan, s_refs, l_refs, send_ref, recv_ref):
            cp.wait_send()
            cp.wait_recv()

    bufs = list(srcs) + list(lands)
    outs = pl.pallas_call(
        body, name=name, out_shape=tuple(pltpu.HBM(b.shape, b.dtype) for b in bufs),
        in_specs=[HBM] * (ns + nl) + [SEM, SEM, ANY], out_specs=tuple([HBM] * (ns + nl)),
        input_output_aliases={i: i for i in range(ns + nl)},
        compiler_params=pltpu.CompilerParams(has_side_effects=EFFECT),
    )(*bufs, send, recv, after)
    return list(outs[:ns]), list(outs[ns:])


def _slot8(px, py, pc):
    return 4 * px + 2 * py + pc


def _slot4(r):
    return lambda px, py, pc: (2 * px + py, pl.ds(pc * r, r))


def _plan_gather_first(slots, windows=None):
    windows = windows or [None] * len(slots)

    def plan(x, y, c):
        peers = [(x, y, c), (x, y, 1 - c), (1 - x, y, c), (x, 1 - y, c), (1 - x, 1 - y, c)]
        return [(("s", a, windows[a]), ("l", a, slot(x, y, c)), to) for a, slot in enumerate(slots) for to in peers]
    return plan


def _plan_gather_pass(slots):
    def plan(x, y, c):
        chips = [(1 - x, y), (x, 1 - y), (1 - x, 1 - y)]
        return [(("l", a, slot(px, py, c)), ("l", a, slot(px, py, c)), (x, y, 1 - c))
                for a, slot in enumerate(slots) for px, py in chips]
    return plan


def _plan_rs_sibling(r):
    def plan(x, y, c):
        return [(("s", 0, (j, pl.ds((1 - c) * r, r))), ("l", 0, j), (x, y, 1 - c)) for j in range(4)]
    return plan


def _plan_rs_plane():
    def plan(x, y, c):
        chips = [(1 - x, y), (x, 1 - y), (1 - x, 1 - y)]
        return [(("s", 0, 2 * px + py), ("l", 0, k), (px, py, c)) for k, (px, py) in enumerate(chips)]
    return plan


def _add_pairs(g, rcv, cidx, *, name):
    _, r, cdim = rcv.shape
    tr = _row_tile(r, cdim, budget=4194304)
    per = r // tr

    def body(c_ref, g_ref, r_ref, o_ref):
        o_ref[...] = (g_ref[...].astype(F32) + r_ref[...].astype(F32)).astype(o_ref.dtype)

    grid_spec = pltpu.PrefetchScalarGridSpec(
        num_scalar_prefetch=1, grid=(4, per),
        in_specs=[pl.BlockSpec((None, tr, cdim), lambda j, i, c_ref: (j, c_ref[0] * per + i, 0)),
                  pl.BlockSpec((None, tr, cdim), lambda j, i, c_ref: (j, i, 0))],
        out_specs=pl.BlockSpec((None, tr, cdim), lambda j, i, c_ref: (j, i, 0)))
    return pl.pallas_call(
        body, name=name, grid_spec=grid_spec, out_shape=jax.ShapeDtypeStruct((4, r, cdim), g.dtype),
        compiler_params=_cp(dimension_semantics=("parallel", "parallel")),
    )(cidx, g, rcv)


def _pack(arrs):
    flat = jnp.concatenate([a.reshape(-1).astype(F32) for a in arrs])
    n = flat.shape[0]
    rows = -(-n // LANE)
    rows = -(-rows // 256) * 256
    return jnp.pad(flat, (0, rows * LANE - n)).reshape(rows, LANE)


def _unpack(packed, shapes):
    flat = packed.reshape(-1)
    out, o = [], 0
    for s in shapes:
        n = math.prod(s)
        out.append(flat[o:o + n].reshape(s))
        o += n
    return out


def _pad_blocks(a, w, wp):
    lead = a.shape[:-1]
    k = a.shape[-1] // w
    pads = [(0, 0)] * (len(lead) + 1) + [(0, wp - w)]
    return jnp.pad(a.reshape(*lead, k, w), pads).reshape(*lead, k * wp)


def kernel(x, ln1_w, w_in, lb_gamma, hg_norm_w, lru_conv_w, lru_conv_b, lru_wa, lru_ba, lru_wx, lru_bx, lru_lambda, lru_norm_w, w_out, ln2_w, ffn_w_up, ffn_conv_w, ffn_conv_b, ffn_w_down, final_norm_w, loss_target, m_ln1_w, m_w_in, m_lb_gamma, m_hg_norm_w, m_lru_conv_w, m_lru_conv_b, m_lru_wa, m_lru_ba, m_lru_wx, m_lru_bx, m_lru_lambda, m_lru_norm_w, m_w_out, m_ln2_w, m_ffn_w_up, m_ffn_conv_w, m_ffn_conv_b, m_ffn_w_down, m_final_norm_w, v_ln1_w, v_w_in, v_lb_gamma, v_hg_norm_w, v_lru_conv_w, v_lru_conv_b, v_lru_wa, v_lru_ba, v_lru_wx, v_lru_bx, v_lru_lambda, v_lru_norm_w, v_w_out, v_ln2_w, v_ffn_w_up, v_ffn_conv_w, v_ffn_conv_b, v_ffn_w_down, v_final_norm_w):
    nb, seq, d = x.shape
    t = nb * seq
    wmix = d // 2
    in_sh = w_in.shape[2]
    up_sh = ffn_w_up.shape[2]
    up_pad = -(-up_sh // LANE) * LANE
    hs = ffn_w_down.shape[1]
    fpad = 4 * up_pad
    cx, cy, cc = _place()
    me = 4 * cx + 2 * cy + cc
    plane = 2 * cx + cy

    def gather(shards, lands, slots, tag, deps=(), also=lambda x, y, c: []):
        first = _plan_gather_first(slots)
        plan = lambda x, y, c: first(x, y, c) + also(x, y, c)
        st = _xfer_start(shards, lands, plan, name=f"gather_{tag}_start", deps=deps)
        return (st, plan, slots, tag), st[4]

    def pass_on(g, after):
        st, plan, slots, tag = g
        _, lands = _xfer_wait(st[0], st[1], st[2], st[3], plan, after, name=f"gather_{tag}_wait")
        st2 = _xfer_start([], lands, _plan_gather_pass(slots), name=f"gather_{tag}_pass")
        return (st2, slots, tag), st2[4]

    def gathered(g, after):
        st2, slots, tag = g
        return _xfer_wait(st2[0], st2[1], [], st2[3], _plan_gather_pass(slots), after, name=f"gather_{tag}_done")[1]

    land8 = lambda s: lax.empty((NDEV,) + s.shape, s.dtype)

    win_s = w_in[0].astype(BF16)
    wout_s = w_out[0].astype(BF16)
    wup_s = jnp.pad(ffn_w_up[0], ((0, 0), (0, up_pad - up_sh))).astype(BF16)
    wdn_s = ffn_w_down[0].astype(BF16)
    fcw_s = jnp.pad(ffn_conv_w[0], ((0, 0), (0, up_pad - up_sh)))
    half = in_sh // 2
    all_rows = pl.ds(0, d)
    win_slot = lambda h: lambda px, py, pc: (_slot8(px, py, pc), all_rows, pl.ds(h * half, half))
    first_a = _plan_gather_first([win_slot(0), _slot8, _slot8], [(all_rows, pl.ds(0, half)), None, None])
    first_b = _plan_gather_first([win_slot(1)], [(all_rows, pl.ds(half, half))])
    pass_a, pass_b = _plan_gather_pass([win_slot(0), _slot8, _slot8]), _plan_gather_pass([win_slot(1)])
    sa = _xfer_start([win_s, lru_conv_w[0], fcw_s], [land8(win_s), land8(lru_conv_w[0]), land8(fcw_s)], first_a,
                     name="gather_a_start")
    sb = _xfer_start([sa[2][0]], [sa[3][0]], first_b, name="gather_b_start", deps=(sa[4],))
    go, tok = gather([wout_s], [land8(wout_s)], [_slot8], "o", deps=(sb[4],))
    uhalf = up_pad // 2
    up_slot = lambda h: lambda px, py, pc: (_slot8(px, py, pc), all_rows, pl.ds(h * uhalf, uhalf))
    first_u = [_plan_gather_first([up_slot(h)], [(all_rows, pl.ds(h * uhalf, uhalf))]) for h in range(2)]
    pass_u = [_plan_gather_pass([up_slot(h)]) for h in range(2)]
    su = _xfer_start([wup_s], [land8(wup_s)], first_u[0], name="gather_u_start", deps=(tok,))
    sv = _xfer_start([su[2][0]], [su[3][0]], first_u[1], name="gather_v_start", deps=(su[4],))
    tok = sv[4]
    npad = up_pad - 2 * hs
    zero_pad = lambda x, y, c: [(("s", 1, None), ("l", 0, (j, pl.ds(2 * hs, npad))), (x, y, c)) for j in range(4)]
    gd, tok = gather([wdn_s] + ([jnp.zeros((npad, d), BF16)] if npad else []), [lax.empty((4, up_pad, d), BF16)],
                     [_slot4(hs)], "d", deps=(tok,), also=zero_pad if npad else lambda x, y, c: [])
    fcb = _pad_blocks(ffn_conv_b, up_sh, up_pad)
    wa_b, wx_b = lru_wa[0].astype(BF16), lru_wx[0].astype(BF16)

    xf = x.reshape(t, d)
    hn = _rms_fwd(xf, ln1_w, name="ln1_fwd", deps=(tok,))
    srcs_a, lands_a = _xfer_wait(sa[0], sa[1], [sb[2][0], sa[2][1], sa[2][2]], [sb[3][0], sa[3][1], sa[3][2]],
                                 first_a, hn, name="gather_a_wait")
    pa = _xfer_start([], lands_a, pass_a, name="gather_a_pass")
    _, (win_l, lcw_g, fcw_g) = _xfer_wait(pa[0], pa[1], [], pa[3], pass_a, pa[4], name="gather_a_done")
    lcw = lcw_g.transpose(1, 0, 2).reshape(lru_conv_w.shape[1], wmix)
    fcw = fcw_g.transpose(1, 0, 2).reshape(ffn_conv_w.shape[1], 2 * fpad)
    proj = _mm(hn, win_l, kind="nn", out_dtype=F32, name="in_proj_a", tm_cap=512, tk_cap=d, col_part=(0, 2))
    _, (win_l,) = _xfer_wait(sb[0], sb[1], [srcs_a[0]], [win_l], first_b, proj, name="gather_b_wait")
    pb = _xfer_start([], [win_l], pass_b, name="gather_b_pass")
    _, (win_g,) = _xfer_wait(pb[0], pb[1], [], pb[3], pass_b, pb[4], name="gather_b_done")
    proj = _mm(hn, win_g, kind="nn", out_dtype=F32, name="in_proj_b", tm_cap=512, tk_cap=d, col_part=(1, 2),
               into=proj)
    go, tok = pass_on(go, proj)
    o_hg, o_pre, states = _hgrn_fwd(proj, lb_gamma, hg_norm_w, nb=nb, seq=seq, name="hgrn_fwd")
    (wout_g,) = gathered(go, o_pre)
    wout_f = wout_g.reshape(1, d, d)
    p_lru, h_lru = _lru_fwd(proj, lcw, lru_conv_b, wa_b, lru_ba, wx_b, lru_bx, lru_lambda,
                            nb=nb, seq=seq, name="lru_fwd", deps=(tok,))
    o_lru = _rms_fwd(p_lru, lru_norm_w, name="lru_norm_fwd")
    srcs_u, lands_u = _xfer_wait(su[0], su[1], [sv[2][0]], [sv[3][0]], first_u[0], o_lru, name="gather_u_wait")
    pu = _xfer_start([], lands_u, pass_u[0], name="gather_u_pass")
    mix = jnp.concatenate([o_hg, o_lru], axis=1)
    h1 = _mm(mix, wout_f, kind="nn", out_dtype=F32, name="out_proj", res=xf, tm_cap=512, tk_cap=d, deps=(pu[4],))
    hn2 = _rms_fwd(h1, ln2_w, name="ln2_fwd")
    _, (wup_l,) = _xfer_wait(pu[0], pu[1], [], pu[3], pass_u[0], hn2, name="gather_u_done")
    up = _mm(hn2, wup_l, kind="nn", out_dtype=F32, name="ffn_up_a", tm_cap=512, tk_cap=d, col_part=(0, 2))
    _, (wup_l,) = _xfer_wait(sv[0], sv[1], [srcs_u[0]], [wup_l], first_u[1], up, name="gather_v_wait")
    pv = _xfer_start([], [wup_l], pass_u[1], name="gather_v_pass")
    _, (wup_g,) = _xfer_wait(pv[0], pv[1], [], pv[3], pass_u[1], pv[4], name="gather_v_done")
    up = _mm(hn2, wup_g, kind="nn", out_dtype=F32, name="ffn_up_b", tm_cap=512, tk_cap=d, col_part=(1, 2), into=up)
    gd, tok = pass_on(gd, up)
    act = _ffn_act_fwd(up, fcw, fcb, nb=nb, seq=seq, name="ffn_act_fwd", deps=(tok,))
    (wdn_g,) = gathered(gd, act)
    wdn_f = wdn_g.reshape(1, fpad, d)
    h2 = _mm(act, wdn_f, kind="nn", out_dtype=F32, name="ffn_down", res=h1, tm_cap=512, tn_cap=1024,
             tk_cap=2 * up_pad)

    dh2, dh2_b, g_fnw, loss_part = _loss_head(h2, final_norm_w.reshape(1, d), loss_target.reshape(t, d),
                                              name="loss_head")
    loss = lax.psum(loss_part[0, 0], ("x", "y", "c"))

    cidx = jnp.reshape(cc, (1,)).astype(jnp.int32)

    def to_sibling(g4, r, tag, deps=()):
        land = lax.empty((4, r, g4.shape[2]), g4.dtype)
        st = _xfer_start([g4], [land], _plan_rs_sibling(r), name=f"rs_sib_start_{tag}", deps=deps)
        return (st, r, tag), st[4]

    def sibling_sum(rs, after):
        st, r, tag = rs
        (g4,), (rcv,) = _xfer_wait(st[0], st[1], st[2], st[3], _plan_rs_sibling(r), after, name=f"rs_sib_wait_{tag}")
        psum = _add_pairs(g4, rcv, cidx, name=f"rs_add_{tag}")
        land = lax.empty((3,) + psum.shape[1:], psum.dtype)
        st2 = _xfer_start([psum], [land], _plan_rs_plane(), name=f"rs_plane_start_{tag}")
        return (st2, tag), st2[4]

    dact = _mm(dh2_b, wdn_f, kind="nt", out_dtype=F32, name="ffn_down_dx", tm_cap=512, tn_cap=up_pad // 2, tk_cap=d)
    g_wdn = _mm(act, dh2_b, kind="tn", out_dtype=BF16, name="ffn_down_dw", tm_cap=1024, tn_cap=1024, tk_cap=t)
    rs_d, tok = to_sibling(g_wdn.reshape(4, up_pad, d), hs, "down")
    dup, s_g, s_v = _ffn_act_bwd(dact, up, fcw, fcb, nb=nb, seq=seq, name="ffn_act_bwd", deps=(tok,))
    pl_d, tok = sibling_sum(rs_d, dup)
    dhn2 = _mm(dup, wup_g, kind="nt", out_dtype=F32, name="ffn_up_dx", tm_cap=512, tn_cap=1024, tk_cap=up_pad,
               k_blocks=2, deps=(tok,))
    g_wup = _mm(dup, hn2, kind="tn", out_dtype=BF16, name="ffn_up_dw", tm_cap=1024, tn_cap=1024, tk_cap=t)
    rs_u, tok = to_sibling(g_wup.reshape(4, 2 * up_pad, d), up_pad, "up")
    dh1, dh1_b, g_ln2 = _rms_bwd(dhn2, h1, ln2_w, name="ln2_bwd", extra=dh2, want_bf16=True, deps=(tok,))
    dmix = _mm(dh1_b, wout_f, kind="nt", out_dtype=F32, name="out_proj_dx", tm_cap=512, tn_cap=1024, tk_cap=d)
    pl_u, tok = sibling_sum(rs_u, dmix)
    dp_lru, g_lnw = _rms_bwd(dmix, p_lru, lru_norm_w, name="lru_norm_bwd", dy_cb=1, deps=(tok,))
    dxr, dyr, s_lru, g_wa, g_wx = _lru_bwd(proj, lcw, lru_conv_b, wa_b, lru_ba, wx_b, lru_bx, lru_lambda,
                                           h_lru, dp_lru, nb=nb, seq=seq, name="lru_bwd")
    dqr, dfr, dir_, dgr, s_hg = _hgrn_bwd(proj, lb_gamma, hg_norm_w, o_pre, states, dmix,
                                          nb=nb, seq=seq, name="hgrn_bwd")
    dproj = jnp.concatenate([dqr, dfr, dir_, dgr, dxr, dyr], axis=1)
    g_win = _mm(hn, dproj, kind="tn", out_dtype=BF16, name="in_proj_dw", out_blocks=NDEV, tm_cap=512, tn_cap=in_sh,
                tk_cap=t)
    rs_i, tok = to_sibling(g_win.reshape(4, 2 * d, in_sh), d, "in")
    g_wout = _mm(mix, dh1_b, kind="tn", out_dtype=BF16, name="out_proj_dw", tn_cap=1024, tk_cap=t, deps=(tok,))
    rs_o, tok = to_sibling(g_wout.reshape(4, d // 4, d), d // NDEV, "out")
    pl_i, tok = sibling_sum(rs_i, tok)
    dhn = _mm(dproj, win_g, kind="nt", out_dtype=F32, name="in_proj_dx", tm_cap=512, tn_cap=1024, tk_cap=in_sh,
              k_blocks=4, deps=(tok,))
    grad_x, g_ln1 = _rms_bwd(dhn, xf, ln1_w, name="ln1_bwd", extra=dh1)

    fcw_parts = jnp.concatenate([s_g[0:3], s_v[0:3]], axis=1).reshape(3, NDEV, up_pad)[:, :, :up_sh]
    fcb_part = jnp.concatenate([s_g[3:4], s_v[3:4]], axis=1).reshape(NDEV, up_pad)[:, :up_sh]
    small_parts = [g_ln1, s_hg[0:2], s_hg[2:3], s_lru[0:4], s_lru[4:5], g_wa, s_lru[5:6], g_wx, s_lru[6:7],
                   s_lru[7:8], g_lnw, g_ln2, fcw_parts, fcb_part, g_fnw]
    reduced = _all_reduce_packed(_pack(small_parts), name="allreduce_small")
    pl_o, _ = sibling_sum(rs_o, reduced)
    summed = _unpack(reduced, [p.shape for p in small_parts])
    (s_ln1, s_lbg, s_hgn, s_lcw, s_lcb, s_wa, s_ba, s_wx, s_bx, s_lam, s_lnw, s_ln2, s_fcw, s_fcb, s_fnw) = summed
    sh_lcw = lru_conv_w.shape[2]
    g_small = {
        "ln1_w": s_ln1, "lb_gamma": s_lbg, "hg_norm_w": s_hgn,
        "lru_conv_w": lax.dynamic_slice_in_dim(s_lcw, me * sh_lcw, sh_lcw, axis=1),
        "lru_conv_b": s_lcb, "lru_wa": s_wa, "lru_ba": s_ba, "lru_wx": s_wx, "lru_bx": s_bx,
        "lru_lambda": s_lam, "lru_norm_w": s_lnw, "ln2_w": s_ln2,
        "ffn_conv_w": lax.dynamic_index_in_dim(s_fcw, me, 1, keepdims=False),
        "ffn_conv_b": s_fcb, "final_norm_w": s_fnw,
    }
    w_small = {"ln1_w": ln1_w, "lb_gamma": lb_gamma, "hg_norm_w": hg_norm_w, "lru_conv_w": lru_conv_w,
               "lru_conv_b": lru_conv_b, "lru_wa": lru_wa, "lru_ba": lru_ba, "lru_wx": lru_wx, "lru_bx": lru_bx,
               "lru_lambda": lru_lambda, "lru_norm_w": lru_norm_w, "ln2_w": ln2_w, "ffn_conv_w": ffn_conv_w,
               "ffn_conv_b": ffn_conv_b, "final_norm_w": final_norm_w}
    m_small = {"ln1_w": m_ln1_w, "lb_gamma": m_lb_gamma, "hg_norm_w": m_hg_norm_w, "lru_conv_w": m_lru_conv_w,
               "lru_conv_b": m_lru_conv_b, "lru_wa": m_lru_wa, "lru_ba": m_lru_ba, "lru_wx": m_lru_wx,
               "lru_bx": m_lru_bx, "lru_lambda": m_lru_lambda, "lru_norm_w": m_lru_norm_w, "ln2_w": m_ln2_w,
               "ffn_conv_w": m_ffn_conv_w, "ffn_conv_b": m_ffn_conv_b, "final_norm_w": m_final_norm_w}
    v_small = {"ln1_w": v_ln1_w, "lb_gamma": v_lb_gamma, "hg_norm_w": v_hg_norm_w, "lru_conv_w": v_lru_conv_w,
               "lru_conv_b": v_lru_conv_b, "lru_wa": v_lru_wa, "lru_ba": v_lru_ba, "lru_wx": v_lru_wx,
               "lru_bx": v_lru_bx, "lru_lambda": v_lru_lambda, "lru_norm_w": v_lru_norm_w, "ln2_w": v_ln2_w,
               "ffn_conv_w": v_ffn_conv_w, "ffn_conv_b": v_ffn_conv_b, "final_norm_w": v_final_norm_w}
    names = list(w_small)
    shapes = [w_small[k].shape for k in names]
    g_small = {k: g_small[k].reshape(w_small[k].shape) for k in names}
    packed = _adamw_packed(_pack([w_small[k] for k in names]), _pack([g_small[k] for k in names]),
                           _pack([m_small[k] for k in names]), _pack([v_small[k] for k in names]),
                           name="adamw_small")
    d_small, nm_small, nv_small = ({k: a for k, a in zip(names, _unpack(pk, shapes))} for pk in packed)

    def finish(pl_x, after, trim=lambda a: a):
        st, tag = pl_x
        (psum,), (rb,) = _xfer_wait(st[0], st[1], st[2], st[3], _plan_rs_plane(), after, name=f"rs_plane_wait_{tag}")
        return trim(lax.dynamic_index_in_dim(psum, plane, 0, keepdims=False)), trim(rb)

    upd_dn = _adamw_shard(ffn_w_down, m_ffn_w_down, v_ffn_w_down, *finish(pl_d, packed[0]), name="adamw_w_down")
    tr3 = lambda a: a.transpose(0, 2, 1)
    upd_up_t = _adamw_shard(tr3(ffn_w_up), tr3(m_ffn_w_up), tr3(v_ffn_w_up), *finish(pl_u, upd_dn[0]),
                            name="adamw_w_up")
    upd_up = [tr3(a) for a in upd_up_t]
    upd_out = _adamw_shard(w_out, m_w_out, v_w_out, *finish(pl_o, upd_up_t[0]), name="adamw_w_out")
    upd_in = _adamw_shard(w_in, m_w_in, v_w_in, *finish(pl_i, upd_out[0]), name="adamw_w_in")

    grads, deltas, new_m, new_v = dict(g_small), dict(d_small), dict(nm_small), dict(nv_small)
    for k, upd in (("w_in", upd_in), ("w_out", upd_out), ("ffn_w_up", upd_up), ("ffn_w_down", upd_dn)):
        grads[k], deltas[k], new_m[k], new_v[k] = upd
    order = ["ln1_w", "w_in", "lb_gamma", "hg_norm_w", "lru_conv_w", "lru_conv_b", "lru_wa", "lru_ba", "lru_wx",
             "lru_bx", "lru_lambda", "lru_norm_w", "w_out", "ln2_w", "ffn_w_up", "ffn_conv_w", "ffn_conv_b",
             "ffn_w_down", "final_norm_w"]
    return (loss, grad_x.reshape(nb, seq, d), *[grads[k] for k in order], *[deltas[k] for k in order],
            *[new_m[k] for k in order], *[new_v[k] for k in order])
```
